```python
import jax, jax.numpy as jnp
from jax import lax
import numpy as np

D_MODEL = 1024
BATCH = 8
SEQ = 16384
DEPTH = 1

MIX_W = D_MODEL
HEAD_DIM = 64
ATTN_W = MIX_W // 2
N_ATTN_HEADS = ATTN_W // HEAD_DIM
N_KV_HEADS = 2
KV_W = N_KV_HEADS * HEAD_DIM
WINDOW = 128
BLOCK = 128
RET_W = MIX_W - ATTN_W
N_RET_HEADS = 4
RET_HEAD_DIM = RET_W // N_RET_HEADS
RET_CHUNK = 128
IN_W = ATTN_W + 2 * KV_W + 4 * RET_W
D_FF = 2816
CONV_WIDTH = 3
RMS_EPS = 1e-6
GN_EPS = 1e-6
MASK_VALUE = -1e30

kernel_name = "hymba_swa_sink_retention_convffn_sandwich"


def rms_norm(x, w):
    xf = x.astype(jnp.float32)
    y = xf * lax.rsqrt(jnp.mean(xf * xf, axis=-1, keepdims=True) + RMS_EPS)
    return (y * w.astype(jnp.float32)).astype(x.dtype)


def sliding_window_sink_attention(q, k, v, sinks):
    b, s, _ = q.shape
    nb = s // BLOCK
    g = N_ATTN_HEADS // N_KV_HEADS
    qb = q.reshape(b, nb, BLOCK, N_KV_HEADS, g, HEAD_DIM)
    kb = k.reshape(b, nb, BLOCK, N_KV_HEADS, HEAD_DIM)
    vb = v.reshape(b, nb, BLOCK, N_KV_HEADS, HEAD_DIM)
    pad = ((0, 0), (1, 0), (0, 0), (0, 0), (0, 0))
    kk = jnp.concatenate([jnp.pad(kb, pad)[:, :-1], kb], axis=2)
    vv = jnp.concatenate([jnp.pad(vb, pad)[:, :-1], vb], axis=2)
    scores = jnp.einsum('bnqhgd,bnkhd->bnhgqk', qb, kk).astype(jnp.float32) * (HEAD_DIM ** -0.5)
    qpos = jnp.arange(BLOCK)[:, None] + BLOCK
    kpos = jnp.arange(2 * BLOCK)[None, :]
    rel = qpos - kpos
    band = (rel >= 0) & (rel < WINDOW)
    not_pad = (jnp.arange(nb)[:, None, None] > 0) | (kpos >= BLOCK)[None]
    valid = band[None] & not_pad
    scores = jnp.where(valid[None, :, None, None], scores, MASK_VALUE)
    sink = jnp.broadcast_to(
        sinks.astype(jnp.float32).reshape(N_KV_HEADS, g)[None, None, :, :, None, None],
        scores.shape[:-1] + (1,))
    probs = jax.nn.softmax(jnp.concatenate([scores, sink], axis=-1), axis=-1)[..., :-1]
    out = jnp.einsum('bnhgqk,bnkhd->bnqhgd', probs.astype(vv.dtype), vv)
    return out.reshape(b, s, ATTN_W)


def rotate_every_two(x):
    x1 = x[..., ::2]
    x2 = x[..., 1::2]
    return jnp.stack([-x2, x1], axis=-1).reshape(x.shape)


def retention_chunkwise(q, k, v):
    b, s, h, dk = q.shape
    dv = v.shape[-1]
    nc = s // RET_CHUNK
    c = RET_CHUNK
    log_gamma = jnp.log(1.0 - jnp.power(2.0, -5.0 - jnp.arange(h, dtype=jnp.float32)))
    idx = jnp.arange(c, dtype=jnp.float32)
    rel = idx[:, None] - idx[None, :]
    d_intra = jnp.where(rel[None] >= 0,
                        jnp.exp(log_gamma[:, None, None] * jnp.maximum(rel, 0.0)[None]), 0.0)
    xi = jnp.exp(log_gamma[None, :] * (idx[:, None] + 1.0))
    zeta = jnp.exp(log_gamma[None, :] * (c - 1.0 - idx[:, None]))
    chunk_decay = jnp.exp(log_gamma * c)
    qc = q.reshape(b, nc, c, h, dk)
    kc = k.reshape(b, nc, c, h, dk)
    vc = v.reshape(b, nc, c, h, dv)
    inner = jnp.einsum('bnqhd,bnkhd->bnhqk', qc, kc) * d_intra[None, None]
    o_inner = jnp.einsum('bnhqk,bnkhe->bnqhe', inner, vc)
    kv_chunk = jnp.einsum('bnkhd,kh,bnkhe->bnhde', kc, zeta, vc)

    def step(state, kv):
        return chunk_decay[None, :, None, None] * state + kv, state

    _, prev = lax.scan(step, jnp.zeros((b, h, dk, dv), jnp.float32), jnp.moveaxis(kv_chunk, 1, 0))
    prev = jnp.moveaxis(prev, 0, 1)
    o_cross = jnp.einsum('bnqhd,bnhde->bnqhe', qc, prev) * xi[None, None, :, :, None]
    return (o_inner + o_cross).reshape(b, s, h, dv)


def retention_group(q, k, v, gate):
    b, s, _ = q.shape
    dtype = q.dtype
    pos = jnp.arange(s, dtype=jnp.float32)
    angle = 1.0 / jnp.power(10000.0, jnp.linspace(0.0, 1.0, RET_HEAD_DIM // 2, dtype=jnp.float32))
    angle = jnp.repeat(angle, 2)
    sin = jnp.sin(pos[:, None] * angle[None])[None, :, None, :]
    cos = jnp.cos(pos[:, None] * angle[None])[None, :, None, :]
    qf = q.astype(jnp.float32).reshape(b, s, N_RET_HEADS, RET_HEAD_DIM)
    kf = k.astype(jnp.float32).reshape(b, s, N_RET_HEADS, RET_HEAD_DIM) * (RET_HEAD_DIM ** -0.5)
    vf = v.astype(jnp.float32).reshape(b, s, N_RET_HEADS, RET_HEAD_DIM)
    qf = qf * cos + rotate_every_two(qf) * sin
    kf = kf * cos + rotate_every_two(kf) * sin
    o = retention_chunkwise(qf, kf, vf)
    mu = jnp.mean(o, axis=-1, keepdims=True)
    var = jnp.mean(jnp.square(o - mu), axis=-1, keepdims=True)
    o = ((o - mu) * lax.rsqrt(var + GN_EPS)).reshape(b, s, RET_W)
    return (jax.nn.silu(gate.astype(jnp.float32)) * o).astype(dtype)


def causal_depthwise_conv(u, w, bias):
    ch = u.shape[-1]
    y = lax.conv_general_dilated(u, w[:, None, :].astype(u.dtype), window_strides=(1,),
                                 padding=[(CONV_WIDTH - 1, 0)],
                                 dimension_numbers=('NWC', 'WIO', 'NWC'),
                                 feature_group_count=ch)
    return y + bias.astype(u.dtype)


def _fwd_setup_inputs(seed: int = 0) -> dict:
    key = jax.random.key(seed)
    ks = jax.random.split(key, 13)
    f32 = jnp.float32

    def gain(k):
        return 1.0 + 0.02 * jax.random.normal(k, (DEPTH, D_MODEL), f32)

    return {
        "x": jax.random.normal(ks[0], (BATCH, SEQ, D_MODEL), f32),
        "mix_pre_norm": gain(ks[1]),
        "w_in": jax.random.normal(ks[2], (DEPTH, D_MODEL, IN_W), f32) * D_MODEL ** -0.5,
        "attn_sinks": jax.random.normal(ks[3], (DEPTH, N_ATTN_HEADS), f32),
        "w_out": jax.random.normal(ks[4], (DEPTH, MIX_W, D_MODEL), f32) * MIX_W ** -0.5,
        "mix_post_norm": gain(ks[5]),
        "ffn_pre_norm": gain(ks[6]),
        "w_up": jax.random.normal(ks[7], (DEPTH, D_MODEL, 2 * D_FF), f32) * D_MODEL ** -0.5,
        "conv_w": jax.random.normal(ks[8], (DEPTH, CONV_WIDTH, 2 * D_FF), f32) * CONV_WIDTH ** -0.5,
        "conv_b": 0.01 * jax.random.normal(ks[9], (DEPTH, 2 * D_FF), f32),
        "w_down": jax.random.normal(ks[10], (DEPTH, D_FF, D_MODEL), f32) * D_FF ** -0.5,
        "ffn_post_norm": gain(ks[11]),
    }


def _fwd_reference(x, mix_pre_norm, w_in, attn_sinks, w_out, mix_post_norm,
              ffn_pre_norm, w_up, conv_w, conv_b, w_down, ffn_post_norm):
    splits = np.cumsum([ATTN_W, KV_W, KV_W, RET_W, RET_W, RET_W]).tolist()
    for l in range(DEPTH):
        h = rms_norm(x, mix_pre_norm[l])
        proj = jnp.einsum('bsd,de->bse', h, w_in[l])
        q_a, k_a, v_a, q_r, k_r, v_r, g_r = jnp.split(proj, splits, axis=-1)
        attn_out = sliding_window_sink_attention(q_a, k_a, v_a, attn_sinks[l])
        ret_out = retention_group(q_r, k_r, v_r, g_r)
        mixed = jnp.einsum('bse,ed->bsd', jnp.concatenate([attn_out, ret_out], axis=-1), w_out[l])
        x = x + rms_norm(mixed, mix_post_norm[l])
        h = rms_norm(x, ffn_pre_norm[l])
        u = causal_depthwise_conv(jnp.einsum('bsd,df->bsf', h, w_up[l]), conv_w[l], conv_b[l])
        u_gate, u_val = jnp.split(u, 2, axis=-1)
        y = jax.nn.gelu(u_gate, approximate=True) * u_val
        y = jnp.einsum('bsf,fd->bsd', y, w_down[l])
        x = x + rms_norm(y, ffn_post_norm[l])
    return x


import jax as _jax
import jax.numpy as _jnp

TWIN_FORMAT = 'train_step'
FWD_PARAMS = ['x', 'mix_pre_norm', 'w_in', 'attn_sinks', 'w_out', 'mix_post_norm', 'ffn_pre_norm', 'w_up', 'conv_w', 'conv_b', 'w_down', 'ffn_post_norm']
TWIN_WEIGHTS = ['mix_pre_norm', 'w_in', 'attn_sinks', 'w_out', 'mix_post_norm', 'ffn_pre_norm', 'w_up', 'conv_w', 'conv_b', 'w_down', 'ffn_post_norm']
TWIN_DIFF_INPUT = 'x'
TWIN_INPUTS = ['x', 'mix_pre_norm', 'w_in', 'attn_sinks', 'w_out', 'mix_post_norm', 'ffn_pre_norm', 'w_up', 'conv_w', 'conv_b', 'w_down', 'ffn_post_norm', 'loss_target', 'm_mix_pre_norm', 'm_w_in', 'm_attn_sinks', 'm_w_out', 'm_mix_post_norm', 'm_ffn_pre_norm', 'm_w_up', 'm_conv_w', 'm_conv_b', 'm_w_down', 'm_ffn_post_norm', 'v_mix_pre_norm', 'v_w_in', 'v_attn_sinks', 'v_w_out', 'v_mix_post_norm', 'v_ffn_pre_norm', 'v_w_up', 'v_conv_w', 'v_conv_b', 'v_w_down', 'v_ffn_post_norm']
TWIN_OUTPUTS = ['loss', 'grad_x', 'grad_mix_pre_norm', 'grad_w_in', 'grad_attn_sinks', 'grad_w_out', 'grad_mix_post_norm', 'grad_ffn_pre_norm', 'grad_w_up', 'grad_conv_w', 'grad_conv_b', 'grad_w_down', 'grad_ffn_post_norm', 'delta_mix_pre_norm', 'delta_w_in', 'delta_attn_sinks', 'delta_w_out', 'delta_mix_post_norm', 'delta_ffn_pre_norm', 'delta_w_up', 'delta_conv_w', 'delta_conv_b', 'delta_w_down', 'delta_ffn_post_norm', 'new_m_mix_pre_norm', 'new_m_w_in', 'new_m_attn_sinks', 'new_m_w_out', 'new_m_mix_post_norm', 'new_m_ffn_pre_norm', 'new_m_w_up', 'new_m_conv_w', 'new_m_conv_b', 'new_m_w_down', 'new_m_ffn_post_norm', 'new_v_mix_pre_norm', 'new_v_w_in', 'new_v_attn_sinks', 'new_v_w_out', 'new_v_mix_post_norm', 'new_v_ffn_pre_norm', 'new_v_w_up', 'new_v_conv_w', 'new_v_conv_b', 'new_v_w_down', 'new_v_ffn_post_norm']
TWIN_LEAF_KINDS = {'loss': 'loss', 'grad_x': 'grad_x', 'grad_mix_pre_norm': 'grad_w', 'grad_w_in': 'grad_w', 'grad_attn_sinks': 'grad_w', 'grad_w_out': 'grad_w', 'grad_mix_post_norm': 'grad_w', 'grad_ffn_pre_norm': 'grad_w', 'grad_w_up': 'grad_w', 'grad_conv_w': 'grad_w', 'grad_conv_b': 'grad_w', 'grad_w_down': 'grad_w', 'grad_ffn_post_norm': 'grad_w', 'delta_mix_pre_norm': 'delta_w', 'delta_w_in': 'delta_w', 'delta_attn_sinks': 'delta_w', 'delta_w_out': 'delta_w', 'delta_mix_post_norm': 'delta_w', 'delta_ffn_pre_norm': 'delta_w', 'delta_w_up': 'delta_w', 'delta_conv_w': 'delta_w', 'delta_conv_b': 'delta_w', 'delta_w_down': 'delta_w', 'delta_ffn_post_norm': 'delta_w', 'new_m_mix_pre_norm': 'new_m', 'new_m_w_in': 'new_m', 'new_m_attn_sinks': 'new_m', 'new_m_w_out': 'new_m', 'new_m_mix_post_norm': 'new_m', 'new_m_ffn_pre_norm': 'new_m', 'new_m_w_up': 'new_m', 'new_m_conv_w': 'new_m', 'new_m_conv_b': 'new_m', 'new_m_w_down': 'new_m', 'new_m_ffn_post_norm': 'new_m', 'new_v_mix_pre_norm': 'new_v', 'new_v_w_in': 'new_v', 'new_v_attn_sinks': 'new_v', 'new_v_w_out': 'new_v', 'new_v_mix_post_norm': 'new_v', 'new_v_ffn_pre_norm': 'new_v', 'new_v_w_up': 'new_v', 'new_v_conv_w': 'new_v', 'new_v_conv_b': 'new_v', 'new_v_w_down': 'new_v', 'new_v_ffn_post_norm': 'new_v'}


def _forward(args):
    return _fwd_reference(*[args[k] for k in FWD_PARAMS])


def _output_shape():
    def fwd():
        inp = _fwd_setup_inputs(0)
        return _fwd_reference(*[inp[k] for k in FWD_PARAMS])
    out = _jax.eval_shape(fwd)
    return out.shape, out.dtype

N_MICROBATCH = 1
ADAM_LR = 0.001
ADAM_B1 = 0.9
ADAM_B2 = 0.999
ADAM_EPS = 1e-08
ADAM_WD = 0.01
ADAM_STEP = 10
PER_EXAMPLE_BATCH_AXIS = {'x': 0, 'loss_target': 0}
SHARED_INPUTS = []
_WEIGHT_DTYPES = {'mix_pre_norm': _jnp.float32, 'w_in': _jnp.float32, 'attn_sinks': _jnp.float32, 'w_out': _jnp.float32, 'mix_post_norm': _jnp.float32, 'ffn_pre_norm': _jnp.float32, 'w_up': _jnp.float32, 'conv_w': _jnp.float32, 'conv_b': _jnp.float32, 'w_down': _jnp.float32, 'ffn_post_norm': _jnp.float32}
MOMENT_SCALE = {'mix_pre_norm': 1.851013e+00, 'w_in': 1.107322e+00, 'attn_sinks': 2.116971e-01, 'w_out': 1.079257e+00, 'mix_post_norm': 1.283525e+02, 'ffn_pre_norm': 1.062163e+00, 'w_up': 5.035376e-01, 'conv_w': 5.629132e-01, 'conv_b': 1.080942e+00, 'w_down': 1.044017e+00, 'ffn_post_norm': 1.283690e+02}


def _to_microbatches(a, axis):
    t = _jnp.moveaxis(a, axis, 0)
    t = t.reshape((N_MICROBATCH, t.shape[0] // N_MICROBATCH) + t.shape[1:])
    return _jnp.moveaxis(t, 1, axis + 1)


def setup_inputs(seed: int = 0) -> dict:
    inp = _fwd_setup_inputs(seed)
    key = _jax.random.fold_in(_jax.random.key(seed), 7919)
    shape, _ = _output_shape()
    out = dict(inp)
    out["loss_target"] = _jax.random.normal(_jax.random.fold_in(key, 0), shape, _jnp.float32)
    for i, name in enumerate(TWIN_WEIGHTS):
        w = inp[name].astype(_jnp.float32)
        if MOMENT_SCALE is None:
            s = _jnp.sqrt(_jnp.mean(_jnp.square(w)) + 1e-30)
        else:
            s = MOMENT_SCALE[name]
        km, kv = _jax.random.split(_jax.random.fold_in(key, i + 1))
        out[name] = w
        out["m_" + name] = s * _jax.random.normal(km, w.shape, _jnp.float32)
        out["v_" + name] = (s * s) * _jax.random.uniform(kv, w.shape, _jnp.float32, 0.5, 1.5)
    if N_MICROBATCH > 1:
        for name, axis in PER_EXAMPLE_BATCH_AXIS.items():
            out[name] = _to_microbatches(out[name], axis)
    return {'x': out['x'], 'mix_pre_norm': out['mix_pre_norm'], 'w_in': out['w_in'], 'attn_sinks': out['attn_sinks'], 'w_out': out['w_out'], 'mix_post_norm': out['mix_post_norm'], 'ffn_pre_norm': out['ffn_pre_norm'], 'w_up': out['w_up'], 'conv_w': out['conv_w'], 'conv_b': out['conv_b'], 'w_down': out['w_down'], 'ffn_post_norm': out['ffn_post_norm'], 'loss_target': out['loss_target'], 'm_mix_pre_norm': out['m_mix_pre_norm'], 'm_w_in': out['m_w_in'], 'm_attn_sinks': out['m_attn_sinks'], 'm_w_out': out['m_w_out'], 'm_mix_post_norm': out['m_mix_post_norm'], 'm_ffn_pre_norm': out['m_ffn_pre_norm'], 'm_w_up': out['m_w_up'], 'm_conv_w': out['m_conv_w'], 'm_conv_b': out['m_conv_b'], 'm_w_down': out['m_w_down'], 'm_ffn_post_norm': out['m_ffn_post_norm'], 'v_mix_pre_norm': out['v_mix_pre_norm'], 'v_w_in': out['v_w_in'], 'v_attn_sinks': out['v_attn_sinks'], 'v_w_out': out['v_w_out'], 'v_mix_post_norm': out['v_mix_post_norm'], 'v_ffn_pre_norm': out['v_ffn_pre_norm'], 'v_w_up': out['v_w_up'], 'v_conv_w': out['v_conv_w'], 'v_conv_b': out['v_conv_b'], 'v_w_down': out['v_w_down'], 'v_ffn_post_norm': out['v_ffn_post_norm']}


def _loss(weights, diff, rest, loss_target):
    with _jax.named_scope("forward"):
        args = {**rest, TWIN_DIFF_INPUT: diff, **{k: w.astype(_WEIGHT_DTYPES[k]) for k, w in weights.items()}}
        y = _forward(args)
    with _jax.named_scope("loss_head"):
        err = _jnp.square(y.astype(_jnp.float32) - loss_target)
        return 0.5 * _jnp.sum(_jnp.mean(err, axis=-1)) if err.ndim else 0.5 * err


def _adamw(w, g, m, v):
    m = ADAM_B1 * m + (1.0 - ADAM_B1) * g
    v = ADAM_B2 * v + (1.0 - ADAM_B2) * _jnp.square(g)
    m_hat = m / (1.0 - ADAM_B1 ** ADAM_STEP)
    v_hat = v / (1.0 - ADAM_B2 ** ADAM_STEP)
    delta = -ADAM_LR * (m_hat / (_jnp.sqrt(v_hat) + ADAM_EPS) + ADAM_WD * w)
    return delta, m, v


def reference(x, mix_pre_norm, w_in, attn_sinks, w_out, mix_post_norm, ffn_pre_norm, w_up, conv_w, conv_b, w_down, ffn_post_norm, loss_target, m_mix_pre_norm, m_w_in, m_attn_sinks, m_w_out, m_mix_post_norm, m_ffn_pre_norm, m_w_up, m_conv_w, m_conv_b, m_w_down, m_ffn_post_norm, v_mix_pre_norm, v_w_in, v_attn_sinks, v_w_out, v_mix_post_norm, v_ffn_pre_norm, v_w_up, v_conv_w, v_conv_b, v_w_down, v_ffn_post_norm):
    given = dict(x=x, mix_pre_norm=mix_pre_norm, w_in=w_in, attn_sinks=attn_sinks, w_out=w_out, mix_post_norm=mix_post_norm, ffn_pre_norm=ffn_pre_norm, w_up=w_up, conv_w=conv_w, conv_b=conv_b, w_down=w_down, ffn_post_norm=ffn_post_norm, loss_target=loss_target, m_mix_pre_norm=m_mix_pre_norm, m_w_in=m_w_in, m_attn_sinks=m_attn_sinks, m_w_out=m_w_out, m_mix_post_norm=m_mix_post_norm, m_ffn_pre_norm=m_ffn_pre_norm, m_w_up=m_w_up, m_conv_w=m_conv_w, m_conv_b=m_conv_b, m_w_down=m_w_down, m_ffn_post_norm=m_ffn_post_norm, v_mix_pre_norm=v_mix_pre_norm, v_w_in=v_w_in, v_attn_sinks=v_attn_sinks, v_w_out=v_w_out, v_mix_post_norm=v_mix_post_norm, v_ffn_pre_norm=v_ffn_pre_norm, v_w_up=v_w_up, v_conv_w=v_conv_w, v_conv_b=v_conv_b, v_w_down=v_w_down, v_ffn_post_norm=v_ffn_post_norm)
    weights = {n: given[n] for n in TWIN_WEIGHTS}
    shared = {n: given[n] for n in SHARED_INPUTS}
    per_example = {n: given[n] for n in ['x']}
    grad_fn = _jax.value_and_grad(_loss, argnums=(0, 1))

    def one_microbatch(ex, loss_target):
        ex = dict(ex)
        diff = ex.pop(TWIN_DIFF_INPUT)
        return grad_fn(weights, diff, {**shared, **ex}, loss_target)

    if N_MICROBATCH == 1:
        loss, (grad_w, grad_x) = one_microbatch(per_example, given["loss_target"])
    else:
        def body(carry, xs):
            loss_sum, grad_sum = carry
            l_k, (gw_k, gx_k) = one_microbatch(xs[0], xs[1])
            with _jax.named_scope("update"):
                return (loss_sum + l_k, _jax.tree.map(_jnp.add, grad_sum, gw_k)), gx_k

        init = (_jnp.zeros((), _jnp.float32), _jax.tree.map(_jnp.zeros_like, weights))
        (loss, grad_w), grad_x = _jax.lax.scan(body, init, (per_example, given["loss_target"]))
    with _jax.named_scope("update"):
        delta_w, new_m, new_v = {}, {}, {}
        for n in TWIN_WEIGHTS:
            delta_w[n], new_m[n], new_v[n] = _adamw(weights[n], grad_w[n], given["m_" + n], given["v_" + n])
    return (loss, grad_x, *[grad_w[n] for n in TWIN_WEIGHTS], *[delta_w[n] for n in TWIN_WEIGHTS],
            *[new_m[n] for n in TWIN_WEIGHTS], *[new_v[n] for n in TWIN_WEIGHTS])
```

```python
import functools
import math

import jax
import jax.numpy as jnp
from jax import lax
from jax.experimental import pallas as pl
from jax.experimental.pallas import tpu as pltpu

F32 = jnp.float32
BF16 = jnp.bfloat16

D_MODEL = 1024
HEAD_DIM = 64
ATTN_W = 512
N_ATTN_HEADS = 8
KV_W = 128
RET_W = 512
N_RET_HEADS = 4
RET_HEAD_DIM = 128
CHUNK = 128
IN_W = 2816
D_FF = 2816
UP_W = 2 * D_FF
CONV_WIDTH = 3
RMS_EPS = 1e-6
GN_EPS = 1e-6
MASK_VALUE = -1e30
ATTN_SCALE = HEAD_DIM ** -0.5
RET_K_SCALE = RET_HEAD_DIM ** -0.5
GELU_C = math.sqrt(2.0 / math.pi)
GELU_A = 0.044715

ADAM_LR = 0.001
ADAM_B1 = 0.9
ADAM_B2 = 0.999
ADAM_EPS = 1e-08
ADAM_WD = 0.01
ADAM_STEP = 10

N_CHIPS = 4
N_DEV = 8
MESH = pl.DeviceIdType.MESH
VMEM_LIMIT_V7X = 56 * 1024 * 1024
TOKEN_TILE = 256

Q_A0, KV_A0, Q_R0, K_R0, V_R0, G_R0 = 0, 512, 768, 1280, 1792, 2304

ROWS_W_IN, ROWS_W_OUT, ROWS_W_UP, ROWS_W_DOWN = 704, 256, 1408, 704
ROWS_PACK = ROWS_W_IN + ROWS_W_OUT + ROWS_W_UP + ROWS_W_DOWN
ROWS_CONV = 8
SMALL_ROWS = 16
CONV_FULL_ROWS = 24


def _params(sem=None, **kw):
    if sem is not None:
        kw["dimension_semantics"] = sem
    return pltpu.CompilerParams(vmem_limit_bytes=VMEM_LIMIT_V7X, **kw)


def _resident(shape):
    zeros = (0,) * len(shape)
    return pl.BlockSpec(shape, lambda *_: zeros, pipeline_mode=pl.Buffered(1))


def _dot(a, b):
    return jnp.dot(a, b, preferred_element_type=F32)


def _dot_nt(a, b):
    return lax.dot_general(a, b, (((1,), (1,)), ((), ())), preferred_element_type=F32)


def _dot_tn(a, b):
    return lax.dot_general(a, b, (((0,), (0,)), ((), ())), preferred_element_type=F32)


def _rstd(v):
    return lax.rsqrt(jnp.mean(v * v, axis=-1, keepdims=True) + RMS_EPS)


def _rms_bwd(dy, v, rstd, gain):
    n = v * rstd
    dgain = jnp.sum(dy * n, axis=0, keepdims=True)
    dn = dy * gain
    dv = rstd * (dn - n * jnp.mean(dn * n, axis=-1, keepdims=True))
    return dv, dgain


def _lane_lo(shape):
    return (lax.broadcasted_iota(jnp.int32, shape, 1) % 128) < HEAD_DIM


def _attn_valid(chunk_index):
    qi = lax.broadcasted_iota(jnp.int32, (CHUNK, 2 * CHUNK), 0)
    kj = lax.broadcasted_iota(jnp.int32, (CHUNK, 2 * CHUNK), 1)
    first_key = jnp.where(chunk_index > 0, 0, CHUNK)
    return jnp.logical_and(jnp.logical_and(kj > qi, kj >= first_key), kj <= qi + CHUNK)


def _attn_head_operands(h, q_pair, kk, kk_r, vv, vv_r, lo_q, lo_kv):
    par, hk = h % 2, h // 4
    half_q = lo_q if par == 0 else jnp.logical_not(lo_q)
    half_kv = lo_kv if par == 0 else jnp.logical_not(lo_kv)
    k_use = kk if hk == par else kk_r
    v_use = vv if hk == par else vv_r
    qe = jnp.where(half_q, q_pair, 0.0).astype(BF16)
    return par, hk, half_q, half_kv, qe, k_use, v_use


def _attn_probs(qe, k_use_b, valid, sink):
    s = _dot_nt(qe, k_use_b) * ATTN_SCALE
    s = jnp.where(valid, s, MASK_VALUE)
    m = jnp.maximum(jnp.max(s, axis=-1, keepdims=True), sink)
    e = jnp.exp(s - m)
    e_sink = jnp.exp(sink - m)
    inv = 1.0 / (jnp.sum(e, axis=-1, keepdims=True) + e_sink)
    return e * inv, e_sink * inv


def _rot2(v):
    w = v.shape[1]
    even = (lax.broadcasted_iota(jnp.int32, v.shape, 1) % 2) == 0
    return jnp.where(even, -pltpu.roll(v, w - 1, 1), pltpu.roll(v, 1, 1))


def _tile4(v):
    return jnp.concatenate([v, v, v, v], axis=-1)


def _sigmoid(v):
    return 1.0 / (1.0 + jnp.exp(-v))


def _ret_constants():
    h = N_RET_HEADS
    log_gamma = jnp.log(1.0 - jnp.power(2.0, -5.0 - jnp.arange(h, dtype=F32)))
    idx = jnp.arange(CHUNK, dtype=F32)
    rel = idx[:, None] - idx[None, :]
    d_intra = jnp.where(rel[None] >= 0, jnp.exp(log_gamma[:, None, None] * jnp.maximum(rel, 0.0)[None]), 0.0)
    xi = jnp.exp(log_gamma[None, :] * (idx[:, None] + 1.0))
    zeta = jnp.exp(log_gamma[None, :] * (CHUNK - 1.0 - idx[:, None]))
    decay = jnp.exp(log_gamma * CHUNK)
    xi_full = jnp.repeat(xi, RET_HEAD_DIM, axis=1)
    zeta_full = jnp.repeat(zeta, RET_HEAD_DIM, axis=1)
    decay_full = jnp.broadcast_to(jnp.repeat(decay, RET_HEAD_DIM)[None, :], (8, RET_W))
    return d_intra.astype(F32), xi_full.astype(F32), zeta_full.astype(F32), decay_full.astype(F32)


def _rope_tables(s):
    pos = jnp.arange(s, dtype=F32)
    angle = 1.0 / jnp.power(10000.0, jnp.linspace(0.0, 1.0, RET_HEAD_DIM // 2, dtype=F32))
    angle = jnp.repeat(angle, 2)
    return jnp.sin(pos[:, None] * angle[None]), jnp.cos(pos[:, None] * angle[None])


def _in_proj(x, gain, w_in):
    s = x.shape[0]
    tm = TOKEN_TILE

    def body(x_ref, g_ref, w_ref, h_ref, p_ref):
        xv = x_ref[...]
        h = (xv * _rstd(xv) * g_ref[...]).astype(BF16)
        h_ref[...] = h
        p_ref[...] = _dot(h, w_ref[...])

    return pl.pallas_call(
        body, name="in_proj", grid=(s // tm,),
        in_specs=[pl.BlockSpec((tm, D_MODEL), lambda i: (i, 0)), _resident((1, D_MODEL)), _resident((D_MODEL, IN_W))],
        out_specs=[pl.BlockSpec((tm, D_MODEL), lambda i: (i, 0)), pl.BlockSpec((tm, IN_W), lambda i: (i, 0))],
        out_shape=[jax.ShapeDtypeStruct((s, D_MODEL), BF16), jax.ShapeDtypeStruct((s, IN_W), F32)],
        compiler_params=_params(("arbitrary",)),
    )(x, gain, w_in)


def _mixer_fwd(proj, sinks, sin, cos, consts):
    s = proj.shape[0]
    nc = s // CHUNK
    d_intra, xi_full, zeta_full, decay_full = consts

    def body(sk_ref, p_ref, pkv_ref, sin_ref, cos_ref, dm_ref, xi_ref, ze_ref, dc_ref, mix_ref, st_ref, state):
        n = pl.program_id(0)

        @pl.when(n == 0)
        def _():
            state[...] = jnp.zeros_like(state)

        kv_cur = p_ref[:, KV_A0:KV_A0 + 2 * KV_W]
        kv_prev = pkv_ref[...]
        kk = jnp.concatenate([kv_prev[:, :KV_W], kv_cur[:, :KV_W]], axis=0)
        vv = jnp.concatenate([kv_prev[:, KV_W:], kv_cur[:, KV_W:]], axis=0)
        kk_b, kk_rb = kk.astype(BF16), pltpu.roll(kk, HEAD_DIM, 1).astype(BF16)
        vv_r = pltpu.roll(vv, HEAD_DIM, 1)
        valid = _attn_valid(n)
        lo_q = _lane_lo((CHUNK, 128))
        lo_kv = _lane_lo((2 * CHUNK, 128))
        for pi in range(N_ATTN_HEADS // 2):
            q_pair = p_ref[:, Q_A0 + pi * 128:Q_A0 + (pi + 1) * 128]
            acc = jnp.zeros((CHUNK, 128), F32)
            for par in range(2):
                h = 2 * pi + par
                _, _, _, half_kv, qe, k_use, v_use = _attn_head_operands(h, q_pair, kk_b, kk_rb, vv, vv_r, lo_q, lo_kv)
                p, _ = _attn_probs(qe, k_use, valid, sk_ref[h])
                ve = jnp.where(half_kv, v_use, 0.0).astype(BF16)
                acc = acc + _dot(p.astype(BF16), ve)
            mix_ref[:, pi * 128:(pi + 1) * 128] = acc.astype(BF16)

        sin4, cos4 = _tile4(sin_ref[...]), _tile4(cos_ref[...])
        q_r = p_ref[:, Q_R0:Q_R0 + RET_W]
        k_r = p_ref[:, K_R0:K_R0 + RET_W] * RET_K_SCALE
        q_r = q_r * cos4 + _rot2(q_r) * sin4
        k_r = k_r * cos4 + _rot2(k_r) * sin4
        kz = k_r * ze_ref[...]
        for h in range(N_RET_HEADS):
            sl = slice(h * RET_HEAD_DIM, (h + 1) * RET_HEAD_DIM)
            qh, kh = q_r[:, sl].astype(BF16), k_r[:, sl].astype(BF16)
            vh = p_ref[:, V_R0 + h * RET_HEAD_DIM:V_R0 + (h + 1) * RET_HEAD_DIM].astype(BF16)
            st = state[h]
            st_ref[0, h] = st
            a = _dot_nt(qh, kh) * dm_ref[h]
            o = _dot(a.astype(BF16), vh) + _dot(qh, st.astype(BF16)) * xi_ref[:, sl]
            state[h] = dc_ref[0:1, sl] * st + _dot_tn(kz[:, sl].astype(BF16), vh)
            mu = jnp.mean(o, axis=-1, keepdims=True)
            oc = o - mu
            on = oc * lax.rsqrt(jnp.mean(oc * oc, axis=-1, keepdims=True) + GN_EPS)
            g = p_ref[:, G_R0 + h * RET_HEAD_DIM:G_R0 + (h + 1) * RET_HEAD_DIM]
            mix_ref[:, ATTN_W + h * RET_HEAD_DIM:ATTN_W + (h + 1) * RET_HEAD_DIM] = (g * _sigmoid(g) * on).astype(BF16)

    return pl.pallas_call(
        body, name="mixer_fwd", grid=(nc,),
        in_specs=[
            pl.BlockSpec(memory_space=pltpu.SMEM),
            pl.BlockSpec((CHUNK, IN_W), lambda n: (n, 0)),
            pl.BlockSpec((CHUNK, 2 * KV_W), lambda n: (jnp.maximum(n - 1, 0), KV_A0 // (2 * KV_W))),
            pl.BlockSpec((CHUNK, RET_HEAD_DIM), lambda n: (n, 0)),
            pl.BlockSpec((CHUNK, RET_HEAD_DIM), lambda n: (n, 0)),
            _resident((N_RET_HEADS, CHUNK, CHUNK)), _resident((CHUNK, RET_W)), _resident((CHUNK, RET_W)), _resident((8, RET_W)),
        ],
        out_specs=[
            pl.BlockSpec((CHUNK, D_MODEL), lambda n: (n, 0)),
            pl.BlockSpec((1, N_RET_HEADS, RET_HEAD_DIM, RET_HEAD_DIM), lambda n: (n, 0, 0, 0)),
        ],
        out_shape=[jax.ShapeDtypeStruct((s, D_MODEL), BF16),
                   jax.ShapeDtypeStruct((nc, N_RET_HEADS, RET_HEAD_DIM, RET_HEAD_DIM), F32)],
        scratch_shapes=[pltpu.VMEM((N_RET_HEADS, RET_HEAD_DIM, RET_HEAD_DIM), F32)],
        compiler_params=_params(("arbitrary",)),
    )(sinks, proj, proj, sin, cos, d_intra, xi_full, zeta_full, decay_full)


def _out_up_proj(mix, x, w_out, g_post, g_pre, w_up):
    s = x.shape[0]
    tm = TOKEN_TILE

    def body(mix_ref, x_ref, wo_ref, g2_ref, g3_ref, wu_ref, mixed_ref, x1_ref, h2_ref, u0_ref):
        mixed = _dot(mix_ref[...], wo_ref[...])
        mixed_ref[...] = mixed
        x1 = x_ref[...] + mixed * _rstd(mixed) * g2_ref[...]
        x1_ref[...] = x1
        h2 = (x1 * _rstd(x1) * g3_ref[...]).astype(BF16)
        h2_ref[...] = h2
        u0_ref[...] = _dot(h2, wu_ref[...]).astype(BF16)

    tok = lambda w: pl.BlockSpec((tm, w), lambda i: (i, 0))
    return pl.pallas_call(
        body, name="out_up_proj", grid=(s // tm,),
        in_specs=[tok(D_MODEL), tok(D_MODEL), _resident((D_MODEL, D_MODEL)), _resident((1, D_MODEL)), _resident((1, D_MODEL)),
                  _resident((D_MODEL, UP_W))],
        out_specs=[tok(D_MODEL), tok(D_MODEL), tok(D_MODEL), tok(UP_W)],
        out_shape=[jax.ShapeDtypeStruct((s, D_MODEL), F32), jax.ShapeDtypeStruct((s, D_MODEL), F32),
                   jax.ShapeDtypeStruct((s, D_MODEL), BF16), jax.ShapeDtypeStruct((s, UP_W), BF16)],
        compiler_params=_params(("arbitrary",)),
    )(mix, x, w_out, g_post, g_pre, w_up)


def _ffn_tail(u0, x1, target, conv_w, conv_b, w_down, g_post):
    s = x1.shape[0]
    tm = TOKEN_TILE

    def body(u0_ref, x1_ref, t_ref, cw_ref, cb_ref, wd_ref, g_ref,
             y_ref, dy2_ref, dout_ref, du_ref, cacc_ref, gacc_ref, ubuf):
        i = pl.program_id(0)

        @pl.when(i == 0)
        def _():
            ubuf[0:8, :] = jnp.zeros((8, UP_W), F32)
            cacc_ref[...] = jnp.zeros_like(cacc_ref)
            gacc_ref[...] = jnp.zeros_like(gacc_ref)

        ubuf[8:8 + tm, :] = u0_ref[...].astype(F32)
        u_c = ubuf[8:8 + tm, :]
        u_1 = ubuf[7:7 + tm, :]
        u_2 = ubuf[6:6 + tm, :]
        u = cw_ref[0:1, :] * u_2 + cw_ref[1:2, :] * u_1 + cw_ref[2:3, :] * u_c + cb_ref[...]
        gate, val = u[:, :D_FF], u[:, D_FF:]
        g2 = gate * gate
        th = jnp.tanh(GELU_C * gate * (1.0 + GELU_A * g2))
        gelu = 0.5 * gate * (1.0 + th)
        dgelu = 0.5 * (1.0 + th) + 0.5 * gate * (1.0 - th * th) * GELU_C * (1.0 + 3.0 * GELU_A * g2)
        y = (gelu * val).astype(BF16)
        y_ref[...] = y
        y2 = _dot(y, wd_ref[...])
        r4 = _rstd(y2)
        gain = g_ref[...]
        out = x1_ref[...] + y2 * r4 * gain
        diff = out - t_ref[...]
        dout = diff * (1.0 / D_MODEL)
        dout_ref[...] = dout
        dy2, dgain = _rms_bwd(dout, y2, r4, gain)
        dy2_b = dy2.astype(BF16)
        dy2_ref[...] = dy2_b
        gacc_ref[0:1, :] += dgain
        gacc_ref[1:2, :] += 0.5 * jnp.sum(diff * dout, axis=0, keepdims=True)
        dy = _dot_nt(dy2_b, wd_ref[...])
        du = jnp.concatenate([dy * val * dgelu, dy * gelu], axis=-1)
        du_ref[...] = du.astype(BF16)
        cacc_ref[0:1, :] += jnp.sum(du * u_2, axis=0, keepdims=True)
        cacc_ref[1:2, :] += jnp.sum(du * u_1, axis=0, keepdims=True)
        cacc_ref[2:3, :] += jnp.sum(du * u_c, axis=0, keepdims=True)
        cacc_ref[3:4, :] += jnp.sum(du, axis=0, keepdims=True)
        ubuf[0:8, :] = ubuf[tm:tm + 8, :]

    tok = lambda w: pl.BlockSpec((tm, w), lambda i: (i, 0))
    return pl.pallas_call(
        body, name="ffn_tail", grid=(s // tm,),
        in_specs=[tok(UP_W), tok(D_MODEL), tok(D_MODEL), _resident((CONV_WIDTH, UP_W)), _resident((1, UP_W)),
                  _resident((D_FF, D_MODEL)), _resident((1, D_MODEL))],
        out_specs=[tok(D_FF), tok(D_MODEL), tok(D_MODEL), tok(UP_W),
                   pl.BlockSpec((8, UP_W), lambda i: (0, 0)), pl.BlockSpec((8, D_MODEL), lambda i: (0, 0))],
        out_shape=[jax.ShapeDtypeStruct((s, D_FF), BF16), jax.ShapeDtypeStruct((s, D_MODEL), BF16),
                   jax.ShapeDtypeStruct((s, D_MODEL), F32), jax.ShapeDtypeStruct((s, UP_W), BF16),
                   jax.ShapeDtypeStruct((8, UP_W), F32), jax.ShapeDtypeStruct((8, D_MODEL), F32)],
        scratch_shapes=[pltpu.VMEM((tm + 8, UP_W), F32)],
        compiler_params=_params(("arbitrary",)),
    )(u0, x1, target, conv_w, conv_b, w_down, g_post)


def _ffn_head_bwd(du, conv_w, w_up, x1, g_pre, dout, mixed, g_post, w_out):
    s = x1.shape[0]
    tm = TOKEN_TILE
    nt = s // tm

    def body(du_ref, halo_ref, cw_ref, wu_ref, x1_ref, g3_ref, dout_ref, mixed_ref, g2_ref, wo_ref,
             du0_ref, dx1_ref, dmixed_ref, dmix_ref, gacc_ref, dbuf):
        i = pl.program_id(0)

        @pl.when(i == 0)
        def _():
            gacc_ref[...] = jnp.zeros_like(gacc_ref)

        dbuf[0:tm, :] = du_ref[...].astype(F32)
        dbuf[tm:tm + 16, :] = jnp.where(i < nt - 1, halo_ref[...].astype(F32), 0.0)
        du0 = cw_ref[2:3, :] * dbuf[0:tm, :] + cw_ref[1:2, :] * dbuf[1:1 + tm, :] + cw_ref[0:1, :] * dbuf[2:2 + tm, :]
        du0_b = du0.astype(BF16)
        du0_ref[...] = du0_b
        dh2 = _dot_nt(du0_b, wu_ref[...])
        x1 = x1_ref[...]
        d3, dg3 = _rms_bwd(dh2, x1, _rstd(x1), g3_ref[...])
        dx1 = dout_ref[...] + d3
        dx1_ref[...] = dx1
        mixed = mixed_ref[...]
        dmixed, dg2 = _rms_bwd(dx1, mixed, _rstd(mixed), g2_ref[...])
        dmixed_b = dmixed.astype(BF16)
        dmixed_ref[...] = dmixed_b
        dmix_ref[...] = _dot_nt(dmixed_b, wo_ref[...]).astype(BF16)
        gacc_ref[0:1, :] += dg3
        gacc_ref[1:2, :] += dg2

    tok = lambda w: pl.BlockSpec((tm, w), lambda i: (i, 0))
    halo = pl.BlockSpec((16, UP_W), lambda i: (jnp.minimum(i + 1, nt - 1) * (tm // 16), 0))
    return pl.pallas_call(
        body, name="ffn_head_bwd", grid=(nt,),
        in_specs=[tok(UP_W), halo, _resident((CONV_WIDTH, UP_W)), _resident((D_MODEL, UP_W)), tok(D_MODEL),
                  _resident((1, D_MODEL)), tok(D_MODEL), tok(D_MODEL), _resident((1, D_MODEL)), _resident((D_MODEL, D_MODEL))],
        out_specs=[tok(UP_W), tok(D_MODEL), tok(D_MODEL), tok(D_MODEL), pl.BlockSpec((8, D_MODEL), lambda i: (0, 0))],
        out_shape=[jax.ShapeDtypeStruct((s, UP_W), BF16), jax.ShapeDtypeStruct((s, D_MODEL), F32),
                   jax.ShapeDtypeStruct((s, D_MODEL), BF16), jax.ShapeDtypeStruct((s, D_MODEL), BF16),
                   jax.ShapeDtypeStruct((8, D_MODEL), F32)],
        scratch_shapes=[pltpu.VMEM((tm + 16, UP_W), F32)],
        compiler_params=_params(("arbitrary",)),
    )(du, du, conv_w, w_up, x1, g_pre, dout, mixed, g_post, w_out)


def _mixer_bwd(proj, dmix, states, sinks, sin, cos, consts):
    s = proj.shape[0]
    nc = s // CHUNK
    d_intra, xi_full, zeta_full, decay_full = consts

    def body(sk_ref, p_ref, pkv_ref, dmix_ref, st_ref, sin_ref, cos_ref, dm_ref, xi_ref, ze_ref, dc_ref,
             dp_ref, dsk_ref, gstate, ckv, dsk_acc):
        i = pl.program_id(0)
        n = nc - 1 - i

        @pl.when(i == 0)
        def _():
            gstate[...] = jnp.zeros_like(gstate)
            ckv[...] = jnp.zeros_like(ckv)
            dsk_acc[...] = jnp.zeros_like(dsk_acc)

        kv_cur = p_ref[:, KV_A0:KV_A0 + 2 * KV_W]
        kv_prev = pkv_ref[...]
        kk = jnp.concatenate([kv_prev[:, :KV_W], kv_cur[:, :KV_W]], axis=0)
        vv = jnp.concatenate([kv_prev[:, KV_W:], kv_cur[:, KV_W:]], axis=0)
        kk_b, kk_rb = kk.astype(BF16), pltpu.roll(kk, HEAD_DIM, 1).astype(BF16)
        vv_b, vv_rb = vv.astype(BF16), pltpu.roll(vv, HEAD_DIM, 1).astype(BF16)
        valid = _attn_valid(n)
        lo_q = _lane_lo((CHUNK, 128))
        lo_kv = _lane_lo((2 * CHUNK, 128))
        lane = lax.broadcasted_iota(jnp.int32, (CHUNK, 128), 1)
        dkk = jnp.zeros((2 * CHUNK, KV_W), F32)
        dvv = jnp.zeros((2 * CHUNK, KV_W), F32)
        dsk = jnp.zeros((CHUNK, 128), F32)
        for pi in range(N_ATTN_HEADS // 2):
            q_pair = p_ref[:, Q_A0 + pi * 128:Q_A0 + (pi + 1) * 128]
            do_pair = dmix_ref[:, pi * 128:(pi + 1) * 128].astype(F32)
            dq_pair = jnp.zeros((CHUNK, 128), F32)
            for par in range(2):
                h = 2 * pi + par
                _, hk, half_q, half_kv, qe, k_use, v_use = _attn_head_operands(h, q_pair, kk_b, kk_rb, vv_b, vv_rb, lo_q, lo_kv)
                p, p_sink = _attn_probs(qe, k_use, valid, sk_ref[h])
                doe = jnp.where(half_q, do_pair, 0.0).astype(BF16)
                dpr = _dot_nt(doe, v_use)
                delta = jnp.sum(p * dpr, axis=-1, keepdims=True)
                ds_b = (p * (dpr - delta) * ATTN_SCALE).astype(BF16)
                dsk = dsk + jnp.where(lane == h, -p_sink * delta, 0.0)
                dq_pair = dq_pair + _dot(ds_b, jnp.where(half_kv, k_use, 0.0).astype(BF16))
                dk_h = _dot_tn(ds_b, qe)
                dv_h = _dot_tn(p.astype(BF16), doe)
                if hk != par:
                    dk_h, dv_h = pltpu.roll(dk_h, HEAD_DIM, 1), pltpu.roll(dv_h, HEAD_DIM, 1)
                dkk = dkk + dk_h
                dvv = dvv + dv_h
            dp_ref[:, Q_A0 + pi * 128:Q_A0 + (pi + 1) * 128] = dq_pair.astype(BF16)
        dp_ref[:, KV_A0:KV_A0 + KV_W] = (dkk[CHUNK:] + ckv[:, :KV_W]).astype(BF16)
        dp_ref[:, KV_A0 + KV_W:KV_A0 + 2 * KV_W] = (dvv[CHUNK:] + ckv[:, KV_W:]).astype(BF16)
        ckv[:, :KV_W] = dkk[:CHUNK]
        ckv[:, KV_W:] = dvv[:CHUNK]
        dsk_acc[...] += dsk

        @pl.when(i == nc - 1)
        def _():
            dsk_ref[...] = jnp.sum(dsk_acc[...], axis=0, keepdims=True)

        sin4, cos4 = _tile4(sin_ref[...]), _tile4(cos_ref[...])
        q_r = p_ref[:, Q_R0:Q_R0 + RET_W]
        k_r = p_ref[:, K_R0:K_R0 + RET_W] * RET_K_SCALE
        q_r = q_r * cos4 + _rot2(q_r) * sin4
        k_r = k_r * cos4 + _rot2(k_r) * sin4
        kz = k_r * ze_ref[...]
        dq_parts, dk_parts = [], []
        for h in range(N_RET_HEADS):
            sl = slice(h * RET_HEAD_DIM, (h + 1) * RET_HEAD_DIM)
            qh, kh = q_r[:, sl].astype(BF16), k_r[:, sl].astype(BF16)
            vh = p_ref[:, V_R0 + h * RET_HEAD_DIM:V_R0 + (h + 1) * RET_HEAD_DIM].astype(BF16)
            st_b = st_ref[0, h].astype(BF16)
            gs = gstate[h]
            gs_b = gs.astype(BF16)
            xi_h = xi_ref[:, sl]
            dm = dm_ref[h]
            a_b = (_dot_nt(qh, kh) * dm).astype(BF16)
            o = _dot(a_b, vh) + _dot(qh, st_b) * xi_h
            mu = jnp.mean(o, axis=-1, keepdims=True)
            oc = o - mu
            rs = lax.rsqrt(jnp.mean(oc * oc, axis=-1, keepdims=True) + GN_EPS)
            on = oc * rs
            g = p_ref[:, G_R0 + h * RET_HEAD_DIM:G_R0 + (h + 1) * RET_HEAD_DIM]
            sg = _sigmoid(g)
            dr = dmix_ref[:, ATTN_W + h * RET_HEAD_DIM:ATTN_W + (h + 1) * RET_HEAD_DIM].astype(F32)
            dp_ref[:, G_R0 + h * RET_HEAD_DIM:G_R0 + (h + 1) * RET_HEAD_DIM] = (
                dr * on * (sg * (1.0 + g * (1.0 - sg)))).astype(BF16)
            don = dr * g * sg
            do = rs * (don - jnp.mean(don, axis=-1, keepdims=True) - on * jnp.mean(don * on, axis=-1, keepdims=True))
            do_b = do.astype(BF16)
            dox_b = (do * xi_h).astype(BF16)
            da_b = (_dot_nt(do_b, vh) * dm).astype(BF16)
            dq_parts.append(_dot(da_b, kh) + _dot_nt(dox_b, st_b))
            dk_parts.append(_dot_tn(da_b, qh) + ze_ref[:, sl] * _dot_nt(vh, gs_b))
            dv = _dot_tn(a_b, do_b) + _dot(kz[:, sl].astype(BF16), gs_b)
            dp_ref[:, V_R0 + h * RET_HEAD_DIM:V_R0 + (h + 1) * RET_HEAD_DIM] = dv.astype(BF16)
            gstate[h] = dc_ref[0:1, sl] * gs + _dot_tn(qh, dox_b)
        dq = jnp.concatenate(dq_parts, axis=-1)
        dk = jnp.concatenate(dk_parts, axis=-1)
        dp_ref[:, Q_R0:Q_R0 + RET_W] = (dq * cos4 - _rot2(dq * sin4)).astype(BF16)
        dp_ref[:, K_R0:K_R0 + RET_W] = (RET_K_SCALE * (dk * cos4 - _rot2(dk * sin4))).astype(BF16)

    rev = lambda i: nc - 1 - i
    return pl.pallas_call(
        body, name="mixer_bwd", grid=(nc,),
        in_specs=[
            pl.BlockSpec(memory_space=pltpu.SMEM),
            pl.BlockSpec((CHUNK, IN_W), lambda i: (rev(i), 0)),
            pl.BlockSpec((CHUNK, 2 * KV_W), lambda i: (jnp.maximum(rev(i) - 1, 0), KV_A0 // (2 * KV_W))),
            pl.BlockSpec((CHUNK, D_MODEL), lambda i: (rev(i), 0)),
            pl.BlockSpec((1, N_RET_HEADS, RET_HEAD_DIM, RET_HEAD_DIM), lambda i: (rev(i), 0, 0, 0)),
            pl.BlockSpec((CHUNK, RET_HEAD_DIM), lambda i: (rev(i), 0)),
            pl.BlockSpec((CHUNK, RET_HEAD_DIM), lambda i: (rev(i), 0)),
            _resident((N_RET_HEADS, CHUNK, CHUNK)), _resident((CHUNK, RET_W)), _resident((CHUNK, RET_W)), _resident((8, RET_W)),
        ],
        out_specs=[pl.BlockSpec((CHUNK, IN_W), lambda i: (rev(i), 0)), pl.BlockSpec((1, 128), lambda i: (0, 0))],
        out_shape=[jax.ShapeDtypeStruct((s, IN_W), BF16), jax.ShapeDtypeStruct((1, 128), F32)],
        scratch_shapes=[pltpu.VMEM((N_RET_HEADS, RET_HEAD_DIM, RET_HEAD_DIM), F32), pltpu.VMEM((CHUNK, 2 * KV_W), F32),
                        pltpu.VMEM((CHUNK, 128), F32)],
        compiler_params=_params(("arbitrary",)),
    )(sinks, proj, proj, dmix, states, sin, cos, d_intra, xi_full, zeta_full, decay_full)


def _in_proj_bwd(dproj, w_in, x, gain, dx1):
    s = x.shape[0]
    tm = TOKEN_TILE

    def body(dp_ref, w_ref, x_ref, g_ref, dx1_ref, dx_ref, gacc_ref):
        @pl.when(pl.program_id(0) == 0)
        def _():
            gacc_ref[...] = jnp.zeros_like(gacc_ref)

        dh = _dot_nt(dp_ref[...], w_ref[...])
        xv = x_ref[...]
        d1, dg = _rms_bwd(dh, xv, _rstd(xv), g_ref[...])
        dx_ref[...] = dx1_ref[...] + d1
        gacc_ref[0:1, :] += dg

    tok = lambda w: pl.BlockSpec((tm, w), lambda i: (i, 0))
    return pl.pallas_call(
        body, name="in_proj_bwd", grid=(s // tm,),
        in_specs=[tok(IN_W), _resident((D_MODEL, IN_W)), tok(D_MODEL), _resident((1, D_MODEL)), tok(D_MODEL)],
        out_specs=[tok(D_MODEL), pl.BlockSpec((8, D_MODEL), lambda i: (0, 0))],
        out_shape=[jax.ShapeDtypeStruct((s, D_MODEL), F32), jax.ShapeDtypeStruct((8, D_MODEL), F32)],
        compiler_params=_params(("arbitrary",)),
    )(dproj, w_in, x, gain, dx1)


def _weight_grad(a, b, tn, name):
    s, m = a.shape
    n = b.shape[1]
    tk = min(512, s)

    def body(a_ref, b_ref, o_ref):
        @pl.when(pl.program_id(1) == 0)
        def _():
            o_ref[...] = jnp.zeros_like(o_ref)

        o_ref[...] += _dot_tn(a_ref[...], b_ref[...])

    return pl.pallas_call(
        body, name=name, grid=(n // tn, s // tk),
        in_specs=[pl.BlockSpec((tk, m), lambda j, k: (k, 0)), pl.BlockSpec((tk, tn), lambda j, k: (k, j))],
        out_specs=pl.BlockSpec((m, tn), lambda j, k: (0, j)),
        out_shape=jax.ShapeDtypeStruct((m, n), F32),
        compiler_params=_params(("arbitrary", "arbitrary")),
    )(a, b)


def _place():
    return lax.axis_index("x"), lax.axis_index("y"), lax.axis_index("c")


def _gather_weights(packed_shard):
    rows, cols = packed_shard.shape
    m = rows // 2

    def body(w_ref, out_ref, send_sems, recv_sems, local_sem):
        x, y, c = _place()
        me, sibling = (x, y, c), (x, y, 1 - c)
        chips = [(1 - x, y), (x, 1 - y), (1 - x, 1 - y)]

        def block(px, py, pc):
            return out_ref.at[pl.ds((4 * px + 2 * py + pc) * m, m), :]

        mine_src = w_ref.at[pl.ds(c * m, m), :]

        def copy(k, blk, to, src=None):
            return pltpu.make_async_remote_copy(
                src_ref=block(*blk) if src is None else src, dst_ref=block(*blk),
                send_sem=send_sems.at[k], recv_sem=recv_sems.at[k], device_id=to, device_id_type=MESH)

        mine = pltpu.make_async_copy(mine_src, block(*me), local_sem)
        mine.start()
        first = [copy(0, me, sibling, src=mine_src)]
        first += [copy(1 + j, me, (*chip, c), src=mine_src) for j, chip in enumerate(chips)]
        for cp in first:
            cp.start()
        passed = [copy(4 + j, (*chip, c), sibling) for j, chip in enumerate(chips)]
        for j, chip in enumerate(chips):
            copy(1 + j, (*chip, c), me).wait_recv()
            passed[j].start()
        copy(0, sibling, me).wait_recv()
        for j, chip in enumerate(chips):
            copy(4 + j, (*chip, 1 - c), me).wait_recv()
        for cp in first + passed:
            cp.wait_send()
        mine.wait()

    out = pl.pallas_call(
        body, name="gather_weights",
        out_shape=jax.ShapeDtypeStruct((N_DEV * m, cols), packed_shard.dtype),
        in_specs=[pl.BlockSpec(memory_space=pl.ANY)],
        out_specs=pl.BlockSpec(memory_space=pl.ANY),
        scratch_shapes=[pltpu.SemaphoreType.DMA((7,)), pltpu.SemaphoreType.DMA((7,)), pltpu.SemaphoreType.DMA],
    )(packed_shard)
    return out.reshape(N_CHIPS, rows, cols)


def _relations(x, y, c):
    rel = []
    for fx in (0, 1):
        for fy in (0, 1):
            for fc in (0, 1):
                if fx or fy or fc:
                    rel.append(((1 - x) if fx else x, (1 - y) if fy else y, (1 - c) if fc else c))
    return rel


def _gather_small(v, name):
    r, cols = v.shape

    def body(v_ref, out_ref, send_sems, recv_sems):
        x, y, c = _place()
        peers = _relations(x, y, c)

        def slot(p):
            return out_ref.at[4 * p[0] + 2 * p[1] + p[2]]

        out_ref[4 * x + 2 * y + c] = v_ref[...]
        sends = [pltpu.make_async_remote_copy(
            src_ref=v_ref, dst_ref=slot((x, y, c)), send_sem=send_sems.at[k], recv_sem=recv_sems.at[k],
            device_id=p, device_id_type=MESH) for k, p in enumerate(peers)]
        for cp in sends:
            cp.start()
        for k, p in enumerate(peers):
            pltpu.make_async_remote_copy(
                src_ref=v_ref, dst_ref=slot(p), send_sem=send_sems.at[k], recv_sem=recv_sems.at[k],
                device_id=p, device_id_type=MESH).wait_recv()
        for cp in sends:
            cp.wait_send()

    return pl.pallas_call(
        body, name=name,
        out_shape=jax.ShapeDtypeStruct((N_DEV, r, cols), v.dtype),
        in_specs=[pl.BlockSpec(memory_space=pltpu.VMEM)],
        out_specs=pl.BlockSpec(memory_space=pltpu.VMEM),
        scratch_shapes=[pltpu.SemaphoreType.DMA((7,)), pltpu.SemaphoreType.DMA((7,))],
    )(v)


def _swap_halves(g):
    n, _, m, cols = g.shape

    def body(g_ref, out_ref, send_sem, recv_sem):
        x, y, c = _place()
        cp = pltpu.make_async_remote_copy(
            src_ref=g_ref.at[:, pl.ds(1 - c, 1)], dst_ref=out_ref, send_sem=send_sem, recv_sem=recv_sem,
            device_id=(x, y, 1 - c), device_id_type=MESH)
        cp.start()
        cp.wait()

    return pl.pallas_call(
        body, name="grad_swap_halves",
        out_shape=jax.ShapeDtypeStruct((n, 1, m, cols), g.dtype),
        in_specs=[pl.BlockSpec(memory_space=pl.ANY)],
        out_specs=pl.BlockSpec(memory_space=pl.ANY),
        scratch_shapes=[pltpu.SemaphoreType.DMA, pltpu.SemaphoreType.DMA],
    )(g)


def _scatter_to_chips(g):
    n, m, cols = g.shape

    def body(g_ref, out_ref, send_sems, recv_sems):
        x, y, c = _place()
        chips = [(1 - x, y), (x, 1 - y), (1 - x, 1 - y)]
        sends = [pltpu.make_async_remote_copy(
            src_ref=g_ref.at[2 * px + py], dst_ref=out_ref.at[k], send_sem=send_sems.at[k], recv_sem=recv_sems.at[k],
            device_id=(px, py, c), device_id_type=MESH) for k, (px, py) in enumerate(chips)]
        for cp in sends:
            cp.start()
        for cp in sends:
            cp.wait()

    return pl.pallas_call(
        body, name="grad_scatter_chips",
        out_shape=jax.ShapeDtypeStruct((3, m, cols), g.dtype),
        in_specs=[pl.BlockSpec(memory_space=pl.ANY)],
        out_specs=pl.BlockSpec(memory_space=pl.ANY),
        scratch_shapes=[pltpu.SemaphoreType.DMA((3,)), pltpu.SemaphoreType.DMA((3,))],
    )(g)


def _join_halves(t):
    m, cols = t.shape

    def body(t_ref, out_ref, send_sem, recv_sem, local_sem):
        x, y, c = _place()
        mine = pltpu.make_async_copy(t_ref, out_ref.at[c], local_sem)
        mine.start()
        cp = pltpu.make_async_remote_copy(
            src_ref=t_ref, dst_ref=out_ref.at[c], send_sem=send_sem, recv_sem=recv_sem,
            device_id=(x, y, 1 - c), device_id_type=MESH)
        cp.start()
        pltpu.make_async_remote_copy(
            src_ref=t_ref, dst_ref=out_ref.at[1 - c], send_sem=send_sem, recv_sem=recv_sem,
            device_id=(x, y, 1 - c), device_id_type=MESH).wait_recv()
        cp.wait_send()
        mine.wait()

    return pl.pallas_call(
        body, name="grad_join_halves",
        out_shape=jax.ShapeDtypeStruct((2, m, cols), t.dtype),
        in_specs=[pl.BlockSpec(memory_space=pl.ANY)],
        out_specs=pl.BlockSpec(memory_space=pl.ANY),
        scratch_shapes=[pltpu.SemaphoreType.DMA, pltpu.SemaphoreType.DMA, pltpu.SemaphoreType.DMA],
    )(t)


ROW_TILE = 512


def _add_my_half(g, other, c):
    n, _, m, cols = g.shape
    tr = ROW_TILE

    def body(c_ref, g_ref, o_ref, out_ref):
        out_ref[...] = g_ref[...] + o_ref[...]

    return pl.pallas_call(
        body, name="grad_add_halves",
        grid_spec=pltpu.PrefetchScalarGridSpec(
            num_scalar_prefetch=1, grid=(n, m // tr),
            in_specs=[pl.BlockSpec((None, None, tr, cols), lambda s, r, c_ref: (s, c_ref[0], r, 0)),
                      pl.BlockSpec((None, None, tr, cols), lambda s, r, c_ref: (s, 0, r, 0))],
            out_specs=pl.BlockSpec((None, tr, cols), lambda s, r, c_ref: (s, r, 0))),
        out_shape=jax.ShapeDtypeStruct((n, m, cols), g.dtype),
        compiler_params=_params(("arbitrary", "arbitrary")),
    )(c, g, other)


def _add_chips(g, got, shard):
    n, m, cols = g.shape
    tr = ROW_TILE

    def body(s_ref, g_ref, r_ref, out_ref):
        out_ref[...] = ((g_ref[...] + r_ref[0]) + r_ref[1]) + r_ref[2]

    return pl.pallas_call(
        body, name="grad_add_chips",
        grid_spec=pltpu.PrefetchScalarGridSpec(
            num_scalar_prefetch=1, grid=(m // tr,),
            in_specs=[pl.BlockSpec((None, tr, cols), lambda r, s_ref: (s_ref[0], r, 0)),
                      pl.BlockSpec((3, tr, cols), lambda r, s_ref: (0, r, 0))],
            out_specs=pl.BlockSpec((tr, cols), lambda r, s_ref: (r, 0))),
        out_shape=jax.ShapeDtypeStruct((m, cols), g.dtype),
        compiler_params=_params(("arbitrary",)),
    )(shard, g, got)


def _adamw_math(w, g, m, v):
    m = ADAM_B1 * m + (1.0 - ADAM_B1) * g
    v = ADAM_B2 * v + (1.0 - ADAM_B2) * (g * g)
    m_hat = m / (1.0 - ADAM_B1 ** ADAM_STEP)
    v_hat = v / (1.0 - ADAM_B2 ** ADAM_STEP)
    delta = -ADAM_LR * (m_hat / (jnp.sqrt(v_hat) + ADAM_EPS) + ADAM_WD * w)
    return delta, m, v


def _adamw(w, g, m, v, name):
    r, cols = w.shape
    tr = ROW_TILE if r % ROW_TILE == 0 else r

    def body(w_ref, g_ref, m_ref, v_ref, d_ref, nm_ref, nv_ref):
        d_ref[...], nm_ref[...], nv_ref[...] = _adamw_math(w_ref[...], g_ref[...], m_ref[...], v_ref[...])

    blk = pl.BlockSpec((tr, cols), lambda i: (i, 0))
    shape = jax.ShapeDtypeStruct((r, cols), F32)
    return pl.pallas_call(
        body, name=name, grid=(r // tr,), in_specs=[blk] * 4, out_specs=[blk] * 3, out_shape=[shape] * 3,
        compiler_params=_params(("arbitrary",)),
    )(w, g, m, v)


def _sum_devices(gathered):
    _, r, cols = gathered.shape

    def body(a_ref, g_ref):
        g = a_ref[0]
        for k in range(1, N_DEV):
            g = g + a_ref[k]
        g_ref[...] = g

    return pl.pallas_call(body, name="sum_small_grads", out_shape=jax.ShapeDtypeStruct((r, cols), F32))(gathered)


def _pack_conv(cw):
    flat = cw.reshape(-1)
    return jnp.pad(flat, (0, ROWS_CONV * D_MODEL - flat.shape[0])).reshape(ROWS_CONV, D_MODEL)


def _unpack_conv(rows):
    return rows.reshape(-1)[:CONV_WIDTH * UP_W // N_CHIPS].reshape(CONV_WIDTH, UP_W // N_CHIPS)


def _pack_shard(w_in, w_out, w_up, w_down):
    return jnp.concatenate([w_in.reshape(ROWS_W_IN, D_MODEL), w_out, w_up.reshape(ROWS_W_UP, D_MODEL), w_down], axis=0)


def _unpack_shard(p):
    o = 0
    w_in = p[o:o + ROWS_W_IN].reshape(D_MODEL, IN_W // N_CHIPS); o += ROWS_W_IN
    w_out = p[o:o + ROWS_W_OUT]; o += ROWS_W_OUT
    w_up = p[o:o + ROWS_W_UP].reshape(D_MODEL, UP_W // N_CHIPS); o += ROWS_W_UP
    w_down = p[o:o + ROWS_W_DOWN]
    return dict(w_in=w_in, w_out=w_out, w_up=w_up, w_down=w_down)


def _columns_to_shards(w):
    r, n = w.shape
    return jnp.transpose(w.reshape(r, N_CHIPS, n // N_CHIPS), (1, 0, 2))


def _shards_to_columns(w):
    _, r, n = w.shape
    return jnp.transpose(w, (1, 0, 2)).reshape(r, N_CHIPS * n)


def _pack_small(g_mix_pre, g_mix_post, g_ffn_pre, g_ffn_post, sinks, conv_b, loss):
    pad_row = lambda v: jnp.pad(v.reshape(1, -1), ((0, 0), (0, D_MODEL - v.size)))
    cb = jnp.pad(conv_b.reshape(-1), (0, 6 * D_MODEL - UP_W)).reshape(6, D_MODEL)
    zeros2 = jnp.zeros((2, D_MODEL), F32)
    return jnp.concatenate([g_mix_pre.reshape(1, -1), g_mix_post.reshape(1, -1), g_ffn_pre.reshape(1, -1),
                            g_ffn_post.reshape(1, -1), pad_row(sinks), pad_row(loss), zeros2, cb, zeros2], axis=0)


def _unpack_small(p):
    return dict(mix_pre_norm=p[0:1], mix_post_norm=p[1:2], ffn_pre_norm=p[2:3], ffn_post_norm=p[3:4],
                attn_sinks=p[4:5, :N_ATTN_HEADS], loss=p[5, 0], conv_b=p[8:14].reshape(1, -1)[:, :UP_W],
                conv_w=_unpack_conv(p[SMALL_ROWS:SMALL_ROWS + ROWS_CONV]))


def _local_step(x, target, g_mix_pre, w_in, sinks, w_out, g_mix_post, g_ffn_pre, w_up, conv_w, conv_b, w_down, g_ffn_post):
    s = x.shape[0]
    consts = _ret_constants()
    sin, cos = _rope_tables(s)

    h1, proj = _in_proj(x, g_mix_pre, w_in)
    mix, states = _mixer_fwd(proj, sinks, sin, cos, consts)
    mixed, x1, h2, u0 = _out_up_proj(mix, x, w_out, g_mix_post, g_ffn_pre, w_up)
    y, dy2, dout, du, conv_acc, tail_acc = _ffn_tail(u0, x1, target, conv_w, conv_b, w_down, g_ffn_post)
    du0, dx1, dmixed, dmix, head_acc = _ffn_head_bwd(du, conv_w, w_up, x1, g_ffn_pre, dout, mixed, g_mix_post, w_out)
    dproj, dsinks = _mixer_bwd(proj, dmix, states, sinks, sin, cos, consts)
    grad_x, in_acc = _in_proj_bwd(dproj, w_in, x, g_mix_pre, dx1)

    d_w_down = _weight_grad(y, dy2, 512, "grad_w_down")
    d_w_up = _weight_grad(h2, du0, UP_W // N_CHIPS, "grad_w_up")
    d_w_out = _weight_grad(mix, dmixed, D_MODEL, "grad_w_out")
    d_w_in = _weight_grad(h1, dproj, IN_W // 2, "grad_w_in")

    small = _pack_small(in_acc[0], head_acc[1], head_acc[0], tail_acc[0], dsinks[0, :N_ATTN_HEADS], conv_acc[3],
                        jnp.sum(tail_acc[1]))
    d_conv = jnp.pad(conv_acc[0:CONV_WIDTH].reshape(-1), (0, CONV_FULL_ROWS * D_MODEL - CONV_WIDTH * UP_W))
    small = jnp.concatenate([small, d_conv.reshape(CONV_FULL_ROWS, D_MODEL)], axis=0)
    return grad_x, d_w_in, d_w_out, d_w_up, d_w_down, small


def kernel(x, mix_pre_norm, w_in, attn_sinks, w_out, mix_post_norm, ffn_pre_norm, w_up, conv_w, conv_b, w_down, ffn_post_norm, loss_target, m_mix_pre_norm, m_w_in, m_attn_sinks, m_w_out, m_mix_post_norm, m_ffn_pre_norm, m_w_up, m_conv_w, m_conv_b, m_w_down, m_ffn_post_norm, v_mix_pre_norm, v_w_in, v_attn_sinks, v_w_out, v_mix_post_norm, v_ffn_pre_norm, v_w_up, v_conv_w, v_conv_b, v_w_down, v_ffn_post_norm):
    cx, cy, cc = _place()
    shard = 2 * cx + cy

    w_p = _pack_shard(w_in[0], w_out[0], w_up[0], w_down[0])
    all_mats = _gather_weights(w_p.astype(BF16))
    conv_all = _gather_small(_pack_conv(conv_w[0]), "gather_conv_w")
    conv_full = jnp.concatenate([_unpack_conv(conv_all[2 * k]) for k in range(N_CHIPS)], axis=1)
    o = 0
    w_in_f = _shards_to_columns(all_mats[:, o:o + ROWS_W_IN].reshape(N_CHIPS, D_MODEL, IN_W // N_CHIPS)); o += ROWS_W_IN
    w_out_f = all_mats[:, o:o + ROWS_W_OUT].reshape(D_MODEL, D_MODEL); o += ROWS_W_OUT
    w_up_f = _shards_to_columns(all_mats[:, o:o + ROWS_W_UP].reshape(N_CHIPS, D_MODEL, UP_W // N_CHIPS)); o += ROWS_W_UP
    w_down_f = all_mats[:, o:o + ROWS_W_DOWN].reshape(D_FF, D_MODEL)

    grad_x, d_w_in, d_w_out, d_w_up, d_w_down, small = _local_step(
        x[0], loss_target[0], mix_pre_norm, w_in_f, attn_sinks.reshape(-1), w_out_f, mix_post_norm, ffn_pre_norm, w_up_f,
        conv_full, conv_b, w_down_f, ffn_post_norm)

    packed = jnp.concatenate([
        _columns_to_shards(d_w_in).reshape(N_CHIPS, ROWS_W_IN, D_MODEL),
        d_w_out.reshape(N_CHIPS, ROWS_W_OUT, D_MODEL),
        _columns_to_shards(d_w_up).reshape(N_CHIPS, ROWS_W_UP, D_MODEL),
        d_w_down.reshape(N_CHIPS, ROWS_W_DOWN, D_MODEL)], axis=1)
    half = ROWS_PACK // 2
    packed = packed.reshape(N_CHIPS, 2, half, D_MODEL)
    from_sibling = _swap_halves(packed)
    chip_sum = _add_my_half(packed, from_sibling, cc.reshape(1).astype(jnp.int32))
    from_chips = _scatter_to_chips(chip_sum)
    my_half = _add_chips(chip_sum, from_chips, shard.reshape(1).astype(jnp.int32))
    g_shard = _join_halves(my_half).reshape(ROWS_PACK, D_MODEL)

    m_p = _pack_shard(m_w_in[0], m_w_out[0], m_w_up[0], m_w_down[0])
    v_p = _pack_shard(v_w_in[0], v_w_out[0], v_w_up[0], v_w_down[0])
    delta_p, new_m_p, new_v_p = _adamw(w_p, g_shard, m_p, v_p, "adamw_shard")

    small_sum = _sum_devices(_gather_small(small, "gather_small_grads"))
    d_conv_full = small_sum[SMALL_ROWS:].reshape(-1)[:CONV_WIDTH * UP_W].reshape(CONV_WIDTH, UP_W)
    d_conv_mine = lax.dynamic_slice_in_dim(d_conv_full, shard * (UP_W // N_CHIPS), UP_W // N_CHIPS, axis=1)
    g_s = jnp.concatenate([small_sum[:SMALL_ROWS], _pack_conv(d_conv_mine)], axis=0)
    zero = jnp.zeros((), F32)
    pack_rep = lambda a, b, c_, d, e, f, cw: jnp.concatenate([_pack_small(a, b, c_, d, e, f, zero), _pack_conv(cw[0])], axis=0)
    w_s = pack_rep(mix_pre_norm, mix_post_norm, ffn_pre_norm, ffn_post_norm, attn_sinks, conv_b, conv_w)
    m_s = pack_rep(m_mix_pre_norm, m_mix_post_norm, m_ffn_pre_norm, m_ffn_post_norm, m_attn_sinks, m_conv_b, m_conv_w)
    v_s = pack_rep(v_mix_pre_norm, v_mix_post_norm, v_ffn_pre_norm, v_ffn_post_norm, v_attn_sinks, v_conv_b, v_conv_w)
    delta_s, new_m_s, new_v_s = _adamw(w_s, g_s, m_s, v_s, "adamw_small")

    names = ["mix_pre_norm", "w_in", "attn_sinks", "w_out", "mix_post_norm", "ffn_pre_norm", "w_up", "conv_w", "conv_b",
             "w_down", "ffn_post_norm"]

    def leaves(packed_shard, packed_small):
        mats_ = _unpack_shard(packed_shard)
        smalls = _unpack_small(packed_small)
        return [mats_[n][None] if n in mats_ else (smalls[n][None] if n == "conv_w" else smalls[n]) for n in names]

    loss = _unpack_small(g_s)["loss"]
    return (loss, grad_x[None], *leaves(g_shard, g_s), *leaves(delta_p, delta_s), *leaves(new_m_p, new_m_s),
            *leaves(new_v_p, new_v_s))
```

```python
import functools
import math

import jax
import jax.numpy as jnp
from jax import lax
from jax.experimental import pallas as pl
from jax.experimental.pallas import tpu as pltpu

F32 = jnp.float32
BF16 = jnp.bfloat16

D_MODEL = 1024
HEAD_DIM = 64
ATTN_W = 512
N_ATTN_HEADS = 8
KV_W = 128
RET_W = 512
N_RET_HEADS = 4
RET_HEAD_DIM = 128
CHUNK = 128
IN_W = 2816
D_FF = 2816
UP_W = 2 * D_FF
CONV_WIDTH = 3
RMS_EPS = 1e-6
GN_EPS = 1e-6
MASK_VALUE = -1e30
ATTN_SCALE = HEAD_DIM ** -0.5
RET_K_SCALE = RET_HEAD_DIM ** -0.5
GELU_C = math.sqrt(2.0 / math.pi)
GELU_A = 0.044715

ADAM_LR = 0.001
ADAM_B1 = 0.9
ADAM_B2 = 0.999
ADAM_EPS = 1e-08
ADAM_WD = 0.01
ADAM_STEP = 10

N_CHIPS = 4
N_DEV = 8
MESH = pl.DeviceIdType.MESH
VMEM_LIMIT_V7X = 56 * 1024 * 1024
TOKEN_TILE = 256
BIG_TOKEN_TILE = 512
WEIGHT_GRAD_TOKENS = 1024

Q_A0, KV_A0, Q_R0, K_R0, V_R0, G_R0 = 0, 512, 768, 1280, 1792, 2304

ROWS_W_IN, ROWS_W_OUT, ROWS_W_UP, ROWS_W_DOWN = 704, 256, 1408, 704
ROWS_PACK = ROWS_W_IN + ROWS_W_OUT + ROWS_W_UP + ROWS_W_DOWN
ROWS_CONV = 8
SMALL_ROWS = 16
CONV_FULL_ROWS = 24


def _params(sem=None, **kw):
    if sem is not None:
        kw["dimension_semantics"] = sem
    return pltpu.CompilerParams(vmem_limit_bytes=VMEM_LIMIT_V7X, **kw)


def _resident(shape):
    zeros = (0,) * len(shape)
    return pl.BlockSpec(shape, lambda *_: zeros, pipeline_mode=pl.Buffered(1))


class _Hosted:
    def __init__(self, ins, outs, aliases, n_pairs, n_local, start, finish):
        self.ins, self.outs, self.aliases = list(ins), list(outs), dict(aliases)
        self.n_pairs, self.n_local, self.start, self.finish = n_pairs, max(n_local, 1), start, finish


def _hosted_call(compute, *, name, grid, in_specs, out_specs, out_shape, scratch_shapes, args, hosted=None):
    params = _params(("arbitrary",))
    if hosted is None:
        res = pl.pallas_call(compute, name=name, grid=grid, in_specs=in_specs, out_specs=out_specs, out_shape=out_shape,
                             scratch_shapes=scratch_shapes, compiler_params=params)(*args)
        return list(res), []
    n_in, n_out, n_scr = len(in_specs), len(out_specs), len(scratch_shapes)
    h_in, h_out = len(hosted.ins), len(hosted.outs)
    last = grid[0] - 1

    def body(*refs):
        ins, refs = refs[:n_in], refs[n_in:]
        h_ins, refs = refs[:h_in], refs[h_in:]
        outs, refs = refs[:n_out], refs[n_out:]
        h_outs, refs = refs[:h_out], refs[h_out:]
        scr, sems = refs[:n_scr], refs[n_scr:]

        @pl.when(pl.program_id(0) == 0)
        def _():
            hosted.start(h_ins, h_outs, *sems)

        compute(*ins, *outs, *scr)

        @pl.when(pl.program_id(0) == last)
        def _():
            hosted.finish(h_ins, h_outs, *sems)

    hbm = pl.BlockSpec(memory_space=pl.ANY)
    res = pl.pallas_call(
        body, name=name, grid=grid,
        in_specs=list(in_specs) + [hbm] * h_in, out_specs=list(out_specs) + [hbm] * h_out,
        out_shape=list(out_shape) + hosted.outs,
        scratch_shapes=list(scratch_shapes) + [pltpu.SemaphoreType.DMA((hosted.n_pairs,)), pltpu.SemaphoreType.DMA((hosted.n_pairs,)),
                                               pltpu.SemaphoreType.DMA((hosted.n_local,))],
        input_output_aliases={n_in + a: n_out + b for a, b in hosted.aliases.items()},
        compiler_params=params,
    )(*args, *hosted.ins)
    return list(res[:n_out]), list(res[n_out:])


def _dot(a, b):
    return jnp.dot(a, b, preferred_element_type=F32)


def _dot_nt(a, b):
    return lax.dot_general(a, b, (((1,), (1,)), ((), ())), preferred_element_type=F32)


def _dot_tn(a, b):
    return lax.dot_general(a, b, (((0,), (0,)), ((), ())), preferred_element_type=F32)


def _rstd(v):
    return lax.rsqrt(jnp.mean(v * v, axis=-1, keepdims=True) + RMS_EPS)


def _rms_bwd(dy, v, rstd, gain):
    n = v * rstd
    dgain = jnp.sum(dy * n, axis=0, keepdims=True)
    dn = dy * gain
    dv = rstd * (dn - n * jnp.mean(dn * n, axis=-1, keepdims=True))
    return dv, dgain


def _lane_lo(shape):
    return (lax.broadcasted_iota(jnp.int32, shape, 1) % 128) < HEAD_DIM


def _attn_valid(chunk_index):
    qi = lax.broadcasted_iota(jnp.int32, (CHUNK, 2 * CHUNK), 0)
    kj = lax.broadcasted_iota(jnp.int32, (CHUNK, 2 * CHUNK), 1)
    first_key = jnp.where(chunk_index > 0, 0, CHUNK)
    return jnp.logical_and(jnp.logical_and(kj > qi, kj >= first_key), kj <= qi + CHUNK)


def _attn_head_operands(h, q_pair, kk, kk_r, vv, vv_r, lo_q, lo_kv):
    par, hk = h % 2, h // 4
    half_q = lo_q if par == 0 else jnp.logical_not(lo_q)
    half_kv = lo_kv if par == 0 else jnp.logical_not(lo_kv)
    k_use = kk if hk == par else kk_r
    v_use = vv if hk == par else vv_r
    qe = jnp.where(half_q, q_pair, 0.0).astype(BF16)
    return par, hk, half_q, half_kv, qe, k_use, v_use


def _attn_probs(qe, k_use_b, valid, sink):
    s = _dot_nt(qe, k_use_b) * ATTN_SCALE
    s = jnp.where(valid, s, MASK_VALUE)
    m = jnp.maximum(jnp.max(s, axis=-1, keepdims=True), sink)
    e = jnp.exp(s - m)
    e_sink = jnp.exp(sink - m)
    inv = 1.0 / (jnp.sum(e, axis=-1, keepdims=True) + e_sink)
    return e * inv, e_sink * inv


def _rot2(v):
    w = v.shape[1]
    even = (lax.broadcasted_iota(jnp.int32, v.shape, 1) % 2) == 0
    return jnp.where(even, -pltpu.roll(v, w - 1, 1), pltpu.roll(v, 1, 1))


def _tile4(v):
    return jnp.concatenate([v, v, v, v], axis=-1)


def _sigmoid(v):
    return 1.0 / (1.0 + jnp.exp(-v))


def _ret_constants():
    h = N_RET_HEADS
    log_gamma = jnp.log(1.0 - jnp.power(2.0, -5.0 - jnp.arange(h, dtype=F32)))
    idx = jnp.arange(CHUNK, dtype=F32)
    rel = idx[:, None] - idx[None, :]
    d_intra = jnp.where(rel[None] >= 0, jnp.exp(log_gamma[:, None, None] * jnp.maximum(rel, 0.0)[None]), 0.0)
    xi = jnp.exp(log_gamma[None, :] * (idx[:, None] + 1.0))
    zeta = jnp.exp(log_gamma[None, :] * (CHUNK - 1.0 - idx[:, None]))
    decay = jnp.exp(log_gamma * CHUNK)
    xi_full = jnp.repeat(xi, RET_HEAD_DIM, axis=1)
    zeta_full = jnp.repeat(zeta, RET_HEAD_DIM, axis=1)
    decay_full = jnp.broadcast_to(jnp.repeat(decay, RET_HEAD_DIM)[None, :], (8, RET_W))
    return d_intra.astype(F32), xi_full.astype(F32), zeta_full.astype(F32), decay_full.astype(F32)


def _rope_tables(s):
    pos = jnp.arange(s, dtype=F32)
    angle = 1.0 / jnp.power(10000.0, jnp.linspace(0.0, 1.0, RET_HEAD_DIM // 2, dtype=F32))
    angle = jnp.repeat(angle, 2)
    return jnp.sin(pos[:, None] * angle[None]), jnp.cos(pos[:, None] * angle[None])


def _in_proj(x, gain, w_in, hosted=None):
    s = x.shape[0]
    tm = min(BIG_TOKEN_TILE, s)

    def body(x_ref, g_ref, w_ref, h_ref, p_ref):
        xv = x_ref[...]
        h = (xv * _rstd(xv) * g_ref[...]).astype(BF16)
        h_ref[...] = h
        p_ref[...] = _dot(h, w_ref[...])

    return _hosted_call(
        body, name="in_proj", grid=(s // tm,),
        in_specs=[pl.BlockSpec((tm, D_MODEL), lambda i: (i, 0)), _resident((1, D_MODEL)), _resident((D_MODEL, IN_W))],
        out_specs=[pl.BlockSpec((tm, D_MODEL), lambda i: (i, 0)), pl.BlockSpec((tm, IN_W), lambda i: (i, 0))],
        out_shape=[jax.ShapeDtypeStruct((s, D_MODEL), BF16), jax.ShapeDtypeStruct((s, IN_W), F32)],
        scratch_shapes=[], args=(x, gain, w_in), hosted=hosted)


def _mixer_fwd(proj, sinks, sin, cos, consts, hosted=None):
    s = proj.shape[0]
    nc = s // CHUNK
    d_intra, xi_full, zeta_full, decay_full = consts

    def body(sk_ref, p_ref, pkv_ref, sin_ref, cos_ref, dm_ref, xi_ref, ze_ref, dc_ref, mix_ref, st_ref, state):
        n = pl.program_id(0)

        @pl.when(n == 0)
        def _():
            state[...] = jnp.zeros_like(state)

        kv_cur = p_ref[:, KV_A0:KV_A0 + 2 * KV_W]
        kv_prev = pkv_ref[...]
        kk = jnp.concatenate([kv_prev[:, :KV_W], kv_cur[:, :KV_W]], axis=0)
        vv = jnp.concatenate([kv_prev[:, KV_W:], kv_cur[:, KV_W:]], axis=0)
        kk_b, kk_rb = kk.astype(BF16), pltpu.roll(kk, HEAD_DIM, 1).astype(BF16)
        vv_r = pltpu.roll(vv, HEAD_DIM, 1)
        valid = _attn_valid(n)
        lo_q = _lane_lo((CHUNK, 128))
        lo_kv = _lane_lo((2 * CHUNK, 128))
        for pi in range(N_ATTN_HEADS // 2):
            q_pair = p_ref[:, Q_A0 + pi * 128:Q_A0 + (pi + 1) * 128]
            acc = jnp.zeros((CHUNK, 128), F32)
            for par in range(2):
                h = 2 * pi + par
                _, _, _, half_kv, qe, k_use, v_use = _attn_head_operands(h, q_pair, kk_b, kk_rb, vv, vv_r, lo_q, lo_kv)
                p, _ = _attn_probs(qe, k_use, valid, sk_ref[h])
                ve = jnp.where(half_kv, v_use, 0.0).astype(BF16)
                acc = acc + _dot(p.astype(BF16), ve)
            mix_ref[:, pi * 128:(pi + 1) * 128] = acc.astype(BF16)

        sin4, cos4 = _tile4(sin_ref[...]), _tile4(cos_ref[...])
        q_r = p_ref[:, Q_R0:Q_R0 + RET_W]
        k_r = p_ref[:, K_R0:K_R0 + RET_W] * RET_K_SCALE
        q_r = q_r * cos4 + _rot2(q_r) * sin4
        k_r = k_r * cos4 + _rot2(k_r) * sin4
        kz = k_r * ze_ref[...]
        for h in range(N_RET_HEADS):
            sl = slice(h * RET_HEAD_DIM, (h + 1) * RET_HEAD_DIM)
            qh, kh = q_r[:, sl].astype(BF16), k_r[:, sl].astype(BF16)
            vh = p_ref[:, V_R0 + h * RET_HEAD_DIM:V_R0 + (h + 1) * RET_HEAD_DIM].astype(BF16)
            st = state[h]
            st_ref[0, h] = st
            a = _dot_nt(qh, kh) * dm_ref[h]
            o = _dot(a.astype(BF16), vh) + _dot(qh, st.astype(BF16)) * xi_ref[:, sl]
            state[h] = dc_ref[0:1, sl] * st + _dot_tn(kz[:, sl].astype(BF16), vh)
            mu = jnp.mean(o, axis=-1, keepdims=True)
            oc = o - mu
            on = oc * lax.rsqrt(jnp.mean(oc * oc, axis=-1, keepdims=True) + GN_EPS)
            g = p_ref[:, G_R0 + h * RET_HEAD_DIM:G_R0 + (h + 1) * RET_HEAD_DIM]
            mix_ref[:, ATTN_W + h * RET_HEAD_DIM:ATTN_W + (h + 1) * RET_HEAD_DIM] = (g * _sigmoid(g) * on).astype(BF16)

    return _hosted_call(
        body, name="mixer_fwd", grid=(nc,),
        in_specs=[
            pl.BlockSpec(memory_space=pltpu.SMEM),
            pl.BlockSpec((CHUNK, IN_W), lambda n: (n, 0)),
            pl.BlockSpec((CHUNK, 2 * KV_W), lambda n: (jnp.maximum(n - 1, 0), KV_A0 // (2 * KV_W))),
            pl.BlockSpec((CHUNK, RET_HEAD_DIM), lambda n: (n, 0)),
            pl.BlockSpec((CHUNK, RET_HEAD_DIM), lambda n: (n, 0)),
            _resident((N_RET_HEADS, CHUNK, CHUNK)), _resident((CHUNK, RET_W)), _resident((CHUNK, RET_W)), _resident((8, RET_W)),
        ],
        out_specs=[
            pl.BlockSpec((CHUNK, D_MODEL), lambda n: (n, 0)),
            pl.BlockSpec((1, N_RET_HEADS, RET_HEAD_DIM, RET_HEAD_DIM), lambda n: (n, 0, 0, 0)),
        ],
        out_shape=[jax.ShapeDtypeStruct((s, D_MODEL), BF16),
                   jax.ShapeDtypeStruct((nc, N_RET_HEADS, RET_HEAD_DIM, RET_HEAD_DIM), F32)],
        scratch_shapes=[pltpu.VMEM((N_RET_HEADS, RET_HEAD_DIM, RET_HEAD_DIM), F32)],
        args=(sinks, proj, proj, sin, cos, d_intra, xi_full, zeta_full, decay_full), hosted=hosted)


def _out_up_proj(mix, x, w_out, g_post, g_pre, w_up):
    s = x.shape[0]
    tm = min(BIG_TOKEN_TILE, s)
    blk = UP_W // N_CHIPS

    def body(mix_ref, x_ref, wo_ref, g2_ref, g3_ref, wu_ref, mixed_ref, x1_ref, h2_ref, u0_ref):
        mixed = _dot(mix_ref[...], wo_ref[...])
        mixed_ref[...] = mixed
        x1 = x_ref[...] + mixed * _rstd(mixed) * g2_ref[...]
        x1_ref[...] = x1
        h2 = (x1 * _rstd(x1) * g3_ref[...]).astype(BF16)
        h2_ref[...] = h2
        for k in range(N_CHIPS):
            u0_ref[:, k * blk:(k + 1) * blk] = _dot(h2, wu_ref[k]).astype(BF16)

    tok = lambda w: pl.BlockSpec((tm, w), lambda i: (i, 0))
    return pl.pallas_call(
        body, name="out_up_proj", grid=(s // tm,),
        in_specs=[tok(D_MODEL), tok(D_MODEL), _resident((D_MODEL, D_MODEL)), _resident((1, D_MODEL)), _resident((1, D_MODEL)),
                  _resident((N_CHIPS, D_MODEL, blk))],
        out_specs=[tok(D_MODEL), tok(D_MODEL), tok(D_MODEL), tok(UP_W)],
        out_shape=[jax.ShapeDtypeStruct((s, D_MODEL), F32), jax.ShapeDtypeStruct((s, D_MODEL), F32),
                   jax.ShapeDtypeStruct((s, D_MODEL), BF16), jax.ShapeDtypeStruct((s, UP_W), BF16)],
        compiler_params=_params(("arbitrary",)),
    )(mix, x, w_out, g_post, g_pre, w_up)


def _ffn_tail(u0, x1, target, conv_w, conv_b, w_down, g_post):
    s = x1.shape[0]
    tm = TOKEN_TILE

    def body(u0_ref, x1_ref, t_ref, cw_ref, cb_ref, wd_ref, g_ref,
             y_ref, dy2_ref, dout_ref, du_ref, cacc_ref, gacc_ref, ubuf):
        i = pl.program_id(0)

        @pl.when(i == 0)
        def _():
            ubuf[0:8, :] = jnp.zeros((8, UP_W), F32)
            cacc_ref[...] = jnp.zeros_like(cacc_ref)
            gacc_ref[...] = jnp.zeros_like(gacc_ref)

        ubuf[8:8 + tm, :] = u0_ref[...].astype(F32)
        u_c = ubuf[8:8 + tm, :]
        u_1 = ubuf[7:7 + tm, :]
        u_2 = ubuf[6:6 + tm, :]
        u = cw_ref[0:1, :] * u_2 + cw_ref[1:2, :] * u_1 + cw_ref[2:3, :] * u_c + cb_ref[...]
        gate, val = u[:, :D_FF], u[:, D_FF:]
        g2 = gate * gate
        th = jnp.tanh(GELU_C * gate * (1.0 + GELU_A * g2))
        gelu = 0.5 * gate * (1.0 + th)
        dgelu = 0.5 * (1.0 + th) + 0.5 * gate * (1.0 - th * th) * GELU_C * (1.0 + 3.0 * GELU_A * g2)
        y = (gelu * val).astype(BF16)
        y_ref[...] = y
        y2 = _dot(y, wd_ref[...])
        r4 = _rstd(y2)
        gain = g_ref[...]
        out = x1_ref[...] + y2 * r4 * gain
        diff = out - t_ref[...]
        dout = diff * (1.0 / D_MODEL)
        dout_ref[...] = dout
        dy2, dgain = _rms_bwd(dout, y2, r4, gain)
        dy2_b = dy2.astype(BF16)
        dy2_ref[...] = dy2_b
        gacc_ref[0:1, :] += dgain
        gacc_ref[1:2, :] += 0.5 * jnp.sum(diff * dout, axis=0, keepdims=True)
        dy = _dot_nt(dy2_b, wd_ref[...])
        du = jnp.concatenate([dy * val * dgelu, dy * gelu], axis=-1)
        du_ref[...] = du.astype(BF16)
        cacc_ref[0:1, :] += jnp.sum(du * u_2, axis=0, keepdims=True)
        cacc_ref[1:2, :] += jnp.sum(du * u_1, axis=0, keepdims=True)
        cacc_ref[2:3, :] += jnp.sum(du * u_c, axis=0, keepdims=True)
        cacc_ref[3:4, :] += jnp.sum(du, axis=0, keepdims=True)
        ubuf[0:8, :] = ubuf[tm:tm + 8, :]

    tok = lambda w: pl.BlockSpec((tm, w), lambda i: (i, 0))
    return pl.pallas_call(
        body, name="ffn_tail", grid=(s // tm,),
        in_specs=[tok(UP_W), tok(D_MODEL), tok(D_MODEL), _resident((CONV_WIDTH, UP_W)), _resident((1, UP_W)),
                  _resident((D_FF, D_MODEL)), _resident((1, D_MODEL))],
        out_specs=[tok(D_FF), tok(D_MODEL), tok(D_MODEL), tok(UP_W),
                   pl.BlockSpec((8, UP_W), lambda i: (0, 0)), pl.BlockSpec((8, D_MODEL), lambda i: (0, 0))],
        out_shape=[jax.ShapeDtypeStruct((s, D_FF), BF16), jax.ShapeDtypeStruct((s, D_MODEL), BF16),
                   jax.ShapeDtypeStruct((s, D_MODEL), F32), jax.ShapeDtypeStruct((s, UP_W), BF16),
                   jax.ShapeDtypeStruct((8, UP_W), F32), jax.ShapeDtypeStruct((8, D_MODEL), F32)],
        scratch_shapes=[pltpu.VMEM((tm + 8, UP_W), F32)],
        compiler_params=_params(("arbitrary",)),
    )(u0, x1, target, conv_w, conv_b, w_down, g_post)


def _ffn_head_bwd(du, conv_w, w_up, x1, g_pre, dout, mixed, g_post, w_out):
    s = x1.shape[0]
    tm = TOKEN_TILE
    nt = s // tm
    blk = UP_W // N_CHIPS

    def body(du_ref, halo_ref, cw_ref, wu_ref, x1_ref, g3_ref, dout_ref, mixed_ref, g2_ref, wo_ref,
             du0_ref, dx1_ref, dmixed_ref, dmix_ref, gacc_ref, dbuf):
        i = pl.program_id(0)

        @pl.when(i == 0)
        def _():
            gacc_ref[...] = jnp.zeros_like(gacc_ref)

        dbuf[0:tm, :] = du_ref[...].astype(F32)
        dbuf[tm:tm + 16, :] = jnp.where(i < nt - 1, halo_ref[...].astype(F32), 0.0)
        du0 = cw_ref[2:3, :] * dbuf[0:tm, :] + cw_ref[1:2, :] * dbuf[1:1 + tm, :] + cw_ref[0:1, :] * dbuf[2:2 + tm, :]
        du0_b = du0.astype(BF16)
        du0_ref[...] = du0_b
        dh2 = _dot_nt(du0_b[:, :blk], wu_ref[0])
        for k in range(1, N_CHIPS):
            dh2 = dh2 + _dot_nt(du0_b[:, k * blk:(k + 1) * blk], wu_ref[k])
        x1 = x1_ref[...]
        d3, dg3 = _rms_bwd(dh2, x1, _rstd(x1), g3_ref[...])
        dx1 = dout_ref[...] + d3
        dx1_ref[...] = dx1
        mixed = mixed_ref[...]
        dmixed, dg2 = _rms_bwd(dx1, mixed, _rstd(mixed), g2_ref[...])
        dmixed_b = dmixed.astype(BF16)
        dmixed_ref[...] = dmixed_b
        dmix_ref[...] = _dot_nt(dmixed_b, wo_ref[...]).astype(BF16)
        gacc_ref[0:1, :] += dg3
        gacc_ref[1:2, :] += dg2

    tok = lambda w: pl.BlockSpec((tm, w), lambda i: (i, 0))
    halo = pl.BlockSpec((16, UP_W), lambda i: (jnp.minimum(i + 1, nt - 1) * (tm // 16), 0))
    return pl.pallas_call(
        body, name="ffn_head_bwd", grid=(nt,),
        in_specs=[tok(UP_W), halo, _resident((CONV_WIDTH, UP_W)), _resident((N_CHIPS, D_MODEL, blk)), tok(D_MODEL),
                  _resident((1, D_MODEL)), tok(D_MODEL), tok(D_MODEL), _resident((1, D_MODEL)), _resident((D_MODEL, D_MODEL))],
        out_specs=[tok(UP_W), tok(D_MODEL), tok(D_MODEL), tok(D_MODEL), pl.BlockSpec((8, D_MODEL), lambda i: (0, 0))],
        out_shape=[jax.ShapeDtypeStruct((s, UP_W), BF16), jax.ShapeDtypeStruct((s, D_MODEL), F32),
                   jax.ShapeDtypeStruct((s, D_MODEL), BF16), jax.ShapeDtypeStruct((s, D_MODEL), BF16),
                   jax.ShapeDtypeStruct((8, D_MODEL), F32)],
        scratch_shapes=[pltpu.VMEM((tm + 16, UP_W), F32)],
        compiler_params=_params(("arbitrary",)),
    )(du, du, conv_w, w_up, x1, g_pre, dout, mixed, g_post, w_out)


def _mixer_bwd(proj, dmix, states, sinks, sin, cos, consts, hosted=None):
    s = proj.shape[0]
    nc = s // CHUNK
    d_intra, xi_full, zeta_full, decay_full = consts

    def body(sk_ref, p_ref, pkv_ref, dmix_ref, st_ref, sin_ref, cos_ref, dm_ref, xi_ref, ze_ref, dc_ref,
             dp_ref, dsk_ref, gstate, ckv, dsk_acc):
        i = pl.program_id(0)
        n = nc - 1 - i

        @pl.when(i == 0)
        def _():
            gstate[...] = jnp.zeros_like(gstate)
            ckv[...] = jnp.zeros_like(ckv)
            dsk_acc[...] = jnp.zeros_like(dsk_acc)

        kv_cur = p_ref[:, KV_A0:KV_A0 + 2 * KV_W]
        kv_prev = pkv_ref[...]
        kk = jnp.concatenate([kv_prev[:, :KV_W], kv_cur[:, :KV_W]], axis=0)
        vv = jnp.concatenate([kv_prev[:, KV_W:], kv_cur[:, KV_W:]], axis=0)
        kk_b, kk_rb = kk.astype(BF16), pltpu.roll(kk, HEAD_DIM, 1).astype(BF16)
        vv_b, vv_rb = vv.astype(BF16), pltpu.roll(vv, HEAD_DIM, 1).astype(BF16)
        valid = _attn_valid(n)
        lo_q = _lane_lo((CHUNK, 128))
        lo_kv = _lane_lo((2 * CHUNK, 128))
        lane = lax.broadcasted_iota(jnp.int32, (CHUNK, 128), 1)
        dkk = jnp.zeros((2 * CHUNK, KV_W), F32)
        dvv = jnp.zeros((2 * CHUNK, KV_W), F32)
        dsk = jnp.zeros((CHUNK, 128), F32)
        for pi in range(N_ATTN_HEADS // 2):
            q_pair = p_ref[:, Q_A0 + pi * 128:Q_A0 + (pi + 1) * 128]
            do_pair = dmix_ref[:, pi * 128:(pi + 1) * 128].astype(F32)
            dq_pair = jnp.zeros((CHUNK, 128), F32)
            for par in range(2):
                h = 2 * pi + par
                _, hk, half_q, half_kv, qe, k_use, v_use = _attn_head_operands(h, q_pair, kk_b, kk_rb, vv_b, vv_rb, lo_q, lo_kv)
                p, p_sink = _attn_probs(qe, k_use, valid, sk_ref[h])
                doe = jnp.where(half_q, do_pair, 0.0).astype(BF16)
                dpr = _dot_nt(doe, v_use)
                delta = jnp.sum(p * dpr, axis=-1, keepdims=True)
                ds_b = (p * (dpr - delta) * ATTN_SCALE).astype(BF16)
                dsk = dsk + jnp.where(lane == h, -p_sink * delta, 0.0)
                dq_pair = dq_pair + _dot(ds_b, jnp.where(half_kv, k_use, 0.0).astype(BF16))
                dk_h = _dot_tn(ds_b, qe)
                dv_h = _dot_tn(p.astype(BF16), doe)
                if hk != par:
                    dk_h, dv_h = pltpu.roll(dk_h, HEAD_DIM, 1), pltpu.roll(dv_h, HEAD_DIM, 1)
                dkk = dkk + dk_h
                dvv = dvv + dv_h
            dp_ref[:, Q_A0 + pi * 128:Q_A0 + (pi + 1) * 128] = dq_pair.astype(BF16)
        dp_ref[:, KV_A0:KV_A0 + KV_W] = (dkk[CHUNK:] + ckv[:, :KV_W]).astype(BF16)
        dp_ref[:, KV_A0 + KV_W:KV_A0 + 2 * KV_W] = (dvv[CHUNK:] + ckv[:, KV_W:]).astype(BF16)
        ckv[:, :KV_W] = dkk[:CHUNK]
        ckv[:, KV_W:] = dvv[:CHUNK]
        dsk_acc[...] += dsk

        @pl.when(i == nc - 1)
        def _():
            dsk_ref[...] = jnp.sum(dsk_acc[...], axis=0, keepdims=True)

        sin4, cos4 = _tile4(sin_ref[...]), _tile4(cos_ref[...])
        q_r = p_ref[:, Q_R0:Q_R0 + RET_W]
        k_r = p_ref[:, K_R0:K_R0 + RET_W] * RET_K_SCALE
        q_r = q_r * cos4 + _rot2(q_r) * sin4
        k_r = k_r * cos4 + _rot2(k_r) * sin4
        kz = k_r * ze_ref[...]
        dq_parts, dk_parts = [], []
        for h in range(N_RET_HEADS):
            sl = slice(h * RET_HEAD_DIM, (h + 1) * RET_HEAD_DIM)
            qh, kh = q_r[:, sl].astype(BF16), k_r[:, sl].astype(BF16)
            vh = p_ref[:, V_R0 + h * RET_HEAD_DIM:V_R0 + (h + 1) * RET_HEAD_DIM].astype(BF16)
            st_b = st_ref[0, h].astype(BF16)
            gs = gstate[h]
            gs_b = gs.astype(BF16)
            xi_h = xi_ref[:, sl]
            dm = dm_ref[h]
            a_b = (_dot_nt(qh, kh) * dm).astype(BF16)
            o = _dot(a_b, vh) + _dot(qh, st_b) * xi_h
            mu = jnp.mean(o, axis=-1, keepdims=True)
            oc = o - mu
            rs = lax.rsqrt(jnp.mean(oc * oc, axis=-1, keepdims=True) + GN_EPS)
            on = oc * rs
            g = p_ref[:, G_R0 + h * RET_HEAD_DIM:G_R0 + (h + 1) * RET_HEAD_DIM]
            sg = _sigmoid(g)
            dr = dmix_ref[:, ATTN_W + h * RET_HEAD_DIM:ATTN_W + (h + 1) * RET_HEAD_DIM].astype(F32)
            dp_ref[:, G_R0 + h * RET_HEAD_DIM:G_R0 + (h + 1) * RET_HEAD_DIM] = (
                dr * on * (sg * (1.0 + g * (1.0 - sg)))).astype(BF16)
            don = dr * g * sg
            do = rs * (don - jnp.mean(don, axis=-1, keepdims=True) - on * jnp.mean(don * on, axis=-1, keepdims=True))
            do_b = do.astype(BF16)
            dox_b = (do * xi_h).astype(BF16)
            da_b = (_dot_nt(do_b, vh) * dm).astype(BF16)
            dq_parts.append(_dot(da_b, kh) + _dot_nt(dox_b, st_b))
            dk_parts.append(_dot_tn(da_b, qh) + ze_ref[:, sl] * _dot_nt(vh, gs_b))
            dv = _dot_tn(a_b, do_b) + _dot(kz[:, sl].astype(BF16), gs_b)
            dp_ref[:, V_R0 + h * RET_HEAD_DIM:V_R0 + (h + 1) * RET_HEAD_DIM] = dv.astype(BF16)
            gstate[h] = dc_ref[0:1, sl] * gs + _dot_tn(qh, dox_b)
        dq = jnp.concatenate(dq_parts, axis=-1)
        dk = jnp.concatenate(dk_parts, axis=-1)
        dp_ref[:, Q_R0:Q_R0 + RET_W] = (dq * cos4 - _rot2(dq * sin4)).astype(BF16)
        dp_ref[:, K_R0:K_R0 + RET_W] = (RET_K_SCALE * (dk * cos4 - _rot2(dk * sin4))).astype(BF16)

    rev = lambda i: nc - 1 - i
    return _hosted_call(
        body, name="mixer_bwd", grid=(nc,),
        in_specs=[
            pl.BlockSpec(memory_space=pltpu.SMEM),
            pl.BlockSpec((CHUNK, IN_W), lambda i: (rev(i), 0)),
            pl.BlockSpec((CHUNK, 2 * KV_W), lambda i: (jnp.maximum(rev(i) - 1, 0), KV_A0 // (2 * KV_W))),
            pl.BlockSpec((CHUNK, D_MODEL), lambda i: (rev(i), 0)),
            pl.BlockSpec((1, N_RET_HEADS, RET_HEAD_DIM, RET_HEAD_DIM), lambda i: (rev(i), 0, 0, 0)),
            pl.BlockSpec((CHUNK, RET_HEAD_DIM), lambda i: (rev(i), 0)),
            pl.BlockSpec((CHUNK, RET_HEAD_DIM), lambda i: (rev(i), 0)),
            _resident((N_RET_HEADS, CHUNK, CHUNK)), _resident((CHUNK, RET_W)), _resident((CHUNK, RET_W)), _resident((8, RET_W)),
        ],
        out_specs=[pl.BlockSpec((CHUNK, IN_W), lambda i: (rev(i), 0)), pl.BlockSpec((1, 128), lambda i: (0, 0))],
        out_shape=[jax.ShapeDtypeStruct((s, IN_W), BF16), jax.ShapeDtypeStruct((1, 128), F32)],
        scratch_shapes=[pltpu.VMEM((N_RET_HEADS, RET_HEAD_DIM, RET_HEAD_DIM), F32), pltpu.VMEM((CHUNK, 2 * KV_W), F32),
                        pltpu.VMEM((CHUNK, 128), F32)],
        args=(sinks, proj, proj, dmix, states, sin, cos, d_intra, xi_full, zeta_full, decay_full), hosted=hosted)


def _in_proj_bwd(dproj, w_in, x, gain, dx1, hosted=None):
    s = x.shape[0]
    tm = min(BIG_TOKEN_TILE, s)

    def body(dp_ref, w_ref, x_ref, g_ref, dx1_ref, dx_ref, gacc_ref):
        @pl.when(pl.program_id(0) == 0)
        def _():
            gacc_ref[...] = jnp.zeros_like(gacc_ref)

        dh = _dot_nt(dp_ref[...], w_ref[...])
        xv = x_ref[...]
        d1, dg = _rms_bwd(dh, xv, _rstd(xv), g_ref[...])
        dx_ref[...] = dx1_ref[...] + d1
        gacc_ref[0:1, :] += dg

    tok = lambda w: pl.BlockSpec((tm, w), lambda i: (i, 0))
    return _hosted_call(
        body, name="in_proj_bwd", grid=(s // tm,),
        in_specs=[tok(IN_W), _resident((D_MODEL, IN_W)), tok(D_MODEL), _resident((1, D_MODEL)), tok(D_MODEL)],
        out_specs=[tok(D_MODEL), pl.BlockSpec((8, D_MODEL), lambda i: (0, 0))],
        out_shape=[jax.ShapeDtypeStruct((s, D_MODEL), F32), jax.ShapeDtypeStruct((8, D_MODEL), F32)],
        scratch_shapes=[], args=(dproj, w_in, x, gain, dx1), hosted=hosted)


def _weight_grad(a, b, tn, name, by_block=False):
    s, m = a.shape
    n = b.shape[1]
    tk = min(WEIGHT_GRAD_TOKENS, s)

    def body(a_ref, b_ref, o_ref):
        @pl.when(pl.program_id(1) == 0)
        def _():
            o_ref[...] = jnp.zeros_like(o_ref)

        o_ref[...] += _dot_tn(a_ref[...], b_ref[...])

    if by_block:
        out_spec = pl.BlockSpec((None, m, tn), lambda j, k: (j, 0, 0))
        out_shape = jax.ShapeDtypeStruct((n // tn, m, tn), F32)
    else:
        out_spec = pl.BlockSpec((m, tn), lambda j, k: (0, j))
        out_shape = jax.ShapeDtypeStruct((m, n), F32)
    return pl.pallas_call(
        body, name=name, grid=(n // tn, s // tk),
        in_specs=[pl.BlockSpec((tk, m), lambda j, k: (k, 0)), pl.BlockSpec((tk, tn), lambda j, k: (k, j))],
        out_specs=out_spec, out_shape=out_shape,
        compiler_params=_params(("arbitrary", "arbitrary")),
    )(a, b)


def _place():
    return lax.axis_index("x"), lax.axis_index("y"), lax.axis_index("c")


def _remote(src, dst, send_sems, recv_sems, k, to):
    return pltpu.make_async_remote_copy(src_ref=src, dst_ref=dst, send_sem=send_sems.at[k], recv_sem=recv_sems.at[k],
                                        device_id=to, device_id_type=MESH)


def _gather_level1_copies(w_refs, out_refs, send_sems, recv_sems, local_sems):
    x, y, c = _place()
    mine_at = 2 * x + y
    peers = [(x, y, 1 - c), (1 - x, y, c), (x, 1 - y, c), (1 - x, 1 - y, c)]
    local, sends, recvs = [], [], []
    for i, (w, out) in enumerate(zip(w_refs, out_refs)):
        half = w.shape[0] // 2
        src = w.at[pl.ds(pl.multiple_of(c * half, 16), half), :]
        mine = out.at[mine_at, c]
        local.append(pltpu.make_async_copy(src, mine, local_sems.at[i]))
        for k, p in enumerate(peers):
            sends.append(_remote(src, mine, send_sems, recv_sems, 4 * i + k, p))
            lands = out.at[mine_at, 1 - c] if k == 0 else out.at[2 * p[0] + p[1], c]
            recvs.append(_remote(src, lands, send_sems, recv_sems, 4 * i + k, p))
    return local, sends, recvs


def _gather_level1_start(w_refs, out_refs, send_sems, recv_sems, local_sems):
    local, sends, _ = _gather_level1_copies(w_refs, out_refs, send_sems, recv_sems, local_sems)
    for cp in local + sends:
        cp.start()


def _gather_level1_finish(w_refs, out_refs, send_sems, recv_sems, local_sems):
    local, sends, recvs = _gather_level1_copies(w_refs, out_refs, send_sems, recv_sems, local_sems)
    for cp in recvs:
        cp.wait_recv()
    for cp in sends:
        cp.wait_send()
    for cp in local:
        cp.wait()


def _gather_level2_copies(in_refs, out_refs, send_sems, recv_sems, local_sems):
    x, y, c = _place()
    chips = [(1 - x, y), (x, 1 - y), (1 - x, 1 - y)]
    sends, recvs = [], []
    for i, (src, out) in enumerate(zip(in_refs, out_refs)):
        for j, (px, py) in enumerate(chips):
            sends.append(_remote(src.at[2 * px + py, c], out.at[2 * px + py, c], send_sems, recv_sems, 3 * i + j, (x, y, 1 - c)))
            recvs.append(_remote(src.at[2 * px + py, c], out.at[2 * px + py, 1 - c], send_sems, recv_sems, 3 * i + j,
                                 (x, y, 1 - c)))
    return sends, recvs


def _gather_level2_start(in_refs, out_refs, send_sems, recv_sems, local_sems):
    for cp in _gather_level2_copies(in_refs, out_refs, send_sems, recv_sems, local_sems)[0]:
        cp.start()


def _gather_level2_finish(in_refs, out_refs, send_sems, recv_sems, local_sems):
    sends, recvs = _gather_level2_copies(in_refs, out_refs, send_sems, recv_sems, local_sems)
    for cp in recvs:
        cp.wait_recv()
    for cp in sends:
        cp.wait_send()


def _gathered_shape(w):
    r, cols = w.shape
    return jax.ShapeDtypeStruct((N_CHIPS, 2, r // 2, cols), w.dtype)


def _hosted_gather_level1(shards):
    n = len(shards)
    return _Hosted(shards, [_gathered_shape(w) for w in shards], {}, 4 * n, n, _gather_level1_start, _gather_level1_finish)


def _hosted_gather_level2(gathered):
    n = len(gathered)
    return _Hosted(gathered, [jax.ShapeDtypeStruct(g.shape, g.dtype) for g in gathered], {i: i for i in range(n)}, 3 * n, 0,
                   _gather_level2_start, _gather_level2_finish)


def _gather_now(w, name):
    def body(w_ref, out_ref, send1, recv1, local1, send2, recv2):
        _gather_level1_start([w_ref], [out_ref], send1, recv1, local1)
        _gather_level1_finish([w_ref], [out_ref], send1, recv1, local1)
        _gather_level2_start([out_ref], [out_ref], send2, recv2, None)
        _gather_level2_finish([out_ref], [out_ref], send2, recv2, None)

    return pl.pallas_call(
        body, name=name, out_shape=_gathered_shape(w),
        in_specs=[pl.BlockSpec(memory_space=pl.ANY)], out_specs=pl.BlockSpec(memory_space=pl.ANY),
        scratch_shapes=[pltpu.SemaphoreType.DMA((4,)), pltpu.SemaphoreType.DMA((4,)), pltpu.SemaphoreType.DMA((1,)),
                        pltpu.SemaphoreType.DMA((3,)), pltpu.SemaphoreType.DMA((3,))],
    )(w)


def _scatter_copies(g_refs, land_refs, send_sems, recv_sems, local_sems):
    x, y, c = _place()
    copies = []
    for i, (g, land) in enumerate(zip(g_refs, land_refs)):
        for k, (px, py, pc) in enumerate(_relations(x, y, c)):
            copies.append(_remote(g.at[2 * px + py, pc], land.at[k], send_sems, recv_sems, 7 * i + k, (px, py, pc)))
    return copies


def _scatter_start(g_refs, land_refs, send_sems, recv_sems, local_sems):
    for cp in _scatter_copies(g_refs, land_refs, send_sems, recv_sems, local_sems):
        cp.start()


def _scatter_finish(g_refs, land_refs, send_sems, recv_sems, local_sems):
    for cp in _scatter_copies(g_refs, land_refs, send_sems, recv_sems, local_sems):
        cp.wait()


def _hosted_scatter(grads):
    lands = [jax.ShapeDtypeStruct((N_DEV - 1,) + g.shape[2:], g.dtype) for g in grads]
    return _Hosted(grads, lands, {}, 7 * len(grads), 0, _scatter_start, _scatter_finish)


def _relations(x, y, c):
    rel = []
    for fx in (0, 1):
        for fy in (0, 1):
            for fc in (0, 1):
                if fx or fy or fc:
                    rel.append(((1 - x) if fx else x, (1 - y) if fy else y, (1 - c) if fc else c))
    return rel


def _gather_small(v, name):
    r, cols = v.shape

    def body(v_ref, out_ref, send_sems, recv_sems):
        x, y, c = _place()
        peers = _relations(x, y, c)

        def slot(p):
            return out_ref.at[4 * p[0] + 2 * p[1] + p[2]]

        out_ref[4 * x + 2 * y + c] = v_ref[...]
        sends = [pltpu.make_async_remote_copy(
            src_ref=v_ref, dst_ref=slot((x, y, c)), send_sem=send_sems.at[k], recv_sem=recv_sems.at[k],
            device_id=p, device_id_type=MESH) for k, p in enumerate(peers)]
        for cp in sends:
            cp.start()
        for k, p in enumerate(peers):
            pltpu.make_async_remote_copy(
                src_ref=v_ref, dst_ref=slot(p), send_sem=send_sems.at[k], recv_sem=recv_sems.at[k],
                device_id=p, device_id_type=MESH).wait_recv()
        for cp in sends:
            cp.wait_send()

    return pl.pallas_call(
        body, name=name,
        out_shape=jax.ShapeDtypeStruct((N_DEV, r, cols), v.dtype),
        in_specs=[pl.BlockSpec(memory_space=pltpu.VMEM)],
        out_specs=pl.BlockSpec(memory_space=pltpu.VMEM),
        scratch_shapes=[pltpu.SemaphoreType.DMA((7,)), pltpu.SemaphoreType.DMA((7,))],
    )(v)


def _join_halves(halves):
    n = len(halves)

    def body(*refs):
        t_refs, out_refs = refs[:n], refs[n:2 * n]
        send_sems, recv_sems, local_sems = refs[2 * n:]
        x, y, c = _place()
        local = [pltpu.make_async_copy(t, out.at[c], local_sems.at[i]) for i, (t, out) in enumerate(zip(t_refs, out_refs))]
        sends = [_remote(t, out.at[c], send_sems, recv_sems, i, (x, y, 1 - c)) for i, (t, out) in enumerate(zip(t_refs, out_refs))]
        recvs = [_remote(t, out.at[1 - c], send_sems, recv_sems, i, (x, y, 1 - c))
                 for i, (t, out) in enumerate(zip(t_refs, out_refs))]
        for cp in local + sends:
            cp.start()
        for cp in recvs:
            cp.wait_recv()
        for cp in sends:
            cp.wait_send()
        for cp in local:
            cp.wait()

    hbm = pl.BlockSpec(memory_space=pl.ANY)
    return pl.pallas_call(
        body, name="grad_join_halves",
        out_shape=[jax.ShapeDtypeStruct((2,) + t.shape, t.dtype) for t in halves],
        in_specs=[hbm] * n, out_specs=[hbm] * n,
        scratch_shapes=[pltpu.SemaphoreType.DMA((n,)), pltpu.SemaphoreType.DMA((n,)), pltpu.SemaphoreType.DMA((n,))],
    )(*halves)


def _row_tile(rows, row_bytes, limit=1 << 20):
    best = 8
    for t in range(8, rows + 1, 8):
        if rows % t == 0 and t * row_bytes <= limit:
            best = t
    return best


def _sum_pieces(g, land, place, name):
    _, _, rh, cols = g.shape
    tr = _row_tile(rh, (N_DEV - 1) * cols * 4, 4 << 20)

    def body(p_ref, g_ref, l_ref, out_ref):
        acc = g_ref[...]
        for k in range(N_DEV - 1):
            acc = acc + l_ref[k]
        out_ref[...] = acc

    return pl.pallas_call(
        body, name=name,
        grid_spec=pltpu.PrefetchScalarGridSpec(
            num_scalar_prefetch=1, grid=(rh // tr,),
            in_specs=[pl.BlockSpec((None, None, tr, cols), lambda r, p: (p[0], p[1], r, 0)),
                      pl.BlockSpec((N_DEV - 1, tr, cols), lambda r, p: (0, r, 0))],
            out_specs=pl.BlockSpec((tr, cols), lambda r, p: (r, 0))),
        out_shape=jax.ShapeDtypeStruct((rh, cols), g.dtype),
        compiler_params=_params(("arbitrary",)),
    )(place, g, land)


def _adamw_math(w, g, m, v):
    m = ADAM_B1 * m + (1.0 - ADAM_B1) * g
    v = ADAM_B2 * v + (1.0 - ADAM_B2) * (g * g)
    m_hat = m / (1.0 - ADAM_B1 ** ADAM_STEP)
    v_hat = v / (1.0 - ADAM_B2 ** ADAM_STEP)
    delta = -ADAM_LR * (m_hat / (jnp.sqrt(v_hat) + ADAM_EPS) + ADAM_WD * w)
    return delta, m, v


def _adamw(w, g, m, v, name):
    r, cols = w.shape
    tr = _row_tile(r, cols * 4)

    def body(w_ref, g_ref, m_ref, v_ref, d_ref, nm_ref, nv_ref):
        d_ref[...], nm_ref[...], nv_ref[...] = _adamw_math(w_ref[...], g_ref[...], m_ref[...], v_ref[...])

    blk = pl.BlockSpec((tr, cols), lambda i: (i, 0))
    shape = jax.ShapeDtypeStruct((r, cols), F32)
    return pl.pallas_call(
        body, name=name, grid=(r // tr,), in_specs=[blk] * 4, out_specs=[blk] * 3, out_shape=[shape] * 3,
        compiler_params=_params(("arbitrary",)),
    )(w, g, m, v)


def _sum_devices(gathered):
    _, r, cols = gathered.shape

    def body(a_ref, g_ref):
        g = a_ref[0]
        for k in range(1, N_DEV):
            g = g + a_ref[k]
        g_ref[...] = g

    return pl.pallas_call(body, name="sum_small_grads", out_shape=jax.ShapeDtypeStruct((r, cols), F32))(gathered)


def _pack_conv(cw):
    flat = cw.reshape(-1)
    return jnp.pad(flat, (0, ROWS_CONV * D_MODEL - flat.shape[0])).reshape(ROWS_CONV, D_MODEL)


def _unpack_conv(rows):
    return rows.reshape(-1)[:CONV_WIDTH * UP_W // N_CHIPS].reshape(CONV_WIDTH, UP_W // N_CHIPS)


def _columns_to_shards(w):
    r, n = w.shape
    return jnp.transpose(w.reshape(r, N_CHIPS, n // N_CHIPS), (1, 0, 2))


def _shards_to_columns(w):
    _, r, n = w.shape
    return jnp.transpose(w, (1, 0, 2)).reshape(r, N_CHIPS * n)


def _pack_small(g_mix_pre, g_mix_post, g_ffn_pre, g_ffn_post, sinks, conv_b, loss):
    pad_row = lambda v: jnp.pad(v.reshape(1, -1), ((0, 0), (0, D_MODEL - v.size)))
    cb = jnp.pad(conv_b.reshape(-1), (0, 6 * D_MODEL - UP_W)).reshape(6, D_MODEL)
    zeros2 = jnp.zeros((2, D_MODEL), F32)
    return jnp.concatenate([g_mix_pre.reshape(1, -1), g_mix_post.reshape(1, -1), g_ffn_pre.reshape(1, -1),
                            g_ffn_post.reshape(1, -1), pad_row(sinks), pad_row(loss), zeros2, cb, zeros2], axis=0)


def _unpack_small(p):
    return dict(mix_pre_norm=p[0:1], mix_post_norm=p[1:2], ffn_pre_norm=p[2:3], ffn_post_norm=p[3:4],
                attn_sinks=p[4:5, :N_ATTN_HEADS], loss=p[5, 0], conv_b=p[8:14].reshape(1, -1)[:, :UP_W],
                conv_w=_unpack_conv(p[SMALL_ROWS:SMALL_ROWS + ROWS_CONV]))


def _local_step(x, target, g_mix_pre, w_in, sinks, w_out, g_mix_post, g_ffn_pre, w_up, conv_w, conv_b, w_down, g_ffn_post,
                distributed=True):
    s = x.shape[0]
    consts = _ret_constants()
    sin, cos = _rope_tables(s)

    by_half = lambda g, rows: g.reshape(N_CHIPS, 2, rows // (2 * N_CHIPS), g.shape[-1])

    if distributed:
        (h1, proj), level1 = _in_proj(x, g_mix_pre, w_in, _hosted_gather_level1([w_out, w_up, w_down]))
        (mix, states), (w_out, w_up, w_down) = _mixer_fwd(proj, sinks, sin, cos, consts, _hosted_gather_level2(level1))
        w_out, w_down = w_out.reshape(D_MODEL, D_MODEL), w_down.reshape(D_FF, D_MODEL)
        w_up = w_up.reshape(N_CHIPS, D_MODEL, UP_W // N_CHIPS)
    else:
        (h1, proj), _ = _in_proj(x, g_mix_pre, w_in)
        (mix, states), _ = _mixer_fwd(proj, sinks, sin, cos, consts)
    mixed, x1, h2, u0 = _out_up_proj(mix, x, w_out, g_mix_post, g_ffn_pre, w_up)
    y, dy2, dout, du, conv_acc, tail_acc = _ffn_tail(u0, x1, target, conv_w, conv_b, w_down, g_ffn_post)
    du0, dx1, dmixed, dmix, head_acc = _ffn_head_bwd(du, conv_w, w_up, x1, g_ffn_pre, dout, mixed, g_mix_post, w_out)

    d_w_down = _weight_grad(y, dy2, 512, "grad_w_down")
    d_w_up = _weight_grad(h2, du0, UP_W // N_CHIPS, "grad_w_up", by_block=True)
    d_w_out = _weight_grad(mix, dmixed, D_MODEL, "grad_w_out")
    early = [by_half(d_w_down, D_FF), by_half(d_w_up, N_CHIPS * D_MODEL), by_half(d_w_out, D_MODEL)]
    (dproj, dsinks), early_lands = _mixer_bwd(proj, dmix, states, sinks, sin, cos, consts,
                                              _hosted_scatter(early) if distributed else None)
    d_w_in = _columns_to_shards(_weight_grad(h1, dproj, IN_W // 2, "grad_w_in"))
    late = [by_half(d_w_in, N_CHIPS * D_MODEL)]
    (grad_x, in_acc), late_lands = _in_proj_bwd(dproj, w_in, x, g_mix_pre, dx1, _hosted_scatter(late) if distributed else None)

    small = _pack_small(in_acc[0], head_acc[1], head_acc[0], tail_acc[0], dsinks[0, :N_ATTN_HEADS], conv_acc[3],
                        jnp.sum(tail_acc[1]))
    d_conv = jnp.pad(conv_acc[0:CONV_WIDTH].reshape(-1), (0, CONV_FULL_ROWS * D_MODEL - CONV_WIDTH * UP_W))
    small = jnp.concatenate([small, d_conv.reshape(CONV_FULL_ROWS, D_MODEL)], axis=0)
    grads = dict(w_down=early[0], w_up=early[1], w_out=early[2], w_in=late[0])
    lands = dict(zip(["w_down", "w_up", "w_out", "w_in"], early_lands + late_lands))
    return grad_x, grads, lands, small


def kernel(x, mix_pre_norm, w_in, attn_sinks, w_out, mix_post_norm, ffn_pre_norm, w_up, conv_w, conv_b, w_down, ffn_post_norm, loss_target, m_mix_pre_norm, m_w_in, m_attn_sinks, m_w_out, m_mix_post_norm, m_ffn_pre_norm, m_w_up, m_conv_w, m_conv_b, m_w_down, m_ffn_post_norm, v_mix_pre_norm, v_w_in, v_attn_sinks, v_w_out, v_mix_post_norm, v_ffn_pre_norm, v_w_up, v_conv_w, v_conv_b, v_w_down, v_ffn_post_norm):
    cx, cy, cc = _place()
    shard = 2 * cx + cy

    w_in_all = _gather_now(w_in[0].astype(BF16), "gather_w_in").reshape(N_CHIPS, D_MODEL, IN_W // N_CHIPS)
    conv_all = _gather_small(_pack_conv(conv_w[0]), "gather_conv_w")
    conv_full = jnp.concatenate([_unpack_conv(conv_all[2 * k]) for k in range(N_CHIPS)], axis=1)

    grad_x, grads, lands, small = _local_step(
        x[0], loss_target[0], mix_pre_norm, _shards_to_columns(w_in_all), attn_sinks.reshape(-1), w_out[0].astype(BF16),
        mix_post_norm, ffn_pre_norm, w_up[0].astype(BF16), conv_full, conv_b, w_down[0].astype(BF16), ffn_post_norm)

    place = jnp.stack([shard, cc]).astype(jnp.int32)
    mats = ["w_in", "w_out", "w_up", "w_down"]
    halves = [_sum_pieces(grads[n], lands[n], place, "sum_grad_" + n) for n in mats]
    weights = dict(w_in=(w_in, m_w_in, v_w_in), w_out=(w_out, m_w_out, v_w_out), w_up=(w_up, m_w_up, v_w_up),
                   w_down=(w_down, m_w_down, v_w_down))
    mat_out = {}
    for n, joined in zip(mats, _join_halves(halves)):
        w, m, v = weights[n]
        g = joined.reshape(w.shape[1:])
        mat_out[n] = (g,) + tuple(_adamw(w[0], g, m[0], v[0], "adamw_" + n))

    small_sum = _sum_devices(_gather_small(small, "gather_small_grads"))
    d_conv_full = small_sum[SMALL_ROWS:].reshape(-1)[:CONV_WIDTH * UP_W].reshape(CONV_WIDTH, UP_W)
    d_conv_mine = lax.dynamic_slice_in_dim(d_conv_full, shard * (UP_W // N_CHIPS), UP_W // N_CHIPS, axis=1)
    g_s = jnp.concatenate([small_sum[:SMALL_ROWS], _pack_conv(d_conv_mine)], axis=0)
    zero = jnp.zeros((), F32)
    pack_rep = lambda a, b, c_, d, e, f, cw: jnp.concatenate([_pack_small(a, b, c_, d, e, f, zero), _pack_conv(cw[0])], axis=0)
    w_s = pack_rep(mix_pre_norm, mix_post_norm, ffn_pre_norm, ffn_post_norm, attn_sinks, conv_b, conv_w)
    m_s = pack_rep(m_mix_pre_norm, m_mix_post_norm, m_ffn_pre_norm, m_ffn_post_norm, m_attn_sinks, m_conv_b, m_conv_w)
    v_s = pack_rep(v_mix_pre_norm, v_mix_post_norm, v_ffn_pre_norm, v_ffn_post_norm, v_attn_sinks, v_conv_b, v_conv_w)
    delta_s, new_m_s, new_v_s = _adamw(w_s, g_s, m_s, v_s, "adamw_small")

    names = ["mix_pre_norm", "w_in", "attn_sinks", "w_out", "mix_post_norm", "ffn_pre_norm", "w_up", "conv_w", "conv_b",
             "w_down", "ffn_post_norm"]

    def leaves(which, packed_small):
        smalls = _unpack_small(packed_small)
        return [mat_out[n][which][None] if n in mat_out else (smalls[n][None] if n == "conv_w" else smalls[n]) for n in names]

    loss = _unpack_small(g_s)["loss"]
    return (loss, grad_x[None], *leaves(0, g_s), *leaves(1, delta_s), *leaves(2, new_m_s), *leaves(3, new_v_s))
```

```python
import functools
import math

import jax
import jax.numpy as jnp
from jax import lax
from jax.experimental import pallas as pl
from jax.experimental.pallas import tpu as pltpu

F32 = jnp.float32
BF16 = jnp.bfloat16

D_MODEL = 1024
HEAD_DIM = 64
ATTN_W = 512
N_ATTN_HEADS = 8
KV_W = 128
RET_W = 512
N_RET_HEADS = 4
RET_HEAD_DIM = 128
CHUNK = 128
IN_W = 2816
D_FF = 2816
UP_W = 2 * D_FF
CONV_WIDTH = 3
RMS_EPS = 1e-6
GN_EPS = 1e-6
MASK_VALUE = -1e30
ATTN_SCALE = HEAD_DIM ** -0.5
RET_K_SCALE = RET_HEAD_DIM ** -0.5
GELU_C = math.sqrt(2.0 / math.pi)
GELU_A = 0.044715

ADAM_LR = 0.001
ADAM_B1 = 0.9
ADAM_B2 = 0.999
ADAM_EPS = 1e-08
ADAM_WD = 0.01
ADAM_STEP = 10

N_CHIPS = 4
N_DEV = 8
MESH = pl.DeviceIdType.MESH
VMEM_LIMIT_V7X = 56 * 1024 * 1024
TOKEN_TILE = 256
BIG_TOKEN_TILE = 512
WEIGHT_GRAD_TOKENS = 1024
FFN_ROW_BLOCK = 64

Q_A0, KV_A0, Q_R0, K_R0, V_R0, G_R0 = 0, 512, 768, 1280, 1792, 2304

ROWS_W_IN, ROWS_W_OUT, ROWS_W_UP, ROWS_W_DOWN = 704, 256, 1408, 704
ROWS_PACK = ROWS_W_IN + ROWS_W_OUT + ROWS_W_UP + ROWS_W_DOWN
ROWS_CONV = 8
SMALL_ROWS = 16
CONV_FULL_ROWS = 24


def _params(sem=None, **kw):
    if sem is not None:
        kw["dimension_semantics"] = sem
    return pltpu.CompilerParams(vmem_limit_bytes=VMEM_LIMIT_V7X, **kw)


def _resident(shape):
    zeros = (0,) * len(shape)
    return pl.BlockSpec(shape, lambda *_: zeros, pipeline_mode=pl.Buffered(1))


class _Hosted:
    def __init__(self, ins, outs, aliases, n_pairs, n_local, start, finish):
        self.ins, self.outs, self.aliases = list(ins), list(outs), dict(aliases)
        self.n_pairs, self.n_local, self.start, self.finish = n_pairs, max(n_local, 1), start, finish


def _hosted_call(compute, *, name, grid, in_specs, out_specs, out_shape, scratch_shapes, args, hosted=None):
    params = _params(("arbitrary",))
    if hosted is None:
        res = pl.pallas_call(compute, name=name, grid=grid, in_specs=in_specs, out_specs=out_specs, out_shape=out_shape,
                             scratch_shapes=scratch_shapes, compiler_params=params)(*args)
        return list(res), []
    n_in, n_out, n_scr = len(in_specs), len(out_specs), len(scratch_shapes)
    h_in, h_out = len(hosted.ins), len(hosted.outs)
    last = grid[0] - 1

    def body(*refs):
        ins, refs = refs[:n_in], refs[n_in:]
        h_ins, refs = refs[:h_in], refs[h_in:]
        outs, refs = refs[:n_out], refs[n_out:]
        h_outs, refs = refs[:h_out], refs[h_out:]
        scr, sems = refs[:n_scr], refs[n_scr:]

        @pl.when(pl.program_id(0) == 0)
        def _():
            hosted.start(h_ins, h_outs, *sems)

        compute(*ins, *outs, *scr)

        @pl.when(pl.program_id(0) == last)
        def _():
            hosted.finish(h_ins, h_outs, *sems)

    hbm = pl.BlockSpec(memory_space=pl.ANY)
    res = pl.pallas_call(
        body, name=name, grid=grid,
        in_specs=list(in_specs) + [hbm] * h_in, out_specs=list(out_specs) + [hbm] * h_out,
        out_shape=list(out_shape) + hosted.outs,
        scratch_shapes=list(scratch_shapes) + [pltpu.SemaphoreType.DMA((hosted.n_pairs,)), pltpu.SemaphoreType.DMA((hosted.n_pairs,)),
                                               pltpu.SemaphoreType.DMA((hosted.n_local,))],
        input_output_aliases={n_in + a: n_out + b for a, b in hosted.aliases.items()},
        compiler_params=params,
    )(*args, *hosted.ins)
    return list(res[:n_out]), list(res[n_out:])


def _dot(a, b):
    return jnp.dot(a, b, preferred_element_type=F32)


def _dot_nt(a, b):
    return lax.dot_general(a, b, (((1,), (1,)), ((), ())), preferred_element_type=F32)


def _dot_tn(a, b):
    return lax.dot_general(a, b, (((0,), (0,)), ((), ())), preferred_element_type=F32)


def _shift_matrix(n, by):
    row = lax.broadcasted_iota(jnp.int32, (n, n), 0)
    col = lax.broadcasted_iota(jnp.int32, (n, n), 1)
    return jnp.where(col == row + by, 1.0, 0.0).astype(BF16)


def _rstd(v):
    return lax.rsqrt(jnp.mean(v * v, axis=-1, keepdims=True) + RMS_EPS)


def _rms_bwd(dy, v, rstd, gain):
    n = v * rstd
    dgain = jnp.sum(dy * n, axis=0, keepdims=True)
    dn = dy * gain
    dv = rstd * (dn - n * jnp.mean(dn * n, axis=-1, keepdims=True))
    return dv, dgain


def _lane_lo(shape):
    return (lax.broadcasted_iota(jnp.int32, shape, 1) % 128) < HEAD_DIM


GROUP = N_ATTN_HEADS // (KV_W // HEAD_DIM)


def _attn_valid(chunk_index):
    qi = lax.broadcasted_iota(jnp.int32, (GROUP * CHUNK, 2 * CHUNK), 0) % CHUNK
    kj = lax.broadcasted_iota(jnp.int32, (GROUP * CHUNK, 2 * CHUNK), 1)
    first_key = jnp.where(chunk_index > 0, 0, CHUNK)
    return jnp.logical_and(jnp.logical_and(kj > qi, kj >= first_key), kj <= qi + CHUNK)


def _half(shape, hk):
    lo = _lane_lo(shape)
    return lo if hk == 0 else jnp.logical_not(lo)


def _stack_heads(ref, col0, hk):
    half = _half((CHUNK, 128), hk)
    parts = []
    for j in range(GROUP):
        h = GROUP * hk + j
        pair = ref[:, col0 + (h // 2) * 128:col0 + (h // 2 + 1) * 128].astype(F32)
        if h % 2 != hk:
            pair = pltpu.roll(pair, HEAD_DIM, 1)
        parts.append(jnp.where(half, pair, 0.0))
    return jnp.concatenate(parts, axis=0)


def _unstack_heads(stacked, hk):
    pairs = []
    for q in range(GROUP // 2):
        even, odd = stacked[2 * q * CHUNK:(2 * q + 1) * CHUNK], stacked[(2 * q + 1) * CHUNK:(2 * q + 2) * CHUNK]
        pairs.append(even + pltpu.roll(odd, HEAD_DIM, 1) if hk == 0 else pltpu.roll(even, HEAD_DIM, 1) + odd)
    return pairs


def _group_sinks(sk_ref, hk):
    row = lax.broadcasted_iota(jnp.int32, (GROUP * CHUNK, 1), 0)
    col = jnp.full((GROUP * CHUNK, 1), sk_ref[GROUP * hk], F32)
    for j in range(1, GROUP):
        col = jnp.where(row >= j * CHUNK, sk_ref[GROUP * hk + j], col)
    return col


def _attn_probs(q_b, kk_b, valid, sink):
    s = _dot_nt(q_b, kk_b) * ATTN_SCALE
    s = jnp.where(valid, s, MASK_VALUE)
    m = jnp.maximum(jnp.max(s, axis=-1, keepdims=True), sink)
    e = jnp.exp(s - m)
    e_sink = jnp.exp(sink - m)
    inv = 1.0 / (jnp.sum(e, axis=-1, keepdims=True) + e_sink)
    return e * inv, e_sink * inv


def _rot2(v):
    w = v.shape[1]
    even = (lax.broadcasted_iota(jnp.int32, v.shape, 1) % 2) == 0
    return jnp.where(even, -pltpu.roll(v, w - 1, 1), pltpu.roll(v, 1, 1))


def _tile4(v):
    return jnp.concatenate([v, v, v, v], axis=-1)


def _sigmoid(v):
    return 1.0 / (1.0 + jnp.exp(-v))


def _ret_constants():
    h = N_RET_HEADS
    log_gamma = jnp.log(1.0 - jnp.power(2.0, -5.0 - jnp.arange(h, dtype=F32)))
    idx = jnp.arange(CHUNK, dtype=F32)
    rel = idx[:, None] - idx[None, :]
    d_intra = jnp.where(rel[None] >= 0, jnp.exp(log_gamma[:, None, None] * jnp.maximum(rel, 0.0)[None]), 0.0)
    xi = jnp.exp(log_gamma[None, :] * (idx[:, None] + 1.0))
    zeta = jnp.exp(log_gamma[None, :] * (CHUNK - 1.0 - idx[:, None]))
    decay = jnp.exp(log_gamma * CHUNK)
    xi_full = jnp.repeat(xi, RET_HEAD_DIM, axis=1)
    zeta_full = jnp.repeat(zeta, RET_HEAD_DIM, axis=1)
    decay_full = jnp.broadcast_to(jnp.repeat(decay, RET_HEAD_DIM)[None, :], (8, RET_W))
    return d_intra.astype(F32), xi_full.astype(F32), zeta_full.astype(F32), decay_full.astype(F32)


def _rope_tables(s):
    pos = jnp.arange(s, dtype=F32)
    angle = 1.0 / jnp.power(10000.0, jnp.linspace(0.0, 1.0, RET_HEAD_DIM // 2, dtype=F32))
    angle = jnp.repeat(angle, 2)
    return jnp.sin(pos[:, None] * angle[None]), jnp.cos(pos[:, None] * angle[None])


def _in_proj(x, gain, w_in, hosted=None):
    s = x.shape[0]
    tm = min(BIG_TOKEN_TILE, s)

    def body(x_ref, g_ref, w_ref, h_ref, p_ref):
        xv = x_ref[...]
        h = (xv * _rstd(xv) * g_ref[...]).astype(BF16)
        h_ref[...] = h
        p_ref[...] = _dot(h, w_ref[...])

    return _hosted_call(
        body, name="in_proj", grid=(s // tm,),
        in_specs=[pl.BlockSpec((tm, D_MODEL), lambda i: (i, 0)), _resident((1, D_MODEL)), _resident((D_MODEL, IN_W))],
        out_specs=[pl.BlockSpec((tm, D_MODEL), lambda i: (i, 0)), pl.BlockSpec((tm, IN_W), lambda i: (i, 0))],
        out_shape=[jax.ShapeDtypeStruct((s, D_MODEL), BF16), jax.ShapeDtypeStruct((s, IN_W), F32)],
        scratch_shapes=[], args=(x, gain, w_in), hosted=hosted)


def _mixer_fwd(proj, sinks, sin, cos, consts, hosted=None):
    s = proj.shape[0]
    nc = s // CHUNK
    d_intra, xi_full, zeta_full, decay_full = consts

    def body(sk_ref, p_ref, pkv_ref, sin_ref, cos_ref, dm_ref, xi_ref, ze_ref, dc_ref, mix_ref, st_ref, state):
        n = pl.program_id(0)

        @pl.when(n == 0)
        def _():
            state[...] = jnp.zeros_like(state)

        kv_cur = p_ref[:, KV_A0:KV_A0 + 2 * KV_W]
        kv_prev = pkv_ref[...]
        kk = jnp.concatenate([kv_prev[:, :KV_W], kv_cur[:, :KV_W]], axis=0)
        vv = jnp.concatenate([kv_prev[:, KV_W:], kv_cur[:, KV_W:]], axis=0)
        kk_b = kk.astype(BF16)
        valid = _attn_valid(n)
        for hk in range(KV_W // HEAD_DIM):
            q_b = _stack_heads(p_ref, Q_A0, hk).astype(BF16)
            p, _ = _attn_probs(q_b, kk_b, valid, _group_sinks(sk_ref, hk))
            v_b = jnp.where(_half((2 * CHUNK, 128), hk), vv, 0.0).astype(BF16)
            for q, pair in enumerate(_unstack_heads(_dot(p.astype(BF16), v_b), hk)):
                pi = (GROUP // 2) * hk + q
                mix_ref[:, pi * 128:(pi + 1) * 128] = pair.astype(BF16)

        sin4, cos4 = _tile4(sin_ref[...]), _tile4(cos_ref[...])
        q_r = p_ref[:, Q_R0:Q_R0 + RET_W]
        k_r = p_ref[:, K_R0:K_R0 + RET_W] * RET_K_SCALE
        q_r = q_r * cos4 + _rot2(q_r) * sin4
        k_r = k_r * cos4 + _rot2(k_r) * sin4
        kz = k_r * ze_ref[...]
        for h in range(N_RET_HEADS):
            sl = slice(h * RET_HEAD_DIM, (h + 1) * RET_HEAD_DIM)
            qh, kh = q_r[:, sl].astype(BF16), k_r[:, sl].astype(BF16)
            vh = p_ref[:, V_R0 + h * RET_HEAD_DIM:V_R0 + (h + 1) * RET_HEAD_DIM].astype(BF16)
            st = state[h]
            st_ref[0, h] = st
            a = _dot_nt(qh, kh) * dm_ref[h]
            o = _dot(a.astype(BF16), vh) + _dot(qh, st.astype(BF16)) * xi_ref[:, sl]
            state[h] = dc_ref[0:1, sl] * st + _dot_tn(kz[:, sl].astype(BF16), vh)
            mu = jnp.mean(o, axis=-1, keepdims=True)
            oc = o - mu
            on = oc * lax.rsqrt(jnp.mean(oc * oc, axis=-1, keepdims=True) + GN_EPS)
            g = p_ref[:, G_R0 + h * RET_HEAD_DIM:G_R0 + (h + 1) * RET_HEAD_DIM]
            mix_ref[:, ATTN_W + h * RET_HEAD_DIM:ATTN_W + (h + 1) * RET_HEAD_DIM] = (g * _sigmoid(g) * on).astype(BF16)

    return _hosted_call(
        body, name="mixer_fwd", grid=(nc,),
        in_specs=[
            pl.BlockSpec(memory_space=pltpu.SMEM),
            pl.BlockSpec((CHUNK, IN_W), lambda n: (n, 0)),
            pl.BlockSpec((CHUNK, 2 * KV_W), lambda n: (jnp.maximum(n - 1, 0), KV_A0 // (2 * KV_W))),
            pl.BlockSpec((CHUNK, RET_HEAD_DIM), lambda n: (n, 0)),
            pl.BlockSpec((CHUNK, RET_HEAD_DIM), lambda n: (n, 0)),
            _resident((N_RET_HEADS, CHUNK, CHUNK)), _resident((CHUNK, RET_W)), _resident((CHUNK, RET_W)), _resident((8, RET_W)),
        ],
        out_specs=[
            pl.BlockSpec((CHUNK, D_MODEL), lambda n: (n, 0)),
            pl.BlockSpec((1, N_RET_HEADS, RET_HEAD_DIM, RET_HEAD_DIM), lambda n: (n, 0, 0, 0)),
        ],
        out_shape=[jax.ShapeDtypeStruct((s, D_MODEL), BF16),
                   jax.ShapeDtypeStruct((nc, N_RET_HEADS, RET_HEAD_DIM, RET_HEAD_DIM), F32)],
        scratch_shapes=[pltpu.VMEM((N_RET_HEADS, RET_HEAD_DIM, RET_HEAD_DIM), F32)],
        args=(sinks, proj, proj, sin, cos, d_intra, xi_full, zeta_full, decay_full), hosted=hosted)


def _out_up_proj(mix, x, w_out, g_post, g_pre, w_up):
    s = x.shape[0]
    tm = min(BIG_TOKEN_TILE, s)
    blk = UP_W // N_CHIPS

    def body(mix_ref, x_ref, wo_ref, g2_ref, g3_ref, wu_ref, mixed_ref, x1_ref, h2_ref, u0_ref):
        mixed = _dot(mix_ref[...], wo_ref[...])
        mixed_ref[...] = mixed
        x1 = x_ref[...] + mixed * _rstd(mixed) * g2_ref[...]
        x1_ref[...] = x1
        h2 = (x1 * _rstd(x1) * g3_ref[...]).astype(BF16)
        h2_ref[...] = h2
        for k in range(N_CHIPS):
            u0_ref[:, k * blk:(k + 1) * blk] = _dot(h2, wu_ref[k]).astype(BF16)

    tok = lambda w: pl.BlockSpec((tm, w), lambda i: (i, 0))
    return pl.pallas_call(
        body, name="out_up_proj", grid=(s // tm,),
        in_specs=[tok(D_MODEL), tok(D_MODEL), _resident((D_MODEL, D_MODEL)), _resident((1, D_MODEL)), _resident((1, D_MODEL)),
                  _resident((N_CHIPS, D_MODEL, blk))],
        out_specs=[tok(D_MODEL), tok(D_MODEL), tok(D_MODEL), tok(UP_W)],
        out_shape=[jax.ShapeDtypeStruct((s, D_MODEL), F32), jax.ShapeDtypeStruct((s, D_MODEL), F32),
                   jax.ShapeDtypeStruct((s, D_MODEL), BF16), jax.ShapeDtypeStruct((s, UP_W), BF16)],
        compiler_params=_params(("arbitrary",)),
    )(mix, x, w_out, g_post, g_pre, w_up)


def _ffn_tail(u0, x1, target, conv_w, conv_b, w_down, g_post):
    s = x1.shape[0]
    tm = TOKEN_TILE
    last = s // tm - 1
    rb, lanes = FFN_ROW_BLOCK, 128

    def body(u0_ref, x1_ref, t_ref, cw_ref, cb_ref, wd_ref, g_ref,
             y_ref, dy2_ref, dout_ref, du_ref, cacc_ref, gacc_ref, u1_s, u2_s, carry, gelu_s, slope_s, dy_s, cacc):
        i = pl.program_id(0)

        @pl.when(i == 0)
        def _():
            carry[...] = jnp.zeros_like(carry)
            cacc[...] = jnp.zeros_like(cacc)
            gacc_ref[...] = jnp.zeros_like(gacc_ref)

        u1_s[...] = _dot(_shift_matrix(tm, -1), u0_ref[...])
        u2_s[...] = _dot(_shift_matrix(tm, -2), u0_ref[...])
        r8 = lax.broadcasted_iota(jnp.int32, (8, 1), 0)
        c14, c15 = carry[14:15, :], carry[15:16, :]
        u1_s[0:8, :] = jnp.where(r8 == 0, c15, u1_s[0:8, :])
        u2_s[0:8, :] = jnp.where(r8 == 0, c14, jnp.where(r8 == 1, c15, u2_s[0:8, :]))
        carry[...] = u0_ref[tm - 16:tm, :].astype(F32)

        def taps(col):
            return (cw_ref[0:1, col:col + lanes], cw_ref[1:2, col:col + lanes], cw_ref[2:3, col:col + lanes],
                    cb_ref[0:1, col:col + lanes])

        def shifted(r0, col):
            return (u2_s[r0:r0 + rb, col:col + lanes], u1_s[r0:r0 + rb, col:col + lanes],
                    u0_ref[r0:r0 + rb, col:col + lanes].astype(F32))

        def conv(r0, col, w):
            u2, u1, uc = shifted(r0, col)
            return w[0] * u2 + w[1] * u1 + w[2] * uc + w[3]

        fold = lambda v: jnp.sum(v.reshape(rb // 8, 8, lanes), axis=0)

        for j in range(D_FF // lanes):
            cg, cv = j * lanes, D_FF + j * lanes
            wg, wv = taps(cg), taps(cv)
            for r0 in range(0, tm, rb):
                gate, val = conv(r0, cg, wg), conv(r0, cv, wv)
                g2 = gate * gate
                th = jnp.tanh(gate * (GELU_C + GELU_C * GELU_A * g2))
                hp = 0.5 * th + 0.5
                gelu = gate * hp
                dgelu = hp + gate * (1.0 - th * th) * (0.5 * GELU_C + 1.5 * GELU_C * GELU_A * g2)
                y_ref[r0:r0 + rb, cg:cg + lanes] = (gelu * val).astype(BF16)
                gelu_s[r0:r0 + rb, cg:cg + lanes] = gelu
                slope_s[r0:r0 + rb, cg:cg + lanes] = dgelu * val

        y2 = _dot(y_ref[...], wd_ref[...])
        r4 = _rstd(y2)
        gain = g_ref[...]
        out = x1_ref[...] + y2 * r4 * gain
        diff = out - t_ref[...]
        dout = diff * (1.0 / D_MODEL)
        dout_ref[...] = dout
        dy2, dgain = _rms_bwd(dout, y2, r4, gain)
        dy2_b = dy2.astype(BF16)
        dy2_ref[...] = dy2_b
        gacc_ref[0:1, :] += dgain
        gacc_ref[1:2, :] += 0.5 * jnp.sum(diff * dout, axis=0, keepdims=True)
        dy_s[...] = _dot_nt(dy2_b, wd_ref[...])

        for j in range(D_FF // lanes):
            cg, cv = j * lanes, D_FF + j * lanes
            acc = [[jnp.zeros((8, lanes), F32) for _ in range(CONV_WIDTH + 1)] for _ in range(2)]
            for r0 in range(0, tm, rb):
                dy = dy_s[r0:r0 + rb, cg:cg + lanes]
                d_gate = dy * slope_s[r0:r0 + rb, cg:cg + lanes]
                d_val = dy * gelu_s[r0:r0 + rb, cg:cg + lanes]
                for side, (col, d) in enumerate(((cg, d_gate), (cv, d_val))):
                    du_ref[r0:r0 + rb, col:col + lanes] = d.astype(BF16)
                    for k, u in enumerate(shifted(r0, col)):
                        acc[side][k] = acc[side][k] + fold(d * u)
                    acc[side][CONV_WIDTH] = acc[side][CONV_WIDTH] + fold(d)
            for side, col in enumerate((cg, cv)):
                for k in range(CONV_WIDTH + 1):
                    cacc[8 * k:8 * k + 8, col:col + lanes] += acc[side][k]

        @pl.when(i == last)
        def _():
            for k in range(CONV_WIDTH + 1):
                cacc_ref[k:k + 1, :] = jnp.sum(cacc[8 * k:8 * k + 8, :], axis=0, keepdims=True)
            cacc_ref[CONV_WIDTH + 1:8, :] = jnp.zeros((8 - CONV_WIDTH - 1, UP_W), F32)

    tok = lambda w: pl.BlockSpec((tm, w), lambda i: (i, 0))
    return pl.pallas_call(
        body, name="ffn_tail", grid=(s // tm,),
        in_specs=[tok(UP_W), tok(D_MODEL), tok(D_MODEL), _resident((CONV_WIDTH, UP_W)), _resident((1, UP_W)),
                  _resident((D_FF, D_MODEL)), _resident((1, D_MODEL))],
        out_specs=[tok(D_FF), tok(D_MODEL), tok(D_MODEL), tok(UP_W),
                   pl.BlockSpec((8, UP_W), lambda i: (0, 0)), pl.BlockSpec((8, D_MODEL), lambda i: (0, 0))],
        out_shape=[jax.ShapeDtypeStruct((s, D_FF), BF16), jax.ShapeDtypeStruct((s, D_MODEL), BF16),
                   jax.ShapeDtypeStruct((s, D_MODEL), F32), jax.ShapeDtypeStruct((s, UP_W), BF16),
                   jax.ShapeDtypeStruct((8, UP_W), F32), jax.ShapeDtypeStruct((8, D_MODEL), F32)],
        scratch_shapes=[pltpu.VMEM((tm, UP_W), F32), pltpu.VMEM((tm, UP_W), F32), pltpu.VMEM((16, UP_W), F32),
                        pltpu.VMEM((tm, D_FF), F32), pltpu.VMEM((tm, D_FF), F32),
                        pltpu.VMEM((tm, D_FF), F32), pltpu.VMEM((8 * (CONV_WIDTH + 1), UP_W), F32)],
        compiler_params=_params(("arbitrary",)),
    )(u0, x1, target, conv_w, conv_b, w_down, g_post)


def _ffn_head_bwd(du, conv_w, w_up, x1, g_pre, dout, mixed, g_post, w_out):
    s = x1.shape[0]
    tm = TOKEN_TILE
    nt = s // tm
    blk = UP_W // N_CHIPS

    def body(du_ref, halo_ref, cw_ref, wu_ref, x1_ref, g3_ref, dout_ref, mixed_ref, g2_ref, wo_ref,
             du0_ref, dx1_ref, dmixed_ref, dmix_ref, gacc_ref, dbuf):
        i = pl.program_id(0)

        @pl.when(i == 0)
        def _():
            gacc_ref[...] = jnp.zeros_like(gacc_ref)

        dbuf[0:tm, :] = du_ref[...].astype(F32)
        dbuf[tm:tm + 16, :] = jnp.where(i < nt - 1, halo_ref[...].astype(F32), 0.0)
        dh2 = jnp.zeros((tm, D_MODEL), F32)
        for k in range(N_CHIPS):
            cols = slice(k * blk, (k + 1) * blk)
            du0_b = (cw_ref[2:3, cols] * dbuf[0:tm, cols] + cw_ref[1:2, cols] * dbuf[1:1 + tm, cols]
                     + cw_ref[0:1, cols] * dbuf[2:2 + tm, cols]).astype(BF16)
            du0_ref[:, cols] = du0_b
            dh2 = dh2 + _dot_nt(du0_b, wu_ref[k])
        x1 = x1_ref[...]
        d3, dg3 = _rms_bwd(dh2, x1, _rstd(x1), g3_ref[...])
        dx1 = dout_ref[...] + d3
        dx1_ref[...] = dx1
        mixed = mixed_ref[...]
        dmixed, dg2 = _rms_bwd(dx1, mixed, _rstd(mixed), g2_ref[...])
        dmixed_b = dmixed.astype(BF16)
        dmixed_ref[...] = dmixed_b
        dmix_ref[...] = _dot_nt(dmixed_b, wo_ref[...]).astype(BF16)
        gacc_ref[0:1, :] += dg3
        gacc_ref[1:2, :] += dg2

    tok = lambda w: pl.BlockSpec((tm, w), lambda i: (i, 0))
    halo = pl.BlockSpec((16, UP_W), lambda i: (jnp.minimum(i + 1, nt - 1) * (tm // 16), 0))
    return pl.pallas_call(
        body, name="ffn_head_bwd", grid=(nt,),
        in_specs=[tok(UP_W), halo, _resident((CONV_WIDTH, UP_W)), _resident((N_CHIPS, D_MODEL, blk)), tok(D_MODEL),
                  _resident((1, D_MODEL)), tok(D_MODEL), tok(D_MODEL), _resident((1, D_MODEL)), _resident((D_MODEL, D_MODEL))],
        out_specs=[tok(UP_W), tok(D_MODEL), tok(D_MODEL), tok(D_MODEL), pl.BlockSpec((8, D_MODEL), lambda i: (0, 0))],
        out_shape=[jax.ShapeDtypeStruct((s, UP_W), BF16), jax.ShapeDtypeStruct((s, D_MODEL), F32),
                   jax.ShapeDtypeStruct((s, D_MODEL), BF16), jax.ShapeDtypeStruct((s, D_MODEL), BF16),
                   jax.ShapeDtypeStruct((8, D_MODEL), F32)],
        scratch_shapes=[pltpu.VMEM((tm + 16, UP_W), F32)],
        compiler_params=_params(("arbitrary",)),
    )(du, du, conv_w, w_up, x1, g_pre, dout, mixed, g_post, w_out)


def _mixer_bwd(proj, dmix, states, sinks, sin, cos, consts, hosted=None):
    s = proj.shape[0]
    nc = s // CHUNK
    d_intra, xi_full, zeta_full, decay_full = consts

    def body(sk_ref, p_ref, pkv_ref, dmix_ref, st_ref, sin_ref, cos_ref, dm_ref, xi_ref, ze_ref, dc_ref,
             dp_ref, dsk_ref, gstate, ckv, dsk_acc):
        i = pl.program_id(0)
        n = nc - 1 - i

        @pl.when(i == 0)
        def _():
            gstate[...] = jnp.zeros_like(gstate)
            ckv[...] = jnp.zeros_like(ckv)
            dsk_acc[...] = jnp.zeros_like(dsk_acc)

        kv_cur = p_ref[:, KV_A0:KV_A0 + 2 * KV_W]
        kv_prev = pkv_ref[...]
        kk = jnp.concatenate([kv_prev[:, :KV_W], kv_cur[:, :KV_W]], axis=0)
        vv = jnp.concatenate([kv_prev[:, KV_W:], kv_cur[:, KV_W:]], axis=0)
        kk_b, vv_b = kk.astype(BF16), vv.astype(BF16)
        valid = _attn_valid(n)
        lane = lax.broadcasted_iota(jnp.int32, (CHUNK, 128), 1)
        dkk = jnp.zeros((2 * CHUNK, KV_W), F32)
        dvv = jnp.zeros((2 * CHUNK, KV_W), F32)
        dsk = jnp.zeros((CHUNK, 128), F32)
        for hk in range(KV_W // HEAD_DIM):
            q_b = _stack_heads(p_ref, Q_A0, hk).astype(BF16)
            do_b = _stack_heads(dmix_ref, 0, hk).astype(BF16)
            p, p_sink = _attn_probs(q_b, kk_b, valid, _group_sinks(sk_ref, hk))
            dpr = _dot_nt(do_b, vv_b)
            delta = jnp.sum(p * dpr, axis=-1, keepdims=True)
            ds_b = (p * (dpr - delta) * ATTN_SCALE).astype(BF16)
            dsink = -p_sink * delta
            for j in range(GROUP):
                dsk = dsk + jnp.where(lane == GROUP * hk + j, dsink[j * CHUNK:(j + 1) * CHUNK], 0.0)
            k_b = jnp.where(_half((2 * CHUNK, 128), hk), kk, 0.0).astype(BF16)
            for q, pair in enumerate(_unstack_heads(_dot(ds_b, k_b), hk)):
                pi = (GROUP // 2) * hk + q
                dp_ref[:, Q_A0 + pi * 128:Q_A0 + (pi + 1) * 128] = pair.astype(BF16)
            dkk = dkk + _dot_tn(ds_b, q_b)
            dvv = dvv + _dot_tn(p.astype(BF16), do_b)
        dp_ref[:, KV_A0:KV_A0 + KV_W] = (dkk[CHUNK:] + ckv[:, :KV_W]).astype(BF16)
        dp_ref[:, KV_A0 + KV_W:KV_A0 + 2 * KV_W] = (dvv[CHUNK:] + ckv[:, KV_W:]).astype(BF16)
        ckv[:, :KV_W] = dkk[:CHUNK]
        ckv[:, KV_W:] = dvv[:CHUNK]
        dsk_acc[...] += dsk

        @pl.when(i == nc - 1)
        def _():
            dsk_ref[...] = jnp.sum(dsk_acc[...], axis=0, keepdims=True)

        sin4, cos4 = _tile4(sin_ref[...]), _tile4(cos_ref[...])
        q_r = p_ref[:, Q_R0:Q_R0 + RET_W]
        k_r = p_ref[:, K_R0:K_R0 + RET_W] * RET_K_SCALE
        q_r = q_r * cos4 + _rot2(q_r) * sin4
        k_r = k_r * cos4 + _rot2(k_r) * sin4
        kz = k_r * ze_ref[...]
        dq_parts, dk_parts = [], []
        for h in range(N_RET_HEADS):
            sl = slice(h * RET_HEAD_DIM, (h + 1) * RET_HEAD_DIM)
            qh, kh = q_r[:, sl].astype(BF16), k_r[:, sl].astype(BF16)
            vh = p_ref[:, V_R0 + h * RET_HEAD_DIM:V_R0 + (h + 1) * RET_HEAD_DIM].astype(BF16)
            st_b = st_ref[0, h].astype(BF16)
            gs = gstate[h]
            gs_b = gs.astype(BF16)
            xi_h = xi_ref[:, sl]
            dm = dm_ref[h]
            a_b = (_dot_nt(qh, kh) * dm).astype(BF16)
            o = _dot(a_b, vh) + _dot(qh, st_b) * xi_h
            mu = jnp.mean(o, axis=-1, keepdims=True)
            oc = o - mu
            rs = lax.rsqrt(jnp.mean(oc * oc, axis=-1, keepdims=True) + GN_EPS)
            on = oc * rs
            g = p_ref[:, G_R0 + h * RET_HEAD_DIM:G_R0 + (h + 1) * RET_HEAD_DIM]
            sg = _sigmoid(g)
            dr = dmix_ref[:, ATTN_W + h * RET_HEAD_DIM:ATTN_W + (h + 1) * RET_HEAD_DIM].astype(F32)
            dp_ref[:, G_R0 + h * RET_HEAD_DIM:G_R0 + (h + 1) * RET_HEAD_DIM] = (
                dr * on * (sg * (1.0 + g * (1.0 - sg)))).astype(BF16)
            don = dr * g * sg
            do = rs * (don - jnp.mean(don, axis=-1, keepdims=True) - on * jnp.mean(don * on, axis=-1, keepdims=True))
            do_b = do.astype(BF16)
            dox_b = (do * xi_h).astype(BF16)
            da_b = (_dot_nt(do_b, vh) * dm).astype(BF16)
            dq_parts.append(_dot(da_b, kh) + _dot_nt(dox_b, st_b))
            dk_parts.append(_dot_tn(da_b, qh) + ze_ref[:, sl] * _dot_nt(vh, gs_b))
            dv = _dot_tn(a_b, do_b) + _dot(kz[:, sl].astype(BF16), gs_b)
            dp_ref[:, V_R0 + h * RET_HEAD_DIM:V_R0 + (h + 1) * RET_HEAD_DIM] = dv.astype(BF16)
            gstate[h] = dc_ref[0:1, sl] * gs + _dot_tn(qh, dox_b)
        dq = jnp.concatenate(dq_parts, axis=-1)
        dk = jnp.concatenate(dk_parts, axis=-1)
        dp_ref[:, Q_R0:Q_R0 + RET_W] = (dq * cos4 - _rot2(dq * sin4)).astype(BF16)
        dp_ref[:, K_R0:K_R0 + RET_W] = (RET_K_SCALE * (dk * cos4 - _rot2(dk * sin4))).astype(BF16)

    rev = lambda i: nc - 1 - i
    return _hosted_call(
        body, name="mixer_bwd", grid=(nc,),
        in_specs=[
            pl.BlockSpec(memory_space=pltpu.SMEM),
            pl.BlockSpec((CHUNK, IN_W), lambda i: (rev(i), 0)),
            pl.BlockSpec((CHUNK, 2 * KV_W), lambda i: (jnp.maximum(rev(i) - 1, 0), KV_A0 // (2 * KV_W))),
            pl.BlockSpec((CHUNK, D_MODEL), lambda i: (rev(i), 0)),
            pl.BlockSpec((1, N_RET_HEADS, RET_HEAD_DIM, RET_HEAD_DIM), lambda i: (rev(i), 0, 0, 0)),
            pl.BlockSpec((CHUNK, RET_HEAD_DIM), lambda i: (rev(i), 0)),
            pl.BlockSpec((CHUNK, RET_HEAD_DIM), lambda i: (rev(i), 0)),
            _resident((N_RET_HEADS, CHUNK, CHUNK)), _resident((CHUNK, RET_W)), _resident((CHUNK, RET_W)), _resident((8, RET_W)),
        ],
        out_specs=[pl.BlockSpec((CHUNK, IN_W), lambda i: (rev(i), 0)), pl.BlockSpec((1, 128), lambda i: (0, 0))],
        out_shape=[jax.ShapeDtypeStruct((s, IN_W), BF16), jax.ShapeDtypeStruct((1, 128), F32)],
        scratch_shapes=[pltpu.VMEM((N_RET_HEADS, RET_HEAD_DIM, RET_HEAD_DIM), F32), pltpu.VMEM((CHUNK, 2 * KV_W), F32),
                        pltpu.VMEM((CHUNK, 128), F32)],
        args=(sinks, proj, proj, dmix, states, sin, cos, d_intra, xi_full, zeta_full, decay_full), hosted=hosted)


def _in_proj_bwd(dproj, w_in, x, gain, dx1, hosted=None):
    s = x.shape[0]
    tm = min(BIG_TOKEN_TILE, s)
    nt = s // tm
    n_tail = max(nt // 8, 1) if nt > 1 else 0
    n_head = nt - n_tail

    def step(dp_ref, w_ref, x_ref, g_ref, dx1_ref, dx_ref, gacc_ref):
        dh = _dot_nt(dp_ref[...], w_ref[...])
        xv = x_ref[...]
        d1, dg = _rms_bwd(dh, xv, _rstd(xv), g_ref[...])
        dx_ref[...] = dx1_ref[...] + d1
        gacc_ref[0:1, :] += dg

    def head(dp_ref, w_ref, x_ref, g_ref, dx1_ref, dx_ref, gacc_ref):
        @pl.when(pl.program_id(0) == 0)
        def _():
            gacc_ref[...] = jnp.zeros_like(gacc_ref)

        step(dp_ref, w_ref, x_ref, g_ref, dx1_ref, dx_ref, gacc_ref)

    def tail(dp_ref, w_ref, x_ref, g_ref, dx1_ref, acc_ref, dx_in_ref, dx_ref, gacc_ref):
        @pl.when(pl.program_id(0) == 0)
        def _():
            gacc_ref[...] = acc_ref[...]

        step(dp_ref, w_ref, x_ref, g_ref, dx1_ref, dx_ref, gacc_ref)

    def specs(first):
        tok = lambda w: pl.BlockSpec((tm, w), lambda i: (first + i, 0))
        return ([tok(IN_W), _resident((D_MODEL, IN_W)), tok(D_MODEL), _resident((1, D_MODEL)), tok(D_MODEL)],
                [tok(D_MODEL), pl.BlockSpec((8, D_MODEL), lambda i: (0, 0))])

    out_shape = [jax.ShapeDtypeStruct((s, D_MODEL), F32), jax.ShapeDtypeStruct((8, D_MODEL), F32)]
    in_specs, out_specs = specs(0)
    (grad_x, acc), lands = _hosted_call(
        head, name="in_proj_bwd", grid=(n_head,), in_specs=in_specs, out_specs=out_specs, out_shape=out_shape,
        scratch_shapes=[], args=(dproj, w_in, x, gain, dx1), hosted=hosted)
    if n_tail:
        in_specs, out_specs = specs(n_head)
        grad_x, acc = pl.pallas_call(
            tail, name="in_proj_bwd_tail", grid=(n_tail,),
            in_specs=in_specs + [_resident((8, D_MODEL)), pl.BlockSpec(memory_space=pl.ANY)], out_specs=out_specs,
            out_shape=out_shape, input_output_aliases={6: 0}, compiler_params=_params(("arbitrary",)),
        )(dproj, w_in, x, gain, dx1, acc, grad_x)
    return [grad_x, acc], lands


def _weight_grad(a, b, tn, name, by_block=False):
    s, m = a.shape
    n = b.shape[1]
    tk = min(WEIGHT_GRAD_TOKENS, s)

    def body(a_ref, b_ref, o_ref):
        @pl.when(pl.program_id(1) == 0)
        def _():
            o_ref[...] = jnp.zeros_like(o_ref)

        o_ref[...] += _dot_tn(a_ref[...], b_ref[...])

    if by_block:
        out_spec = pl.BlockSpec((None, m, tn), lambda j, k: (j, 0, 0))
        out_shape = jax.ShapeDtypeStruct((n // tn, m, tn), F32)
    else:
        out_spec = pl.BlockSpec((m, tn), lambda j, k: (0, j))
        out_shape = jax.ShapeDtypeStruct((m, n), F32)
    return pl.pallas_call(
        body, name=name, grid=(n // tn, s // tk),
        in_specs=[pl.BlockSpec((tk, m), lambda j, k: (k, 0)), pl.BlockSpec((tk, tn), lambda j, k: (k, j))],
        out_specs=out_spec, out_shape=out_shape,
        compiler_params=_params(("arbitrary", "arbitrary")),
    )(a, b)


def _place():
    return lax.axis_index("x"), lax.axis_index("y"), lax.axis_index("c")


def _remote(src, dst, send_sems, recv_sems, k, to):
    return pltpu.make_async_remote_copy(src_ref=src, dst_ref=dst, send_sem=send_sems.at[k], recv_sem=recv_sems.at[k],
                                        device_id=to, device_id_type=MESH)


def _gather_level1_copies(w_refs, out_refs, send_sems, recv_sems, local_sems):
    x, y, c = _place()
    mine_at = 2 * x + y
    peers = [(x, y, 1 - c), (1 - x, y, c), (x, 1 - y, c), (1 - x, 1 - y, c)]
    local, sends, recvs = [], [], []
    for i, (w, out) in enumerate(zip(w_refs, out_refs)):
        half = w.shape[0] // 2
        src = w.at[pl.ds(pl.multiple_of(c * half, 16), half), :]
        mine = out.at[mine_at, c]
        local.append(pltpu.make_async_copy(src, mine, local_sems.at[i]))
        for k, p in enumerate(peers):
            sends.append(_remote(src, mine, send_sems, recv_sems, 4 * i + k, p))
            lands = out.at[mine_at, 1 - c] if k == 0 else out.at[2 * p[0] + p[1], c]
            recvs.append(_remote(src, lands, send_sems, recv_sems, 4 * i + k, p))
    return local, sends, recvs


def _gather_level1_start(w_refs, out_refs, send_sems, recv_sems, local_sems):
    local, sends, _ = _gather_level1_copies(w_refs, out_refs, send_sems, recv_sems, local_sems)
    for cp in local + sends:
        cp.start()


def _gather_level1_finish(w_refs, out_refs, send_sems, recv_sems, local_sems):
    local, sends, recvs = _gather_level1_copies(w_refs, out_refs, send_sems, recv_sems, local_sems)
    for cp in recvs:
        cp.wait_recv()
    for cp in sends:
        cp.wait_send()
    for cp in local:
        cp.wait()


def _gather_level2_copies(in_refs, out_refs, send_sems, recv_sems, local_sems):
    x, y, c = _place()
    chips = [(1 - x, y), (x, 1 - y), (1 - x, 1 - y)]
    sends, recvs = [], []
    for i, (src, out) in enumerate(zip(in_refs, out_refs)):
        for j, (px, py) in enumerate(chips):
            sends.append(_remote(src.at[2 * px + py, c], out.at[2 * px + py, c], send_sems, recv_sems, 3 * i + j, (x, y, 1 - c)))
            recvs.append(_remote(src.at[2 * px + py, c], out.at[2 * px + py, 1 - c], send_sems, recv_sems, 3 * i + j,
                                 (x, y, 1 - c)))
    return sends, recvs


def _gather_level2_start(in_refs, out_refs, send_sems, recv_sems, local_sems):
    for cp in _gather_level2_copies(in_refs, out_refs, send_sems, recv_sems, local_sems)[0]:
        cp.start()


def _gather_level2_finish(in_refs, out_refs, send_sems, recv_sems, local_sems):
    sends, recvs = _gather_level2_copies(in_refs, out_refs, send_sems, recv_sems, local_sems)
    for cp in recvs:
        cp.wait_recv()
    for cp in sends:
        cp.wait_send()


def _gathered_shape(w):
    r, cols = w.shape
    return jax.ShapeDtypeStruct((N_CHIPS, 2, r // 2, cols), w.dtype)


def _hosted_gather_level1(shards):
    n = len(shards)
    return _Hosted(shards, [_gathered_shape(w) for w in shards], {}, 4 * n, n, _gather_level1_start, _gather_level1_finish)


def _hosted_gather_level2(gathered):
    n = len(gathered)
    return _Hosted(gathered, [jax.ShapeDtypeStruct(g.shape, g.dtype) for g in gathered], {i: i for i in range(n)}, 3 * n, 0,
                   _gather_level2_start, _gather_level2_finish)


def _gather_now(w, name):
    def body(w_ref, out_ref, send1, recv1, local1, send2, recv2):
        _gather_level1_start([w_ref], [out_ref], send1, recv1, local1)
        _gather_level1_finish([w_ref], [out_ref], send1, recv1, local1)
        _gather_level2_start([out_ref], [out_ref], send2, recv2, None)
        _gather_level2_finish([out_ref], [out_ref], send2, recv2, None)

    return pl.pallas_call(
        body, name=name, out_shape=_gathered_shape(w),
        in_specs=[pl.BlockSpec(memory_space=pl.ANY)], out_specs=pl.BlockSpec(memory_space=pl.ANY),
        scratch_shapes=[pltpu.SemaphoreType.DMA((4,)), pltpu.SemaphoreType.DMA((4,)), pltpu.SemaphoreType.DMA((1,)),
                        pltpu.SemaphoreType.DMA((3,)), pltpu.SemaphoreType.DMA((3,))],
    )(w)


def _scatter_copies(g_refs, land_refs, send_sems, recv_sems, local_sems):
    x, y, c = _place()
    copies = []
    for i, (g, land) in enumerate(zip(g_refs, land_refs)):
        for k, (px, py, pc) in enumerate(_relations(x, y, c)):
            copies.append(_remote(g.at[2 * px + py, pc], land.at[k], send_sems, recv_sems, 7 * i + k, (px, py, pc)))
    return copies


def _scatter_start(g_refs, land_refs, send_sems, recv_sems, local_sems):
    for cp in _scatter_copies(g_refs, land_refs, send_sems, recv_sems, local_sems):
        cp.start()


def _scatter_finish(g_refs, land_refs, send_sems, recv_sems, local_sems):
    for cp in _scatter_copies(g_refs, land_refs, send_sems, recv_sems, local_sems):
        cp.wait()


def _hosted_scatter(grads):
    lands = [jax.ShapeDtypeStruct((N_DEV - 1,) + g.shape[2:], g.dtype) for g in grads]
    return _Hosted(grads, lands, {}, 7 * len(grads), 0, _scatter_start, _scatter_finish)


def _relations(x, y, c):
    rel = []
    for fx in (0, 1):
        for fy in (0, 1):
            for fc in (0, 1):
                if fx or fy or fc:
                    rel.append(((1 - x) if fx else x, (1 - y) if fy else y, (1 - c) if fc else c))
    return rel


def _gather_small(v, name):
    r, cols = v.shape

    def body(v_ref, out_ref, send_sems, recv_sems):
        x, y, c = _place()
        peers = _relations(x, y, c)

        def slot(p):
            return out_ref.at[4 * p[0] + 2 * p[1] + p[2]]

        out_ref[4 * x + 2 * y + c] = v_ref[...]
        sends = [pltpu.make_async_remote_copy(
            src_ref=v_ref, dst_ref=slot((x, y, c)), send_sem=send_sems.at[k], recv_sem=recv_sems.at[k],
            device_id=p, device_id_type=MESH) for k, p in enumerate(peers)]
        for cp in sends:
            cp.start()
        for k, p in enumerate(peers):
            pltpu.make_async_remote_copy(
                src_ref=v_ref, dst_ref=slot(p), send_sem=send_sems.at[k], recv_sem=recv_sems.at[k],
                device_id=p, device_id_type=MESH).wait_recv()
        for cp in sends:
            cp.wait_send()

    return pl.pallas_call(
        body, name=name,
        out_shape=jax.ShapeDtypeStruct((N_DEV, r, cols), v.dtype),
        in_specs=[pl.BlockSpec(memory_space=pltpu.VMEM)],
        out_specs=pl.BlockSpec(memory_space=pltpu.VMEM),
        scratch_shapes=[pltpu.SemaphoreType.DMA((7,)), pltpu.SemaphoreType.DMA((7,))],
    )(v)


def _join_halves(shards):
    n = len(shards)

    def body(*refs):
        in_refs, out_refs = refs[:n], refs[n:2 * n]
        send_sems, recv_sems = refs[2 * n:]
        x, y, c = _place()
        sends = [_remote(src.at[c], out.at[c], send_sems, recv_sems, i, (x, y, 1 - c))
                 for i, (src, out) in enumerate(zip(in_refs, out_refs))]
        recvs = [_remote(src.at[c], out.at[1 - c], send_sems, recv_sems, i, (x, y, 1 - c))
                 for i, (src, out) in enumerate(zip(in_refs, out_refs))]
        for cp in sends:
            cp.start()
        for cp in recvs:
            cp.wait_recv()
        for cp in sends:
            cp.wait_send()

    hbm = pl.BlockSpec(memory_space=pl.ANY)
    return pl.pallas_call(
        body, name="grad_join_halves",
        out_shape=[jax.ShapeDtypeStruct(t.shape, t.dtype) for t in shards],
        in_specs=[hbm] * n, out_specs=[hbm] * n, input_output_aliases={i: i for i in range(n)},
        scratch_shapes=[pltpu.SemaphoreType.DMA((n,)), pltpu.SemaphoreType.DMA((n,))],
    )(*shards)


def _row_tile(rows, row_bytes, limit=1 << 20):
    best = 8
    for t in range(8, rows + 1, 8):
        if rows % t == 0 and t * row_bytes <= limit:
            best = t
    return best


def _sum_pieces(g, land, place, name):
    _, _, rh, cols = g.shape
    tr = _row_tile(rh, (N_DEV - 1) * cols * 4, 4 << 20)

    def body(p_ref, g_ref, l_ref, out_ref):
        acc = g_ref[...]
        for k in range(N_DEV - 1):
            acc = acc + l_ref[k]
        out_ref[...] = acc

    return pl.pallas_call(
        body, name=name,
        grid_spec=pltpu.PrefetchScalarGridSpec(
            num_scalar_prefetch=1, grid=(rh // tr,),
            in_specs=[pl.BlockSpec((None, None, tr, cols), lambda r, p: (p[0], p[1], r, 0)),
                      pl.BlockSpec((N_DEV - 1, tr, cols), lambda r, p: (0, r, 0))],
            out_specs=pl.BlockSpec((None, tr, cols), lambda r, p: (p[1], r, 0))),
        out_shape=jax.ShapeDtypeStruct((2, rh, cols), g.dtype),
        compiler_params=_params(("arbitrary",)),
    )(place, g, land)


def _adamw_math(w, g, m, v):
    m = ADAM_B1 * m + (1.0 - ADAM_B1) * g
    v = ADAM_B2 * v + (1.0 - ADAM_B2) * (g * g)
    m_hat = m / (1.0 - ADAM_B1 ** ADAM_STEP)
    v_hat = v / (1.0 - ADAM_B2 ** ADAM_STEP)
    delta = -ADAM_LR * (m_hat / (jnp.sqrt(v_hat) + ADAM_EPS) + ADAM_WD * w)
    return delta, m, v


def _adamw(w, g, m, v, name):
    r, cols = w.shape
    tr = _row_tile(r, cols * 4)

    def body(w_ref, g_ref, m_ref, v_ref, d_ref, nm_ref, nv_ref):
        d_ref[...], nm_ref[...], nv_ref[...] = _adamw_math(w_ref[...], g_ref[...], m_ref[...], v_ref[...])

    blk = pl.BlockSpec((tr, cols), lambda i: (i, 0))
    shape = jax.ShapeDtypeStruct((r, cols), F32)
    return pl.pallas_call(
        body, name=name, grid=(r // tr,), in_specs=[blk] * 4, out_specs=[blk] * 3, out_shape=[shape] * 3,
        compiler_params=_params(("arbitrary",)),
    )(w, g, m, v)


def _sum_devices(gathered):
    _, r, cols = gathered.shape

    def body(a_ref, g_ref):
        g = a_ref[0]
        for k in range(1, N_DEV):
            g = g + a_ref[k]
        g_ref[...] = g

    return pl.pallas_call(body, name="sum_small_grads", out_shape=jax.ShapeDtypeStruct((r, cols), F32))(gathered)


def _pack_conv(cw):
    flat = cw.reshape(-1)
    return jnp.pad(flat, (0, ROWS_CONV * D_MODEL - flat.shape[0])).reshape(ROWS_CONV, D_MODEL)


def _unpack_conv(rows):
    return rows.reshape(-1)[:CONV_WIDTH * UP_W // N_CHIPS].reshape(CONV_WIDTH, UP_W // N_CHIPS)


def _columns_to_shards(w):
    r, n = w.shape
    return jnp.transpose(w.reshape(r, N_CHIPS, n // N_CHIPS), (1, 0, 2))


def _shards_to_columns(w):
    _, r, n = w.shape
    return jnp.transpose(w, (1, 0, 2)).reshape(r, N_CHIPS * n)


def _pack_small(g_mix_pre, g_mix_post, g_ffn_pre, g_ffn_post, sinks, conv_b, loss):
    pad_row = lambda v: jnp.pad(v.reshape(1, -1), ((0, 0), (0, D_MODEL - v.size)))
    cb = jnp.pad(conv_b.reshape(-1), (0, 6 * D_MODEL - UP_W)).reshape(6, D_MODEL)
    zeros2 = jnp.zeros((2, D_MODEL), F32)
    return jnp.concatenate([g_mix_pre.reshape(1, -1), g_mix_post.reshape(1, -1), g_ffn_pre.reshape(1, -1),
                            g_ffn_post.reshape(1, -1), pad_row(sinks), pad_row(loss), zeros2, cb, zeros2], axis=0)


def _unpack_small(p):
    return dict(mix_pre_norm=p[0:1], mix_post_norm=p[1:2], ffn_pre_norm=p[2:3], ffn_post_norm=p[3:4],
                attn_sinks=p[4:5, :N_ATTN_HEADS], loss=p[5, 0], conv_b=p[8:14].reshape(1, -1)[:, :UP_W],
                conv_w=_unpack_conv(p[SMALL_ROWS:SMALL_ROWS + ROWS_CONV]))


def _local_step(x, target, g_mix_pre, w_in, sinks, w_out, g_mix_post, g_ffn_pre, w_up, conv_w, conv_b, w_down, g_ffn_post,
                distributed=True):
    s = x.shape[0]
    consts = _ret_constants()
    sin, cos = _rope_tables(s)

    by_half = lambda g, rows: g.reshape(N_CHIPS, 2, rows // (2 * N_CHIPS), g.shape[-1])

    if distributed:
        (h1, proj), level1 = _in_proj(x, g_mix_pre, w_in, _hosted_gather_level1([w_out, w_up, w_down]))
        (mix, states), (w_out, w_up, w_down) = _mixer_fwd(proj, sinks, sin, cos, consts, _hosted_gather_level2(level1))
        w_out, w_down = w_out.reshape(D_MODEL, D_MODEL), w_down.reshape(D_FF, D_MODEL)
        w_up = w_up.reshape(N_CHIPS, D_MODEL, UP_W // N_CHIPS)
    else:
        (h1, proj), _ = _in_proj(x, g_mix_pre, w_in)
        (mix, states), _ = _mixer_fwd(proj, sinks, sin, cos, consts)
    mixed, x1, h2, u0 = _out_up_proj(mix, x, w_out, g_mix_post, g_ffn_pre, w_up)
    y, dy2, dout, du, conv_acc, tail_acc = _ffn_tail(u0, x1, target, conv_w, conv_b, w_down, g_ffn_post)
    du0, dx1, dmixed, dmix, head_acc = _ffn_head_bwd(du, conv_w, w_up, x1, g_ffn_pre, dout, mixed, g_mix_post, w_out)

    d_w_down = _weight_grad(y, dy2, 512, "grad_w_down")
    d_w_up = _weight_grad(h2, du0, UP_W // N_CHIPS, "grad_w_up", by_block=True)
    d_w_out = _weight_grad(mix, dmixed, D_MODEL, "grad_w_out")
    early = [by_half(d_w_down, D_FF), by_half(d_w_up, N_CHIPS * D_MODEL), by_half(d_w_out, D_MODEL)]
    (dproj, dsinks), early_lands = _mixer_bwd(proj, dmix, states, sinks, sin, cos, consts,
                                              _hosted_scatter(early) if distributed else None)
    d_w_in = _columns_to_shards(_weight_grad(h1, dproj, IN_W // 2, "grad_w_in"))
    late = [by_half(d_w_in, N_CHIPS * D_MODEL)]
    (grad_x, in_acc), late_lands = _in_proj_bwd(dproj, w_in, x, g_mix_pre, dx1, _hosted_scatter(late) if distributed else None)

    small = _pack_small(in_acc[0], head_acc[1], head_acc[0], tail_acc[0], dsinks[0, :N_ATTN_HEADS], conv_acc[3],
                        jnp.sum(tail_acc[1]))
    d_conv = jnp.pad(conv_acc[0:CONV_WIDTH].reshape(-1), (0, CONV_FULL_ROWS * D_MODEL - CONV_WIDTH * UP_W))
    small = jnp.concatenate([small, d_conv.reshape(CONV_FULL_ROWS, D_MODEL)], axis=0)
    grads = dict(w_down=early[0], w_up=early[1], w_out=early[2], w_in=late[0])
    lands = dict(zip(["w_down", "w_up", "w_out", "w_in"], early_lands + late_lands))
    return grad_x, grads, lands, small


def kernel(x, mix_pre_norm, w_in, attn_sinks, w_out, mix_post_norm, ffn_pre_norm, w_up, conv_w, conv_b, w_down, ffn_post_norm, loss_target, m_mix_pre_norm, m_w_in, m_attn_sinks, m_w_out, m_mix_post_norm, m_ffn_pre_norm, m_w_up, m_conv_w, m_conv_b, m_w_down, m_ffn_post_norm, v_mix_pre_norm, v_w_in, v_attn_sinks, v_w_out, v_mix_post_norm, v_ffn_pre_norm, v_w_up, v_conv_w, v_conv_b, v_w_down, v_ffn_post_norm):
    cx, cy, cc = _place()
    shard = 2 * cx + cy

    w_in_all = _gather_now(w_in[0].astype(BF16), "gather_w_in").reshape(N_CHIPS, D_MODEL, IN_W // N_CHIPS)
    conv_all = _gather_small(_pack_conv(conv_w[0]), "gather_conv_w")
    conv_full = jnp.concatenate([_unpack_conv(conv_all[2 * k]) for k in range(N_CHIPS)], axis=1)

    grad_x, grads, lands, small = _local_step(
        x[0], loss_target[0], mix_pre_norm, _shards_to_columns(w_in_all), attn_sinks.reshape(-1), w_out[0].astype(BF16),
        mix_post_norm, ffn_pre_norm, w_up[0].astype(BF16), conv_full, conv_b, w_down[0].astype(BF16), ffn_post_norm)

    place = jnp.stack([shard, cc]).astype(jnp.int32)
    mats = ["w_in", "w_out", "w_up", "w_down"]
    halves = [_sum_pieces(grads[n], lands[n], place, "sum_grad_" + n) for n in mats]
    weights = dict(w_in=(w_in, m_w_in, v_w_in), w_out=(w_out, m_w_out, v_w_out), w_up=(w_up, m_w_up, v_w_up),
                   w_down=(w_down, m_w_down, v_w_down))
    mat_out = {}
    for n, joined in zip(mats, _join_halves(halves)):
        w, m, v = weights[n]
        g = joined.reshape(w.shape[1:])
        mat_out[n] = (g,) + tuple(_adamw(w[0], g, m[0], v[0], "adamw_" + n))

    small_sum = _sum_devices(_gather_small(small, "gather_small_grads"))
    d_conv_full = small_sum[SMALL_ROWS:].reshape(-1)[:CONV_WIDTH * UP_W].reshape(CONV_WIDTH, UP_W)
    d_conv_mine = lax.dynamic_slice_in_dim(d_conv_full, shard * (UP_W // N_CHIPS), UP_W // N_CHIPS, axis=1)
    g_s = jnp.concatenate([small_sum[:SMALL_ROWS], _pack_conv(d_conv_mine)], axis=0)
    zero = jnp.zeros((), F32)
    pack_rep = lambda a, b, c_, d, e, f, cw: jnp.concatenate([_pack_small(a, b, c_, d, e, f, zero), _pack_conv(cw[0])], axis=0)
    w_s = pack_rep(mix_pre_norm, mix_post_norm, ffn_pre_norm, ffn_post_norm, attn_sinks, conv_b, conv_w)
    m_s = pack_rep(m_mix_pre_norm, m_mix_post_norm, m_ffn_pre_norm, m_ffn_post_norm, m_attn_sinks, m_conv_b, m_conv_w)
    v_s = pack_rep(v_mix_pre_norm, v_mix_post_norm, v_ffn_pre_norm, v_ffn_post_norm, v_attn_sinks, v_conv_b, v_conv_w)
    delta_s, new_m_s, new_v_s = _adamw(w_s, g_s, m_s, v_s, "adamw_small")

    names = ["mix_pre_norm", "w_in", "attn_sinks", "w_out", "mix_post_norm", "ffn_pre_norm", "w_up", "conv_w", "conv_b",
             "w_down", "ffn_post_norm"]

    def leaves(which, packed_small):
        smalls = _unpack_small(packed_small)
        return [mat_out[n][which][None] if n in mat_out else (smalls[n][None] if n == "conv_w" else smalls[n]) for n in names]

    loss = _unpack_small(g_s)["loss"]
    return (loss, grad_x[None], *leaves(0, g_s), *leaves(1, delta_s), *leaves(2, new_m_s), *leaves(3, new_v_s))
```

```python
import functools
import math

import jax
import jax.numpy as jnp
from jax import lax
from jax.experimental import pallas as pl
from jax.experimental.pallas import tpu as pltpu

F32 = jnp.float32
BF16 = jnp.bfloat16

D_MODEL = 1024
HEAD_DIM = 64
ATTN_W = 512
N_ATTN_HEADS = 8
KV_W = 128
RET_W = 512
N_RET_HEADS = 4
RET_HEAD_DIM = 128
CHUNK = 128
IN_W = 2816
D_FF = 2816
UP_W = 2 * D_FF
CONV_WIDTH = 3
RMS_EPS = 1e-6
GN_EPS = 1e-6
MASK_VALUE = -1e30
ATTN_SCALE = HEAD_DIM ** -0.5
RET_K_SCALE = RET_HEAD_DIM ** -0.5
GELU_C = math.sqrt(2.0 / math.pi)
GELU_A = 0.044715

ADAM_LR = 0.001
ADAM_B1 = 0.9
ADAM_B2 = 0.999
ADAM_EPS = 1e-08
ADAM_WD = 0.01
ADAM_STEP = 10

N_CHIPS = 4
N_DEV = 8
MESH = pl.DeviceIdType.MESH
VMEM_LIMIT_V7X = 56 * 1024 * 1024
TOKEN_TILE = 256
BIG_TOKEN_TILE = 512
WEIGHT_GRAD_TOKENS = 1024
FFN_ROW_BLOCK = 64
HEAD_BWD_COLS = 512
MIXER_CHUNKS_PER_STEP = 4

Q_A0, KV_A0, Q_R0, K_R0, V_R0, G_R0 = 0, 512, 768, 1280, 1792, 2304

ROWS_W_IN, ROWS_W_OUT, ROWS_W_UP, ROWS_W_DOWN = 704, 256, 1408, 704
ROWS_PACK = ROWS_W_IN + ROWS_W_OUT + ROWS_W_UP + ROWS_W_DOWN
ROWS_CONV = 8
SMALL_ROWS = 16
CONV_FULL_ROWS = 24


def _params(sem=None, **kw):
    if sem is not None:
        kw["dimension_semantics"] = sem
    return pltpu.CompilerParams(vmem_limit_bytes=VMEM_LIMIT_V7X, **kw)


def _resident(shape):
    zeros = (0,) * len(shape)
    return pl.BlockSpec(shape, lambda *_: zeros, pipeline_mode=pl.Buffered(1))


class _Hosted:
    def __init__(self, ins, outs, aliases, n_pairs, n_local, start, finish):
        self.ins, self.outs, self.aliases = list(ins), list(outs), dict(aliases)
        self.n_pairs, self.n_local, self.start, self.finish = n_pairs, max(n_local, 1), start, finish


def _hosted_call(compute, *, name, grid, in_specs, out_specs, out_shape, scratch_shapes, args, hosted=None):
    params = _params(("arbitrary",))
    if hosted is None:
        res = pl.pallas_call(compute, name=name, grid=grid, in_specs=in_specs, out_specs=out_specs, out_shape=out_shape,
                             scratch_shapes=scratch_shapes, compiler_params=params)(*args)
        return list(res), []
    n_in, n_out, n_scr = len(in_specs), len(out_specs), len(scratch_shapes)
    h_in, h_out = len(hosted.ins), len(hosted.outs)
    last = grid[0] - 1

    def body(*refs):
        ins, refs = refs[:n_in], refs[n_in:]
        h_ins, refs = refs[:h_in], refs[h_in:]
        outs, refs = refs[:n_out], refs[n_out:]
        h_outs, refs = refs[:h_out], refs[h_out:]
        scr, sems = refs[:n_scr], refs[n_scr:]

        @pl.when(pl.program_id(0) == 0)
        def _():
            hosted.start(h_ins, h_outs, *sems)

        compute(*ins, *outs, *scr)

        @pl.when(pl.program_id(0) == last)
        def _():
            hosted.finish(h_ins, h_outs, *sems)

    hbm = pl.BlockSpec(memory_space=pl.ANY)
    res = pl.pallas_call(
        body, name=name, grid=grid,
        in_specs=list(in_specs) + [hbm] * h_in, out_specs=list(out_specs) + [hbm] * h_out,
        out_shape=list(out_shape) + hosted.outs,
        scratch_shapes=list(scratch_shapes) + [pltpu.SemaphoreType.DMA((hosted.n_pairs,)), pltpu.SemaphoreType.DMA((hosted.n_pairs,)),
                                               pltpu.SemaphoreType.DMA((hosted.n_local,))],
        input_output_aliases={n_in + a: n_out + b for a, b in hosted.aliases.items()},
        compiler_params=params,
    )(*args, *hosted.ins)
    return list(res[:n_out]), list(res[n_out:])


def _dot(a, b):
    return jnp.dot(a, b, preferred_element_type=F32)


def _dot_nt(a, b):
    return lax.dot_general(a, b, (((1,), (1,)), ((), ())), preferred_element_type=F32)


def _dot_tn(a, b):
    return lax.dot_general(a, b, (((0,), (0,)), ((), ())), preferred_element_type=F32)


def _shift_matrix(n, by):
    row = lax.broadcasted_iota(jnp.int32, (n, n), 0)
    col = lax.broadcasted_iota(jnp.int32, (n, n), 1)
    return jnp.where(col == row + by, 1.0, 0.0).astype(BF16)


def _rstd(v):
    return lax.rsqrt(jnp.mean(v * v, axis=-1, keepdims=True) + RMS_EPS)


def _rms_bwd(dy, v, rstd, gain):
    n = v * rstd
    dgain = jnp.sum(dy * n, axis=0, keepdims=True)
    dn = dy * gain
    dv = rstd * (dn - n * jnp.mean(dn * n, axis=-1, keepdims=True))
    return dv, dgain


def _lane_lo(shape):
    return (lax.broadcasted_iota(jnp.int32, shape, 1) % 128) < HEAD_DIM


GROUP = N_ATTN_HEADS // (KV_W // HEAD_DIM)


def _attn_valid(chunk_index):
    qi = lax.broadcasted_iota(jnp.int32, (GROUP * CHUNK, 2 * CHUNK), 0) % CHUNK
    kj = lax.broadcasted_iota(jnp.int32, (GROUP * CHUNK, 2 * CHUNK), 1)
    first_key = jnp.where(chunk_index > 0, 0, CHUNK)
    return jnp.logical_and(jnp.logical_and(kj > qi, kj >= first_key), kj <= qi + CHUNK)


def _half(shape, hk):
    lo = _lane_lo(shape)
    return lo if hk == 0 else jnp.logical_not(lo)


def _stack_heads(ref, row0, col0, hk):
    half = _half((CHUNK, 128), hk)
    parts = []
    for j in range(GROUP):
        h = GROUP * hk + j
        pair = ref[row0:row0 + CHUNK, col0 + (h // 2) * 128:col0 + (h // 2 + 1) * 128].astype(F32)
        if h % 2 != hk:
            pair = pltpu.roll(pair, HEAD_DIM, 1)
        parts.append(jnp.where(half, pair, 0.0))
    return jnp.concatenate(parts, axis=0)


def _unstack_heads(stacked, hk):
    pairs = []
    for q in range(GROUP // 2):
        even, odd = stacked[2 * q * CHUNK:(2 * q + 1) * CHUNK], stacked[(2 * q + 1) * CHUNK:(2 * q + 2) * CHUNK]
        pairs.append(even + pltpu.roll(odd, HEAD_DIM, 1) if hk == 0 else pltpu.roll(even, HEAD_DIM, 1) + odd)
    return pairs


def _group_sinks(sk_ref, hk):
    row = lax.broadcasted_iota(jnp.int32, (GROUP * CHUNK, 1), 0)
    col = jnp.full((GROUP * CHUNK, 1), sk_ref[GROUP * hk], F32)
    for j in range(1, GROUP):
        col = jnp.where(row >= j * CHUNK, sk_ref[GROUP * hk + j], col)
    return col


def _attn_probs(q_b, kk_b, valid, sink):
    s = _dot_nt(q_b, kk_b) * ATTN_SCALE
    s = jnp.where(valid, s, MASK_VALUE)
    m = jnp.maximum(jnp.max(s, axis=-1, keepdims=True), sink)
    e = jnp.exp(s - m)
    e_sink = jnp.exp(sink - m)
    inv = 1.0 / (jnp.sum(e, axis=-1, keepdims=True) + e_sink)
    return e * inv, e_sink * inv


def _rot2(v):
    w = v.shape[1]
    even = (lax.broadcasted_iota(jnp.int32, v.shape, 1) % 2) == 0
    return jnp.where(even, -pltpu.roll(v, w - 1, 1), pltpu.roll(v, 1, 1))


def _tile4(v):
    return jnp.concatenate([v, v, v, v], axis=-1)


def _sigmoid(v):
    return 1.0 / (1.0 + jnp.exp(-v))


def _ret_constants():
    h = N_RET_HEADS
    log_gamma = jnp.log(1.0 - jnp.power(2.0, -5.0 - jnp.arange(h, dtype=F32)))
    idx = jnp.arange(CHUNK, dtype=F32)
    rel = idx[:, None] - idx[None, :]
    d_intra = jnp.where(rel[None] >= 0, jnp.exp(log_gamma[:, None, None] * jnp.maximum(rel, 0.0)[None]), 0.0)
    xi = jnp.exp(log_gamma[None, :] * (idx[:, None] + 1.0))
    zeta = jnp.exp(log_gamma[None, :] * (CHUNK - 1.0 - idx[:, None]))
    decay = jnp.exp(log_gamma * CHUNK)
    xi_full = jnp.repeat(xi, RET_HEAD_DIM, axis=1)
    zeta_full = jnp.repeat(zeta, RET_HEAD_DIM, axis=1)
    decay_full = jnp.broadcast_to(jnp.repeat(decay, RET_HEAD_DIM)[None, :], (8, RET_W))
    return d_intra.astype(F32), xi_full.astype(F32), zeta_full.astype(F32), decay_full.astype(F32)


def _rope_tables(s):
    pos = jnp.arange(s, dtype=F32)
    angle = 1.0 / jnp.power(10000.0, jnp.linspace(0.0, 1.0, RET_HEAD_DIM // 2, dtype=F32))
    angle = jnp.repeat(angle, 2)
    return jnp.sin(pos[:, None] * angle[None]), jnp.cos(pos[:, None] * angle[None])


def _in_proj(x, gain, w_in, hosted=None):
    s = x.shape[0]
    tm = min(BIG_TOKEN_TILE, s)

    def body(x_ref, g_ref, w_ref, h_ref, p_ref):
        xv = x_ref[...]
        h = (xv * _rstd(xv) * g_ref[...]).astype(BF16)
        h_ref[...] = h
        p_ref[...] = _dot(h, w_ref[...])

    return _hosted_call(
        body, name="in_proj", grid=(s // tm,),
        in_specs=[pl.BlockSpec((tm, D_MODEL), lambda i: (i, 0)), _resident((1, D_MODEL)), _resident((D_MODEL, IN_W))],
        out_specs=[pl.BlockSpec((tm, D_MODEL), lambda i: (i, 0)), pl.BlockSpec((tm, IN_W), lambda i: (i, 0))],
        out_shape=[jax.ShapeDtypeStruct((s, D_MODEL), BF16), jax.ShapeDtypeStruct((s, IN_W), F32)],
        scratch_shapes=[], args=(x, gain, w_in), hosted=hosted)


def _mixer_fwd(proj, sinks, sin, cos, consts, hosted=None):
    s = proj.shape[0]
    nc = s // CHUNK
    cps = MIXER_CHUNKS_PER_STEP
    d_intra, xi_full, zeta_full, decay_full = consts

    def body(sk_ref, p_ref, pkv_ref, sin_ref, cos_ref, dm_ref, xi_ref, ze_ref, dc_ref, mix_ref, st_ref, state):
        i = pl.program_id(0)

        @pl.when(i == 0)
        def _():
            state[...] = jnp.zeros_like(state)

        st = [state[h] for h in range(N_RET_HEADS)]
        for c in range(cps):
            r0 = c * CHUNK
            rows = slice(r0, r0 + CHUNK)

            kv_cur = p_ref[rows, KV_A0:KV_A0 + 2 * KV_W]
            kv_prev = pkv_ref[...] if c == 0 else p_ref[r0 - CHUNK:r0, KV_A0:KV_A0 + 2 * KV_W]
            kk = jnp.concatenate([kv_prev[:, :KV_W], kv_cur[:, :KV_W]], axis=0)
            vv = jnp.concatenate([kv_prev[:, KV_W:], kv_cur[:, KV_W:]], axis=0)
            kk_b = kk.astype(BF16)
            valid = _attn_valid(cps * i + c)
            for hk in range(KV_W // HEAD_DIM):
                q_b = _stack_heads(p_ref, r0, Q_A0, hk).astype(BF16)
                p, _ = _attn_probs(q_b, kk_b, valid, _group_sinks(sk_ref, hk))
                v_b = jnp.where(_half((2 * CHUNK, 128), hk), vv, 0.0).astype(BF16)
                for q, pair in enumerate(_unstack_heads(_dot(p.astype(BF16), v_b), hk)):
                    pi = (GROUP // 2) * hk + q
                    mix_ref[rows, pi * 128:(pi + 1) * 128] = pair.astype(BF16)

            sin4, cos4 = _tile4(sin_ref[rows, :]), _tile4(cos_ref[rows, :])
            q_r = p_ref[rows, Q_R0:Q_R0 + RET_W]
            k_r = p_ref[rows, K_R0:K_R0 + RET_W] * RET_K_SCALE
            q_r = q_r * cos4 + _rot2(q_r) * sin4
            k_r = k_r * cos4 + _rot2(k_r) * sin4
            kz = k_r * ze_ref[...]
            for h in range(N_RET_HEADS):
                sl = slice(h * RET_HEAD_DIM, (h + 1) * RET_HEAD_DIM)
                qh, kh = q_r[:, sl].astype(BF16), k_r[:, sl].astype(BF16)
                vh = p_ref[rows, V_R0 + h * RET_HEAD_DIM:V_R0 + (h + 1) * RET_HEAD_DIM].astype(BF16)
                st_ref[c, h] = st[h]
                a = _dot_nt(qh, kh) * dm_ref[h]
                o = _dot(a.astype(BF16), vh) + _dot(qh, st[h].astype(BF16)) * xi_ref[:, sl]
                st[h] = dc_ref[0:1, sl] * st[h] + _dot_tn(kz[:, sl].astype(BF16), vh)
                mu = jnp.mean(o, axis=-1, keepdims=True)
                oc = o - mu
                on = oc * lax.rsqrt(jnp.mean(oc * oc, axis=-1, keepdims=True) + GN_EPS)
                g = p_ref[rows, G_R0 + h * RET_HEAD_DIM:G_R0 + (h + 1) * RET_HEAD_DIM]
                mix_ref[rows, ATTN_W + h * RET_HEAD_DIM:ATTN_W + (h + 1) * RET_HEAD_DIM] = (g * _sigmoid(g) * on).astype(BF16)
        for h in range(N_RET_HEADS):
            state[h] = st[h]

    return _hosted_call(
        body, name="mixer_fwd", grid=(nc // cps,),
        in_specs=[
            pl.BlockSpec(memory_space=pltpu.SMEM),
            pl.BlockSpec((cps * CHUNK, IN_W), lambda i: (i, 0)),
            pl.BlockSpec((CHUNK, 2 * KV_W), lambda i: (jnp.maximum(cps * i - 1, 0), KV_A0 // (2 * KV_W))),
            pl.BlockSpec((cps * CHUNK, RET_HEAD_DIM), lambda i: (i, 0)),
            pl.BlockSpec((cps * CHUNK, RET_HEAD_DIM), lambda i: (i, 0)),
            _resident((N_RET_HEADS, CHUNK, CHUNK)), _resident((CHUNK, RET_W)), _resident((CHUNK, RET_W)), _resident((8, RET_W)),
        ],
        out_specs=[
            pl.BlockSpec((cps * CHUNK, D_MODEL), lambda i: (i, 0)),
            pl.BlockSpec((cps, N_RET_HEADS, RET_HEAD_DIM, RET_HEAD_DIM), lambda i: (i, 0, 0, 0)),
        ],
        out_shape=[jax.ShapeDtypeStruct((s, D_MODEL), BF16),
                   jax.ShapeDtypeStruct((nc, N_RET_HEADS, RET_HEAD_DIM, RET_HEAD_DIM), F32)],
        scratch_shapes=[pltpu.VMEM((N_RET_HEADS, RET_HEAD_DIM, RET_HEAD_DIM), F32)],
        args=(sinks, proj, proj, sin, cos, d_intra, xi_full, zeta_full, decay_full), hosted=hosted)


def _out_up_proj(mix, x, w_out, g_post, g_pre, w_up):
    s = x.shape[0]
    tm = min(BIG_TOKEN_TILE, s)
    blk = UP_W // N_CHIPS

    def body(mix_ref, x_ref, wo_ref, g2_ref, g3_ref, wu_ref, mixed_ref, x1_ref, h2_ref, u0_ref):
        mixed = _dot(mix_ref[...], wo_ref[...])
        mixed_ref[...] = mixed
        x1 = x_ref[...] + mixed * _rstd(mixed) * g2_ref[...]
        x1_ref[...] = x1
        h2 = (x1 * _rstd(x1) * g3_ref[...]).astype(BF16)
        h2_ref[...] = h2
        for k in range(N_CHIPS):
            u0_ref[:, k * blk:(k + 1) * blk] = _dot(h2, wu_ref[k]).astype(BF16)

    tok = lambda w: pl.BlockSpec((tm, w), lambda i: (i, 0))
    return pl.pallas_call(
        body, name="out_up_proj", grid=(s // tm,),
        in_specs=[tok(D_MODEL), tok(D_MODEL), _resident((D_MODEL, D_MODEL)), _resident((1, D_MODEL)), _resident((1, D_MODEL)),
                  _resident((N_CHIPS, D_MODEL, blk))],
        out_specs=[tok(D_MODEL), tok(D_MODEL), tok(D_MODEL), tok(UP_W)],
        out_shape=[jax.ShapeDtypeStruct((s, D_MODEL), F32), jax.ShapeDtypeStruct((s, D_MODEL), F32),
                   jax.ShapeDtypeStruct((s, D_MODEL), BF16), jax.ShapeDtypeStruct((s, UP_W), BF16)],
        compiler_params=_params(("arbitrary",)),
    )(mix, x, w_out, g_post, g_pre, w_up)


def _ffn_tail(u0, x1, target, conv_w, conv_b, w_down, g_post):
    s = x1.shape[0]
    tm = TOKEN_TILE
    last = s // tm - 1
    rb, lanes = FFN_ROW_BLOCK, 128

    def body(u0_ref, x1_ref, t_ref, cw_ref, cb_ref, wd_ref, g_ref,
             y_ref, dy2_ref, dout_ref, du_ref, cacc_ref, gacc_ref, u1_s, u2_s, carry, gelu_s, slope_s, dy_s, cacc):
        i = pl.program_id(0)

        @pl.when(i == 0)
        def _():
            carry[...] = jnp.zeros_like(carry)
            cacc[...] = jnp.zeros_like(cacc)
            gacc_ref[...] = jnp.zeros_like(gacc_ref)

        shift1, shift2 = _shift_matrix(tm, -1), _shift_matrix(tm, -2)
        r8 = lax.broadcasted_iota(jnp.int32, (8, 1), 0)
        wide = 2 * lanes

        def shift_block(col):
            cols = slice(col, col + wide)
            u1_s[:, cols] = _dot(shift1, u0_ref[:, cols])
            u2_s[:, cols] = _dot(shift2, u0_ref[:, cols])
            c14, c15 = carry[14:15, cols], carry[15:16, cols]
            u1_s[0:8, cols] = jnp.where(r8 == 0, c15, u1_s[0:8, cols])
            u2_s[0:8, cols] = jnp.where(r8 == 0, c14, jnp.where(r8 == 1, c15, u2_s[0:8, cols]))

        def taps(col):
            return (cw_ref[0:1, col:col + lanes], cw_ref[1:2, col:col + lanes], cw_ref[2:3, col:col + lanes],
                    cb_ref[0:1, col:col + lanes])

        def shifted(r0, col):
            return (u2_s[r0:r0 + rb, col:col + lanes], u1_s[r0:r0 + rb, col:col + lanes],
                    u0_ref[r0:r0 + rb, col:col + lanes].astype(F32))

        def conv(r0, col, w):
            u2, u1, uc = shifted(r0, col)
            return w[0] * u2 + w[1] * u1 + w[2] * uc + w[3]

        fold = lambda v: jnp.sum(v.reshape(rb // 8, 8, lanes), axis=0)

        for j in range(D_FF // lanes):
            cg, cv = j * lanes, D_FF + j * lanes
            if cg % wide == 0:
                shift_block(cg)
                shift_block(cv)
            wg, wv = taps(cg), taps(cv)
            for r0 in range(0, tm, rb):
                gate, val = conv(r0, cg, wg), conv(r0, cv, wv)
                g2 = gate * gate
                th = jnp.tanh(gate * (GELU_C + GELU_C * GELU_A * g2))
                hp = 0.5 * th + 0.5
                gelu = gate * hp
                dgelu = hp + gate * (1.0 - th * th) * (0.5 * GELU_C + 1.5 * GELU_C * GELU_A * g2)
                y_ref[r0:r0 + rb, cg:cg + lanes] = (gelu * val).astype(BF16)
                gelu_s[r0:r0 + rb, cg:cg + lanes] = gelu
                slope_s[r0:r0 + rb, cg:cg + lanes] = dgelu * val

        y2 = _dot(y_ref[...], wd_ref[...])
        r4 = _rstd(y2)
        gain = g_ref[...]
        out = x1_ref[...] + y2 * r4 * gain
        diff = out - t_ref[...]
        dout = diff * (1.0 / D_MODEL)
        dout_ref[...] = dout
        dy2, dgain = _rms_bwd(dout, y2, r4, gain)
        dy2_b = dy2.astype(BF16)
        dy2_ref[...] = dy2_b
        gacc_ref[0:1, :] += dgain
        gacc_ref[1:2, :] += 0.5 * jnp.sum(diff * dout, axis=0, keepdims=True)
        carry[...] = u0_ref[tm - 16:tm, :].astype(F32)

        for j in range(D_FF // lanes):
            cg, cv = j * lanes, D_FF + j * lanes
            if cg % wide == 0:
                dy_s[:, cg:cg + wide] = _dot_nt(dy2_b, wd_ref[cg:cg + wide, :])
            acc = [[jnp.zeros((8, lanes), F32) for _ in range(CONV_WIDTH + 1)] for _ in range(2)]
            for r0 in range(0, tm, rb):
                dy = dy_s[r0:r0 + rb, cg:cg + lanes]
                d_gate = dy * slope_s[r0:r0 + rb, cg:cg + lanes]
                d_val = dy * gelu_s[r0:r0 + rb, cg:cg + lanes]
                for side, (col, d) in enumerate(((cg, d_gate), (cv, d_val))):
                    du_ref[r0:r0 + rb, col:col + lanes] = d.astype(BF16)
                    for k, u in enumerate(shifted(r0, col)):
                        acc[side][k] = acc[side][k] + fold(d * u)
                    acc[side][CONV_WIDTH] = acc[side][CONV_WIDTH] + fold(d)
            for side, col in enumerate((cg, cv)):
                for k in range(CONV_WIDTH + 1):
                    cacc[8 * k:8 * k + 8, col:col + lanes] += acc[side][k]

        @pl.when(i == last)
        def _():
            for k in range(CONV_WIDTH + 1):
                cacc_ref[k:k + 1, :] = jnp.sum(cacc[8 * k:8 * k + 8, :], axis=0, keepdims=True)
            cacc_ref[CONV_WIDTH + 1:8, :] = jnp.zeros((8 - CONV_WIDTH - 1, UP_W), F32)

    tok = lambda w: pl.BlockSpec((tm, w), lambda i: (i, 0))
    return pl.pallas_call(
        body, name="ffn_tail", grid=(s // tm,),
        in_specs=[tok(UP_W), tok(D_MODEL), tok(D_MODEL), _resident((CONV_WIDTH, UP_W)), _resident((1, UP_W)),
                  _resident((D_FF, D_MODEL)), _resident((1, D_MODEL))],
        out_specs=[tok(D_FF), tok(D_MODEL), tok(D_MODEL), tok(UP_W),
                   pl.BlockSpec((8, UP_W), lambda i: (0, 0)), pl.BlockSpec((8, D_MODEL), lambda i: (0, 0))],
        out_shape=[jax.ShapeDtypeStruct((s, D_FF), BF16), jax.ShapeDtypeStruct((s, D_MODEL), BF16),
                   jax.ShapeDtypeStruct((s, D_MODEL), F32), jax.ShapeDtypeStruct((s, UP_W), BF16),
                   jax.ShapeDtypeStruct((8, UP_W), F32), jax.ShapeDtypeStruct((8, D_MODEL), F32)],
        scratch_shapes=[pltpu.VMEM((tm, UP_W), F32), pltpu.VMEM((tm, UP_W), F32), pltpu.VMEM((16, UP_W), F32),
                        pltpu.VMEM((tm, D_FF), F32), pltpu.VMEM((tm, D_FF), F32),
                        pltpu.VMEM((tm, D_FF), F32), pltpu.VMEM((8 * (CONV_WIDTH + 1), UP_W), F32)],
        compiler_params=_params(("arbitrary",)),
    )(u0, x1, target, conv_w, conv_b, w_down, g_post)


def _ffn_head_bwd(du, conv_w, w_up, x1, g_pre, dout, mixed, g_post, w_out):
    s = x1.shape[0]
    tm = TOKEN_TILE
    nt = s // tm
    blk = UP_W // N_CHIPS

    def body(du_ref, halo_ref, cw_ref, wu_ref, x1_ref, g3_ref, dout_ref, mixed_ref, g2_ref, wo_ref,
             du0_ref, dx1_ref, dmixed_ref, dmix_ref, gacc_ref, dbuf):
        i = pl.program_id(0)

        @pl.when(i == 0)
        def _():
            gacc_ref[...] = jnp.zeros_like(gacc_ref)

        dbuf[0:tm, :] = du_ref[...].astype(F32)
        dbuf[tm:tm + 16, :] = jnp.where(i < nt - 1, halo_ref[...].astype(F32), 0.0)
        dh2 = jnp.zeros((tm, D_MODEL), F32)
        for k in range(N_CHIPS):
            for c0 in range(0, blk, HEAD_BWD_COLS):
                width = min(HEAD_BWD_COLS, blk - c0)
                cols = slice(k * blk + c0, k * blk + c0 + width)
                du0_b = (cw_ref[2:3, cols] * dbuf[0:tm, cols] + cw_ref[1:2, cols] * dbuf[1:1 + tm, cols]
                         + cw_ref[0:1, cols] * dbuf[2:2 + tm, cols]).astype(BF16)
                du0_ref[:, cols] = du0_b
                dh2 = dh2 + _dot_nt(du0_b, wu_ref[k, :, c0:c0 + width])
        x1 = x1_ref[...]
        d3, dg3 = _rms_bwd(dh2, x1, _rstd(x1), g3_ref[...])
        dx1 = dout_ref[...] + d3
        dx1_ref[...] = dx1
        mixed = mixed_ref[...]
        dmixed, dg2 = _rms_bwd(dx1, mixed, _rstd(mixed), g2_ref[...])
        dmixed_b = dmixed.astype(BF16)
        dmixed_ref[...] = dmixed_b
        dmix_ref[...] = _dot_nt(dmixed_b, wo_ref[...]).astype(BF16)
        gacc_ref[0:1, :] += dg3
        gacc_ref[1:2, :] += dg2

    tok = lambda w: pl.BlockSpec((tm, w), lambda i: (i, 0))
    halo = pl.BlockSpec((16, UP_W), lambda i: (jnp.minimum(i + 1, nt - 1) * (tm // 16), 0))
    return pl.pallas_call(
        body, name="ffn_head_bwd", grid=(nt,),
        in_specs=[tok(UP_W), halo, _resident((CONV_WIDTH, UP_W)), _resident((N_CHIPS, D_MODEL, blk)), tok(D_MODEL),
                  _resident((1, D_MODEL)), tok(D_MODEL), tok(D_MODEL), _resident((1, D_MODEL)), _resident((D_MODEL, D_MODEL))],
        out_specs=[tok(UP_W), tok(D_MODEL), tok(D_MODEL), tok(D_MODEL), pl.BlockSpec((8, D_MODEL), lambda i: (0, 0))],
        out_shape=[jax.ShapeDtypeStruct((s, UP_W), BF16), jax.ShapeDtypeStruct((s, D_MODEL), F32),
                   jax.ShapeDtypeStruct((s, D_MODEL), BF16), jax.ShapeDtypeStruct((s, D_MODEL), BF16),
                   jax.ShapeDtypeStruct((8, D_MODEL), F32)],
        scratch_shapes=[pltpu.VMEM((tm + 16, UP_W), F32)],
        compiler_params=_params(("arbitrary",)),
    )(du, du, conv_w, w_up, x1, g_pre, dout, mixed, g_post, w_out)


def _mixer_bwd(proj, dmix, states, sinks, sin, cos, consts, hosted=None):
    s = proj.shape[0]
    nc = s // CHUNK
    cps = MIXER_CHUNKS_PER_STEP
    nb = nc // cps
    d_intra, xi_full, zeta_full, decay_full = consts

    def body(sk_ref, p_ref, pkv_ref, dmix_ref, st_ref, sin_ref, cos_ref, dm_ref, xi_ref, ze_ref, dc_ref,
             dp_ref, dsk_ref, gstate, ckv, dsk_acc):
        i = pl.program_id(0)
        block = nb - 1 - i

        @pl.when(i == 0)
        def _():
            gstate[...] = jnp.zeros_like(gstate)
            ckv[...] = jnp.zeros_like(ckv)
            dsk_acc[...] = jnp.zeros_like(dsk_acc)

        gs_all = [gstate[h] for h in range(N_RET_HEADS)]
        later_kv = ckv[...]
        lane = lax.broadcasted_iota(jnp.int32, (CHUNK, 128), 1)
        dsk = jnp.zeros((CHUNK, 128), F32)
        for c in reversed(range(cps)):
            r0 = c * CHUNK
            rows = slice(r0, r0 + CHUNK)

            kv_cur = p_ref[rows, KV_A0:KV_A0 + 2 * KV_W]
            kv_prev = pkv_ref[...] if c == 0 else p_ref[r0 - CHUNK:r0, KV_A0:KV_A0 + 2 * KV_W]
            kk = jnp.concatenate([kv_prev[:, :KV_W], kv_cur[:, :KV_W]], axis=0)
            vv = jnp.concatenate([kv_prev[:, KV_W:], kv_cur[:, KV_W:]], axis=0)
            kk_b, vv_b = kk.astype(BF16), vv.astype(BF16)
            valid = _attn_valid(cps * block + c)
            dkk = jnp.zeros((2 * CHUNK, KV_W), F32)
            dvv = jnp.zeros((2 * CHUNK, KV_W), F32)
            for hk in range(KV_W // HEAD_DIM):
                q_b = _stack_heads(p_ref, r0, Q_A0, hk).astype(BF16)
                do_b = _stack_heads(dmix_ref, r0, 0, hk).astype(BF16)
                p, p_sink = _attn_probs(q_b, kk_b, valid, _group_sinks(sk_ref, hk))
                dpr = _dot_nt(do_b, vv_b)
                delta = jnp.sum(p * dpr, axis=-1, keepdims=True)
                ds_b = (p * (dpr - delta) * ATTN_SCALE).astype(BF16)
                dsink = -p_sink * delta
                for j in range(GROUP):
                    dsk = dsk + jnp.where(lane == GROUP * hk + j, dsink[j * CHUNK:(j + 1) * CHUNK], 0.0)
                k_b = jnp.where(_half((2 * CHUNK, 128), hk), kk, 0.0).astype(BF16)
                for q, pair in enumerate(_unstack_heads(_dot(ds_b, k_b), hk)):
                    pi = (GROUP // 2) * hk + q
                    dp_ref[rows, Q_A0 + pi * 128:Q_A0 + (pi + 1) * 128] = pair.astype(BF16)
                dkk = dkk + _dot_tn(ds_b, q_b)
                dvv = dvv + _dot_tn(p.astype(BF16), do_b)
            dp_ref[rows, KV_A0:KV_A0 + KV_W] = (dkk[CHUNK:] + later_kv[:, :KV_W]).astype(BF16)
            dp_ref[rows, KV_A0 + KV_W:KV_A0 + 2 * KV_W] = (dvv[CHUNK:] + later_kv[:, KV_W:]).astype(BF16)
            later_kv = jnp.concatenate([dkk[:CHUNK], dvv[:CHUNK]], axis=1)

            sin4, cos4 = _tile4(sin_ref[rows, :]), _tile4(cos_ref[rows, :])
            q_r = p_ref[rows, Q_R0:Q_R0 + RET_W]
            k_r = p_ref[rows, K_R0:K_R0 + RET_W] * RET_K_SCALE
            q_r = q_r * cos4 + _rot2(q_r) * sin4
            k_r = k_r * cos4 + _rot2(k_r) * sin4
            kz = k_r * ze_ref[...]
            dq_parts, dk_parts = [], []
            for h in range(N_RET_HEADS):
                sl = slice(h * RET_HEAD_DIM, (h + 1) * RET_HEAD_DIM)
                qh, kh = q_r[:, sl].astype(BF16), k_r[:, sl].astype(BF16)
                vh = p_ref[rows, V_R0 + h * RET_HEAD_DIM:V_R0 + (h + 1) * RET_HEAD_DIM].astype(BF16)
                st_b = st_ref[c, h].astype(BF16)
                gs = gs_all[h]
                gs_b = gs.astype(BF16)
                xi_h = xi_ref[:, sl]
                dm = dm_ref[h]
                a_b = (_dot_nt(qh, kh) * dm).astype(BF16)
                o = _dot(a_b, vh) + _dot(qh, st_b) * xi_h
                mu = jnp.mean(o, axis=-1, keepdims=True)
                oc = o - mu
                rs = lax.rsqrt(jnp.mean(oc * oc, axis=-1, keepdims=True) + GN_EPS)
                on = oc * rs
                g = p_ref[rows, G_R0 + h * RET_HEAD_DIM:G_R0 + (h + 1) * RET_HEAD_DIM]
                sg = _sigmoid(g)
                dr = dmix_ref[rows, ATTN_W + h * RET_HEAD_DIM:ATTN_W + (h + 1) * RET_HEAD_DIM].astype(F32)
                dp_ref[rows, G_R0 + h * RET_HEAD_DIM:G_R0 + (h + 1) * RET_HEAD_DIM] = (
                    dr * on * (sg * (1.0 + g * (1.0 - sg)))).astype(BF16)
                don = dr * g * sg
                do = rs * (don - jnp.mean(don, axis=-1, keepdims=True) - on * jnp.mean(don * on, axis=-1, keepdims=True))
                do_b = do.astype(BF16)
                dox_b = (do * xi_h).astype(BF16)
                da_b = (_dot_nt(do_b, vh) * dm).astype(BF16)
                dq_parts.append(_dot(da_b, kh) + _dot_nt(dox_b, st_b))
                dk_parts.append(_dot_tn(da_b, qh) + ze_ref[:, sl] * _dot_nt(vh, gs_b))
                dv = _dot_tn(a_b, do_b) + _dot(kz[:, sl].astype(BF16), gs_b)
                dp_ref[rows, V_R0 + h * RET_HEAD_DIM:V_R0 + (h + 1) * RET_HEAD_DIM] = dv.astype(BF16)
                gs_all[h] = dc_ref[0:1, sl] * gs + _dot_tn(qh, dox_b)
            dq = jnp.concatenate(dq_parts, axis=-1)
            dk = jnp.concatenate(dk_parts, axis=-1)
            dp_ref[rows, Q_R0:Q_R0 + RET_W] = (dq * cos4 - _rot2(dq * sin4)).astype(BF16)
            dp_ref[rows, K_R0:K_R0 + RET_W] = (RET_K_SCALE * (dk * cos4 - _rot2(dk * sin4))).astype(BF16)

        for h in range(N_RET_HEADS):
            gstate[h] = gs_all[h]
        ckv[...] = later_kv
        dsk_acc[...] += dsk

        @pl.when(i == nb - 1)
        def _():
            dsk_ref[...] = jnp.sum(dsk_acc[...], axis=0, keepdims=True)

    rev = lambda i: nb - 1 - i
    return _hosted_call(
        body, name="mixer_bwd", grid=(nb,),
        in_specs=[
            pl.BlockSpec(memory_space=pltpu.SMEM),
            pl.BlockSpec((cps * CHUNK, IN_W), lambda i: (rev(i), 0)),
            pl.BlockSpec((CHUNK, 2 * KV_W), lambda i: (jnp.maximum(cps * rev(i) - 1, 0), KV_A0 // (2 * KV_W))),
            pl.BlockSpec((cps * CHUNK, D_MODEL), lambda i: (rev(i), 0)),
            pl.BlockSpec((cps, N_RET_HEADS, RET_HEAD_DIM, RET_HEAD_DIM), lambda i: (rev(i), 0, 0, 0)),
            pl.BlockSpec((cps * CHUNK, RET_HEAD_DIM), lambda i: (rev(i), 0)),
            pl.BlockSpec((cps * CHUNK, RET_HEAD_DIM), lambda i: (rev(i), 0)),
            _resident((N_RET_HEADS, CHUNK, CHUNK)), _resident((CHUNK, RET_W)), _resident((CHUNK, RET_W)), _resident((8, RET_W)),
        ],
        out_specs=[pl.BlockSpec((cps * CHUNK, IN_W), lambda i: (rev(i), 0)), pl.BlockSpec((1, 128), lambda i: (0, 0))],
        out_shape=[jax.ShapeDtypeStruct((s, IN_W), BF16), jax.ShapeDtypeStruct((1, 128), F32)],
        scratch_shapes=[pltpu.VMEM((N_RET_HEADS, RET_HEAD_DIM, RET_HEAD_DIM), F32), pltpu.VMEM((CHUNK, 2 * KV_W), F32),
                        pltpu.VMEM((CHUNK, 128), F32)],
        args=(sinks, proj, proj, dmix, states, sin, cos, d_intra, xi_full, zeta_full, decay_full), hosted=hosted)


def _in_proj_bwd(dproj, w_in, x, gain, dx1, hosted=None):
    s = x.shape[0]
    tm = min(BIG_TOKEN_TILE, s)

    def body(dp_ref, w_ref, x_ref, g_ref, dx1_ref, dx_ref, gacc_ref):
        @pl.when(pl.program_id(0) == 0)
        def _():
            gacc_ref[...] = jnp.zeros_like(gacc_ref)

        dh = _dot_nt(dp_ref[...], w_ref[...])
        xv = x_ref[...]
        d1, dg = _rms_bwd(dh, xv, _rstd(xv), g_ref[...])
        dx_ref[...] = dx1_ref[...] + d1
        gacc_ref[0:1, :] += dg

    tok = lambda w: pl.BlockSpec((tm, w), lambda i: (i, 0))
    return _hosted_call(
        body, name="in_proj_bwd", grid=(s // tm,),
        in_specs=[tok(IN_W), _resident((D_MODEL, IN_W)), tok(D_MODEL), _resident((1, D_MODEL)), tok(D_MODEL)],
        out_specs=[tok(D_MODEL), pl.BlockSpec((8, D_MODEL), lambda i: (0, 0))],
        out_shape=[jax.ShapeDtypeStruct((s, D_MODEL), F32), jax.ShapeDtypeStruct((8, D_MODEL), F32)],
        scratch_shapes=[], args=(dproj, w_in, x, gain, dx1), hosted=hosted)


def _weight_grad(a, b, tn, name, by_block=False):
    s, m = a.shape
    n = b.shape[1]
    tk = min(WEIGHT_GRAD_TOKENS, s)

    def body(a_ref, b_ref, o_ref):
        @pl.when(pl.program_id(1) == 0)
        def _():
            o_ref[...] = jnp.zeros_like(o_ref)

        o_ref[...] += _dot_tn(a_ref[...], b_ref[...])

    if by_block:
        out_spec = pl.BlockSpec((None, m, tn), lambda j, k: (j, 0, 0))
        out_shape = jax.ShapeDtypeStruct((n // tn, m, tn), F32)
    else:
        out_spec = pl.BlockSpec((m, tn), lambda j, k: (0, j))
        out_shape = jax.ShapeDtypeStruct((m, n), F32)
    return pl.pallas_call(
        body, name=name, grid=(n // tn, s // tk),
        in_specs=[pl.BlockSpec((tk, m), lambda j, k: (k, 0)), pl.BlockSpec((tk, tn), lambda j, k: (k, j))],
        out_specs=out_spec, out_shape=out_shape,
        compiler_params=_params(("arbitrary", "arbitrary")),
    )(a, b)


def _place():
    return lax.axis_index("x"), lax.axis_index("y"), lax.axis_index("c")


def _remote(src, dst, send_sems, recv_sems, k, to):
    return pltpu.make_async_remote_copy(src_ref=src, dst_ref=dst, send_sem=send_sems.at[k], recv_sem=recv_sems.at[k],
                                        device_id=to, device_id_type=MESH)


def _gather_level1_copies(w_refs, out_refs, send_sems, recv_sems, local_sems):
    x, y, c = _place()
    mine_at = 2 * x + y
    peers = [(x, y, 1 - c), (1 - x, y, c), (x, 1 - y, c), (1 - x, 1 - y, c)]
    local, sends, recvs = [], [], []
    for i, (w, out) in enumerate(zip(w_refs, out_refs)):
        half = w.shape[0] // 2
        src = w.at[pl.ds(pl.multiple_of(c * half, 16), half), :]
        mine = out.at[mine_at, c]
        local.append(pltpu.make_async_copy(src, mine, local_sems.at[i]))
        for k, p in enumerate(peers):
            sends.append(_remote(src, mine, send_sems, recv_sems, 4 * i + k, p))
            lands = out.at[mine_at, 1 - c] if k == 0 else out.at[2 * p[0] + p[1], c]
            recvs.append(_remote(src, lands, send_sems, recv_sems, 4 * i + k, p))
    return local, sends, recvs


def _gather_level1_start(w_refs, out_refs, send_sems, recv_sems, local_sems):
    local, sends, _ = _gather_level1_copies(w_refs, out_refs, send_sems, recv_sems, local_sems)
    for cp in local + sends:
        cp.start()


def _gather_level1_finish(w_refs, out_refs, send_sems, recv_sems, local_sems):
    local, sends, recvs = _gather_level1_copies(w_refs, out_refs, send_sems, recv_sems, local_sems)
    for cp in recvs:
        cp.wait_recv()
    for cp in sends:
        cp.wait_send()
    for cp in local:
        cp.wait()


def _gather_level2_copies(in_refs, out_refs, send_sems, recv_sems, local_sems):
    x, y, c = _place()
    chips = [(1 - x, y), (x, 1 - y), (1 - x, 1 - y)]
    sends, recvs = [], []
    for i, (src, out) in enumerate(zip(in_refs, out_refs)):
        for j, (px, py) in enumerate(chips):
            sends.append(_remote(src.at[2 * px + py, c], out.at[2 * px + py, c], send_sems, recv_sems, 3 * i + j, (x, y, 1 - c)))
            recvs.append(_remote(src.at[2 * px + py, c], out.at[2 * px + py, 1 - c], send_sems, recv_sems, 3 * i + j,
                                 (x, y, 1 - c)))
    return sends, recvs


def _gather_level2_start(in_refs, out_refs, send_sems, recv_sems, local_sems):
    for cp in _gather_level2_copies(in_refs, out_refs, send_sems, recv_sems, local_sems)[0]:
        cp.start()


def _gather_level2_finish(in_refs, out_refs, send_sems, recv_sems, local_sems):
    sends, recvs = _gather_level2_copies(in_refs, out_refs, send_sems, recv_sems, local_sems)
    for cp in recvs:
        cp.wait_recv()
    for cp in sends:
        cp.wait_send()


def _gathered_shape(w):
    r, cols = w.shape
    return jax.ShapeDtypeStruct((N_CHIPS, 2, r // 2, cols), w.dtype)


def _hosted_gather_level1(shards):
    n = len(shards)
    return _Hosted(shards, [_gathered_shape(w) for w in shards], {}, 4 * n, n, _gather_level1_start, _gather_level1_finish)


def _hosted_gather_level2(gathered):
    n = len(gathered)
    return _Hosted(gathered, [jax.ShapeDtypeStruct(g.shape, g.dtype) for g in gathered], {i: i for i in range(n)}, 3 * n, 0,
                   _gather_level2_start, _gather_level2_finish)


def _gather_now(w, name):
    def body(w_ref, out_ref, send1, recv1, local1, send2, recv2):
        _gather_level1_start([w_ref], [out_ref], send1, recv1, local1)
        _gather_level1_finish([w_ref], [out_ref], send1, recv1, local1)
        _gather_level2_start([out_ref], [out_ref], send2, recv2, None)
        _gather_level2_finish([out_ref], [out_ref], send2, recv2, None)

    return pl.pallas_call(
        body, name=name, out_shape=_gathered_shape(w),
        in_specs=[pl.BlockSpec(memory_space=pl.ANY)], out_specs=pl.BlockSpec(memory_space=pl.ANY),
        scratch_shapes=[pltpu.SemaphoreType.DMA((4,)), pltpu.SemaphoreType.DMA((4,)), pltpu.SemaphoreType.DMA((1,)),
                        pltpu.SemaphoreType.DMA((3,)), pltpu.SemaphoreType.DMA((3,))],
    )(w)


def _scatter_copies(g_refs, land_refs, send_sems, recv_sems, local_sems):
    x, y, c = _place()
    copies = []
    for i, (g, land) in enumerate(zip(g_refs, land_refs)):
        for k, (px, py, pc) in enumerate(_relations(x, y, c)):
            copies.append(_remote(g.at[2 * px + py, pc], land.at[k], send_sems, recv_sems, 7 * i + k, (px, py, pc)))
    return copies


def _scatter_start(g_refs, land_refs, send_sems, recv_sems, local_sems):
    for cp in _scatter_copies(g_refs, land_refs, send_sems, recv_sems, local_sems):
        cp.start()


def _scatter_finish(g_refs, land_refs, send_sems, recv_sems, local_sems):
    for cp in _scatter_copies(g_refs, land_refs, send_sems, recv_sems, local_sems):
        cp.wait()


def _hosted_scatter(grads):
    lands = [jax.ShapeDtypeStruct((N_DEV - 1,) + g.shape[2:], g.dtype) for g in grads]
    return _Hosted(grads, lands, {}, 7 * len(grads), 0, _scatter_start, _scatter_finish)


def _relations(x, y, c):
    rel = []
    for fx in (0, 1):
        for fy in (0, 1):
            for fc in (0, 1):
                if fx or fy or fc:
                    rel.append(((1 - x) if fx else x, (1 - y) if fy else y, (1 - c) if fc else c))
    return rel


def _gather_small(v, name):
    r, cols = v.shape

    def body(v_ref, out_ref, send_sems, recv_sems):
        x, y, c = _place()
        peers = _relations(x, y, c)

        def slot(p):
            return out_ref.at[4 * p[0] + 2 * p[1] + p[2]]

        out_ref[4 * x + 2 * y + c] = v_ref[...]
        sends = [pltpu.make_async_remote_copy(
            src_ref=v_ref, dst_ref=slot((x, y, c)), send_sem=send_sems.at[k], recv_sem=recv_sems.at[k],
            device_id=p, device_id_type=MESH) for k, p in enumerate(peers)]
        for cp in sends:
            cp.start()
        for k, p in enumerate(peers):
            pltpu.make_async_remote_copy(
                src_ref=v_ref, dst_ref=slot(p), send_sem=send_sems.at[k], recv_sem=recv_sems.at[k],
                device_id=p, device_id_type=MESH).wait_recv()
        for cp in sends:
            cp.wait_send()

    return pl.pallas_call(
        body, name=name,
        out_shape=jax.ShapeDtypeStruct((N_DEV, r, cols), v.dtype),
        in_specs=[pl.BlockSpec(memory_space=pltpu.VMEM)],
        out_specs=pl.BlockSpec(memory_space=pltpu.VMEM),
        scratch_shapes=[pltpu.SemaphoreType.DMA((7,)), pltpu.SemaphoreType.DMA((7,))],
    )(v)


def _join_halves(shards):
    n = len(shards)

    def body(*refs):
        in_refs, out_refs = refs[:n], refs[n:2 * n]
        send_sems, recv_sems = refs[2 * n:]
        x, y, c = _place()
        sends = [_remote(src.at[c], out.at[c], send_sems, recv_sems, i, (x, y, 1 - c))
                 for i, (src, out) in enumerate(zip(in_refs, out_refs))]
        recvs = [_remote(src.at[c], out.at[1 - c], send_sems, recv_sems, i, (x, y, 1 - c))
                 for i, (src, out) in enumerate(zip(in_refs, out_refs))]
        for cp in sends:
            cp.start()
        for cp in recvs:
            cp.wait_recv()
        for cp in sends:
            cp.wait_send()

    hbm = pl.BlockSpec(memory_space=pl.ANY)
    return pl.pallas_call(
        body, name="grad_join_halves",
        out_shape=[jax.ShapeDtypeStruct(t.shape, t.dtype) for t in shards],
        in_specs=[hbm] * n, out_specs=[hbm] * n, input_output_aliases={i: i for i in range(n)},
        scratch_shapes=[pltpu.SemaphoreType.DMA((n,)), pltpu.SemaphoreType.DMA((n,))],
    )(*shards)


def _row_tile(rows, row_bytes, limit=1 << 20):
    best = 8
    for t in range(8, rows + 1, 8):
        if rows % t == 0 and t * row_bytes <= limit:
            best = t
    return best


def _sum_pieces(g, land, place, name):
    _, _, rh, cols = g.shape
    tr = _row_tile(rh, (N_DEV - 1) * cols * 4, 4 << 20)

    def body(p_ref, g_ref, l_ref, out_ref):
        acc = g_ref[...]
        for k in range(N_DEV - 1):
            acc = acc + l_ref[k]
        out_ref[...] = acc

    return pl.pallas_call(
        body, name=name,
        grid_spec=pltpu.PrefetchScalarGridSpec(
            num_scalar_prefetch=1, grid=(rh // tr,),
            in_specs=[pl.BlockSpec((None, None, tr, cols), lambda r, p: (p[0], p[1], r, 0)),
                      pl.BlockSpec((N_DEV - 1, tr, cols), lambda r, p: (0, r, 0))],
            out_specs=pl.BlockSpec((None, tr, cols), lambda r, p: (p[1], r, 0))),
        out_shape=jax.ShapeDtypeStruct((2, rh, cols), g.dtype),
        compiler_params=_params(("arbitrary",)),
    )(place, g, land)


def _adamw_math(w, g, m, v):
    m = ADAM_B1 * m + (1.0 - ADAM_B1) * g
    v = ADAM_B2 * v + (1.0 - ADAM_B2) * (g * g)
    m_hat = m / (1.0 - ADAM_B1 ** ADAM_STEP)
    v_hat = v / (1.0 - ADAM_B2 ** ADAM_STEP)
    delta = -ADAM_LR * (m_hat / (jnp.sqrt(v_hat) + ADAM_EPS) + ADAM_WD * w)
    return delta, m, v


def _adamw(w, g, m, v, name):
    r, cols = w.shape
    tr = _row_tile(r, cols * 4)

    def body(w_ref, g_ref, m_ref, v_ref, d_ref, nm_ref, nv_ref):
        d_ref[...], nm_ref[...], nv_ref[...] = _adamw_math(w_ref[...], g_ref[...], m_ref[...], v_ref[...])

    blk = pl.BlockSpec((tr, cols), lambda i: (i, 0))
    shape = jax.ShapeDtypeStruct((r, cols), F32)
    return pl.pallas_call(
        body, name=name, grid=(r // tr,), in_specs=[blk] * 4, out_specs=[blk] * 3, out_shape=[shape] * 3,
        compiler_params=_params(("arbitrary",)),
    )(w, g, m, v)


def _sum_devices(gathered):
    _, r, cols = gathered.shape

    def body(a_ref, g_ref):
        g = a_ref[0]
        for k in range(1, N_DEV):
            g = g + a_ref[k]
        g_ref[...] = g

    return pl.pallas_call(body, name="sum_small_grads", out_shape=jax.ShapeDtypeStruct((r, cols), F32))(gathered)


def _pack_conv(cw):
    flat = cw.reshape(-1)
    return jnp.pad(flat, (0, ROWS_CONV * D_MODEL - flat.shape[0])).reshape(ROWS_CONV, D_MODEL)


def _unpack_conv(rows):
    return rows.reshape(-1)[:CONV_WIDTH * UP_W // N_CHIPS].reshape(CONV_WIDTH, UP_W // N_CHIPS)


def _columns_to_shards(w):
    r, n = w.shape
    return jnp.transpose(w.reshape(r, N_CHIPS, n // N_CHIPS), (1, 0, 2))


def _shards_to_columns(w):
    _, r, n = w.shape
    return jnp.transpose(w, (1, 0, 2)).reshape(r, N_CHIPS * n)


def _pack_small(g_mix_pre, g_mix_post, g_ffn_pre, g_ffn_post, sinks, conv_b, loss):
    pad_row = lambda v: jnp.pad(v.reshape(1, -1), ((0, 0), (0, D_MODEL - v.size)))
    cb = jnp.pad(conv_b.reshape(-1), (0, 6 * D_MODEL - UP_W)).reshape(6, D_MODEL)
    zeros2 = jnp.zeros((2, D_MODEL), F32)
    return jnp.concatenate([g_mix_pre.reshape(1, -1), g_mix_post.reshape(1, -1), g_ffn_pre.reshape(1, -1),
                            g_ffn_post.reshape(1, -1), pad_row(sinks), pad_row(loss), zeros2, cb, zeros2], axis=0)


def _unpack_small(p):
    return dict(mix_pre_norm=p[0:1], mix_post_norm=p[1:2], ffn_pre_norm=p[2:3], ffn_post_norm=p[3:4],
                attn_sinks=p[4:5, :N_ATTN_HEADS], loss=p[5, 0], conv_b=p[8:14].reshape(1, -1)[:, :UP_W],
                conv_w=_unpack_conv(p[SMALL_ROWS:SMALL_ROWS + ROWS_CONV]))


def _local_step(x, target, g_mix_pre, w_in, sinks, w_out, g_mix_post, g_ffn_pre, w_up, conv_w, conv_b, w_down, g_ffn_post,
                distributed=True):
    s = x.shape[0]
    consts = _ret_constants()
    sin, cos = _rope_tables(s)

    by_half = lambda g, rows: g.reshape(N_CHIPS, 2, rows // (2 * N_CHIPS), g.shape[-1])

    if distributed:
        (h1, proj), level1 = _in_proj(x, g_mix_pre, w_in, _hosted_gather_level1([w_out, w_up, w_down]))
        (mix, states), (w_out, w_up, w_down) = _mixer_fwd(proj, sinks, sin, cos, consts, _hosted_gather_level2(level1))
        w_out, w_down = w_out.reshape(D_MODEL, D_MODEL), w_down.reshape(D_FF, D_MODEL)
        w_up = w_up.reshape(N_CHIPS, D_MODEL, UP_W // N_CHIPS)
    else:
        (h1, proj), _ = _in_proj(x, g_mix_pre, w_in)
        (mix, states), _ = _mixer_fwd(proj, sinks, sin, cos, consts)
    mixed, x1, h2, u0 = _out_up_proj(mix, x, w_out, g_mix_post, g_ffn_pre, w_up)
    y, dy2, dout, du, conv_acc, tail_acc = _ffn_tail(u0, x1, target, conv_w, conv_b, w_down, g_ffn_post)
    du0, dx1, dmixed, dmix, head_acc = _ffn_head_bwd(du, conv_w, w_up, x1, g_ffn_pre, dout, mixed, g_mix_post, w_out)

    d_w_down = _weight_grad(y, dy2, 512, "grad_w_down")
    d_w_up = _weight_grad(h2, du0, UP_W // N_CHIPS, "grad_w_up", by_block=True)
    d_w_out = _weight_grad(mix, dmixed, D_MODEL, "grad_w_out")
    early = [by_half(d_w_down, D_FF), by_half(d_w_up, N_CHIPS * D_MODEL), by_half(d_w_out, D_MODEL)]
    (dproj, dsinks), early_lands = _mixer_bwd(proj, dmix, states, sinks, sin, cos, consts,
                                              _hosted_scatter(early) if distributed else None)
    d_w_in = _columns_to_shards(_weight_grad(h1, dproj, IN_W // 2, "grad_w_in"))
    late = [by_half(d_w_in, N_CHIPS * D_MODEL)]
    (grad_x, in_acc), late_lands = _in_proj_bwd(dproj, w_in, x, g_mix_pre, dx1, _hosted_scatter(late) if distributed else None)

    small = _pack_small(in_acc[0], head_acc[1], head_acc[0], tail_acc[0], dsinks[0, :N_ATTN_HEADS], conv_acc[3],
                        jnp.sum(tail_acc[1]))
    d_conv = jnp.pad(conv_acc[0:CONV_WIDTH].reshape(-1), (0, CONV_FULL_ROWS * D_MODEL - CONV_WIDTH * UP_W))
    small = jnp.concatenate([small, d_conv.reshape(CONV_FULL_ROWS, D_MODEL)], axis=0)
    grads = dict(w_down=early[0], w_up=early[1], w_out=early[2], w_in=late[0])
    lands = dict(zip(["w_down", "w_up", "w_out", "w_in"], early_lands + late_lands))
    return grad_x, grads, lands, small


def kernel(x, mix_pre_norm, w_in, attn_sinks, w_out, mix_post_norm, ffn_pre_norm, w_up, conv_w, conv_b, w_down, ffn_post_norm, loss_target, m_mix_pre_norm, m_w_in, m_attn_sinks, m_w_out, m_mix_post_norm, m_ffn_pre_norm, m_w_up, m_conv_w, m_conv_b, m_w_down, m_ffn_post_norm, v_mix_pre_norm, v_w_in, v_attn_sinks, v_w_out, v_mix_post_norm, v_ffn_pre_norm, v_w_up, v_conv_w, v_conv_b, v_w_down, v_ffn_post_norm):
    cx, cy, cc = _place()
    shard = 2 * cx + cy

    w_in_all = _gather_now(w_in[0].astype(BF16), "gather_w_in").reshape(N_CHIPS, D_MODEL, IN_W // N_CHIPS)
    conv_all = _gather_small(_pack_conv(conv_w[0]), "gather_conv_w")
    conv_full = jnp.concatenate([_unpack_conv(conv_all[2 * k]) for k in range(N_CHIPS)], axis=1)

    grad_x, grads, lands, small = _local_step(
        x[0], loss_target[0], mix_pre_norm, _shards_to_columns(w_in_all), attn_sinks.reshape(-1), w_out[0].astype(BF16),
        mix_post_norm, ffn_pre_norm, w_up[0].astype(BF16), conv_full, conv_b, w_down[0].astype(BF16), ffn_post_norm)

    place = jnp.stack([shard, cc]).astype(jnp.int32)
    mats = ["w_in", "w_out", "w_up", "w_down"]
    halves = [_sum_pieces(grads[n], lands[n], place, "sum_grad_" + n) for n in mats]
    weights = dict(w_in=(w_in, m_w_in, v_w_in), w_out=(w_out, m_w_out, v_w_out), w_up=(w_up, m_w_up, v_w_up),
                   w_down=(w_down, m_w_down, v_w_down))
    mat_out = {}
    for n, joined in zip(mats, _join_halves(halves)):
        w, m, v = weights[n]
        g = joined.reshape(w.shape[1:])
        mat_out[n] = (g,) + tuple(_adamw(w[0], g, m[0], v[0], "adamw_" + n))

    small_sum = _sum_devices(_gather_small(small, "gather_small_grads"))
    d_conv_full = small_sum[SMALL_ROWS:].reshape(-1)[:CONV_WIDTH * UP_W].reshape(CONV_WIDTH, UP_W)
    d_conv_mine = lax.dynamic_slice_in_dim(d_conv_full, shard * (UP_W // N_CHIPS), UP_W // N_CHIPS, axis=1)
    g_s = jnp.concatenate([small_sum[:SMALL_ROWS], _pack_conv(d_conv_mine)], axis=0)
    zero = jnp.zeros((), F32)
    pack_rep = lambda a, b, c_, d, e, f, cw: jnp.concatenate([_pack_small(a, b, c_, d, e, f, zero), _pack_conv(cw[0])], axis=0)
    w_s = pack_rep(mix_pre_norm, mix_post_norm, ffn_pre_norm, ffn_post_norm, attn_sinks, conv_b, conv_w)
    m_s = pack_rep(m_mix_pre_norm, m_mix_post_norm, m_ffn_pre_norm, m_ffn_post_norm, m_attn_sinks, m_conv_b, m_conv_w)
    v_s = pack_rep(v_mix_pre_norm, v_mix_post_norm, v_ffn_pre_norm, v_ffn_post_norm, v_attn_sinks, v_conv_b, v_conv_w)
    delta_s, new_m_s, new_v_s = _adamw(w_s, g_s, m_s, v_s, "adamw_small")

    names = ["mix_pre_norm", "w_in", "attn_sinks", "w_out", "mix_post_norm", "ffn_pre_norm", "w_up", "conv_w", "conv_b",
             "w_down", "ffn_post_norm"]

    def leaves(which, packed_small):
        smalls = _unpack_small(packed_small)
        return [mat_out[n][which][None] if n in mat_out else (smalls[n][None] if n == "conv_w" else smalls[n]) for n in names]

    loss = _unpack_small(g_s)["loss"]
    return (loss, grad_x[None], *leaves(0, g_s), *leaves(1, delta_s), *leaves(2, new_m_s), *leaves(3, new_v_s))
```

```python
import functools
import math

import jax
import jax.numpy as jnp
from jax import lax
from jax.experimental import pallas as pl
from jax.experimental.pallas import tpu as pltpu

F32 = jnp.float32
BF16 = jnp.bfloat16

D_MODEL = 1024
HEAD_DIM = 64
ATTN_W = 512
N_ATTN_HEADS = 8
KV_W = 128
RET_W = 512
N_RET_HEADS = 4
RET_HEAD_DIM = 128
CHUNK = 128
IN_W = 2816
D_FF = 2816
UP_W = 2 * D_FF
CONV_WIDTH = 3
RMS_EPS = 1e-6
GN_EPS = 1e-6
MASK_VALUE = -1e30
ATTN_SCALE = HEAD_DIM ** -0.5
RET_K_SCALE = RET_HEAD_DIM ** -0.5
GELU_C = math.sqrt(2.0 / math.pi)
GELU_A = 0.044715

ADAM_LR = 0.001
ADAM_B1 = 0.9
ADAM_B2 = 0.999
ADAM_EPS = 1e-08
ADAM_WD = 0.01
ADAM_STEP = 10

N_CHIPS = 4
N_DEV = 8
MESH = pl.DeviceIdType.MESH
VMEM_LIMIT_V7X = 56 * 1024 * 1024
TOKEN_TILE = 256
BIG_TOKEN_TILE = 512
IN_PROJ_TOKEN_TILE = 1024
WEIGHT_GRAD_TOKENS = 2048
FFN_ROW_BLOCK = 64
HEAD_BWD_COLS = 512
MIXER_CHUNKS_PER_STEP = 4
Q_A0, KV_A0, Q_R0, K_R0, V_R0, G_R0 = 0, 512, 768, 1280, 1792, 2304

ROWS_W_IN, ROWS_W_OUT, ROWS_W_UP, ROWS_W_DOWN = 704, 256, 1408, 704
ROWS_PACK = ROWS_W_IN + ROWS_W_OUT + ROWS_W_UP + ROWS_W_DOWN
ROWS_CONV = 8
SMALL_ROWS = 16
CONV_FULL_ROWS = 24


def _params(sem=None, **kw):
    if sem is not None:
        kw["dimension_semantics"] = sem
    return pltpu.CompilerParams(vmem_limit_bytes=VMEM_LIMIT_V7X, **kw)


def _resident(shape):
    zeros = (0,) * len(shape)
    return pl.BlockSpec(shape, lambda *_: zeros, pipeline_mode=pl.Buffered(1))


class _Hosted:
    def __init__(self, ins, outs, aliases, n_pairs, n_local, start, finish):
        self.ins, self.outs, self.aliases = list(ins), list(outs), dict(aliases)
        self.n_pairs, self.n_local, self.start, self.finish = n_pairs, max(n_local, 1), start, finish


def _hosted_call(compute, *, name, grid, in_specs, out_specs, out_shape, scratch_shapes, args, hosted=None):
    params = _params(("arbitrary",) * len(grid))
    if hosted is None:
        res = pl.pallas_call(compute, name=name, grid=grid, in_specs=in_specs, out_specs=out_specs, out_shape=out_shape,
                             scratch_shapes=scratch_shapes, compiler_params=params)(*args)
        return list(res), []
    n_in, n_out, n_scr = len(in_specs), len(out_specs), len(scratch_shapes)
    h_in, h_out = len(hosted.ins), len(hosted.outs)

    def at(step_of):
        cond = pl.program_id(0) == step_of(grid[0])
        for d in range(1, len(grid)):
            cond = jnp.logical_and(cond, pl.program_id(d) == step_of(grid[d]))
        return cond

    def body(*refs):
        ins, refs = refs[:n_in], refs[n_in:]
        h_ins, refs = refs[:h_in], refs[h_in:]
        outs, refs = refs[:n_out], refs[n_out:]
        h_outs, refs = refs[:h_out], refs[h_out:]
        scr, sems = refs[:n_scr], refs[n_scr:]

        @pl.when(at(lambda n: 0))
        def _():
            hosted.start(h_ins, h_outs, *sems)

        compute(*ins, *outs, *scr)

        @pl.when(at(lambda n: n - 1))
        def _():
            hosted.finish(h_ins, h_outs, *sems)

    hbm = pl.BlockSpec(memory_space=pl.ANY)
    res = pl.pallas_call(
        body, name=name, grid=grid,
        in_specs=list(in_specs) + [hbm] * h_in, out_specs=list(out_specs) + [hbm] * h_out,
        out_shape=list(out_shape) + hosted.outs,
        scratch_shapes=list(scratch_shapes) + [pltpu.SemaphoreType.DMA((hosted.n_pairs,)), pltpu.SemaphoreType.DMA((hosted.n_pairs,)),
                                               pltpu.SemaphoreType.DMA((hosted.n_local,))],
        input_output_aliases={n_in + a: n_out + b for a, b in hosted.aliases.items()},
        compiler_params=params,
    )(*args, *hosted.ins)
    return list(res[:n_out]), list(res[n_out:])


def _dot(a, b):
    return jnp.dot(a, b, preferred_element_type=F32)


def _dot_nt(a, b):
    return lax.dot_general(a, b, (((1,), (1,)), ((), ())), preferred_element_type=F32)


def _dot_tn(a, b):
    return lax.dot_general(a, b, (((0,), (0,)), ((), ())), preferred_element_type=F32)


def _shift_matrix(n, by):
    row = lax.broadcasted_iota(jnp.int32, (n, n), 0)
    col = lax.broadcasted_iota(jnp.int32, (n, n), 1)
    return jnp.where(col == row + by, 1.0, 0.0).astype(BF16)


def _rstd(v):
    return lax.rsqrt(jnp.mean(v * v, axis=-1, keepdims=True) + RMS_EPS)


def _rms_bwd(dy, v, rstd, gain):
    n = v * rstd
    dgain = jnp.sum(dy * n, axis=0, keepdims=True)
    dn = dy * gain
    dv = rstd * (dn - n * jnp.mean(dn * n, axis=-1, keepdims=True))
    return dv, dgain


def _lane_lo(shape):
    return (lax.broadcasted_iota(jnp.int32, shape, 1) % 128) < HEAD_DIM


GROUP = N_ATTN_HEADS // (KV_W // HEAD_DIM)


def _attn_valid(chunk_index):
    qi = lax.broadcasted_iota(jnp.int32, (GROUP * CHUNK, 2 * CHUNK), 0) % CHUNK
    kj = lax.broadcasted_iota(jnp.int32, (GROUP * CHUNK, 2 * CHUNK), 1)
    first_key = jnp.where(chunk_index > 0, 0, CHUNK)
    return jnp.logical_and(jnp.logical_and(kj > qi, kj >= first_key), kj <= qi + CHUNK)


def _half(shape, hk):
    lo = _lane_lo(shape)
    return lo if hk == 0 else jnp.logical_not(lo)


def _stack_heads(ref, row0, col0, hk):
    half = _half((CHUNK, 128), hk)
    parts = []
    for j in range(GROUP):
        h = GROUP * hk + j
        pair = ref[row0:row0 + CHUNK, col0 + (h // 2) * 128:col0 + (h // 2 + 1) * 128].astype(F32)
        if h % 2 != hk:
            pair = pltpu.roll(pair, HEAD_DIM, 1)
        parts.append(jnp.where(half, pair, 0.0))
    return jnp.concatenate(parts, axis=0)


def _unstack_heads(stacked, hk):
    pairs = []
    for q in range(GROUP // 2):
        even, odd = stacked[2 * q * CHUNK:(2 * q + 1) * CHUNK], stacked[(2 * q + 1) * CHUNK:(2 * q + 2) * CHUNK]
        pairs.append(even + pltpu.roll(odd, HEAD_DIM, 1) if hk == 0 else pltpu.roll(even, HEAD_DIM, 1) + odd)
    return pairs


def _group_sinks(sk_ref, hk):
    row = lax.broadcasted_iota(jnp.int32, (GROUP * CHUNK, 1), 0)
    col = jnp.full((GROUP * CHUNK, 1), sk_ref[GROUP * hk], F32)
    for j in range(1, GROUP):
        col = jnp.where(row >= j * CHUNK, sk_ref[GROUP * hk + j], col)
    return col


def _attn_probs(q_b, kk_b, valid, sink):
    s = _dot_nt(q_b, kk_b) * ATTN_SCALE
    s = jnp.where(valid, s, MASK_VALUE)
    m = jnp.maximum(jnp.max(s, axis=-1, keepdims=True), sink)
    e = jnp.exp(s - m)
    e_sink = jnp.exp(sink - m)
    inv = 1.0 / (jnp.sum(e, axis=-1, keepdims=True) + e_sink)
    return e * inv, e_sink * inv


def _rot2(v):
    w = v.shape[1]
    even = (lax.broadcasted_iota(jnp.int32, v.shape, 1) % 2) == 0
    return jnp.where(even, -pltpu.roll(v, w - 1, 1), pltpu.roll(v, 1, 1))


def _tile4(v):
    return jnp.concatenate([v, v, v, v], axis=-1)


def _sigmoid(v):
    return 1.0 / (1.0 + jnp.exp(-v))


def _ret_constants():
    h = N_RET_HEADS
    log_gamma = jnp.log(1.0 - jnp.power(2.0, -5.0 - jnp.arange(h, dtype=F32)))
    idx = jnp.arange(CHUNK, dtype=F32)
    rel = idx[:, None] - idx[None, :]
    d_intra = jnp.where(rel[None] >= 0, jnp.exp(log_gamma[:, None, None] * jnp.maximum(rel, 0.0)[None]), 0.0)
    xi = jnp.exp(log_gamma[None, :] * (idx[:, None] + 1.0))
    zeta = jnp.exp(log_gamma[None, :] * (CHUNK - 1.0 - idx[:, None]))
    decay = jnp.exp(log_gamma * CHUNK)
    xi_full = jnp.repeat(xi, RET_HEAD_DIM, axis=1)
    zeta_full = jnp.repeat(zeta, RET_HEAD_DIM, axis=1)
    decay_full = jnp.broadcast_to(jnp.repeat(decay, RET_HEAD_DIM)[None, :], (8, RET_W))
    return d_intra.astype(F32), xi_full.astype(F32), zeta_full.astype(F32), decay_full.astype(F32)


def _rope_tables(s):
    pos = jnp.arange(s, dtype=F32)
    angle = 1.0 / jnp.power(10000.0, jnp.linspace(0.0, 1.0, RET_HEAD_DIM // 2, dtype=F32))
    angle = jnp.repeat(angle, 2)
    return jnp.sin(pos[:, None] * angle[None]), jnp.cos(pos[:, None] * angle[None])


def _in_proj(x, gain, w_in, hosted=None):
    s = x.shape[0]
    tm = min(IN_PROJ_TOKEN_TILE, s)

    def body(x_ref, g_ref, w_ref, h_ref, p_ref):
        xv = x_ref[...]
        h = (xv * _rstd(xv) * g_ref[...]).astype(BF16)
        h_ref[...] = h
        p_ref[...] = _dot(h, w_ref[...])

    return _hosted_call(
        body, name="in_proj", grid=(s // tm,),
        in_specs=[pl.BlockSpec((tm, D_MODEL), lambda i: (i, 0)), _resident((1, D_MODEL)), _resident((D_MODEL, IN_W))],
        out_specs=[pl.BlockSpec((tm, D_MODEL), lambda i: (i, 0)), pl.BlockSpec((tm, IN_W), lambda i: (i, 0))],
        out_shape=[jax.ShapeDtypeStruct((s, D_MODEL), BF16), jax.ShapeDtypeStruct((s, IN_W), F32)],
        scratch_shapes=[], args=(x, gain, w_in), hosted=hosted)


def _mixer_fwd(proj, sinks, sin, cos, consts, hosted=None):
    s = proj.shape[0]
    nc = s // CHUNK
    cps = MIXER_CHUNKS_PER_STEP
    d_intra, xi_full, zeta_full, decay_full = consts

    def body(sk_ref, p_ref, pkv_ref, sin_ref, cos_ref, dm_ref, xi_ref, ze_ref, dc_ref, mix_ref, st_ref, state):
        i = pl.program_id(0)

        @pl.when(i == 0)
        def _():
            state[...] = jnp.zeros_like(state)

        st = [state[h] for h in range(N_RET_HEADS)]
        for c in range(cps):
            r0 = c * CHUNK
            rows = slice(r0, r0 + CHUNK)

            kv_cur = p_ref[rows, KV_A0:KV_A0 + 2 * KV_W]
            kv_prev = pkv_ref[...] if c == 0 else p_ref[r0 - CHUNK:r0, KV_A0:KV_A0 + 2 * KV_W]
            kk = jnp.concatenate([kv_prev[:, :KV_W], kv_cur[:, :KV_W]], axis=0)
            vv = jnp.concatenate([kv_prev[:, KV_W:], kv_cur[:, KV_W:]], axis=0)
            kk_b = kk.astype(BF16)
            valid = _attn_valid(cps * i + c)
            for hk in range(KV_W // HEAD_DIM):
                q_b = _stack_heads(p_ref, r0, Q_A0, hk).astype(BF16)
                p, _ = _attn_probs(q_b, kk_b, valid, _group_sinks(sk_ref, hk))
                v_b = jnp.where(_half((2 * CHUNK, 128), hk), vv, 0.0).astype(BF16)
                for q, pair in enumerate(_unstack_heads(_dot(p.astype(BF16), v_b), hk)):
                    pi = (GROUP // 2) * hk + q
                    mix_ref[rows, pi * 128:(pi + 1) * 128] = pair.astype(BF16)

            sin4, cos4 = _tile4(sin_ref[rows, :]), _tile4(cos_ref[rows, :])
            q_r = p_ref[rows, Q_R0:Q_R0 + RET_W]
            k_r = p_ref[rows, K_R0:K_R0 + RET_W] * RET_K_SCALE
            q_r = q_r * cos4 + _rot2(q_r) * sin4
            k_r = k_r * cos4 + _rot2(k_r) * sin4
            kz = k_r * ze_ref[...]
            for h in range(N_RET_HEADS):
                sl = slice(h * RET_HEAD_DIM, (h + 1) * RET_HEAD_DIM)
                qh, kh = q_r[:, sl].astype(BF16), k_r[:, sl].astype(BF16)
                vh = p_ref[rows, V_R0 + h * RET_HEAD_DIM:V_R0 + (h + 1) * RET_HEAD_DIM].astype(BF16)
                st_ref[c, h] = st[h]
                a = _dot_nt(qh, kh) * dm_ref[h]
                o = _dot(a.astype(BF16), vh) + _dot(qh, st[h].astype(BF16)) * xi_ref[:, sl]
                st[h] = dc_ref[0:1, sl] * st[h] + _dot_tn(kz[:, sl].astype(BF16), vh)
                mu = jnp.mean(o, axis=-1, keepdims=True)
                oc = o - mu
                on = oc * lax.rsqrt(jnp.mean(oc * oc, axis=-1, keepdims=True) + GN_EPS)
                g = p_ref[rows, G_R0 + h * RET_HEAD_DIM:G_R0 + (h + 1) * RET_HEAD_DIM]
                mix_ref[rows, ATTN_W + h * RET_HEAD_DIM:ATTN_W + (h + 1) * RET_HEAD_DIM] = (g * _sigmoid(g) * on).astype(BF16)
        for h in range(N_RET_HEADS):
            state[h] = st[h]

    return _hosted_call(
        body, name="mixer_fwd", grid=(nc // cps,),
        in_specs=[
            pl.BlockSpec(memory_space=pltpu.SMEM),
            pl.BlockSpec((cps * CHUNK, IN_W), lambda i: (i, 0)),
            pl.BlockSpec((CHUNK, 2 * KV_W), lambda i: (jnp.maximum(cps * i - 1, 0), KV_A0 // (2 * KV_W))),
            pl.BlockSpec((cps * CHUNK, RET_HEAD_DIM), lambda i: (i, 0)),
            pl.BlockSpec((cps * CHUNK, RET_HEAD_DIM), lambda i: (i, 0)),
            _resident((N_RET_HEADS, CHUNK, CHUNK)), _resident((CHUNK, RET_W)), _resident((CHUNK, RET_W)), _resident((8, RET_W)),
        ],
        out_specs=[
            pl.BlockSpec((cps * CHUNK, D_MODEL), lambda i: (i, 0)),
            pl.BlockSpec((cps, N_RET_HEADS, RET_HEAD_DIM, RET_HEAD_DIM), lambda i: (i, 0, 0, 0)),
        ],
        out_shape=[jax.ShapeDtypeStruct((s, D_MODEL), BF16),
                   jax.ShapeDtypeStruct((nc, N_RET_HEADS, RET_HEAD_DIM, RET_HEAD_DIM), F32)],
        scratch_shapes=[pltpu.VMEM((N_RET_HEADS, RET_HEAD_DIM, RET_HEAD_DIM), F32)],
        args=(sinks, proj, proj, sin, cos, d_intra, xi_full, zeta_full, decay_full), hosted=hosted)


def _out_up_proj(mix, x, w_out, g_post, g_pre, w_up):
    s = x.shape[0]
    tm = min(BIG_TOKEN_TILE, s)
    blk = UP_W // N_CHIPS

    def body(mix_ref, x_ref, wo_ref, g2_ref, g3_ref, wu_ref, mixed_ref, x1_ref, h2_ref, u0_ref):
        mixed = _dot(mix_ref[...], wo_ref[...])
        mixed_ref[...] = mixed
        x1 = x_ref[...] + mixed * _rstd(mixed) * g2_ref[...]
        x1_ref[...] = x1
        h2 = (x1 * _rstd(x1) * g3_ref[...]).astype(BF16)
        h2_ref[...] = h2
        for k in range(N_CHIPS):
            u0_ref[:, k * blk:(k + 1) * blk] = _dot(h2, wu_ref[k]).astype(BF16)

    tok = lambda w: pl.BlockSpec((tm, w), lambda i: (i, 0))
    return pl.pallas_call(
        body, name="out_up_proj", grid=(s // tm,),
        in_specs=[tok(D_MODEL), tok(D_MODEL), _resident((D_MODEL, D_MODEL)), _resident((1, D_MODEL)), _resident((1, D_MODEL)),
                  _resident((N_CHIPS, D_MODEL, blk))],
        out_specs=[tok(D_MODEL), tok(D_MODEL), tok(D_MODEL), tok(UP_W)],
        out_shape=[jax.ShapeDtypeStruct((s, D_MODEL), F32), jax.ShapeDtypeStruct((s, D_MODEL), F32),
                   jax.ShapeDtypeStruct((s, D_MODEL), BF16), jax.ShapeDtypeStruct((s, UP_W), BF16)],
        compiler_params=_params(("arbitrary",)),
    )(mix, x, w_out, g_post, g_pre, w_up)


def _ffn_tail(u0, x1, target, conv_w, conv_b, w_down, g_post):
    s = x1.shape[0]
    tm = TOKEN_TILE
    last = s // tm - 1
    rb, lanes = FFN_ROW_BLOCK, 128

    def body(u0_ref, x1_ref, t_ref, cw_ref, cb_ref, wd_ref, g_ref,
             y_ref, dy2_ref, dout_ref, du_ref, cacc_ref, gacc_ref, u1_s, u2_s, carry, gelu_s, slope_s, dy_s, cacc):
        i = pl.program_id(0)

        @pl.when(i == 0)
        def _():
            carry[...] = jnp.zeros_like(carry)
            cacc[...] = jnp.zeros_like(cacc)
            gacc_ref[...] = jnp.zeros_like(gacc_ref)

        shift1, shift2 = _shift_matrix(tm, -1), _shift_matrix(tm, -2)
        r8 = lax.broadcasted_iota(jnp.int32, (8, 1), 0)
        wide = 2 * lanes

        def shift_block(col):
            cols = slice(col, col + wide)
            u1_s[:, cols] = _dot(shift1, u0_ref[:, cols])
            u2_s[:, cols] = _dot(shift2, u0_ref[:, cols])
            c14, c15 = carry[14:15, cols], carry[15:16, cols]
            u1_s[0:8, cols] = jnp.where(r8 == 0, c15, u1_s[0:8, cols])
            u2_s[0:8, cols] = jnp.where(r8 == 0, c14, jnp.where(r8 == 1, c15, u2_s[0:8, cols]))

        def taps(col):
            return (cw_ref[0:1, col:col + lanes], cw_ref[1:2, col:col + lanes], cw_ref[2:3, col:col + lanes],
                    cb_ref[0:1, col:col + lanes])

        def shifted(r0, col):
            return (u2_s[r0:r0 + rb, col:col + lanes], u1_s[r0:r0 + rb, col:col + lanes],
                    u0_ref[r0:r0 + rb, col:col + lanes].astype(F32))

        def conv(r0, col, w):
            u2, u1, uc = shifted(r0, col)
            return w[0] * u2 + w[1] * u1 + w[2] * uc + w[3]

        fold = lambda v: jnp.sum(v.reshape(rb // 8, 8, lanes), axis=0)

        shift_block(0)
        shift_block(D_FF)
        for j in range(D_FF // lanes):
            cg, cv = j * lanes, D_FF + j * lanes
            if cg % wide == 0 and cg + wide < D_FF:
                shift_block(cg + wide)
                shift_block(cv + wide)
            wg, wv = taps(cg), taps(cv)
            for r0 in range(0, tm, rb):
                gate, val = conv(r0, cg, wg), conv(r0, cv, wv)
                g2 = gate * gate
                th = jnp.tanh(gate * (GELU_C + GELU_C * GELU_A * g2))
                hp = 0.5 * th + 0.5
                gelu = gate * hp
                dgelu = hp + gate * (1.0 - th * th) * (0.5 * GELU_C + 1.5 * GELU_C * GELU_A * g2)
                y_ref[r0:r0 + rb, cg:cg + lanes] = (gelu * val).astype(BF16)
                gelu_s[r0:r0 + rb, cg:cg + lanes] = gelu
                slope_s[r0:r0 + rb, cg:cg + lanes] = dgelu * val

        y2 = _dot(y_ref[...], wd_ref[...])
        r4 = _rstd(y2)
        gain = g_ref[...]
        out = x1_ref[...] + y2 * r4 * gain
        diff = out - t_ref[...]
        dout = diff * (1.0 / D_MODEL)
        dout_ref[...] = dout
        dy2, dgain = _rms_bwd(dout, y2, r4, gain)
        dy2_b = dy2.astype(BF16)
        dy2_ref[...] = dy2_b
        gacc_ref[0:1, :] += dgain
        gacc_ref[1:2, :] += 0.5 * jnp.sum(diff * dout, axis=0, keepdims=True)
        carry[...] = u0_ref[tm - 16:tm, :].astype(F32)

        dy_s[:, 0:wide] = _dot_nt(dy2_b, wd_ref[0:wide, :])
        for j in range(D_FF // lanes):
            cg, cv = j * lanes, D_FF + j * lanes
            if cg % wide == 0 and cg + wide < D_FF:
                dy_s[:, cg + wide:cg + 2 * wide] = _dot_nt(dy2_b, wd_ref[cg + wide:cg + 2 * wide, :])
            acc = [[jnp.zeros((8, lanes), F32) for _ in range(CONV_WIDTH + 1)] for _ in range(2)]
            for r0 in range(0, tm, rb):
                dy = dy_s[r0:r0 + rb, cg:cg + lanes]
                d_gate = dy * slope_s[r0:r0 + rb, cg:cg + lanes]
                d_val = dy * gelu_s[r0:r0 + rb, cg:cg + lanes]
                for side, (col, d) in enumerate(((cg, d_gate), (cv, d_val))):
                    du_ref[r0:r0 + rb, col:col + lanes] = d.astype(BF16)
                    for k, u in enumerate(shifted(r0, col)):
                        acc[side][k] = acc[side][k] + fold(d * u)
                    acc[side][CONV_WIDTH] = acc[side][CONV_WIDTH] + fold(d)
            for side, col in enumerate((cg, cv)):
                for k in range(CONV_WIDTH + 1):
                    cacc[8 * k:8 * k + 8, col:col + lanes] += acc[side][k]

        @pl.when(i == last)
        def _():
            for k in range(CONV_WIDTH + 1):
                cacc_ref[k:k + 1, :] = jnp.sum(cacc[8 * k:8 * k + 8, :], axis=0, keepdims=True)
            cacc_ref[CONV_WIDTH + 1:8, :] = jnp.zeros((8 - CONV_WIDTH - 1, UP_W), F32)

    tok = lambda w: pl.BlockSpec((tm, w), lambda i: (i, 0))
    return pl.pallas_call(
        body, name="ffn_tail", grid=(s // tm,),
        in_specs=[tok(UP_W), tok(D_MODEL), tok(D_MODEL), _resident((CONV_WIDTH, UP_W)), _resident((1, UP_W)),
                  _resident((D_FF, D_MODEL)), _resident((1, D_MODEL))],
        out_specs=[tok(D_FF), tok(D_MODEL), tok(D_MODEL), tok(UP_W),
                   pl.BlockSpec((8, UP_W), lambda i: (0, 0)), pl.BlockSpec((8, D_MODEL), lambda i: (0, 0))],
        out_shape=[jax.ShapeDtypeStruct((s, D_FF), BF16), jax.ShapeDtypeStruct((s, D_MODEL), BF16),
                   jax.ShapeDtypeStruct((s, D_MODEL), F32), jax.ShapeDtypeStruct((s, UP_W), BF16),
                   jax.ShapeDtypeStruct((8, UP_W), F32), jax.ShapeDtypeStruct((8, D_MODEL), F32)],
        scratch_shapes=[pltpu.VMEM((tm, UP_W), F32), pltpu.VMEM((tm, UP_W), F32), pltpu.VMEM((16, UP_W), F32),
                        pltpu.VMEM((tm, D_FF), F32), pltpu.VMEM((tm, D_FF), F32),
                        pltpu.VMEM((tm, D_FF), F32), pltpu.VMEM((8 * (CONV_WIDTH + 1), UP_W), F32)],
        compiler_params=_params(("arbitrary",)),
    )(u0, x1, target, conv_w, conv_b, w_down, g_post)


def _ffn_head_bwd(du, conv_w, w_up, x1, g_pre, dout, mixed, g_post, w_out):
    s = x1.shape[0]
    tm = TOKEN_TILE
    nt = s // tm
    blk = UP_W // N_CHIPS

    def body(du_ref, halo_ref, cw_ref, wu_ref, x1_ref, g3_ref, dout_ref, mixed_ref, g2_ref, wo_ref,
             du0_ref, dx1_ref, dmixed_ref, dmix_ref, gacc_ref, dbuf):
        i = pl.program_id(0)

        @pl.when(i == 0)
        def _():
            gacc_ref[...] = jnp.zeros_like(gacc_ref)

        dbuf[0:tm, :] = du_ref[...].astype(F32)
        dbuf[tm:tm + 16, :] = jnp.where(i < nt - 1, halo_ref[...].astype(F32), 0.0)
        dh2 = jnp.zeros((tm, D_MODEL), F32)
        for k in range(N_CHIPS):
            for c0 in range(0, blk, HEAD_BWD_COLS):
                width = min(HEAD_BWD_COLS, blk - c0)
                cols = slice(k * blk + c0, k * blk + c0 + width)
                du0_b = (cw_ref[2:3, cols] * dbuf[0:tm, cols] + cw_ref[1:2, cols] * dbuf[1:1 + tm, cols]
                         + cw_ref[0:1, cols] * dbuf[2:2 + tm, cols]).astype(BF16)
                du0_ref[:, cols] = du0_b
                dh2 = dh2 + _dot_nt(du0_b, wu_ref[k, :, c0:c0 + width])
        x1 = x1_ref[...]
        d3, dg3 = _rms_bwd(dh2, x1, _rstd(x1), g3_ref[...])
        dx1 = dout_ref[...] + d3
        dx1_ref[...] = dx1
        mixed = mixed_ref[...]
        dmixed, dg2 = _rms_bwd(dx1, mixed, _rstd(mixed), g2_ref[...])
        dmixed_b = dmixed.astype(BF16)
        dmixed_ref[...] = dmixed_b
        dmix_ref[...] = _dot_nt(dmixed_b, wo_ref[...]).astype(BF16)
        gacc_ref[0:1, :] += dg3
        gacc_ref[1:2, :] += dg2

    tok = lambda w: pl.BlockSpec((tm, w), lambda i: (i, 0))
    halo = pl.BlockSpec((16, UP_W), lambda i: (jnp.minimum(i + 1, nt - 1) * (tm // 16), 0))
    return pl.pallas_call(
        body, name="ffn_head_bwd", grid=(nt,),
        in_specs=[tok(UP_W), halo, _resident((CONV_WIDTH, UP_W)), _resident((N_CHIPS, D_MODEL, blk)), tok(D_MODEL),
                  _resident((1, D_MODEL)), tok(D_MODEL), tok(D_MODEL), _resident((1, D_MODEL)), _resident((D_MODEL, D_MODEL))],
        out_specs=[tok(UP_W), tok(D_MODEL), tok(D_MODEL), tok(D_MODEL), pl.BlockSpec((8, D_MODEL), lambda i: (0, 0))],
        out_shape=[jax.ShapeDtypeStruct((s, UP_W), BF16), jax.ShapeDtypeStruct((s, D_MODEL), F32),
                   jax.ShapeDtypeStruct((s, D_MODEL), BF16), jax.ShapeDtypeStruct((s, D_MODEL), BF16),
                   jax.ShapeDtypeStruct((8, D_MODEL), F32)],
        scratch_shapes=[pltpu.VMEM((tm + 16, UP_W), F32)],
        compiler_params=_params(("arbitrary",)),
    )(du, du, conv_w, w_up, x1, g_pre, dout, mixed, g_post, w_out)


def _mixer_bwd(proj, dmix, states, sinks, sin, cos, consts, hosted=None):
    s = proj.shape[0]
    nc = s // CHUNK
    cps = MIXER_CHUNKS_PER_STEP
    nb = nc // cps
    d_intra, xi_full, zeta_full, decay_full = consts

    def body(sk_ref, p_ref, pkv_ref, dmix_ref, st_ref, sin_ref, cos_ref, dm_ref, xi_ref, ze_ref, dc_ref,
             dp_ref, dsk_ref, gstate, ckv, dsk_acc):
        i = pl.program_id(0)
        block = nb - 1 - i

        @pl.when(i == 0)
        def _():
            gstate[...] = jnp.zeros_like(gstate)
            ckv[...] = jnp.zeros_like(ckv)
            dsk_acc[...] = jnp.zeros_like(dsk_acc)

        gs_all = [gstate[h] for h in range(N_RET_HEADS)]
        later_kv = ckv[...]
        lane = lax.broadcasted_iota(jnp.int32, (CHUNK, 128), 1)
        dsk = jnp.zeros((CHUNK, 128), F32)
        for c in reversed(range(cps)):
            r0 = c * CHUNK
            rows = slice(r0, r0 + CHUNK)

            kv_cur = p_ref[rows, KV_A0:KV_A0 + 2 * KV_W]
            kv_prev = pkv_ref[...] if c == 0 else p_ref[r0 - CHUNK:r0, KV_A0:KV_A0 + 2 * KV_W]
            kk = jnp.concatenate([kv_prev[:, :KV_W], kv_cur[:, :KV_W]], axis=0)
            vv = jnp.concatenate([kv_prev[:, KV_W:], kv_cur[:, KV_W:]], axis=0)
            kk_b, vv_b = kk.astype(BF16), vv.astype(BF16)
            valid = _attn_valid(cps * block + c)
            dkk = jnp.zeros((2 * CHUNK, KV_W), F32)
            dvv = jnp.zeros((2 * CHUNK, KV_W), F32)
            for hk in range(KV_W // HEAD_DIM):
                q_b = _stack_heads(p_ref, r0, Q_A0, hk).astype(BF16)
                do_b = _stack_heads(dmix_ref, r0, 0, hk).astype(BF16)
                p, p_sink = _attn_probs(q_b, kk_b, valid, _group_sinks(sk_ref, hk))
                dpr = _dot_nt(do_b, vv_b)
                delta = jnp.sum(p * dpr, axis=-1, keepdims=True)
                ds_b = (p * (dpr - delta) * ATTN_SCALE).astype(BF16)
                dsink = -p_sink * delta
                for j in range(GROUP):
                    dsk = dsk + jnp.where(lane == GROUP * hk + j, dsink[j * CHUNK:(j + 1) * CHUNK], 0.0)
                k_b = jnp.where(_half((2 * CHUNK, 128), hk), kk, 0.0).astype(BF16)
                for q, pair in enumerate(_unstack_heads(_dot(ds_b, k_b), hk)):
                    pi = (GROUP // 2) * hk + q
                    dp_ref[rows, Q_A0 + pi * 128:Q_A0 + (pi + 1) * 128] = pair.astype(BF16)
                dkk = dkk + _dot_tn(ds_b, q_b)
                dvv = dvv + _dot_tn(p.astype(BF16), do_b)
            dp_ref[rows, KV_A0:KV_A0 + KV_W] = (dkk[CHUNK:] + later_kv[:, :KV_W]).astype(BF16)
            dp_ref[rows, KV_A0 + KV_W:KV_A0 + 2 * KV_W] = (dvv[CHUNK:] + later_kv[:, KV_W:]).astype(BF16)
            later_kv = jnp.concatenate([dkk[:CHUNK], dvv[:CHUNK]], axis=1)

            sin4, cos4 = _tile4(sin_ref[rows, :]), _tile4(cos_ref[rows, :])
            q_r = p_ref[rows, Q_R0:Q_R0 + RET_W]
            k_r = p_ref[rows, K_R0:K_R0 + RET_W] * RET_K_SCALE
            q_r = q_r * cos4 + _rot2(q_r) * sin4
            k_r = k_r * cos4 + _rot2(k_r) * sin4
            kz = k_r * ze_ref[...]
            dq_parts, dk_parts = [], []
            for h in range(N_RET_HEADS):
                sl = slice(h * RET_HEAD_DIM, (h + 1) * RET_HEAD_DIM)
                qh, kh = q_r[:, sl].astype(BF16), k_r[:, sl].astype(BF16)
                vh = p_ref[rows, V_R0 + h * RET_HEAD_DIM:V_R0 + (h + 1) * RET_HEAD_DIM].astype(BF16)
                st_b = st_ref[c, h].astype(BF16)
                gs = gs_all[h]
                gs_b = gs.astype(BF16)
                xi_h = xi_ref[:, sl]
                dm = dm_ref[h]
                a_b = (_dot_nt(qh, kh) * dm).astype(BF16)
                o = _dot(a_b, vh) + _dot(qh, st_b) * xi_h
                mu = jnp.mean(o, axis=-1, keepdims=True)
                oc = o - mu
                rs = lax.rsqrt(jnp.mean(oc * oc, axis=-1, keepdims=True) + GN_EPS)
                on = oc * rs
                g = p_ref[rows, G_R0 + h * RET_HEAD_DIM:G_R0 + (h + 1) * RET_HEAD_DIM]
                sg = _sigmoid(g)
                dr = dmix_ref[rows, ATTN_W + h * RET_HEAD_DIM:ATTN_W + (h + 1) * RET_HEAD_DIM].astype(F32)
                dp_ref[rows, G_R0 + h * RET_HEAD_DIM:G_R0 + (h + 1) * RET_HEAD_DIM] = (
                    dr * on * (sg * (1.0 + g * (1.0 - sg)))).astype(BF16)
                don = dr * g * sg
                do = rs * (don - jnp.mean(don, axis=-1, keepdims=True) - on * jnp.mean(don * on, axis=-1, keepdims=True))
                do_b = do.astype(BF16)
                dox_b = (do * xi_h).astype(BF16)
                da_b = (_dot_nt(do_b, vh) * dm).astype(BF16)
                dq_parts.append(_dot(da_b, kh) + _dot_nt(dox_b, st_b))
                dk_parts.append(_dot_tn(da_b, qh) + ze_ref[:, sl] * _dot_nt(vh, gs_b))
                dv = _dot_tn(a_b, do_b) + _dot(kz[:, sl].astype(BF16), gs_b)
                dp_ref[rows, V_R0 + h * RET_HEAD_DIM:V_R0 + (h + 1) * RET_HEAD_DIM] = dv.astype(BF16)
                gs_all[h] = dc_ref[0:1, sl] * gs + _dot_tn(qh, dox_b)
            dq = jnp.concatenate(dq_parts, axis=-1)
            dk = jnp.concatenate(dk_parts, axis=-1)
            dp_ref[rows, Q_R0:Q_R0 + RET_W] = (dq * cos4 - _rot2(dq * sin4)).astype(BF16)
            dp_ref[rows, K_R0:K_R0 + RET_W] = (RET_K_SCALE * (dk * cos4 - _rot2(dk * sin4))).astype(BF16)

        for h in range(N_RET_HEADS):
            gstate[h] = gs_all[h]
        ckv[...] = later_kv
        dsk_acc[...] += dsk

        @pl.when(i == nb - 1)
        def _():
            dsk_ref[...] = jnp.sum(dsk_acc[...], axis=0, keepdims=True)

    rev = lambda i: nb - 1 - i
    return _hosted_call(
        body, name="mixer_bwd", grid=(nb,),
        in_specs=[
            pl.BlockSpec(memory_space=pltpu.SMEM),
            pl.BlockSpec((cps * CHUNK, IN_W), lambda i: (rev(i), 0)),
            pl.BlockSpec((CHUNK, 2 * KV_W), lambda i: (jnp.maximum(cps * rev(i) - 1, 0), KV_A0 // (2 * KV_W))),
            pl.BlockSpec((cps * CHUNK, D_MODEL), lambda i: (rev(i), 0)),
            pl.BlockSpec((cps, N_RET_HEADS, RET_HEAD_DIM, RET_HEAD_DIM), lambda i: (rev(i), 0, 0, 0)),
            pl.BlockSpec((cps * CHUNK, RET_HEAD_DIM), lambda i: (rev(i), 0)),
            pl.BlockSpec((cps * CHUNK, RET_HEAD_DIM), lambda i: (rev(i), 0)),
            _resident((N_RET_HEADS, CHUNK, CHUNK)), _resident((CHUNK, RET_W)), _resident((CHUNK, RET_W)), _resident((8, RET_W)),
        ],
        out_specs=[pl.BlockSpec((cps * CHUNK, IN_W), lambda i: (rev(i), 0)), pl.BlockSpec((1, 128), lambda i: (0, 0))],
        out_shape=[jax.ShapeDtypeStruct((s, IN_W), BF16), jax.ShapeDtypeStruct((1, 128), F32)],
        scratch_shapes=[pltpu.VMEM((N_RET_HEADS, RET_HEAD_DIM, RET_HEAD_DIM), F32), pltpu.VMEM((CHUNK, 2 * KV_W), F32),
                        pltpu.VMEM((CHUNK, 128), F32)],
        args=(sinks, proj, proj, dmix, states, sin, cos, d_intra, xi_full, zeta_full, decay_full), hosted=hosted)


def _in_proj_bwd(dproj, w_in, x, gain, dx1, hosted=None):
    s = x.shape[0]
    tm = min(IN_PROJ_TOKEN_TILE, s)

    def body(dp_ref, w_ref, x_ref, g_ref, dx1_ref, dx_ref, gacc_ref):
        @pl.when(pl.program_id(0) == 0)
        def _():
            gacc_ref[...] = jnp.zeros_like(gacc_ref)

        dh = _dot_nt(dp_ref[...], w_ref[...])
        xv = x_ref[...]
        d1, dg = _rms_bwd(dh, xv, _rstd(xv), g_ref[...])
        dx_ref[...] = dx1_ref[...] + d1
        gacc_ref[0:1, :] += dg

    tok = lambda w: pl.BlockSpec((tm, w), lambda i: (i, 0))
    return _hosted_call(
        body, name="in_proj_bwd", grid=(s // tm,),
        in_specs=[tok(IN_W), _resident((D_MODEL, IN_W)), tok(D_MODEL), _resident((1, D_MODEL)), tok(D_MODEL)],
        out_specs=[tok(D_MODEL), pl.BlockSpec((8, D_MODEL), lambda i: (0, 0))],
        out_shape=[jax.ShapeDtypeStruct((s, D_MODEL), F32), jax.ShapeDtypeStruct((8, D_MODEL), F32)],
        scratch_shapes=[], args=(dproj, w_in, x, gain, dx1), hosted=hosted)


def _weight_grad(a, b, tn, name, by_block=False, hosted=None):
    s, m = a.shape
    n = b.shape[1]
    tk = min(WEIGHT_GRAD_TOKENS if m <= D_MODEL else WEIGHT_GRAD_TOKENS // 2, s)

    def body(a_ref, b_ref, o_ref):
        @pl.when(pl.program_id(1) == 0)
        def _():
            o_ref[...] = jnp.zeros_like(o_ref)

        o_ref[...] += _dot_tn(a_ref[...], b_ref[...])

    if by_block:
        out_spec = pl.BlockSpec((None, m, tn), lambda j, k: (j, 0, 0))
        out_shape = jax.ShapeDtypeStruct((n // tn, m, tn), F32)
    else:
        out_spec = pl.BlockSpec((m, tn), lambda j, k: (0, j))
        out_shape = jax.ShapeDtypeStruct((m, n), F32)
    (out,), lands = _hosted_call(
        body, name=name, grid=(n // tn, s // tk),
        in_specs=[pl.BlockSpec((tk, m), lambda j, k: (k, 0)), pl.BlockSpec((tk, tn), lambda j, k: (k, j))],
        out_specs=[out_spec], out_shape=[out_shape], scratch_shapes=[], args=(a, b), hosted=hosted)
    return out if hosted is None else (out, lands)


def _place():
    return lax.axis_index("x"), lax.axis_index("y"), lax.axis_index("c")


def _remote(src, dst, send_sems, recv_sems, k, to):
    return pltpu.make_async_remote_copy(src_ref=src, dst_ref=dst, send_sem=send_sems.at[k], recv_sem=recv_sems.at[k],
                                        device_id=to, device_id_type=MESH)


def _gather_level1_copies(w_refs, out_refs, send_sems, recv_sems, local_sems):
    x, y, c = _place()
    mine_at = 2 * x + y
    peers = [(x, y, 1 - c), (1 - x, y, c), (x, 1 - y, c), (1 - x, 1 - y, c)]
    local, sends, recvs = [], [], []
    for i, (w, out) in enumerate(zip(w_refs, out_refs)):
        half = w.shape[0] // 2
        src = w.at[pl.ds(pl.multiple_of(c * half, 16), half), :]
        mine = out.at[mine_at, c]
        local.append(pltpu.make_async_copy(src, mine, local_sems.at[i]))
        for k, p in enumerate(peers):
            sends.append(_remote(src, mine, send_sems, recv_sems, 4 * i + k, p))
            lands = out.at[mine_at, 1 - c] if k == 0 else out.at[2 * p[0] + p[1], c]
            recvs.append(_remote(src, lands, send_sems, recv_sems, 4 * i + k, p))
    return local, sends, recvs


def _gather_level1_start(w_refs, out_refs, send_sems, recv_sems, local_sems):
    local, sends, _ = _gather_level1_copies(w_refs, out_refs, send_sems, recv_sems, local_sems)
    for cp in local + sends:
        cp.start()


def _gather_level1_finish(w_refs, out_refs, send_sems, recv_sems, local_sems):
    local, sends, recvs = _gather_level1_copies(w_refs, out_refs, send_sems, recv_sems, local_sems)
    for cp in recvs:
        cp.wait_recv()
    for cp in sends:
        cp.wait_send()
    for cp in local:
        cp.wait()


def _gather_level2_copies(in_refs, out_refs, send_sems, recv_sems, local_sems):
    x, y, c = _place()
    chips = [(1 - x, y), (x, 1 - y), (1 - x, 1 - y)]
    sends, recvs = [], []
    for i, (src, out) in enumerate(zip(in_refs, out_refs)):
        for j, (px, py) in enumerate(chips):
            sends.append(_remote(src.at[2 * px + py, c], out.at[2 * px + py, c], send_sems, recv_sems, 3 * i + j, (x, y, 1 - c)))
            recvs.append(_remote(src.at[2 * px + py, c], out.at[2 * px + py, 1 - c], send_sems, recv_sems, 3 * i + j,
                                 (x, y, 1 - c)))
    return sends, recvs


def _gather_level2_start(in_refs, out_refs, send_sems, recv_sems, local_sems):
    for cp in _gather_level2_copies(in_refs, out_refs, send_sems, recv_sems, local_sems)[0]:
        cp.start()


def _gather_level2_finish(in_refs, out_refs, send_sems, recv_sems, local_sems):
    sends, recvs = _gather_level2_copies(in_refs, out_refs, send_sems, recv_sems, local_sems)
    for cp in recvs:
        cp.wait_recv()
    for cp in sends:
        cp.wait_send()


def _gathered_shape(w):
    r, cols = w.shape
    return jax.ShapeDtypeStruct((N_CHIPS, 2, r // 2, cols), w.dtype)


def _hosted_gather_level1(shards):
    n = len(shards)
    return _Hosted(shards, [_gathered_shape(w) for w in shards], {}, 4 * n, n, _gather_level1_start, _gather_level1_finish)


def _hosted_gather_level2(gathered):
    n = len(gathered)
    return _Hosted(gathered, [jax.ShapeDtypeStruct(g.shape, g.dtype) for g in gathered], {i: i for i in range(n)}, 3 * n, 0,
                   _gather_level2_start, _gather_level2_finish)


def _gather_now(w, name):
    def body(w_ref, out_ref, send1, recv1, local1, send2, recv2):
        _gather_level1_start([w_ref], [out_ref], send1, recv1, local1)
        _gather_level1_finish([w_ref], [out_ref], send1, recv1, local1)
        _gather_level2_start([out_ref], [out_ref], send2, recv2, None)
        _gather_level2_finish([out_ref], [out_ref], send2, recv2, None)

    return pl.pallas_call(
        body, name=name, out_shape=_gathered_shape(w),
        in_specs=[pl.BlockSpec(memory_space=pl.ANY)], out_specs=pl.BlockSpec(memory_space=pl.ANY),
        scratch_shapes=[pltpu.SemaphoreType.DMA((4,)), pltpu.SemaphoreType.DMA((4,)), pltpu.SemaphoreType.DMA((1,)),
                        pltpu.SemaphoreType.DMA((3,)), pltpu.SemaphoreType.DMA((3,))],
    )(w)


def _scatter_copies(g_refs, land_refs, send_sems, recv_sems, local_sems):
    x, y, c = _place()
    copies = []
    for i, (g, land) in enumerate(zip(g_refs, land_refs)):
        for k, (px, py, pc) in enumerate(_relations(x, y, c)):
            copies.append(_remote(g.at[2 * px + py, pc], land.at[k], send_sems, recv_sems, 7 * i + k, (px, py, pc)))
    return copies


def _scatter_start(g_refs, land_refs, send_sems, recv_sems, local_sems):
    for cp in _scatter_copies(g_refs, land_refs, send_sems, recv_sems, local_sems):
        cp.start()


def _scatter_finish(g_refs, land_refs, send_sems, recv_sems, local_sems):
    for cp in _scatter_copies(g_refs, land_refs, send_sems, recv_sems, local_sems):
        cp.wait()


def _hosted_scatter(grads):
    lands = [jax.ShapeDtypeStruct((N_DEV - 1,) + g.shape[2:], g.dtype) for g in grads]
    return _Hosted(grads, lands, {}, 7 * len(grads), 0, _scatter_start, _scatter_finish)


def _relations(x, y, c):
    rel = []
    for fx in (0, 1):
        for fy in (0, 1):
            for fc in (0, 1):
                if fx or fy or fc:
                    rel.append(((1 - x) if fx else x, (1 - y) if fy else y, (1 - c) if fc else c))
    return rel


def _gather_small(v, name):
    r, cols = v.shape

    def body(v_ref, out_ref, send_sems, recv_sems):
        x, y, c = _place()
        peers = _relations(x, y, c)

        def slot(p):
            return out_ref.at[4 * p[0] + 2 * p[1] + p[2]]

        out_ref[4 * x + 2 * y + c] = v_ref[...]
        sends = [pltpu.make_async_remote_copy(
            src_ref=v_ref, dst_ref=slot((x, y, c)), send_sem=send_sems.at[k], recv_sem=recv_sems.at[k],
            device_id=p, device_id_type=MESH) for k, p in enumerate(peers)]
        for cp in sends:
            cp.start()
        for k, p in enumerate(peers):
            pltpu.make_async_remote_copy(
                src_ref=v_ref, dst_ref=slot(p), send_sem=send_sems.at[k], recv_sem=recv_sems.at[k],
                device_id=p, device_id_type=MESH).wait_recv()
        for cp in sends:
            cp.wait_send()

    return pl.pallas_call(
        body, name=name,
        out_shape=jax.ShapeDtypeStruct((N_DEV, r, cols), v.dtype),
        in_specs=[pl.BlockSpec(memory_space=pltpu.VMEM)],
        out_specs=pl.BlockSpec(memory_space=pltpu.VMEM),
        scratch_shapes=[pltpu.SemaphoreType.DMA((7,)), pltpu.SemaphoreType.DMA((7,))],
    )(v)


def _join_halves(shards):
    n = len(shards)

    def body(*refs):
        in_refs, out_refs = refs[:n], refs[n:2 * n]
        send_sems, recv_sems = refs[2 * n:]
        x, y, c = _place()
        sends = [_remote(src.at[c], out.at[c], send_sems, recv_sems, i, (x, y, 1 - c))
                 for i, (src, out) in enumerate(zip(in_refs, out_refs))]
        recvs = [_remote(src.at[c], out.at[1 - c], send_sems, recv_sems, i, (x, y, 1 - c))
                 for i, (src, out) in enumerate(zip(in_refs, out_refs))]
        for cp in sends:
            cp.start()
        for cp in recvs:
            cp.wait_recv()
        for cp in sends:
            cp.wait_send()

    hbm = pl.BlockSpec(memory_space=pl.ANY)
    return pl.pallas_call(
        body, name="grad_join_halves",
        out_shape=[jax.ShapeDtypeStruct(t.shape, t.dtype) for t in shards],
        in_specs=[hbm] * n, out_specs=[hbm] * n, input_output_aliases={i: i for i in range(n)},
        scratch_shapes=[pltpu.SemaphoreType.DMA((n,)), pltpu.SemaphoreType.DMA((n,))],
    )(*shards)


def _row_tile(rows, row_bytes, limit=1 << 20):
    best = 8
    for t in range(8, rows + 1, 8):
        if rows % t == 0 and t * row_bytes <= limit:
            best = t
    return best


def _sum_pieces(g, land, place, name):
    _, _, rh, cols = g.shape
    tr = _row_tile(rh, (N_DEV - 1) * cols * 4, 4 << 20)

    def body(p_ref, g_ref, l_ref, out_ref):
        acc = g_ref[...]
        for k in range(N_DEV - 1):
            acc = acc + l_ref[k].astype(F32)
        out_ref[...] = acc

    return pl.pallas_call(
        body, name=name,
        grid_spec=pltpu.PrefetchScalarGridSpec(
            num_scalar_prefetch=1, grid=(rh // tr,),
            in_specs=[pl.BlockSpec((None, None, tr, cols), lambda r, p: (p[0], p[1], r, 0)),
                      pl.BlockSpec((N_DEV - 1, tr, cols), lambda r, p: (0, r, 0))],
            out_specs=pl.BlockSpec((None, tr, cols), lambda r, p: (p[1], r, 0))),
        out_shape=jax.ShapeDtypeStruct((2, rh, cols), g.dtype),
        compiler_params=_params(("arbitrary",)),
    )(place, g, land)


def _adamw_math(w, g, m, v):
    m = ADAM_B1 * m + (1.0 - ADAM_B1) * g
    v = ADAM_B2 * v + (1.0 - ADAM_B2) * (g * g)
    m_hat = m / (1.0 - ADAM_B1 ** ADAM_STEP)
    v_hat = v / (1.0 - ADAM_B2 ** ADAM_STEP)
    delta = -ADAM_LR * (m_hat / (jnp.sqrt(v_hat) + ADAM_EPS) + ADAM_WD * w)
    return delta, m, v


def _adamw(w, g, m, v, name):
    r, cols = w.shape
    tr = _row_tile(r, cols * 4)

    def body(w_ref, g_ref, m_ref, v_ref, d_ref, nm_ref, nv_ref):
        d_ref[...], nm_ref[...], nv_ref[...] = _adamw_math(w_ref[...], g_ref[...], m_ref[...], v_ref[...])

    blk = pl.BlockSpec((tr, cols), lambda i: (i, 0))
    shape = jax.ShapeDtypeStruct((r, cols), F32)
    return pl.pallas_call(
        body, name=name, grid=(r // tr,), in_specs=[blk] * 4, out_specs=[blk] * 3, out_shape=[shape] * 3,
        compiler_params=_params(("arbitrary",)),
    )(w, g, m, v)


def _sum_devices(gathered):
    _, r, cols = gathered.shape

    def body(a_ref, g_ref):
        g = a_ref[0]
        for k in range(1, N_DEV):
            g = g + a_ref[k]
        g_ref[...] = g

    return pl.pallas_call(body, name="sum_small_grads", out_shape=jax.ShapeDtypeStruct((r, cols), F32))(gathered)


def _pack_conv(cw):
    flat = cw.reshape(-1)
    return jnp.pad(flat, (0, ROWS_CONV * D_MODEL - flat.shape[0])).reshape(ROWS_CONV, D_MODEL)


def _unpack_conv(rows):
    return rows.reshape(-1)[:CONV_WIDTH * UP_W // N_CHIPS].reshape(CONV_WIDTH, UP_W // N_CHIPS)


def _columns_to_shards(w):
    r, n = w.shape
    return jnp.transpose(w.reshape(r, N_CHIPS, n // N_CHIPS), (1, 0, 2))


def _shards_to_columns(w):
    _, r, n = w.shape
    return jnp.transpose(w, (1, 0, 2)).reshape(r, N_CHIPS * n)


def _pack_small(g_mix_pre, g_mix_post, g_ffn_pre, g_ffn_post, sinks, conv_b, loss):
    pad_row = lambda v: jnp.pad(v.reshape(1, -1), ((0, 0), (0, D_MODEL - v.size)))
    cb = jnp.pad(conv_b.reshape(-1), (0, 6 * D_MODEL - UP_W)).reshape(6, D_MODEL)
    zeros2 = jnp.zeros((2, D_MODEL), F32)
    return jnp.concatenate([g_mix_pre.reshape(1, -1), g_mix_post.reshape(1, -1), g_ffn_pre.reshape(1, -1),
                            g_ffn_post.reshape(1, -1), pad_row(sinks), pad_row(loss), zeros2, cb, zeros2], axis=0)


def _unpack_small(p):
    return dict(mix_pre_norm=p[0:1], mix_post_norm=p[1:2], ffn_pre_norm=p[2:3], ffn_post_norm=p[3:4],
                attn_sinks=p[4:5, :N_ATTN_HEADS], loss=p[5, 0], conv_b=p[8:14].reshape(1, -1)[:, :UP_W],
                conv_w=_unpack_conv(p[SMALL_ROWS:SMALL_ROWS + ROWS_CONV]))


def _local_step(x, target, g_mix_pre, w_in, sinks, w_out, g_mix_post, g_ffn_pre, w_up, conv_w, conv_b, w_down, g_ffn_post,
                distributed=True):
    s = x.shape[0]
    consts = _ret_constants()
    sin, cos = _rope_tables(s)

    by_half = lambda g, rows: g.reshape(N_CHIPS, 2, rows // (2 * N_CHIPS), g.shape[-1])

    if distributed:
        (h1, proj), level1 = _in_proj(x, g_mix_pre, w_in, _hosted_gather_level1([w_out, w_up, w_down]))
        (mix, states), (w_out, w_up, w_down) = _mixer_fwd(proj, sinks, sin, cos, consts, _hosted_gather_level2(level1))
        w_out, w_down = w_out.reshape(D_MODEL, D_MODEL), w_down.reshape(D_FF, D_MODEL)
        w_up = w_up.reshape(N_CHIPS, D_MODEL, UP_W // N_CHIPS)
    else:
        (h1, proj), _ = _in_proj(x, g_mix_pre, w_in)
        (mix, states), _ = _mixer_fwd(proj, sinks, sin, cos, consts)
    mixed, x1, h2, u0 = _out_up_proj(mix, x, w_out, g_mix_post, g_ffn_pre, w_up)
    y, dy2, dout, du, conv_acc, tail_acc = _ffn_tail(u0, x1, target, conv_w, conv_b, w_down, g_ffn_post)
    du0, dx1, dmixed, dmix, head_acc = _ffn_head_bwd(du, conv_w, w_up, x1, g_ffn_pre, dout, mixed, g_mix_post, w_out)

    d_w_down = _weight_grad(y, dy2, 512, "grad_w_down")
    d_w_up = _weight_grad(h2, du0, UP_W // N_CHIPS, "grad_w_up", by_block=True)
    d_w_out = _weight_grad(mix, dmixed, D_MODEL, "grad_w_out")
    early = [by_half(d_w_down, D_FF), by_half(d_w_up, N_CHIPS * D_MODEL), by_half(d_w_out, D_MODEL)]
    (dproj, dsinks), early_lands = _mixer_bwd(proj, dmix, states, sinks, sin, cos, consts,
                                              _hosted_scatter(early) if distributed else None)
    d_w_in = _columns_to_shards(_weight_grad(h1, dproj, IN_W // 2, "grad_w_in"))
    late = [by_half(d_w_in, N_CHIPS * D_MODEL)]
    (grad_x, in_acc), late_lands = _in_proj_bwd(dproj, w_in, x, g_mix_pre, dx1, _hosted_scatter(late) if distributed else None)

    small = _pack_small(in_acc[0], head_acc[1], head_acc[0], tail_acc[0], dsinks[0, :N_ATTN_HEADS], conv_acc[3],
                        jnp.sum(tail_acc[1]))
    d_conv = jnp.pad(conv_acc[0:CONV_WIDTH].reshape(-1), (0, CONV_FULL_ROWS * D_MODEL - CONV_WIDTH * UP_W))
    small = jnp.concatenate([small, d_conv.reshape(CONV_FULL_ROWS, D_MODEL)], axis=0)
    grads = dict(w_down=early[0], w_up=early[1], w_out=early[2], w_in=late[0])
    lands = dict(zip(["w_down", "w_up", "w_out", "w_in"], early_lands + late_lands))
    return grad_x, grads, lands, small


def kernel(x, mix_pre_norm, w_in, attn_sinks, w_out, mix_post_norm, ffn_pre_norm, w_up, conv_w, conv_b, w_down, ffn_post_norm, loss_target, m_mix_pre_norm, m_w_in, m_attn_sinks, m_w_out, m_mix_post_norm, m_ffn_pre_norm, m_w_up, m_conv_w, m_conv_b, m_w_down, m_ffn_post_norm, v_mix_pre_norm, v_w_in, v_attn_sinks, v_w_out, v_mix_post_norm, v_ffn_pre_norm, v_w_up, v_conv_w, v_conv_b, v_w_down, v_ffn_post_norm):
    cx, cy, cc = _place()
    shard = 2 * cx + cy

    w_in_all = _gather_now(w_in[0].astype(BF16), "gather_w_in").reshape(N_CHIPS, D_MODEL, IN_W // N_CHIPS)
    conv_all = _gather_small(_pack_conv(conv_w[0]), "gather_conv_w")
    conv_full = jnp.concatenate([_unpack_conv(conv_all[2 * k]) for k in range(N_CHIPS)], axis=1)

    grad_x, grads, lands, small = _local_step(
        x[0], loss_target[0], mix_pre_norm, _shards_to_columns(w_in_all), attn_sinks.reshape(-1), w_out[0].astype(BF16),
        mix_post_norm, ffn_pre_norm, w_up[0].astype(BF16), conv_full, conv_b, w_down[0].astype(BF16), ffn_post_norm)

    place = jnp.stack([shard, cc]).astype(jnp.int32)
    mats = ["w_in", "w_out", "w_up", "w_down"]
    halves = [_sum_pieces(grads[n], lands[n], place, "sum_grad_" + n) for n in mats]
    weights = dict(w_in=(w_in, m_w_in, v_w_in), w_out=(w_out, m_w_out, v_w_out), w_up=(w_up, m_w_up, v_w_up),
                   w_down=(w_down, m_w_down, v_w_down))
    mat_out = {}
    for n, joined in zip(mats, _join_halves(halves)):
        w, m, v = weights[n]
        g = joined.reshape(w.shape[1:])
        mat_out[n] = (g,) + tuple(_adamw(w[0], g, m[0], v[0], "adamw_" + n))

    small_sum = _sum_devices(_gather_small(small, "gather_small_grads"))
    d_conv_full = small_sum[SMALL_ROWS:].reshape(-1)[:CONV_WIDTH * UP_W].reshape(CONV_WIDTH, UP_W)
    d_conv_mine = lax.dynamic_slice_in_dim(d_conv_full, shard * (UP_W // N_CHIPS), UP_W // N_CHIPS, axis=1)
    g_s = jnp.concatenate([small_sum[:SMALL_ROWS], _pack_conv(d_conv_mine)], axis=0)
    zero = jnp.zeros((), F32)
    pack_rep = lambda a, b, c_, d, e, f, cw: jnp.concatenate([_pack_small(a, b, c_, d, e, f, zero), _pack_conv(cw[0])], axis=0)
    w_s = pack_rep(mix_pre_norm, mix_post_norm, ffn_pre_norm, ffn_post_norm, attn_sinks, conv_b, conv_w)
    m_s = pack_rep(m_mix_pre_norm, m_mix_post_norm, m_ffn_pre_norm, m_ffn_post_norm, m_attn_sinks, m_conv_b, m_conv_w)
    v_s = pack_rep(v_mix_pre_norm, v_mix_post_norm, v_ffn_pre_norm, v_ffn_post_norm, v_attn_sinks, v_conv_b, v_conv_w)
    delta_s, new_m_s, new_v_s = _adamw(w_s, g_s, m_s, v_s, "adamw_small")

    names = ["mix_pre_norm", "w_in", "attn_sinks", "w_out", "mix_post_norm", "ffn_pre_norm", "w_up", "conv_w", "conv_b",
             "w_down", "ffn_post_norm"]

    def leaves(which, packed_small):
        smalls = _unpack_small(packed_small)
        return [mat_out[n][which][None] if n in mat_out else (smalls[n][None] if n == "conv_w" else smalls[n]) for n in names]

    loss = _unpack_small(g_s)["loss"]
    return (loss, grad_x[None], *leaves(0, g_s), *leaves(1, delta_s), *leaves(2, new_m_s), *leaves(3, new_v_s))
```

```python
import functools
import math

import jax
import jax.numpy as jnp
from jax import lax
from jax.experimental import pallas as pl
from jax.experimental.pallas import tpu as pltpu

F32 = jnp.float32
BF16 = jnp.bfloat16

D_MODEL = 1024
HEAD_DIM = 64
ATTN_W = 512
N_ATTN_HEADS = 8
KV_W = 128
RET_W = 512
N_RET_HEADS = 4
RET_HEAD_DIM = 128
CHUNK = 128
IN_W = 2816
D_FF = 2816
UP_W = 2 * D_FF
CONV_WIDTH = 3
RMS_EPS = 1e-6
GN_EPS = 1e-6
MASK_VALUE = -1e30
ATTN_SCALE = HEAD_DIM ** -0.5
RET_K_SCALE = RET_HEAD_DIM ** -0.5
GELU_C = math.sqrt(2.0 / math.pi)
GELU_A = 0.044715

ADAM_LR = 0.001
ADAM_B1 = 0.9
ADAM_B2 = 0.999
ADAM_EPS = 1e-08
ADAM_WD = 0.01
ADAM_STEP = 10

N_CHIPS = 4
N_DEV = 8
MESH = pl.DeviceIdType.MESH
VMEM_LIMIT_V7X = 56 * 1024 * 1024
TOKEN_TILE = 256
BIG_TOKEN_TILE = 512
IN_PROJ_TOKEN_TILE = 1024
WEIGHT_GRAD_TOKENS = 2048
FFN_ROW_BLOCK = 64
HEAD_BWD_COLS = 512
MIXER_CHUNKS_PER_STEP = 4
Q_A0, KV_A0, Q_R0, K_R0, V_R0, G_R0 = 0, 512, 768, 1280, 1792, 2304

ROWS_W_IN, ROWS_W_OUT, ROWS_W_UP, ROWS_W_DOWN = 704, 256, 1408, 704
ROWS_PACK = ROWS_W_IN + ROWS_W_OUT + ROWS_W_UP + ROWS_W_DOWN
ROWS_CONV = 8
SMALL_ROWS = 16
CONV_FULL_ROWS = 24


def _params(sem=None, **kw):
    if sem is not None:
        kw["dimension_semantics"] = sem
    return pltpu.CompilerParams(vmem_limit_bytes=VMEM_LIMIT_V7X, **kw)


def _resident(shape):
    zeros = (0,) * len(shape)
    return pl.BlockSpec(shape, lambda *_: zeros, pipeline_mode=pl.Buffered(1))


class _Hosted:
    def __init__(self, ins, outs, aliases, n_pairs, n_local, start, finish):
        self.ins, self.outs, self.aliases = list(ins), list(outs), dict(aliases)
        self.n_pairs, self.n_local, self.start, self.finish = n_pairs, max(n_local, 1), start, finish


def _hosted_call(compute, *, name, grid, in_specs, out_specs, out_shape, scratch_shapes, args, hosted=None):
    params = _params(("arbitrary",) * len(grid))
    if hosted is None:
        res = pl.pallas_call(compute, name=name, grid=grid, in_specs=in_specs, out_specs=out_specs, out_shape=out_shape,
                             scratch_shapes=scratch_shapes, compiler_params=params)(*args)
        return list(res), []
    n_in, n_out, n_scr = len(in_specs), len(out_specs), len(scratch_shapes)
    h_in, h_out = len(hosted.ins), len(hosted.outs)

    def at(step_of):
        cond = pl.program_id(0) == step_of(grid[0])
        for d in range(1, len(grid)):
            cond = jnp.logical_and(cond, pl.program_id(d) == step_of(grid[d]))
        return cond

    def body(*refs):
        ins, refs = refs[:n_in], refs[n_in:]
        h_ins, refs = refs[:h_in], refs[h_in:]
        outs, refs = refs[:n_out], refs[n_out:]
        h_outs, refs = refs[:h_out], refs[h_out:]
        scr, sems = refs[:n_scr], refs[n_scr:]

        @pl.when(at(lambda n: 0))
        def _():
            hosted.start(h_ins, h_outs, *sems)

        compute(*ins, *outs, *scr)

        @pl.when(at(lambda n: n - 1))
        def _():
            hosted.finish(h_ins, h_outs, *sems)

    hbm = pl.BlockSpec(memory_space=pl.ANY)
    res = pl.pallas_call(
        body, name=name, grid=grid,
        in_specs=list(in_specs) + [hbm] * h_in, out_specs=list(out_specs) + [hbm] * h_out,
        out_shape=list(out_shape) + hosted.outs,
        scratch_shapes=list(scratch_shapes) + [pltpu.SemaphoreType.DMA((hosted.n_pairs,)), pltpu.SemaphoreType.DMA((hosted.n_pairs,)),
                                               pltpu.SemaphoreType.DMA((hosted.n_local,))],
        input_output_aliases={n_in + a: n_out + b for a, b in hosted.aliases.items()},
        compiler_params=params,
    )(*args, *hosted.ins)
    return list(res[:n_out]), list(res[n_out:])


def _dot(a, b):
    return jnp.dot(a, b, preferred_element_type=F32)


def _dot_nt(a, b):
    return lax.dot_general(a, b, (((1,), (1,)), ((), ())), preferred_element_type=F32)


def _dot_tn(a, b):
    return lax.dot_general(a, b, (((0,), (0,)), ((), ())), preferred_element_type=F32)


def _shift_matrix(n, by):
    row = lax.broadcasted_iota(jnp.int32, (n, n), 0)
    col = lax.broadcasted_iota(jnp.int32, (n, n), 1)
    return jnp.where(col == row + by, 1.0, 0.0).astype(BF16)


def _rstd(v):
    return lax.rsqrt(jnp.mean(v * v, axis=-1, keepdims=True) + RMS_EPS)


def _rms_bwd(dy, v, rstd, gain):
    n = v * rstd
    dgain = jnp.sum(dy * n, axis=0, keepdims=True)
    dn = dy * gain
    dv = rstd * (dn - n * jnp.mean(dn * n, axis=-1, keepdims=True))
    return dv, dgain


def _lane_lo(shape):
    return (lax.broadcasted_iota(jnp.int32, shape, 1) % 128) < HEAD_DIM


GROUP = N_ATTN_HEADS // (KV_W // HEAD_DIM)


def _attn_bias(first_chunk):
    qi = lax.broadcasted_iota(jnp.int32, (GROUP * CHUNK, 2 * CHUNK), 0) % CHUNK
    kj = lax.broadcasted_iota(jnp.int32, (GROUP * CHUNK, 2 * CHUNK), 1)
    valid = jnp.logical_and(kj > qi, kj <= qi + CHUNK)
    if first_chunk:
        valid = jnp.logical_and(valid, kj >= CHUNK)
    return jnp.where(valid, 0.0, MASK_VALUE)


def _half(shape, hk):
    lo = _lane_lo(shape)
    return lo if hk == 0 else jnp.logical_not(lo)


def _stack_heads(ref, row0, col0, hk):
    half = _half((CHUNK, 128), hk)
    parts = []
    for j in range(GROUP):
        h = GROUP * hk + j
        pair = ref[row0:row0 + CHUNK, col0 + (h // 2) * 128:col0 + (h // 2 + 1) * 128].astype(F32)
        if h % 2 != hk:
            pair = pltpu.roll(pair, HEAD_DIM, 1)
        parts.append(jnp.where(half, pair, 0.0))
    return jnp.concatenate(parts, axis=0)


def _unstack_heads(stacked, hk):
    pairs = []
    for q in range(GROUP // 2):
        even, odd = stacked[2 * q * CHUNK:(2 * q + 1) * CHUNK], stacked[(2 * q + 1) * CHUNK:(2 * q + 2) * CHUNK]
        pairs.append(even + pltpu.roll(odd, HEAD_DIM, 1) if hk == 0 else pltpu.roll(even, HEAD_DIM, 1) + odd)
    return pairs


def _group_sinks(sk_ref, hk):
    row = lax.broadcasted_iota(jnp.int32, (GROUP * CHUNK, 1), 0)
    col = jnp.full((GROUP * CHUNK, 1), sk_ref[GROUP * hk], F32)
    for j in range(1, GROUP):
        col = jnp.where(row >= j * CHUNK, sk_ref[GROUP * hk + j], col)
    return col


def _attn_probs(q_b, kk_b, bias, sink):
    s = _dot_nt(q_b, kk_b) * ATTN_SCALE + bias
    m = jnp.maximum(jnp.max(s, axis=-1, keepdims=True), sink)
    e = jnp.exp(s - m)
    e_sink = jnp.exp(sink - m)
    inv = 1.0 / (jnp.sum(e, axis=-1, keepdims=True) + e_sink)
    return e * inv, e_sink * inv


def _even_lanes(shape):
    return (lax.broadcasted_iota(jnp.int32, shape, 1) % 2) == 0


def _swap2(v, even):
    return jnp.where(even, pltpu.roll(v, v.shape[1] - 1, 1), pltpu.roll(v, 1, 1))


def _tile4(v):
    return jnp.concatenate([v, v, v, v], axis=-1)


def _sigmoid(v):
    return 1.0 / (1.0 + jnp.exp(-v))


def _ret_constants():
    h = N_RET_HEADS
    log_gamma = jnp.log(1.0 - jnp.power(2.0, -5.0 - jnp.arange(h, dtype=F32)))
    idx = jnp.arange(CHUNK, dtype=F32)
    rel = idx[:, None] - idx[None, :]
    d_intra = jnp.where(rel[None] >= 0, jnp.exp(log_gamma[:, None, None] * jnp.maximum(rel, 0.0)[None]), 0.0)
    xi = jnp.exp(log_gamma[None, :] * (idx[:, None] + 1.0))
    zeta = jnp.exp(log_gamma[None, :] * (CHUNK - 1.0 - idx[:, None]))
    decay = jnp.exp(log_gamma * CHUNK)
    xi_full = jnp.repeat(xi, RET_HEAD_DIM, axis=1)
    zeta_full = jnp.repeat(zeta, RET_HEAD_DIM, axis=1)
    decay_full = jnp.broadcast_to(jnp.repeat(decay, RET_HEAD_DIM)[None, :], (8, RET_W))
    return d_intra.astype(F32), xi_full.astype(F32), zeta_full.astype(F32), decay_full.astype(F32)


def _rope_tables(s):
    pos = jnp.arange(s, dtype=F32)
    angle = 1.0 / jnp.power(10000.0, jnp.linspace(0.0, 1.0, RET_HEAD_DIM // 2, dtype=F32))
    angle = jnp.repeat(angle, 2)
    sign = jnp.where(jnp.arange(RET_HEAD_DIM) % 2 == 0, -1.0, 1.0).astype(F32)
    return jnp.sin(pos[:, None] * angle[None]) * sign[None], jnp.cos(pos[:, None] * angle[None])


def _in_proj(x, gain, w_in, hosted=None):
    s = x.shape[0]
    tm = min(IN_PROJ_TOKEN_TILE, s)

    def body(x_ref, g_ref, w_ref, h_ref, p_ref):
        xv = x_ref[...]
        h = (xv * _rstd(xv) * g_ref[...]).astype(BF16)
        h_ref[...] = h
        p_ref[...] = _dot(h, w_ref[...])

    return _hosted_call(
        body, name="in_proj", grid=(s // tm,),
        in_specs=[pl.BlockSpec((tm, D_MODEL), lambda i: (i, 0)), _resident((1, D_MODEL)), _resident((D_MODEL, IN_W))],
        out_specs=[pl.BlockSpec((tm, D_MODEL), lambda i: (i, 0)), pl.BlockSpec((tm, IN_W), lambda i: (i, 0))],
        out_shape=[jax.ShapeDtypeStruct((s, D_MODEL), BF16), jax.ShapeDtypeStruct((s, IN_W), F32)],
        scratch_shapes=[], args=(x, gain, w_in), hosted=hosted)


def _mixer_fwd(proj, sinks, sin, cos, consts, hosted=None):
    s = proj.shape[0]
    nc = s // CHUNK
    cps = MIXER_CHUNKS_PER_STEP
    d_intra, xi_full, zeta_full, decay_full = consts

    def body(sk_ref, p_ref, pkv_ref, sin_ref, cos_ref, dm_ref, xi_ref, ze_ref, dc_ref, mix_ref, st_ref, state):
        i = pl.program_id(0)

        @pl.when(i == 0)
        def _():
            state[...] = jnp.zeros_like(state)

        st = [state[h] for h in range(N_RET_HEADS)]
        bias_any = _attn_bias(False)
        bias_c0 = jnp.where(i == 0, _attn_bias(True), bias_any)
        even = _even_lanes((CHUNK, RET_W))
        for c in range(cps):
            r0 = c * CHUNK
            rows = slice(r0, r0 + CHUNK)

            kv_cur = p_ref[rows, KV_A0:KV_A0 + 2 * KV_W]
            kv_prev = pkv_ref[...] if c == 0 else p_ref[r0 - CHUNK:r0, KV_A0:KV_A0 + 2 * KV_W]
            kk = jnp.concatenate([kv_prev[:, :KV_W], kv_cur[:, :KV_W]], axis=0)
            vv = jnp.concatenate([kv_prev[:, KV_W:], kv_cur[:, KV_W:]], axis=0)
            kk_b = kk.astype(BF16)
            bias = bias_c0 if c == 0 else bias_any
            for hk in range(KV_W // HEAD_DIM):
                q_b = _stack_heads(p_ref, r0, Q_A0, hk).astype(BF16)
                p, _ = _attn_probs(q_b, kk_b, bias, _group_sinks(sk_ref, hk))
                v_b = jnp.where(_half((2 * CHUNK, 128), hk), vv, 0.0).astype(BF16)
                for q, pair in enumerate(_unstack_heads(_dot(p.astype(BF16), v_b), hk)):
                    pi = (GROUP // 2) * hk + q
                    mix_ref[rows, pi * 128:(pi + 1) * 128] = pair.astype(BF16)

            sin4, cos4 = _tile4(sin_ref[rows, :]), _tile4(cos_ref[rows, :])
            q_r = p_ref[rows, Q_R0:Q_R0 + RET_W]
            k_r = p_ref[rows, K_R0:K_R0 + RET_W] * RET_K_SCALE
            q_r = q_r * cos4 + _swap2(q_r, even) * sin4
            k_r = k_r * cos4 + _swap2(k_r, even) * sin4
            kz = k_r * ze_ref[...]
            for h in range(N_RET_HEADS):
                sl = slice(h * RET_HEAD_DIM, (h + 1) * RET_HEAD_DIM)
                qh, kh = q_r[:, sl].astype(BF16), k_r[:, sl].astype(BF16)
                vh = p_ref[rows, V_R0 + h * RET_HEAD_DIM:V_R0 + (h + 1) * RET_HEAD_DIM].astype(BF16)
                st_ref[c, h] = st[h]
                a = _dot_nt(qh, kh) * dm_ref[h]
                o = _dot(a.astype(BF16), vh) + _dot(qh, st[h].astype(BF16)) * xi_ref[:, sl]
                st[h] = dc_ref[0:1, sl] * st[h] + _dot_tn(kz[:, sl].astype(BF16), vh)
                mu = jnp.mean(o, axis=-1, keepdims=True)
                oc = o - mu
                on = oc * lax.rsqrt(jnp.mean(oc * oc, axis=-1, keepdims=True) + GN_EPS)
                g = p_ref[rows, G_R0 + h * RET_HEAD_DIM:G_R0 + (h + 1) * RET_HEAD_DIM]
                mix_ref[rows, ATTN_W + h * RET_HEAD_DIM:ATTN_W + (h + 1) * RET_HEAD_DIM] = (g * _sigmoid(g) * on).astype(BF16)
        for h in range(N_RET_HEADS):
            state[h] = st[h]

    return _hosted_call(
        body, name="mixer_fwd", grid=(nc // cps,),
        in_specs=[
            pl.BlockSpec(memory_space=pltpu.SMEM),
            pl.BlockSpec((cps * CHUNK, IN_W), lambda i: (i, 0)),
            pl.BlockSpec((CHUNK, 2 * KV_W), lambda i: (jnp.maximum(cps * i - 1, 0), KV_A0 // (2 * KV_W))),
            pl.BlockSpec((cps * CHUNK, RET_HEAD_DIM), lambda i: (i, 0)),
            pl.BlockSpec((cps * CHUNK, RET_HEAD_DIM), lambda i: (i, 0)),
            _resident((N_RET_HEADS, CHUNK, CHUNK)), _resident((CHUNK, RET_W)), _resident((CHUNK, RET_W)), _resident((8, RET_W)),
        ],
        out_specs=[
            pl.BlockSpec((cps * CHUNK, D_MODEL), lambda i: (i, 0)),
            pl.BlockSpec((cps, N_RET_HEADS, RET_HEAD_DIM, RET_HEAD_DIM), lambda i: (i, 0, 0, 0)),
        ],
        out_shape=[jax.ShapeDtypeStruct((s, D_MODEL), BF16),
                   jax.ShapeDtypeStruct((nc, N_RET_HEADS, RET_HEAD_DIM, RET_HEAD_DIM), F32)],
        scratch_shapes=[pltpu.VMEM((N_RET_HEADS, RET_HEAD_DIM, RET_HEAD_DIM), F32)],
        args=(sinks, proj, proj, sin, cos, d_intra, xi_full, zeta_full, decay_full), hosted=hosted)


def _out_up_proj(mix, x, w_out, g_post, g_pre, w_up):
    s = x.shape[0]
    tm = min(BIG_TOKEN_TILE, s)
    blk = UP_W // N_CHIPS

    def body(mix_ref, x_ref, wo_ref, g2_ref, g3_ref, wu_ref, mixed_ref, x1_ref, h2_ref, u0_ref):
        mixed = _dot(mix_ref[...], wo_ref[...])
        mixed_ref[...] = mixed
        x1 = x_ref[...] + mixed * _rstd(mixed) * g2_ref[...]
        x1_ref[...] = x1
        h2 = (x1 * _rstd(x1) * g3_ref[...]).astype(BF16)
        h2_ref[...] = h2
        for k in range(N_CHIPS):
            u0_ref[:, k * blk:(k + 1) * blk] = _dot(h2, wu_ref[k]).astype(BF16)

    tok = lambda w: pl.BlockSpec((tm, w), lambda i: (i, 0))
    return pl.pallas_call(
        body, name="out_up_proj", grid=(s // tm,),
        in_specs=[tok(D_MODEL), tok(D_MODEL), _resident((D_MODEL, D_MODEL)), _resident((1, D_MODEL)), _resident((1, D_MODEL)),
                  _resident((N_CHIPS, D_MODEL, blk))],
        out_specs=[tok(D_MODEL), tok(D_MODEL), tok(D_MODEL), tok(UP_W)],
        out_shape=[jax.ShapeDtypeStruct((s, D_MODEL), F32), jax.ShapeDtypeStruct((s, D_MODEL), F32),
                   jax.ShapeDtypeStruct((s, D_MODEL), BF16), jax.ShapeDtypeStruct((s, UP_W), BF16)],
        compiler_params=_params(("arbitrary",)),
    )(mix, x, w_out, g_post, g_pre, w_up)


def _ffn_tail(u0, x1, target, conv_w, conv_b, w_down, g_post):
    s = x1.shape[0]
    tm = TOKEN_TILE
    last = s // tm - 1
    rb, lanes = FFN_ROW_BLOCK, 128

    def body(u0_ref, x1_ref, t_ref, cw_ref, cb_ref, wd_ref, g_ref,
             y_ref, dy2_ref, dout_ref, du_ref, cacc_ref, gacc_ref, u1_s, u2_s, carry, gelu_s, slope_s, dy_s, cacc):
        i = pl.program_id(0)

        @pl.when(i == 0)
        def _():
            carry[...] = jnp.zeros_like(carry)
            cacc[...] = jnp.zeros_like(cacc)
            gacc_ref[...] = jnp.zeros_like(gacc_ref)

        shift1, shift2 = _shift_matrix(tm, -1), _shift_matrix(tm, -2)
        r8 = lax.broadcasted_iota(jnp.int32, (8, 1), 0)
        wide = 2 * lanes

        def shift_block(col):
            cols = slice(col, col + wide)
            u1_s[:, cols] = _dot(shift1, u0_ref[:, cols])
            u2_s[:, cols] = _dot(shift2, u0_ref[:, cols])
            c14, c15 = carry[14:15, cols], carry[15:16, cols]
            u1_s[0:8, cols] = jnp.where(r8 == 0, c15, u1_s[0:8, cols])
            u2_s[0:8, cols] = jnp.where(r8 == 0, c14, jnp.where(r8 == 1, c15, u2_s[0:8, cols]))

        def taps(col):
            return (cw_ref[0:1, col:col + lanes], cw_ref[1:2, col:col + lanes], cw_ref[2:3, col:col + lanes],
                    cb_ref[0:1, col:col + lanes])

        def shifted(r0, col):
            return (u2_s[r0:r0 + rb, col:col + lanes], u1_s[r0:r0 + rb, col:col + lanes],
                    u0_ref[r0:r0 + rb, col:col + lanes].astype(F32))

        def conv(r0, col, w):
            u2, u1, uc = shifted(r0, col)
            return w[0] * u2 + w[1] * u1 + w[2] * uc + w[3]

        fold = lambda v: jnp.sum(v.reshape(rb // 8, 8, lanes), axis=0)

        shift_block(0)
        shift_block(D_FF)
        for j in range(D_FF // lanes):
            cg, cv = j * lanes, D_FF + j * lanes
            if cg % wide == 0 and cg + wide < D_FF:
                shift_block(cg + wide)
                shift_block(cv + wide)
            wg, wv = taps(cg), taps(cv)
            for r0 in range(0, tm, rb):
                gate, val = conv(r0, cg, wg), conv(r0, cv, wv)
                g2 = gate * gate
                th = jnp.tanh(gate * (GELU_C + GELU_C * GELU_A * g2))
                hp = 0.5 * th + 0.5
                gelu = gate * hp
                dgelu = hp + gate * (1.0 - th * th) * (0.5 * GELU_C + 1.5 * GELU_C * GELU_A * g2)
                y_ref[r0:r0 + rb, cg:cg + lanes] = (gelu * val).astype(BF16)
                gelu_s[r0:r0 + rb, cg:cg + lanes] = gelu
                slope_s[r0:r0 + rb, cg:cg + lanes] = dgelu * val

        y2 = _dot(y_ref[...], wd_ref[...])
        r4 = _rstd(y2)
        gain = g_ref[...]
        out = x1_ref[...] + y2 * r4 * gain
        diff = out - t_ref[...]
        dout = diff * (1.0 / D_MODEL)
        dout_ref[...] = dout
        dy2, dgain = _rms_bwd(dout, y2, r4, gain)
        dy2_b = dy2.astype(BF16)
        dy2_ref[...] = dy2_b
        gacc_ref[0:1, :] += dgain
        gacc_ref[1:2, :] += 0.5 * jnp.sum(diff * dout, axis=0, keepdims=True)
        carry[...] = u0_ref[tm - 16:tm, :].astype(F32)

        dy_s[:, 0:wide] = _dot_nt(dy2_b, wd_ref[0:wide, :])
        for j in range(D_FF // lanes):
            cg, cv = j * lanes, D_FF + j * lanes
            if cg % wide == 0 and cg + wide < D_FF:
                dy_s[:, cg + wide:cg + 2 * wide] = _dot_nt(dy2_b, wd_ref[cg + wide:cg + 2 * wide, :])
            acc = [[jnp.zeros((8, lanes), F32) for _ in range(CONV_WIDTH + 1)] for _ in range(2)]
            for r0 in range(0, tm, rb):
                dy = dy_s[r0:r0 + rb, cg:cg + lanes]
                d_gate = dy * slope_s[r0:r0 + rb, cg:cg + lanes]
                d_val = dy * gelu_s[r0:r0 + rb, cg:cg + lanes]
                for side, (col, d) in enumerate(((cg, d_gate), (cv, d_val))):
                    du_ref[r0:r0 + rb, col:col + lanes] = d.astype(BF16)
                    for k, u in enumerate(shifted(r0, col)):
                        acc[side][k] = acc[side][k] + fold(d * u)
                    acc[side][CONV_WIDTH] = acc[side][CONV_WIDTH] + fold(d)
            for side, col in enumerate((cg, cv)):
                for k in range(CONV_WIDTH + 1):
                    cacc[8 * k:8 * k + 8, col:col + lanes] += acc[side][k]

        @pl.when(i == last)
        def _():
            for k in range(CONV_WIDTH + 1):
                cacc_ref[k:k + 1, :] = jnp.sum(cacc[8 * k:8 * k + 8, :], axis=0, keepdims=True)
            cacc_ref[CONV_WIDTH + 1:8, :] = jnp.zeros((8 - CONV_WIDTH - 1, UP_W), F32)

    tok = lambda w: pl.BlockSpec((tm, w), lambda i: (i, 0))
    return pl.pallas_call(
        body, name="ffn_tail", grid=(s // tm,),
        in_specs=[tok(UP_W), tok(D_MODEL), tok(D_MODEL), _resident((CONV_WIDTH, UP_W)), _resident((1, UP_W)),
                  _resident((D_FF, D_MODEL)), _resident((1, D_MODEL))],
        out_specs=[tok(D_FF), tok(D_MODEL), tok(D_MODEL), tok(UP_W),
                   pl.BlockSpec((8, UP_W), lambda i: (0, 0)), pl.BlockSpec((8, D_MODEL), lambda i: (0, 0))],
        out_shape=[jax.ShapeDtypeStruct((s, D_FF), BF16), jax.ShapeDtypeStruct((s, D_MODEL), BF16),
                   jax.ShapeDtypeStruct((s, D_MODEL), F32), jax.ShapeDtypeStruct((s, UP_W), BF16),
                   jax.ShapeDtypeStruct((8, UP_W), F32), jax.ShapeDtypeStruct((8, D_MODEL), F32)],
        scratch_shapes=[pltpu.VMEM((tm, UP_W), F32), pltpu.VMEM((tm, UP_W), F32), pltpu.VMEM((16, UP_W), F32),
                        pltpu.VMEM((tm, D_FF), F32), pltpu.VMEM((tm, D_FF), F32),
                        pltpu.VMEM((tm, D_FF), F32), pltpu.VMEM((8 * (CONV_WIDTH + 1), UP_W), F32)],
        compiler_params=_params(("arbitrary",)),
    )(u0, x1, target, conv_w, conv_b, w_down, g_post)


def _ffn_head_bwd(du, conv_w, w_up, x1, g_pre, dout, mixed, g_post, w_out):
    s = x1.shape[0]
    tm = TOKEN_TILE
    nt = s // tm
    blk = UP_W // N_CHIPS

    def body(du_ref, halo_ref, cw_ref, wu_ref, x1_ref, g3_ref, dout_ref, mixed_ref, g2_ref, wo_ref,
             du0_ref, dx1_ref, dmixed_ref, dmix_ref, gacc_ref, dbuf):
        i = pl.program_id(0)

        @pl.when(i == 0)
        def _():
            gacc_ref[...] = jnp.zeros_like(gacc_ref)

        dbuf[0:tm, :] = du_ref[...].astype(F32)
        dbuf[tm:tm + 16, :] = jnp.where(i < nt - 1, halo_ref[...].astype(F32), 0.0)
        dh2 = jnp.zeros((tm, D_MODEL), F32)
        for k in range(N_CHIPS):
            for c0 in range(0, blk, HEAD_BWD_COLS):
                width = min(HEAD_BWD_COLS, blk - c0)
                cols = slice(k * blk + c0, k * blk + c0 + width)
                du0_b = (cw_ref[2:3, cols] * dbuf[0:tm, cols] + cw_ref[1:2, cols] * dbuf[1:1 + tm, cols]
                         + cw_ref[0:1, cols] * dbuf[2:2 + tm, cols]).astype(BF16)
                du0_ref[:, cols] = du0_b
                dh2 = dh2 + _dot_nt(du0_b, wu_ref[k, :, c0:c0 + width])
        x1 = x1_ref[...]
        d3, dg3 = _rms_bwd(dh2, x1, _rstd(x1), g3_ref[...])
        dx1 = dout_ref[...] + d3
        dx1_ref[...] = dx1
        mixed = mixed_ref[...]
        dmixed, dg2 = _rms_bwd(dx1, mixed, _rstd(mixed), g2_ref[...])
        dmixed_b = dmixed.astype(BF16)
        dmixed_ref[...] = dmixed_b
        dmix_ref[...] = _dot_nt(dmixed_b, wo_ref[...]).astype(BF16)
        gacc_ref[0:1, :] += dg3
        gacc_ref[1:2, :] += dg2

    tok = lambda w: pl.BlockSpec((tm, w), lambda i: (i, 0))
    halo = pl.BlockSpec((16, UP_W), lambda i: (jnp.minimum(i + 1, nt - 1) * (tm // 16), 0))
    return pl.pallas_call(
        body, name="ffn_head_bwd", grid=(nt,),
        in_specs=[tok(UP_W), halo, _resident((CONV_WIDTH, UP_W)), _resident((N_CHIPS, D_MODEL, blk)), tok(D_MODEL),
                  _resident((1, D_MODEL)), tok(D_MODEL), tok(D_MODEL), _resident((1, D_MODEL)), _resident((D_MODEL, D_MODEL))],
        out_specs=[tok(UP_W), tok(D_MODEL), tok(D_MODEL), tok(D_MODEL), pl.BlockSpec((8, D_MODEL), lambda i: (0, 0))],
        out_shape=[jax.ShapeDtypeStruct((s, UP_W), BF16), jax.ShapeDtypeStruct((s, D_MODEL), F32),
                   jax.ShapeDtypeStruct((s, D_MODEL), BF16), jax.ShapeDtypeStruct((s, D_MODEL), BF16),
                   jax.ShapeDtypeStruct((8, D_MODEL), F32)],
        scratch_shapes=[pltpu.VMEM((tm + 16, UP_W), F32)],
        compiler_params=_params(("arbitrary",)),
    )(du, du, conv_w, w_up, x1, g_pre, dout, mixed, g_post, w_out)


def _mixer_bwd(proj, dmix, states, sinks, sin, cos, consts, hosted=None):
    s = proj.shape[0]
    nc = s // CHUNK
    cps = MIXER_CHUNKS_PER_STEP
    nb = nc // cps
    d_intra, xi_full, zeta_full, decay_full = consts

    def body(sk_ref, p_ref, pkv_ref, dmix_ref, st_ref, sin_ref, cos_ref, dm_ref, xi_ref, ze_ref, dc_ref,
             dp_ref, dsk_ref, gstate, ckv, dsk_acc):
        i = pl.program_id(0)
        block = nb - 1 - i

        @pl.when(i == 0)
        def _():
            gstate[...] = jnp.zeros_like(gstate)
            ckv[...] = jnp.zeros_like(ckv)
            dsk_acc[...] = jnp.zeros_like(dsk_acc)

        gs_all = [gstate[h] for h in range(N_RET_HEADS)]
        later_kv = ckv[...]
        lane = lax.broadcasted_iota(jnp.int32, (CHUNK, 128), 1)
        dsk = jnp.zeros((CHUNK, 128), F32)
        bias_any = _attn_bias(False)
        bias_c0 = jnp.where(block == 0, _attn_bias(True), bias_any)
        even = _even_lanes((CHUNK, RET_W))
        for c in reversed(range(cps)):
            r0 = c * CHUNK
            rows = slice(r0, r0 + CHUNK)

            kv_cur = p_ref[rows, KV_A0:KV_A0 + 2 * KV_W]
            kv_prev = pkv_ref[...] if c == 0 else p_ref[r0 - CHUNK:r0, KV_A0:KV_A0 + 2 * KV_W]
            kk = jnp.concatenate([kv_prev[:, :KV_W], kv_cur[:, :KV_W]], axis=0)
            vv = jnp.concatenate([kv_prev[:, KV_W:], kv_cur[:, KV_W:]], axis=0)
            kk_b, vv_b = kk.astype(BF16), vv.astype(BF16)
            bias = bias_c0 if c == 0 else bias_any
            dkk = jnp.zeros((2 * CHUNK, KV_W), F32)
            dvv = jnp.zeros((2 * CHUNK, KV_W), F32)
            for hk in range(KV_W // HEAD_DIM):
                q_b = _stack_heads(p_ref, r0, Q_A0, hk).astype(BF16)
                do_b = _stack_heads(dmix_ref, r0, 0, hk).astype(BF16)
                p, p_sink = _attn_probs(q_b, kk_b, bias, _group_sinks(sk_ref, hk))
                dpr = _dot_nt(do_b, vv_b)
                delta = jnp.sum(p * dpr, axis=-1, keepdims=True)
                ds_b = (p * (dpr - delta) * ATTN_SCALE).astype(BF16)
                dsink = -p_sink * delta
                for j in range(GROUP):
                    dsk = dsk + jnp.where(lane == GROUP * hk + j, dsink[j * CHUNK:(j + 1) * CHUNK], 0.0)
                k_b = jnp.where(_half((2 * CHUNK, 128), hk), kk, 0.0).astype(BF16)
                for q, pair in enumerate(_unstack_heads(_dot(ds_b, k_b), hk)):
                    pi = (GROUP // 2) * hk + q
                    dp_ref[rows, Q_A0 + pi * 128:Q_A0 + (pi + 1) * 128] = pair.astype(BF16)
                dkk = dkk + _dot_tn(ds_b, q_b)
                dvv = dvv + _dot_tn(p.astype(BF16), do_b)
            dp_ref[rows, KV_A0:KV_A0 + KV_W] = (dkk[CHUNK:] + later_kv[:, :KV_W]).astype(BF16)
            dp_ref[rows, KV_A0 + KV_W:KV_A0 + 2 * KV_W] = (dvv[CHUNK:] + later_kv[:, KV_W:]).astype(BF16)
            later_kv = jnp.concatenate([dkk[:CHUNK], dvv[:CHUNK]], axis=1)

            sin4, cos4 = _tile4(sin_ref[rows, :]), _tile4(cos_ref[rows, :])
            q_r = p_ref[rows, Q_R0:Q_R0 + RET_W]
            k_r = p_ref[rows, K_R0:K_R0 + RET_W] * RET_K_SCALE
            q_r = q_r * cos4 + _swap2(q_r, even) * sin4
            k_r = k_r * cos4 + _swap2(k_r, even) * sin4
            kz = k_r * ze_ref[...]
            dq_parts, dk_parts = [], []
            for h in range(N_RET_HEADS):
                sl = slice(h * RET_HEAD_DIM, (h + 1) * RET_HEAD_DIM)
                qh, kh = q_r[:, sl].astype(BF16), k_r[:, sl].astype(BF16)
                vh = p_ref[rows, V_R0 + h * RET_HEAD_DIM:V_R0 + (h + 1) * RET_HEAD_DIM].astype(BF16)
                st_b = st_ref[c, h].astype(BF16)
                gs = gs_all[h]
                gs_b = gs.astype(BF16)
                xi_h = xi_ref[:, sl]
                dm = dm_ref[h]
                a_b = (_dot_nt(qh, kh) * dm).astype(BF16)
                o = _dot(a_b, vh) + _dot(qh, st_b) * xi_h
                mu = jnp.mean(o, axis=-1, keepdims=True)
                oc = o - mu
                rs = lax.rsqrt(jnp.mean(oc * oc, axis=-1, keepdims=True) + GN_EPS)
                on = oc * rs
                g = p_ref[rows, G_R0 + h * RET_HEAD_DIM:G_R0 + (h + 1) * RET_HEAD_DIM]
                sg = _sigmoid(g)
                dr = dmix_ref[rows, ATTN_W + h * RET_HEAD_DIM:ATTN_W + (h + 1) * RET_HEAD_DIM].astype(F32)
                dp_ref[rows, G_R0 + h * RET_HEAD_DIM:G_R0 + (h + 1) * RET_HEAD_DIM] = (
                    dr * on * (sg * (1.0 + g * (1.0 - sg)))).astype(BF16)
                don = dr * g * sg
                do = rs * (don - jnp.mean(don, axis=-1, keepdims=True) - on * jnp.mean(don * on, axis=-1, keepdims=True))
                do_b = do.astype(BF16)
                dox_b = (do * xi_h).astype(BF16)
                da_b = (_dot_nt(do_b, vh) * dm).astype(BF16)
                dq_parts.append(_dot(da_b, kh) + _dot_nt(dox_b, st_b))
                dk_parts.append(_dot_tn(da_b, qh) + ze_ref[:, sl] * _dot_nt(vh, gs_b))
                dv = _dot_tn(a_b, do_b) + _dot(kz[:, sl].astype(BF16), gs_b)
                dp_ref[rows, V_R0 + h * RET_HEAD_DIM:V_R0 + (h + 1) * RET_HEAD_DIM] = dv.astype(BF16)
                gs_all[h] = dc_ref[0:1, sl] * gs + _dot_tn(qh, dox_b)
            dq = jnp.concatenate(dq_parts, axis=-1)
            dk = jnp.concatenate(dk_parts, axis=-1)
            dp_ref[rows, Q_R0:Q_R0 + RET_W] = (dq * cos4 - _swap2(dq, even) * sin4).astype(BF16)
            dp_ref[rows, K_R0:K_R0 + RET_W] = (RET_K_SCALE * (dk * cos4 - _swap2(dk, even) * sin4)).astype(BF16)

        for h in range(N_RET_HEADS):
            gstate[h] = gs_all[h]
        ckv[...] = later_kv
        dsk_acc[...] += dsk

        @pl.when(i == nb - 1)
        def _():
            dsk_ref[...] = jnp.sum(dsk_acc[...], axis=0, keepdims=True)

    rev = lambda i: nb - 1 - i
    return _hosted_call(
        body, name="mixer_bwd", grid=(nb,),
        in_specs=[
            pl.BlockSpec(memory_space=pltpu.SMEM),
            pl.BlockSpec((cps * CHUNK, IN_W), lambda i: (rev(i), 0)),
            pl.BlockSpec((CHUNK, 2 * KV_W), lambda i: (jnp.maximum(cps * rev(i) - 1, 0), KV_A0 // (2 * KV_W))),
            pl.BlockSpec((cps * CHUNK, D_MODEL), lambda i: (rev(i), 0)),
            pl.BlockSpec((cps, N_RET_HEADS, RET_HEAD_DIM, RET_HEAD_DIM), lambda i: (rev(i), 0, 0, 0)),
            pl.BlockSpec((cps * CHUNK, RET_HEAD_DIM), lambda i: (rev(i), 0)),
            pl.BlockSpec((cps * CHUNK, RET_HEAD_DIM), lambda i: (rev(i), 0)),
            _resident((N_RET_HEADS, CHUNK, CHUNK)), _resident((CHUNK, RET_W)), _resident((CHUNK, RET_W)), _resident((8, RET_W)),
        ],
        out_specs=[pl.BlockSpec((cps * CHUNK, IN_W), lambda i: (rev(i), 0)), pl.BlockSpec((1, 128), lambda i: (0, 0))],
        out_shape=[jax.ShapeDtypeStruct((s, IN_W), BF16), jax.ShapeDtypeStruct((1, 128), F32)],
        scratch_shapes=[pltpu.VMEM((N_RET_HEADS, RET_HEAD_DIM, RET_HEAD_DIM), F32), pltpu.VMEM((CHUNK, 2 * KV_W), F32),
                        pltpu.VMEM((CHUNK, 128), F32)],
        args=(sinks, proj, proj, dmix, states, sin, cos, d_intra, xi_full, zeta_full, decay_full), hosted=hosted)


def _in_proj_bwd(dproj, w_in, x, gain, dx1, hosted=None):
    s = x.shape[0]
    tm = min(BIG_TOKEN_TILE, s)

    def body(dp_ref, w_ref, x_ref, g_ref, dx1_ref, dx_ref, gacc_ref):
        @pl.when(pl.program_id(0) == 0)
        def _():
            gacc_ref[...] = jnp.zeros_like(gacc_ref)

        dh = _dot_nt(dp_ref[...], w_ref[...])
        xv = x_ref[...]
        d1, dg = _rms_bwd(dh, xv, _rstd(xv), g_ref[...])
        dx_ref[...] = dx1_ref[...] + d1
        gacc_ref[0:1, :] += dg

    tok = lambda w: pl.BlockSpec((tm, w), lambda i: (i, 0))
    return _hosted_call(
        body, name="in_proj_bwd", grid=(s // tm,),
        in_specs=[tok(IN_W), _resident((D_MODEL, IN_W)), tok(D_MODEL), _resident((1, D_MODEL)), tok(D_MODEL)],
        out_specs=[tok(D_MODEL), pl.BlockSpec((8, D_MODEL), lambda i: (0, 0))],
        out_shape=[jax.ShapeDtypeStruct((s, D_MODEL), F32), jax.ShapeDtypeStruct((8, D_MODEL), F32)],
        scratch_shapes=[], args=(dproj, w_in, x, gain, dx1), hosted=hosted)


def _weight_grad(a, b, tn, name, by_block=False, hosted=None):
    s, m = a.shape
    n = b.shape[1]
    tk = min(WEIGHT_GRAD_TOKENS if m <= D_MODEL else WEIGHT_GRAD_TOKENS // 2, s)

    def body(a_ref, b_ref, o_ref):
        @pl.when(pl.program_id(1) == 0)
        def _():
            o_ref[...] = jnp.zeros_like(o_ref)

        o_ref[...] += _dot_tn(a_ref[...], b_ref[...])

    if by_block:
        out_spec = pl.BlockSpec((None, m, tn), lambda j, k: (j, 0, 0))
        out_shape = jax.ShapeDtypeStruct((n // tn, m, tn), F32)
    else:
        out_spec = pl.BlockSpec((m, tn), lambda j, k: (0, j))
        out_shape = jax.ShapeDtypeStruct((m, n), F32)
    (out,), lands = _hosted_call(
        body, name=name, grid=(n // tn, s // tk),
        in_specs=[pl.BlockSpec((tk, m), lambda j, k: (k, 0)), pl.BlockSpec((tk, tn), lambda j, k: (k, j))],
        out_specs=[out_spec], out_shape=[out_shape], scratch_shapes=[], args=(a, b), hosted=hosted)
    return out if hosted is None else (out, lands)


def _place():
    return lax.axis_index("x"), lax.axis_index("y"), lax.axis_index("c")


def _remote(src, dst, send_sems, recv_sems, k, to):
    return pltpu.make_async_remote_copy(src_ref=src, dst_ref=dst, send_sem=send_sems.at[k], recv_sem=recv_sems.at[k],
                                        device_id=to, device_id_type=MESH)


def _gather_level1_copies(w_refs, out_refs, send_sems, recv_sems, local_sems):
    x, y, c = _place()
    mine_at = 2 * x + y
    peers = [(x, y, 1 - c), (1 - x, y, c), (x, 1 - y, c), (1 - x, 1 - y, c)]
    local, sends, recvs = [], [], []
    for i, (w, out) in enumerate(zip(w_refs, out_refs)):
        half = w.shape[0] // 2
        src = w.at[pl.ds(pl.multiple_of(c * half, 16), half), :]
        mine = out.at[mine_at, c]
        local.append(pltpu.make_async_copy(src, mine, local_sems.at[i]))
        for k, p in enumerate(peers):
            sends.append(_remote(src, mine, send_sems, recv_sems, 4 * i + k, p))
            lands = out.at[mine_at, 1 - c] if k == 0 else out.at[2 * p[0] + p[1], c]
            recvs.append(_remote(src, lands, send_sems, recv_sems, 4 * i + k, p))
    return local, sends, recvs


def _gather_level1_start(w_refs, out_refs, send_sems, recv_sems, local_sems):
    local, sends, _ = _gather_level1_copies(w_refs, out_refs, send_sems, recv_sems, local_sems)
    for cp in local + sends:
        cp.start()


def _gather_level1_finish(w_refs, out_refs, send_sems, recv_sems, local_sems):
    local, sends, recvs = _gather_level1_copies(w_refs, out_refs, send_sems, recv_sems, local_sems)
    for cp in recvs:
        cp.wait_recv()
    for cp in sends:
        cp.wait_send()
    for cp in local:
        cp.wait()


def _gather_level2_copies(in_refs, out_refs, send_sems, recv_sems, local_sems):
    x, y, c = _place()
    chips = [(1 - x, y), (x, 1 - y), (1 - x, 1 - y)]
    sends, recvs = [], []
    for i, (src, out) in enumerate(zip(in_refs, out_refs)):
        for j, (px, py) in enumerate(chips):
            sends.append(_remote(src.at[2 * px + py, c], out.at[2 * px + py, c], send_sems, recv_sems, 3 * i + j, (x, y, 1 - c)))
            recvs.append(_remote(src.at[2 * px + py, c], out.at[2 * px + py, 1 - c], send_sems, recv_sems, 3 * i + j,
                                 (x, y, 1 - c)))
    return sends, recvs


def _gather_level2_start(in_refs, out_refs, send_sems, recv_sems, local_sems):
    for cp in _gather_level2_copies(in_refs, out_refs, send_sems, recv_sems, local_sems)[0]:
        cp.start()


def _gather_level2_finish(in_refs, out_refs, send_sems, recv_sems, local_sems):
    sends, recvs = _gather_level2_copies(in_refs, out_refs, send_sems, recv_sems, local_sems)
    for cp in recvs:
        cp.wait_recv()
    for cp in sends:
        cp.wait_send()


def _gathered_shape(w):
    r, cols = w.shape
    return jax.ShapeDtypeStruct((N_CHIPS, 2, r // 2, cols), w.dtype)


def _hosted_gather_level1(shards):
    n = len(shards)
    return _Hosted(shards, [_gathered_shape(w) for w in shards], {}, 4 * n, n, _gather_level1_start, _gather_level1_finish)


def _hosted_gather_level2(gathered):
    n = len(gathered)
    return _Hosted(gathered, [jax.ShapeDtypeStruct(g.shape, g.dtype) for g in gathered], {i: i for i in range(n)}, 3 * n, 0,
                   _gather_level2_start, _gather_level2_finish)


def _gather_now(w, name):
    def body(w_ref, out_ref, send1, recv1, local1, send2, recv2):
        _gather_level1_start([w_ref], [out_ref], send1, recv1, local1)
        _gather_level1_finish([w_ref], [out_ref], send1, recv1, local1)
        _gather_level2_start([out_ref], [out_ref], send2, recv2, None)
        _gather_level2_finish([out_ref], [out_ref], send2, recv2, None)

    return pl.pallas_call(
        body, name=name, out_shape=_gathered_shape(w),
        in_specs=[pl.BlockSpec(memory_space=pl.ANY)], out_specs=pl.BlockSpec(memory_space=pl.ANY),
        scratch_shapes=[pltpu.SemaphoreType.DMA((4,)), pltpu.SemaphoreType.DMA((4,)), pltpu.SemaphoreType.DMA((1,)),
                        pltpu.SemaphoreType.DMA((3,)), pltpu.SemaphoreType.DMA((3,))],
    )(w)


def _scatter_copies(g_refs, land_refs, send_sems, recv_sems, local_sems):
    x, y, c = _place()
    copies = []
    for i, (g, land) in enumerate(zip(g_refs, land_refs)):
        for k, (px, py, pc) in enumerate(_relations(x, y, c)):
            copies.append(_remote(g.at[2 * px + py, pc], land.at[k], send_sems, recv_sems, 7 * i + k, (px, py, pc)))
    return copies


def _scatter_start(g_refs, land_refs, send_sems, recv_sems, local_sems):
    for cp in _scatter_copies(g_refs, land_refs, send_sems, recv_sems, local_sems):
        cp.start()


def _scatter_finish(g_refs, land_refs, send_sems, recv_sems, local_sems):
    for cp in _scatter_copies(g_refs, land_refs, send_sems, recv_sems, local_sems):
        cp.wait()


def _hosted_scatter(grads):
    lands = [jax.ShapeDtypeStruct((N_DEV - 1,) + g.shape[2:], g.dtype) for g in grads]
    return _Hosted(grads, lands, {}, 7 * len(grads), 0, _scatter_start, _scatter_finish)


def _relations(x, y, c):
    rel = []
    for fx in (0, 1):
        for fy in (0, 1):
            for fc in (0, 1):
                if fx or fy or fc:
                    rel.append(((1 - x) if fx else x, (1 - y) if fy else y, (1 - c) if fc else c))
    return rel


def _gather_small(v, name):
    r, cols = v.shape

    def body(v_ref, out_ref, send_sems, recv_sems):
        x, y, c = _place()
        peers = _relations(x, y, c)

        def slot(p):
            return out_ref.at[4 * p[0] + 2 * p[1] + p[2]]

        out_ref[4 * x + 2 * y + c] = v_ref[...]
        sends = [pltpu.make_async_remote_copy(
            src_ref=v_ref, dst_ref=slot((x, y, c)), send_sem=send_sems.at[k], recv_sem=recv_sems.at[k],
            device_id=p, device_id_type=MESH) for k, p in enumerate(peers)]
        for cp in sends:
            cp.start()
        for k, p in enumerate(peers):
            pltpu.make_async_remote_copy(
                src_ref=v_ref, dst_ref=slot(p), send_sem=send_sems.at[k], recv_sem=recv_sems.at[k],
                device_id=p, device_id_type=MESH).wait_recv()
        for cp in sends:
            cp.wait_send()

    return pl.pallas_call(
        body, name=name,
        out_shape=jax.ShapeDtypeStruct((N_DEV, r, cols), v.dtype),
        in_specs=[pl.BlockSpec(memory_space=pltpu.VMEM)],
        out_specs=pl.BlockSpec(memory_space=pltpu.VMEM),
        scratch_shapes=[pltpu.SemaphoreType.DMA((7,)), pltpu.SemaphoreType.DMA((7,))],
    )(v)


def _join_halves(shards):
    n = len(shards)

    def body(*refs):
        in_refs, out_refs = refs[:n], refs[n:2 * n]
        send_sems, recv_sems = refs[2 * n:]
        x, y, c = _place()
        sends = [_remote(src.at[c], out.at[c], send_sems, recv_sems, i, (x, y, 1 - c))
                 for i, (src, out) in enumerate(zip(in_refs, out_refs))]
        recvs = [_remote(src.at[c], out.at[1 - c], send_sems, recv_sems, i, (x, y, 1 - c))
                 for i, (src, out) in enumerate(zip(in_refs, out_refs))]
        for cp in sends:
            cp.start()
        for cp in recvs:
            cp.wait_recv()
        for cp in sends:
            cp.wait_send()

    hbm = pl.BlockSpec(memory_space=pl.ANY)
    return pl.pallas_call(
        body, name="grad_join_halves",
        out_shape=[jax.ShapeDtypeStruct(t.shape, t.dtype) for t in shards],
        in_specs=[hbm] * n, out_specs=[hbm] * n, input_output_aliases={i: i for i in range(n)},
        scratch_shapes=[pltpu.SemaphoreType.DMA((n,)), pltpu.SemaphoreType.DMA((n,))],
    )(*shards)


def _row_tile(rows, row_bytes, limit=1 << 20):
    best = 8
    for t in range(8, rows + 1, 8):
        if rows % t == 0 and t * row_bytes <= limit:
            best = t
    return best


def _sum_pieces(g, land, place, name):
    _, _, rh, cols = g.shape
    tr = _row_tile(rh, (N_DEV - 1) * cols * 4, 4 << 20)

    def body(p_ref, g_ref, l_ref, out_ref):
        acc = g_ref[...]
        for k in range(N_DEV - 1):
            acc = acc + l_ref[k].astype(F32)
        out_ref[...] = acc

    return pl.pallas_call(
        body, name=name,
        grid_spec=pltpu.PrefetchScalarGridSpec(
            num_scalar_prefetch=1, grid=(rh // tr,),
            in_specs=[pl.BlockSpec((None, None, tr, cols), lambda r, p: (p[0], p[1], r, 0)),
                      pl.BlockSpec((N_DEV - 1, tr, cols), lambda r, p: (0, r, 0))],
            out_specs=pl.BlockSpec((None, tr, cols), lambda r, p: (p[1], r, 0))),
        out_shape=jax.ShapeDtypeStruct((2, rh, cols), g.dtype),
        compiler_params=_params(("arbitrary",)),
    )(place, g, land)


def _adamw_math(w, g, m, v):
    m = ADAM_B1 * m + (1.0 - ADAM_B1) * g
    v = ADAM_B2 * v + (1.0 - ADAM_B2) * (g * g)
    m_hat = m / (1.0 - ADAM_B1 ** ADAM_STEP)
    v_hat = v / (1.0 - ADAM_B2 ** ADAM_STEP)
    delta = -ADAM_LR * (m_hat / (jnp.sqrt(v_hat) + ADAM_EPS) + ADAM_WD * w)
    return delta, m, v


def _adamw(w, g, m, v, name):
    r, cols = w.shape
    tr = _row_tile(r, cols * 4)

    def body(w_ref, g_ref, m_ref, v_ref, d_ref, nm_ref, nv_ref):
        d_ref[...], nm_ref[...], nv_ref[...] = _adamw_math(w_ref[...], g_ref[...], m_ref[...], v_ref[...])

    blk = pl.BlockSpec((tr, cols), lambda i: (i, 0))
    shape = jax.ShapeDtypeStruct((r, cols), F32)
    return pl.pallas_call(
        body, name=name, grid=(r // tr,), in_specs=[blk] * 4, out_specs=[blk] * 3, out_shape=[shape] * 3,
        compiler_params=_params(("arbitrary",)),
    )(w, g, m, v)


def _sum_devices(gathered):
    _, r, cols = gathered.shape

    def body(a_ref, g_ref):
        g = a_ref[0]
        for k in range(1, N_DEV):
            g = g + a_ref[k]
        g_ref[...] = g

    return pl.pallas_call(body, name="sum_small_grads", out_shape=jax.ShapeDtypeStruct((r, cols), F32))(gathered)


def _pack_conv(cw):
    flat = cw.reshape(-1)
    return jnp.pad(flat, (0, ROWS_CONV * D_MODEL - flat.shape[0])).reshape(ROWS_CONV, D_MODEL)


def _unpack_conv(rows):
    return rows.reshape(-1)[:CONV_WIDTH * UP_W // N_CHIPS].reshape(CONV_WIDTH, UP_W // N_CHIPS)


def _columns_to_shards(w):
    r, n = w.shape
    return jnp.transpose(w.reshape(r, N_CHIPS, n // N_CHIPS), (1, 0, 2))


def _shards_to_columns(w):
    _, r, n = w.shape
    return jnp.transpose(w, (1, 0, 2)).reshape(r, N_CHIPS * n)


def _pack_small(g_mix_pre, g_mix_post, g_ffn_pre, g_ffn_post, sinks, conv_b, loss):
    pad_row = lambda v: jnp.pad(v.reshape(1, -1), ((0, 0), (0, D_MODEL - v.size)))
    cb = jnp.pad(conv_b.reshape(-1), (0, 6 * D_MODEL - UP_W)).reshape(6, D_MODEL)
    zeros2 = jnp.zeros((2, D_MODEL), F32)
    return jnp.concatenate([g_mix_pre.reshape(1, -1), g_mix_post.reshape(1, -1), g_ffn_pre.reshape(1, -1),
                            g_ffn_post.reshape(1, -1), pad_row(sinks), pad_row(loss), zeros2, cb, zeros2], axis=0)


def _unpack_small(p):
    return dict(mix_pre_norm=p[0:1], mix_post_norm=p[1:2], ffn_pre_norm=p[2:3], ffn_post_norm=p[3:4],
                attn_sinks=p[4:5, :N_ATTN_HEADS], loss=p[5, 0], conv_b=p[8:14].reshape(1, -1)[:, :UP_W],
                conv_w=_unpack_conv(p[SMALL_ROWS:SMALL_ROWS + ROWS_CONV]))


def _local_step(x, target, g_mix_pre, w_in, sinks, w_out, g_mix_post, g_ffn_pre, w_up, conv_w, conv_b, w_down, g_ffn_post,
                distributed=True):
    s = x.shape[0]
    consts = _ret_constants()
    sin, cos = _rope_tables(s)

    by_half = lambda g, rows: g.reshape(N_CHIPS, 2, rows // (2 * N_CHIPS), g.shape[-1])

    if distributed:
        (h1, proj), level1 = _in_proj(x, g_mix_pre, w_in, _hosted_gather_level1([w_out, w_up, w_down]))
        (mix, states), (w_out, w_up, w_down) = _mixer_fwd(proj, sinks, sin, cos, consts, _hosted_gather_level2(level1))
        w_out, w_down = w_out.reshape(D_MODEL, D_MODEL), w_down.reshape(D_FF, D_MODEL)
        w_up = w_up.reshape(N_CHIPS, D_MODEL, UP_W // N_CHIPS)
    else:
        (h1, proj), _ = _in_proj(x, g_mix_pre, w_in)
        (mix, states), _ = _mixer_fwd(proj, sinks, sin, cos, consts)
    mixed, x1, h2, u0 = _out_up_proj(mix, x, w_out, g_mix_post, g_ffn_pre, w_up)
    y, dy2, dout, du, conv_acc, tail_acc = _ffn_tail(u0, x1, target, conv_w, conv_b, w_down, g_ffn_post)
    du0, dx1, dmixed, dmix, head_acc = _ffn_head_bwd(du, conv_w, w_up, x1, g_ffn_pre, dout, mixed, g_mix_post, w_out)

    d_w_down = _weight_grad(y, dy2, 512, "grad_w_down")
    d_w_up = _weight_grad(h2, du0, UP_W // N_CHIPS, "grad_w_up", by_block=True)
    d_w_out = _weight_grad(mix, dmixed, D_MODEL, "grad_w_out")
    early = [by_half(d_w_down, D_FF), by_half(d_w_up, N_CHIPS * D_MODEL), by_half(d_w_out, D_MODEL)]
    (dproj, dsinks), early_lands = _mixer_bwd(proj, dmix, states, sinks, sin, cos, consts,
                                              _hosted_scatter(early) if distributed else None)
    d_w_in = _columns_to_shards(_weight_grad(h1, dproj, IN_W // 2, "grad_w_in"))
    late = [by_half(d_w_in, N_CHIPS * D_MODEL)]
    (grad_x, in_acc), late_lands = _in_proj_bwd(dproj, w_in, x, g_mix_pre, dx1, _hosted_scatter(late) if distributed else None)

    small = _pack_small(in_acc[0], head_acc[1], head_acc[0], tail_acc[0], dsinks[0, :N_ATTN_HEADS], conv_acc[3],
                        jnp.sum(tail_acc[1]))
    d_conv = jnp.pad(conv_acc[0:CONV_WIDTH].reshape(-1), (0, CONV_FULL_ROWS * D_MODEL - CONV_WIDTH * UP_W))
    small = jnp.concatenate([small, d_conv.reshape(CONV_FULL_ROWS, D_MODEL)], axis=0)
    grads = dict(w_down=early[0], w_up=early[1], w_out=early[2], w_in=late[0])
    lands = dict(zip(["w_down", "w_up", "w_out", "w_in"], early_lands + late_lands))
    return grad_x, grads, lands, small


def kernel(x, mix_pre_norm, w_in, attn_sinks, w_out, mix_post_norm, ffn_pre_norm, w_up, conv_w, conv_b, w_down, ffn_post_norm, loss_target, m_mix_pre_norm, m_w_in, m_attn_sinks, m_w_out, m_mix_post_norm, m_ffn_pre_norm, m_w_up, m_conv_w, m_conv_b, m_w_down, m_ffn_post_norm, v_mix_pre_norm, v_w_in, v_attn_sinks, v_w_out, v_mix_post_norm, v_ffn_pre_norm, v_w_up, v_conv_w, v_conv_b, v_w_down, v_ffn_post_norm):
    cx, cy, cc = _place()
    shard = 2 * cx + cy

    w_in_all = _gather_now(w_in[0].astype(BF16), "gather_w_in").reshape(N_CHIPS, D_MODEL, IN_W // N_CHIPS)
    conv_all = _gather_small(_pack_conv(conv_w[0]), "gather_conv_w")
    conv_full = jnp.concatenate([_unpack_conv(conv_all[2 * k]) for k in range(N_CHIPS)], axis=1)

    grad_x, grads, lands, small = _local_step(
        x[0], loss_target[0], mix_pre_norm, _shards_to_columns(w_in_all), attn_sinks.reshape(-1), w_out[0].astype(BF16),
        mix_post_norm, ffn_pre_norm, w_up[0].astype(BF16), conv_full, conv_b, w_down[0].astype(BF16), ffn_post_norm)

    place = jnp.stack([shard, cc]).astype(jnp.int32)
    mats = ["w_in", "w_out", "w_up", "w_down"]
    halves = [_sum_pieces(grads[n], lands[n], place, "sum_grad_" + n) for n in mats]
    weights = dict(w_in=(w_in, m_w_in, v_w_in), w_out=(w_out, m_w_out, v_w_out), w_up=(w_up, m_w_up, v_w_up),
                   w_down=(w_down, m_w_down, v_w_down))
    mat_out = {}
    for n, joined in zip(mats, _join_halves(halves)):
        w, m, v = weights[n]
        g = joined.reshape(w.shape[1:])
        mat_out[n] = (g,) + tuple(_adamw(w[0], g, m[0], v[0], "adamw_" + n))

    small_sum = _sum_devices(_gather_small(small, "gather_small_grads"))
    d_conv_full = small_sum[SMALL_ROWS:].reshape(-1)[:CONV_WIDTH * UP_W].reshape(CONV_WIDTH, UP_W)
    d_conv_mine = lax.dynamic_slice_in_dim(d_conv_full, shard * (UP_W // N_CHIPS), UP_W // N_CHIPS, axis=1)
    g_s = jnp.concatenate([small_sum[:SMALL_ROWS], _pack_conv(d_conv_mine)], axis=0)
    zero = jnp.zeros((), F32)
    pack_rep = lambda a, b, c_, d, e, f, cw: jnp.concatenate([_pack_small(a, b, c_, d, e, f, zero), _pack_conv(cw[0])], axis=0)
    w_s = pack_rep(mix_pre_norm, mix_post_norm, ffn_pre_norm, ffn_post_norm, attn_sinks, conv_b, conv_w)
    m_s = pack_rep(m_mix_pre_norm, m_mix_post_norm, m_ffn_pre_norm, m_ffn_post_norm, m_attn_sinks, m_conv_b, m_conv_w)
    v_s = pack_rep(v_mix_pre_norm, v_mix_post_norm, v_ffn_pre_norm, v_ffn_post_norm, v_attn_sinks, v_conv_b, v_conv_w)
    delta_s, new_m_s, new_v_s = _adamw(w_s, g_s, m_s, v_s, "adamw_small")

    names = ["mix_pre_norm", "w_in", "attn_sinks", "w_out", "mix_post_norm", "ffn_pre_norm", "w_up", "conv_w", "conv_b",
             "w_down", "ffn_post_norm"]

    def leaves(which, packed_small):
        smalls = _unpack_small(packed_small)
        return [mat_out[n][which][None] if n in mat_out else (smalls[n][None] if n == "conv_w" else smalls[n]) for n in names]

    loss = _unpack_small(g_s)["loss"]
    return (loss, grad_x[None], *leaves(0, g_s), *leaves(1, delta_s), *leaves(2, new_m_s), *leaves(3, new_v_s))
```

```python
import math

import jax
import jax.numpy as jnp
from jax import lax
from jax.experimental import pallas as pl
from jax.experimental.pallas import tpu as pltpu

F32 = jnp.float32
BF16 = jnp.bfloat16

D_MODEL = 1024
HEAD_DIM = 64
ATTN_W = 512
N_ATTN_HEADS = 8
KV_W = 128
RET_W = 512
N_RET_HEADS = 4
RET_HEAD_DIM = 128
CHUNK = 128
IN_W = 2816
D_FF = 2816
UP_W = 2 * D_FF
CONV_WIDTH = 3
RMS_EPS = 1e-6
GN_EPS = 1e-6
MASK_VALUE = -1e30
ATTN_SCALE = HEAD_DIM ** -0.5
RET_K_SCALE = RET_HEAD_DIM ** -0.5
GELU_C = math.sqrt(2.0 / math.pi)
GELU_A = 0.044715

ADAM_LR = 0.001
ADAM_B1 = 0.9
ADAM_B2 = 0.999
ADAM_EPS = 1e-08
ADAM_WD = 0.01
ADAM_STEP = 10

N_CHIPS = 4
N_DEV = 8
MESH = pl.DeviceIdType.MESH
VMEM_LIMIT_V7X = 56 * 1024 * 1024
TOKEN_TILE = 256
BIG_TOKEN_TILE = 512
IN_PROJ_TOKEN_TILE = 1024
WEIGHT_GRAD_TOKENS = 2048
FFN_ROW_BLOCK = 64
HEAD_BWD_COLS = 512
MIXER_CHUNKS_PER_STEP = 4
Q_A0, KV_A0, Q_R0, K_R0, V_R0, G_R0 = 0, 512, 768, 1280, 1792, 2304

ROWS_CONV = 8
SMALL_ROWS = 16
CONV_FULL_ROWS = 24


def _params(sem=None, **kw):
    if sem is not None:
        kw["dimension_semantics"] = sem
    return pltpu.CompilerParams(vmem_limit_bytes=VMEM_LIMIT_V7X, **kw)


def _resident(shape):
    zeros = (0,) * len(shape)
    return pl.BlockSpec(shape, lambda *_: zeros, pipeline_mode=pl.Buffered(1))


class _Hosted:
    def __init__(self, ins, outs, aliases, n_pairs, n_local, start, finish):
        self.ins, self.outs, self.aliases = list(ins), list(outs), dict(aliases)
        self.n_pairs, self.n_local, self.start, self.finish = n_pairs, max(n_local, 1), start, finish


def _hosted_call(compute, *, name, grid, in_specs, out_specs, out_shape, scratch_shapes, args, hosted=None):
    params = _params(("arbitrary",) * len(grid))
    if hosted is None:
        res = pl.pallas_call(compute, name=name, grid=grid, in_specs=in_specs, out_specs=out_specs, out_shape=out_shape,
                             scratch_shapes=scratch_shapes, compiler_params=params)(*args)
        return list(res), []
    n_in, n_out, n_scr = len(in_specs), len(out_specs), len(scratch_shapes)
    h_in, h_out = len(hosted.ins), len(hosted.outs)

    def at(step_of):
        cond = pl.program_id(0) == step_of(grid[0])
        for d in range(1, len(grid)):
            cond = jnp.logical_and(cond, pl.program_id(d) == step_of(grid[d]))
        return cond

    def body(*refs):
        ins, refs = refs[:n_in], refs[n_in:]
        h_ins, refs = refs[:h_in], refs[h_in:]
        outs, refs = refs[:n_out], refs[n_out:]
        h_outs, refs = refs[:h_out], refs[h_out:]
        scr, sems = refs[:n_scr], refs[n_scr:]

        @pl.when(at(lambda n: 0))
        def _():
            hosted.start(h_ins, h_outs, *sems)

        compute(*ins, *outs, *scr)

        @pl.when(at(lambda n: n - 1))
        def _():
            hosted.finish(h_ins, h_outs, *sems)

    hbm = pl.BlockSpec(memory_space=pl.ANY)
    res = pl.pallas_call(
        body, name=name, grid=grid,
        in_specs=list(in_specs) + [hbm] * h_in, out_specs=list(out_specs) + [hbm] * h_out,
        out_shape=list(out_shape) + hosted.outs,
        scratch_shapes=list(scratch_shapes) + [pltpu.SemaphoreType.DMA((hosted.n_pairs,)), pltpu.SemaphoreType.DMA((hosted.n_pairs,)),
                                               pltpu.SemaphoreType.DMA((hosted.n_local,))],
        input_output_aliases={n_in + a: n_out + b for a, b in hosted.aliases.items()},
        compiler_params=params,
    )(*args, *hosted.ins)
    return list(res[:n_out]), list(res[n_out:])


def _dot(a, b):
    return jnp.dot(a, b, preferred_element_type=F32)


def _dot_nt(a, b):
    return lax.dot_general(a, b, (((1,), (1,)), ((), ())), preferred_element_type=F32)


def _dot_tn(a, b):
    return lax.dot_general(a, b, (((0,), (0,)), ((), ())), preferred_element_type=F32)


def _shift_matrix(n, by):
    row = lax.broadcasted_iota(jnp.int32, (n, n), 0)
    col = lax.broadcasted_iota(jnp.int32, (n, n), 1)
    return jnp.where(col == row + by, 1.0, 0.0).astype(BF16)


def _rstd(v):
    return lax.rsqrt(jnp.mean(v * v, axis=-1, keepdims=True) + RMS_EPS)


def _rms_bwd(dy, v, rstd, gain):
    n = v * rstd
    dgain = jnp.sum(dy * n, axis=0, keepdims=True)
    dn = dy * gain
    dv = rstd * (dn - n * jnp.mean(dn * n, axis=-1, keepdims=True))
    return dv, dgain


def _lane_lo(shape):
    return (lax.broadcasted_iota(jnp.int32, shape, 1) % 128) < HEAD_DIM


GROUP = N_ATTN_HEADS // (KV_W // HEAD_DIM)


def _attn_bias(first_chunk):
    qi = lax.broadcasted_iota(jnp.int32, (GROUP * CHUNK, 2 * CHUNK), 0) % CHUNK
    kj = lax.broadcasted_iota(jnp.int32, (GROUP * CHUNK, 2 * CHUNK), 1)
    valid = jnp.logical_and(kj > qi, kj <= qi + CHUNK)
    if first_chunk:
        valid = jnp.logical_and(valid, kj >= CHUNK)
    return jnp.where(valid, 0.0, MASK_VALUE)


def _half(shape, hk):
    lo = _lane_lo(shape)
    return lo if hk == 0 else jnp.logical_not(lo)


class _GroupMasks:
    def __init__(self, sk_ref):
        groups = range(KV_W // HEAD_DIM)
        self.q = [_half((CHUNK, 128), hk) for hk in groups]
        self.kv = [_half((2 * CHUNK, 128), hk) for hk in groups]
        self.sinks = [_group_sinks(sk_ref, hk) for hk in groups]


def _stack_heads(ref, row0, col0, hk, half):
    parts = []
    for j in range(GROUP):
        h = GROUP * hk + j
        pair = ref[row0:row0 + CHUNK, col0 + (h // 2) * 128:col0 + (h // 2 + 1) * 128].astype(F32)
        if h % 2 != hk:
            pair = pltpu.roll(pair, HEAD_DIM, 1)
        parts.append(jnp.where(half, pair, 0.0))
    return jnp.concatenate(parts, axis=0)


def _unstack_heads(stacked, hk):
    pairs = []
    for q in range(GROUP // 2):
        even, odd = stacked[2 * q * CHUNK:(2 * q + 1) * CHUNK], stacked[(2 * q + 1) * CHUNK:(2 * q + 2) * CHUNK]
        pairs.append(even + pltpu.roll(odd, HEAD_DIM, 1) if hk == 0 else pltpu.roll(even, HEAD_DIM, 1) + odd)
    return pairs


def _group_sinks(sk_ref, hk):
    row = lax.broadcasted_iota(jnp.int32, (GROUP * CHUNK, 1), 0)
    col = jnp.full((GROUP * CHUNK, 1), sk_ref[GROUP * hk], F32)
    for j in range(1, GROUP):
        col = jnp.where(row >= j * CHUNK, sk_ref[GROUP * hk + j], col)
    return col


def _attn_probs(q_b, kk_b, bias, sink):
    s = _dot_nt(q_b, kk_b) * ATTN_SCALE + bias
    m = jnp.maximum(jnp.max(s, axis=-1, keepdims=True), sink)
    e = jnp.exp(s - m)
    e_sink = jnp.exp(sink - m)
    inv = 1.0 / (jnp.sum(e, axis=-1, keepdims=True) + e_sink)
    return e * inv, e_sink * inv


def _even_lanes(shape):
    return (lax.broadcasted_iota(jnp.int32, shape, 1) % 2) == 0


def _swap2(v, even):
    return jnp.where(even, pltpu.roll(v, v.shape[1] - 1, 1), pltpu.roll(v, 1, 1))


def _tile4(v):
    return jnp.concatenate([v, v, v, v], axis=-1)


def _sigmoid(v):
    return 1.0 / (1.0 + jnp.exp(-v))


def _ret_constants():
    h = N_RET_HEADS
    log_gamma = jnp.log(1.0 - jnp.power(2.0, -5.0 - jnp.arange(h, dtype=F32)))
    idx = jnp.arange(CHUNK, dtype=F32)
    rel = idx[:, None] - idx[None, :]
    d_intra = jnp.where(rel[None] >= 0, jnp.exp(log_gamma[:, None, None] * jnp.maximum(rel, 0.0)[None]), 0.0)
    xi = jnp.exp(log_gamma[None, :] * (idx[:, None] + 1.0))
    zeta = jnp.exp(log_gamma[None, :] * (CHUNK - 1.0 - idx[:, None]))
    decay = jnp.exp(log_gamma * CHUNK)
    xi_full = jnp.repeat(xi, RET_HEAD_DIM, axis=1)
    zeta_full = jnp.repeat(zeta, RET_HEAD_DIM, axis=1)
    decay_full = jnp.broadcast_to(jnp.repeat(decay, RET_HEAD_DIM)[None, :], (8, RET_W))
    return d_intra.astype(F32), xi_full.astype(F32), zeta_full.astype(F32), decay_full.astype(F32)


def _rope_tables(s):
    pos = jnp.arange(s, dtype=F32)
    angle = 1.0 / jnp.power(10000.0, jnp.linspace(0.0, 1.0, RET_HEAD_DIM // 2, dtype=F32))
    angle = jnp.repeat(angle, 2)
    sign = jnp.where(jnp.arange(RET_HEAD_DIM) % 2 == 0, -1.0, 1.0).astype(F32)
    return jnp.sin(pos[:, None] * angle[None]) * sign[None], jnp.cos(pos[:, None] * angle[None])


def _in_proj(x, gain, w_in, hosted=None):
    s = x.shape[0]
    tm = min(IN_PROJ_TOKEN_TILE, s)

    def body(x_ref, g_ref, w_ref, h_ref, p_ref):
        xv = x_ref[...]
        h = (xv * _rstd(xv) * g_ref[...]).astype(BF16)
        h_ref[...] = h
        p_ref[...] = _dot(h, w_ref[...])

    return _hosted_call(
        body, name="in_proj", grid=(s // tm,),
        in_specs=[pl.BlockSpec((tm, D_MODEL), lambda i: (i, 0)), _resident((1, D_MODEL)), _resident((D_MODEL, IN_W))],
        out_specs=[pl.BlockSpec((tm, D_MODEL), lambda i: (i, 0)), pl.BlockSpec((tm, IN_W), lambda i: (i, 0))],
        out_shape=[jax.ShapeDtypeStruct((s, D_MODEL), BF16), jax.ShapeDtypeStruct((s, IN_W), F32)],
        scratch_shapes=[], args=(x, gain, w_in), hosted=hosted)


def _mixer_fwd(proj, sinks, sin, cos, consts, hosted=None):
    s = proj.shape[0]
    nc = s // CHUNK
    cps = MIXER_CHUNKS_PER_STEP
    d_intra, xi_full, zeta_full, decay_full = consts

    def body(sk_ref, p_ref, pkv_ref, sin_ref, cos_ref, dm_ref, xi_ref, ze_ref, dc_ref, mix_ref, st_ref, state):
        i = pl.program_id(0)

        @pl.when(i == 0)
        def _():
            state[...] = jnp.zeros_like(state)

        st = [state[h] for h in range(N_RET_HEADS)]
        bias_any = _attn_bias(False)
        bias_c0 = jnp.where(i == 0, _attn_bias(True), bias_any)
        even = _even_lanes((CHUNK, RET_W))
        masks = _GroupMasks(sk_ref)
        for c in range(cps):
            r0 = c * CHUNK
            rows = slice(r0, r0 + CHUNK)

            kv_cur = p_ref[rows, KV_A0:KV_A0 + 2 * KV_W]
            kv_prev = pkv_ref[...] if c == 0 else p_ref[r0 - CHUNK:r0, KV_A0:KV_A0 + 2 * KV_W]
            kk = jnp.concatenate([kv_prev[:, :KV_W], kv_cur[:, :KV_W]], axis=0)
            vv = jnp.concatenate([kv_prev[:, KV_W:], kv_cur[:, KV_W:]], axis=0)
            kk_b = kk.astype(BF16)
            bias = bias_c0 if c == 0 else bias_any
            for hk in range(KV_W // HEAD_DIM):
                q_b = _stack_heads(p_ref, r0, Q_A0, hk, masks.q[hk]).astype(BF16)
                p, _ = _attn_probs(q_b, kk_b, bias, masks.sinks[hk])
                v_b = jnp.where(masks.kv[hk], vv, 0.0).astype(BF16)
                for q, pair in enumerate(_unstack_heads(_dot(p.astype(BF16), v_b), hk)):
                    pi = (GROUP // 2) * hk + q
                    mix_ref[rows, pi * 128:(pi + 1) * 128] = pair.astype(BF16)

            sin4, cos4 = _tile4(sin_ref[rows, :]), _tile4(cos_ref[rows, :])
            q_r = p_ref[rows, Q_R0:Q_R0 + RET_W]
            k_r = p_ref[rows, K_R0:K_R0 + RET_W] * RET_K_SCALE
            q_r = q_r * cos4 + _swap2(q_r, even) * sin4
            k_r = k_r * cos4 + _swap2(k_r, even) * sin4
            kz = k_r * ze_ref[...]
            for h in range(N_RET_HEADS):
                sl = slice(h * RET_HEAD_DIM, (h + 1) * RET_HEAD_DIM)
                qh, kh = q_r[:, sl].astype(BF16), k_r[:, sl].astype(BF16)
                vh = p_ref[rows, V_R0 + h * RET_HEAD_DIM:V_R0 + (h + 1) * RET_HEAD_DIM].astype(BF16)
                st_ref[c, h] = st[h]
                a = _dot_nt(qh, kh) * dm_ref[h]
                qx = (q_r[:, sl] * xi_ref[:, sl]).astype(BF16)
                o = _dot(jnp.concatenate([a.astype(BF16), qx], axis=1), jnp.concatenate([vh, st[h].astype(BF16)], axis=0))
                st[h] = dc_ref[0:1, sl] * st[h] + _dot_tn(kz[:, sl].astype(BF16), vh)
                mu = jnp.mean(o, axis=-1, keepdims=True)
                oc = o - mu
                on = oc * lax.rsqrt(jnp.mean(oc * oc, axis=-1, keepdims=True) + GN_EPS)
                g = p_ref[rows, G_R0 + h * RET_HEAD_DIM:G_R0 + (h + 1) * RET_HEAD_DIM]
                mix_ref[rows, ATTN_W + h * RET_HEAD_DIM:ATTN_W + (h + 1) * RET_HEAD_DIM] = (g * _sigmoid(g) * on).astype(BF16)
        for h in range(N_RET_HEADS):
            state[h] = st[h]

    return _hosted_call(
        body, name="mixer_fwd", grid=(nc // cps,),
        in_specs=[
            pl.BlockSpec(memory_space=pltpu.SMEM),
            pl.BlockSpec((cps * CHUNK, IN_W), lambda i: (i, 0)),
            pl.BlockSpec((CHUNK, 2 * KV_W), lambda i: (jnp.maximum(cps * i - 1, 0), KV_A0 // (2 * KV_W))),
            pl.BlockSpec((cps * CHUNK, RET_HEAD_DIM), lambda i: (i, 0)),
            pl.BlockSpec((cps * CHUNK, RET_HEAD_DIM), lambda i: (i, 0)),
            _resident((N_RET_HEADS, CHUNK, CHUNK)), _resident((CHUNK, RET_W)), _resident((CHUNK, RET_W)), _resident((8, RET_W)),
        ],
        out_specs=[
            pl.BlockSpec((cps * CHUNK, D_MODEL), lambda i: (i, 0)),
            pl.BlockSpec((cps, N_RET_HEADS, RET_HEAD_DIM, RET_HEAD_DIM), lambda i: (i, 0, 0, 0)),
        ],
        out_shape=[jax.ShapeDtypeStruct((s, D_MODEL), BF16),
                   jax.ShapeDtypeStruct((nc, N_RET_HEADS, RET_HEAD_DIM, RET_HEAD_DIM), F32)],
        scratch_shapes=[pltpu.VMEM((N_RET_HEADS, RET_HEAD_DIM, RET_HEAD_DIM), F32)],
        args=(sinks, proj, proj, sin, cos, d_intra, xi_full, zeta_full, decay_full), hosted=hosted)


def _out_up_proj(mix, x, w_out, g_post, g_pre, w_up):
    s = x.shape[0]
    tm = min(BIG_TOKEN_TILE, s)
    blk = UP_W // N_CHIPS

    def body(mix_ref, x_ref, wo_ref, g2_ref, g3_ref, wu_ref, mixed_ref, x1_ref, h2_ref, u0_ref):
        mixed = _dot(mix_ref[...], wo_ref[...])
        mixed_ref[...] = mixed
        x1 = x_ref[...] + mixed * _rstd(mixed) * g2_ref[...]
        x1_ref[...] = x1
        h2 = (x1 * _rstd(x1) * g3_ref[...]).astype(BF16)
        h2_ref[...] = h2
        for k in range(N_CHIPS):
            u0_ref[:, k * blk:(k + 1) * blk] = _dot(h2, wu_ref[k]).astype(BF16)

    tok = lambda w: pl.BlockSpec((tm, w), lambda i: (i, 0))
    return pl.pallas_call(
        body, name="out_up_proj", grid=(s // tm,),
        in_specs=[tok(D_MODEL), tok(D_MODEL), _resident((D_MODEL, D_MODEL)), _resident((1, D_MODEL)), _resident((1, D_MODEL)),
                  _resident((N_CHIPS, D_MODEL, blk))],
        out_specs=[tok(D_MODEL), tok(D_MODEL), tok(D_MODEL), tok(UP_W)],
        out_shape=[jax.ShapeDtypeStruct((s, D_MODEL), F32), jax.ShapeDtypeStruct((s, D_MODEL), F32),
                   jax.ShapeDtypeStruct((s, D_MODEL), BF16), jax.ShapeDtypeStruct((s, UP_W), BF16)],
        compiler_params=_params(("arbitrary",)),
    )(mix, x, w_out, g_post, g_pre, w_up)


def _ffn_tail(u0, x1, target, conv_w, conv_b, w_down, g_post):
    s = x1.shape[0]
    tm = TOKEN_TILE
    last = s // tm - 1
    rb, lanes = FFN_ROW_BLOCK, 128

    def body(u0_ref, x1_ref, t_ref, cw_ref, cb_ref, wd_ref, g_ref,
             y_ref, dy2_ref, dout_ref, du_ref, cacc_ref, gacc_ref, u1_s, u2_s, carry, gelu_s, slope_s, dy_s, cacc):
        i = pl.program_id(0)

        @pl.when(i == 0)
        def _():
            carry[...] = jnp.zeros_like(carry)
            cacc[...] = jnp.zeros_like(cacc)
            gacc_ref[...] = jnp.zeros_like(gacc_ref)

        shift1, shift2 = _shift_matrix(tm, -1), _shift_matrix(tm, -2)
        r8 = lax.broadcasted_iota(jnp.int32, (8, 1), 0)
        wide = 2 * lanes

        def shift_block(col):
            cols = slice(col, col + wide)
            u1_s[:, cols] = _dot(shift1, u0_ref[:, cols])
            u2_s[:, cols] = _dot(shift2, u0_ref[:, cols])
            c14, c15 = carry[14:15, cols], carry[15:16, cols]
            u1_s[0:8, cols] = jnp.where(r8 == 0, c15, u1_s[0:8, cols])
            u2_s[0:8, cols] = jnp.where(r8 == 0, c14, jnp.where(r8 == 1, c15, u2_s[0:8, cols]))

        def taps(col):
            return (cw_ref[0:1, col:col + lanes], cw_ref[1:2, col:col + lanes], cw_ref[2:3, col:col + lanes],
                    cb_ref[0:1, col:col + lanes])

        def shifted(r0, col):
            return (u2_s[r0:r0 + rb, col:col + lanes], u1_s[r0:r0 + rb, col:col + lanes],
                    u0_ref[r0:r0 + rb, col:col + lanes].astype(F32))

        def conv(r0, col, w):
            u2, u1, uc = shifted(r0, col)
            return w[0] * u2 + w[1] * u1 + w[2] * uc + w[3]

        fold = lambda v: jnp.sum(v.reshape(rb // 8, 8, lanes), axis=0)

        shift_block(0)
        shift_block(D_FF)
        for j in range(D_FF // lanes):
            cg, cv = j * lanes, D_FF + j * lanes
            if cg % wide == 0 and cg + wide < D_FF:
                shift_block(cg + wide)
                shift_block(cv + wide)
            wg, wv = taps(cg), taps(cv)
            for r0 in range(0, tm, rb):
                gate, val = conv(r0, cg, wg), conv(r0, cv, wv)
                g2 = gate * gate
                th = jnp.tanh(gate * (GELU_C + GELU_C * GELU_A * g2))
                hp = 0.5 * th + 0.5
                gelu = gate * hp
                dgelu = hp + gate * (1.0 - th * th) * (0.5 * GELU_C + 1.5 * GELU_C * GELU_A * g2)
                y_ref[r0:r0 + rb, cg:cg + lanes] = (gelu * val).astype(BF16)
                gelu_s[r0:r0 + rb, cg:cg + lanes] = gelu
                slope_s[r0:r0 + rb, cg:cg + lanes] = dgelu * val

        y2 = _dot(y_ref[...], wd_ref[...])
        r4 = _rstd(y2)
        gain = g_ref[...]
        out = x1_ref[...] + y2 * r4 * gain
        diff = out - t_ref[...]
        dout = diff * (1.0 / D_MODEL)
        dout_ref[...] = dout
        dy2, dgain = _rms_bwd(dout, y2, r4, gain)
        dy2_b = dy2.astype(BF16)
        dy2_ref[...] = dy2_b
        gacc_ref[0:1, :] += dgain
        gacc_ref[1:2, :] += 0.5 * jnp.sum(diff * dout, axis=0, keepdims=True)
        carry[...] = u0_ref[tm - 16:tm, :].astype(F32)

        dy_s[:, 0:wide] = _dot_nt(dy2_b, wd_ref[0:wide, :])
        for j in range(D_FF // lanes):
            cg, cv = j * lanes, D_FF + j * lanes
            if cg % wide == 0 and cg + wide < D_FF:
                dy_s[:, cg + wide:cg + 2 * wide] = _dot_nt(dy2_b, wd_ref[cg + wide:cg + 2 * wide, :])
            acc = [[jnp.zeros((8, lanes), F32) for _ in range(CONV_WIDTH + 1)] for _ in range(2)]
            for r0 in range(0, tm, rb):
                dy = dy_s[r0:r0 + rb, cg:cg + lanes]
                d_gate = dy * slope_s[r0:r0 + rb, cg:cg + lanes]
                d_val = dy * gelu_s[r0:r0 + rb, cg:cg + lanes]
                for side, (col, d) in enumerate(((cg, d_gate), (cv, d_val))):
                    du_ref[r0:r0 + rb, col:col + lanes] = d.astype(BF16)
                    for k, u in enumerate(shifted(r0, col)):
                        acc[side][k] = acc[side][k] + fold(d * u)
                    acc[side][CONV_WIDTH] = acc[side][CONV_WIDTH] + fold(d)
            for side, col in enumerate((cg, cv)):
                for k in range(CONV_WIDTH + 1):
                    cacc[8 * k:8 * k + 8, col:col + lanes] += acc[side][k]

        @pl.when(i == last)
        def _():
            for k in range(CONV_WIDTH + 1):
                cacc_ref[k:k + 1, :] = jnp.sum(cacc[8 * k:8 * k + 8, :], axis=0, keepdims=True)
            cacc_ref[CONV_WIDTH + 1:8, :] = jnp.zeros((8 - CONV_WIDTH - 1, UP_W), F32)

    tok = lambda w: pl.BlockSpec((tm, w), lambda i: (i, 0))
    return pl.pallas_call(
        body, name="ffn_tail", grid=(s // tm,),
        in_specs=[tok(UP_W), tok(D_MODEL), tok(D_MODEL), _resident((CONV_WIDTH, UP_W)), _resident((1, UP_W)),
                  _resident((D_FF, D_MODEL)), _resident((1, D_MODEL))],
        out_specs=[tok(D_FF), tok(D_MODEL), tok(D_MODEL), tok(UP_W),
                   pl.BlockSpec((8, UP_W), lambda i: (0, 0)), pl.BlockSpec((8, D_MODEL), lambda i: (0, 0))],
        out_shape=[jax.ShapeDtypeStruct((s, D_FF), BF16), jax.ShapeDtypeStruct((s, D_MODEL), BF16),
                   jax.ShapeDtypeStruct((s, D_MODEL), F32), jax.ShapeDtypeStruct((s, UP_W), BF16),
                   jax.ShapeDtypeStruct((8, UP_W), F32), jax.ShapeDtypeStruct((8, D_MODEL), F32)],
        scratch_shapes=[pltpu.VMEM((tm, UP_W), F32), pltpu.VMEM((tm, UP_W), F32), pltpu.VMEM((16, UP_W), F32),
                        pltpu.VMEM((tm, D_FF), F32), pltpu.VMEM((tm, D_FF), F32),
                        pltpu.VMEM((tm, D_FF), F32), pltpu.VMEM((8 * (CONV_WIDTH + 1), UP_W), F32)],
        compiler_params=_params(("arbitrary",)),
    )(u0, x1, target, conv_w, conv_b, w_down, g_post)


def _ffn_head_bwd(du, conv_w, w_up, x1, g_pre, dout, mixed, g_post, w_out):
    s = x1.shape[0]
    tm = TOKEN_TILE
    nt = s // tm
    blk = UP_W // N_CHIPS

    def body(du_ref, halo_ref, cw_ref, wu_ref, x1_ref, g3_ref, dout_ref, mixed_ref, g2_ref, wo_ref,
             du0_ref, dx1_ref, dmixed_ref, dmix_ref, gacc_ref, dbuf):
        i = pl.program_id(0)

        @pl.when(i == 0)
        def _():
            gacc_ref[...] = jnp.zeros_like(gacc_ref)

        dbuf[0:tm, :] = du_ref[...].astype(F32)
        dbuf[tm:tm + 16, :] = jnp.where(i < nt - 1, halo_ref[...].astype(F32), 0.0)
        dh2 = jnp.zeros((tm, D_MODEL), F32)
        for k in range(N_CHIPS):
            for c0 in range(0, blk, HEAD_BWD_COLS):
                width = min(HEAD_BWD_COLS, blk - c0)
                cols = slice(k * blk + c0, k * blk + c0 + width)
                du0_b = (cw_ref[2:3, cols] * dbuf[0:tm, cols] + cw_ref[1:2, cols] * dbuf[1:1 + tm, cols]
                         + cw_ref[0:1, cols] * dbuf[2:2 + tm, cols]).astype(BF16)
                du0_ref[:, cols] = du0_b
                dh2 = dh2 + _dot_nt(du0_b, wu_ref[k, :, c0:c0 + width])
        x1 = x1_ref[...]
        d3, dg3 = _rms_bwd(dh2, x1, _rstd(x1), g3_ref[...])
        dx1 = dout_ref[...] + d3
        dx1_ref[...] = dx1
        mixed = mixed_ref[...]
        dmixed, dg2 = _rms_bwd(dx1, mixed, _rstd(mixed), g2_ref[...])
        dmixed_b = dmixed.astype(BF16)
        dmixed_ref[...] = dmixed_b
        dmix_ref[...] = _dot_nt(dmixed_b, wo_ref[...]).astype(BF16)
        gacc_ref[0:1, :] += dg3
        gacc_ref[1:2, :] += dg2

    tok = lambda w: pl.BlockSpec((tm, w), lambda i: (i, 0))
    halo = pl.BlockSpec((16, UP_W), lambda i: (jnp.minimum(i + 1, nt - 1) * (tm // 16), 0))
    return pl.pallas_call(
        body, name="ffn_head_bwd", grid=(nt,),
        in_specs=[tok(UP_W), halo, _resident((CONV_WIDTH, UP_W)), _resident((N_CHIPS, D_MODEL, blk)), tok(D_MODEL),
                  _resident((1, D_MODEL)), tok(D_MODEL), tok(D_MODEL), _resident((1, D_MODEL)), _resident((D_MODEL, D_MODEL))],
        out_specs=[tok(UP_W), tok(D_MODEL), tok(D_MODEL), tok(D_MODEL), pl.BlockSpec((8, D_MODEL), lambda i: (0, 0))],
        out_shape=[jax.ShapeDtypeStruct((s, UP_W), BF16), jax.ShapeDtypeStruct((s, D_MODEL), F32),
                   jax.ShapeDtypeStruct((s, D_MODEL), BF16), jax.ShapeDtypeStruct((s, D_MODEL), BF16),
                   jax.ShapeDtypeStruct((8, D_MODEL), F32)],
        scratch_shapes=[pltpu.VMEM((tm + 16, UP_W), F32)],
        compiler_params=_params(("arbitrary",)),
    )(du, du, conv_w, w_up, x1, g_pre, dout, mixed, g_post, w_out)


def _mixer_bwd(proj, dmix, states, sinks, sin, cos, consts, hosted=None):
    s = proj.shape[0]
    nc = s // CHUNK
    cps = MIXER_CHUNKS_PER_STEP
    nb = nc // cps
    d_intra, xi_full, zeta_full, decay_full = consts

    def body(sk_ref, p_ref, pkv_ref, dmix_ref, st_ref, sin_ref, cos_ref, dm_ref, xi_ref, ze_ref, dc_ref,
             dp_ref, dsk_ref, gstate, ckv, dsk_acc):
        i = pl.program_id(0)
        block = nb - 1 - i

        @pl.when(i == 0)
        def _():
            gstate[...] = jnp.zeros_like(gstate)
            ckv[...] = jnp.zeros_like(ckv)
            dsk_acc[...] = jnp.zeros_like(dsk_acc)

        gs_all = [gstate[h] for h in range(N_RET_HEADS)]
        later_kv = ckv[...]
        lane = lax.broadcasted_iota(jnp.int32, (CHUNK, 128), 1)
        dsk = jnp.zeros((CHUNK, 128), F32)
        bias_any = _attn_bias(False)
        bias_c0 = jnp.where(block == 0, _attn_bias(True), bias_any)
        even = _even_lanes((CHUNK, RET_W))
        masks = _GroupMasks(sk_ref)
        for c in reversed(range(cps)):
            r0 = c * CHUNK
            rows = slice(r0, r0 + CHUNK)

            kv_cur = p_ref[rows, KV_A0:KV_A0 + 2 * KV_W]
            kv_prev = pkv_ref[...] if c == 0 else p_ref[r0 - CHUNK:r0, KV_A0:KV_A0 + 2 * KV_W]
            kk = jnp.concatenate([kv_prev[:, :KV_W], kv_cur[:, :KV_W]], axis=0)
            vv = jnp.concatenate([kv_prev[:, KV_W:], kv_cur[:, KV_W:]], axis=0)
            kk_b, vv_b = kk.astype(BF16), vv.astype(BF16)
            bias = bias_c0 if c == 0 else bias_any
            dkk = jnp.zeros((2 * CHUNK, KV_W), F32)
            dvv = jnp.zeros((2 * CHUNK, KV_W), F32)
            for hk in range(KV_W // HEAD_DIM):
                q_b = _stack_heads(p_ref, r0, Q_A0, hk, masks.q[hk]).astype(BF16)
                do_b = _stack_heads(dmix_ref, r0, 0, hk, masks.q[hk]).astype(BF16)
                p, p_sink = _attn_probs(q_b, kk_b, bias, masks.sinks[hk])
                dpr = _dot_nt(do_b, vv_b)
                delta = jnp.sum(p * dpr, axis=-1, keepdims=True)
                ds_b = (p * (dpr - delta) * ATTN_SCALE).astype(BF16)
                dsink = -p_sink * delta
                for j in range(GROUP):
                    dsk = dsk + jnp.where(lane == GROUP * hk + j, dsink[j * CHUNK:(j + 1) * CHUNK], 0.0)
                k_b = jnp.where(masks.kv[hk], kk, 0.0).astype(BF16)
                for q, pair in enumerate(_unstack_heads(_dot(ds_b, k_b), hk)):
                    pi = (GROUP // 2) * hk + q
                    dp_ref[rows, Q_A0 + pi * 128:Q_A0 + (pi + 1) * 128] = pair.astype(BF16)
                dkk = dkk + _dot_tn(ds_b, q_b)
                dvv = dvv + _dot_tn(p.astype(BF16), do_b)
            dp_ref[rows, KV_A0:KV_A0 + KV_W] = (dkk[CHUNK:] + later_kv[:, :KV_W]).astype(BF16)
            dp_ref[rows, KV_A0 + KV_W:KV_A0 + 2 * KV_W] = (dvv[CHUNK:] + later_kv[:, KV_W:]).astype(BF16)
            later_kv = jnp.concatenate([dkk[:CHUNK], dvv[:CHUNK]], axis=1)

            sin4, cos4 = _tile4(sin_ref[rows, :]), _tile4(cos_ref[rows, :])
            q_r = p_ref[rows, Q_R0:Q_R0 + RET_W]
            k_r = p_ref[rows, K_R0:K_R0 + RET_W] * RET_K_SCALE
            q_r = q_r * cos4 + _swap2(q_r, even) * sin4
            k_r = k_r * cos4 + _swap2(k_r, even) * sin4
            kz = k_r * ze_ref[...]
            dq_parts, dk_parts = [], []
            for h in range(N_RET_HEADS):
                sl = slice(h * RET_HEAD_DIM, (h + 1) * RET_HEAD_DIM)
                qh, kh = q_r[:, sl].astype(BF16), k_r[:, sl].astype(BF16)
                vh = p_ref[rows, V_R0 + h * RET_HEAD_DIM:V_R0 + (h + 1) * RET_HEAD_DIM].astype(BF16)
                st_b = st_ref[c, h].astype(BF16)
                gs = gs_all[h]
                gs_b = gs.astype(BF16)
                xi_h = xi_ref[:, sl]
                dm = dm_ref[h]
                a_b = (_dot_nt(qh, kh) * dm).astype(BF16)
                qx = (q_r[:, sl] * xi_h).astype(BF16)
                o = _dot(jnp.concatenate([a_b, qx], axis=1), jnp.concatenate([vh, st_b], axis=0))
                mu = jnp.mean(o, axis=-1, keepdims=True)
                oc = o - mu
                rs = lax.rsqrt(jnp.mean(oc * oc, axis=-1, keepdims=True) + GN_EPS)
                on = oc * rs
                g = p_ref[rows, G_R0 + h * RET_HEAD_DIM:G_R0 + (h + 1) * RET_HEAD_DIM]
                sg = _sigmoid(g)
                dr = dmix_ref[rows, ATTN_W + h * RET_HEAD_DIM:ATTN_W + (h + 1) * RET_HEAD_DIM].astype(F32)
                dp_ref[rows, G_R0 + h * RET_HEAD_DIM:G_R0 + (h + 1) * RET_HEAD_DIM] = (
                    dr * on * (sg * (1.0 + g * (1.0 - sg)))).astype(BF16)
                don = dr * g * sg
                do = rs * (don - jnp.mean(don, axis=-1, keepdims=True) - on * jnp.mean(don * on, axis=-1, keepdims=True))
                do_b = do.astype(BF16)
                dox_b = (do * xi_h).astype(BF16)
                da_b = (_dot_nt(do_b, vh) * dm).astype(BF16)
                dq_parts.append(_dot(da_b, kh) + _dot_nt(dox_b, st_b))
                dk_parts.append(_dot_tn(da_b, qh) + ze_ref[:, sl] * _dot_nt(vh, gs_b))
                dv = _dot_tn(a_b, do_b) + _dot(kz[:, sl].astype(BF16), gs_b)
                dp_ref[rows, V_R0 + h * RET_HEAD_DIM:V_R0 + (h + 1) * RET_HEAD_DIM] = dv.astype(BF16)
                gs_all[h] = dc_ref[0:1, sl] * gs + _dot_tn(qh, dox_b)
            dq = jnp.concatenate(dq_parts, axis=-1)
            dk = jnp.concatenate(dk_parts, axis=-1)
            dp_ref[rows, Q_R0:Q_R0 + RET_W] = (dq * cos4 - _swap2(dq, even) * sin4).astype(BF16)
            dp_ref[rows, K_R0:K_R0 + RET_W] = (RET_K_SCALE * (dk * cos4 - _swap2(dk, even) * sin4)).astype(BF16)

        for h in range(N_RET_HEADS):
            gstate[h] = gs_all[h]
        ckv[...] = later_kv
        dsk_acc[...] += dsk

        @pl.when(i == nb - 1)
        def _():
            dsk_ref[...] = jnp.sum(dsk_acc[...], axis=0, keepdims=True)

    rev = lambda i: nb - 1 - i
    return _hosted_call(
        body, name="mixer_bwd", grid=(nb,),
        in_specs=[
            pl.BlockSpec(memory_space=pltpu.SMEM),
            pl.BlockSpec((cps * CHUNK, IN_W), lambda i: (rev(i), 0)),
            pl.BlockSpec((CHUNK, 2 * KV_W), lambda i: (jnp.maximum(cps * rev(i) - 1, 0), KV_A0 // (2 * KV_W))),
            pl.BlockSpec((cps * CHUNK, D_MODEL), lambda i: (rev(i), 0)),
            pl.BlockSpec((cps, N_RET_HEADS, RET_HEAD_DIM, RET_HEAD_DIM), lambda i: (rev(i), 0, 0, 0)),
            pl.BlockSpec((cps * CHUNK, RET_HEAD_DIM), lambda i: (rev(i), 0)),
            pl.BlockSpec((cps * CHUNK, RET_HEAD_DIM), lambda i: (rev(i), 0)),
            _resident((N_RET_HEADS, CHUNK, CHUNK)), _resident((CHUNK, RET_W)), _resident((CHUNK, RET_W)), _resident((8, RET_W)),
        ],
        out_specs=[pl.BlockSpec((cps * CHUNK, IN_W), lambda i: (rev(i), 0)), pl.BlockSpec((1, 128), lambda i: (0, 0))],
        out_shape=[jax.ShapeDtypeStruct((s, IN_W), BF16), jax.ShapeDtypeStruct((1, 128), F32)],
        scratch_shapes=[pltpu.VMEM((N_RET_HEADS, RET_HEAD_DIM, RET_HEAD_DIM), F32), pltpu.VMEM((CHUNK, 2 * KV_W), F32),
                        pltpu.VMEM((CHUNK, 128), F32)],
        args=(sinks, proj, proj, dmix, states, sin, cos, d_intra, xi_full, zeta_full, decay_full), hosted=hosted)


def _in_proj_bwd(dproj, w_in, x, gain, dx1, hosted=None):
    s = x.shape[0]
    tm = min(BIG_TOKEN_TILE, s)

    def body(dp_ref, w_ref, x_ref, g_ref, dx1_ref, dx_ref, gacc_ref):
        @pl.when(pl.program_id(0) == 0)
        def _():
            gacc_ref[...] = jnp.zeros_like(gacc_ref)

        dh = _dot_nt(dp_ref[...], w_ref[...])
        xv = x_ref[...]
        d1, dg = _rms_bwd(dh, xv, _rstd(xv), g_ref[...])
        dx_ref[...] = dx1_ref[...] + d1
        gacc_ref[0:1, :] += dg

    tok = lambda w: pl.BlockSpec((tm, w), lambda i: (i, 0))
    return _hosted_call(
        body, name="in_proj_bwd", grid=(s // tm,),
        in_specs=[tok(IN_W), _resident((D_MODEL, IN_W)), tok(D_MODEL), _resident((1, D_MODEL)), tok(D_MODEL)],
        out_specs=[tok(D_MODEL), pl.BlockSpec((8, D_MODEL), lambda i: (0, 0))],
        out_shape=[jax.ShapeDtypeStruct((s, D_MODEL), F32), jax.ShapeDtypeStruct((8, D_MODEL), F32)],
        scratch_shapes=[], args=(dproj, w_in, x, gain, dx1), hosted=hosted)


def _weight_grad(a, b, tn, name, by_block=False, hosted=None):
    s, m = a.shape
    n = b.shape[1]
    tk = min(WEIGHT_GRAD_TOKENS if m <= D_MODEL else WEIGHT_GRAD_TOKENS // 2, s)

    def body(a_ref, b_ref, o_ref):
        @pl.when(pl.program_id(1) == 0)
        def _():
            o_ref[...] = jnp.zeros_like(o_ref)

        o_ref[...] += _dot_tn(a_ref[...], b_ref[...])

    if by_block:
        out_spec = pl.BlockSpec((None, m, tn), lambda j, k: (j, 0, 0))
        out_shape = jax.ShapeDtypeStruct((n // tn, m, tn), F32)
    else:
        out_spec = pl.BlockSpec((m, tn), lambda j, k: (0, j))
        out_shape = jax.ShapeDtypeStruct((m, n), F32)
    (out,), lands = _hosted_call(
        body, name=name, grid=(n // tn, s // tk),
        in_specs=[pl.BlockSpec((tk, m), lambda j, k: (k, 0)), pl.BlockSpec((tk, tn), lambda j, k: (k, j))],
        out_specs=[out_spec], out_shape=[out_shape], scratch_shapes=[], args=(a, b), hosted=hosted)
    return out if hosted is None else (out, lands)


def _place():
    return lax.axis_index("x"), lax.axis_index("y"), lax.axis_index("c")


def _remote(src, dst, send_sems, recv_sems, k, to):
    return pltpu.make_async_remote_copy(src_ref=src, dst_ref=dst, send_sem=send_sems.at[k], recv_sem=recv_sems.at[k],
                                        device_id=to, device_id_type=MESH)


def _gather_level1_copies(w_refs, out_refs, send_sems, recv_sems, local_sems):
    x, y, c = _place()
    mine_at = 2 * x + y
    peers = [(x, y, 1 - c), (1 - x, y, c), (x, 1 - y, c), (1 - x, 1 - y, c)]
    local, sends, recvs = [], [], []
    for i, (w, out) in enumerate(zip(w_refs, out_refs)):
        half = w.shape[0] // 2
        src = w.at[pl.ds(pl.multiple_of(c * half, 16 if half % 16 == 0 else 8), half), :]
        mine = out.at[mine_at, c]
        local.append(pltpu.make_async_copy(src, mine, local_sems.at[i]))
        for k, p in enumerate(peers):
            sends.append(_remote(src, mine, send_sems, recv_sems, 4 * i + k, p))
            lands = out.at[mine_at, 1 - c] if k == 0 else out.at[2 * p[0] + p[1], c]
            recvs.append(_remote(src, lands, send_sems, recv_sems, 4 * i + k, p))
    return local, sends, recvs


def _gather_level1_start(w_refs, out_refs, send_sems, recv_sems, local_sems):
    local, sends, _ = _gather_level1_copies(w_refs, out_refs, send_sems, recv_sems, local_sems)
    for cp in local + sends:
        cp.start()


def _gather_level1_finish(w_refs, out_refs, send_sems, recv_sems, local_sems):
    local, sends, recvs = _gather_level1_copies(w_refs, out_refs, send_sems, recv_sems, local_sems)
    for cp in recvs:
        cp.wait_recv()
    for cp in sends:
        cp.wait_send()
    for cp in local:
        cp.wait()


def _gather_level2_copies(in_refs, out_refs, send_sems, recv_sems, local_sems):
    x, y, c = _place()
    chips = [(1 - x, y), (x, 1 - y), (1 - x, 1 - y)]
    sends, recvs = [], []
    for i, (src, out) in enumerate(zip(in_refs, out_refs)):
        for j, (px, py) in enumerate(chips):
            sends.append(_remote(src.at[2 * px + py, c], out.at[2 * px + py, c], send_sems, recv_sems, 3 * i + j, (x, y, 1 - c)))
            recvs.append(_remote(src.at[2 * px + py, c], out.at[2 * px + py, 1 - c], send_sems, recv_sems, 3 * i + j,
                                 (x, y, 1 - c)))
    return sends, recvs


def _gather_level2_start(in_refs, out_refs, send_sems, recv_sems, local_sems):
    for cp in _gather_level2_copies(in_refs, out_refs, send_sems, recv_sems, local_sems)[0]:
        cp.start()


def _gather_level2_finish(in_refs, out_refs, send_sems, recv_sems, local_sems):
    sends, recvs = _gather_level2_copies(in_refs, out_refs, send_sems, recv_sems, local_sems)
    for cp in recvs:
        cp.wait_recv()
    for cp in sends:
        cp.wait_send()


def _gathered_shape(w):
    r, cols = w.shape
    return jax.ShapeDtypeStruct((N_CHIPS, 2, r // 2, cols), w.dtype)


def _hosted_gather_level1(shards):
    n = len(shards)
    return _Hosted(shards, [_gathered_shape(w) for w in shards], {}, 4 * n, n, _gather_level1_start, _gather_level1_finish)


def _hosted_gather_level2(gathered):
    n = len(gathered)
    return _Hosted(gathered, [jax.ShapeDtypeStruct(g.shape, g.dtype) for g in gathered], {i: i for i in range(n)}, 3 * n, 0,
                   _gather_level2_start, _gather_level2_finish)


def _gather_now(shards, name):
    n = len(shards)

    def body(*refs):
        w_refs, out_refs = list(refs[:n]), list(refs[n:2 * n])
        send1, recv1, local1, send2, recv2 = refs[2 * n:]
        _gather_level1_start(w_refs, out_refs, send1, recv1, local1)
        _gather_level1_finish(w_refs, out_refs, send1, recv1, local1)
        _gather_level2_start(out_refs, out_refs, send2, recv2, None)
        _gather_level2_finish(out_refs, out_refs, send2, recv2, None)

    hbm = pl.BlockSpec(memory_space=pl.ANY)
    return pl.pallas_call(
        body, name=name, out_shape=[_gathered_shape(w) for w in shards], in_specs=[hbm] * n, out_specs=[hbm] * n,
        scratch_shapes=[pltpu.SemaphoreType.DMA((4 * n,)), pltpu.SemaphoreType.DMA((4 * n,)), pltpu.SemaphoreType.DMA((n,)),
                        pltpu.SemaphoreType.DMA((3 * n,)), pltpu.SemaphoreType.DMA((3 * n,))],
    )(*shards)


def _scatter_copies(g_refs, land_refs, send_sems, recv_sems, local_sems):
    x, y, c = _place()
    copies = []
    for i, (g, land) in enumerate(zip(g_refs, land_refs)):
        for k, (px, py, pc) in enumerate(_relations(x, y, c)):
            copies.append(_remote(g.at[2 * px + py, pc], land.at[k], send_sems, recv_sems, 7 * i + k, (px, py, pc)))
    return copies


def _scatter_start(g_refs, land_refs, send_sems, recv_sems, local_sems):
    for cp in _scatter_copies(g_refs, land_refs, send_sems, recv_sems, local_sems):
        cp.start()


def _scatter_finish(g_refs, land_refs, send_sems, recv_sems, local_sems):
    for cp in _scatter_copies(g_refs, land_refs, send_sems, recv_sems, local_sems):
        cp.wait()


def _hosted_scatter(grads):
    lands = [jax.ShapeDtypeStruct((N_DEV - 1,) + g.shape[2:], g.dtype) for g in grads]
    return _Hosted(grads, lands, {}, 7 * len(grads), 0, _scatter_start, _scatter_finish)


def _relations(x, y, c):
    rel = []
    for fx in (0, 1):
        for fy in (0, 1):
            for fc in (0, 1):
                if fx or fy or fc:
                    rel.append(((1 - x) if fx else x, (1 - y) if fy else y, (1 - c) if fc else c))
    return rel


def _gather_small(v, name):
    r, cols = v.shape

    def body(v_ref, out_ref, send_sems, recv_sems):
        x, y, c = _place()
        peers = _relations(x, y, c)

        def slot(p):
            return out_ref.at[4 * p[0] + 2 * p[1] + p[2]]

        out_ref[4 * x + 2 * y + c] = v_ref[...]
        sends = [pltpu.make_async_remote_copy(
            src_ref=v_ref, dst_ref=slot((x, y, c)), send_sem=send_sems.at[k], recv_sem=recv_sems.at[k],
            device_id=p, device_id_type=MESH) for k, p in enumerate(peers)]
        for cp in sends:
            cp.start()
        for k, p in enumerate(peers):
            pltpu.make_async_remote_copy(
                src_ref=v_ref, dst_ref=slot(p), send_sem=send_sems.at[k], recv_sem=recv_sems.at[k],
                device_id=p, device_id_type=MESH).wait_recv()
        for cp in sends:
            cp.wait_send()

    return pl.pallas_call(
        body, name=name,
        out_shape=jax.ShapeDtypeStruct((N_DEV, r, cols), v.dtype),
        in_specs=[pl.BlockSpec(memory_space=pltpu.VMEM)],
        out_specs=pl.BlockSpec(memory_space=pltpu.VMEM),
        scratch_shapes=[pltpu.SemaphoreType.DMA((7,)), pltpu.SemaphoreType.DMA((7,))],
    )(v)


def _join_halves(shards):
    n = len(shards)

    def body(*refs):
        in_refs, out_refs = refs[:n], refs[n:2 * n]
        send_sems, recv_sems = refs[2 * n:]
        x, y, c = _place()
        sends = [_remote(src.at[c], out.at[c], send_sems, recv_sems, i, (x, y, 1 - c))
                 for i, (src, out) in enumerate(zip(in_refs, out_refs))]
        recvs = [_remote(src.at[c], out.at[1 - c], send_sems, recv_sems, i, (x, y, 1 - c))
                 for i, (src, out) in enumerate(zip(in_refs, out_refs))]
        for cp in sends:
            cp.start()
        for cp in recvs:
            cp.wait_recv()
        for cp in sends:
            cp.wait_send()

    hbm = pl.BlockSpec(memory_space=pl.ANY)
    return pl.pallas_call(
        body, name="grad_join_halves",
        out_shape=[jax.ShapeDtypeStruct(t.shape, t.dtype) for t in shards],
        in_specs=[hbm] * n, out_specs=[hbm] * n, input_output_aliases={i: i for i in range(n)},
        scratch_shapes=[pltpu.SemaphoreType.DMA((n,)), pltpu.SemaphoreType.DMA((n,))],
    )(*shards)


def _row_tile(rows, row_bytes, limit=1 << 20):
    best = 8
    for t in range(8, rows + 1, 8):
        if rows % t == 0 and t * row_bytes <= limit:
            best = t
    return best


def _sum_pieces(g, land, place, name):
    _, _, rh, cols = g.shape
    tr = _row_tile(rh, (N_DEV - 1) * cols * 4, 4 << 20)

    def body(p_ref, g_ref, l_ref, out_ref):
        acc = g_ref[...]
        for k in range(N_DEV - 1):
            acc = acc + l_ref[k].astype(F32)
        out_ref[...] = acc

    return pl.pallas_call(
        body, name=name,
        grid_spec=pltpu.PrefetchScalarGridSpec(
            num_scalar_prefetch=1, grid=(rh // tr,),
            in_specs=[pl.BlockSpec((None, None, tr, cols), lambda r, p: (p[0], p[1], r, 0)),
                      pl.BlockSpec((N_DEV - 1, tr, cols), lambda r, p: (0, r, 0))],
            out_specs=pl.BlockSpec((None, tr, cols), lambda r, p: (p[1], r, 0))),
        out_shape=jax.ShapeDtypeStruct((2, rh, cols), g.dtype),
        compiler_params=_params(("arbitrary",)),
    )(place, g, land)


def _adamw_math(w, g, m, v):
    m = ADAM_B1 * m + (1.0 - ADAM_B1) * g
    v = ADAM_B2 * v + (1.0 - ADAM_B2) * (g * g)
    m_hat = m / (1.0 - ADAM_B1 ** ADAM_STEP)
    v_hat = v / (1.0 - ADAM_B2 ** ADAM_STEP)
    delta = -ADAM_LR * (m_hat / (jnp.sqrt(v_hat) + ADAM_EPS) + ADAM_WD * w)
    return delta, m, v


def _adamw(w, g, m, v, name):
    r, cols = w.shape
    tr = _row_tile(r, cols * 4)

    def body(w_ref, g_ref, m_ref, v_ref, d_ref, nm_ref, nv_ref):
        d_ref[...], nm_ref[...], nv_ref[...] = _adamw_math(w_ref[...], g_ref[...], m_ref[...], v_ref[...])

    blk = pl.BlockSpec((tr, cols), lambda i: (i, 0))
    shape = jax.ShapeDtypeStruct((r, cols), F32)
    return pl.pallas_call(
        body, name=name, grid=(r // tr,), in_specs=[blk] * 4, out_specs=[blk] * 3, out_shape=[shape] * 3,
        compiler_params=_params(("arbitrary",)),
    )(w, g, m, v)


def _sum_devices(gathered):
    _, r, cols = gathered.shape

    def body(a_ref, g_ref):
        g = a_ref[0]
        for k in range(1, N_DEV):
            g = g + a_ref[k]
        g_ref[...] = g

    return pl.pallas_call(body, name="sum_small_grads", out_shape=jax.ShapeDtypeStruct((r, cols), F32))(gathered)


def _pack_conv(cw):
    flat = cw.reshape(-1)
    return jnp.pad(flat, (0, ROWS_CONV * D_MODEL - flat.shape[0])).reshape(ROWS_CONV, D_MODEL)


def _unpack_conv(rows):
    return rows.reshape(-1)[:CONV_WIDTH * UP_W // N_CHIPS].reshape(CONV_WIDTH, UP_W // N_CHIPS)


def _columns_to_shards(w):
    r, n = w.shape
    return jnp.transpose(w.reshape(r, N_CHIPS, n // N_CHIPS), (1, 0, 2))


def _shards_to_columns(w):
    _, r, n = w.shape
    return jnp.transpose(w, (1, 0, 2)).reshape(r, N_CHIPS * n)


def _pack_small(g_mix_pre, g_mix_post, g_ffn_pre, g_ffn_post, sinks, conv_b, loss):
    pad_row = lambda v: jnp.pad(v.reshape(1, -1), ((0, 0), (0, D_MODEL - v.size)))
    cb = jnp.pad(conv_b.reshape(-1), (0, 6 * D_MODEL - UP_W)).reshape(6, D_MODEL)
    zeros2 = jnp.zeros((2, D_MODEL), F32)
    return jnp.concatenate([g_mix_pre.reshape(1, -1), g_mix_post.reshape(1, -1), g_ffn_pre.reshape(1, -1),
                            g_ffn_post.reshape(1, -1), pad_row(sinks), pad_row(loss), zeros2, cb, zeros2], axis=0)


def _unpack_small(p):
    return dict(mix_pre_norm=p[0:1], mix_post_norm=p[1:2], ffn_pre_norm=p[2:3], ffn_post_norm=p[3:4],
                attn_sinks=p[4:5, :N_ATTN_HEADS], loss=p[5, 0], conv_b=p[8:14].reshape(1, -1)[:, :UP_W],
                conv_w=_unpack_conv(p[SMALL_ROWS:SMALL_ROWS + ROWS_CONV]))


def _local_step(x, target, g_mix_pre, w_in, sinks, w_out, g_mix_post, g_ffn_pre, w_up, conv_w, conv_b, w_down, g_ffn_post,
                distributed=True):
    s = x.shape[0]
    consts = _ret_constants()
    sin, cos = _rope_tables(s)

    by_half = lambda g, rows: g.reshape(N_CHIPS, 2, rows // (2 * N_CHIPS), g.shape[-1])

    if distributed:
        (h1, proj), level1 = _in_proj(x, g_mix_pre, w_in, _hosted_gather_level1([w_out, w_up, w_down]))
        (mix, states), (w_out, w_up, w_down) = _mixer_fwd(proj, sinks, sin, cos, consts, _hosted_gather_level2(level1))
        w_out, w_down = w_out.reshape(D_MODEL, D_MODEL), w_down.reshape(D_FF, D_MODEL)
        w_up = w_up.reshape(N_CHIPS, D_MODEL, UP_W // N_CHIPS)
    else:
        (h1, proj), _ = _in_proj(x, g_mix_pre, w_in)
        (mix, states), _ = _mixer_fwd(proj, sinks, sin, cos, consts)
    mixed, x1, h2, u0 = _out_up_proj(mix, x, w_out, g_mix_post, g_ffn_pre, w_up)
    y, dy2, dout, du, conv_acc, tail_acc = _ffn_tail(u0, x1, target, conv_w, conv_b, w_down, g_ffn_post)
    du0, dx1, dmixed, dmix, head_acc = _ffn_head_bwd(du, conv_w, w_up, x1, g_ffn_pre, dout, mixed, g_mix_post, w_out)

    d_w_down = _weight_grad(y, dy2, 512, "grad_w_down")
    d_w_up = _weight_grad(h2, du0, UP_W // N_CHIPS, "grad_w_up", by_block=True)
    d_w_out = _weight_grad(mix, dmixed, D_MODEL, "grad_w_out")
    early = [by_half(d_w_down, D_FF), by_half(d_w_up, N_CHIPS * D_MODEL), by_half(d_w_out, D_MODEL)]
    (dproj, dsinks), early_lands = _mixer_bwd(proj, dmix, states, sinks, sin, cos, consts,
                                              _hosted_scatter(early) if distributed else None)
    d_w_in = _columns_to_shards(_weight_grad(h1, dproj, IN_W // 2, "grad_w_in"))
    late = [by_half(d_w_in, N_CHIPS * D_MODEL)]
    (grad_x, in_acc), late_lands = _in_proj_bwd(dproj, w_in, x, g_mix_pre, dx1, _hosted_scatter(late) if distributed else None)

    small = _pack_small(in_acc[0], head_acc[1], head_acc[0], tail_acc[0], dsinks[0, :N_ATTN_HEADS], conv_acc[3],
                        jnp.sum(tail_acc[1]))
    d_conv = jnp.pad(conv_acc[0:CONV_WIDTH].reshape(-1), (0, CONV_FULL_ROWS * D_MODEL - CONV_WIDTH * UP_W))
    small = jnp.concatenate([small, d_conv.reshape(CONV_FULL_ROWS, D_MODEL)], axis=0)
    grads = dict(w_down=early[0], w_up=early[1], w_out=early[2], w_in=late[0])
    lands = dict(zip(["w_down", "w_up", "w_out", "w_in"], early_lands + late_lands))
    return grad_x, grads, lands, small


def kernel(x, mix_pre_norm, w_in, attn_sinks, w_out, mix_post_norm, ffn_pre_norm, w_up, conv_w, conv_b, w_down, ffn_post_norm, loss_target, m_mix_pre_norm, m_w_in, m_attn_sinks, m_w_out, m_mix_post_norm, m_ffn_pre_norm, m_w_up, m_conv_w, m_conv_b, m_w_down, m_ffn_post_norm, v_mix_pre_norm, v_w_in, v_attn_sinks, v_w_out, v_mix_post_norm, v_ffn_pre_norm, v_w_up, v_conv_w, v_conv_b, v_w_down, v_ffn_post_norm):
    cx, cy, cc = _place()
    shard = 2 * cx + cy

    conv_rows = jnp.pad(conv_w[0], ((0, 16 - CONV_WIDTH), (0, 0)))
    w_in_all, conv_all = _gather_now([w_in[0].astype(BF16), conv_rows], "gather_w_in")
    w_in_all = w_in_all.reshape(N_CHIPS, D_MODEL, IN_W // N_CHIPS)
    conv_full = _shards_to_columns(conv_all[:, 0, :CONV_WIDTH])

    grad_x, grads, lands, small = _local_step(
        x[0], loss_target[0], mix_pre_norm, _shards_to_columns(w_in_all), attn_sinks.reshape(-1), w_out[0].astype(BF16),
        mix_post_norm, ffn_pre_norm, w_up[0].astype(BF16), conv_full, conv_b, w_down[0].astype(BF16), ffn_post_norm)

    place = jnp.stack([shard, cc]).astype(jnp.int32)
    mats = ["w_in", "w_out", "w_up", "w_down"]
    halves = [_sum_pieces(grads[n], lands[n], place, "sum_grad_" + n) for n in mats]
    weights = dict(w_in=(w_in, m_w_in, v_w_in), w_out=(w_out, m_w_out, v_w_out), w_up=(w_up, m_w_up, v_w_up),
                   w_down=(w_down, m_w_down, v_w_down))
    mat_out = {}
    for n, joined in zip(mats, _join_halves(halves)):
        w, m, v = weights[n]
        g = joined.reshape(w.shape[1:])
        mat_out[n] = (g,) + tuple(_adamw(w[0], g, m[0], v[0], "adamw_" + n))

    small_sum = _sum_devices(_gather_small(small, "gather_small_grads"))
    d_conv_full = small_sum[SMALL_ROWS:].reshape(-1)[:CONV_WIDTH * UP_W].reshape(CONV_WIDTH, UP_W)
    d_conv_mine = lax.dynamic_slice_in_dim(d_conv_full, shard * (UP_W // N_CHIPS), UP_W // N_CHIPS, axis=1)
    g_s = jnp.concatenate([small_sum[:SMALL_ROWS], _pack_conv(d_conv_mine)], axis=0)
    zero = jnp.zeros((), F32)
    pack_rep = lambda a, b, c_, d, e, f, cw: jnp.concatenate([_pack_small(a, b, c_, d, e, f, zero), _pack_conv(cw[0])], axis=0)
    w_s = pack_rep(mix_pre_norm, mix_post_norm, ffn_pre_norm, ffn_post_norm, attn_sinks, conv_b, conv_w)
    m_s = pack_rep(m_mix_pre_norm, m_mix_post_norm, m_ffn_pre_norm, m_ffn_post_norm, m_attn_sinks, m_conv_b, m_conv_w)
    v_s = pack_rep(v_mix_pre_norm, v_mix_post_norm, v_ffn_pre_norm, v_ffn_post_norm, v_attn_sinks, v_conv_b, v_conv_w)
    delta_s, new_m_s, new_v_s = _adamw(w_s, g_s, m_s, v_s, "adamw_small")

    names = ["mix_pre_norm", "w_in", "attn_sinks", "w_out", "mix_post_norm", "ffn_pre_norm", "w_up", "conv_w", "conv_b",
             "w_down", "ffn_post_norm"]

    def leaves(which, packed_small):
        smalls = _unpack_small(packed_small)
        return [mat_out[n][which][None] if n in mat_out else (smalls[n][None] if n == "conv_w" else smalls[n]) for n in names]

    loss = _unpack_small(g_s)["loss"]
    return (loss, grad_x[None], *leaves(0, g_s), *leaves(1, delta_s), *leaves(2, new_m_s), *leaves(3, new_v_s))
```

```python
import math

import jax
import jax.numpy as jnp
from jax import lax
from jax.experimental import pallas as pl
from jax.experimental.pallas import tpu as pltpu

F32 = jnp.float32
BF16 = jnp.bfloat16

D_MODEL = 1024
HEAD_DIM = 64
ATTN_W = 512
N_ATTN_HEADS = 8
KV_W = 128
RET_W = 512
N_RET_HEADS = 4
RET_HEAD_DIM = 128
CHUNK = 128
IN_W = 2816
D_FF = 2816
UP_W = 2 * D_FF
CONV_WIDTH = 3
RMS_EPS = 1e-6
GN_EPS = 1e-6
MASK_VALUE = -1e30
ATTN_SCALE = HEAD_DIM ** -0.5
RET_K_SCALE = RET_HEAD_DIM ** -0.5
GELU_C = math.sqrt(2.0 / math.pi)
GELU_A = 0.044715

ADAM_LR = 0.001
ADAM_B1 = 0.9
ADAM_B2 = 0.999
ADAM_EPS = 1e-08
ADAM_WD = 0.01
ADAM_STEP = 10

N_CHIPS = 4
N_DEV = 8
MESH = pl.DeviceIdType.MESH
VMEM_LIMIT_V7X = 56 * 1024 * 1024
TOKEN_TILE = 256
BIG_TOKEN_TILE = 512
IN_PROJ_TOKEN_TILE = 1024
WEIGHT_GRAD_TOKENS = 2048
FFN_ROW_BLOCK = 64
HEAD_BWD_COLS = 512
MIXER_CHUNKS_PER_STEP = 4
Q_A0, KV_A0, Q_R0, K_R0, V_R0, G_R0 = 0, 512, 768, 1280, 1792, 2304

ROWS_CONV = 8
SMALL_ROWS = 16
CONV_FULL_ROWS = 24


def _params(sem=None, **kw):
    if sem is not None:
        kw["dimension_semantics"] = sem
    return pltpu.CompilerParams(vmem_limit_bytes=VMEM_LIMIT_V7X, **kw)


def _resident(shape):
    zeros = (0,) * len(shape)
    return pl.BlockSpec(shape, lambda *_: zeros, pipeline_mode=pl.Buffered(1))


class _Hosted:
    def __init__(self, ins, outs, aliases, n_pairs, n_local, start, finish):
        self.ins, self.outs, self.aliases = list(ins), list(outs), dict(aliases)
        self.n_pairs, self.n_local, self.start, self.finish = n_pairs, max(n_local, 1), start, finish


def _hosted_call(compute, *, name, grid, in_specs, out_specs, out_shape, scratch_shapes, args, hosted=None):
    params = _params(("arbitrary",) * len(grid))
    if hosted is None:
        res = pl.pallas_call(compute, name=name, grid=grid, in_specs=in_specs, out_specs=out_specs, out_shape=out_shape,
                             scratch_shapes=scratch_shapes, compiler_params=params)(*args)
        return list(res), []
    n_in, n_out, n_scr = len(in_specs), len(out_specs), len(scratch_shapes)
    h_in, h_out = len(hosted.ins), len(hosted.outs)

    def at(step_of):
        cond = pl.program_id(0) == step_of(grid[0])
        for d in range(1, len(grid)):
            cond = jnp.logical_and(cond, pl.program_id(d) == step_of(grid[d]))
        return cond

    def body(*refs):
        ins, refs = refs[:n_in], refs[n_in:]
        h_ins, refs = refs[:h_in], refs[h_in:]
        outs, refs = refs[:n_out], refs[n_out:]
        h_outs, refs = refs[:h_out], refs[h_out:]
        scr, sems = refs[:n_scr], refs[n_scr:]

        @pl.when(at(lambda n: 0))
        def _():
            hosted.start(h_ins, h_outs, *sems)

        compute(*ins, *outs, *scr)

        @pl.when(at(lambda n: n - 1))
        def _():
            hosted.finish(h_ins, h_outs, *sems)

    hbm = pl.BlockSpec(memory_space=pl.ANY)
    res = pl.pallas_call(
        body, name=name, grid=grid,
        in_specs=list(in_specs) + [hbm] * h_in, out_specs=list(out_specs) + [hbm] * h_out,
        out_shape=list(out_shape) + hosted.outs,
        scratch_shapes=list(scratch_shapes) + [pltpu.SemaphoreType.DMA((hosted.n_pairs,)), pltpu.SemaphoreType.DMA((hosted.n_pairs,)),
                                               pltpu.SemaphoreType.DMA((hosted.n_local,))],
        input_output_aliases={n_in + a: n_out + b for a, b in hosted.aliases.items()},
        compiler_params=params,
    )(*args, *hosted.ins)
    return list(res[:n_out]), list(res[n_out:])


def _dot(a, b):
    return jnp.dot(a, b, preferred_element_type=F32)


def _dot_nt(a, b):
    return lax.dot_general(a, b, (((1,), (1,)), ((), ())), preferred_element_type=F32)


def _dot_tn(a, b):
    return lax.dot_general(a, b, (((0,), (0,)), ((), ())), preferred_element_type=F32)


def _shift_matrix(n, by):
    row = lax.broadcasted_iota(jnp.int32, (n, n), 0)
    col = lax.broadcasted_iota(jnp.int32, (n, n), 1)
    return jnp.where(col == row + by, 1.0, 0.0).astype(BF16)


def _rstd(v):
    return lax.rsqrt(jnp.mean(v * v, axis=-1, keepdims=True) + RMS_EPS)


def _rms_bwd(dy, v, rstd, gain):
    n = v * rstd
    dgain = jnp.sum(dy * n, axis=0, keepdims=True)
    dn = dy * gain
    dv = rstd * (dn - n * jnp.mean(dn * n, axis=-1, keepdims=True))
    return dv, dgain


def _lane_lo(shape):
    return (lax.broadcasted_iota(jnp.int32, shape, 1) % 128) < HEAD_DIM


GROUP = N_ATTN_HEADS // (KV_W // HEAD_DIM)


def _attn_bias(first_chunk):
    qi = lax.broadcasted_iota(jnp.int32, (GROUP * CHUNK, 2 * CHUNK), 0) % CHUNK
    kj = lax.broadcasted_iota(jnp.int32, (GROUP * CHUNK, 2 * CHUNK), 1)
    valid = jnp.logical_and(kj > qi, kj <= qi + CHUNK)
    if first_chunk:
        valid = jnp.logical_and(valid, kj >= CHUNK)
    return jnp.where(valid, 0.0, MASK_VALUE)


def _half(shape, hk):
    lo = _lane_lo(shape)
    return lo if hk == 0 else jnp.logical_not(lo)


class _GroupMasks:
    def __init__(self, sk_ref):
        groups = range(KV_W // HEAD_DIM)
        self.q = [_half((CHUNK, 128), hk) for hk in groups]
        self.kv = [_half((2 * CHUNK, 128), hk) for hk in groups]
        self.sinks = [_group_sinks(sk_ref, hk) for hk in groups]


def _stack_heads(ref, row0, col0, hk, half):
    parts = []
    for j in range(GROUP):
        h = GROUP * hk + j
        pair = ref[row0:row0 + CHUNK, col0 + (h // 2) * 128:col0 + (h // 2 + 1) * 128].astype(F32)
        if h % 2 != hk:
            pair = pltpu.roll(pair, HEAD_DIM, 1)
        parts.append(jnp.where(half, pair, 0.0))
    return jnp.concatenate(parts, axis=0)


def _unstack_heads(stacked, hk):
    pairs = []
    for q in range(GROUP // 2):
        even, odd = stacked[2 * q * CHUNK:(2 * q + 1) * CHUNK], stacked[(2 * q + 1) * CHUNK:(2 * q + 2) * CHUNK]
        pairs.append(even + pltpu.roll(odd, HEAD_DIM, 1) if hk == 0 else pltpu.roll(even, HEAD_DIM, 1) + odd)
    return pairs


def _group_sinks(sk_ref, hk):
    row = lax.broadcasted_iota(jnp.int32, (GROUP * CHUNK, 1), 0)
    col = jnp.full((GROUP * CHUNK, 1), sk_ref[GROUP * hk], F32)
    for j in range(1, GROUP):
        col = jnp.where(row >= j * CHUNK, sk_ref[GROUP * hk + j], col)
    return col


def _attn_probs(q_b, kk_b, bias, sink):
    s = _dot_nt(q_b, kk_b) * ATTN_SCALE + bias
    m = jnp.maximum(jnp.max(s, axis=-1, keepdims=True), sink)
    e = jnp.exp(s - m)
    e_sink = jnp.exp(sink - m)
    inv = 1.0 / (jnp.sum(e, axis=-1, keepdims=True) + e_sink)
    return e * inv, e_sink * inv


def _even_lanes(shape):
    return (lax.broadcasted_iota(jnp.int32, shape, 1) % 2) == 0


def _swap2(v, even):
    return jnp.where(even, pltpu.roll(v, v.shape[1] - 1, 1), pltpu.roll(v, 1, 1))


def _tile4(v):
    return jnp.concatenate([v, v, v, v], axis=-1)


def _sigmoid(v):
    return 1.0 / (1.0 + jnp.exp(-v))


def _ret_constants():
    h = N_RET_HEADS
    log_gamma = jnp.log(1.0 - jnp.power(2.0, -5.0 - jnp.arange(h, dtype=F32)))
    idx = jnp.arange(CHUNK, dtype=F32)
    rel = idx[:, None] - idx[None, :]
    d_intra = jnp.where(rel[None] >= 0, jnp.exp(log_gamma[:, None, None] * jnp.maximum(rel, 0.0)[None]), 0.0)
    xi = jnp.exp(log_gamma[None, :] * (idx[:, None] + 1.0))
    zeta = jnp.exp(log_gamma[None, :] * (CHUNK - 1.0 - idx[:, None]))
    decay = jnp.exp(log_gamma * CHUNK)
    xi_full = jnp.repeat(xi, RET_HEAD_DIM, axis=1)
    zeta_full = jnp.repeat(zeta, RET_HEAD_DIM, axis=1)
    decay_full = jnp.broadcast_to(jnp.repeat(decay, RET_HEAD_DIM)[None, :], (8, RET_W))
    return d_intra.astype(F32), xi_full.astype(F32), zeta_full.astype(F32), decay_full.astype(F32)


def _rope_tables(s):
    pos = jnp.arange(s, dtype=F32)
    angle = 1.0 / jnp.power(10000.0, jnp.linspace(0.0, 1.0, RET_HEAD_DIM // 2, dtype=F32))
    angle = jnp.repeat(angle, 2)
    sign = jnp.where(jnp.arange(RET_HEAD_DIM) % 2 == 0, -1.0, 1.0).astype(F32)
    return jnp.sin(pos[:, None] * angle[None]) * sign[None], jnp.cos(pos[:, None] * angle[None])


def _in_proj(x, gain, w_in, hosted=None):
    s = x.shape[0]
    tm = min(IN_PROJ_TOKEN_TILE, s)

    def body(x_ref, g_ref, w_ref, h_ref, p_ref):
        xv = x_ref[...]
        h = (xv * _rstd(xv) * g_ref[...]).astype(BF16)
        h_ref[...] = h
        p_ref[...] = _dot(h, w_ref[...])

    return _hosted_call(
        body, name="in_proj", grid=(s // tm,),
        in_specs=[pl.BlockSpec((tm, D_MODEL), lambda i: (i, 0)), _resident((1, D_MODEL)), _resident((D_MODEL, IN_W))],
        out_specs=[pl.BlockSpec((tm, D_MODEL), lambda i: (i, 0)), pl.BlockSpec((tm, IN_W), lambda i: (i, 0))],
        out_shape=[jax.ShapeDtypeStruct((s, D_MODEL), BF16), jax.ShapeDtypeStruct((s, IN_W), F32)],
        scratch_shapes=[], args=(x, gain, w_in), hosted=hosted)


def _mixer_fwd(proj, sinks, sin, cos, consts, hosted=None):
    s = proj.shape[0]
    nc = s // CHUNK
    cps = MIXER_CHUNKS_PER_STEP
    d_intra, xi_full, zeta_full, decay_full = consts

    def body(sk_ref, p_ref, pkv_ref, sin_ref, cos_ref, dm_ref, xi_ref, ze_ref, dc_ref, mix_ref, st_ref, state):
        i = pl.program_id(0)

        @pl.when(i == 0)
        def _():
            state[...] = jnp.zeros_like(state)

        st = [state[h] for h in range(N_RET_HEADS)]
        bias_any = _attn_bias(False)
        bias_c0 = jnp.where(i == 0, _attn_bias(True), bias_any)
        even = _even_lanes((CHUNK, RET_W))
        masks = _GroupMasks(sk_ref)
        for c in range(cps):
            r0 = c * CHUNK
            rows = slice(r0, r0 + CHUNK)

            kv_cur = p_ref[rows, KV_A0:KV_A0 + 2 * KV_W]
            kv_prev = pkv_ref[...] if c == 0 else p_ref[r0 - CHUNK:r0, KV_A0:KV_A0 + 2 * KV_W]
            kk = jnp.concatenate([kv_prev[:, :KV_W], kv_cur[:, :KV_W]], axis=0)
            vv = jnp.concatenate([kv_prev[:, KV_W:], kv_cur[:, KV_W:]], axis=0)
            kk_b = kk.astype(BF16)
            bias = bias_c0 if c == 0 else bias_any
            for hk in range(KV_W // HEAD_DIM):
                q_b = _stack_heads(p_ref, r0, Q_A0, hk, masks.q[hk]).astype(BF16)
                p, _ = _attn_probs(q_b, kk_b, bias, masks.sinks[hk])
                v_b = jnp.where(masks.kv[hk], vv, 0.0).astype(BF16)
                for q, pair in enumerate(_unstack_heads(_dot(p.astype(BF16), v_b), hk)):
                    pi = (GROUP // 2) * hk + q
                    mix_ref[rows, pi * 128:(pi + 1) * 128] = pair.astype(BF16)

            sin4, cos4 = _tile4(sin_ref[rows, :]), _tile4(cos_ref[rows, :])
            q_r = p_ref[rows, Q_R0:Q_R0 + RET_W]
            k_r = p_ref[rows, K_R0:K_R0 + RET_W] * RET_K_SCALE
            q_r = q_r * cos4 + _swap2(q_r, even) * sin4
            k_r = k_r * cos4 + _swap2(k_r, even) * sin4
            kz = k_r * ze_ref[...]
            for h in range(N_RET_HEADS):
                sl = slice(h * RET_HEAD_DIM, (h + 1) * RET_HEAD_DIM)
                qh, kh = q_r[:, sl].astype(BF16), k_r[:, sl].astype(BF16)
                vh = p_ref[rows, V_R0 + h * RET_HEAD_DIM:V_R0 + (h + 1) * RET_HEAD_DIM].astype(BF16)
                st_ref[c, h] = st[h]
                a = _dot_nt(qh, kh) * dm_ref[h]
                qx = (q_r[:, sl] * xi_ref[:, sl]).astype(BF16)
                o = _dot(jnp.concatenate([a.astype(BF16), qx], axis=1), jnp.concatenate([vh, st[h].astype(BF16)], axis=0))
                st[h] = dc_ref[0:1, sl] * st[h] + _dot_tn(kz[:, sl].astype(BF16), vh)
                mu = jnp.mean(o, axis=-1, keepdims=True)
                oc = o - mu
                on = oc * lax.rsqrt(jnp.mean(oc * oc, axis=-1, keepdims=True) + GN_EPS)
                g = p_ref[rows, G_R0 + h * RET_HEAD_DIM:G_R0 + (h + 1) * RET_HEAD_DIM]
                mix_ref[rows, ATTN_W + h * RET_HEAD_DIM:ATTN_W + (h + 1) * RET_HEAD_DIM] = (g * _sigmoid(g) * on).astype(BF16)
        for h in range(N_RET_HEADS):
            state[h] = st[h]

    return _hosted_call(
        body, name="mixer_fwd", grid=(nc // cps,),
        in_specs=[
            pl.BlockSpec(memory_space=pltpu.SMEM),
            pl.BlockSpec((cps * CHUNK, IN_W), lambda i: (i, 0)),
            pl.BlockSpec((CHUNK, 2 * KV_W), lambda i: (jnp.maximum(cps * i - 1, 0), KV_A0 // (2 * KV_W))),
            pl.BlockSpec((cps * CHUNK, RET_HEAD_DIM), lambda i: (i, 0)),
            pl.BlockSpec((cps * CHUNK, RET_HEAD_DIM), lambda i: (i, 0)),
            _resident((N_RET_HEADS, CHUNK, CHUNK)), _resident((CHUNK, RET_W)), _resident((CHUNK, RET_W)), _resident((8, RET_W)),
        ],
        out_specs=[
            pl.BlockSpec((cps * CHUNK, D_MODEL), lambda i: (i, 0)),
            pl.BlockSpec((cps, N_RET_HEADS, RET_HEAD_DIM, RET_HEAD_DIM), lambda i: (i, 0, 0, 0)),
        ],
        out_shape=[jax.ShapeDtypeStruct((s, D_MODEL), BF16),
                   jax.ShapeDtypeStruct((nc, N_RET_HEADS, RET_HEAD_DIM, RET_HEAD_DIM), F32)],
        scratch_shapes=[pltpu.VMEM((N_RET_HEADS, RET_HEAD_DIM, RET_HEAD_DIM), F32)],
        args=(sinks, proj, proj, sin, cos, d_intra, xi_full, zeta_full, decay_full), hosted=hosted)


def _out_up_proj(mix, x, w_out, g_post, g_pre, w_up):
    s = x.shape[0]
    tm = min(BIG_TOKEN_TILE, s)
    blk = UP_W // N_CHIPS

    def body(mix_ref, x_ref, wo_ref, g2_ref, g3_ref, wu_ref, mixed_ref, x1_ref, h2_ref, u0_ref):
        mixed = _dot(mix_ref[...], wo_ref[...])
        mixed_ref[...] = mixed
        x1 = x_ref[...] + mixed * _rstd(mixed) * g2_ref[...]
        x1_ref[...] = x1
        h2 = (x1 * _rstd(x1) * g3_ref[...]).astype(BF16)
        h2_ref[...] = h2
        for k in range(N_CHIPS):
            u0_ref[:, k * blk:(k + 1) * blk] = _dot(h2, wu_ref[k]).astype(BF16)

    tok = lambda w: pl.BlockSpec((tm, w), lambda i: (i, 0))
    return pl.pallas_call(
        body, name="out_up_proj", grid=(s // tm,),
        in_specs=[tok(D_MODEL), tok(D_MODEL), _resident((D_MODEL, D_MODEL)), _resident((1, D_MODEL)), _resident((1, D_MODEL)),
                  _resident((N_CHIPS, D_MODEL, blk))],
        out_specs=[tok(D_MODEL), tok(D_MODEL), tok(D_MODEL), tok(UP_W)],
        out_shape=[jax.ShapeDtypeStruct((s, D_MODEL), F32), jax.ShapeDtypeStruct((s, D_MODEL), F32),
                   jax.ShapeDtypeStruct((s, D_MODEL), BF16), jax.ShapeDtypeStruct((s, UP_W), BF16)],
        compiler_params=_params(("arbitrary",)),
    )(mix, x, w_out, g_post, g_pre, w_up)


def _ffn_tail(u0, x1, target, conv_w, conv_b, w_down, g_post):
    s = x1.shape[0]
    tm = TOKEN_TILE
    last = s // tm - 1
    rb, lanes = FFN_ROW_BLOCK, 128

    def body(u0_ref, x1_ref, t_ref, cw_ref, cb_ref, wd_ref, g_ref,
             y_ref, dy2_ref, dout_ref, du_ref, cacc_ref, gacc_ref, u1_s, u2_s, carry, gelu_s, slope_s, dy_s, cacc):
        i = pl.program_id(0)

        @pl.when(i == 0)
        def _():
            carry[...] = jnp.zeros_like(carry)
            cacc[...] = jnp.zeros_like(cacc)
            gacc_ref[...] = jnp.zeros_like(gacc_ref)

        shift1, shift2 = _shift_matrix(tm, -1), _shift_matrix(tm, -2)
        r8 = lax.broadcasted_iota(jnp.int32, (8, 1), 0)
        wide = 2 * lanes

        def shift_block(col):
            cols = slice(col, col + wide)
            u1_s[:, cols] = _dot(shift1, u0_ref[:, cols])
            u2_s[:, cols] = _dot(shift2, u0_ref[:, cols])
            c14, c15 = carry[14:15, cols], carry[15:16, cols]
            u1_s[0:8, cols] = jnp.where(r8 == 0, c15, u1_s[0:8, cols])
            u2_s[0:8, cols] = jnp.where(r8 == 0, c14, jnp.where(r8 == 1, c15, u2_s[0:8, cols]))

        def taps(col):
            return (cw_ref[0:1, col:col + lanes], cw_ref[1:2, col:col + lanes], cw_ref[2:3, col:col + lanes],
                    cb_ref[0:1, col:col + lanes])

        def shifted(r0, col):
            return (u2_s[r0:r0 + rb, col:col + lanes], u1_s[r0:r0 + rb, col:col + lanes],
                    u0_ref[r0:r0 + rb, col:col + lanes].astype(F32))

        def conv(r0, col, w):
            u2, u1, uc = shifted(r0, col)
            return w[0] * u2 + w[1] * u1 + w[2] * uc + w[3]

        fold = lambda v: jnp.sum(v.reshape(rb // 8, 8, lanes), axis=0)

        shift_block(0)
        shift_block(D_FF)
        for j in range(D_FF // lanes):
            cg, cv = j * lanes, D_FF + j * lanes
            if cg % wide == 0 and cg + wide < D_FF:
                shift_block(cg + wide)
                shift_block(cv + wide)
            wg, wv = taps(cg), taps(cv)
            for r0 in range(0, tm, rb):
                gate, val = conv(r0, cg, wg), conv(r0, cv, wv)
                g2 = gate * gate
                th = jnp.tanh(gate * (GELU_C + GELU_C * GELU_A * g2))
                hp = 0.5 * th + 0.5
                gelu = gate * hp
                dgelu = hp + gate * (1.0 - th * th) * (0.5 * GELU_C + 1.5 * GELU_C * GELU_A * g2)
                y_ref[r0:r0 + rb, cg:cg + lanes] = (gelu * val).astype(BF16)
                gelu_s[r0:r0 + rb, cg:cg + lanes] = gelu
                slope_s[r0:r0 + rb, cg:cg + lanes] = dgelu * val

        y2 = _dot(y_ref[...], wd_ref[...])
        r4 = _rstd(y2)
        gain = g_ref[...]
        out = x1_ref[...] + y2 * r4 * gain
        diff = out - t_ref[...]
        dout = diff * (1.0 / D_MODEL)
        dout_ref[...] = dout
        dy2, dgain = _rms_bwd(dout, y2, r4, gain)
        dy2_b = dy2.astype(BF16)
        dy2_ref[...] = dy2_b
        gacc_ref[0:1, :] += dgain
        gacc_ref[1:2, :] += 0.5 * jnp.sum(diff * dout, axis=0, keepdims=True)
        carry[...] = u0_ref[tm - 16:tm, :].astype(F32)

        dy_s[:, 0:wide] = _dot_nt(dy2_b, wd_ref[0:wide, :])
        for j in range(D_FF // lanes):
            cg, cv = j * lanes, D_FF + j * lanes
            if cg % wide == 0 and cg + wide < D_FF:
                dy_s[:, cg + wide:cg + 2 * wide] = _dot_nt(dy2_b, wd_ref[cg + wide:cg + 2 * wide, :])
            acc = [[jnp.zeros((8, lanes), F32) for _ in range(CONV_WIDTH + 1)] for _ in range(2)]
            for r0 in range(0, tm, rb):
                dy = dy_s[r0:r0 + rb, cg:cg + lanes]
                d_gate = dy * slope_s[r0:r0 + rb, cg:cg + lanes]
                d_val = dy * gelu_s[r0:r0 + rb, cg:cg + lanes]
                for side, (col, d) in enumerate(((cg, d_gate), (cv, d_val))):
                    du_ref[r0:r0 + rb, col:col + lanes] = d.astype(BF16)
                    for k, u in enumerate(shifted(r0, col)):
                        acc[side][k] = acc[side][k] + fold(d * u)
                    acc[side][CONV_WIDTH] = acc[side][CONV_WIDTH] + fold(d)
            for side, col in enumerate((cg, cv)):
                for k in range(CONV_WIDTH + 1):
                    cacc[8 * k:8 * k + 8, col:col + lanes] += acc[side][k]

        @pl.when(i == last)
        def _():
            for k in range(CONV_WIDTH + 1):
                cacc_ref[k:k + 1, :] = jnp.sum(cacc[8 * k:8 * k + 8, :], axis=0, keepdims=True)
            cacc_ref[CONV_WIDTH + 1:8, :] = jnp.zeros((8 - CONV_WIDTH - 1, UP_W), F32)

    tok = lambda w: pl.BlockSpec((tm, w), lambda i: (i, 0))
    return pl.pallas_call(
        body, name="ffn_tail", grid=(s // tm,),
        in_specs=[tok(UP_W), tok(D_MODEL), tok(D_MODEL), _resident((CONV_WIDTH, UP_W)), _resident((1, UP_W)),
                  _resident((D_FF, D_MODEL)), _resident((1, D_MODEL))],
        out_specs=[tok(D_FF), tok(D_MODEL), tok(D_MODEL), tok(UP_W),
                   pl.BlockSpec((8, UP_W), lambda i: (0, 0)), pl.BlockSpec((8, D_MODEL), lambda i: (0, 0))],
        out_shape=[jax.ShapeDtypeStruct((s, D_FF), BF16), jax.ShapeDtypeStruct((s, D_MODEL), BF16),
                   jax.ShapeDtypeStruct((s, D_MODEL), F32), jax.ShapeDtypeStruct((s, UP_W), BF16),
                   jax.ShapeDtypeStruct((8, UP_W), F32), jax.ShapeDtypeStruct((8, D_MODEL), F32)],
        scratch_shapes=[pltpu.VMEM((tm, UP_W), F32), pltpu.VMEM((tm, UP_W), F32), pltpu.VMEM((16, UP_W), F32),
                        pltpu.VMEM((tm, D_FF), F32), pltpu.VMEM((tm, D_FF), F32),
                        pltpu.VMEM((tm, D_FF), F32), pltpu.VMEM((8 * (CONV_WIDTH + 1), UP_W), F32)],
        compiler_params=_params(("arbitrary",)),
    )(u0, x1, target, conv_w, conv_b, w_down, g_post)


def _ffn_head_bwd(du, conv_w, w_up, x1, g_pre, dout, mixed, g_post, w_out):
    s = x1.shape[0]
    tm = TOKEN_TILE
    nt = s // tm
    blk = UP_W // N_CHIPS

    def body(du_ref, halo_ref, cw_ref, wu_ref, x1_ref, g3_ref, dout_ref, mixed_ref, g2_ref, wo_ref,
             du0_ref, dx1_ref, dmixed_ref, dmix_ref, gacc_ref, dbuf):
        i = pl.program_id(0)

        @pl.when(i == 0)
        def _():
            gacc_ref[...] = jnp.zeros_like(gacc_ref)

        dbuf[0:tm, :] = du_ref[...].astype(F32)
        dbuf[tm:tm + 16, :] = jnp.where(i < nt - 1, halo_ref[...].astype(F32), 0.0)
        dh2 = jnp.zeros((tm, D_MODEL), F32)
        for k in range(N_CHIPS):
            for c0 in range(0, blk, HEAD_BWD_COLS):
                width = min(HEAD_BWD_COLS, blk - c0)
                cols = slice(k * blk + c0, k * blk + c0 + width)
                du0_b = (cw_ref[2:3, cols] * dbuf[0:tm, cols] + cw_ref[1:2, cols] * dbuf[1:1 + tm, cols]
                         + cw_ref[0:1, cols] * dbuf[2:2 + tm, cols]).astype(BF16)
                du0_ref[:, cols] = du0_b
                dh2 = dh2 + _dot_nt(du0_b, wu_ref[k, :, c0:c0 + width])
        x1 = x1_ref[...]
        d3, dg3 = _rms_bwd(dh2, x1, _rstd(x1), g3_ref[...])
        dx1 = dout_ref[...] + d3
        dx1_ref[...] = dx1
        mixed = mixed_ref[...]
        dmixed, dg2 = _rms_bwd(dx1, mixed, _rstd(mixed), g2_ref[...])
        dmixed_b = dmixed.astype(BF16)
        dmixed_ref[...] = dmixed_b
        dmix_ref[...] = _dot_nt(dmixed_b, wo_ref[...]).astype(BF16)
        gacc_ref[0:1, :] += dg3
        gacc_ref[1:2, :] += dg2

    tok = lambda w: pl.BlockSpec((tm, w), lambda i: (i, 0))
    halo = pl.BlockSpec((16, UP_W), lambda i: (jnp.minimum(i + 1, nt - 1) * (tm // 16), 0))
    return pl.pallas_call(
        body, name="ffn_head_bwd", grid=(nt,),
        in_specs=[tok(UP_W), halo, _resident((CONV_WIDTH, UP_W)), _resident((N_CHIPS, D_MODEL, blk)), tok(D_MODEL),
                  _resident((1, D_MODEL)), tok(D_MODEL), tok(D_MODEL), _resident((1, D_MODEL)), _resident((D_MODEL, D_MODEL))],
        out_specs=[tok(UP_W), tok(D_MODEL), tok(D_MODEL), tok(D_MODEL), pl.BlockSpec((8, D_MODEL), lambda i: (0, 0))],
        out_shape=[jax.ShapeDtypeStruct((s, UP_W), BF16), jax.ShapeDtypeStruct((s, D_MODEL), F32),
                   jax.ShapeDtypeStruct((s, D_MODEL), BF16), jax.ShapeDtypeStruct((s, D_MODEL), BF16),
                   jax.ShapeDtypeStruct((8, D_MODEL), F32)],
        scratch_shapes=[pltpu.VMEM((tm + 16, UP_W), F32)],
        compiler_params=_params(("arbitrary",)),
    )(du, du, conv_w, w_up, x1, g_pre, dout, mixed, g_post, w_out)


def _mixer_bwd(proj, dmix, states, sinks, sin, cos, consts, hosted=None):
    s = proj.shape[0]
    nc = s // CHUNK
    cps = MIXER_CHUNKS_PER_STEP
    nb = nc // cps
    d_intra, xi_full, zeta_full, decay_full = consts

    def body(sk_ref, p_ref, pkv_ref, dmix_ref, st_ref, sin_ref, cos_ref, dm_ref, xi_ref, ze_ref, dc_ref,
             dp_ref, dsk_ref, gstate, ckv, dsk_acc):
        i = pl.program_id(0)
        block = nb - 1 - i

        @pl.when(i == 0)
        def _():
            gstate[...] = jnp.zeros_like(gstate)
            ckv[...] = jnp.zeros_like(ckv)
            dsk_acc[...] = jnp.zeros_like(dsk_acc)

        gs_all = [gstate[h] for h in range(N_RET_HEADS)]
        later_kv = ckv[...]
        lane = lax.broadcasted_iota(jnp.int32, (CHUNK, 128), 1)
        dsk = jnp.zeros((CHUNK, 128), F32)
        bias_any = _attn_bias(False)
        bias_c0 = jnp.where(block == 0, _attn_bias(True), bias_any)
        even = _even_lanes((CHUNK, RET_W))
        masks = _GroupMasks(sk_ref)
        for c in reversed(range(cps)):
            r0 = c * CHUNK
            rows = slice(r0, r0 + CHUNK)

            kv_cur = p_ref[rows, KV_A0:KV_A0 + 2 * KV_W]
            kv_prev = pkv_ref[...] if c == 0 else p_ref[r0 - CHUNK:r0, KV_A0:KV_A0 + 2 * KV_W]
            kk = jnp.concatenate([kv_prev[:, :KV_W], kv_cur[:, :KV_W]], axis=0)
            vv = jnp.concatenate([kv_prev[:, KV_W:], kv_cur[:, KV_W:]], axis=0)
            kk_b, vv_b = kk.astype(BF16), vv.astype(BF16)
            bias = bias_c0 if c == 0 else bias_any
            dkk = jnp.zeros((2 * CHUNK, KV_W), F32)
            dvv = jnp.zeros((2 * CHUNK, KV_W), F32)
            for hk in range(KV_W // HEAD_DIM):
                q_b = _stack_heads(p_ref, r0, Q_A0, hk, masks.q[hk]).astype(BF16)
                do_b = _stack_heads(dmix_ref, r0, 0, hk, masks.q[hk]).astype(BF16)
                p, p_sink = _attn_probs(q_b, kk_b, bias, masks.sinks[hk])
                dpr = _dot_nt(do_b, vv_b)
                delta = jnp.sum(p * dpr, axis=-1, keepdims=True)
                ds_b = (p * (dpr - delta) * ATTN_SCALE).astype(BF16)
                dsink = -p_sink * delta
                for j in range(GROUP):
                    dsk = dsk + jnp.where(lane == GROUP * hk + j, dsink[j * CHUNK:(j + 1) * CHUNK], 0.0)
                k_b = jnp.where(masks.kv[hk], kk, 0.0).astype(BF16)
                for q, pair in enumerate(_unstack_heads(_dot(ds_b, k_b), hk)):
                    pi = (GROUP // 2) * hk + q
                    dp_ref[rows, Q_A0 + pi * 128:Q_A0 + (pi + 1) * 128] = pair.astype(BF16)
                dkk = dkk + _dot_tn(ds_b, q_b)
                dvv = dvv + _dot_tn(p.astype(BF16), do_b)
            dp_ref[rows, KV_A0:KV_A0 + KV_W] = (dkk[CHUNK:] + later_kv[:, :KV_W]).astype(BF16)
            dp_ref[rows, KV_A0 + KV_W:KV_A0 + 2 * KV_W] = (dvv[CHUNK:] + later_kv[:, KV_W:]).astype(BF16)
            later_kv = jnp.concatenate([dkk[:CHUNK], dvv[:CHUNK]], axis=1)

            sin4, cos4 = _tile4(sin_ref[rows, :]), _tile4(cos_ref[rows, :])
            q_r = p_ref[rows, Q_R0:Q_R0 + RET_W]
            k_r = p_ref[rows, K_R0:K_R0 + RET_W] * RET_K_SCALE
            q_r = q_r * cos4 + _swap2(q_r, even) * sin4
            k_r = k_r * cos4 + _swap2(k_r, even) * sin4
            kz = k_r * ze_ref[...]
            dq_parts, dk_parts = [], []
            for h in range(N_RET_HEADS):
                sl = slice(h * RET_HEAD_DIM, (h + 1) * RET_HEAD_DIM)
                qh, kh = q_r[:, sl].astype(BF16), k_r[:, sl].astype(BF16)
                vh = p_ref[rows, V_R0 + h * RET_HEAD_DIM:V_R0 + (h + 1) * RET_HEAD_DIM].astype(BF16)
                st_b = st_ref[c, h].astype(BF16)
                gs = gs_all[h]
                gs_b = gs.astype(BF16)
                xi_h = xi_ref[:, sl]
                dm = dm_ref[h]
                a_b = (_dot_nt(qh, kh) * dm).astype(BF16)
                qx = (q_r[:, sl] * xi_h).astype(BF16)
                o = _dot(jnp.concatenate([a_b, qx], axis=1), jnp.concatenate([vh, st_b], axis=0))
                mu = jnp.mean(o, axis=-1, keepdims=True)
                oc = o - mu
                rs = lax.rsqrt(jnp.mean(oc * oc, axis=-1, keepdims=True) + GN_EPS)
                on = oc * rs
                g = p_ref[rows, G_R0 + h * RET_HEAD_DIM:G_R0 + (h + 1) * RET_HEAD_DIM]
                sg = _sigmoid(g)
                dr = dmix_ref[rows, ATTN_W + h * RET_HEAD_DIM:ATTN_W + (h + 1) * RET_HEAD_DIM].astype(F32)
                dp_ref[rows, G_R0 + h * RET_HEAD_DIM:G_R0 + (h + 1) * RET_HEAD_DIM] = (
                    dr * on * (sg * (1.0 + g * (1.0 - sg)))).astype(BF16)
                don = dr * g * sg
                do = rs * (don - jnp.mean(don, axis=-1, keepdims=True) - on * jnp.mean(don * on, axis=-1, keepdims=True))
                do_b = do.astype(BF16)
                dox_b = (do * xi_h).astype(BF16)
                da_b = (_dot_nt(do_b, vh) * dm).astype(BF16)
                dq_parts.append(_dot(da_b, kh) + _dot_nt(dox_b, st_b))
                dk_parts.append(_dot_tn(da_b, qh) + ze_ref[:, sl] * _dot_nt(vh, gs_b))
                dv = _dot_tn(a_b, do_b) + _dot(kz[:, sl].astype(BF16), gs_b)
                dp_ref[rows, V_R0 + h * RET_HEAD_DIM:V_R0 + (h + 1) * RET_HEAD_DIM] = dv.astype(BF16)
                gs_all[h] = dc_ref[0:1, sl] * gs + _dot_tn(qh, dox_b)
            dq = jnp.concatenate(dq_parts, axis=-1)
            dk = jnp.concatenate(dk_parts, axis=-1)
            dp_ref[rows, Q_R0:Q_R0 + RET_W] = (dq * cos4 - _swap2(dq, even) * sin4).astype(BF16)
            dp_ref[rows, K_R0:K_R0 + RET_W] = (RET_K_SCALE * (dk * cos4 - _swap2(dk, even) * sin4)).astype(BF16)

        for h in range(N_RET_HEADS):
            gstate[h] = gs_all[h]
        ckv[...] = later_kv
        dsk_acc[...] += dsk

        @pl.when(i == nb - 1)
        def _():
            dsk_ref[...] = jnp.sum(dsk_acc[...], axis=0, keepdims=True)

    rev = lambda i: nb - 1 - i
    return _hosted_call(
        body, name="mixer_bwd", grid=(nb,),
        in_specs=[
            pl.BlockSpec(memory_space=pltpu.SMEM),
            pl.BlockSpec((cps * CHUNK, IN_W), lambda i: (rev(i), 0)),
            pl.BlockSpec((CHUNK, 2 * KV_W), lambda i: (jnp.maximum(cps * rev(i) - 1, 0), KV_A0 // (2 * KV_W))),
            pl.BlockSpec((cps * CHUNK, D_MODEL), lambda i: (rev(i), 0)),
            pl.BlockSpec((cps, N_RET_HEADS, RET_HEAD_DIM, RET_HEAD_DIM), lambda i: (rev(i), 0, 0, 0)),
            pl.BlockSpec((cps * CHUNK, RET_HEAD_DIM), lambda i: (rev(i), 0)),
            pl.BlockSpec((cps * CHUNK, RET_HEAD_DIM), lambda i: (rev(i), 0)),
            _resident((N_RET_HEADS, CHUNK, CHUNK)), _resident((CHUNK, RET_W)), _resident((CHUNK, RET_W)), _resident((8, RET_W)),
        ],
        out_specs=[pl.BlockSpec((cps * CHUNK, IN_W), lambda i: (rev(i), 0)), pl.BlockSpec((1, 128), lambda i: (0, 0))],
        out_shape=[jax.ShapeDtypeStruct((s, IN_W), BF16), jax.ShapeDtypeStruct((1, 128), F32)],
        scratch_shapes=[pltpu.VMEM((N_RET_HEADS, RET_HEAD_DIM, RET_HEAD_DIM), F32), pltpu.VMEM((CHUNK, 2 * KV_W), F32),
                        pltpu.VMEM((CHUNK, 128), F32)],
        args=(sinks, proj, proj, dmix, states, sin, cos, d_intra, xi_full, zeta_full, decay_full), hosted=hosted)


def _in_proj_bwd(dproj, w_in, x, gain, dx1, hosted=None):
    s = x.shape[0]
    tm = min(BIG_TOKEN_TILE, s)

    def body(dp_ref, w_ref, x_ref, g_ref, dx1_ref, dx_ref, gacc_ref):
        @pl.when(pl.program_id(0) == 0)
        def _():
            gacc_ref[...] = jnp.zeros_like(gacc_ref)

        dh = _dot_nt(dp_ref[...], w_ref[...])
        xv = x_ref[...]
        d1, dg = _rms_bwd(dh, xv, _rstd(xv), g_ref[...])
        dx_ref[...] = dx1_ref[...] + d1
        gacc_ref[0:1, :] += dg

    tok = lambda w: pl.BlockSpec((tm, w), lambda i: (i, 0))
    return _hosted_call(
        body, name="in_proj_bwd", grid=(s // tm,),
        in_specs=[tok(IN_W), _resident((D_MODEL, IN_W)), tok(D_MODEL), _resident((1, D_MODEL)), tok(D_MODEL)],
        out_specs=[tok(D_MODEL), pl.BlockSpec((8, D_MODEL), lambda i: (0, 0))],
        out_shape=[jax.ShapeDtypeStruct((s, D_MODEL), F32), jax.ShapeDtypeStruct((8, D_MODEL), F32)],
        scratch_shapes=[], args=(dproj, w_in, x, gain, dx1), hosted=hosted)


def _weight_grad(a, b, tn, name, by_block=False, hosted=None):
    s, m = a.shape
    n = b.shape[1]
    tk = min(WEIGHT_GRAD_TOKENS if m <= D_MODEL else WEIGHT_GRAD_TOKENS // 2, s)

    def body(a_ref, b_ref, o_ref):
        @pl.when(pl.program_id(1) == 0)
        def _():
            o_ref[...] = jnp.zeros_like(o_ref)

        o_ref[...] += _dot_tn(a_ref[...], b_ref[...])

    if by_block:
        out_spec = pl.BlockSpec((None, m, tn), lambda j, k: (j, 0, 0))
        out_shape = jax.ShapeDtypeStruct((n // tn, m, tn), F32)
    else:
        out_spec = pl.BlockSpec((m, tn), lambda j, k: (0, j))
        out_shape = jax.ShapeDtypeStruct((m, n), F32)
    (out,), lands = _hosted_call(
        body, name=name, grid=(n // tn, s // tk),
        in_specs=[pl.BlockSpec((tk, m), lambda j, k: (k, 0)), pl.BlockSpec((tk, tn), lambda j, k: (k, j))],
        out_specs=[out_spec], out_shape=[out_shape], scratch_shapes=[], args=(a, b), hosted=hosted)
    return out if hosted is None else (out, lands)


def _place():
    return lax.axis_index("x"), lax.axis_index("y"), lax.axis_index("c")


def _remote(src, dst, send_sems, recv_sems, k, to):
    return pltpu.make_async_remote_copy(src_ref=src, dst_ref=dst, send_sem=send_sems.at[k], recv_sem=recv_sems.at[k],
                                        device_id=to, device_id_type=MESH)


def _gather_level1_copies(w_refs, out_refs, send_sems, recv_sems, local_sems):
    x, y, c = _place()
    mine_at = 2 * x + y
    peers = [(x, y, 1 - c), (1 - x, y, c), (x, 1 - y, c), (1 - x, 1 - y, c)]
    local, sends, recvs = [], [], []
    for i, (w, out) in enumerate(zip(w_refs, out_refs)):
        half = w.shape[0] // 2
        src = w.at[pl.ds(pl.multiple_of(c * half, 16 if half % 16 == 0 else 8), half), :]
        mine = out.at[mine_at, c]
        local.append(pltpu.make_async_copy(src, mine, local_sems.at[i]))
        for k, p in enumerate(peers):
            sends.append(_remote(src, mine, send_sems, recv_sems, 4 * i + k, p))
            lands = out.at[mine_at, 1 - c] if k == 0 else out.at[2 * p[0] + p[1], c]
            recvs.append(_remote(src, lands, send_sems, recv_sems, 4 * i + k, p))
    return local, sends, recvs


def _gather_level1_start(w_refs, out_refs, send_sems, recv_sems, local_sems):
    local, sends, _ = _gather_level1_copies(w_refs, out_refs, send_sems, recv_sems, local_sems)
    for cp in local + sends:
        cp.start()


def _gather_level1_finish(w_refs, out_refs, send_sems, recv_sems, local_sems):
    local, sends, recvs = _gather_level1_copies(w_refs, out_refs, send_sems, recv_sems, local_sems)
    for cp in recvs:
        cp.wait_recv()
    for cp in sends:
        cp.wait_send()
    for cp in local:
        cp.wait()


def _gather_level2_copies(in_refs, out_refs, send_sems, recv_sems, local_sems):
    x, y, c = _place()
    chips = [(1 - x, y), (x, 1 - y), (1 - x, 1 - y)]
    sends, recvs = [], []
    for i, (src, out) in enumerate(zip(in_refs, out_refs)):
        for j, (px, py) in enumerate(chips):
            sends.append(_remote(src.at[2 * px + py, c], out.at[2 * px + py, c], send_sems, recv_sems, 3 * i + j, (x, y, 1 - c)))
            recvs.append(_remote(src.at[2 * px + py, c], out.at[2 * px + py, 1 - c], send_sems, recv_sems, 3 * i + j,
                                 (x, y, 1 - c)))
    return sends, recvs


def _gather_level2_start(in_refs, out_refs, send_sems, recv_sems, local_sems):
    for cp in _gather_level2_copies(in_refs, out_refs, send_sems, recv_sems, local_sems)[0]:
        cp.start()


def _gather_level2_finish(in_refs, out_refs, send_sems, recv_sems, local_sems):
    sends, recvs = _gather_level2_copies(in_refs, out_refs, send_sems, recv_sems, local_sems)
    for cp in recvs:
        cp.wait_recv()
    for cp in sends:
        cp.wait_send()


def _gathered_shape(w):
    r, cols = w.shape
    return jax.ShapeDtypeStruct((N_CHIPS, 2, r // 2, cols), w.dtype)


def _hosted_gather_level1(shards):
    n = len(shards)
    return _Hosted(shards, [_gathered_shape(w) for w in shards], {}, 4 * n, n, _gather_level1_start, _gather_level1_finish)


def _hosted_gather_level2(gathered):
    n = len(gathered)
    return _Hosted(gathered, [jax.ShapeDtypeStruct(g.shape, g.dtype) for g in gathered], {i: i for i in range(n)}, 3 * n, 0,
                   _gather_level2_start, _gather_level2_finish)


def _gather_now(shards, name, seq_len):
    n = len(shards)
    rows = min(512, seq_len)
    angle = 1.0 / jnp.power(10000.0, jnp.linspace(0.0, 1.0, RET_HEAD_DIM // 2, dtype=F32))
    sign = jnp.where(jnp.arange(RET_HEAD_DIM) % 2 == 0, -1.0, 1.0).astype(F32)
    angle_sign = jnp.concatenate([jnp.repeat(angle, 2)[None], sign[None], jnp.zeros((6, RET_HEAD_DIM), F32)], axis=0)

    def body(*refs):
        w_refs, as_ref, out_refs = list(refs[:n]), refs[n], list(refs[n + 1:2 * n + 1])
        sin_ref, cos_ref, send1, recv1, local1, send2, recv2 = refs[2 * n + 1:]
        _gather_level1_start(w_refs, out_refs, send1, recv1, local1)

        def fill(i, carry):
            r0 = pl.multiple_of(i * rows, rows)
            pos = (lax.broadcasted_iota(jnp.int32, (rows, RET_HEAD_DIM), 0) + i * rows).astype(F32)
            arg = pos * as_ref[0:1, :]
            sin_ref[pl.ds(r0, rows), :] = jnp.sin(arg) * as_ref[1:2, :]
            cos_ref[pl.ds(r0, rows), :] = jnp.cos(arg)
            return carry

        lax.fori_loop(0, seq_len // rows, fill, 0)
        _gather_level1_finish(w_refs, out_refs, send1, recv1, local1)
        _gather_level2_start(out_refs, out_refs, send2, recv2, None)
        _gather_level2_finish(out_refs, out_refs, send2, recv2, None)

    hbm, vmem = pl.BlockSpec(memory_space=pl.ANY), pl.BlockSpec(memory_space=pltpu.VMEM)
    table = jax.ShapeDtypeStruct((seq_len, RET_HEAD_DIM), F32)
    res = pl.pallas_call(
        body, name=name, out_shape=[_gathered_shape(w) for w in shards] + [table, table],
        in_specs=[hbm] * n + [vmem], out_specs=[hbm] * n + [vmem, vmem],
        scratch_shapes=[pltpu.SemaphoreType.DMA((4 * n,)), pltpu.SemaphoreType.DMA((4 * n,)), pltpu.SemaphoreType.DMA((n,)),
                        pltpu.SemaphoreType.DMA((3 * n,)), pltpu.SemaphoreType.DMA((3 * n,))],
        compiler_params=_params(),
    )(*shards, angle_sign)
    return res[:n], res[n], res[n + 1]


def _scatter_copies(g_refs, land_refs, send_sems, recv_sems, local_sems):
    x, y, c = _place()
    copies = []
    for i, (g, land) in enumerate(zip(g_refs, land_refs)):
        for k, (px, py, pc) in enumerate(_relations(x, y, c)):
            copies.append(_remote(g.at[2 * px + py, pc], land.at[k], send_sems, recv_sems, 7 * i + k, (px, py, pc)))
    return copies


def _scatter_start(g_refs, land_refs, send_sems, recv_sems, local_sems):
    for cp in _scatter_copies(g_refs, land_refs, send_sems, recv_sems, local_sems):
        cp.start()


def _scatter_finish(g_refs, land_refs, send_sems, recv_sems, local_sems):
    for cp in _scatter_copies(g_refs, land_refs, send_sems, recv_sems, local_sems):
        cp.wait()


def _hosted_scatter(grads):
    lands = [jax.ShapeDtypeStruct((N_DEV - 1,) + g.shape[2:], g.dtype) for g in grads]
    return _Hosted(grads, lands, {}, 7 * len(grads), 0, _scatter_start, _scatter_finish)


def _relations(x, y, c):
    rel = []
    for fx in (0, 1):
        for fy in (0, 1):
            for fc in (0, 1):
                if fx or fy or fc:
                    rel.append(((1 - x) if fx else x, (1 - y) if fy else y, (1 - c) if fc else c))
    return rel


def _gather_small(v, name):
    r, cols = v.shape

    def body(v_ref, out_ref, send_sems, recv_sems):
        x, y, c = _place()
        peers = _relations(x, y, c)

        def slot(p):
            return out_ref.at[4 * p[0] + 2 * p[1] + p[2]]

        out_ref[4 * x + 2 * y + c] = v_ref[...]
        sends = [pltpu.make_async_remote_copy(
            src_ref=v_ref, dst_ref=slot((x, y, c)), send_sem=send_sems.at[k], recv_sem=recv_sems.at[k],
            device_id=p, device_id_type=MESH) for k, p in enumerate(peers)]
        for cp in sends:
            cp.start()
        for k, p in enumerate(peers):
            pltpu.make_async_remote_copy(
                src_ref=v_ref, dst_ref=slot(p), send_sem=send_sems.at[k], recv_sem=recv_sems.at[k],
                device_id=p, device_id_type=MESH).wait_recv()
        for cp in sends:
            cp.wait_send()

    return pl.pallas_call(
        body, name=name,
        out_shape=jax.ShapeDtypeStruct((N_DEV, r, cols), v.dtype),
        in_specs=[pl.BlockSpec(memory_space=pltpu.VMEM)],
        out_specs=pl.BlockSpec(memory_space=pltpu.VMEM),
        scratch_shapes=[pltpu.SemaphoreType.DMA((7,)), pltpu.SemaphoreType.DMA((7,))],
    )(v)


def _join_halves(shards):
    n = len(shards)

    def body(*refs):
        in_refs, out_refs = refs[:n], refs[n:2 * n]
        send_sems, recv_sems = refs[2 * n:]
        x, y, c = _place()
        sends = [_remote(src.at[c], out.at[c], send_sems, recv_sems, i, (x, y, 1 - c))
                 for i, (src, out) in enumerate(zip(in_refs, out_refs))]
        recvs = [_remote(src.at[c], out.at[1 - c], send_sems, recv_sems, i, (x, y, 1 - c))
                 for i, (src, out) in enumerate(zip(in_refs, out_refs))]
        for cp in sends:
            cp.start()
        for cp in recvs:
            cp.wait_recv()
        for cp in sends:
            cp.wait_send()

    hbm = pl.BlockSpec(memory_space=pl.ANY)
    return pl.pallas_call(
        body, name="grad_join_halves",
        out_shape=[jax.ShapeDtypeStruct(t.shape, t.dtype) for t in shards],
        in_specs=[hbm] * n, out_specs=[hbm] * n, input_output_aliases={i: i for i in range(n)},
        scratch_shapes=[pltpu.SemaphoreType.DMA((n,)), pltpu.SemaphoreType.DMA((n,))],
    )(*shards)


def _row_tile(rows, row_bytes, limit=1 << 20):
    best = 8
    for t in range(8, rows + 1, 8):
        if rows % t == 0 and t * row_bytes <= limit:
            best = t
    return best


def _sum_pieces(g, land, place, name):
    _, _, rh, cols = g.shape
    tr = _row_tile(rh, (N_DEV - 1) * cols * 4, 4 << 20)

    def body(p_ref, g_ref, l_ref, out_ref):
        acc = g_ref[...]
        for k in range(N_DEV - 1):
            acc = acc + l_ref[k].astype(F32)
        out_ref[...] = acc

    return pl.pallas_call(
        body, name=name,
        grid_spec=pltpu.PrefetchScalarGridSpec(
            num_scalar_prefetch=1, grid=(rh // tr,),
            in_specs=[pl.BlockSpec((None, None, tr, cols), lambda r, p: (p[0], p[1], r, 0)),
                      pl.BlockSpec((N_DEV - 1, tr, cols), lambda r, p: (0, r, 0))],
            out_specs=pl.BlockSpec((None, tr, cols), lambda r, p: (p[1], r, 0))),
        out_shape=jax.ShapeDtypeStruct((2, rh, cols), g.dtype),
        compiler_params=_params(("arbitrary",)),
    )(place, g, land)


def _adamw_math(w, g, m, v):
    m = ADAM_B1 * m + (1.0 - ADAM_B1) * g
    v = ADAM_B2 * v + (1.0 - ADAM_B2) * (g * g)
    m_hat = m / (1.0 - ADAM_B1 ** ADAM_STEP)
    v_hat = v / (1.0 - ADAM_B2 ** ADAM_STEP)
    delta = -ADAM_LR * (m_hat / (jnp.sqrt(v_hat) + ADAM_EPS) + ADAM_WD * w)
    return delta, m, v


def _adamw(w, g, m, v, name):
    r, cols = w.shape
    tr = _row_tile(r, cols * 4)

    def body(w_ref, g_ref, m_ref, v_ref, d_ref, nm_ref, nv_ref):
        d_ref[...], nm_ref[...], nv_ref[...] = _adamw_math(w_ref[...], g_ref[...], m_ref[...], v_ref[...])

    blk = pl.BlockSpec((tr, cols), lambda i: (i, 0))
    shape = jax.ShapeDtypeStruct((r, cols), F32)
    return pl.pallas_call(
        body, name=name, grid=(r // tr,), in_specs=[blk] * 4, out_specs=[blk] * 3, out_shape=[shape] * 3,
        compiler_params=_params(("arbitrary",)),
    )(w, g, m, v)


def _sum_devices(gathered):
    _, r, cols = gathered.shape

    def body(a_ref, g_ref):
        g = a_ref[0]
        for k in range(1, N_DEV):
            g = g + a_ref[k]
        g_ref[...] = g

    return pl.pallas_call(body, name="sum_small_grads", out_shape=jax.ShapeDtypeStruct((r, cols), F32))(gathered)


def _pack_conv(cw):
    flat = cw.reshape(-1)
    return jnp.pad(flat, (0, ROWS_CONV * D_MODEL - flat.shape[0])).reshape(ROWS_CONV, D_MODEL)


def _unpack_conv(rows):
    return rows.reshape(-1)[:CONV_WIDTH * UP_W // N_CHIPS].reshape(CONV_WIDTH, UP_W // N_CHIPS)


def _columns_to_shards(w):
    r, n = w.shape
    return jnp.transpose(w.reshape(r, N_CHIPS, n // N_CHIPS), (1, 0, 2))


def _shards_to_columns(w):
    _, r, n = w.shape
    return jnp.transpose(w, (1, 0, 2)).reshape(r, N_CHIPS * n)


def _pack_small(g_mix_pre, g_mix_post, g_ffn_pre, g_ffn_post, sinks, conv_b, loss):
    pad_row = lambda v: jnp.pad(v.reshape(1, -1), ((0, 0), (0, D_MODEL - v.size)))
    cb = jnp.pad(conv_b.reshape(-1), (0, 6 * D_MODEL - UP_W)).reshape(6, D_MODEL)
    zeros2 = jnp.zeros((2, D_MODEL), F32)
    return jnp.concatenate([g_mix_pre.reshape(1, -1), g_mix_post.reshape(1, -1), g_ffn_pre.reshape(1, -1),
                            g_ffn_post.reshape(1, -1), pad_row(sinks), pad_row(loss), zeros2, cb, zeros2], axis=0)


def _unpack_small(p):
    return dict(mix_pre_norm=p[0:1], mix_post_norm=p[1:2], ffn_pre_norm=p[2:3], ffn_post_norm=p[3:4],
                attn_sinks=p[4:5, :N_ATTN_HEADS], loss=p[5, 0], conv_b=p[8:14].reshape(1, -1)[:, :UP_W],
                conv_w=_unpack_conv(p[SMALL_ROWS:SMALL_ROWS + ROWS_CONV]))


def _local_step(x, target, g_mix_pre, w_in, sinks, w_out, g_mix_post, g_ffn_pre, w_up, conv_w, conv_b, w_down, g_ffn_post,
                distributed=True, rope=None):
    s = x.shape[0]
    consts = _ret_constants()
    sin, cos = _rope_tables(s) if rope is None else rope

    by_half = lambda g, rows: g.reshape(N_CHIPS, 2, rows // (2 * N_CHIPS), g.shape[-1])

    if distributed:
        (h1, proj), level1 = _in_proj(x, g_mix_pre, w_in, _hosted_gather_level1([w_out, w_up, w_down]))
        (mix, states), (w_out, w_up, w_down) = _mixer_fwd(proj, sinks, sin, cos, consts, _hosted_gather_level2(level1))
        w_out, w_down = w_out.reshape(D_MODEL, D_MODEL), w_down.reshape(D_FF, D_MODEL)
        w_up = w_up.reshape(N_CHIPS, D_MODEL, UP_W // N_CHIPS)
    else:
        (h1, proj), _ = _in_proj(x, g_mix_pre, w_in)
        (mix, states), _ = _mixer_fwd(proj, sinks, sin, cos, consts)
    mixed, x1, h2, u0 = _out_up_proj(mix, x, w_out, g_mix_post, g_ffn_pre, w_up)
    y, dy2, dout, du, conv_acc, tail_acc = _ffn_tail(u0, x1, target, conv_w, conv_b, w_down, g_ffn_post)
    du0, dx1, dmixed, dmix, head_acc = _ffn_head_bwd(du, conv_w, w_up, x1, g_ffn_pre, dout, mixed, g_mix_post, w_out)

    d_w_down = _weight_grad(y, dy2, 512, "grad_w_down")
    d_w_up = _weight_grad(h2, du0, UP_W // N_CHIPS, "grad_w_up", by_block=True)
    d_w_out = _weight_grad(mix, dmixed, D_MODEL, "grad_w_out")
    early = [by_half(d_w_down, D_FF), by_half(d_w_up, N_CHIPS * D_MODEL), by_half(d_w_out, D_MODEL)]
    (dproj, dsinks), early_lands = _mixer_bwd(proj, dmix, states, sinks, sin, cos, consts,
                                              _hosted_scatter(early) if distributed else None)
    d_w_in = _columns_to_shards(_weight_grad(h1, dproj, IN_W // 2, "grad_w_in"))
    late = [by_half(d_w_in, N_CHIPS * D_MODEL)]
    (grad_x, in_acc), late_lands = _in_proj_bwd(dproj, w_in, x, g_mix_pre, dx1, _hosted_scatter(late) if distributed else None)

    small = _pack_small(in_acc[0], head_acc[1], head_acc[0], tail_acc[0], dsinks[0, :N_ATTN_HEADS], conv_acc[3],
                        jnp.sum(tail_acc[1]))
    d_conv = jnp.pad(conv_acc[0:CONV_WIDTH].reshape(-1), (0, CONV_FULL_ROWS * D_MODEL - CONV_WIDTH * UP_W))
    small = jnp.concatenate([small, d_conv.reshape(CONV_FULL_ROWS, D_MODEL)], axis=0)
    grads = dict(w_down=early[0], w_up=early[1], w_out=early[2], w_in=late[0])
    lands = dict(zip(["w_down", "w_up", "w_out", "w_in"], early_lands + late_lands))
    return grad_x, grads, lands, small


def kernel(x, mix_pre_norm, w_in, attn_sinks, w_out, mix_post_norm, ffn_pre_norm, w_up, conv_w, conv_b, w_down, ffn_post_norm, loss_target, m_mix_pre_norm, m_w_in, m_attn_sinks, m_w_out, m_mix_post_norm, m_ffn_pre_norm, m_w_up, m_conv_w, m_conv_b, m_w_down, m_ffn_post_norm, v_mix_pre_norm, v_w_in, v_attn_sinks, v_w_out, v_mix_post_norm, v_ffn_pre_norm, v_w_up, v_conv_w, v_conv_b, v_w_down, v_ffn_post_norm):
    cx, cy, cc = _place()
    shard = 2 * cx + cy

    conv_rows = jnp.pad(conv_w[0], ((0, 16 - CONV_WIDTH), (0, 0)))
    (w_in_all, conv_all), sin, cos = _gather_now([w_in[0].astype(BF16), conv_rows], "gather_w_in", x.shape[1])
    w_in_all = w_in_all.reshape(N_CHIPS, D_MODEL, IN_W // N_CHIPS)
    conv_full = _shards_to_columns(conv_all[:, 0, :CONV_WIDTH])

    grad_x, grads, lands, small = _local_step(
        x[0], loss_target[0], mix_pre_norm, _shards_to_columns(w_in_all), attn_sinks.reshape(-1), w_out[0].astype(BF16),
        mix_post_norm, ffn_pre_norm, w_up[0].astype(BF16), conv_full, conv_b, w_down[0].astype(BF16), ffn_post_norm,
        rope=(sin, cos))

    place = jnp.stack([shard, cc]).astype(jnp.int32)
    mats = ["w_in", "w_out", "w_up", "w_down"]
    halves = [_sum_pieces(grads[n], lands[n], place, "sum_grad_" + n) for n in mats]
    weights = dict(w_in=(w_in, m_w_in, v_w_in), w_out=(w_out, m_w_out, v_w_out), w_up=(w_up, m_w_up, v_w_up),
                   w_down=(w_down, m_w_down, v_w_down))
    mat_out = {}
    for n, joined in zip(mats, _join_halves(halves)):
        w, m, v = weights[n]
        g = joined.reshape(w.shape[1:])
        mat_out[n] = (g,) + tuple(_adamw(w[0], g, m[0], v[0], "adamw_" + n))

    small_sum = _sum_devices(_gather_small(small, "gather_small_grads"))
    d_conv_full = small_sum[SMALL_ROWS:].reshape(-1)[:CONV_WIDTH * UP_W].reshape(CONV_WIDTH, UP_W)
    d_conv_mine = lax.dynamic_slice_in_dim(d_conv_full, shard * (UP_W // N_CHIPS), UP_W // N_CHIPS, axis=1)
    g_s = jnp.concatenate([small_sum[:SMALL_ROWS], _pack_conv(d_conv_mine)], axis=0)
    zero = jnp.zeros((), F32)
    pack_rep = lambda a, b, c_, d, e, f, cw: jnp.concatenate([_pack_small(a, b, c_, d, e, f, zero), _pack_conv(cw[0])], axis=0)
    w_s = pack_rep(mix_pre_norm, mix_post_norm, ffn_pre_norm, ffn_post_norm, attn_sinks, conv_b, conv_w)
    m_s = pack_rep(m_mix_pre_norm, m_mix_post_norm, m_ffn_pre_norm, m_ffn_post_norm, m_attn_sinks, m_conv_b, m_conv_w)
    v_s = pack_rep(v_mix_pre_norm, v_mix_post_norm, v_ffn_pre_norm, v_ffn_post_norm, v_attn_sinks, v_conv_b, v_conv_w)
    delta_s, new_m_s, new_v_s = _adamw(w_s, g_s, m_s, v_s, "adamw_small")

    names = ["mix_pre_norm", "w_in", "attn_sinks", "w_out", "mix_post_norm", "ffn_pre_norm", "w_up", "conv_w", "conv_b",
             "w_down", "ffn_post_norm"]

    def leaves(which, packed_small):
        smalls = _unpack_small(packed_small)
        return [mat_out[n][which][None] if n in mat_out else (smalls[n][None] if n == "conv_w" else smalls[n]) for n in names]

    loss = _unpack_small(g_s)["loss"]
    return (loss, grad_x[None], *leaves(0, g_s), *leaves(1, delta_s), *leaves(2, new_m_s), *leaves(3, new_v_s))
```

```python
import math

import jax
import jax.numpy as jnp
from jax import lax
from jax.experimental import pallas as pl
from jax.experimental.pallas import tpu as pltpu

F32 = jnp.float32
BF16 = jnp.bfloat16

D_MODEL = 1024
HEAD_DIM = 64
ATTN_W = 512
N_ATTN_HEADS = 8
KV_W = 128
RET_W = 512
N_RET_HEADS = 4
RET_HEAD_DIM = 128
CHUNK = 128
IN_W = 2816
D_FF = 2816
UP_W = 2 * D_FF
CONV_WIDTH = 3
RMS_EPS = 1e-6
GN_EPS = 1e-6
MASK_VALUE = -1e30
ATTN_SCALE = HEAD_DIM ** -0.5
RET_K_SCALE = RET_HEAD_DIM ** -0.5
GELU_C = math.sqrt(2.0 / math.pi)
GELU_A = 0.044715

ADAM_LR = 0.001
ADAM_B1 = 0.9
ADAM_B2 = 0.999
ADAM_EPS = 1e-08
ADAM_WD = 0.01
ADAM_STEP = 10

N_CHIPS = 4
N_DEV = 8
MESH = pl.DeviceIdType.MESH
VMEM_LIMIT_V7X = 56 * 1024 * 1024
TOKEN_TILE = 256
BIG_TOKEN_TILE = 512
IN_PROJ_TOKEN_TILE = 1024
WEIGHT_GRAD_TOKENS = 2048
FFN_ROW_BLOCK = 64
HEAD_BWD_COLS = 512
MIXER_CHUNKS_PER_STEP = 4
Q_A0, KV_A0, Q_R0, K_R0, V_R0, G_R0 = 0, 512, 768, 1280, 1792, 2304

ROWS_CONV = 8
SMALL_ROWS = 16
CONV_FULL_ROWS = 24


def _params(sem=None, **kw):
    if sem is not None:
        kw["dimension_semantics"] = sem
    return pltpu.CompilerParams(vmem_limit_bytes=VMEM_LIMIT_V7X, **kw)


def _resident(shape):
    zeros = (0,) * len(shape)
    return pl.BlockSpec(shape, lambda *_: zeros, pipeline_mode=pl.Buffered(1))


class _Hosted:
    def __init__(self, ins, outs, aliases, n_pairs, n_local, start, finish):
        self.ins, self.outs, self.aliases = list(ins), list(outs), dict(aliases)
        self.n_pairs, self.n_local, self.start, self.finish = n_pairs, max(n_local, 1), start, finish


def _hosted_call(compute, *, name, grid, in_specs, out_specs, out_shape, scratch_shapes, args, hosted=None):
    params = _params(("arbitrary",) * len(grid))
    if hosted is None:
        res = pl.pallas_call(compute, name=name, grid=grid, in_specs=in_specs, out_specs=out_specs, out_shape=out_shape,
                             scratch_shapes=scratch_shapes, compiler_params=params)(*args)
        return list(res), []
    n_in, n_out, n_scr = len(in_specs), len(out_specs), len(scratch_shapes)
    h_in, h_out = len(hosted.ins), len(hosted.outs)

    def at(step_of):
        cond = pl.program_id(0) == step_of(grid[0])
        for d in range(1, len(grid)):
            cond = jnp.logical_and(cond, pl.program_id(d) == step_of(grid[d]))
        return cond

    def body(*refs):
        ins, refs = refs[:n_in], refs[n_in:]
        h_ins, refs = refs[:h_in], refs[h_in:]
        outs, refs = refs[:n_out], refs[n_out:]
        h_outs, refs = refs[:h_out], refs[h_out:]
        scr, sems = refs[:n_scr], refs[n_scr:]

        @pl.when(at(lambda n: 0))
        def _():
            hosted.start(h_ins, h_outs, *sems)

        compute(*ins, *outs, *scr)

        @pl.when(at(lambda n: n - 1))
        def _():
            hosted.finish(h_ins, h_outs, *sems)

    hbm = pl.BlockSpec(memory_space=pl.ANY)
    res = pl.pallas_call(
        body, name=name, grid=grid,
        in_specs=list(in_specs) + [hbm] * h_in, out_specs=list(out_specs) + [hbm] * h_out,
        out_shape=list(out_shape) + hosted.outs,
        scratch_shapes=list(scratch_shapes) + [pltpu.SemaphoreType.DMA((hosted.n_pairs,)), pltpu.SemaphoreType.DMA((hosted.n_pairs,)),
                                               pltpu.SemaphoreType.DMA((hosted.n_local,))],
        input_output_aliases={n_in + a: n_out + b for a, b in hosted.aliases.items()},
        compiler_params=params,
    )(*args, *hosted.ins)
    return list(res[:n_out]), list(res[n_out:])


def _dot(a, b):
    return jnp.dot(a, b, preferred_element_type=F32)


def _dot_nt(a, b):
    return lax.dot_general(a, b, (((1,), (1,)), ((), ())), preferred_element_type=F32)


def _dot_tn(a, b):
    return lax.dot_general(a, b, (((0,), (0,)), ((), ())), preferred_element_type=F32)


def _shift_matrix(n, by):
    row = lax.broadcasted_iota(jnp.int32, (n, n), 0)
    col = lax.broadcasted_iota(jnp.int32, (n, n), 1)
    return jnp.where(col == row + by, 1.0, 0.0).astype(BF16)


def _rstd(v):
    return lax.rsqrt(jnp.mean(v * v, axis=-1, keepdims=True) + RMS_EPS)


def _rms_bwd(dy, v, rstd, gain):
    n = v * rstd
    dgain = jnp.sum(dy * n, axis=0, keepdims=True)
    dn = dy * gain
    dv = rstd * (dn - n * jnp.mean(dn * n, axis=-1, keepdims=True))
    return dv, dgain


def _lane_lo(shape):
    return (lax.broadcasted_iota(jnp.int32, shape, 1) % 128) < HEAD_DIM


GROUP = N_ATTN_HEADS // (KV_W // HEAD_DIM)


def _attn_bias(first_chunk):
    qi = lax.broadcasted_iota(jnp.int32, (GROUP * CHUNK, 2 * CHUNK), 0) % CHUNK
    kj = lax.broadcasted_iota(jnp.int32, (GROUP * CHUNK, 2 * CHUNK), 1)
    valid = jnp.logical_and(kj > qi, kj <= qi + CHUNK)
    if first_chunk:
        valid = jnp.logical_and(valid, kj >= CHUNK)
    return jnp.where(valid, 0.0, MASK_VALUE)


def _half(shape, hk):
    lo = _lane_lo(shape)
    return lo if hk == 0 else jnp.logical_not(lo)


class _GroupMasks:
    def __init__(self, sk_ref):
        groups = range(KV_W // HEAD_DIM)
        self.q = [_half((CHUNK, 128), hk) for hk in groups]
        self.kv = [_half((2 * CHUNK, 128), hk) for hk in groups]
        self.sinks = [_group_sinks(sk_ref, hk) for hk in groups]


def _stack_heads(ref, row0, col0, hk, half):
    parts = []
    for j in range(GROUP):
        h = GROUP * hk + j
        pair = ref[row0:row0 + CHUNK, col0 + (h // 2) * 128:col0 + (h // 2 + 1) * 128].astype(F32)
        if h % 2 != hk:
            pair = pltpu.roll(pair, HEAD_DIM, 1)
        parts.append(jnp.where(half, pair, 0.0))
    return jnp.concatenate(parts, axis=0)


def _unstack_heads(stacked, hk):
    pairs = []
    for q in range(GROUP // 2):
        even, odd = stacked[2 * q * CHUNK:(2 * q + 1) * CHUNK], stacked[(2 * q + 1) * CHUNK:(2 * q + 2) * CHUNK]
        pairs.append(even + pltpu.roll(odd, HEAD_DIM, 1) if hk == 0 else pltpu.roll(even, HEAD_DIM, 1) + odd)
    return pairs


def _group_sinks(sk_ref, hk):
    row = lax.broadcasted_iota(jnp.int32, (GROUP * CHUNK, 1), 0)
    col = jnp.full((GROUP * CHUNK, 1), sk_ref[GROUP * hk], F32)
    for j in range(1, GROUP):
        col = jnp.where(row >= j * CHUNK, sk_ref[GROUP * hk + j], col)
    return col


def _attn_probs(q_b, kk_b, bias, sink):
    s = _dot_nt(q_b, kk_b) * ATTN_SCALE + bias
    m = jnp.maximum(jnp.max(s, axis=-1, keepdims=True), sink)
    e = jnp.exp(s - m)
    e_sink = jnp.exp(sink - m)
    inv = 1.0 / (jnp.sum(e, axis=-1, keepdims=True) + e_sink)
    return e * inv, e_sink * inv


def _even_lanes(shape):
    return (lax.broadcasted_iota(jnp.int32, shape, 1) % 2) == 0


def _swap2(v, even):
    return jnp.where(even, pltpu.roll(v, v.shape[1] - 1, 1), pltpu.roll(v, 1, 1))


def _tile4(v):
    return jnp.concatenate([v, v, v, v], axis=-1)


def _sigmoid(v):
    return 1.0 / (1.0 + jnp.exp(-v))


def _ret_constants():
    h = N_RET_HEADS
    log_gamma = jnp.log(1.0 - jnp.power(2.0, -5.0 - jnp.arange(h, dtype=F32)))
    idx = jnp.arange(CHUNK, dtype=F32)
    rel = idx[:, None] - idx[None, :]
    d_intra = jnp.where(rel[None] >= 0, jnp.exp(log_gamma[:, None, None] * jnp.maximum(rel, 0.0)[None]), 0.0)
    xi = jnp.exp(log_gamma[None, :] * (idx[:, None] + 1.0))
    zeta = jnp.exp(log_gamma[None, :] * (CHUNK - 1.0 - idx[:, None]))
    decay = jnp.exp(log_gamma * CHUNK)
    xi_full = jnp.repeat(xi, RET_HEAD_DIM, axis=1)
    zeta_full = jnp.repeat(zeta, RET_HEAD_DIM, axis=1)
    decay_full = jnp.broadcast_to(jnp.repeat(decay, RET_HEAD_DIM)[None, :], (8, RET_W))
    return d_intra.astype(F32), xi_full.astype(F32), zeta_full.astype(F32), decay_full.astype(F32)


def _rope_tables(s):
    pos = jnp.arange(s, dtype=F32)
    angle = 1.0 / jnp.power(10000.0, jnp.linspace(0.0, 1.0, RET_HEAD_DIM // 2, dtype=F32))
    angle = jnp.repeat(angle, 2)
    sign = jnp.where(jnp.arange(RET_HEAD_DIM) % 2 == 0, -1.0, 1.0).astype(F32)
    return jnp.sin(pos[:, None] * angle[None]) * sign[None], jnp.cos(pos[:, None] * angle[None])


def _in_proj(x, gain, w_in_t, hosted=None):
    s = x.shape[0]
    tm = min(IN_PROJ_TOKEN_TILE, s)

    def body(x_ref, g_ref, w_ref, h_ref, p_ref):
        xv = x_ref[...]
        h = (xv * _rstd(xv) * g_ref[...]).astype(BF16)
        h_ref[...] = h
        p_ref[...] = _dot_nt(h, w_ref[...])

    return _hosted_call(
        body, name="in_proj", grid=(s // tm,),
        in_specs=[pl.BlockSpec((tm, D_MODEL), lambda i: (i, 0)), _resident((1, D_MODEL)), _resident((IN_W, D_MODEL))],
        out_specs=[pl.BlockSpec((tm, D_MODEL), lambda i: (i, 0)), pl.BlockSpec((tm, IN_W), lambda i: (i, 0))],
        out_shape=[jax.ShapeDtypeStruct((s, D_MODEL), BF16), jax.ShapeDtypeStruct((s, IN_W), F32)],
        scratch_shapes=[], args=(x, gain, w_in_t), hosted=hosted)


def _mixer_fwd(proj, sinks, sin, cos, consts, hosted=None):
    s = proj.shape[0]
    nc = s // CHUNK
    cps = MIXER_CHUNKS_PER_STEP
    d_intra, xi_full, zeta_full, decay_full = consts

    def body(sk_ref, p_ref, pkv_ref, sin_ref, cos_ref, dm_ref, xi_ref, ze_ref, dc_ref, mix_ref, st_ref, state):
        i = pl.program_id(0)

        @pl.when(i == 0)
        def _():
            state[...] = jnp.zeros_like(state)

        st = [state[h] for h in range(N_RET_HEADS)]
        bias_any = _attn_bias(False)
        bias_c0 = jnp.where(i == 0, _attn_bias(True), bias_any)
        even = _even_lanes((CHUNK, RET_W))
        masks = _GroupMasks(sk_ref)
        for c in range(cps):
            r0 = c * CHUNK
            rows = slice(r0, r0 + CHUNK)

            kv_cur = p_ref[rows, KV_A0:KV_A0 + 2 * KV_W]
            kv_prev = pkv_ref[...] if c == 0 else p_ref[r0 - CHUNK:r0, KV_A0:KV_A0 + 2 * KV_W]
            kk = jnp.concatenate([kv_prev[:, :KV_W], kv_cur[:, :KV_W]], axis=0)
            vv = jnp.concatenate([kv_prev[:, KV_W:], kv_cur[:, KV_W:]], axis=0)
            kk_b = kk.astype(BF16)
            bias = bias_c0 if c == 0 else bias_any
            for hk in range(KV_W // HEAD_DIM):
                q_b = _stack_heads(p_ref, r0, Q_A0, hk, masks.q[hk]).astype(BF16)
                p, _ = _attn_probs(q_b, kk_b, bias, masks.sinks[hk])
                v_b = jnp.where(masks.kv[hk], vv, 0.0).astype(BF16)
                for q, pair in enumerate(_unstack_heads(_dot(p.astype(BF16), v_b), hk)):
                    pi = (GROUP // 2) * hk + q
                    mix_ref[rows, pi * 128:(pi + 1) * 128] = pair.astype(BF16)

            sin4, cos4 = _tile4(sin_ref[rows, :]), _tile4(cos_ref[rows, :])
            q_r = p_ref[rows, Q_R0:Q_R0 + RET_W]
            k_r = p_ref[rows, K_R0:K_R0 + RET_W] * RET_K_SCALE
            q_r = q_r * cos4 + _swap2(q_r, even) * sin4
            k_r = k_r * cos4 + _swap2(k_r, even) * sin4
            kz = k_r * ze_ref[...]
            for h in range(N_RET_HEADS):
                sl = slice(h * RET_HEAD_DIM, (h + 1) * RET_HEAD_DIM)
                qh, kh = q_r[:, sl].astype(BF16), k_r[:, sl].astype(BF16)
                vh = p_ref[rows, V_R0 + h * RET_HEAD_DIM:V_R0 + (h + 1) * RET_HEAD_DIM].astype(BF16)
                st_ref[c, h] = st[h]
                a = _dot_nt(qh, kh) * dm_ref[h]
                qx = (q_r[:, sl] * xi_ref[:, sl]).astype(BF16)
                o = _dot(jnp.concatenate([a.astype(BF16), qx], axis=1), jnp.concatenate([vh, st[h].astype(BF16)], axis=0))
                st[h] = dc_ref[0:1, sl] * st[h] + _dot_tn(kz[:, sl].astype(BF16), vh)
                mu = jnp.mean(o, axis=-1, keepdims=True)
                oc = o - mu
                on = oc * lax.rsqrt(jnp.mean(oc * oc, axis=-1, keepdims=True) + GN_EPS)
                g = p_ref[rows, G_R0 + h * RET_HEAD_DIM:G_R0 + (h + 1) * RET_HEAD_DIM]
                mix_ref[rows, ATTN_W + h * RET_HEAD_DIM:ATTN_W + (h + 1) * RET_HEAD_DIM] = (g * _sigmoid(g) * on).astype(BF16)
        for h in range(N_RET_HEADS):
            state[h] = st[h]

    return _hosted_call(
        body, name="mixer_fwd", grid=(nc // cps,),
        in_specs=[
            pl.BlockSpec(memory_space=pltpu.SMEM),
            pl.BlockSpec((cps * CHUNK, IN_W), lambda i: (i, 0)),
            pl.BlockSpec((CHUNK, 2 * KV_W), lambda i: (jnp.maximum(cps * i - 1, 0), KV_A0 // (2 * KV_W))),
            pl.BlockSpec((cps * CHUNK, RET_HEAD_DIM), lambda i: (i, 0)),
            pl.BlockSpec((cps * CHUNK, RET_HEAD_DIM), lambda i: (i, 0)),
            _resident((N_RET_HEADS, CHUNK, CHUNK)), _resident((CHUNK, RET_W)), _resident((CHUNK, RET_W)), _resident((8, RET_W)),
        ],
        out_specs=[
            pl.BlockSpec((cps * CHUNK, D_MODEL), lambda i: (i, 0)),
            pl.BlockSpec((cps, N_RET_HEADS, RET_HEAD_DIM, RET_HEAD_DIM), lambda i: (i, 0, 0, 0)),
        ],
        out_shape=[jax.ShapeDtypeStruct((s, D_MODEL), BF16),
                   jax.ShapeDtypeStruct((nc, N_RET_HEADS, RET_HEAD_DIM, RET_HEAD_DIM), F32)],
        scratch_shapes=[pltpu.VMEM((N_RET_HEADS, RET_HEAD_DIM, RET_HEAD_DIM), F32)],
        args=(sinks, proj, proj, sin, cos, d_intra, xi_full, zeta_full, decay_full), hosted=hosted)


def _out_up_proj(mix, x, w_out, g_post, g_pre, w_up):
    s = x.shape[0]
    tm = min(BIG_TOKEN_TILE, s)
    blk = UP_W // N_CHIPS

    def body(mix_ref, x_ref, wo_ref, g2_ref, g3_ref, wu_ref, mixed_ref, x1_ref, h2_ref, u0_ref):
        mixed = _dot(mix_ref[...], wo_ref[...])
        mixed_ref[...] = mixed
        x1 = x_ref[...] + mixed * _rstd(mixed) * g2_ref[...]
        x1_ref[...] = x1
        h2 = (x1 * _rstd(x1) * g3_ref[...]).astype(BF16)
        h2_ref[...] = h2
        for k in range(N_CHIPS):
            u0_ref[:, k * blk:(k + 1) * blk] = _dot(h2, wu_ref[k]).astype(BF16)

    tok = lambda w: pl.BlockSpec((tm, w), lambda i: (i, 0))
    return pl.pallas_call(
        body, name="out_up_proj", grid=(s // tm,),
        in_specs=[tok(D_MODEL), tok(D_MODEL), _resident((D_MODEL, D_MODEL)), _resident((1, D_MODEL)), _resident((1, D_MODEL)),
                  _resident((N_CHIPS, D_MODEL, blk))],
        out_specs=[tok(D_MODEL), tok(D_MODEL), tok(D_MODEL), tok(UP_W)],
        out_shape=[jax.ShapeDtypeStruct((s, D_MODEL), F32), jax.ShapeDtypeStruct((s, D_MODEL), F32),
                   jax.ShapeDtypeStruct((s, D_MODEL), BF16), jax.ShapeDtypeStruct((s, UP_W), BF16)],
        compiler_params=_params(("arbitrary",)),
    )(mix, x, w_out, g_post, g_pre, w_up)


def _ffn_tail(u0, x1, target, conv_w, conv_b, w_down, g_post):
    s = x1.shape[0]
    tm = TOKEN_TILE
    last = s // tm - 1
    rb, lanes = FFN_ROW_BLOCK, 128

    def body(u0_ref, x1_ref, t_ref, cw_ref, cb_ref, wd_ref, g_ref,
             y_ref, dy2_ref, dout_ref, du_ref, cacc_ref, gacc_ref, u1_s, u2_s, carry, gelu_s, slope_s, dy_s, cacc):
        i = pl.program_id(0)

        @pl.when(i == 0)
        def _():
            carry[...] = jnp.zeros_like(carry)
            cacc[...] = jnp.zeros_like(cacc)
            gacc_ref[...] = jnp.zeros_like(gacc_ref)

        shift1, shift2 = _shift_matrix(tm, -1), _shift_matrix(tm, -2)
        r8 = lax.broadcasted_iota(jnp.int32, (8, 1), 0)
        wide = 2 * lanes

        def shift_block(col):
            cols = slice(col, col + wide)
            u1_s[:, cols] = _dot(shift1, u0_ref[:, cols])
            u2_s[:, cols] = _dot(shift2, u0_ref[:, cols])
            c14, c15 = carry[14:15, cols], carry[15:16, cols]
            u1_s[0:8, cols] = jnp.where(r8 == 0, c15, u1_s[0:8, cols])
            u2_s[0:8, cols] = jnp.where(r8 == 0, c14, jnp.where(r8 == 1, c15, u2_s[0:8, cols]))

        def taps(col):
            return (cw_ref[0:1, col:col + lanes], cw_ref[1:2, col:col + lanes], cw_ref[2:3, col:col + lanes],
                    cb_ref[0:1, col:col + lanes])

        def shifted(r0, col):
            return (u2_s[r0:r0 + rb, col:col + lanes], u1_s[r0:r0 + rb, col:col + lanes],
                    u0_ref[r0:r0 + rb, col:col + lanes].astype(F32))

        def conv(r0, col, w):
            u2, u1, uc = shifted(r0, col)
            return w[0] * u2 + w[1] * u1 + w[2] * uc + w[3]

        fold = lambda v: jnp.sum(v.reshape(rb // 8, 8, lanes), axis=0)

        shift_block(0)
        shift_block(D_FF)
        for j in range(D_FF // lanes):
            cg, cv = j * lanes, D_FF + j * lanes
            if cg % wide == 0 and cg + wide < D_FF:
                shift_block(cg + wide)
                shift_block(cv + wide)
            wg, wv = taps(cg), taps(cv)
            for r0 in range(0, tm, rb):
                gate, val = conv(r0, cg, wg), conv(r0, cv, wv)
                g2 = gate * gate
                th = jnp.tanh(gate * (GELU_C + GELU_C * GELU_A * g2))
                hp = 0.5 * th + 0.5
                gelu = gate * hp
                dgelu = hp + gate * (1.0 - th * th) * (0.5 * GELU_C + 1.5 * GELU_C * GELU_A * g2)
                y_ref[r0:r0 + rb, cg:cg + lanes] = (gelu * val).astype(BF16)
                gelu_s[r0:r0 + rb, cg:cg + lanes] = gelu
                slope_s[r0:r0 + rb, cg:cg + lanes] = dgelu * val

        y2 = _dot(y_ref[...], wd_ref[...])
        r4 = _rstd(y2)
        gain = g_ref[...]
        out = x1_ref[...] + y2 * r4 * gain
        diff = out - t_ref[...]
        dout = diff * (1.0 / D_MODEL)
        dout_ref[...] = dout
        dy2, dgain = _rms_bwd(dout, y2, r4, gain)
        dy2_b = dy2.astype(BF16)
        dy2_ref[...] = dy2_b
        gacc_ref[0:1, :] += dgain
        gacc_ref[1:2, :] += 0.5 * jnp.sum(diff * dout, axis=0, keepdims=True)
        carry[...] = u0_ref[tm - 16:tm, :].astype(F32)

        dy_s[:, 0:wide] = _dot_nt(dy2_b, wd_ref[0:wide, :])
        for j in range(D_FF // lanes):
            cg, cv = j * lanes, D_FF + j * lanes
            if cg % wide == 0 and cg + wide < D_FF:
                dy_s[:, cg + wide:cg + 2 * wide] = _dot_nt(dy2_b, wd_ref[cg + wide:cg + 2 * wide, :])
            acc = [[jnp.zeros((8, lanes), F32) for _ in range(CONV_WIDTH + 1)] for _ in range(2)]
            for r0 in range(0, tm, rb):
                dy = dy_s[r0:r0 + rb, cg:cg + lanes]
                d_gate = dy * slope_s[r0:r0 + rb, cg:cg + lanes]
                d_val = dy * gelu_s[r0:r0 + rb, cg:cg + lanes]
                for side, (col, d) in enumerate(((cg, d_gate), (cv, d_val))):
                    du_ref[r0:r0 + rb, col:col + lanes] = d.astype(BF16)
                    for k, u in enumerate(shifted(r0, col)):
                        acc[side][k] = acc[side][k] + fold(d * u)
                    acc[side][CONV_WIDTH] = acc[side][CONV_WIDTH] + fold(d)
            for side, col in enumerate((cg, cv)):
                for k in range(CONV_WIDTH + 1):
                    cacc[8 * k:8 * k + 8, col:col + lanes] += acc[side][k]

        @pl.when(i == last)
        def _():
            for k in range(CONV_WIDTH + 1):
                cacc_ref[k:k + 1, :] = jnp.sum(cacc[8 * k:8 * k + 8, :], axis=0, keepdims=True)
            cacc_ref[CONV_WIDTH + 1:8, :] = jnp.zeros((8 - CONV_WIDTH - 1, UP_W), F32)

    tok = lambda w: pl.BlockSpec((tm, w), lambda i: (i, 0))
    return pl.pallas_call(
        body, name="ffn_tail", grid=(s // tm,),
        in_specs=[tok(UP_W), tok(D_MODEL), tok(D_MODEL), _resident((CONV_WIDTH, UP_W)), _resident((1, UP_W)),
                  _resident((D_FF, D_MODEL)), _resident((1, D_MODEL))],
        out_specs=[tok(D_FF), tok(D_MODEL), tok(D_MODEL), tok(UP_W),
                   pl.BlockSpec((8, UP_W), lambda i: (0, 0)), pl.BlockSpec((8, D_MODEL), lambda i: (0, 0))],
        out_shape=[jax.ShapeDtypeStruct((s, D_FF), BF16), jax.ShapeDtypeStruct((s, D_MODEL), BF16),
                   jax.ShapeDtypeStruct((s, D_MODEL), F32), jax.ShapeDtypeStruct((s, UP_W), BF16),
                   jax.ShapeDtypeStruct((8, UP_W), F32), jax.ShapeDtypeStruct((8, D_MODEL), F32)],
        scratch_shapes=[pltpu.VMEM((tm, UP_W), F32), pltpu.VMEM((tm, UP_W), F32), pltpu.VMEM((16, UP_W), F32),
                        pltpu.VMEM((tm, D_FF), F32), pltpu.VMEM((tm, D_FF), F32),
                        pltpu.VMEM((tm, D_FF), F32), pltpu.VMEM((8 * (CONV_WIDTH + 1), UP_W), F32)],
        compiler_params=_params(("arbitrary",)),
    )(u0, x1, target, conv_w, conv_b, w_down, g_post)


def _ffn_head_bwd(du, conv_w, w_up, x1, g_pre, dout, mixed, g_post, w_out):
    s = x1.shape[0]
    tm = TOKEN_TILE
    nt = s // tm
    blk = UP_W // N_CHIPS

    def body(du_ref, halo_ref, cw_ref, wu_ref, x1_ref, g3_ref, dout_ref, mixed_ref, g2_ref, wo_ref,
             du0_ref, dx1_ref, dmixed_ref, dmix_ref, gacc_ref, dbuf):
        i = pl.program_id(0)

        @pl.when(i == 0)
        def _():
            gacc_ref[...] = jnp.zeros_like(gacc_ref)

        dbuf[0:tm, :] = du_ref[...].astype(F32)
        dbuf[tm:tm + 16, :] = jnp.where(i < nt - 1, halo_ref[...].astype(F32), 0.0)
        dh2 = jnp.zeros((tm, D_MODEL), F32)
        for k in range(N_CHIPS):
            for c0 in range(0, blk, HEAD_BWD_COLS):
                width = min(HEAD_BWD_COLS, blk - c0)
                cols = slice(k * blk + c0, k * blk + c0 + width)
                du0_b = (cw_ref[2:3, cols] * dbuf[0:tm, cols] + cw_ref[1:2, cols] * dbuf[1:1 + tm, cols]
                         + cw_ref[0:1, cols] * dbuf[2:2 + tm, cols]).astype(BF16)
                du0_ref[:, cols] = du0_b
                dh2 = dh2 + _dot_nt(du0_b, wu_ref[k, :, c0:c0 + width])
        x1 = x1_ref[...]
        d3, dg3 = _rms_bwd(dh2, x1, _rstd(x1), g3_ref[...])
        dx1 = dout_ref[...] + d3
        dx1_ref[...] = dx1
        mixed = mixed_ref[...]
        dmixed, dg2 = _rms_bwd(dx1, mixed, _rstd(mixed), g2_ref[...])
        dmixed_b = dmixed.astype(BF16)
        dmixed_ref[...] = dmixed_b
        dmix_ref[...] = _dot_nt(dmixed_b, wo_ref[...]).astype(BF16)
        gacc_ref[0:1, :] += dg3
        gacc_ref[1:2, :] += dg2

    tok = lambda w: pl.BlockSpec((tm, w), lambda i: (i, 0))
    halo = pl.BlockSpec((16, UP_W), lambda i: (jnp.minimum(i + 1, nt - 1) * (tm // 16), 0))
    return pl.pallas_call(
        body, name="ffn_head_bwd", grid=(nt,),
        in_specs=[tok(UP_W), halo, _resident((CONV_WIDTH, UP_W)), _resident((N_CHIPS, D_MODEL, blk)), tok(D_MODEL),
                  _resident((1, D_MODEL)), tok(D_MODEL), tok(D_MODEL), _resident((1, D_MODEL)), _resident((D_MODEL, D_MODEL))],
        out_specs=[tok(UP_W), tok(D_MODEL), tok(D_MODEL), tok(D_MODEL), pl.BlockSpec((8, D_MODEL), lambda i: (0, 0))],
        out_shape=[jax.ShapeDtypeStruct((s, UP_W), BF16), jax.ShapeDtypeStruct((s, D_MODEL), F32),
                   jax.ShapeDtypeStruct((s, D_MODEL), BF16), jax.ShapeDtypeStruct((s, D_MODEL), BF16),
                   jax.ShapeDtypeStruct((8, D_MODEL), F32)],
        scratch_shapes=[pltpu.VMEM((tm + 16, UP_W), F32)],
        compiler_params=_params(("arbitrary",)),
    )(du, du, conv_w, w_up, x1, g_pre, dout, mixed, g_post, w_out)


def _mixer_bwd(proj, dmix, states, sinks, sin, cos, consts, hosted=None):
    s = proj.shape[0]
    nc = s // CHUNK
    cps = MIXER_CHUNKS_PER_STEP
    nb = nc // cps
    d_intra, xi_full, zeta_full, decay_full = consts

    def body(sk_ref, p_ref, pkv_ref, dmix_ref, st_ref, sin_ref, cos_ref, dm_ref, xi_ref, ze_ref, dc_ref,
             dp_ref, dsk_ref, gstate, ckv, dsk_acc):
        i = pl.program_id(0)
        block = nb - 1 - i

        @pl.when(i == 0)
        def _():
            gstate[...] = jnp.zeros_like(gstate)
            ckv[...] = jnp.zeros_like(ckv)
            dsk_acc[...] = jnp.zeros_like(dsk_acc)

        gs_all = [gstate[h] for h in range(N_RET_HEADS)]
        later_kv = ckv[...]
        lane = lax.broadcasted_iota(jnp.int32, (CHUNK, 128), 1)
        dsk = jnp.zeros((CHUNK, 128), F32)
        bias_any = _attn_bias(False)
        bias_c0 = jnp.where(block == 0, _attn_bias(True), bias_any)
        even = _even_lanes((CHUNK, RET_W))
        masks = _GroupMasks(sk_ref)
        for c in reversed(range(cps)):
            r0 = c * CHUNK
            rows = slice(r0, r0 + CHUNK)

            kv_cur = p_ref[rows, KV_A0:KV_A0 + 2 * KV_W]
            kv_prev = pkv_ref[...] if c == 0 else p_ref[r0 - CHUNK:r0, KV_A0:KV_A0 + 2 * KV_W]
            kk = jnp.concatenate([kv_prev[:, :KV_W], kv_cur[:, :KV_W]], axis=0)
            vv = jnp.concatenate([kv_prev[:, KV_W:], kv_cur[:, KV_W:]], axis=0)
            kk_b, vv_b = kk.astype(BF16), vv.astype(BF16)
            bias = bias_c0 if c == 0 else bias_any
            dkk = jnp.zeros((2 * CHUNK, KV_W), F32)
            dvv = jnp.zeros((2 * CHUNK, KV_W), F32)
            for hk in range(KV_W // HEAD_DIM):
                q_b = _stack_heads(p_ref, r0, Q_A0, hk, masks.q[hk]).astype(BF16)
                do_b = _stack_heads(dmix_ref, r0, 0, hk, masks.q[hk]).astype(BF16)
                p, p_sink = _attn_probs(q_b, kk_b, bias, masks.sinks[hk])
                dpr = _dot_nt(do_b, vv_b)
                delta = jnp.sum(p * dpr, axis=-1, keepdims=True)
                ds_b = (p * (dpr - delta) * ATTN_SCALE).astype(BF16)
                dsink = -p_sink * delta
                for j in range(GROUP):
                    dsk = dsk + jnp.where(lane == GROUP * hk + j, dsink[j * CHUNK:(j + 1) * CHUNK], 0.0)
                k_b = jnp.where(masks.kv[hk], kk, 0.0).astype(BF16)
                for q, pair in enumerate(_unstack_heads(_dot(ds_b, k_b), hk)):
                    pi = (GROUP // 2) * hk + q
                    dp_ref[rows, Q_A0 + pi * 128:Q_A0 + (pi + 1) * 128] = pair.astype(BF16)
                dkk = dkk + _dot_tn(ds_b, q_b)
                dvv = dvv + _dot_tn(p.astype(BF16), do_b)
            dp_ref[rows, KV_A0:KV_A0 + KV_W] = (dkk[CHUNK:] + later_kv[:, :KV_W]).astype(BF16)
            dp_ref[rows, KV_A0 + KV_W:KV_A0 + 2 * KV_W] = (dvv[CHUNK:] + later_kv[:, KV_W:]).astype(BF16)
            later_kv = jnp.concatenate([dkk[:CHUNK], dvv[:CHUNK]], axis=1)

            sin4, cos4 = _tile4(sin_ref[rows, :]), _tile4(cos_ref[rows, :])
            q_r = p_ref[rows, Q_R0:Q_R0 + RET_W]
            k_r = p_ref[rows, K_R0:K_R0 + RET_W] * RET_K_SCALE
            q_r = q_r * cos4 + _swap2(q_r, even) * sin4
            k_r = k_r * cos4 + _swap2(k_r, even) * sin4
            kz = k_r * ze_ref[...]
            dq_parts, dk_parts = [], []
            for h in range(N_RET_HEADS):
                sl = slice(h * RET_HEAD_DIM, (h + 1) * RET_HEAD_DIM)
                qh, kh = q_r[:, sl].astype(BF16), k_r[:, sl].astype(BF16)
                vh = p_ref[rows, V_R0 + h * RET_HEAD_DIM:V_R0 + (h + 1) * RET_HEAD_DIM].astype(BF16)
                st_b = st_ref[c, h].astype(BF16)
                gs = gs_all[h]
                gs_b = gs.astype(BF16)
                xi_h = xi_ref[:, sl]
                dm = dm_ref[h]
                a_b = (_dot_nt(qh, kh) * dm).astype(BF16)
                qx = (q_r[:, sl] * xi_h).astype(BF16)
                o = _dot(jnp.concatenate([a_b, qx], axis=1), jnp.concatenate([vh, st_b], axis=0))
                mu = jnp.mean(o, axis=-1, keepdims=True)
                oc = o - mu
                rs = lax.rsqrt(jnp.mean(oc * oc, axis=-1, keepdims=True) + GN_EPS)
                on = oc * rs
                g = p_ref[rows, G_R0 + h * RET_HEAD_DIM:G_R0 + (h + 1) * RET_HEAD_DIM]
                sg = _sigmoid(g)
                dr = dmix_ref[rows, ATTN_W + h * RET_HEAD_DIM:ATTN_W + (h + 1) * RET_HEAD_DIM].astype(F32)
                dp_ref[rows, G_R0 + h * RET_HEAD_DIM:G_R0 + (h + 1) * RET_HEAD_DIM] = (
                    dr * on * (sg * (1.0 + g * (1.0 - sg)))).astype(BF16)
                don = dr * g * sg
                do = rs * (don - jnp.mean(don, axis=-1, keepdims=True) - on * jnp.mean(don * on, axis=-1, keepdims=True))
                do_b = do.astype(BF16)
                dox_b = (do * xi_h).astype(BF16)
                da_b = (_dot_nt(do_b, vh) * dm).astype(BF16)
                dq_parts.append(_dot(da_b, kh) + _dot_nt(dox_b, st_b))
                dk_parts.append(_dot_tn(da_b, qh) + ze_ref[:, sl] * _dot_nt(vh, gs_b))
                dv = _dot_tn(a_b, do_b) + _dot(kz[:, sl].astype(BF16), gs_b)
                dp_ref[rows, V_R0 + h * RET_HEAD_DIM:V_R0 + (h + 1) * RET_HEAD_DIM] = dv.astype(BF16)
                gs_all[h] = dc_ref[0:1, sl] * gs + _dot_tn(qh, dox_b)
            dq = jnp.concatenate(dq_parts, axis=-1)
            dk = jnp.concatenate(dk_parts, axis=-1)
            dp_ref[rows, Q_R0:Q_R0 + RET_W] = (dq * cos4 - _swap2(dq, even) * sin4).astype(BF16)
            dp_ref[rows, K_R0:K_R0 + RET_W] = (RET_K_SCALE * (dk * cos4 - _swap2(dk, even) * sin4)).astype(BF16)

        for h in range(N_RET_HEADS):
            gstate[h] = gs_all[h]
        ckv[...] = later_kv
        dsk_acc[...] += dsk

        @pl.when(i == nb - 1)
        def _():
            dsk_ref[...] = jnp.sum(dsk_acc[...], axis=0, keepdims=True)

    rev = lambda i: nb - 1 - i
    return _hosted_call(
        body, name="mixer_bwd", grid=(nb,),
        in_specs=[
            pl.BlockSpec(memory_space=pltpu.SMEM),
            pl.BlockSpec((cps * CHUNK, IN_W), lambda i: (rev(i), 0)),
            pl.BlockSpec((CHUNK, 2 * KV_W), lambda i: (jnp.maximum(cps * rev(i) - 1, 0), KV_A0 // (2 * KV_W))),
            pl.BlockSpec((cps * CHUNK, D_MODEL), lambda i: (rev(i), 0)),
            pl.BlockSpec((cps, N_RET_HEADS, RET_HEAD_DIM, RET_HEAD_DIM), lambda i: (rev(i), 0, 0, 0)),
            pl.BlockSpec((cps * CHUNK, RET_HEAD_DIM), lambda i: (rev(i), 0)),
            pl.BlockSpec((cps * CHUNK, RET_HEAD_DIM), lambda i: (rev(i), 0)),
            _resident((N_RET_HEADS, CHUNK, CHUNK)), _resident((CHUNK, RET_W)), _resident((CHUNK, RET_W)), _resident((8, RET_W)),
        ],
        out_specs=[pl.BlockSpec((cps * CHUNK, IN_W), lambda i: (rev(i), 0)), pl.BlockSpec((1, 128), lambda i: (0, 0))],
        out_shape=[jax.ShapeDtypeStruct((s, IN_W), BF16), jax.ShapeDtypeStruct((1, 128), F32)],
        scratch_shapes=[pltpu.VMEM((N_RET_HEADS, RET_HEAD_DIM, RET_HEAD_DIM), F32), pltpu.VMEM((CHUNK, 2 * KV_W), F32),
                        pltpu.VMEM((CHUNK, 128), F32)],
        args=(sinks, proj, proj, dmix, states, sin, cos, d_intra, xi_full, zeta_full, decay_full), hosted=hosted)


def _in_proj_bwd(dproj, w_in_t, x, gain, dx1, hosted=None):
    s = x.shape[0]
    tm = min(BIG_TOKEN_TILE, s)

    def body(dp_ref, w_ref, x_ref, g_ref, dx1_ref, dx_ref, gacc_ref):
        @pl.when(pl.program_id(0) == 0)
        def _():
            gacc_ref[...] = jnp.zeros_like(gacc_ref)

        dh = _dot(dp_ref[...], w_ref[...])
        xv = x_ref[...]
        d1, dg = _rms_bwd(dh, xv, _rstd(xv), g_ref[...])
        dx_ref[...] = dx1_ref[...] + d1
        gacc_ref[0:1, :] += dg

    tok = lambda w: pl.BlockSpec((tm, w), lambda i: (i, 0))
    return _hosted_call(
        body, name="in_proj_bwd", grid=(s // tm,),
        in_specs=[tok(IN_W), _resident((IN_W, D_MODEL)), tok(D_MODEL), _resident((1, D_MODEL)), tok(D_MODEL)],
        out_specs=[tok(D_MODEL), pl.BlockSpec((8, D_MODEL), lambda i: (0, 0))],
        out_shape=[jax.ShapeDtypeStruct((s, D_MODEL), F32), jax.ShapeDtypeStruct((8, D_MODEL), F32)],
        scratch_shapes=[], args=(dproj, w_in_t, x, gain, dx1), hosted=hosted)


def _weight_grad(a, b, tn, name, by_block=False, hosted=None):
    s, m = a.shape
    n = b.shape[1]
    tk = min(WEIGHT_GRAD_TOKENS if m <= D_MODEL else WEIGHT_GRAD_TOKENS // 2, s)

    def body(a_ref, b_ref, o_ref):
        @pl.when(pl.program_id(1) == 0)
        def _():
            o_ref[...] = jnp.zeros_like(o_ref)

        o_ref[...] += _dot_tn(a_ref[...], b_ref[...])

    if by_block:
        out_spec = pl.BlockSpec((None, m, tn), lambda j, k: (j, 0, 0))
        out_shape = jax.ShapeDtypeStruct((n // tn, m, tn), F32)
    else:
        out_spec = pl.BlockSpec((m, tn), lambda j, k: (0, j))
        out_shape = jax.ShapeDtypeStruct((m, n), F32)
    (out,), lands = _hosted_call(
        body, name=name, grid=(n // tn, s // tk),
        in_specs=[pl.BlockSpec((tk, m), lambda j, k: (k, 0)), pl.BlockSpec((tk, tn), lambda j, k: (k, j))],
        out_specs=[out_spec], out_shape=[out_shape], scratch_shapes=[], args=(a, b), hosted=hosted)
    return out if hosted is None else (out, lands)


def _place():
    return lax.axis_index("x"), lax.axis_index("y"), lax.axis_index("c")


def _remote(src, dst, send_sems, recv_sems, k, to):
    return pltpu.make_async_remote_copy(src_ref=src, dst_ref=dst, send_sem=send_sems.at[k], recv_sem=recv_sems.at[k],
                                        device_id=to, device_id_type=MESH)


def _gather_level1_copies(w_refs, out_refs, send_sems, recv_sems, local_sems):
    x, y, c = _place()
    mine_at = 2 * x + y
    peers = [(x, y, 1 - c), (1 - x, y, c), (x, 1 - y, c), (1 - x, 1 - y, c)]
    local, sends, recvs = [], [], []
    for i, (w, out) in enumerate(zip(w_refs, out_refs)):
        half = w.shape[0] // 2
        src = w.at[pl.ds(pl.multiple_of(c * half, 16 if half % 16 == 0 else 8), half), :]
        mine = out.at[mine_at, c]
        local.append(pltpu.make_async_copy(src, mine, local_sems.at[i]))
        for k, p in enumerate(peers):
            sends.append(_remote(src, mine, send_sems, recv_sems, 4 * i + k, p))
            lands = out.at[mine_at, 1 - c] if k == 0 else out.at[2 * p[0] + p[1], c]
            recvs.append(_remote(src, lands, send_sems, recv_sems, 4 * i + k, p))
    return local, sends, recvs


def _gather_level1_start(w_refs, out_refs, send_sems, recv_sems, local_sems):
    local, sends, _ = _gather_level1_copies(w_refs, out_refs, send_sems, recv_sems, local_sems)
    for cp in local + sends:
        cp.start()


def _gather_level1_finish(w_refs, out_refs, send_sems, recv_sems, local_sems):
    local, sends, recvs = _gather_level1_copies(w_refs, out_refs, send_sems, recv_sems, local_sems)
    for cp in recvs:
        cp.wait_recv()
    for cp in sends:
        cp.wait_send()
    for cp in local:
        cp.wait()


def _gather_level2_copies(in_refs, out_refs, send_sems, recv_sems, local_sems):
    x, y, c = _place()
    chips = [(1 - x, y), (x, 1 - y), (1 - x, 1 - y)]
    sends, recvs = [], []
    for i, (src, out) in enumerate(zip(in_refs, out_refs)):
        for j, (px, py) in enumerate(chips):
            sends.append(_remote(src.at[2 * px + py, c], out.at[2 * px + py, c], send_sems, recv_sems, 3 * i + j, (x, y, 1 - c)))
            recvs.append(_remote(src.at[2 * px + py, c], out.at[2 * px + py, 1 - c], send_sems, recv_sems, 3 * i + j,
                                 (x, y, 1 - c)))
    return sends, recvs


def _gather_level2_start(in_refs, out_refs, send_sems, recv_sems, local_sems):
    for cp in _gather_level2_copies(in_refs, out_refs, send_sems, recv_sems, local_sems)[0]:
        cp.start()


def _gather_level2_finish(in_refs, out_refs, send_sems, recv_sems, local_sems):
    sends, recvs = _gather_level2_copies(in_refs, out_refs, send_sems, recv_sems, local_sems)
    for cp in recvs:
        cp.wait_recv()
    for cp in sends:
        cp.wait_send()


def _gathered_shape(w):
    r, cols = w.shape
    return jax.ShapeDtypeStruct((N_CHIPS, 2, r // 2, cols), w.dtype)


def _hosted_gather_level1(shards):
    n = len(shards)
    return _Hosted(shards, [_gathered_shape(w) for w in shards], {}, 4 * n, n, _gather_level1_start, _gather_level1_finish)


def _hosted_gather_level2(gathered):
    n = len(gathered)
    return _Hosted(gathered, [jax.ShapeDtypeStruct(g.shape, g.dtype) for g in gathered], {i: i for i in range(n)}, 3 * n, 0,
                   _gather_level2_start, _gather_level2_finish)


def _gather_now(shards, name, seq_len):
    n = len(shards)
    rows = min(512, seq_len)
    angle = 1.0 / jnp.power(10000.0, jnp.linspace(0.0, 1.0, RET_HEAD_DIM // 2, dtype=F32))
    sign = jnp.where(jnp.arange(RET_HEAD_DIM) % 2 == 0, -1.0, 1.0).astype(F32)
    angle_sign = jnp.concatenate([jnp.repeat(angle, 2)[None], sign[None], jnp.zeros((6, RET_HEAD_DIM), F32)], axis=0)

    def body(*refs):
        w_refs, as_ref, out_refs = list(refs[:n]), refs[n], list(refs[n + 1:2 * n + 1])
        sin_ref, cos_ref, send1, recv1, local1, send2, recv2 = refs[2 * n + 1:]
        _gather_level1_start(w_refs, out_refs, send1, recv1, local1)

        def fill(i, carry):
            r0 = pl.multiple_of(i * rows, rows)
            pos = (lax.broadcasted_iota(jnp.int32, (rows, RET_HEAD_DIM), 0) + i * rows).astype(F32)
            arg = pos * as_ref[0:1, :]
            sin_ref[pl.ds(r0, rows), :] = jnp.sin(arg) * as_ref[1:2, :]
            cos_ref[pl.ds(r0, rows), :] = jnp.cos(arg)
            return carry

        lax.fori_loop(0, seq_len // rows, fill, 0)
        _gather_level1_finish(w_refs, out_refs, send1, recv1, local1)
        _gather_level2_start(out_refs, out_refs, send2, recv2, None)
        _gather_level2_finish(out_refs, out_refs, send2, recv2, None)

    hbm, vmem = pl.BlockSpec(memory_space=pl.ANY), pl.BlockSpec(memory_space=pltpu.VMEM)
    table = jax.ShapeDtypeStruct((seq_len, RET_HEAD_DIM), F32)
    res = pl.pallas_call(
        body, name=name, out_shape=[_gathered_shape(w) for w in shards] + [table, table],
        in_specs=[hbm] * n + [vmem], out_specs=[hbm] * n + [vmem, vmem],
        scratch_shapes=[pltpu.SemaphoreType.DMA((4 * n,)), pltpu.SemaphoreType.DMA((4 * n,)), pltpu.SemaphoreType.DMA((n,)),
                        pltpu.SemaphoreType.DMA((3 * n,)), pltpu.SemaphoreType.DMA((3 * n,))],
        compiler_params=_params(),
    )(*shards, angle_sign)
    return res[:n], res[n], res[n + 1]


def _scatter_copies(g_refs, land_refs, send_sems, recv_sems, local_sems):
    x, y, c = _place()
    copies = []
    for i, (g, land) in enumerate(zip(g_refs, land_refs)):
        for k, (px, py, pc) in enumerate(_relations(x, y, c)):
            copies.append(_remote(g.at[2 * px + py, pc], land.at[k], send_sems, recv_sems, 7 * i + k, (px, py, pc)))
    return copies


def _scatter_start(g_refs, land_refs, send_sems, recv_sems, local_sems):
    for cp in _scatter_copies(g_refs, land_refs, send_sems, recv_sems, local_sems):
        cp.start()


def _scatter_finish(g_refs, land_refs, send_sems, recv_sems, local_sems):
    for cp in _scatter_copies(g_refs, land_refs, send_sems, recv_sems, local_sems):
        cp.wait()


def _hosted_scatter(grads):
    lands = [jax.ShapeDtypeStruct((N_DEV - 1,) + g.shape[2:], g.dtype) for g in grads]
    return _Hosted(grads, lands, {}, 7 * len(grads), 0, _scatter_start, _scatter_finish)


def _relations(x, y, c):
    rel = []
    for fx in (0, 1):
        for fy in (0, 1):
            for fc in (0, 1):
                if fx or fy or fc:
                    rel.append(((1 - x) if fx else x, (1 - y) if fy else y, (1 - c) if fc else c))
    return rel


def _gather_small(v, name):
    r, cols = v.shape

    def body(v_ref, out_ref, send_sems, recv_sems):
        x, y, c = _place()
        peers = _relations(x, y, c)

        def slot(p):
            return out_ref.at[4 * p[0] + 2 * p[1] + p[2]]

        out_ref[4 * x + 2 * y + c] = v_ref[...]
        sends = [pltpu.make_async_remote_copy(
            src_ref=v_ref, dst_ref=slot((x, y, c)), send_sem=send_sems.at[k], recv_sem=recv_sems.at[k],
            device_id=p, device_id_type=MESH) for k, p in enumerate(peers)]
        for cp in sends:
            cp.start()
        for k, p in enumerate(peers):
            pltpu.make_async_remote_copy(
                src_ref=v_ref, dst_ref=slot(p), send_sem=send_sems.at[k], recv_sem=recv_sems.at[k],
                device_id=p, device_id_type=MESH).wait_recv()
        for cp in sends:
            cp.wait_send()

    return pl.pallas_call(
        body, name=name,
        out_shape=jax.ShapeDtypeStruct((N_DEV, r, cols), v.dtype),
        in_specs=[pl.BlockSpec(memory_space=pltpu.VMEM)],
        out_specs=pl.BlockSpec(memory_space=pltpu.VMEM),
        scratch_shapes=[pltpu.SemaphoreType.DMA((7,)), pltpu.SemaphoreType.DMA((7,))],
    )(v)


def _join_halves(shards):
    n = len(shards)

    def body(*refs):
        in_refs, out_refs = refs[:n], refs[n:2 * n]
        send_sems, recv_sems = refs[2 * n:]
        x, y, c = _place()
        sends = [_remote(src.at[c], out.at[c], send_sems, recv_sems, i, (x, y, 1 - c))
                 for i, (src, out) in enumerate(zip(in_refs, out_refs))]
        recvs = [_remote(src.at[c], out.at[1 - c], send_sems, recv_sems, i, (x, y, 1 - c))
                 for i, (src, out) in enumerate(zip(in_refs, out_refs))]
        for cp in sends:
            cp.start()
        for cp in recvs:
            cp.wait_recv()
        for cp in sends:
            cp.wait_send()

    hbm = pl.BlockSpec(memory_space=pl.ANY)
    return pl.pallas_call(
        body, name="grad_join_halves",
        out_shape=[jax.ShapeDtypeStruct(t.shape, t.dtype) for t in shards],
        in_specs=[hbm] * n, out_specs=[hbm] * n, input_output_aliases={i: i for i in range(n)},
        scratch_shapes=[pltpu.SemaphoreType.DMA((n,)), pltpu.SemaphoreType.DMA((n,))],
    )(*shards)


def _row_tile(rows, row_bytes, limit=1 << 20):
    best = 8
    for t in range(8, rows + 1, 8):
        if rows % t == 0 and t * row_bytes <= limit:
            best = t
    return best


def _sum_pieces(g, land, place, name):
    _, _, rh, cols = g.shape
    tr = _row_tile(rh, (N_DEV - 1) * cols * 4, 4 << 20)

    def body(p_ref, g_ref, l_ref, out_ref):
        acc = g_ref[...]
        for k in range(N_DEV - 1):
            acc = acc + l_ref[k].astype(F32)
        out_ref[...] = acc

    return pl.pallas_call(
        body, name=name,
        grid_spec=pltpu.PrefetchScalarGridSpec(
            num_scalar_prefetch=1, grid=(rh // tr,),
            in_specs=[pl.BlockSpec((None, None, tr, cols), lambda r, p: (p[0], p[1], r, 0)),
                      pl.BlockSpec((N_DEV - 1, tr, cols), lambda r, p: (0, r, 0))],
            out_specs=pl.BlockSpec((None, tr, cols), lambda r, p: (p[1], r, 0))),
        out_shape=jax.ShapeDtypeStruct((2, rh, cols), g.dtype),
        compiler_params=_params(("arbitrary",)),
    )(place, g, land)


def _adamw_math(w, g, m, v):
    m = ADAM_B1 * m + (1.0 - ADAM_B1) * g
    v = ADAM_B2 * v + (1.0 - ADAM_B2) * (g * g)
    m_hat = m / (1.0 - ADAM_B1 ** ADAM_STEP)
    v_hat = v / (1.0 - ADAM_B2 ** ADAM_STEP)
    delta = -ADAM_LR * (m_hat / (jnp.sqrt(v_hat) + ADAM_EPS) + ADAM_WD * w)
    return delta, m, v


def _adamw(w, g, m, v, name):
    r, cols = w.shape
    tr = _row_tile(r, cols * 4)

    def body(w_ref, g_ref, m_ref, v_ref, d_ref, nm_ref, nv_ref):
        d_ref[...], nm_ref[...], nv_ref[...] = _adamw_math(w_ref[...], g_ref[...], m_ref[...], v_ref[...])

    blk = pl.BlockSpec((tr, cols), lambda i: (i, 0))
    shape = jax.ShapeDtypeStruct((r, cols), F32)
    return pl.pallas_call(
        body, name=name, grid=(r // tr,), in_specs=[blk] * 4, out_specs=[blk] * 3, out_shape=[shape] * 3,
        compiler_params=_params(("arbitrary",)),
    )(w, g, m, v)


def _sum_devices(gathered):
    _, r, cols = gathered.shape

    def body(a_ref, g_ref):
        g = a_ref[0]
        for k in range(1, N_DEV):
            g = g + a_ref[k]
        g_ref[...] = g

    return pl.pallas_call(body, name="sum_small_grads", out_shape=jax.ShapeDtypeStruct((r, cols), F32))(gathered)


def _pack_conv(cw):
    flat = cw.reshape(-1)
    return jnp.pad(flat, (0, ROWS_CONV * D_MODEL - flat.shape[0])).reshape(ROWS_CONV, D_MODEL)


def _unpack_conv(rows):
    return rows.reshape(-1)[:CONV_WIDTH * UP_W // N_CHIPS].reshape(CONV_WIDTH, UP_W // N_CHIPS)


def _columns_to_shards(w):
    r, n = w.shape
    return jnp.transpose(w.reshape(r, N_CHIPS, n // N_CHIPS), (1, 0, 2))


def _shards_to_columns(w):
    _, r, n = w.shape
    return jnp.transpose(w, (1, 0, 2)).reshape(r, N_CHIPS * n)


def _pack_small(g_mix_pre, g_mix_post, g_ffn_pre, g_ffn_post, sinks, conv_b, loss):
    pad_row = lambda v: jnp.pad(v.reshape(1, -1), ((0, 0), (0, D_MODEL - v.size)))
    cb = jnp.pad(conv_b.reshape(-1), (0, 6 * D_MODEL - UP_W)).reshape(6, D_MODEL)
    zeros2 = jnp.zeros((2, D_MODEL), F32)
    return jnp.concatenate([g_mix_pre.reshape(1, -1), g_mix_post.reshape(1, -1), g_ffn_pre.reshape(1, -1),
                            g_ffn_post.reshape(1, -1), pad_row(sinks), pad_row(loss), zeros2, cb, zeros2], axis=0)


def _unpack_small(p):
    return dict(mix_pre_norm=p[0:1], mix_post_norm=p[1:2], ffn_pre_norm=p[2:3], ffn_post_norm=p[3:4],
                attn_sinks=p[4:5, :N_ATTN_HEADS], loss=p[5, 0], conv_b=p[8:14].reshape(1, -1)[:, :UP_W],
                conv_w=_unpack_conv(p[SMALL_ROWS:SMALL_ROWS + ROWS_CONV]))


def _local_step(x, target, g_mix_pre, w_in, sinks, w_out, g_mix_post, g_ffn_pre, w_up, conv_w, conv_b, w_down, g_ffn_post,
                distributed=True, rope=None):
    s = x.shape[0]
    consts = _ret_constants()
    sin, cos = _rope_tables(s) if rope is None else rope

    by_half = lambda g, rows: g.reshape(N_CHIPS, 2, rows // (2 * N_CHIPS), g.shape[-1])

    if distributed:
        (h1, proj), level1 = _in_proj(x, g_mix_pre, w_in, _hosted_gather_level1([w_out, w_up, w_down]))
        (mix, states), (w_out, w_up, w_down) = _mixer_fwd(proj, sinks, sin, cos, consts, _hosted_gather_level2(level1))
        w_out, w_down = w_out.reshape(D_MODEL, D_MODEL), w_down.reshape(D_FF, D_MODEL)
        w_up = w_up.reshape(N_CHIPS, D_MODEL, UP_W // N_CHIPS)
    else:
        (h1, proj), _ = _in_proj(x, g_mix_pre, w_in)
        (mix, states), _ = _mixer_fwd(proj, sinks, sin, cos, consts)
    mixed, x1, h2, u0 = _out_up_proj(mix, x, w_out, g_mix_post, g_ffn_pre, w_up)
    y, dy2, dout, du, conv_acc, tail_acc = _ffn_tail(u0, x1, target, conv_w, conv_b, w_down, g_ffn_post)
    du0, dx1, dmixed, dmix, head_acc = _ffn_head_bwd(du, conv_w, w_up, x1, g_ffn_pre, dout, mixed, g_mix_post, w_out)

    d_w_down = _weight_grad(y, dy2, 512, "grad_w_down")
    d_w_up = _weight_grad(h2, du0, UP_W // N_CHIPS, "grad_w_up", by_block=True)
    d_w_out = _weight_grad(mix, dmixed, D_MODEL, "grad_w_out")
    early = [by_half(d_w_down, D_FF), by_half(d_w_up, N_CHIPS * D_MODEL), by_half(d_w_out, D_MODEL)]
    (dproj, dsinks), early_lands = _mixer_bwd(proj, dmix, states, sinks, sin, cos, consts,
                                              _hosted_scatter(early) if distributed else None)
    d_w_in_t = _weight_grad(dproj, h1, 512, "grad_w_in")
    late = [by_half(d_w_in_t, IN_W)]
    (grad_x, in_acc), late_lands = _in_proj_bwd(dproj, w_in, x, g_mix_pre, dx1, _hosted_scatter(late) if distributed else None)

    small = _pack_small(in_acc[0], head_acc[1], head_acc[0], tail_acc[0], dsinks[0, :N_ATTN_HEADS], conv_acc[3],
                        jnp.sum(tail_acc[1]))
    d_conv = jnp.pad(conv_acc[0:CONV_WIDTH].reshape(-1), (0, CONV_FULL_ROWS * D_MODEL - CONV_WIDTH * UP_W))
    small = jnp.concatenate([small, d_conv.reshape(CONV_FULL_ROWS, D_MODEL)], axis=0)
    grads = dict(w_down=early[0], w_up=early[1], w_out=early[2], w_in=late[0])
    lands = dict(zip(["w_down", "w_up", "w_out", "w_in"], early_lands + late_lands))
    return grad_x, grads, lands, small


def kernel(x, mix_pre_norm, w_in, attn_sinks, w_out, mix_post_norm, ffn_pre_norm, w_up, conv_w, conv_b, w_down, ffn_post_norm, loss_target, m_mix_pre_norm, m_w_in, m_attn_sinks, m_w_out, m_mix_post_norm, m_ffn_pre_norm, m_w_up, m_conv_w, m_conv_b, m_w_down, m_ffn_post_norm, v_mix_pre_norm, v_w_in, v_attn_sinks, v_w_out, v_mix_post_norm, v_ffn_pre_norm, v_w_up, v_conv_w, v_conv_b, v_w_down, v_ffn_post_norm):
    cx, cy, cc = _place()
    shard = 2 * cx + cy

    conv_rows = jnp.pad(conv_w[0], ((0, 16 - CONV_WIDTH), (0, 0)))
    w_in_t = jnp.swapaxes(w_in[0], 0, 1)
    (w_in_all, conv_all), sin, cos = _gather_now([w_in_t.astype(BF16), conv_rows], "gather_w_in", x.shape[1])
    conv_full = _shards_to_columns(conv_all[:, 0, :CONV_WIDTH])

    grad_x, grads, lands, small = _local_step(
        x[0], loss_target[0], mix_pre_norm, w_in_all.reshape(IN_W, D_MODEL), attn_sinks.reshape(-1), w_out[0].astype(BF16),
        mix_post_norm, ffn_pre_norm, w_up[0].astype(BF16), conv_full, conv_b, w_down[0].astype(BF16), ffn_post_norm,
        rope=(sin, cos))

    place = jnp.stack([shard, cc]).astype(jnp.int32)
    mats = ["w_in", "w_out", "w_up", "w_down"]
    halves = [_sum_pieces(grads[n], lands[n], place, "sum_grad_" + n) for n in mats]
    weights = dict(w_in=(w_in, m_w_in, v_w_in), w_out=(w_out, m_w_out, v_w_out), w_up=(w_up, m_w_up, v_w_up),
                   w_down=(w_down, m_w_down, v_w_down))
    mat_out = {}
    for n, joined in zip(mats, _join_halves(halves)):
        w, m, v = (t[0] for t in weights[n])
        if n == "w_in":
            w, m, v = (jnp.swapaxes(t, 0, 1) for t in (w, m, v))
        res = (joined.reshape(w.shape),) + tuple(_adamw(w, joined.reshape(w.shape), m, v, "adamw_" + n))
        mat_out[n] = tuple(jnp.swapaxes(t, 0, 1) for t in res) if n == "w_in" else res

    small_sum = _sum_devices(_gather_small(small, "gather_small_grads"))
    d_conv_full = small_sum[SMALL_ROWS:].reshape(-1)[:CONV_WIDTH * UP_W].reshape(CONV_WIDTH, UP_W)
    d_conv_mine = lax.dynamic_slice_in_dim(d_conv_full, shard * (UP_W // N_CHIPS), UP_W // N_CHIPS, axis=1)
    g_s = jnp.concatenate([small_sum[:SMALL_ROWS], _pack_conv(d_conv_mine)], axis=0)
    zero = jnp.zeros((), F32)
    pack_rep = lambda a, b, c_, d, e, f, cw: jnp.concatenate([_pack_small(a, b, c_, d, e, f, zero), _pack_conv(cw[0])], axis=0)
    w_s = pack_rep(mix_pre_norm, mix_post_norm, ffn_pre_norm, ffn_post_norm, attn_sinks, conv_b, conv_w)
    m_s = pack_rep(m_mix_pre_norm, m_mix_post_norm, m_ffn_pre_norm, m_ffn_post_norm, m_attn_sinks, m_conv_b, m_conv_w)
    v_s = pack_rep(v_mix_pre_norm, v_mix_post_norm, v_ffn_pre_norm, v_ffn_post_norm, v_attn_sinks, v_conv_b, v_conv_w)
    delta_s, new_m_s, new_v_s = _adamw(w_s, g_s, m_s, v_s, "adamw_small")

    names = ["mix_pre_norm", "w_in", "attn_sinks", "w_out", "mix_post_norm", "ffn_pre_norm", "w_up", "conv_w", "conv_b",
             "w_down", "ffn_post_norm"]

    def leaves(which, packed_small):
        smalls = _unpack_small(packed_small)
        return [mat_out[n][which][None] if n in mat_out else (smalls[n][None] if n == "conv_w" else smalls[n]) for n in names]

    loss = _unpack_small(g_s)["loss"]
    return (loss, grad_x[None], *leaves(0, g_s), *leaves(1, delta_s), *leaves(2, new_m_s), *leaves(3, new_v_s))
```

```python
import math

import jax
import jax.numpy as jnp
from jax import lax
from jax.experimental import pallas as pl
from jax.experimental.pallas import tpu as pltpu

F32 = jnp.float32
BF16 = jnp.bfloat16

D_MODEL = 1024
HEAD_DIM = 64
ATTN_W = 512
N_ATTN_HEADS = 8
KV_W = 128
RET_W = 512
N_RET_HEADS = 4
RET_HEAD_DIM = 128
CHUNK = 128
IN_W = 2816
D_FF = 2816
UP_W = 2 * D_FF
CONV_WIDTH = 3
RMS_EPS = 1e-6
GN_EPS = 1e-6
MASK_VALUE = -1e30
ATTN_SCALE = HEAD_DIM ** -0.5
RET_K_SCALE = RET_HEAD_DIM ** -0.5
GELU_C = math.sqrt(2.0 / math.pi)
GELU_A = 0.044715

ADAM_LR = 0.001
ADAM_B1 = 0.9
ADAM_B2 = 0.999
ADAM_EPS = 1e-08
ADAM_WD = 0.01
ADAM_STEP = 10

N_CHIPS = 4
N_DEV = 8
MESH = pl.DeviceIdType.MESH
VMEM_LIMIT_V7X = 56 * 1024 * 1024
TOKEN_TILE = 256
BIG_TOKEN_TILE = 512
IN_PROJ_TOKEN_TILE = 1024
WEIGHT_GRAD_TOKENS = 2048
FFN_ROW_BLOCK = 64
HEAD_BWD_COLS = 512
MIXER_CHUNKS_PER_STEP = 4
Q_A0, KV_A0, Q_R0, K_R0, V_R0, G_R0 = 0, 512, 768, 1280, 1792, 2304

ROWS_CONV = 8
SMALL_ROWS = 16
CONV_FULL_ROWS = 24


def _params(sem=None, **kw):
    if sem is not None:
        kw["dimension_semantics"] = sem
    return pltpu.CompilerParams(vmem_limit_bytes=VMEM_LIMIT_V7X, **kw)


def _resident(shape):
    zeros = (0,) * len(shape)
    return pl.BlockSpec(shape, lambda *_: zeros, pipeline_mode=pl.Buffered(1))


class _Hosted:
    def __init__(self, ins, outs, aliases, n_pairs, n_local, start, finish):
        self.ins, self.outs, self.aliases = list(ins), list(outs), dict(aliases)
        self.n_pairs, self.n_local, self.start, self.finish = n_pairs, max(n_local, 1), start, finish


def _hosted_call(compute, *, name, grid, in_specs, out_specs, out_shape, scratch_shapes, args, hosted=None):
    params = _params(("arbitrary",) * len(grid))
    if hosted is None:
        res = pl.pallas_call(compute, name=name, grid=grid, in_specs=in_specs, out_specs=out_specs, out_shape=out_shape,
                             scratch_shapes=scratch_shapes, compiler_params=params)(*args)
        return list(res), []
    n_in, n_out, n_scr = len(in_specs), len(out_specs), len(scratch_shapes)
    h_in, h_out = len(hosted.ins), len(hosted.outs)

    def at(step_of):
        cond = pl.program_id(0) == step_of(grid[0])
        for d in range(1, len(grid)):
            cond = jnp.logical_and(cond, pl.program_id(d) == step_of(grid[d]))
        return cond

    def body(*refs):
        ins, refs = refs[:n_in], refs[n_in:]
        h_ins, refs = refs[:h_in], refs[h_in:]
        outs, refs = refs[:n_out], refs[n_out:]
        h_outs, refs = refs[:h_out], refs[h_out:]
        scr, sems = refs[:n_scr], refs[n_scr:]

        @pl.when(at(lambda n: 0))
        def _():
            hosted.start(h_ins, h_outs, *sems)

        compute(*ins, *outs, *scr)

        @pl.when(at(lambda n: n - 1))
        def _():
            hosted.finish(h_ins, h_outs, *sems)

    hbm = pl.BlockSpec(memory_space=pl.ANY)
    res = pl.pallas_call(
        body, name=name, grid=grid,
        in_specs=list(in_specs) + [hbm] * h_in, out_specs=list(out_specs) + [hbm] * h_out,
        out_shape=list(out_shape) + hosted.outs,
        scratch_shapes=list(scratch_shapes) + [pltpu.SemaphoreType.DMA((hosted.n_pairs,)), pltpu.SemaphoreType.DMA((hosted.n_pairs,)),
                                               pltpu.SemaphoreType.DMA((hosted.n_local,))],
        input_output_aliases={n_in + a: n_out + b for a, b in hosted.aliases.items()},
        compiler_params=params,
    )(*args, *hosted.ins)
    return list(res[:n_out]), list(res[n_out:])


def _dot(a, b):
    return jnp.dot(a, b, preferred_element_type=F32)


def _dot_nt(a, b):
    return lax.dot_general(a, b, (((1,), (1,)), ((), ())), preferred_element_type=F32)


def _dot_tn(a, b):
    return lax.dot_general(a, b, (((0,), (0,)), ((), ())), preferred_element_type=F32)


def _shift_matrix(n, by):
    row = lax.broadcasted_iota(jnp.int32, (n, n), 0)
    col = lax.broadcasted_iota(jnp.int32, (n, n), 1)
    return jnp.where(col == row + by, 1.0, 0.0).astype(BF16)


def _rstd(v):
    return lax.rsqrt(jnp.mean(v * v, axis=-1, keepdims=True) + RMS_EPS)


def _rms_bwd(dy, v, rstd, gain):
    n = v * rstd
    dgain = jnp.sum(dy * n, axis=0, keepdims=True)
    dn = dy * gain
    dv = rstd * (dn - n * jnp.mean(dn * n, axis=-1, keepdims=True))
    return dv, dgain


def _lane_lo(shape):
    return (lax.broadcasted_iota(jnp.int32, shape, 1) % 128) < HEAD_DIM


GROUP = N_ATTN_HEADS // (KV_W // HEAD_DIM)


def _attn_bias(first_chunk):
    qi = lax.broadcasted_iota(jnp.int32, (GROUP * CHUNK, 2 * CHUNK), 0) % CHUNK
    kj = lax.broadcasted_iota(jnp.int32, (GROUP * CHUNK, 2 * CHUNK), 1)
    valid = jnp.logical_and(kj > qi, kj <= qi + CHUNK)
    if first_chunk:
        valid = jnp.logical_and(valid, kj >= CHUNK)
    return jnp.where(valid, 0.0, MASK_VALUE)


def _half(shape, hk):
    lo = _lane_lo(shape)
    return lo if hk == 0 else jnp.logical_not(lo)


class _GroupMasks:
    def __init__(self, sk_ref):
        groups = range(KV_W // HEAD_DIM)
        self.q = [_half((CHUNK, 128), hk) for hk in groups]
        self.kv = [_half((2 * CHUNK, 128), hk) for hk in groups]
        self.sinks = [_group_sinks(sk_ref, hk) for hk in groups]


def _stack_heads(ref, row0, col0, hk, half):
    parts = []
    for j in range(GROUP):
        h = GROUP * hk + j
        pair = ref[row0:row0 + CHUNK, col0 + (h // 2) * 128:col0 + (h // 2 + 1) * 128].astype(F32)
        if h % 2 != hk:
            pair = pltpu.roll(pair, HEAD_DIM, 1)
        parts.append(jnp.where(half, pair, 0.0))
    return jnp.concatenate(parts, axis=0)


def _unstack_heads(stacked, hk):
    pairs = []
    for q in range(GROUP // 2):
        even, odd = stacked[2 * q * CHUNK:(2 * q + 1) * CHUNK], stacked[(2 * q + 1) * CHUNK:(2 * q + 2) * CHUNK]
        pairs.append(even + pltpu.roll(odd, HEAD_DIM, 1) if hk == 0 else pltpu.roll(even, HEAD_DIM, 1) + odd)
    return pairs


def _group_sinks(sk_ref, hk):
    row = lax.broadcasted_iota(jnp.int32, (GROUP * CHUNK, 1), 0)
    col = jnp.full((GROUP * CHUNK, 1), sk_ref[GROUP * hk], F32)
    for j in range(1, GROUP):
        col = jnp.where(row >= j * CHUNK, sk_ref[GROUP * hk + j], col)
    return col


def _attn_probs(q_b, kk_b, bias, sink):
    s = _dot_nt(q_b, kk_b) * ATTN_SCALE + bias
    m = jnp.maximum(jnp.max(s, axis=-1, keepdims=True), sink)
    e = jnp.exp(s - m)
    e_sink = jnp.exp(sink - m)
    inv = 1.0 / (jnp.sum(e, axis=-1, keepdims=True) + e_sink)
    return e * inv, e_sink * inv


def _even_lanes(shape):
    return (lax.broadcasted_iota(jnp.int32, shape, 1) % 2) == 0


def _swap2(v, even):
    return jnp.where(even, pltpu.roll(v, v.shape[1] - 1, 1), pltpu.roll(v, 1, 1))


def _tile4(v):
    return jnp.concatenate([v, v, v, v], axis=-1)


def _sigmoid(v):
    return 1.0 / (1.0 + jnp.exp(-v))


def _ret_constants():
    h = N_RET_HEADS
    log_gamma = jnp.log(1.0 - jnp.power(2.0, -5.0 - jnp.arange(h, dtype=F32)))
    idx = jnp.arange(CHUNK, dtype=F32)
    rel = idx[:, None] - idx[None, :]
    d_intra = jnp.where(rel[None] >= 0, jnp.exp(log_gamma[:, None, None] * jnp.maximum(rel, 0.0)[None]), 0.0)
    xi = jnp.exp(log_gamma[None, :] * (idx[:, None] + 1.0))
    zeta = jnp.exp(log_gamma[None, :] * (CHUNK - 1.0 - idx[:, None]))
    decay = jnp.exp(log_gamma * CHUNK)
    xi_full = jnp.repeat(xi, RET_HEAD_DIM, axis=1)
    zeta_full = jnp.repeat(zeta, RET_HEAD_DIM, axis=1)
    decay_full = jnp.broadcast_to(jnp.repeat(decay, RET_HEAD_DIM)[None, :], (8, RET_W))
    return d_intra.astype(F32), xi_full.astype(F32), zeta_full.astype(F32), decay_full.astype(F32)


def _rope_tables(s):
    pos = jnp.arange(s, dtype=F32)
    angle = 1.0 / jnp.power(10000.0, jnp.linspace(0.0, 1.0, RET_HEAD_DIM // 2, dtype=F32))
    angle = jnp.repeat(angle, 2)
    sign = jnp.where(jnp.arange(RET_HEAD_DIM) % 2 == 0, -1.0, 1.0).astype(F32)
    return jnp.sin(pos[:, None] * angle[None]) * sign[None], jnp.cos(pos[:, None] * angle[None])


def _in_proj(x, gain, w_in_t, hosted=None):
    s = x.shape[0]
    tm = min(IN_PROJ_TOKEN_TILE, s)

    def body(x_ref, g_ref, w_ref, h_ref, p_ref):
        xv = x_ref[...]
        h = (xv * _rstd(xv) * g_ref[...]).astype(BF16)
        h_ref[...] = h
        p_ref[...] = _dot_nt(h, w_ref[...])

    return _hosted_call(
        body, name="in_proj", grid=(s // tm,),
        in_specs=[pl.BlockSpec((tm, D_MODEL), lambda i: (i, 0)), _resident((1, D_MODEL)), _resident((IN_W, D_MODEL))],
        out_specs=[pl.BlockSpec((tm, D_MODEL), lambda i: (i, 0)), pl.BlockSpec((tm, IN_W), lambda i: (i, 0))],
        out_shape=[jax.ShapeDtypeStruct((s, D_MODEL), BF16), jax.ShapeDtypeStruct((s, IN_W), F32)],
        scratch_shapes=[], args=(x, gain, w_in_t), hosted=hosted)


def _mixer_fwd(proj, sinks, sin, cos, consts, hosted=None):
    s = proj.shape[0]
    nc = s // CHUNK
    cps = MIXER_CHUNKS_PER_STEP
    d_intra, xi_full, zeta_full, decay_full = consts

    def body(sk_ref, p_ref, pkv_ref, sin_ref, cos_ref, dm_ref, xi_ref, ze_ref, dc_ref, mix_ref, st_ref, state):
        i = pl.program_id(0)

        @pl.when(i == 0)
        def _():
            state[...] = jnp.zeros_like(state)

        st = [state[h] for h in range(N_RET_HEADS)]
        bias_any = _attn_bias(False)
        bias_c0 = jnp.where(i == 0, _attn_bias(True), bias_any)
        even = _even_lanes((CHUNK, RET_W))
        masks = _GroupMasks(sk_ref)
        for c in range(cps):
            r0 = c * CHUNK
            rows = slice(r0, r0 + CHUNK)

            kv_cur = p_ref[rows, KV_A0:KV_A0 + 2 * KV_W]
            kv_prev = pkv_ref[...] if c == 0 else p_ref[r0 - CHUNK:r0, KV_A0:KV_A0 + 2 * KV_W]
            kk = jnp.concatenate([kv_prev[:, :KV_W], kv_cur[:, :KV_W]], axis=0)
            vv = jnp.concatenate([kv_prev[:, KV_W:], kv_cur[:, KV_W:]], axis=0)
            kk_b = kk.astype(BF16)
            bias = bias_c0 if c == 0 else bias_any
            for hk in range(KV_W // HEAD_DIM):
                q_b = _stack_heads(p_ref, r0, Q_A0, hk, masks.q[hk]).astype(BF16)
                p, _ = _attn_probs(q_b, kk_b, bias, masks.sinks[hk])
                v_b = jnp.where(masks.kv[hk], vv, 0.0).astype(BF16)
                for q, pair in enumerate(_unstack_heads(_dot(p.astype(BF16), v_b), hk)):
                    pi = (GROUP // 2) * hk + q
                    mix_ref[rows, pi * 128:(pi + 1) * 128] = pair.astype(BF16)

            sin4, cos4 = _tile4(sin_ref[rows, :]), _tile4(cos_ref[rows, :])
            q_r = p_ref[rows, Q_R0:Q_R0 + RET_W]
            k_r = p_ref[rows, K_R0:K_R0 + RET_W] * RET_K_SCALE
            q_r = q_r * cos4 + _swap2(q_r, even) * sin4
            k_r = k_r * cos4 + _swap2(k_r, even) * sin4
            kz = k_r * ze_ref[...]
            for h in range(N_RET_HEADS):
                sl = slice(h * RET_HEAD_DIM, (h + 1) * RET_HEAD_DIM)
                qh, kh = q_r[:, sl].astype(BF16), k_r[:, sl].astype(BF16)
                vh = p_ref[rows, V_R0 + h * RET_HEAD_DIM:V_R0 + (h + 1) * RET_HEAD_DIM].astype(BF16)
                st_ref[c, h] = st[h]
                a = _dot_nt(qh, kh) * dm_ref[h]
                qx = (q_r[:, sl] * xi_ref[:, sl]).astype(BF16)
                o = _dot(jnp.concatenate([a.astype(BF16), qx], axis=1), jnp.concatenate([vh, st[h].astype(BF16)], axis=0))
                st[h] = dc_ref[0:1, sl] * st[h] + _dot_tn(kz[:, sl].astype(BF16), vh)
                mu = jnp.mean(o, axis=-1, keepdims=True)
                oc = o - mu
                on = oc * lax.rsqrt(jnp.mean(oc * oc, axis=-1, keepdims=True) + GN_EPS)
                g = p_ref[rows, G_R0 + h * RET_HEAD_DIM:G_R0 + (h + 1) * RET_HEAD_DIM]
                mix_ref[rows, ATTN_W + h * RET_HEAD_DIM:ATTN_W + (h + 1) * RET_HEAD_DIM] = (g * _sigmoid(g) * on).astype(BF16)
        for h in range(N_RET_HEADS):
            state[h] = st[h]

    return _hosted_call(
        body, name="mixer_fwd", grid=(nc // cps,),
        in_specs=[
            pl.BlockSpec(memory_space=pltpu.SMEM),
            pl.BlockSpec((cps * CHUNK, IN_W), lambda i: (i, 0)),
            pl.BlockSpec((CHUNK, 2 * KV_W), lambda i: (jnp.maximum(cps * i - 1, 0), KV_A0 // (2 * KV_W))),
            pl.BlockSpec((cps * CHUNK, RET_HEAD_DIM), lambda i: (i, 0)),
            pl.BlockSpec((cps * CHUNK, RET_HEAD_DIM), lambda i: (i, 0)),
            _resident((N_RET_HEADS, CHUNK, CHUNK)), _resident((CHUNK, RET_W)), _resident((CHUNK, RET_W)), _resident((8, RET_W)),
        ],
        out_specs=[
            pl.BlockSpec((cps * CHUNK, D_MODEL), lambda i: (i, 0)),
            pl.BlockSpec((cps, N_RET_HEADS, RET_HEAD_DIM, RET_HEAD_DIM), lambda i: (i, 0, 0, 0)),
        ],
        out_shape=[jax.ShapeDtypeStruct((s, D_MODEL), BF16),
                   jax.ShapeDtypeStruct((nc, N_RET_HEADS, RET_HEAD_DIM, RET_HEAD_DIM), F32)],
        scratch_shapes=[pltpu.VMEM((N_RET_HEADS, RET_HEAD_DIM, RET_HEAD_DIM), F32)],
        args=(sinks, proj, proj, sin, cos, d_intra, xi_full, zeta_full, decay_full), hosted=hosted)


def _out_up_proj(mix, x, w_out, g_post, g_pre, w_up):
    s = x.shape[0]
    tm = min(BIG_TOKEN_TILE, s)
    blk = UP_W // N_CHIPS

    def body(mix_ref, x_ref, wo_ref, g2_ref, g3_ref, wu_ref, mixed_ref, x1_ref, h2_ref, u0_ref):
        mixed = _dot(mix_ref[...], wo_ref[...])
        mixed_ref[...] = mixed
        x1 = x_ref[...] + mixed * _rstd(mixed) * g2_ref[...]
        x1_ref[...] = x1
        h2 = (x1 * _rstd(x1) * g3_ref[...]).astype(BF16)
        h2_ref[...] = h2
        for k in range(N_CHIPS):
            u0_ref[:, k * blk:(k + 1) * blk] = _dot(h2, wu_ref[k]).astype(BF16)

    tok = lambda w: pl.BlockSpec((tm, w), lambda i: (i, 0))
    return pl.pallas_call(
        body, name="out_up_proj", grid=(s // tm,),
        in_specs=[tok(D_MODEL), tok(D_MODEL), _resident((D_MODEL, D_MODEL)), _resident((1, D_MODEL)), _resident((1, D_MODEL)),
                  _resident((N_CHIPS, D_MODEL, blk))],
        out_specs=[tok(D_MODEL), tok(D_MODEL), tok(D_MODEL), tok(UP_W)],
        out_shape=[jax.ShapeDtypeStruct((s, D_MODEL), F32), jax.ShapeDtypeStruct((s, D_MODEL), F32),
                   jax.ShapeDtypeStruct((s, D_MODEL), BF16), jax.ShapeDtypeStruct((s, UP_W), BF16)],
        compiler_params=_params(("arbitrary",)),
    )(mix, x, w_out, g_post, g_pre, w_up)


def _ffn_tail(u0, x1, target, conv_w, conv_b, w_down, g_post):
    s = x1.shape[0]
    tm = TOKEN_TILE
    last = s // tm - 1
    rb, lanes = FFN_ROW_BLOCK, 128

    def body(u0_ref, x1_ref, t_ref, cw_ref, cb_ref, wd_ref, g_ref,
             y_ref, dy2_ref, dout_ref, du_ref, cacc_ref, gacc_ref, u1_s, u2_s, carry, gelu_s, slope_s, dy_s, cacc):
        i = pl.program_id(0)

        @pl.when(i == 0)
        def _():
            carry[...] = jnp.zeros_like(carry)
            cacc[...] = jnp.zeros_like(cacc)
            gacc_ref[...] = jnp.zeros_like(gacc_ref)

        shift1, shift2 = _shift_matrix(tm, -1), _shift_matrix(tm, -2)
        r8 = lax.broadcasted_iota(jnp.int32, (8, 1), 0)
        wide = 2 * lanes

        def shift_block(col):
            cols = slice(col, col + wide)
            u1_s[:, cols] = _dot(shift1, u0_ref[:, cols])
            u2_s[:, cols] = _dot(shift2, u0_ref[:, cols])
            c14, c15 = carry[14:15, cols], carry[15:16, cols]
            u1_s[0:8, cols] = jnp.where(r8 == 0, c15, u1_s[0:8, cols])
            u2_s[0:8, cols] = jnp.where(r8 == 0, c14, jnp.where(r8 == 1, c15, u2_s[0:8, cols]))

        def taps(col):
            return (cw_ref[0:1, col:col + lanes], cw_ref[1:2, col:col + lanes], cw_ref[2:3, col:col + lanes],
                    cb_ref[0:1, col:col + lanes])

        def shifted(r0, col):
            return (u2_s[r0:r0 + rb, col:col + lanes], u1_s[r0:r0 + rb, col:col + lanes],
                    u0_ref[r0:r0 + rb, col:col + lanes].astype(F32))

        def conv(r0, col, w):
            u2, u1, uc = shifted(r0, col)
            return w[0] * u2 + w[1] * u1 + w[2] * uc + w[3]

        fold = lambda v: jnp.sum(v.reshape(rb // 8, 8, lanes), axis=0)

        shift_block(0)
        shift_block(D_FF)
        for j in range(D_FF // lanes):
            cg, cv = j * lanes, D_FF + j * lanes
            if cg % wide == 0 and cg + wide < D_FF:
                shift_block(cg + wide)
                shift_block(cv + wide)
            wg, wv = taps(cg), taps(cv)
            for r0 in range(0, tm, rb):
                gate, val = conv(r0, cg, wg), conv(r0, cv, wv)
                g2 = gate * gate
                th = jnp.tanh(gate * (GELU_C + GELU_C * GELU_A * g2))
                hp = 0.5 * th + 0.5
                gelu = gate * hp
                dgelu = hp + gate * (1.0 - th * th) * (0.5 * GELU_C + 1.5 * GELU_C * GELU_A * g2)
                y_ref[r0:r0 + rb, cg:cg + lanes] = (gelu * val).astype(BF16)
                gelu_s[r0:r0 + rb, cg:cg + lanes] = gelu
                slope_s[r0:r0 + rb, cg:cg + lanes] = dgelu * val

        y2 = _dot(y_ref[...], wd_ref[...])
        r4 = _rstd(y2)
        gain = g_ref[...]
        out = x1_ref[...] + y2 * r4 * gain
        diff = out - t_ref[...]
        dout = diff * (1.0 / D_MODEL)
        dout_ref[...] = dout
        dy2, dgain = _rms_bwd(dout, y2, r4, gain)
        dy2_b = dy2.astype(BF16)
        dy2_ref[...] = dy2_b
        gacc_ref[0:1, :] += dgain
        gacc_ref[1:2, :] += 0.5 * jnp.sum(diff * dout, axis=0, keepdims=True)
        carry[...] = u0_ref[tm - 16:tm, :].astype(F32)

        dy_s[:, 0:wide] = _dot_nt(dy2_b, wd_ref[0:wide, :])
        for j in range(D_FF // lanes):
            cg, cv = j * lanes, D_FF + j * lanes
            if cg % wide == 0 and cg + wide < D_FF:
                dy_s[:, cg + wide:cg + 2 * wide] = _dot_nt(dy2_b, wd_ref[cg + wide:cg + 2 * wide, :])
            acc = [[jnp.zeros((8, lanes), F32) for _ in range(CONV_WIDTH + 1)] for _ in range(2)]
            for r0 in range(0, tm, rb):
                dy = dy_s[r0:r0 + rb, cg:cg + lanes]
                d_gate = dy * slope_s[r0:r0 + rb, cg:cg + lanes]
                d_val = dy * gelu_s[r0:r0 + rb, cg:cg + lanes]
                for side, (col, d) in enumerate(((cg, d_gate), (cv, d_val))):
                    du_ref[r0:r0 + rb, col:col + lanes] = d.astype(BF16)
                    for k, u in enumerate(shifted(r0, col)):
                        acc[side][k] = acc[side][k] + fold(d * u)
                    acc[side][CONV_WIDTH] = acc[side][CONV_WIDTH] + fold(d)
            for side, col in enumerate((cg, cv)):
                for k in range(CONV_WIDTH + 1):
                    cacc[8 * k:8 * k + 8, col:col + lanes] += acc[side][k]

        @pl.when(i == last)
        def _():
            for k in range(CONV_WIDTH + 1):
                cacc_ref[k:k + 1, :] = jnp.sum(cacc[8 * k:8 * k + 8, :], axis=0, keepdims=True)
            cacc_ref[CONV_WIDTH + 1:8, :] = jnp.zeros((8 - CONV_WIDTH - 1, UP_W), F32)

    tok = lambda w: pl.BlockSpec((tm, w), lambda i: (i, 0))
    return pl.pallas_call(
        body, name="ffn_tail", grid=(s // tm,),
        in_specs=[tok(UP_W), tok(D_MODEL), tok(D_MODEL), _resident((CONV_WIDTH, UP_W)), _resident((1, UP_W)),
                  _resident((D_FF, D_MODEL)), _resident((1, D_MODEL))],
        out_specs=[tok(D_FF), tok(D_MODEL), tok(D_MODEL), tok(UP_W),
                   pl.BlockSpec((8, UP_W), lambda i: (0, 0)), pl.BlockSpec((8, D_MODEL), lambda i: (0, 0))],
        out_shape=[jax.ShapeDtypeStruct((s, D_FF), BF16), jax.ShapeDtypeStruct((s, D_MODEL), BF16),
                   jax.ShapeDtypeStruct((s, D_MODEL), F32), jax.ShapeDtypeStruct((s, UP_W), BF16),
                   jax.ShapeDtypeStruct((8, UP_W), F32), jax.ShapeDtypeStruct((8, D_MODEL), F32)],
        scratch_shapes=[pltpu.VMEM((tm, UP_W), F32), pltpu.VMEM((tm, UP_W), F32), pltpu.VMEM((16, UP_W), F32),
                        pltpu.VMEM((tm, D_FF), F32), pltpu.VMEM((tm, D_FF), F32),
                        pltpu.VMEM((tm, D_FF), F32), pltpu.VMEM((8 * (CONV_WIDTH + 1), UP_W), F32)],
        compiler_params=_params(("arbitrary",)),
    )(u0, x1, target, conv_w, conv_b, w_down, g_post)


def _ffn_head_bwd(du, conv_w, w_up, x1, g_pre, dout, mixed, g_post, w_out):
    s = x1.shape[0]
    tm = TOKEN_TILE
    nt = s // tm
    blk = UP_W // N_CHIPS

    def body(du_ref, halo_ref, cw_ref, wu_ref, x1_ref, g3_ref, dout_ref, mixed_ref, g2_ref, wo_ref,
             du0_ref, dx1_ref, dmixed_ref, dmix_ref, gacc_ref, dbuf):
        i = pl.program_id(0)

        @pl.when(i == 0)
        def _():
            gacc_ref[...] = jnp.zeros_like(gacc_ref)

        dbuf[0:tm, :] = du_ref[...].astype(F32)
        dbuf[tm:tm + 16, :] = jnp.where(i < nt - 1, halo_ref[...].astype(F32), 0.0)
        dh2 = jnp.zeros((tm, D_MODEL), F32)
        for k in range(N_CHIPS):
            for c0 in range(0, blk, HEAD_BWD_COLS):
                width = min(HEAD_BWD_COLS, blk - c0)
                cols = slice(k * blk + c0, k * blk + c0 + width)
                du0_b = (cw_ref[2:3, cols] * dbuf[0:tm, cols] + cw_ref[1:2, cols] * dbuf[1:1 + tm, cols]
                         + cw_ref[0:1, cols] * dbuf[2:2 + tm, cols]).astype(BF16)
                du0_ref[:, cols] = du0_b
                dh2 = dh2 + _dot_nt(du0_b, wu_ref[k, :, c0:c0 + width])
        x1 = x1_ref[...]
        d3, dg3 = _rms_bwd(dh2, x1, _rstd(x1), g3_ref[...])
        dx1 = dout_ref[...] + d3
        dx1_ref[...] = dx1
        mixed = mixed_ref[...]
        dmixed, dg2 = _rms_bwd(dx1, mixed, _rstd(mixed), g2_ref[...])
        dmixed_b = dmixed.astype(BF16)
        dmixed_ref[...] = dmixed_b
        dmix_ref[...] = _dot_nt(dmixed_b, wo_ref[...]).astype(BF16)
        gacc_ref[0:1, :] += dg3
        gacc_ref[1:2, :] += dg2

    tok = lambda w: pl.BlockSpec((tm, w), lambda i: (i, 0))
    halo = pl.BlockSpec((16, UP_W), lambda i: (jnp.minimum(i + 1, nt - 1) * (tm // 16), 0))
    return pl.pallas_call(
        body, name="ffn_head_bwd", grid=(nt,),
        in_specs=[tok(UP_W), halo, _resident((CONV_WIDTH, UP_W)), _resident((N_CHIPS, D_MODEL, blk)), tok(D_MODEL),
                  _resident((1, D_MODEL)), tok(D_MODEL), tok(D_MODEL), _resident((1, D_MODEL)), _resident((D_MODEL, D_MODEL))],
        out_specs=[tok(UP_W), tok(D_MODEL), tok(D_MODEL), tok(D_MODEL), pl.BlockSpec((8, D_MODEL), lambda i: (0, 0))],
        out_shape=[jax.ShapeDtypeStruct((s, UP_W), BF16), jax.ShapeDtypeStruct((s, D_MODEL), F32),
                   jax.ShapeDtypeStruct((s, D_MODEL), BF16), jax.ShapeDtypeStruct((s, D_MODEL), BF16),
                   jax.ShapeDtypeStruct((8, D_MODEL), F32)],
        scratch_shapes=[pltpu.VMEM((tm + 16, UP_W), F32)],
        compiler_params=_params(("arbitrary",)),
    )(du, du, conv_w, w_up, x1, g_pre, dout, mixed, g_post, w_out)


def _mixer_bwd(proj, dmix, states, sinks, sin, cos, consts, hosted=None):
    s = proj.shape[0]
    nc = s // CHUNK
    cps = MIXER_CHUNKS_PER_STEP
    nb = nc // cps
    d_intra, xi_full, zeta_full, decay_full = consts

    def body(sk_ref, p_ref, pkv_ref, dmix_ref, st_ref, sin_ref, cos_ref, dm_ref, xi_ref, ze_ref, dc_ref,
             dp_ref, dsk_ref, gstate, ckv, dsk_acc):
        i = pl.program_id(0)
        block = nb - 1 - i

        @pl.when(i == 0)
        def _():
            gstate[...] = jnp.zeros_like(gstate)
            ckv[...] = jnp.zeros_like(ckv)
            dsk_acc[...] = jnp.zeros_like(dsk_acc)

        gs_all = [gstate[h] for h in range(N_RET_HEADS)]
        later_kv = ckv[...]
        lane = lax.broadcasted_iota(jnp.int32, (CHUNK, 128), 1)
        dsk = jnp.zeros((CHUNK, 128), F32)
        bias_any = _attn_bias(False)
        bias_c0 = jnp.where(block == 0, _attn_bias(True), bias_any)
        even = _even_lanes((CHUNK, RET_W))
        masks = _GroupMasks(sk_ref)
        for c in reversed(range(cps)):
            r0 = c * CHUNK
            rows = slice(r0, r0 + CHUNK)

            kv_cur = p_ref[rows, KV_A0:KV_A0 + 2 * KV_W]
            kv_prev = pkv_ref[...] if c == 0 else p_ref[r0 - CHUNK:r0, KV_A0:KV_A0 + 2 * KV_W]
            kk = jnp.concatenate([kv_prev[:, :KV_W], kv_cur[:, :KV_W]], axis=0)
            vv = jnp.concatenate([kv_prev[:, KV_W:], kv_cur[:, KV_W:]], axis=0)
            kk_b, vv_b = kk.astype(BF16), vv.astype(BF16)
            bias = bias_c0 if c == 0 else bias_any
            dkk = jnp.zeros((2 * CHUNK, KV_W), F32)
            dvv = jnp.zeros((2 * CHUNK, KV_W), F32)
            for hk in range(KV_W // HEAD_DIM):
                q_b = _stack_heads(p_ref, r0, Q_A0, hk, masks.q[hk]).astype(BF16)
                do_b = _stack_heads(dmix_ref, r0, 0, hk, masks.q[hk]).astype(BF16)
                p, p_sink = _attn_probs(q_b, kk_b, bias, masks.sinks[hk])
                dpr = _dot_nt(do_b, vv_b)
                delta = jnp.sum(p * dpr, axis=-1, keepdims=True)
                ds_b = (p * (dpr - delta) * ATTN_SCALE).astype(BF16)
                dsink = -p_sink * delta
                for j in range(GROUP):
                    dsk = dsk + jnp.where(lane == GROUP * hk + j, dsink[j * CHUNK:(j + 1) * CHUNK], 0.0)
                k_b = jnp.where(masks.kv[hk], kk, 0.0).astype(BF16)
                for q, pair in enumerate(_unstack_heads(_dot(ds_b, k_b), hk)):
                    pi = (GROUP // 2) * hk + q
                    dp_ref[rows, Q_A0 + pi * 128:Q_A0 + (pi + 1) * 128] = pair.astype(BF16)
                dkk = dkk + _dot_tn(ds_b, q_b)
                dvv = dvv + _dot_tn(p.astype(BF16), do_b)
            dp_ref[rows, KV_A0:KV_A0 + KV_W] = (dkk[CHUNK:] + later_kv[:, :KV_W]).astype(BF16)
            dp_ref[rows, KV_A0 + KV_W:KV_A0 + 2 * KV_W] = (dvv[CHUNK:] + later_kv[:, KV_W:]).astype(BF16)
            later_kv = jnp.concatenate([dkk[:CHUNK], dvv[:CHUNK]], axis=1)

            sin4, cos4 = _tile4(sin_ref[rows, :]), _tile4(cos_ref[rows, :])
            q_r = p_ref[rows, Q_R0:Q_R0 + RET_W]
            k_r = p_ref[rows, K_R0:K_R0 + RET_W] * RET_K_SCALE
            q_r = q_r * cos4 + _swap2(q_r, even) * sin4
            k_r = k_r * cos4 + _swap2(k_r, even) * sin4
            kz = k_r * ze_ref[...]
            dq_parts, dk_parts = [], []
            for h in range(N_RET_HEADS):
                sl = slice(h * RET_HEAD_DIM, (h + 1) * RET_HEAD_DIM)
                qh, kh = q_r[:, sl].astype(BF16), k_r[:, sl].astype(BF16)
                vh = p_ref[rows, V_R0 + h * RET_HEAD_DIM:V_R0 + (h + 1) * RET_HEAD_DIM].astype(BF16)
                st_b = st_ref[c, h].astype(BF16)
                gs = gs_all[h]
                gs_b = gs.astype(BF16)
                xi_h = xi_ref[:, sl]
                dm = dm_ref[h]
                a_b = (_dot_nt(qh, kh) * dm).astype(BF16)
                qx = (q_r[:, sl] * xi_h).astype(BF16)
                o = _dot(jnp.concatenate([a_b, qx], axis=1), jnp.concatenate([vh, st_b], axis=0))
                mu = jnp.mean(o, axis=-1, keepdims=True)
                oc = o - mu
                rs = lax.rsqrt(jnp.mean(oc * oc, axis=-1, keepdims=True) + GN_EPS)
                on = oc * rs
                g = p_ref[rows, G_R0 + h * RET_HEAD_DIM:G_R0 + (h + 1) * RET_HEAD_DIM]
                sg = _sigmoid(g)
                dr = dmix_ref[rows, ATTN_W + h * RET_HEAD_DIM:ATTN_W + (h + 1) * RET_HEAD_DIM].astype(F32)
                dp_ref[rows, G_R0 + h * RET_HEAD_DIM:G_R0 + (h + 1) * RET_HEAD_DIM] = (
                    dr * on * (sg * (1.0 + g * (1.0 - sg)))).astype(BF16)
                don = dr * g * sg
                do = rs * (don - jnp.mean(don, axis=-1, keepdims=True) - on * jnp.mean(don * on, axis=-1, keepdims=True))
                do_b = do.astype(BF16)
                dox_b = (do * xi_h).astype(BF16)
                da_b = (_dot_nt(do_b, vh) * dm).astype(BF16)
                dq_parts.append(_dot(da_b, kh) + _dot_nt(dox_b, st_b))
                dk_parts.append(_dot_tn(da_b, qh) + ze_ref[:, sl] * _dot_nt(vh, gs_b))
                dv = _dot_tn(a_b, do_b) + _dot(kz[:, sl].astype(BF16), gs_b)
                dp_ref[rows, V_R0 + h * RET_HEAD_DIM:V_R0 + (h + 1) * RET_HEAD_DIM] = dv.astype(BF16)
                gs_all[h] = dc_ref[0:1, sl] * gs + _dot_tn(qh, dox_b)
            dq = jnp.concatenate(dq_parts, axis=-1)
            dk = jnp.concatenate(dk_parts, axis=-1)
            dp_ref[rows, Q_R0:Q_R0 + RET_W] = (dq * cos4 - _swap2(dq, even) * sin4).astype(BF16)
            dp_ref[rows, K_R0:K_R0 + RET_W] = (RET_K_SCALE * (dk * cos4 - _swap2(dk, even) * sin4)).astype(BF16)

        for h in range(N_RET_HEADS):
            gstate[h] = gs_all[h]
        ckv[...] = later_kv
        dsk_acc[...] += dsk

        @pl.when(i == nb - 1)
        def _():
            dsk_ref[...] = jnp.sum(dsk_acc[...], axis=0, keepdims=True)

    rev = lambda i: nb - 1 - i
    return _hosted_call(
        body, name="mixer_bwd", grid=(nb,),
        in_specs=[
            pl.BlockSpec(memory_space=pltpu.SMEM),
            pl.BlockSpec((cps * CHUNK, IN_W), lambda i: (rev(i), 0)),
            pl.BlockSpec((CHUNK, 2 * KV_W), lambda i: (jnp.maximum(cps * rev(i) - 1, 0), KV_A0 // (2 * KV_W))),
            pl.BlockSpec((cps * CHUNK, D_MODEL), lambda i: (rev(i), 0)),
            pl.BlockSpec((cps, N_RET_HEADS, RET_HEAD_DIM, RET_HEAD_DIM), lambda i: (rev(i), 0, 0, 0)),
            pl.BlockSpec((cps * CHUNK, RET_HEAD_DIM), lambda i: (rev(i), 0)),
            pl.BlockSpec((cps * CHUNK, RET_HEAD_DIM), lambda i: (rev(i), 0)),
            _resident((N_RET_HEADS, CHUNK, CHUNK)), _resident((CHUNK, RET_W)), _resident((CHUNK, RET_W)), _resident((8, RET_W)),
        ],
        out_specs=[pl.BlockSpec((cps * CHUNK, IN_W), lambda i: (rev(i), 0)), pl.BlockSpec((1, 128), lambda i: (0, 0))],
        out_shape=[jax.ShapeDtypeStruct((s, IN_W), BF16), jax.ShapeDtypeStruct((1, 128), F32)],
        scratch_shapes=[pltpu.VMEM((N_RET_HEADS, RET_HEAD_DIM, RET_HEAD_DIM), F32), pltpu.VMEM((CHUNK, 2 * KV_W), F32),
                        pltpu.VMEM((CHUNK, 128), F32)],
        args=(sinks, proj, proj, dmix, states, sin, cos, d_intra, xi_full, zeta_full, decay_full), hosted=hosted)


def _in_proj_bwd(dproj, w_in_t, x, gain, dx1, hosted=None):
    s = x.shape[0]
    tm = min(BIG_TOKEN_TILE, s)

    def body(dp_ref, w_ref, x_ref, g_ref, dx1_ref, dx_ref, gacc_ref):
        @pl.when(pl.program_id(0) == 0)
        def _():
            gacc_ref[...] = jnp.zeros_like(gacc_ref)

        dh = _dot(dp_ref[...], w_ref[...])
        xv = x_ref[...]
        d1, dg = _rms_bwd(dh, xv, _rstd(xv), g_ref[...])
        dx_ref[...] = dx1_ref[...] + d1
        gacc_ref[0:1, :] += dg

    tok = lambda w: pl.BlockSpec((tm, w), lambda i: (i, 0))
    return _hosted_call(
        body, name="in_proj_bwd", grid=(s // tm,),
        in_specs=[tok(IN_W), _resident((IN_W, D_MODEL)), tok(D_MODEL), _resident((1, D_MODEL)), tok(D_MODEL)],
        out_specs=[tok(D_MODEL), pl.BlockSpec((8, D_MODEL), lambda i: (0, 0))],
        out_shape=[jax.ShapeDtypeStruct((s, D_MODEL), F32), jax.ShapeDtypeStruct((8, D_MODEL), F32)],
        scratch_shapes=[], args=(dproj, w_in_t, x, gain, dx1), hosted=hosted)


def _weight_grad(a, b, tn, name, by_block=False, hosted=None):
    s, m = a.shape
    n = b.shape[1]
    tk = min(WEIGHT_GRAD_TOKENS if m <= D_MODEL else WEIGHT_GRAD_TOKENS // 2, s)

    def body(a_ref, b_ref, o_ref):
        @pl.when(pl.program_id(1) == 0)
        def _():
            o_ref[...] = jnp.zeros_like(o_ref)

        o_ref[...] += _dot_tn(a_ref[...], b_ref[...])

    if by_block:
        out_spec = pl.BlockSpec((None, m, tn), lambda j, k: (j, 0, 0))
        out_shape = jax.ShapeDtypeStruct((n // tn, m, tn), F32)
    else:
        out_spec = pl.BlockSpec((m, tn), lambda j, k: (0, j))
        out_shape = jax.ShapeDtypeStruct((m, n), F32)
    (out,), lands = _hosted_call(
        body, name=name, grid=(n // tn, s // tk),
        in_specs=[pl.BlockSpec((tk, m), lambda j, k: (k, 0)), pl.BlockSpec((tk, tn), lambda j, k: (k, j))],
        out_specs=[out_spec], out_shape=[out_shape], scratch_shapes=[], args=(a, b), hosted=hosted)
    return out if hosted is None else (out, lands)


def _place():
    return lax.axis_index("x"), lax.axis_index("y"), lax.axis_index("c")


def _remote(src, dst, send_sems, recv_sems, k, to):
    return pltpu.make_async_remote_copy(src_ref=src, dst_ref=dst, send_sem=send_sems.at[k], recv_sem=recv_sems.at[k],
                                        device_id=to, device_id_type=MESH)


def _gather_level1_copies(w_refs, out_refs, send_sems, recv_sems, local_sems):
    x, y, c = _place()
    mine_at = 2 * x + y
    peers = [(x, y, 1 - c), (1 - x, y, c), (x, 1 - y, c), (1 - x, 1 - y, c)]
    local, sends, recvs = [], [], []
    for i, (w, out) in enumerate(zip(w_refs, out_refs)):
        half = w.shape[0] // 2
        src = w.at[pl.ds(pl.multiple_of(c * half, 16 if half % 16 == 0 else 8), half), :]
        mine = out.at[mine_at, c]
        local.append(pltpu.make_async_copy(src, mine, local_sems.at[i]))
        for k, p in enumerate(peers):
            sends.append(_remote(src, mine, send_sems, recv_sems, 4 * i + k, p))
            lands = out.at[mine_at, 1 - c] if k == 0 else out.at[2 * p[0] + p[1], c]
            recvs.append(_remote(src, lands, send_sems, recv_sems, 4 * i + k, p))
    return local, sends, recvs


def _gather_level1_start(w_refs, out_refs, send_sems, recv_sems, local_sems):
    local, sends, _ = _gather_level1_copies(w_refs, out_refs, send_sems, recv_sems, local_sems)
    for cp in local + sends:
        cp.start()


def _gather_level1_finish(w_refs, out_refs, send_sems, recv_sems, local_sems):
    local, sends, recvs = _gather_level1_copies(w_refs, out_refs, send_sems, recv_sems, local_sems)
    for cp in recvs:
        cp.wait_recv()
    for cp in sends:
        cp.wait_send()
    for cp in local:
        cp.wait()


def _gather_level2_copies(in_refs, out_refs, send_sems, recv_sems, local_sems):
    x, y, c = _place()
    chips = [(1 - x, y), (x, 1 - y), (1 - x, 1 - y)]
    sends, recvs = [], []
    for i, (src, out) in enumerate(zip(in_refs, out_refs)):
        for j, (px, py) in enumerate(chips):
            sends.append(_remote(src.at[2 * px + py, c], out.at[2 * px + py, c], send_sems, recv_sems, 3 * i + j, (x, y, 1 - c)))
            recvs.append(_remote(src.at[2 * px + py, c], out.at[2 * px + py, 1 - c], send_sems, recv_sems, 3 * i + j,
                                 (x, y, 1 - c)))
    return sends, recvs


def _gather_level2_start(in_refs, out_refs, send_sems, recv_sems, local_sems):
    for cp in _gather_level2_copies(in_refs, out_refs, send_sems, recv_sems, local_sems)[0]:
        cp.start()


def _gather_level2_finish(in_refs, out_refs, send_sems, recv_sems, local_sems):
    sends, recvs = _gather_level2_copies(in_refs, out_refs, send_sems, recv_sems, local_sems)
    for cp in recvs:
        cp.wait_recv()
    for cp in sends:
        cp.wait_send()


def _gathered_shape(w):
    r, cols = w.shape
    return jax.ShapeDtypeStruct((N_CHIPS, 2, r // 2, cols), w.dtype)


def _hosted_gather_level1(shards):
    n = len(shards)
    return _Hosted(shards, [_gathered_shape(w) for w in shards], {}, 4 * n, n, _gather_level1_start, _gather_level1_finish)


def _hosted_gather_level2(gathered):
    n = len(gathered)
    return _Hosted(gathered, [jax.ShapeDtypeStruct(g.shape, g.dtype) for g in gathered], {i: i for i in range(n)}, 3 * n, 0,
                   _gather_level2_start, _gather_level2_finish)


def _gather_now(shards, name, seq_len):
    n = len(shards)
    rows = min(512, seq_len)
    angle = 1.0 / jnp.power(10000.0, jnp.linspace(0.0, 1.0, RET_HEAD_DIM // 2, dtype=F32))
    sign = jnp.where(jnp.arange(RET_HEAD_DIM) % 2 == 0, -1.0, 1.0).astype(F32)
    angle_sign = jnp.concatenate([jnp.repeat(angle, 2)[None], sign[None], jnp.zeros((6, RET_HEAD_DIM), F32)], axis=0)

    def body(*refs):
        w_refs, as_ref, out_refs = list(refs[:n]), refs[n], list(refs[n + 1:2 * n + 1])
        sin_ref, cos_ref, send1, recv1, local1, send2, recv2 = refs[2 * n + 1:]
        _gather_level1_start(w_refs, out_refs, send1, recv1, local1)

        def fill(i, carry):
            r0 = pl.multiple_of(i * rows, rows)
            pos = (lax.broadcasted_iota(jnp.int32, (rows, RET_HEAD_DIM), 0) + i * rows).astype(F32)
            arg = pos * as_ref[0:1, :]
            sin_ref[pl.ds(r0, rows), :] = jnp.sin(arg) * as_ref[1:2, :]
            cos_ref[pl.ds(r0, rows), :] = jnp.cos(arg)
            return carry

        lax.fori_loop(0, seq_len // rows, fill, 0)
        _gather_level1_finish(w_refs, out_refs, send1, recv1, local1)
        _gather_level2_start(out_refs, out_refs, send2, recv2, None)
        _gather_level2_finish(out_refs, out_refs, send2, recv2, None)

    hbm, vmem = pl.BlockSpec(memory_space=pl.ANY), pl.BlockSpec(memory_space=pltpu.VMEM)
    table = jax.ShapeDtypeStruct((seq_len, RET_HEAD_DIM), F32)
    res = pl.pallas_call(
        body, name=name, out_shape=[_gathered_shape(w) for w in shards] + [table, table],
        in_specs=[hbm] * n + [vmem], out_specs=[hbm] * n + [vmem, vmem],
        scratch_shapes=[pltpu.SemaphoreType.DMA((4 * n,)), pltpu.SemaphoreType.DMA((4 * n,)), pltpu.SemaphoreType.DMA((n,)),
                        pltpu.SemaphoreType.DMA((3 * n,)), pltpu.SemaphoreType.DMA((3 * n,))],
        compiler_params=_params(),
    )(*shards, angle_sign)
    return res[:n], res[n], res[n + 1]


def _scatter_copies(g_refs, land_refs, send_sems, recv_sems, local_sems):
    x, y, c = _place()
    copies = []
    for i, (g, land) in enumerate(zip(g_refs, land_refs)):
        for k, (px, py, pc) in enumerate(_relations(x, y, c)):
            copies.append(_remote(g.at[2 * px + py, pc], land.at[k], send_sems, recv_sems, 7 * i + k, (px, py, pc)))
    return copies


def _scatter_start(g_refs, land_refs, send_sems, recv_sems, local_sems):
    for cp in _scatter_copies(g_refs, land_refs, send_sems, recv_sems, local_sems):
        cp.start()


def _scatter_finish(g_refs, land_refs, send_sems, recv_sems, local_sems):
    for cp in _scatter_copies(g_refs, land_refs, send_sems, recv_sems, local_sems):
        cp.wait()


def _hosted_scatter(grads):
    lands = [jax.ShapeDtypeStruct((N_DEV - 1,) + g.shape[2:], g.dtype) for g in grads]
    return _Hosted(grads, lands, {}, 7 * len(grads), 0, _scatter_start, _scatter_finish)


def _relations(x, y, c):
    rel = []
    for fx in (0, 1):
        for fy in (0, 1):
            for fc in (0, 1):
                if fx or fy or fc:
                    rel.append(((1 - x) if fx else x, (1 - y) if fy else y, (1 - c) if fc else c))
    return rel


def _join_halves(shards, small):
    n = len(shards)

    def body(*refs):
        in_refs, small_ref, out_refs, all_ref = refs[:n], refs[n], refs[n + 1:2 * n + 1], refs[2 * n + 1]
        send_sems, recv_sems = refs[2 * n + 2:]
        x, y, c = _place()
        slot = lambda p: all_ref.at[4 * p[0] + 2 * p[1] + p[2]]
        all_ref[4 * x + 2 * y + c] = small_ref[...]
        sends = [_remote(src.at[c], out.at[c], send_sems, recv_sems, i, (x, y, 1 - c))
                 for i, (src, out) in enumerate(zip(in_refs, out_refs))]
        recvs = [_remote(src.at[c], out.at[1 - c], send_sems, recv_sems, i, (x, y, 1 - c))
                 for i, (src, out) in enumerate(zip(in_refs, out_refs))]
        for k, p in enumerate(_relations(x, y, c)):
            sends.append(_remote(small_ref, slot((x, y, c)), send_sems, recv_sems, n + k, p))
            recvs.append(_remote(small_ref, slot(p), send_sems, recv_sems, n + k, p))
        for cp in sends:
            cp.start()
        for cp in recvs:
            cp.wait_recv()
        for cp in sends:
            cp.wait_send()

    hbm, vmem = pl.BlockSpec(memory_space=pl.ANY), pl.BlockSpec(memory_space=pltpu.VMEM)
    pairs = n + N_DEV - 1
    res = pl.pallas_call(
        body, name="grad_join_halves",
        out_shape=[jax.ShapeDtypeStruct(t.shape, t.dtype) for t in shards] + [jax.ShapeDtypeStruct((N_DEV,) + small.shape, F32)],
        in_specs=[hbm] * n + [vmem], out_specs=[hbm] * n + [vmem], input_output_aliases={i: i for i in range(n)},
        scratch_shapes=[pltpu.SemaphoreType.DMA((pairs,)), pltpu.SemaphoreType.DMA((pairs,))],
    )(*shards, small)
    return res[:n], res[n]


def _row_tile(rows, row_bytes, limit=1 << 20):
    best = 8
    for t in range(8, rows + 1, 8):
        if rows % t == 0 and t * row_bytes <= limit:
            best = t
    return best


def _sum_pieces(g, land, place, name):
    _, _, rh, cols = g.shape
    tr = _row_tile(rh, (N_DEV - 1) * cols * 4, 4 << 20)

    def body(p_ref, g_ref, l_ref, out_ref):
        acc = g_ref[...]
        for k in range(N_DEV - 1):
            acc = acc + l_ref[k].astype(F32)
        out_ref[...] = acc

    return pl.pallas_call(
        body, name=name,
        grid_spec=pltpu.PrefetchScalarGridSpec(
            num_scalar_prefetch=1, grid=(rh // tr,),
            in_specs=[pl.BlockSpec((None, None, tr, cols), lambda r, p: (p[0], p[1], r, 0)),
                      pl.BlockSpec((N_DEV - 1, tr, cols), lambda r, p: (0, r, 0))],
            out_specs=pl.BlockSpec((None, tr, cols), lambda r, p: (p[1], r, 0))),
        out_shape=jax.ShapeDtypeStruct((2, rh, cols), g.dtype),
        compiler_params=_params(("arbitrary",)),
    )(place, g, land)


def _adamw_math(w, g, m, v):
    m = ADAM_B1 * m + (1.0 - ADAM_B1) * g
    v = ADAM_B2 * v + (1.0 - ADAM_B2) * (g * g)
    m_hat = m / (1.0 - ADAM_B1 ** ADAM_STEP)
    v_hat = v / (1.0 - ADAM_B2 ** ADAM_STEP)
    delta = -ADAM_LR * (m_hat / (jnp.sqrt(v_hat) + ADAM_EPS) + ADAM_WD * w)
    return delta, m, v


def _adamw(w, g, m, v, name):
    r, cols = w.shape
    tr = _row_tile(r, cols * 4)

    def body(w_ref, g_ref, m_ref, v_ref, d_ref, nm_ref, nv_ref):
        d_ref[...], nm_ref[...], nv_ref[...] = _adamw_math(w_ref[...], g_ref[...], m_ref[...], v_ref[...])

    blk = pl.BlockSpec((tr, cols), lambda i: (i, 0))
    shape = jax.ShapeDtypeStruct((r, cols), F32)
    return pl.pallas_call(
        body, name=name, grid=(r // tr,), in_specs=[blk] * 4, out_specs=[blk] * 3, out_shape=[shape] * 3,
        compiler_params=_params(("arbitrary",)),
    )(w, g, m, v)


def _sum_devices(gathered):
    _, r, cols = gathered.shape

    def body(a_ref, g_ref):
        g = a_ref[0]
        for k in range(1, N_DEV):
            g = g + a_ref[k]
        g_ref[...] = g

    return pl.pallas_call(body, name="sum_small_grads", out_shape=jax.ShapeDtypeStruct((r, cols), F32))(gathered)


def _pack_conv(cw):
    flat = cw.reshape(-1)
    return jnp.pad(flat, (0, ROWS_CONV * D_MODEL - flat.shape[0])).reshape(ROWS_CONV, D_MODEL)


def _unpack_conv(rows):
    return rows.reshape(-1)[:CONV_WIDTH * UP_W // N_CHIPS].reshape(CONV_WIDTH, UP_W // N_CHIPS)


def _columns_to_shards(w):
    r, n = w.shape
    return jnp.transpose(w.reshape(r, N_CHIPS, n // N_CHIPS), (1, 0, 2))


def _shards_to_columns(w):
    _, r, n = w.shape
    return jnp.transpose(w, (1, 0, 2)).reshape(r, N_CHIPS * n)


def _pack_small(g_mix_pre, g_mix_post, g_ffn_pre, g_ffn_post, sinks, conv_b, loss):
    pad_row = lambda v: jnp.pad(v.reshape(1, -1), ((0, 0), (0, D_MODEL - v.size)))
    cb = jnp.pad(conv_b.reshape(-1), (0, 6 * D_MODEL - UP_W)).reshape(6, D_MODEL)
    zeros2 = jnp.zeros((2, D_MODEL), F32)
    return jnp.concatenate([g_mix_pre.reshape(1, -1), g_mix_post.reshape(1, -1), g_ffn_pre.reshape(1, -1),
                            g_ffn_post.reshape(1, -1), pad_row(sinks), pad_row(loss), zeros2, cb, zeros2], axis=0)


def _unpack_small(p):
    return dict(mix_pre_norm=p[0:1], mix_post_norm=p[1:2], ffn_pre_norm=p[2:3], ffn_post_norm=p[3:4],
                attn_sinks=p[4:5, :N_ATTN_HEADS], loss=p[5, 0], conv_b=p[8:14].reshape(1, -1)[:, :UP_W],
                conv_w=_unpack_conv(p[SMALL_ROWS:SMALL_ROWS + ROWS_CONV]))


def _local_step(x, target, g_mix_pre, w_in, sinks, w_out, g_mix_post, g_ffn_pre, w_up, conv_w, conv_b, w_down, g_ffn_post,
                distributed=True, rope=None):
    s = x.shape[0]
    consts = _ret_constants()
    sin, cos = _rope_tables(s) if rope is None else rope

    by_half = lambda g, rows: g.reshape(N_CHIPS, 2, rows // (2 * N_CHIPS), g.shape[-1])

    if distributed:
        (h1, proj), level1 = _in_proj(x, g_mix_pre, w_in, _hosted_gather_level1([w_out, w_up, w_down]))
        (mix, states), (w_out, w_up, w_down) = _mixer_fwd(proj, sinks, sin, cos, consts, _hosted_gather_level2(level1))
        w_out, w_down = w_out.reshape(D_MODEL, D_MODEL), w_down.reshape(D_FF, D_MODEL)
        w_up = w_up.reshape(N_CHIPS, D_MODEL, UP_W // N_CHIPS)
    else:
        (h1, proj), _ = _in_proj(x, g_mix_pre, w_in)
        (mix, states), _ = _mixer_fwd(proj, sinks, sin, cos, consts)
    mixed, x1, h2, u0 = _out_up_proj(mix, x, w_out, g_mix_post, g_ffn_pre, w_up)
    y, dy2, dout, du, conv_acc, tail_acc = _ffn_tail(u0, x1, target, conv_w, conv_b, w_down, g_ffn_post)
    du0, dx1, dmixed, dmix, head_acc = _ffn_head_bwd(du, conv_w, w_up, x1, g_ffn_pre, dout, mixed, g_mix_post, w_out)

    d_w_down = _weight_grad(y, dy2, 512, "grad_w_down")
    d_w_up = _weight_grad(h2, du0, UP_W // N_CHIPS, "grad_w_up", by_block=True)
    d_w_out = _weight_grad(mix, dmixed, D_MODEL, "grad_w_out")
    early = [by_half(d_w_down, D_FF), by_half(d_w_up, N_CHIPS * D_MODEL), by_half(d_w_out, D_MODEL)]
    (dproj, dsinks), early_lands = _mixer_bwd(proj, dmix, states, sinks, sin, cos, consts,
                                              _hosted_scatter(early) if distributed else None)
    d_w_in_t = _weight_grad(dproj, h1, 512, "grad_w_in")
    late = [by_half(d_w_in_t, IN_W)]
    (grad_x, in_acc), late_lands = _in_proj_bwd(dproj, w_in, x, g_mix_pre, dx1, _hosted_scatter(late) if distributed else None)

    small = _pack_small(in_acc[0], head_acc[1], head_acc[0], tail_acc[0], dsinks[0, :N_ATTN_HEADS], conv_acc[3],
                        jnp.sum(tail_acc[1]))
    d_conv = jnp.pad(conv_acc[0:CONV_WIDTH].reshape(-1), (0, CONV_FULL_ROWS * D_MODEL - CONV_WIDTH * UP_W))
    small = jnp.concatenate([small, d_conv.reshape(CONV_FULL_ROWS, D_MODEL)], axis=0)
    grads = dict(w_down=early[0], w_up=early[1], w_out=early[2], w_in=late[0])
    lands = dict(zip(["w_down", "w_up", "w_out", "w_in"], early_lands + late_lands))
    return grad_x, grads, lands, small


def kernel(x, mix_pre_norm, w_in, attn_sinks, w_out, mix_post_norm, ffn_pre_norm, w_up, conv_w, conv_b, w_down, ffn_post_norm, loss_target, m_mix_pre_norm, m_w_in, m_attn_sinks, m_w_out, m_mix_post_norm, m_ffn_pre_norm, m_w_up, m_conv_w, m_conv_b, m_w_down, m_ffn_post_norm, v_mix_pre_norm, v_w_in, v_attn_sinks, v_w_out, v_mix_post_norm, v_ffn_pre_norm, v_w_up, v_conv_w, v_conv_b, v_w_down, v_ffn_post_norm):
    cx, cy, cc = _place()
    shard = 2 * cx + cy

    conv_rows = jnp.pad(conv_w[0], ((0, 16 - CONV_WIDTH), (0, 0)))
    w_in_t = jnp.swapaxes(w_in[0], 0, 1)
    (w_in_all, conv_all), sin, cos = _gather_now([w_in_t.astype(BF16), conv_rows], "gather_w_in", x.shape[1])
    conv_full = _shards_to_columns(conv_all[:, 0, :CONV_WIDTH])

    grad_x, grads, lands, small = _local_step(
        x[0], loss_target[0], mix_pre_norm, w_in_all.reshape(IN_W, D_MODEL), attn_sinks.reshape(-1), w_out[0].astype(BF16),
        mix_post_norm, ffn_pre_norm, w_up[0].astype(BF16), conv_full, conv_b, w_down[0].astype(BF16), ffn_post_norm,
        rope=(sin, cos))

    place = jnp.stack([shard, cc]).astype(jnp.int32)
    mats = ["w_in", "w_out", "w_up", "w_down"]
    halves = [_sum_pieces(grads[n], lands[n], place, "sum_grad_" + n) for n in mats]
    weights = dict(w_in=(w_in, m_w_in, v_w_in), w_out=(w_out, m_w_out, v_w_out), w_up=(w_up, m_w_up, v_w_up),
                   w_down=(w_down, m_w_down, v_w_down))
    mat_out = {}
    joined_all, small_all = _join_halves(halves, small)
    for n, joined in zip(mats, joined_all):
        w, m, v = (t[0] for t in weights[n])
        if n == "w_in":
            w, m, v = (jnp.swapaxes(t, 0, 1) for t in (w, m, v))
        res = (joined.reshape(w.shape),) + tuple(_adamw(w, joined.reshape(w.shape), m, v, "adamw_" + n))
        mat_out[n] = tuple(jnp.swapaxes(t, 0, 1) for t in res) if n == "w_in" else res

    small_sum = _sum_devices(small_all)
    d_conv_full = small_sum[SMALL_ROWS:].reshape(-1)[:CONV_WIDTH * UP_W].reshape(CONV_WIDTH, UP_W)
    d_conv_mine = lax.dynamic_slice_in_dim(d_conv_full, shard * (UP_W // N_CHIPS), UP_W // N_CHIPS, axis=1)
    g_s = jnp.concatenate([small_sum[:SMALL_ROWS], _pack_conv(d_conv_mine)], axis=0)
    zero = jnp.zeros((), F32)
    pack_rep = lambda a, b, c_, d, e, f, cw: jnp.concatenate([_pack_small(a, b, c_, d, e, f, zero), _pack_conv(cw[0])], axis=0)
    w_s = pack_rep(mix_pre_norm, mix_post_norm, ffn_pre_norm, ffn_post_norm, attn_sinks, conv_b, conv_w)
    m_s = pack_rep(m_mix_pre_norm, m_mix_post_norm, m_ffn_pre_norm, m_ffn_post_norm, m_attn_sinks, m_conv_b, m_conv_w)
    v_s = pack_rep(v_mix_pre_norm, v_mix_post_norm, v_ffn_pre_norm, v_ffn_post_norm, v_attn_sinks, v_conv_b, v_conv_w)
    delta_s, new_m_s, new_v_s = _adamw(w_s, g_s, m_s, v_s, "adamw_small")

    names = ["mix_pre_norm", "w_in", "attn_sinks", "w_out", "mix_post_norm", "ffn_pre_norm", "w_up", "conv_w", "conv_b",
             "w_down", "ffn_post_norm"]

    def leaves(which, packed_small):
        smalls = _unpack_small(packed_small)
        return [mat_out[n][which][None] if n in mat_out else (smalls[n][None] if n == "conv_w" else smalls[n]) for n in names]

    loss = _unpack_small(g_s)["loss"]
    return (loss, grad_x[None], *leaves(0, g_s), *leaves(1, delta_s), *leaves(2, new_m_s), *leaves(3, new_v_s))
```

```python
import math

import jax
import jax.numpy as jnp
from jax import lax
from jax.experimental import pallas as pl
from jax.experimental.pallas import tpu as pltpu

F32 = jnp.float32
BF16 = jnp.bfloat16

D_MODEL = 1024
HEAD_DIM = 64
ATTN_W = 512
N_ATTN_HEADS = 8
KV_W = 128
RET_W = 512
N_RET_HEADS = 4
RET_HEAD_DIM = 128
CHUNK = 128
IN_W = 2816
D_FF = 2816
UP_W = 2 * D_FF
CONV_WIDTH = 3
RMS_EPS = 1e-6
GN_EPS = 1e-6
MASK_VALUE = -1e30
ATTN_SCALE = HEAD_DIM ** -0.5
RET_K_SCALE = RET_HEAD_DIM ** -0.5
GELU_C = math.sqrt(2.0 / math.pi)
GELU_A = 0.044715

ADAM_LR = 0.001
ADAM_B1 = 0.9
ADAM_B2 = 0.999
ADAM_EPS = 1e-08
ADAM_WD = 0.01
ADAM_STEP = 10

N_CHIPS = 4
N_DEV = 8
MESH = pl.DeviceIdType.MESH
VMEM_LIMIT_V7X = 56 * 1024 * 1024
TOKEN_TILE = 256
BIG_TOKEN_TILE = 512
IN_PROJ_TOKEN_TILE = 1024
WEIGHT_GRAD_TOKENS = 2048
FFN_ROW_BLOCK = 64
HEAD_BWD_COLS = 512
MIXER_CHUNKS_PER_STEP = 4
Q_A0, KV_A0, Q_R0, K_R0, V_R0, G_R0 = 0, 512, 768, 1280, 1792, 2304

ROWS_CONV = 8
SMALL_ROWS = 16
CONV_FULL_ROWS = 24


def _params(sem=None, **kw):
    if sem is not None:
        kw["dimension_semantics"] = sem
    return pltpu.CompilerParams(vmem_limit_bytes=VMEM_LIMIT_V7X, **kw)


def _resident(shape):
    zeros = (0,) * len(shape)
    return pl.BlockSpec(shape, lambda *_: zeros, pipeline_mode=pl.Buffered(1))


class _Hosted:
    def __init__(self, ins, outs, aliases, n_pairs, n_local, start, finish):
        self.ins, self.outs, self.aliases = list(ins), list(outs), dict(aliases)
        self.n_pairs, self.n_local, self.start, self.finish = n_pairs, max(n_local, 1), start, finish


def _hosted_call(compute, *, name, grid, in_specs, out_specs, out_shape, scratch_shapes, args, hosted=None):
    params = _params(("arbitrary",) * len(grid))
    if hosted is None:
        res = pl.pallas_call(compute, name=name, grid=grid, in_specs=in_specs, out_specs=out_specs, out_shape=out_shape,
                             scratch_shapes=scratch_shapes, compiler_params=params)(*args)
        return list(res), []
    n_in, n_out, n_scr = len(in_specs), len(out_specs), len(scratch_shapes)
    h_in, h_out = len(hosted.ins), len(hosted.outs)

    def at(step_of):
        cond = pl.program_id(0) == step_of(grid[0])
        for d in range(1, len(grid)):
            cond = jnp.logical_and(cond, pl.program_id(d) == step_of(grid[d]))
        return cond

    def body(*refs):
        ins, refs = refs[:n_in], refs[n_in:]
        h_ins, refs = refs[:h_in], refs[h_in:]
        outs, refs = refs[:n_out], refs[n_out:]
        h_outs, refs = refs[:h_out], refs[h_out:]
        scr, sems = refs[:n_scr], refs[n_scr:]

        @pl.when(at(lambda n: 0))
        def _():
            hosted.start(h_ins, h_outs, *sems)

        compute(*ins, *outs, *scr)

        @pl.when(at(lambda n: n - 1))
        def _():
            hosted.finish(h_ins, h_outs, *sems)

    hbm = pl.BlockSpec(memory_space=pl.ANY)
    res = pl.pallas_call(
        body, name=name, grid=grid,
        in_specs=list(in_specs) + [hbm] * h_in, out_specs=list(out_specs) + [hbm] * h_out,
        out_shape=list(out_shape) + hosted.outs,
        scratch_shapes=list(scratch_shapes) + [pltpu.SemaphoreType.DMA((hosted.n_pairs,)), pltpu.SemaphoreType.DMA((hosted.n_pairs,)),
                                               pltpu.SemaphoreType.DMA((hosted.n_local,))],
        input_output_aliases={n_in + a: n_out + b for a, b in hosted.aliases.items()},
        compiler_params=params,
    )(*args, *hosted.ins)
    return list(res[:n_out]), list(res[n_out:])


def _dot(a, b):
    return jnp.dot(a, b, preferred_element_type=F32)


def _dot_nt(a, b):
    return lax.dot_general(a, b, (((1,), (1,)), ((), ())), preferred_element_type=F32)


def _dot_tn(a, b):
    return lax.dot_general(a, b, (((0,), (0,)), ((), ())), preferred_element_type=F32)


def _shift_matrix(n, by):
    row = lax.broadcasted_iota(jnp.int32, (n, n), 0)
    col = lax.broadcasted_iota(jnp.int32, (n, n), 1)
    return jnp.where(col == row + by, 1.0, 0.0).astype(BF16)


def _rstd(v):
    return lax.rsqrt(jnp.mean(v * v, axis=-1, keepdims=True) + RMS_EPS)


def _rms_bwd(dy, v, rstd, gain):
    n = v * rstd
    dgain = jnp.sum(dy * n, axis=0, keepdims=True)
    dn = dy * gain
    dv = rstd * (dn - n * jnp.mean(dn * n, axis=-1, keepdims=True))
    return dv, dgain


def _lane_lo(shape):
    return (lax.broadcasted_iota(jnp.int32, shape, 1) % 128) < HEAD_DIM


GROUP = N_ATTN_HEADS // (KV_W // HEAD_DIM)


def _attn_bias(first_chunk):
    qi = lax.broadcasted_iota(jnp.int32, (GROUP * CHUNK, 2 * CHUNK), 0) % CHUNK
    kj = lax.broadcasted_iota(jnp.int32, (GROUP * CHUNK, 2 * CHUNK), 1)
    valid = jnp.logical_and(kj > qi, kj <= qi + CHUNK)
    if first_chunk:
        valid = jnp.logical_and(valid, kj >= CHUNK)
    return jnp.where(valid, 0.0, MASK_VALUE)


def _half(shape, hk):
    lo = _lane_lo(shape)
    return lo if hk == 0 else jnp.logical_not(lo)


class _GroupMasks:
    def __init__(self, sk_ref):
        groups = range(KV_W // HEAD_DIM)
        self.q = [_half((CHUNK, 128), hk) for hk in groups]
        self.kv = [_half((2 * CHUNK, 128), hk) for hk in groups]
        self.sinks = [_group_sinks(sk_ref, hk) for hk in groups]


def _stack_heads(ref, row0, col0, hk, half):
    parts = []
    for j in range(GROUP):
        h = GROUP * hk + j
        pair = ref[row0:row0 + CHUNK, col0 + (h // 2) * 128:col0 + (h // 2 + 1) * 128].astype(F32)
        if h % 2 != hk:
            pair = pltpu.roll(pair, HEAD_DIM, 1)
        parts.append(jnp.where(half, pair, 0.0))
    return jnp.concatenate(parts, axis=0)


def _unstack_heads(stacked, hk):
    pairs = []
    for q in range(GROUP // 2):
        even, odd = stacked[2 * q * CHUNK:(2 * q + 1) * CHUNK], stacked[(2 * q + 1) * CHUNK:(2 * q + 2) * CHUNK]
        pairs.append(even + pltpu.roll(odd, HEAD_DIM, 1) if hk == 0 else pltpu.roll(even, HEAD_DIM, 1) + odd)
    return pairs


def _group_sinks(sk_ref, hk):
    row = lax.broadcasted_iota(jnp.int32, (GROUP * CHUNK, 1), 0)
    col = jnp.full((GROUP * CHUNK, 1), sk_ref[GROUP * hk], F32)
    for j in range(1, GROUP):
        col = jnp.where(row >= j * CHUNK, sk_ref[GROUP * hk + j], col)
    return col


def _attn_probs(q_b, kk_b, bias, sink):
    s = _dot_nt(q_b, kk_b) * ATTN_SCALE + bias
    m = jnp.maximum(jnp.max(s, axis=-1, keepdims=True), sink)
    e = jnp.exp(s - m)
    e_sink = jnp.exp(sink - m)
    inv = 1.0 / (jnp.sum(e, axis=-1, keepdims=True) + e_sink)
    return e * inv, e_sink * inv


def _even_lanes(shape):
    return (lax.broadcasted_iota(jnp.int32, shape, 1) % 2) == 0


def _swap2(v, even):
    return jnp.where(even, pltpu.roll(v, v.shape[1] - 1, 1), pltpu.roll(v, 1, 1))


def _tile4(v):
    return jnp.concatenate([v, v, v, v], axis=-1)


def _sigmoid(v):
    return 1.0 / (1.0 + jnp.exp(-v))


def _ret_constants():
    h = N_RET_HEADS
    log_gamma = jnp.log(1.0 - jnp.power(2.0, -5.0 - jnp.arange(h, dtype=F32)))
    idx = jnp.arange(CHUNK, dtype=F32)
    rel = idx[:, None] - idx[None, :]
    d_intra = jnp.where(rel[None] >= 0, jnp.exp(log_gamma[:, None, None] * jnp.maximum(rel, 0.0)[None]), 0.0)
    xi = jnp.exp(log_gamma[None, :] * (idx[:, None] + 1.0))
    zeta = jnp.exp(log_gamma[None, :] * (CHUNK - 1.0 - idx[:, None]))
    decay = jnp.exp(log_gamma * CHUNK)
    xi_full = jnp.repeat(xi, RET_HEAD_DIM, axis=1)
    zeta_full = jnp.repeat(zeta, RET_HEAD_DIM, axis=1)
    decay_full = jnp.broadcast_to(jnp.repeat(decay, RET_HEAD_DIM)[None, :], (8, RET_W))
    return d_intra.astype(F32), xi_full.astype(F32), zeta_full.astype(F32), decay_full.astype(F32)


def _rope_tables(s):
    pos = jnp.arange(s, dtype=F32)
    angle = 1.0 / jnp.power(10000.0, jnp.linspace(0.0, 1.0, RET_HEAD_DIM // 2, dtype=F32))
    angle = jnp.repeat(angle, 2)
    sign = jnp.where(jnp.arange(RET_HEAD_DIM) % 2 == 0, -1.0, 1.0).astype(F32)
    return jnp.sin(pos[:, None] * angle[None]) * sign[None], jnp.cos(pos[:, None] * angle[None])


def _in_proj(x, gain, w_in_t, hosted=None):
    s = x.shape[0]
    tm = min(IN_PROJ_TOKEN_TILE, s)

    def body(x_ref, g_ref, w_ref, h_ref, p_ref):
        xv = x_ref[...]
        h = (xv * _rstd(xv) * g_ref[...]).astype(BF16)
        h_ref[...] = h
        p_ref[...] = _dot_nt(h, w_ref[...])

    return _hosted_call(
        body, name="in_proj", grid=(s // tm,),
        in_specs=[pl.BlockSpec((tm, D_MODEL), lambda i: (i, 0)), _resident((1, D_MODEL)), _resident((IN_W, D_MODEL))],
        out_specs=[pl.BlockSpec((tm, D_MODEL), lambda i: (i, 0)), pl.BlockSpec((tm, IN_W), lambda i: (i, 0))],
        out_shape=[jax.ShapeDtypeStruct((s, D_MODEL), BF16), jax.ShapeDtypeStruct((s, IN_W), F32)],
        scratch_shapes=[], args=(x, gain, w_in_t), hosted=hosted)


def _mixer_fwd(proj, sinks, sin, cos, consts, hosted=None):
    s = proj.shape[0]
    nc = s // CHUNK
    cps = MIXER_CHUNKS_PER_STEP
    groups = KV_W // HEAD_DIM
    d_intra, xi_full, zeta_full, decay_full = consts

    def body(sk_ref, p_ref, pkv_ref, sin_ref, cos_ref, dm_ref, xi_ref, ze_ref, dc_ref, mix_ref, st_ref, pr_ref, ps_ref, state):
        i = pl.program_id(0)

        @pl.when(i == 0)
        def _():
            state[...] = jnp.zeros_like(state)

        st = [state[h] for h in range(N_RET_HEADS)]
        bias_any = _attn_bias(False)
        bias_c0 = jnp.where(i == 0, _attn_bias(True), bias_any)
        even = _even_lanes((CHUNK, RET_W))
        masks = _GroupMasks(sk_ref)
        for c in range(cps):
            r0 = c * CHUNK
            rows = slice(r0, r0 + CHUNK)

            kv_cur = p_ref[rows, KV_A0:KV_A0 + 2 * KV_W]
            kv_prev = pkv_ref[...] if c == 0 else p_ref[r0 - CHUNK:r0, KV_A0:KV_A0 + 2 * KV_W]
            kk = jnp.concatenate([kv_prev[:, :KV_W], kv_cur[:, :KV_W]], axis=0)
            vv = jnp.concatenate([kv_prev[:, KV_W:], kv_cur[:, KV_W:]], axis=0)
            kk_b = kk.astype(BF16)
            bias = bias_c0 if c == 0 else bias_any
            for hk in range(KV_W // HEAD_DIM):
                q_b = _stack_heads(p_ref, r0, Q_A0, hk, masks.q[hk]).astype(BF16)
                p, p_sink = _attn_probs(q_b, kk_b, bias, masks.sinks[hk])
                p_b = p.astype(BF16)
                pr_ref[c, hk] = p_b
                ps_ref[c, hk] = p_sink
                v_b = jnp.where(masks.kv[hk], vv, 0.0).astype(BF16)
                for q, pair in enumerate(_unstack_heads(_dot(p_b, v_b), hk)):
                    pi = (GROUP // 2) * hk + q
                    mix_ref[rows, pi * 128:(pi + 1) * 128] = pair.astype(BF16)

            sin4, cos4 = _tile4(sin_ref[rows, :]), _tile4(cos_ref[rows, :])
            q_r = p_ref[rows, Q_R0:Q_R0 + RET_W]
            k_r = p_ref[rows, K_R0:K_R0 + RET_W] * RET_K_SCALE
            q_r = q_r * cos4 + _swap2(q_r, even) * sin4
            k_r = k_r * cos4 + _swap2(k_r, even) * sin4
            kz = k_r * ze_ref[...]
            for h in range(N_RET_HEADS):
                sl = slice(h * RET_HEAD_DIM, (h + 1) * RET_HEAD_DIM)
                qh, kh = q_r[:, sl].astype(BF16), k_r[:, sl].astype(BF16)
                vh = p_ref[rows, V_R0 + h * RET_HEAD_DIM:V_R0 + (h + 1) * RET_HEAD_DIM].astype(BF16)
                st_ref[c, h] = st[h]
                a = _dot_nt(qh, kh) * dm_ref[h]
                qx = (q_r[:, sl] * xi_ref[:, sl]).astype(BF16)
                o = _dot(jnp.concatenate([a.astype(BF16), qx], axis=1), jnp.concatenate([vh, st[h].astype(BF16)], axis=0))
                st[h] = dc_ref[0:1, sl] * st[h] + _dot_tn(kz[:, sl].astype(BF16), vh)
                mu = jnp.mean(o, axis=-1, keepdims=True)
                oc = o - mu
                on = oc * lax.rsqrt(jnp.mean(oc * oc, axis=-1, keepdims=True) + GN_EPS)
                g = p_ref[rows, G_R0 + h * RET_HEAD_DIM:G_R0 + (h + 1) * RET_HEAD_DIM]
                mix_ref[rows, ATTN_W + h * RET_HEAD_DIM:ATTN_W + (h + 1) * RET_HEAD_DIM] = (g * _sigmoid(g) * on).astype(BF16)
        for h in range(N_RET_HEADS):
            state[h] = st[h]

    return _hosted_call(
        body, name="mixer_fwd", grid=(nc // cps,),
        in_specs=[
            pl.BlockSpec(memory_space=pltpu.SMEM),
            pl.BlockSpec((cps * CHUNK, IN_W), lambda i: (i, 0)),
            pl.BlockSpec((CHUNK, 2 * KV_W), lambda i: (jnp.maximum(cps * i - 1, 0), KV_A0 // (2 * KV_W))),
            pl.BlockSpec((cps * CHUNK, RET_HEAD_DIM), lambda i: (i, 0)),
            pl.BlockSpec((cps * CHUNK, RET_HEAD_DIM), lambda i: (i, 0)),
            _resident((N_RET_HEADS, CHUNK, CHUNK)), _resident((CHUNK, RET_W)), _resident((CHUNK, RET_W)), _resident((8, RET_W)),
        ],
        out_specs=[
            pl.BlockSpec((cps * CHUNK, D_MODEL), lambda i: (i, 0)),
            pl.BlockSpec((cps, N_RET_HEADS, RET_HEAD_DIM, RET_HEAD_DIM), lambda i: (i, 0, 0, 0)),
            pl.BlockSpec((cps, groups, GROUP * CHUNK, 2 * CHUNK), lambda i: (i, 0, 0, 0)),
            pl.BlockSpec((cps, groups, GROUP * CHUNK, 1), lambda i: (i, 0, 0, 0)),
        ],
        out_shape=[jax.ShapeDtypeStruct((s, D_MODEL), BF16),
                   jax.ShapeDtypeStruct((nc, N_RET_HEADS, RET_HEAD_DIM, RET_HEAD_DIM), F32),
                   jax.ShapeDtypeStruct((nc, groups, GROUP * CHUNK, 2 * CHUNK), BF16),
                   jax.ShapeDtypeStruct((nc, groups, GROUP * CHUNK, 1), F32)],
        scratch_shapes=[pltpu.VMEM((N_RET_HEADS, RET_HEAD_DIM, RET_HEAD_DIM), F32)],
        args=(sinks, proj, proj, sin, cos, d_intra, xi_full, zeta_full, decay_full), hosted=hosted)


def _out_up_proj(mix, x, w_out, g_post, g_pre, w_up):
    s = x.shape[0]
    tm = min(BIG_TOKEN_TILE, s)
    blk = UP_W // N_CHIPS

    def body(mix_ref, x_ref, wo_ref, g2_ref, g3_ref, wu_ref, mixed_ref, x1_ref, h2_ref, u0_ref):
        mixed = _dot(mix_ref[...], wo_ref[...])
        mixed_ref[...] = mixed
        x1 = x_ref[...] + mixed * _rstd(mixed) * g2_ref[...]
        x1_ref[...] = x1
        h2 = (x1 * _rstd(x1) * g3_ref[...]).astype(BF16)
        h2_ref[...] = h2
        for k in range(N_CHIPS):
            u0_ref[:, k * blk:(k + 1) * blk] = _dot(h2, wu_ref[k]).astype(BF16)

    tok = lambda w: pl.BlockSpec((tm, w), lambda i: (i, 0))
    return pl.pallas_call(
        body, name="out_up_proj", grid=(s // tm,),
        in_specs=[tok(D_MODEL), tok(D_MODEL), _resident((D_MODEL, D_MODEL)), _resident((1, D_MODEL)), _resident((1, D_MODEL)),
                  _resident((N_CHIPS, D_MODEL, blk))],
        out_specs=[tok(D_MODEL), tok(D_MODEL), tok(D_MODEL), tok(UP_W)],
        out_shape=[jax.ShapeDtypeStruct((s, D_MODEL), F32), jax.ShapeDtypeStruct((s, D_MODEL), F32),
                   jax.ShapeDtypeStruct((s, D_MODEL), BF16), jax.ShapeDtypeStruct((s, UP_W), BF16)],
        compiler_params=_params(("arbitrary",)),
    )(mix, x, w_out, g_post, g_pre, w_up)


def _ffn_tail(u0, x1, target, conv_w, conv_b, w_down, g_post):
    s = x1.shape[0]
    tm = TOKEN_TILE
    last = s // tm - 1
    rb, lanes = FFN_ROW_BLOCK, 128

    def body(u0_ref, x1_ref, t_ref, cw_ref, cb_ref, wd_ref, g_ref,
             y_ref, dy2_ref, dout_ref, du_ref, cacc_ref, gacc_ref, u1_s, u2_s, carry, gelu_s, slope_s, dy_s, cacc):
        i = pl.program_id(0)

        @pl.when(i == 0)
        def _():
            carry[...] = jnp.zeros_like(carry)
            cacc[...] = jnp.zeros_like(cacc)
            gacc_ref[...] = jnp.zeros_like(gacc_ref)

        shift1, shift2 = _shift_matrix(tm, -1), _shift_matrix(tm, -2)
        r8 = lax.broadcasted_iota(jnp.int32, (8, 1), 0)
        wide = 2 * lanes

        def shift_block(col):
            cols = slice(col, col + wide)
            u1_s[:, cols] = _dot(shift1, u0_ref[:, cols])
            u2_s[:, cols] = _dot(shift2, u0_ref[:, cols])
            c14, c15 = carry[14:15, cols], carry[15:16, cols]
            u1_s[0:8, cols] = jnp.where(r8 == 0, c15, u1_s[0:8, cols])
            u2_s[0:8, cols] = jnp.where(r8 == 0, c14, jnp.where(r8 == 1, c15, u2_s[0:8, cols]))

        def taps(col):
            return (cw_ref[0:1, col:col + lanes], cw_ref[1:2, col:col + lanes], cw_ref[2:3, col:col + lanes],
                    cb_ref[0:1, col:col + lanes])

        def shifted(r0, col):
            return (u2_s[r0:r0 + rb, col:col + lanes], u1_s[r0:r0 + rb, col:col + lanes],
                    u0_ref[r0:r0 + rb, col:col + lanes].astype(F32))

        def conv(r0, col, w):
            u2, u1, uc = shifted(r0, col)
            return w[0] * u2 + w[1] * u1 + w[2] * uc + w[3]

        fold = lambda v: jnp.sum(v.reshape(rb // 8, 8, lanes), axis=0)

        shift_block(0)
        shift_block(D_FF)
        for j in range(D_FF // lanes):
            cg, cv = j * lanes, D_FF + j * lanes
            if cg % wide == 0 and cg + wide < D_FF:
                shift_block(cg + wide)
                shift_block(cv + wide)
            wg, wv = taps(cg), taps(cv)
            for r0 in range(0, tm, rb):
                gate, val = conv(r0, cg, wg), conv(r0, cv, wv)
                g2 = gate * gate
                th = jnp.tanh(gate * (GELU_C + GELU_C * GELU_A * g2))
                hp = 0.5 * th + 0.5
                gelu = gate * hp
                dgelu = hp + gate * (1.0 - th * th) * (0.5 * GELU_C + 1.5 * GELU_C * GELU_A * g2)
                y_ref[r0:r0 + rb, cg:cg + lanes] = (gelu * val).astype(BF16)
                gelu_s[r0:r0 + rb, cg:cg + lanes] = gelu
                slope_s[r0:r0 + rb, cg:cg + lanes] = dgelu * val

        y2 = _dot(y_ref[...], wd_ref[...])
        r4 = _rstd(y2)
        gain = g_ref[...]
        out = x1_ref[...] + y2 * r4 * gain
        diff = out - t_ref[...]
        dout = diff * (1.0 / D_MODEL)
        dout_ref[...] = dout
        dy2, dgain = _rms_bwd(dout, y2, r4, gain)
        dy2_b = dy2.astype(BF16)
        dy2_ref[...] = dy2_b
        gacc_ref[0:1, :] += dgain
        gacc_ref[1:2, :] += 0.5 * jnp.sum(diff * dout, axis=0, keepdims=True)
        carry[...] = u0_ref[tm - 16:tm, :].astype(F32)

        dy_s[:, 0:wide] = _dot_nt(dy2_b, wd_ref[0:wide, :])
        for j in range(D_FF // lanes):
            cg, cv = j * lanes, D_FF + j * lanes
            if cg % wide == 0 and cg + wide < D_FF:
                dy_s[:, cg + wide:cg + 2 * wide] = _dot_nt(dy2_b, wd_ref[cg + wide:cg + 2 * wide, :])
            acc = [[jnp.zeros((8, lanes), F32) for _ in range(CONV_WIDTH + 1)] for _ in range(2)]
            for r0 in range(0, tm, rb):
                dy = dy_s[r0:r0 + rb, cg:cg + lanes]
                d_gate = dy * slope_s[r0:r0 + rb, cg:cg + lanes]
                d_val = dy * gelu_s[r0:r0 + rb, cg:cg + lanes]
                for side, (col, d) in enumerate(((cg, d_gate), (cv, d_val))):
                    du_ref[r0:r0 + rb, col:col + lanes] = d.astype(BF16)
                    for k, u in enumerate(shifted(r0, col)):
                        acc[side][k] = acc[side][k] + fold(d * u)
                    acc[side][CONV_WIDTH] = acc[side][CONV_WIDTH] + fold(d)
            for side, col in enumerate((cg, cv)):
                for k in range(CONV_WIDTH + 1):
                    cacc[8 * k:8 * k + 8, col:col + lanes] += acc[side][k]

        @pl.when(i == last)
        def _():
            for k in range(CONV_WIDTH + 1):
                cacc_ref[k:k + 1, :] = jnp.sum(cacc[8 * k:8 * k + 8, :], axis=0, keepdims=True)
            cacc_ref[CONV_WIDTH + 1:8, :] = jnp.zeros((8 - CONV_WIDTH - 1, UP_W), F32)

    tok = lambda w: pl.BlockSpec((tm, w), lambda i: (i, 0))
    return pl.pallas_call(
        body, name="ffn_tail", grid=(s // tm,),
        in_specs=[tok(UP_W), tok(D_MODEL), tok(D_MODEL), _resident((CONV_WIDTH, UP_W)), _resident((1, UP_W)),
                  _resident((D_FF, D_MODEL)), _resident((1, D_MODEL))],
        out_specs=[tok(D_FF), tok(D_MODEL), tok(D_MODEL), tok(UP_W),
                   pl.BlockSpec((8, UP_W), lambda i: (0, 0)), pl.BlockSpec((8, D_MODEL), lambda i: (0, 0))],
        out_shape=[jax.ShapeDtypeStruct((s, D_FF), BF16), jax.ShapeDtypeStruct((s, D_MODEL), BF16),
                   jax.ShapeDtypeStruct((s, D_MODEL), F32), jax.ShapeDtypeStruct((s, UP_W), BF16),
                   jax.ShapeDtypeStruct((8, UP_W), F32), jax.ShapeDtypeStruct((8, D_MODEL), F32)],
        scratch_shapes=[pltpu.VMEM((tm, UP_W), F32), pltpu.VMEM((tm, UP_W), F32), pltpu.VMEM((16, UP_W), F32),
                        pltpu.VMEM((tm, D_FF), F32), pltpu.VMEM((tm, D_FF), F32),
                        pltpu.VMEM((tm, D_FF), F32), pltpu.VMEM((8 * (CONV_WIDTH + 1), UP_W), F32)],
        compiler_params=_params(("arbitrary",)),
    )(u0, x1, target, conv_w, conv_b, w_down, g_post)


def _ffn_head_bwd(du, conv_w, w_up, x1, g_pre, dout, mixed, g_post, w_out):
    s = x1.shape[0]
    tm = TOKEN_TILE
    nt = s // tm
    blk = UP_W // N_CHIPS

    def body(du_ref, halo_ref, cw_ref, wu_ref, x1_ref, g3_ref, dout_ref, mixed_ref, g2_ref, wo_ref,
             du0_ref, dx1_ref, dmixed_ref, dmix_ref, gacc_ref, dbuf):
        i = pl.program_id(0)

        @pl.when(i == 0)
        def _():
            gacc_ref[...] = jnp.zeros_like(gacc_ref)

        dbuf[0:tm, :] = du_ref[...].astype(F32)
        dbuf[tm:tm + 16, :] = jnp.where(i < nt - 1, halo_ref[...].astype(F32), 0.0)
        dh2 = jnp.zeros((tm, D_MODEL), F32)
        for k in range(N_CHIPS):
            for c0 in range(0, blk, HEAD_BWD_COLS):
                width = min(HEAD_BWD_COLS, blk - c0)
                cols = slice(k * blk + c0, k * blk + c0 + width)
                du0_b = (cw_ref[2:3, cols] * dbuf[0:tm, cols] + cw_ref[1:2, cols] * dbuf[1:1 + tm, cols]
                         + cw_ref[0:1, cols] * dbuf[2:2 + tm, cols]).astype(BF16)
                du0_ref[:, cols] = du0_b
                dh2 = dh2 + _dot_nt(du0_b, wu_ref[k, :, c0:c0 + width])
        x1 = x1_ref[...]
        d3, dg3 = _rms_bwd(dh2, x1, _rstd(x1), g3_ref[...])
        dx1 = dout_ref[...] + d3
        dx1_ref[...] = dx1
        mixed = mixed_ref[...]
        dmixed, dg2 = _rms_bwd(dx1, mixed, _rstd(mixed), g2_ref[...])
        dmixed_b = dmixed.astype(BF16)
        dmixed_ref[...] = dmixed_b
        dmix_ref[...] = _dot_nt(dmixed_b, wo_ref[...]).astype(BF16)
        gacc_ref[0:1, :] += dg3
        gacc_ref[1:2, :] += dg2

    tok = lambda w: pl.BlockSpec((tm, w), lambda i: (i, 0))
    halo = pl.BlockSpec((16, UP_W), lambda i: (jnp.minimum(i + 1, nt - 1) * (tm // 16), 0))
    return pl.pallas_call(
        body, name="ffn_head_bwd", grid=(nt,),
        in_specs=[tok(UP_W), halo, _resident((CONV_WIDTH, UP_W)), _resident((N_CHIPS, D_MODEL, blk)), tok(D_MODEL),
                  _resident((1, D_MODEL)), tok(D_MODEL), tok(D_MODEL), _resident((1, D_MODEL)), _resident((D_MODEL, D_MODEL))],
        out_specs=[tok(UP_W), tok(D_MODEL), tok(D_MODEL), tok(D_MODEL), pl.BlockSpec((8, D_MODEL), lambda i: (0, 0))],
        out_shape=[jax.ShapeDtypeStruct((s, UP_W), BF16), jax.ShapeDtypeStruct((s, D_MODEL), F32),
                   jax.ShapeDtypeStruct((s, D_MODEL), BF16), jax.ShapeDtypeStruct((s, D_MODEL), BF16),
                   jax.ShapeDtypeStruct((8, D_MODEL), F32)],
        scratch_shapes=[pltpu.VMEM((tm + 16, UP_W), F32)],
        compiler_params=_params(("arbitrary",)),
    )(du, du, conv_w, w_up, x1, g_pre, dout, mixed, g_post, w_out)


def _mixer_bwd(proj, dmix, states, probs, p_sinks, sin, cos, consts, hosted=None):
    s = proj.shape[0]
    nc = s // CHUNK
    cps = MIXER_CHUNKS_PER_STEP
    nb = nc // cps
    groups = KV_W // HEAD_DIM
    d_intra, xi_full, zeta_full, decay_full = consts

    def body(p_ref, pkv_ref, dmix_ref, st_ref, pr_ref, ps_ref, sin_ref, cos_ref, dm_ref, xi_ref, ze_ref, dc_ref,
             dp_ref, dsk_ref, gstate, ckv, dsk_acc):
        i = pl.program_id(0)
        block = nb - 1 - i

        @pl.when(i == 0)
        def _():
            gstate[...] = jnp.zeros_like(gstate)
            ckv[...] = jnp.zeros_like(ckv)
            dsk_acc[...] = jnp.zeros_like(dsk_acc)

        gs_all = [gstate[h] for h in range(N_RET_HEADS)]
        later_kv = ckv[...]
        lane = lax.broadcasted_iota(jnp.int32, (CHUNK, 128), 1)
        dsk = jnp.zeros((CHUNK, 128), F32)
        even = _even_lanes((CHUNK, RET_W))
        half_q = [_half((CHUNK, 128), hk) for hk in range(groups)]
        half_kv = [_half((2 * CHUNK, 128), hk) for hk in range(groups)]
        for c in reversed(range(cps)):
            r0 = c * CHUNK
            rows = slice(r0, r0 + CHUNK)

            kv_cur = p_ref[rows, KV_A0:KV_A0 + 2 * KV_W]
            kv_prev = pkv_ref[...] if c == 0 else p_ref[r0 - CHUNK:r0, KV_A0:KV_A0 + 2 * KV_W]
            kk = jnp.concatenate([kv_prev[:, :KV_W], kv_cur[:, :KV_W]], axis=0)
            vv = jnp.concatenate([kv_prev[:, KV_W:], kv_cur[:, KV_W:]], axis=0)
            vv_b = vv.astype(BF16)
            dkk = jnp.zeros((2 * CHUNK, KV_W), F32)
            dvv = jnp.zeros((2 * CHUNK, KV_W), F32)
            for hk in range(groups):
                q_b = _stack_heads(p_ref, r0, Q_A0, hk, half_q[hk]).astype(BF16)
                do_b = _stack_heads(dmix_ref, r0, 0, hk, half_q[hk]).astype(BF16)
                p_b = pr_ref[c, hk]
                p = p_b.astype(F32)
                dpr = _dot_nt(do_b, vv_b)
                delta = jnp.sum(p * dpr, axis=-1, keepdims=True)
                ds_b = (p * (dpr - delta) * ATTN_SCALE).astype(BF16)
                dsink = -ps_ref[c, hk] * delta
                for j in range(GROUP):
                    dsk = dsk + jnp.where(lane == GROUP * hk + j, dsink[j * CHUNK:(j + 1) * CHUNK], 0.0)
                k_b = jnp.where(half_kv[hk], kk, 0.0).astype(BF16)
                for q, pair in enumerate(_unstack_heads(_dot(ds_b, k_b), hk)):
                    pi = (GROUP // 2) * hk + q
                    dp_ref[rows, Q_A0 + pi * 128:Q_A0 + (pi + 1) * 128] = pair.astype(BF16)
                dkk = dkk + _dot_tn(ds_b, q_b)
                dvv = dvv + _dot_tn(p_b, do_b)
            dp_ref[rows, KV_A0:KV_A0 + KV_W] = (dkk[CHUNK:] + later_kv[:, :KV_W]).astype(BF16)
            dp_ref[rows, KV_A0 + KV_W:KV_A0 + 2 * KV_W] = (dvv[CHUNK:] + later_kv[:, KV_W:]).astype(BF16)
            later_kv = jnp.concatenate([dkk[:CHUNK], dvv[:CHUNK]], axis=1)

            sin4, cos4 = _tile4(sin_ref[rows, :]), _tile4(cos_ref[rows, :])
            q_r = p_ref[rows, Q_R0:Q_R0 + RET_W]
            k_r = p_ref[rows, K_R0:K_R0 + RET_W] * RET_K_SCALE
            q_r = q_r * cos4 + _swap2(q_r, even) * sin4
            k_r = k_r * cos4 + _swap2(k_r, even) * sin4
            kz = k_r * ze_ref[...]
            dq_parts, dk_parts = [], []
            for h in range(N_RET_HEADS):
                sl = slice(h * RET_HEAD_DIM, (h + 1) * RET_HEAD_DIM)
                qh, kh = q_r[:, sl].astype(BF16), k_r[:, sl].astype(BF16)
                vh = p_ref[rows, V_R0 + h * RET_HEAD_DIM:V_R0 + (h + 1) * RET_HEAD_DIM].astype(BF16)
                st_b = st_ref[c, h].astype(BF16)
                gs = gs_all[h]
                gs_b = gs.astype(BF16)
                xi_h = xi_ref[:, sl]
                dm = dm_ref[h]
                a_b = (_dot_nt(qh, kh) * dm).astype(BF16)
                qx = (q_r[:, sl] * xi_h).astype(BF16)
                o = _dot(jnp.concatenate([a_b, qx], axis=1), jnp.concatenate([vh, st_b], axis=0))
                mu = jnp.mean(o, axis=-1, keepdims=True)
                oc = o - mu
                rs = lax.rsqrt(jnp.mean(oc * oc, axis=-1, keepdims=True) + GN_EPS)
                on = oc * rs
                g = p_ref[rows, G_R0 + h * RET_HEAD_DIM:G_R0 + (h + 1) * RET_HEAD_DIM]
                sg = _sigmoid(g)
                dr = dmix_ref[rows, ATTN_W + h * RET_HEAD_DIM:ATTN_W + (h + 1) * RET_HEAD_DIM].astype(F32)
                dp_ref[rows, G_R0 + h * RET_HEAD_DIM:G_R0 + (h + 1) * RET_HEAD_DIM] = (
                    dr * on * (sg * (1.0 + g * (1.0 - sg)))).astype(BF16)
                don = dr * g * sg
                do = rs * (don - jnp.mean(don, axis=-1, keepdims=True) - on * jnp.mean(don * on, axis=-1, keepdims=True))
                do_b = do.astype(BF16)
                dox_b = (do * xi_h).astype(BF16)
                da_b = (_dot_nt(do_b, vh) * dm).astype(BF16)
                dq_parts.append(_dot(da_b, kh) + _dot_nt(dox_b, st_b))
                dk_parts.append(_dot_tn(da_b, qh) + ze_ref[:, sl] * _dot_nt(vh, gs_b))
                dv = _dot_tn(a_b, do_b) + _dot(kz[:, sl].astype(BF16), gs_b)
                dp_ref[rows, V_R0 + h * RET_HEAD_DIM:V_R0 + (h + 1) * RET_HEAD_DIM] = dv.astype(BF16)
                gs_all[h] = dc_ref[0:1, sl] * gs + _dot_tn(qh, dox_b)
            dq = jnp.concatenate(dq_parts, axis=-1)
            dk = jnp.concatenate(dk_parts, axis=-1)
            dp_ref[rows, Q_R0:Q_R0 + RET_W] = (dq * cos4 - _swap2(dq, even) * sin4).astype(BF16)
            dp_ref[rows, K_R0:K_R0 + RET_W] = (RET_K_SCALE * (dk * cos4 - _swap2(dk, even) * sin4)).astype(BF16)

        for h in range(N_RET_HEADS):
            gstate[h] = gs_all[h]
        ckv[...] = later_kv
        dsk_acc[...] += dsk

        @pl.when(i == nb - 1)
        def _():
            dsk_ref[...] = jnp.sum(dsk_acc[...], axis=0, keepdims=True)

    rev = lambda i: nb - 1 - i
    return _hosted_call(
        body, name="mixer_bwd", grid=(nb,),
        in_specs=[
            pl.BlockSpec((cps * CHUNK, IN_W), lambda i: (rev(i), 0)),
            pl.BlockSpec((CHUNK, 2 * KV_W), lambda i: (jnp.maximum(cps * rev(i) - 1, 0), KV_A0 // (2 * KV_W))),
            pl.BlockSpec((cps * CHUNK, D_MODEL), lambda i: (rev(i), 0)),
            pl.BlockSpec((cps, N_RET_HEADS, RET_HEAD_DIM, RET_HEAD_DIM), lambda i: (rev(i), 0, 0, 0)),
            pl.BlockSpec((cps, groups, GROUP * CHUNK, 2 * CHUNK), lambda i: (rev(i), 0, 0, 0)),
            pl.BlockSpec((cps, groups, GROUP * CHUNK, 1), lambda i: (rev(i), 0, 0, 0)),
            pl.BlockSpec((cps * CHUNK, RET_HEAD_DIM), lambda i: (rev(i), 0)),
            pl.BlockSpec((cps * CHUNK, RET_HEAD_DIM), lambda i: (rev(i), 0)),
            _resident((N_RET_HEADS, CHUNK, CHUNK)), _resident((CHUNK, RET_W)), _resident((CHUNK, RET_W)), _resident((8, RET_W)),
        ],
        out_specs=[pl.BlockSpec((cps * CHUNK, IN_W), lambda i: (rev(i), 0)), pl.BlockSpec((1, 128), lambda i: (0, 0))],
        out_shape=[jax.ShapeDtypeStruct((s, IN_W), BF16), jax.ShapeDtypeStruct((1, 128), F32)],
        scratch_shapes=[pltpu.VMEM((N_RET_HEADS, RET_HEAD_DIM, RET_HEAD_DIM), F32), pltpu.VMEM((CHUNK, 2 * KV_W), F32),
                        pltpu.VMEM((CHUNK, 128), F32)],
        args=(proj, proj, dmix, states, probs, p_sinks, sin, cos, d_intra, xi_full, zeta_full, decay_full), hosted=hosted)


def _in_proj_bwd(dproj, w_in_t, x, gain, dx1, hosted=None):
    s = x.shape[0]
    tm = min(BIG_TOKEN_TILE, s)

    def body(dp_ref, w_ref, x_ref, g_ref, dx1_ref, dx_ref, gacc_ref):
        @pl.when(pl.program_id(0) == 0)
        def _():
            gacc_ref[...] = jnp.zeros_like(gacc_ref)

        dh = _dot(dp_ref[...], w_ref[...])
        xv = x_ref[...]
        d1, dg = _rms_bwd(dh, xv, _rstd(xv), g_ref[...])
        dx_ref[...] = dx1_ref[...] + d1
        gacc_ref[0:1, :] += dg

    tok = lambda w: pl.BlockSpec((tm, w), lambda i: (i, 0))
    return _hosted_call(
        body, name="in_proj_bwd", grid=(s // tm,),
        in_specs=[tok(IN_W), _resident((IN_W, D_MODEL)), tok(D_MODEL), _resident((1, D_MODEL)), tok(D_MODEL)],
        out_specs=[tok(D_MODEL), pl.BlockSpec((8, D_MODEL), lambda i: (0, 0))],
        out_shape=[jax.ShapeDtypeStruct((s, D_MODEL), F32), jax.ShapeDtypeStruct((8, D_MODEL), F32)],
        scratch_shapes=[], args=(dproj, w_in_t, x, gain, dx1), hosted=hosted)


def _weight_grad(a, b, tn, name, by_block=False, hosted=None):
    s, m = a.shape
    n = b.shape[1]
    tk = min(WEIGHT_GRAD_TOKENS if m <= D_MODEL else WEIGHT_GRAD_TOKENS // 2, s)

    def body(a_ref, b_ref, o_ref):
        @pl.when(pl.program_id(1) == 0)
        def _():
            o_ref[...] = jnp.zeros_like(o_ref)

        o_ref[...] += _dot_tn(a_ref[...], b_ref[...])

    if by_block:
        out_spec = pl.BlockSpec((None, m, tn), lambda j, k: (j, 0, 0))
        out_shape = jax.ShapeDtypeStruct((n // tn, m, tn), F32)
    else:
        out_spec = pl.BlockSpec((m, tn), lambda j, k: (0, j))
        out_shape = jax.ShapeDtypeStruct((m, n), F32)
    (out,), lands = _hosted_call(
        body, name=name, grid=(n // tn, s // tk),
        in_specs=[pl.BlockSpec((tk, m), lambda j, k: (k, 0)), pl.BlockSpec((tk, tn), lambda j, k: (k, j))],
        out_specs=[out_spec], out_shape=[out_shape], scratch_shapes=[], args=(a, b), hosted=hosted)
    return out if hosted is None else (out, lands)


def _place():
    return lax.axis_index("x"), lax.axis_index("y"), lax.axis_index("c")


def _remote(src, dst, send_sems, recv_sems, k, to):
    return pltpu.make_async_remote_copy(src_ref=src, dst_ref=dst, send_sem=send_sems.at[k], recv_sem=recv_sems.at[k],
                                        device_id=to, device_id_type=MESH)


def _gather_level1_copies(w_refs, out_refs, send_sems, recv_sems, local_sems):
    x, y, c = _place()
    mine_at = 2 * x + y
    peers = [(x, y, 1 - c), (1 - x, y, c), (x, 1 - y, c), (1 - x, 1 - y, c)]
    local, sends, recvs = [], [], []
    for i, (w, out) in enumerate(zip(w_refs, out_refs)):
        half = w.shape[0] // 2
        src = w.at[pl.ds(pl.multiple_of(c * half, 16 if half % 16 == 0 else 8), half), :]
        mine = out.at[mine_at, c]
        local.append(pltpu.make_async_copy(src, mine, local_sems.at[i]))
        for k, p in enumerate(peers):
            sends.append(_remote(src, mine, send_sems, recv_sems, 4 * i + k, p))
            lands = out.at[mine_at, 1 - c] if k == 0 else out.at[2 * p[0] + p[1], c]
            recvs.append(_remote(src, lands, send_sems, recv_sems, 4 * i + k, p))
    return local, sends, recvs


def _gather_level1_start(w_refs, out_refs, send_sems, recv_sems, local_sems):
    local, sends, _ = _gather_level1_copies(w_refs, out_refs, send_sems, recv_sems, local_sems)
    for cp in local + sends:
        cp.start()


def _gather_level1_finish(w_refs, out_refs, send_sems, recv_sems, local_sems):
    local, sends, recvs = _gather_level1_copies(w_refs, out_refs, send_sems, recv_sems, local_sems)
    for cp in recvs:
        cp.wait_recv()
    for cp in sends:
        cp.wait_send()
    for cp in local:
        cp.wait()


def _gather_level2_copies(in_refs, out_refs, send_sems, recv_sems, local_sems):
    x, y, c = _place()
    chips = [(1 - x, y), (x, 1 - y), (1 - x, 1 - y)]
    sends, recvs = [], []
    for i, (src, out) in enumerate(zip(in_refs, out_refs)):
        for j, (px, py) in enumerate(chips):
            sends.append(_remote(src.at[2 * px + py, c], out.at[2 * px + py, c], send_sems, recv_sems, 3 * i + j, (x, y, 1 - c)))
            recvs.append(_remote(src.at[2 * px + py, c], out.at[2 * px + py, 1 - c], send_sems, recv_sems, 3 * i + j,
                                 (x, y, 1 - c)))
    return sends, recvs


def _gather_level2_start(in_refs, out_refs, send_sems, recv_sems, local_sems):
    for cp in _gather_level2_copies(in_refs, out_refs, send_sems, recv_sems, local_sems)[0]:
        cp.start()


def _gather_level2_finish(in_refs, out_refs, send_sems, recv_sems, local_sems):
    sends, recvs = _gather_level2_copies(in_refs, out_refs, send_sems, recv_sems, local_sems)
    for cp in recvs:
        cp.wait_recv()
    for cp in sends:
        cp.wait_send()


def _gathered_shape(w):
    r, cols = w.shape
    return jax.ShapeDtypeStruct((N_CHIPS, 2, r // 2, cols), w.dtype)


def _hosted_gather_level1(shards):
    n = len(shards)
    return _Hosted(shards, [_gathered_shape(w) for w in shards], {}, 4 * n, n, _gather_level1_start, _gather_level1_finish)


def _hosted_gather_level2(gathered):
    n = len(gathered)
    return _Hosted(gathered, [jax.ShapeDtypeStruct(g.shape, g.dtype) for g in gathered], {i: i for i in range(n)}, 3 * n, 0,
                   _gather_level2_start, _gather_level2_finish)


def _gather_now(shards, name, seq_len):
    n = len(shards)
    rows = min(512, seq_len)
    angle = 1.0 / jnp.power(10000.0, jnp.linspace(0.0, 1.0, RET_HEAD_DIM // 2, dtype=F32))
    sign = jnp.where(jnp.arange(RET_HEAD_DIM) % 2 == 0, -1.0, 1.0).astype(F32)
    angle_sign = jnp.concatenate([jnp.repeat(angle, 2)[None], sign[None], jnp.zeros((6, RET_HEAD_DIM), F32)], axis=0)

    def body(*refs):
        w_refs, as_ref, out_refs = list(refs[:n]), refs[n], list(refs[n + 1:2 * n + 1])
        sin_ref, cos_ref, send1, recv1, local1, send2, recv2 = refs[2 * n + 1:]
        _gather_level1_start(w_refs, out_refs, send1, recv1, local1)

        def fill(i, carry):
            r0 = pl.multiple_of(i * rows, rows)
            pos = (lax.broadcasted_iota(jnp.int32, (rows, RET_HEAD_DIM), 0) + i * rows).astype(F32)
            arg = pos * as_ref[0:1, :]
            sin_ref[pl.ds(r0, rows), :] = jnp.sin(arg) * as_ref[1:2, :]
            cos_ref[pl.ds(r0, rows), :] = jnp.cos(arg)
            return carry

        lax.fori_loop(0, seq_len // rows, fill, 0)
        _gather_level1_finish(w_refs, out_refs, send1, recv1, local1)
        _gather_level2_start(out_refs, out_refs, send2, recv2, None)
        _gather_level2_finish(out_refs, out_refs, send2, recv2, None)

    hbm, vmem = pl.BlockSpec(memory_space=pl.ANY), pl.BlockSpec(memory_space=pltpu.VMEM)
    table = jax.ShapeDtypeStruct((seq_len, RET_HEAD_DIM), F32)
    res = pl.pallas_call(
        body, name=name, out_shape=[_gathered_shape(w) for w in shards] + [table, table],
        in_specs=[hbm] * n + [vmem], out_specs=[hbm] * n + [vmem, vmem],
        scratch_shapes=[pltpu.SemaphoreType.DMA((4 * n,)), pltpu.SemaphoreType.DMA((4 * n,)), pltpu.SemaphoreType.DMA((n,)),
                        pltpu.SemaphoreType.DMA((3 * n,)), pltpu.SemaphoreType.DMA((3 * n,))],
        compiler_params=_params(),
    )(*shards, angle_sign)
    return res[:n], res[n], res[n + 1]


def _scatter_copies(g_refs, land_refs, send_sems, recv_sems, local_sems):
    x, y, c = _place()
    copies = []
    for i, (g, land) in enumerate(zip(g_refs, land_refs)):
        for k, (px, py, pc) in enumerate(_relations(x, y, c)):
            copies.append(_remote(g.at[2 * px + py, pc], land.at[k], send_sems, recv_sems, 7 * i + k, (px, py, pc)))
    return copies


def _scatter_start(g_refs, land_refs, send_sems, recv_sems, local_sems):
    for cp in _scatter_copies(g_refs, land_refs, send_sems, recv_sems, local_sems):
        cp.start()


def _scatter_finish(g_refs, land_refs, send_sems, recv_sems, local_sems):
    for cp in _scatter_copies(g_refs, land_refs, send_sems, recv_sems, local_sems):
        cp.wait()


def _hosted_scatter(grads):
    lands = [jax.ShapeDtypeStruct((N_DEV - 1,) + g.shape[2:], g.dtype) for g in grads]
    return _Hosted(grads, lands, {}, 7 * len(grads), 0, _scatter_start, _scatter_finish)


def _relations(x, y, c):
    rel = []
    for fx in (0, 1):
        for fy in (0, 1):
            for fc in (0, 1):
                if fx or fy or fc:
                    rel.append(((1 - x) if fx else x, (1 - y) if fy else y, (1 - c) if fc else c))
    return rel


def _join_halves(shards, small):
    n = len(shards)

    def body(*refs):
        in_refs, small_ref, out_refs, all_ref = refs[:n], refs[n], refs[n + 1:2 * n + 1], refs[2 * n + 1]
        send_sems, recv_sems = refs[2 * n + 2:]
        x, y, c = _place()
        slot = lambda p: all_ref.at[4 * p[0] + 2 * p[1] + p[2]]
        all_ref[4 * x + 2 * y + c] = small_ref[...]
        sends = [_remote(src.at[c], out.at[c], send_sems, recv_sems, i, (x, y, 1 - c))
                 for i, (src, out) in enumerate(zip(in_refs, out_refs))]
        recvs = [_remote(src.at[c], out.at[1 - c], send_sems, recv_sems, i, (x, y, 1 - c))
                 for i, (src, out) in enumerate(zip(in_refs, out_refs))]
        for k, p in enumerate(_relations(x, y, c)):
            sends.append(_remote(small_ref, slot((x, y, c)), send_sems, recv_sems, n + k, p))
            recvs.append(_remote(small_ref, slot(p), send_sems, recv_sems, n + k, p))
        for cp in sends:
            cp.start()
        for cp in recvs:
            cp.wait_recv()
        for cp in sends:
            cp.wait_send()

    hbm, vmem = pl.BlockSpec(memory_space=pl.ANY), pl.BlockSpec(memory_space=pltpu.VMEM)
    pairs = n + N_DEV - 1
    res = pl.pallas_call(
        body, name="grad_join_halves",
        out_shape=[jax.ShapeDtypeStruct(t.shape, t.dtype) for t in shards] + [jax.ShapeDtypeStruct((N_DEV,) + small.shape, F32)],
        in_specs=[hbm] * n + [vmem], out_specs=[hbm] * n + [vmem], input_output_aliases={i: i for i in range(n)},
        scratch_shapes=[pltpu.SemaphoreType.DMA((pairs,)), pltpu.SemaphoreType.DMA((pairs,))],
    )(*shards, small)
    return res[:n], res[n]


def _row_tile(rows, row_bytes, limit=1 << 20):
    best = 8
    for t in range(8, rows + 1, 8):
        if rows % t == 0 and t * row_bytes <= limit:
            best = t
    return best


def _sum_pieces(g, land, place, name):
    _, _, rh, cols = g.shape
    tr = _row_tile(rh, (N_DEV - 1) * cols * 4, 4 << 20)

    def body(p_ref, g_ref, l_ref, out_ref):
        acc = g_ref[...]
        for k in range(N_DEV - 1):
            acc = acc + l_ref[k].astype(F32)
        out_ref[...] = acc

    return pl.pallas_call(
        body, name=name,
        grid_spec=pltpu.PrefetchScalarGridSpec(
            num_scalar_prefetch=1, grid=(rh // tr,),
            in_specs=[pl.BlockSpec((None, None, tr, cols), lambda r, p: (p[0], p[1], r, 0)),
                      pl.BlockSpec((N_DEV - 1, tr, cols), lambda r, p: (0, r, 0))],
            out_specs=pl.BlockSpec((None, tr, cols), lambda r, p: (p[1], r, 0))),
        out_shape=jax.ShapeDtypeStruct((2, rh, cols), g.dtype),
        compiler_params=_params(("arbitrary",)),
    )(place, g, land)


def _adamw_math(w, g, m, v):
    m = ADAM_B1 * m + (1.0 - ADAM_B1) * g
    v = ADAM_B2 * v + (1.0 - ADAM_B2) * (g * g)
    m_hat = m / (1.0 - ADAM_B1 ** ADAM_STEP)
    v_hat = v / (1.0 - ADAM_B2 ** ADAM_STEP)
    delta = -ADAM_LR * (m_hat / (jnp.sqrt(v_hat) + ADAM_EPS) + ADAM_WD * w)
    return delta, m, v


def _adamw(w, g, m, v, name):
    r, cols = w.shape
    tr = _row_tile(r, cols * 4)

    def body(w_ref, g_ref, m_ref, v_ref, d_ref, nm_ref, nv_ref):
        d_ref[...], nm_ref[...], nv_ref[...] = _adamw_math(w_ref[...], g_ref[...], m_ref[...], v_ref[...])

    blk = pl.BlockSpec((tr, cols), lambda i: (i, 0))
    shape = jax.ShapeDtypeStruct((r, cols), F32)
    return pl.pallas_call(
        body, name=name, grid=(r // tr,), in_specs=[blk] * 4, out_specs=[blk] * 3, out_shape=[shape] * 3,
        compiler_params=_params(("arbitrary",)),
    )(w, g, m, v)


def _sum_devices(gathered):
    _, r, cols = gathered.shape

    def body(a_ref, g_ref):
        g = a_ref[0]
        for k in range(1, N_DEV):
            g = g + a_ref[k]
        g_ref[...] = g

    return pl.pallas_call(body, name="sum_small_grads", out_shape=jax.ShapeDtypeStruct((r, cols), F32))(gathered)


def _pack_conv(cw):
    flat = cw.reshape(-1)
    return jnp.pad(flat, (0, ROWS_CONV * D_MODEL - flat.shape[0])).reshape(ROWS_CONV, D_MODEL)


def _unpack_conv(rows):
    return rows.reshape(-1)[:CONV_WIDTH * UP_W // N_CHIPS].reshape(CONV_WIDTH, UP_W // N_CHIPS)


def _columns_to_shards(w):
    r, n = w.shape
    return jnp.transpose(w.reshape(r, N_CHIPS, n // N_CHIPS), (1, 0, 2))


def _shards_to_columns(w):
    _, r, n = w.shape
    return jnp.transpose(w, (1, 0, 2)).reshape(r, N_CHIPS * n)


def _pack_small(g_mix_pre, g_mix_post, g_ffn_pre, g_ffn_post, sinks, conv_b, loss):
    pad_row = lambda v: jnp.pad(v.reshape(1, -1), ((0, 0), (0, D_MODEL - v.size)))
    cb = jnp.pad(conv_b.reshape(-1), (0, 6 * D_MODEL - UP_W)).reshape(6, D_MODEL)
    zeros2 = jnp.zeros((2, D_MODEL), F32)
    return jnp.concatenate([g_mix_pre.reshape(1, -1), g_mix_post.reshape(1, -1), g_ffn_pre.reshape(1, -1),
                            g_ffn_post.reshape(1, -1), pad_row(sinks), pad_row(loss), zeros2, cb, zeros2], axis=0)


def _unpack_small(p):
    return dict(mix_pre_norm=p[0:1], mix_post_norm=p[1:2], ffn_pre_norm=p[2:3], ffn_post_norm=p[3:4],
                attn_sinks=p[4:5, :N_ATTN_HEADS], loss=p[5, 0], conv_b=p[8:14].reshape(1, -1)[:, :UP_W],
                conv_w=_unpack_conv(p[SMALL_ROWS:SMALL_ROWS + ROWS_CONV]))


def _local_step(x, target, g_mix_pre, w_in, sinks, w_out, g_mix_post, g_ffn_pre, w_up, conv_w, conv_b, w_down, g_ffn_post,
                distributed=True, rope=None):
    s = x.shape[0]
    consts = _ret_constants()
    sin, cos = _rope_tables(s) if rope is None else rope

    by_half = lambda g, rows: g.reshape(N_CHIPS, 2, rows // (2 * N_CHIPS), g.shape[-1])

    if distributed:
        (h1, proj), level1 = _in_proj(x, g_mix_pre, w_in, _hosted_gather_level1([w_out, w_up, w_down]))
        (mix, states, probs, p_sinks), (w_out, w_up, w_down) = _mixer_fwd(proj, sinks, sin, cos, consts,
                                                                          _hosted_gather_level2(level1))
        w_out, w_down = w_out.reshape(D_MODEL, D_MODEL), w_down.reshape(D_FF, D_MODEL)
        w_up = w_up.reshape(N_CHIPS, D_MODEL, UP_W // N_CHIPS)
    else:
        (h1, proj), _ = _in_proj(x, g_mix_pre, w_in)
        (mix, states, probs, p_sinks), _ = _mixer_fwd(proj, sinks, sin, cos, consts)
    mixed, x1, h2, u0 = _out_up_proj(mix, x, w_out, g_mix_post, g_ffn_pre, w_up)
    y, dy2, dout, du, conv_acc, tail_acc = _ffn_tail(u0, x1, target, conv_w, conv_b, w_down, g_ffn_post)
    du0, dx1, dmixed, dmix, head_acc = _ffn_head_bwd(du, conv_w, w_up, x1, g_ffn_pre, dout, mixed, g_mix_post, w_out)

    d_w_down = _weight_grad(y, dy2, 512, "grad_w_down")
    d_w_up = _weight_grad(h2, du0, UP_W // N_CHIPS, "grad_w_up", by_block=True)
    d_w_out = _weight_grad(mix, dmixed, D_MODEL, "grad_w_out")
    early = [by_half(d_w_down, D_FF), by_half(d_w_up, N_CHIPS * D_MODEL), by_half(d_w_out, D_MODEL)]
    (dproj, dsinks), early_lands = _mixer_bwd(proj, dmix, states, probs, p_sinks, sin, cos, consts,
                                              _hosted_scatter(early) if distributed else None)
    d_w_in_t = _weight_grad(dproj, h1, 512, "grad_w_in")
    late = [by_half(d_w_in_t, IN_W)]
    (grad_x, in_acc), late_lands = _in_proj_bwd(dproj, w_in, x, g_mix_pre, dx1, _hosted_scatter(late) if distributed else None)

    small = _pack_small(in_acc[0], head_acc[1], head_acc[0], tail_acc[0], dsinks[0, :N_ATTN_HEADS], conv_acc[3],
                        jnp.sum(tail_acc[1]))
    d_conv = jnp.pad(conv_acc[0:CONV_WIDTH].reshape(-1), (0, CONV_FULL_ROWS * D_MODEL - CONV_WIDTH * UP_W))
    small = jnp.concatenate([small, d_conv.reshape(CONV_FULL_ROWS, D_MODEL)], axis=0)
    grads = dict(w_down=early[0], w_up=early[1], w_out=early[2], w_in=late[0])
    lands = dict(zip(["w_down", "w_up", "w_out", "w_in"], early_lands + late_lands))
    return grad_x, grads, lands, small


def kernel(x, mix_pre_norm, w_in, attn_sinks, w_out, mix_post_norm, ffn_pre_norm, w_up, conv_w, conv_b, w_down, ffn_post_norm, loss_target, m_mix_pre_norm, m_w_in, m_attn_sinks, m_w_out, m_mix_post_norm, m_ffn_pre_norm, m_w_up, m_conv_w, m_conv_b, m_w_down, m_ffn_post_norm, v_mix_pre_norm, v_w_in, v_attn_sinks, v_w_out, v_mix_post_norm, v_ffn_pre_norm, v_w_up, v_conv_w, v_conv_b, v_w_down, v_ffn_post_norm):
    cx, cy, cc = _place()
    shard = 2 * cx + cy

    conv_rows = jnp.pad(conv_w[0], ((0, 16 - CONV_WIDTH), (0, 0)))
    w_in_t = jnp.swapaxes(w_in[0], 0, 1)
    (w_in_all, conv_all), sin, cos = _gather_now([w_in_t.astype(BF16), conv_rows], "gather_w_in", x.shape[1])
    conv_full = _shards_to_columns(conv_all[:, 0, :CONV_WIDTH])

    grad_x, grads, lands, small = _local_step(
        x[0], loss_target[0], mix_pre_norm, w_in_all.reshape(IN_W, D_MODEL), attn_sinks.reshape(-1), w_out[0].astype(BF16),
        mix_post_norm, ffn_pre_norm, w_up[0].astype(BF16), conv_full, conv_b, w_down[0].astype(BF16), ffn_post_norm,
        rope=(sin, cos))

    place = jnp.stack([shard, cc]).astype(jnp.int32)
    mats = ["w_in", "w_out", "w_up", "w_down"]
    halves = [_sum_pieces(grads[n], lands[n], place, "sum_grad_" + n) for n in mats]
    weights = dict(w_in=(w_in, m_w_in, v_w_in), w_out=(w_out, m_w_out, v_w_out), w_up=(w_up, m_w_up, v_w_up),
                   w_down=(w_down, m_w_down, v_w_down))
    mat_out = {}
    joined_all, small_all = _join_halves(halves, small)
    for n, joined in zip(mats, joined_all):
        w, m, v = (t[0] for t in weights[n])
        if n == "w_in":
            w, m, v = (jnp.swapaxes(t, 0, 1) for t in (w, m, v))
        res = (joined.reshape(w.shape),) + tuple(_adamw(w, joined.reshape(w.shape), m, v, "adamw_" + n))
        mat_out[n] = tuple(jnp.swapaxes(t, 0, 1) for t in res) if n == "w_in" else res

    small_sum = _sum_devices(small_all)
    d_conv_full = small_sum[SMALL_ROWS:].reshape(-1)[:CONV_WIDTH * UP_W].reshape(CONV_WIDTH, UP_W)
    d_conv_mine = lax.dynamic_slice_in_dim(d_conv_full, shard * (UP_W // N_CHIPS), UP_W // N_CHIPS, axis=1)
    g_s = jnp.concatenate([small_sum[:SMALL_ROWS], _pack_conv(d_conv_mine)], axis=0)
    zero = jnp.zeros((), F32)
    pack_rep = lambda a, b, c_, d, e, f, cw: jnp.concatenate([_pack_small(a, b, c_, d, e, f, zero), _pack_conv(cw[0])], axis=0)
    w_s = pack_rep(mix_pre_norm, mix_post_norm, ffn_pre_norm, ffn_post_norm, attn_sinks, conv_b, conv_w)
    m_s = pack_rep(m_mix_pre_norm, m_mix_post_norm, m_ffn_pre_norm, m_ffn_post_norm, m_attn_sinks, m_conv_b, m_conv_w)
    v_s = pack_rep(v_mix_pre_norm, v_mix_post_norm, v_ffn_pre_norm, v_ffn_post_norm, v_attn_sinks, v_conv_b, v_conv_w)
    delta_s, new_m_s, new_v_s = _adamw(w_s, g_s, m_s, v_s, "adamw_small")

    names = ["mix_pre_norm", "w_in", "attn_sinks", "w_out", "mix_post_norm", "ffn_pre_norm", "w_up", "conv_w", "conv_b",
             "w_down", "ffn_post_norm"]

    def leaves(which, packed_small):
        smalls = _unpack_small(packed_small)
        return [mat_out[n][which][None] if n in mat_out else (smalls[n][None] if n == "conv_w" else smalls[n]) for n in names]

    loss = _unpack_small(g_s)["loss"]
    return (loss, grad_x[None], *leaves(0, g_s), *leaves(1, delta_s), *leaves(2, new_m_s), *leaves(3, new_v_s))
```

```python
import math

import jax
import jax.numpy as jnp
from jax import lax
from jax.experimental import pallas as pl
from jax.experimental.pallas import tpu as pltpu

F32 = jnp.float32
BF16 = jnp.bfloat16

D_MODEL = 1024
HEAD_DIM = 64
ATTN_W = 512
N_ATTN_HEADS = 8
KV_W = 128
RET_W = 512
N_RET_HEADS = 4
RET_HEAD_DIM = 128
CHUNK = 128
IN_W = 2816
D_FF = 2816
UP_W = 2 * D_FF
CONV_WIDTH = 3
RMS_EPS = 1e-6
GN_EPS = 1e-6
MASK_VALUE = -1e30
ATTN_SCALE = HEAD_DIM ** -0.5
RET_K_SCALE = RET_HEAD_DIM ** -0.5
GELU_C = math.sqrt(2.0 / math.pi)
GELU_A = 0.044715

ADAM_LR = 0.001
ADAM_B1 = 0.9
ADAM_B2 = 0.999
ADAM_EPS = 1e-08
ADAM_WD = 0.01
ADAM_STEP = 10

N_CHIPS = 4
N_DEV = 8
MESH = pl.DeviceIdType.MESH
VMEM_LIMIT_V7X = 56 * 1024 * 1024
TOKEN_TILE = 256
BIG_TOKEN_TILE = 512
IN_PROJ_TOKEN_TILE = 1024
WEIGHT_GRAD_TOKENS = 2048
FFN_ROW_BLOCK = 64
HEAD_BWD_COLS = 512
MIXER_CHUNKS_PER_STEP = 4
Q_A0, KV_A0, Q_R0, K_R0, V_R0, G_R0 = 0, 512, 768, 1280, 1792, 2304

ROWS_CONV = 8
SMALL_ROWS = 16
CONV_FULL_ROWS = 24


def _params(sem=None, **kw):
    if sem is not None:
        kw["dimension_semantics"] = sem
    return pltpu.CompilerParams(vmem_limit_bytes=VMEM_LIMIT_V7X, **kw)


def _resident(shape):
    zeros = (0,) * len(shape)
    return pl.BlockSpec(shape, lambda *_: zeros, pipeline_mode=pl.Buffered(1))


class _Hosted:
    def __init__(self, ins, outs, aliases, n_pairs, n_local, start, finish):
        self.ins, self.outs, self.aliases = list(ins), list(outs), dict(aliases)
        self.n_pairs, self.n_local, self.start, self.finish = n_pairs, max(n_local, 1), start, finish


def _hosted_call(compute, *, name, grid, in_specs, out_specs, out_shape, scratch_shapes, args, hosted=None):
    params = _params(("arbitrary",) * len(grid))
    if hosted is None:
        res = pl.pallas_call(compute, name=name, grid=grid, in_specs=in_specs, out_specs=out_specs, out_shape=out_shape,
                             scratch_shapes=scratch_shapes, compiler_params=params)(*args)
        return list(res), []
    n_in, n_out, n_scr = len(in_specs), len(out_specs), len(scratch_shapes)
    h_in, h_out = len(hosted.ins), len(hosted.outs)

    def at(step_of):
        cond = pl.program_id(0) == step_of(grid[0])
        for d in range(1, len(grid)):
            cond = jnp.logical_and(cond, pl.program_id(d) == step_of(grid[d]))
        return cond

    def body(*refs):
        ins, refs = refs[:n_in], refs[n_in:]
        h_ins, refs = refs[:h_in], refs[h_in:]
        outs, refs = refs[:n_out], refs[n_out:]
        h_outs, refs = refs[:h_out], refs[h_out:]
        scr, sems = refs[:n_scr], refs[n_scr:]

        @pl.when(at(lambda n: 0))
        def _():
            hosted.start(h_ins, h_outs, *sems)

        compute(*ins, *outs, *scr)

        @pl.when(at(lambda n: n - 1))
        def _():
            hosted.finish(h_ins, h_outs, *sems)

    hbm = pl.BlockSpec(memory_space=pl.ANY)
    res = pl.pallas_call(
        body, name=name, grid=grid,
        in_specs=list(in_specs) + [hbm] * h_in, out_specs=list(out_specs) + [hbm] * h_out,
        out_shape=list(out_shape) + hosted.outs,
        scratch_shapes=list(scratch_shapes) + [pltpu.SemaphoreType.DMA((hosted.n_pairs,)), pltpu.SemaphoreType.DMA((hosted.n_pairs,)),
                                               pltpu.SemaphoreType.DMA((hosted.n_local,))],
        input_output_aliases={n_in + a: n_out + b for a, b in hosted.aliases.items()},
        compiler_params=params,
    )(*args, *hosted.ins)
    return list(res[:n_out]), list(res[n_out:])


def _dot(a, b):
    return jnp.dot(a, b, preferred_element_type=F32)


def _dot_nt(a, b):
    return lax.dot_general(a, b, (((1,), (1,)), ((), ())), preferred_element_type=F32)


def _dot_tn(a, b):
    return lax.dot_general(a, b, (((0,), (0,)), ((), ())), preferred_element_type=F32)


def _shift_matrix(n, by):
    row = lax.broadcasted_iota(jnp.int32, (n, n), 0)
    col = lax.broadcasted_iota(jnp.int32, (n, n), 1)
    return jnp.where(col == row + by, 1.0, 0.0).astype(BF16)


def _rstd(v):
    return lax.rsqrt(jnp.mean(v * v, axis=-1, keepdims=True) + RMS_EPS)


def _rms_bwd(dy, v, rstd, gain):
    n = v * rstd
    dgain = jnp.sum(dy * n, axis=0, keepdims=True)
    dn = dy * gain
    dv = rstd * (dn - n * jnp.mean(dn * n, axis=-1, keepdims=True))
    return dv, dgain


def _lane_lo(shape):
    return (lax.broadcasted_iota(jnp.int32, shape, 1) % 128) < HEAD_DIM


GROUP = N_ATTN_HEADS // (KV_W // HEAD_DIM)


def _attn_bias(first_chunk):
    qi = lax.broadcasted_iota(jnp.int32, (GROUP * CHUNK, 2 * CHUNK), 0) % CHUNK
    kj = lax.broadcasted_iota(jnp.int32, (GROUP * CHUNK, 2 * CHUNK), 1)
    valid = jnp.logical_and(kj > qi, kj <= qi + CHUNK)
    if first_chunk:
        valid = jnp.logical_and(valid, kj >= CHUNK)
    return jnp.where(valid, 0.0, MASK_VALUE)


def _half(shape, hk):
    lo = _lane_lo(shape)
    return lo if hk == 0 else jnp.logical_not(lo)


class _GroupMasks:
    def __init__(self, sk_ref):
        groups = range(KV_W // HEAD_DIM)
        self.q = [_half((CHUNK, 128), hk) for hk in groups]
        self.kv = [_half((2 * CHUNK, 128), hk) for hk in groups]
        self.sinks = [_group_sinks(sk_ref, hk) for hk in groups]


def _stack_heads(ref, row0, col0, hk, half):
    parts = []
    for j in range(GROUP):
        h = GROUP * hk + j
        pair = ref[row0:row0 + CHUNK, col0 + (h // 2) * 128:col0 + (h // 2 + 1) * 128].astype(F32)
        if h % 2 != hk:
            pair = pltpu.roll(pair, HEAD_DIM, 1)
        parts.append(jnp.where(half, pair, 0.0))
    return jnp.concatenate(parts, axis=0)


def _unstack_heads(stacked, hk):
    pairs = []
    for q in range(GROUP // 2):
        even, odd = stacked[2 * q * CHUNK:(2 * q + 1) * CHUNK], stacked[(2 * q + 1) * CHUNK:(2 * q + 2) * CHUNK]
        pairs.append(even + pltpu.roll(odd, HEAD_DIM, 1) if hk == 0 else pltpu.roll(even, HEAD_DIM, 1) + odd)
    return pairs


def _group_sinks(sk_ref, hk):
    row = lax.broadcasted_iota(jnp.int32, (GROUP * CHUNK, 1), 0)
    col = jnp.full((GROUP * CHUNK, 1), sk_ref[GROUP * hk], F32)
    for j in range(1, GROUP):
        col = jnp.where(row >= j * CHUNK, sk_ref[GROUP * hk + j], col)
    return col


def _attn_probs(q_b, kk_b, bias, sink):
    s = _dot_nt(q_b, kk_b) * ATTN_SCALE + bias
    m = jnp.maximum(jnp.max(s, axis=-1, keepdims=True), sink)
    e = jnp.exp(s - m)
    e_sink = jnp.exp(sink - m)
    inv = 1.0 / (jnp.sum(e, axis=-1, keepdims=True) + e_sink)
    return e * inv, e_sink * inv


def _even_lanes(shape):
    return (lax.broadcasted_iota(jnp.int32, shape, 1) % 2) == 0


def _swap2(v, even):
    return jnp.where(even, pltpu.roll(v, v.shape[1] - 1, 1), pltpu.roll(v, 1, 1))


def _tile4(v):
    return jnp.concatenate([v, v, v, v], axis=-1)


def _sigmoid(v):
    return 1.0 / (1.0 + jnp.exp(-v))


def _ret_constants():
    h = N_RET_HEADS
    log_gamma = jnp.log(1.0 - jnp.power(2.0, -5.0 - jnp.arange(h, dtype=F32)))
    idx = jnp.arange(CHUNK, dtype=F32)
    rel = idx[:, None] - idx[None, :]
    d_intra = jnp.where(rel[None] >= 0, jnp.exp(log_gamma[:, None, None] * jnp.maximum(rel, 0.0)[None]), 0.0)
    xi = jnp.exp(log_gamma[None, :] * (idx[:, None] + 1.0))
    zeta = jnp.exp(log_gamma[None, :] * (CHUNK - 1.0 - idx[:, None]))
    decay = jnp.exp(log_gamma * CHUNK)
    xi_full = jnp.repeat(xi, RET_HEAD_DIM, axis=1)
    zeta_full = jnp.repeat(zeta, RET_HEAD_DIM, axis=1)
    decay_full = jnp.broadcast_to(jnp.repeat(decay, RET_HEAD_DIM)[None, :], (8, RET_W))
    return d_intra.astype(F32), xi_full.astype(F32), zeta_full.astype(F32), decay_full.astype(F32)


def _rope_tables(s):
    pos = jnp.arange(s, dtype=F32)
    angle = 1.0 / jnp.power(10000.0, jnp.linspace(0.0, 1.0, RET_HEAD_DIM // 2, dtype=F32))
    angle = jnp.repeat(angle, 2)
    sign = jnp.where(jnp.arange(RET_HEAD_DIM) % 2 == 0, -1.0, 1.0).astype(F32)
    return jnp.sin(pos[:, None] * angle[None]) * sign[None], jnp.cos(pos[:, None] * angle[None])


def _in_proj(x, gain, w_in_t, hosted=None):
    s = x.shape[0]
    tm = min(IN_PROJ_TOKEN_TILE, s)

    def body(x_ref, g_ref, w_ref, h_ref, p_ref):
        xv = x_ref[...]
        h = (xv * _rstd(xv) * g_ref[...]).astype(BF16)
        h_ref[...] = h
        p_ref[...] = _dot_nt(h, w_ref[...])

    return _hosted_call(
        body, name="in_proj", grid=(s // tm,),
        in_specs=[pl.BlockSpec((tm, D_MODEL), lambda i: (i, 0)), _resident((1, D_MODEL)), _resident((IN_W, D_MODEL))],
        out_specs=[pl.BlockSpec((tm, D_MODEL), lambda i: (i, 0)), pl.BlockSpec((tm, IN_W), lambda i: (i, 0))],
        out_shape=[jax.ShapeDtypeStruct((s, D_MODEL), BF16), jax.ShapeDtypeStruct((s, IN_W), F32)],
        scratch_shapes=[], args=(x, gain, w_in_t), hosted=hosted)


def _mixer_fwd(proj, sinks, sin, cos, consts, hosted=None):
    s = proj.shape[0]
    nc = s // CHUNK
    cps = MIXER_CHUNKS_PER_STEP
    groups = KV_W // HEAD_DIM
    d_intra, xi_full, zeta_full, decay_full = consts

    def body(sk_ref, p_ref, pkv_ref, sin_ref, cos_ref, dm_ref, xi_ref, ze_ref, dc_ref,
             mix_ref, st_ref, pr_ref, ps_ref, ra_ref, on_ref, rs_ref, state):
        i = pl.program_id(0)

        @pl.when(i == 0)
        def _():
            state[...] = jnp.zeros_like(state)

        st = [state[h] for h in range(N_RET_HEADS)]
        bias_any = _attn_bias(False)
        bias_c0 = jnp.where(i == 0, _attn_bias(True), bias_any)
        even = _even_lanes((CHUNK, RET_W))
        masks = _GroupMasks(sk_ref)
        for c in range(cps):
            r0 = c * CHUNK
            rows = slice(r0, r0 + CHUNK)

            kv_cur = p_ref[rows, KV_A0:KV_A0 + 2 * KV_W]
            kv_prev = pkv_ref[...] if c == 0 else p_ref[r0 - CHUNK:r0, KV_A0:KV_A0 + 2 * KV_W]
            kk = jnp.concatenate([kv_prev[:, :KV_W], kv_cur[:, :KV_W]], axis=0)
            vv = jnp.concatenate([kv_prev[:, KV_W:], kv_cur[:, KV_W:]], axis=0)
            kk_b = kk.astype(BF16)
            bias = bias_c0 if c == 0 else bias_any
            for hk in range(KV_W // HEAD_DIM):
                q_b = _stack_heads(p_ref, r0, Q_A0, hk, masks.q[hk]).astype(BF16)
                p, p_sink = _attn_probs(q_b, kk_b, bias, masks.sinks[hk])
                p_b = p.astype(BF16)
                pr_ref[c, hk] = p_b
                ps_ref[c, hk] = p_sink
                v_b = jnp.where(masks.kv[hk], vv, 0.0).astype(BF16)
                for q, pair in enumerate(_unstack_heads(_dot(p_b, v_b), hk)):
                    pi = (GROUP // 2) * hk + q
                    mix_ref[rows, pi * 128:(pi + 1) * 128] = pair.astype(BF16)

            sin4, cos4 = _tile4(sin_ref[rows, :]), _tile4(cos_ref[rows, :])
            q_r = p_ref[rows, Q_R0:Q_R0 + RET_W]
            k_r = p_ref[rows, K_R0:K_R0 + RET_W] * RET_K_SCALE
            q_r = q_r * cos4 + _swap2(q_r, even) * sin4
            k_r = k_r * cos4 + _swap2(k_r, even) * sin4
            kz = k_r * ze_ref[...]
            for h in range(N_RET_HEADS):
                sl = slice(h * RET_HEAD_DIM, (h + 1) * RET_HEAD_DIM)
                qh, kh = q_r[:, sl].astype(BF16), k_r[:, sl].astype(BF16)
                vh = p_ref[rows, V_R0 + h * RET_HEAD_DIM:V_R0 + (h + 1) * RET_HEAD_DIM].astype(BF16)
                st_ref[c, h] = st[h]
                a_b = (_dot_nt(qh, kh) * dm_ref[h]).astype(BF16)
                qx = (q_r[:, sl] * xi_ref[:, sl]).astype(BF16)
                o = _dot(jnp.concatenate([a_b, qx], axis=1), jnp.concatenate([vh, st[h].astype(BF16)], axis=0))
                st[h] = dc_ref[0:1, sl] * st[h] + _dot_tn(kz[:, sl].astype(BF16), vh)
                mu = jnp.mean(o, axis=-1, keepdims=True)
                oc = o - mu
                rs = lax.rsqrt(jnp.mean(oc * oc, axis=-1, keepdims=True) + GN_EPS)
                on = oc * rs
                ra_ref[c, h], on_ref[rows, sl], rs_ref[c, h] = a_b, on, rs
                g = p_ref[rows, G_R0 + h * RET_HEAD_DIM:G_R0 + (h + 1) * RET_HEAD_DIM]
                mix_ref[rows, ATTN_W + h * RET_HEAD_DIM:ATTN_W + (h + 1) * RET_HEAD_DIM] = (g * _sigmoid(g) * on).astype(BF16)
        for h in range(N_RET_HEADS):
            state[h] = st[h]

    return _hosted_call(
        body, name="mixer_fwd", grid=(nc // cps,),
        in_specs=[
            pl.BlockSpec(memory_space=pltpu.SMEM),
            pl.BlockSpec((cps * CHUNK, IN_W), lambda i: (i, 0)),
            pl.BlockSpec((CHUNK, 2 * KV_W), lambda i: (jnp.maximum(cps * i - 1, 0), KV_A0 // (2 * KV_W))),
            pl.BlockSpec((cps * CHUNK, RET_HEAD_DIM), lambda i: (i, 0)),
            pl.BlockSpec((cps * CHUNK, RET_HEAD_DIM), lambda i: (i, 0)),
            _resident((N_RET_HEADS, CHUNK, CHUNK)), _resident((CHUNK, RET_W)), _resident((CHUNK, RET_W)), _resident((8, RET_W)),
        ],
        out_specs=[
            pl.BlockSpec((cps * CHUNK, D_MODEL), lambda i: (i, 0)),
            pl.BlockSpec((cps, N_RET_HEADS, RET_HEAD_DIM, RET_HEAD_DIM), lambda i: (i, 0, 0, 0)),
            pl.BlockSpec((cps, groups, GROUP * CHUNK, 2 * CHUNK), lambda i: (i, 0, 0, 0)),
            pl.BlockSpec((cps, groups, GROUP * CHUNK, 1), lambda i: (i, 0, 0, 0)),
            pl.BlockSpec((cps, N_RET_HEADS, CHUNK, CHUNK), lambda i: (i, 0, 0, 0)),
            pl.BlockSpec((cps * CHUNK, RET_W), lambda i: (i, 0)),
            pl.BlockSpec((cps, N_RET_HEADS, CHUNK, 1), lambda i: (i, 0, 0, 0)),
        ],
        out_shape=[jax.ShapeDtypeStruct((s, D_MODEL), BF16),
                   jax.ShapeDtypeStruct((nc, N_RET_HEADS, RET_HEAD_DIM, RET_HEAD_DIM), F32),
                   jax.ShapeDtypeStruct((nc, groups, GROUP * CHUNK, 2 * CHUNK), BF16),
                   jax.ShapeDtypeStruct((nc, groups, GROUP * CHUNK, 1), F32),
                   jax.ShapeDtypeStruct((nc, N_RET_HEADS, CHUNK, CHUNK), BF16),
                   jax.ShapeDtypeStruct((s, RET_W), F32),
                   jax.ShapeDtypeStruct((nc, N_RET_HEADS, CHUNK, 1), F32)],
        scratch_shapes=[pltpu.VMEM((N_RET_HEADS, RET_HEAD_DIM, RET_HEAD_DIM), F32)],
        args=(sinks, proj, proj, sin, cos, d_intra, xi_full, zeta_full, decay_full), hosted=hosted)


def _out_up_proj(mix, x, w_out, g_post, g_pre, w_up):
    s = x.shape[0]
    tm = min(BIG_TOKEN_TILE, s)
    blk = UP_W // N_CHIPS

    def body(mix_ref, x_ref, wo_ref, g2_ref, g3_ref, wu_ref, mixed_ref, x1_ref, h2_ref, u0_ref):
        mixed = _dot(mix_ref[...], wo_ref[...])
        mixed_ref[...] = mixed
        x1 = x_ref[...] + mixed * _rstd(mixed) * g2_ref[...]
        x1_ref[...] = x1
        h2 = (x1 * _rstd(x1) * g3_ref[...]).astype(BF16)
        h2_ref[...] = h2
        for k in range(N_CHIPS):
            u0_ref[:, k * blk:(k + 1) * blk] = _dot(h2, wu_ref[k]).astype(BF16)

    tok = lambda w: pl.BlockSpec((tm, w), lambda i: (i, 0))
    return pl.pallas_call(
        body, name="out_up_proj", grid=(s // tm,),
        in_specs=[tok(D_MODEL), tok(D_MODEL), _resident((D_MODEL, D_MODEL)), _resident((1, D_MODEL)), _resident((1, D_MODEL)),
                  _resident((N_CHIPS, D_MODEL, blk))],
        out_specs=[tok(D_MODEL), tok(D_MODEL), tok(D_MODEL), tok(UP_W)],
        out_shape=[jax.ShapeDtypeStruct((s, D_MODEL), F32), jax.ShapeDtypeStruct((s, D_MODEL), F32),
                   jax.ShapeDtypeStruct((s, D_MODEL), BF16), jax.ShapeDtypeStruct((s, UP_W), BF16)],
        compiler_params=_params(("arbitrary",)),
    )(mix, x, w_out, g_post, g_pre, w_up)


def _ffn_tail(u0, x1, target, conv_w, conv_b, w_down, g_post):
    s = x1.shape[0]
    tm = TOKEN_TILE
    last = s // tm - 1
    rb, lanes = FFN_ROW_BLOCK, 128

    def body(u0_ref, x1_ref, t_ref, cw_ref, cb_ref, wd_ref, g_ref,
             y_ref, dy2_ref, dout_ref, du_ref, cacc_ref, gacc_ref, u1_s, u2_s, carry, gelu_s, slope_s, dy_s, cacc):
        i = pl.program_id(0)

        @pl.when(i == 0)
        def _():
            carry[...] = jnp.zeros_like(carry)
            cacc[...] = jnp.zeros_like(cacc)
            gacc_ref[...] = jnp.zeros_like(gacc_ref)

        shift1, shift2 = _shift_matrix(tm, -1), _shift_matrix(tm, -2)
        r8 = lax.broadcasted_iota(jnp.int32, (8, 1), 0)
        wide = 2 * lanes

        def shift_block(col):
            cols = slice(col, col + wide)
            u1_s[:, cols] = _dot(shift1, u0_ref[:, cols])
            u2_s[:, cols] = _dot(shift2, u0_ref[:, cols])
            c14, c15 = carry[14:15, cols], carry[15:16, cols]
            u1_s[0:8, cols] = jnp.where(r8 == 0, c15, u1_s[0:8, cols])
            u2_s[0:8, cols] = jnp.where(r8 == 0, c14, jnp.where(r8 == 1, c15, u2_s[0:8, cols]))

        def taps(col):
            return (cw_ref[0:1, col:col + lanes], cw_ref[1:2, col:col + lanes], cw_ref[2:3, col:col + lanes],
                    cb_ref[0:1, col:col + lanes])

        def shifted(r0, col):
            return (u2_s[r0:r0 + rb, col:col + lanes], u1_s[r0:r0 + rb, col:col + lanes],
                    u0_ref[r0:r0 + rb, col:col + lanes].astype(F32))

        def conv(r0, col, w):
            u2, u1, uc = shifted(r0, col)
            return w[0] * u2 + w[1] * u1 + w[2] * uc + w[3]

        fold = lambda v: jnp.sum(v.reshape(rb // 8, 8, lanes), axis=0)

        shift_block(0)
        shift_block(D_FF)
        for j in range(D_FF // lanes):
            cg, cv = j * lanes, D_FF + j * lanes
            if cg % wide == 0 and cg + wide < D_FF:
                shift_block(cg + wide)
                shift_block(cv + wide)
            wg, wv = taps(cg), taps(cv)
            for r0 in range(0, tm, rb):
                gate, val = conv(r0, cg, wg), conv(r0, cv, wv)
                g2 = gate * gate
                th = jnp.tanh(gate * (GELU_C + GELU_C * GELU_A * g2))
                hp = 0.5 * th + 0.5
                gelu = gate * hp
                dgelu = hp + gate * (1.0 - th * th) * (0.5 * GELU_C + 1.5 * GELU_C * GELU_A * g2)
                y_ref[r0:r0 + rb, cg:cg + lanes] = (gelu * val).astype(BF16)
                gelu_s[r0:r0 + rb, cg:cg + lanes] = gelu
                slope_s[r0:r0 + rb, cg:cg + lanes] = dgelu * val

        y2 = _dot(y_ref[...], wd_ref[...])
        r4 = _rstd(y2)
        gain = g_ref[...]
        out = x1_ref[...] + y2 * r4 * gain
        diff = out - t_ref[...]
        dout = diff * (1.0 / D_MODEL)
        dout_ref[...] = dout
        dy2, dgain = _rms_bwd(dout, y2, r4, gain)
        dy2_b = dy2.astype(BF16)
        dy2_ref[...] = dy2_b
        gacc_ref[0:1, :] += dgain
        gacc_ref[1:2, :] += 0.5 * jnp.sum(diff * dout, axis=0, keepdims=True)
        carry[...] = u0_ref[tm - 16:tm, :].astype(F32)

        dy_s[:, 0:wide] = _dot_nt(dy2_b, wd_ref[0:wide, :])
        for j in range(D_FF // lanes):
            cg, cv = j * lanes, D_FF + j * lanes
            if cg % wide == 0 and cg + wide < D_FF:
                dy_s[:, cg + wide:cg + 2 * wide] = _dot_nt(dy2_b, wd_ref[cg + wide:cg + 2 * wide, :])
            acc = [[jnp.zeros((8, lanes), F32) for _ in range(CONV_WIDTH + 1)] for _ in range(2)]
            for r0 in range(0, tm, rb):
                dy = dy_s[r0:r0 + rb, cg:cg + lanes]
                d_gate = dy * slope_s[r0:r0 + rb, cg:cg + lanes]
                d_val = dy * gelu_s[r0:r0 + rb, cg:cg + lanes]
                for side, (col, d) in enumerate(((cg, d_gate), (cv, d_val))):
                    du_ref[r0:r0 + rb, col:col + lanes] = d.astype(BF16)
                    for k, u in enumerate(shifted(r0, col)):
                        acc[side][k] = acc[side][k] + fold(d * u)
                    acc[side][CONV_WIDTH] = acc[side][CONV_WIDTH] + fold(d)
            for side, col in enumerate((cg, cv)):
                for k in range(CONV_WIDTH + 1):
                    cacc[8 * k:8 * k + 8, col:col + lanes] += acc[side][k]

        @pl.when(i == last)
        def _():
            for k in range(CONV_WIDTH + 1):
                cacc_ref[k:k + 1, :] = jnp.sum(cacc[8 * k:8 * k + 8, :], axis=0, keepdims=True)
            cacc_ref[CONV_WIDTH + 1:8, :] = jnp.zeros((8 - CONV_WIDTH - 1, UP_W), F32)

    tok = lambda w: pl.BlockSpec((tm, w), lambda i: (i, 0))
    return pl.pallas_call(
        body, name="ffn_tail", grid=(s // tm,),
        in_specs=[tok(UP_W), tok(D_MODEL), tok(D_MODEL), _resident((CONV_WIDTH, UP_W)), _resident((1, UP_W)),
                  _resident((D_FF, D_MODEL)), _resident((1, D_MODEL))],
        out_specs=[tok(D_FF), tok(D_MODEL), tok(D_MODEL), tok(UP_W),
                   pl.BlockSpec((8, UP_W), lambda i: (0, 0)), pl.BlockSpec((8, D_MODEL), lambda i: (0, 0))],
        out_shape=[jax.ShapeDtypeStruct((s, D_FF), BF16), jax.ShapeDtypeStruct((s, D_MODEL), BF16),
                   jax.ShapeDtypeStruct((s, D_MODEL), F32), jax.ShapeDtypeStruct((s, UP_W), BF16),
                   jax.ShapeDtypeStruct((8, UP_W), F32), jax.ShapeDtypeStruct((8, D_MODEL), F32)],
        scratch_shapes=[pltpu.VMEM((tm, UP_W), F32), pltpu.VMEM((tm, UP_W), F32), pltpu.VMEM((16, UP_W), F32),
                        pltpu.VMEM((tm, D_FF), F32), pltpu.VMEM((tm, D_FF), F32),
                        pltpu.VMEM((tm, D_FF), F32), pltpu.VMEM((8 * (CONV_WIDTH + 1), UP_W), F32)],
        compiler_params=_params(("arbitrary",)),
    )(u0, x1, target, conv_w, conv_b, w_down, g_post)


def _ffn_head_bwd(du, conv_w, w_up, x1, g_pre, dout, mixed, g_post, w_out):
    s = x1.shape[0]
    tm = TOKEN_TILE
    nt = s // tm
    blk = UP_W // N_CHIPS

    def body(du_ref, halo_ref, cw_ref, wu_ref, x1_ref, g3_ref, dout_ref, mixed_ref, g2_ref, wo_ref,
             du0_ref, dx1_ref, dmixed_ref, dmix_ref, gacc_ref, dbuf):
        i = pl.program_id(0)

        @pl.when(i == 0)
        def _():
            gacc_ref[...] = jnp.zeros_like(gacc_ref)

        dbuf[0:tm, :] = du_ref[...].astype(F32)
        dbuf[tm:tm + 16, :] = jnp.where(i < nt - 1, halo_ref[...].astype(F32), 0.0)
        dh2 = jnp.zeros((tm, D_MODEL), F32)
        for k in range(N_CHIPS):
            for c0 in range(0, blk, HEAD_BWD_COLS):
                width = min(HEAD_BWD_COLS, blk - c0)
                cols = slice(k * blk + c0, k * blk + c0 + width)
                du0_b = (cw_ref[2:3, cols] * dbuf[0:tm, cols] + cw_ref[1:2, cols] * dbuf[1:1 + tm, cols]
                         + cw_ref[0:1, cols] * dbuf[2:2 + tm, cols]).astype(BF16)
                du0_ref[:, cols] = du0_b
                dh2 = dh2 + _dot_nt(du0_b, wu_ref[k, :, c0:c0 + width])
        x1 = x1_ref[...]
        d3, dg3 = _rms_bwd(dh2, x1, _rstd(x1), g3_ref[...])
        dx1 = dout_ref[...] + d3
        dx1_ref[...] = dx1
        mixed = mixed_ref[...]
        dmixed, dg2 = _rms_bwd(dx1, mixed, _rstd(mixed), g2_ref[...])
        dmixed_b = dmixed.astype(BF16)
        dmixed_ref[...] = dmixed_b
        dmix_ref[...] = _dot_nt(dmixed_b, wo_ref[...]).astype(BF16)
        gacc_ref[0:1, :] += dg3
        gacc_ref[1:2, :] += dg2

    tok = lambda w: pl.BlockSpec((tm, w), lambda i: (i, 0))
    halo = pl.BlockSpec((16, UP_W), lambda i: (jnp.minimum(i + 1, nt - 1) * (tm // 16), 0))
    return pl.pallas_call(
        body, name="ffn_head_bwd", grid=(nt,),
        in_specs=[tok(UP_W), halo, _resident((CONV_WIDTH, UP_W)), _resident((N_CHIPS, D_MODEL, blk)), tok(D_MODEL),
                  _resident((1, D_MODEL)), tok(D_MODEL), tok(D_MODEL), _resident((1, D_MODEL)), _resident((D_MODEL, D_MODEL))],
        out_specs=[tok(UP_W), tok(D_MODEL), tok(D_MODEL), tok(D_MODEL), pl.BlockSpec((8, D_MODEL), lambda i: (0, 0))],
        out_shape=[jax.ShapeDtypeStruct((s, UP_W), BF16), jax.ShapeDtypeStruct((s, D_MODEL), F32),
                   jax.ShapeDtypeStruct((s, D_MODEL), BF16), jax.ShapeDtypeStruct((s, D_MODEL), BF16),
                   jax.ShapeDtypeStruct((8, D_MODEL), F32)],
        scratch_shapes=[pltpu.VMEM((tm + 16, UP_W), F32)],
        compiler_params=_params(("arbitrary",)),
    )(du, du, conv_w, w_up, x1, g_pre, dout, mixed, g_post, w_out)


def _mixer_bwd(proj, dmix, states, kept, sin, cos, consts, hosted=None):
    probs, p_sinks, ret_scores, ret_normed, ret_rstd = kept
    s = proj.shape[0]
    nc = s // CHUNK
    cps = MIXER_CHUNKS_PER_STEP
    nb = nc // cps
    groups = KV_W // HEAD_DIM
    d_intra, xi_full, zeta_full, decay_full = consts

    def body(p_ref, pkv_ref, dmix_ref, st_ref, pr_ref, ps_ref, ra_ref, on_ref, rs_ref, sin_ref, cos_ref, dm_ref, xi_ref, ze_ref,
             dc_ref, dp_ref, dsk_ref, gstate, ckv, dsk_acc):
        i = pl.program_id(0)
        block = nb - 1 - i

        @pl.when(i == 0)
        def _():
            gstate[...] = jnp.zeros_like(gstate)
            ckv[...] = jnp.zeros_like(ckv)
            dsk_acc[...] = jnp.zeros_like(dsk_acc)

        gs_all = [gstate[h] for h in range(N_RET_HEADS)]
        later_kv = ckv[...]
        lane = lax.broadcasted_iota(jnp.int32, (CHUNK, 128), 1)
        dsk = jnp.zeros((CHUNK, 128), F32)
        even = _even_lanes((CHUNK, RET_W))
        half_q = [_half((CHUNK, 128), hk) for hk in range(groups)]
        half_kv = [_half((2 * CHUNK, 128), hk) for hk in range(groups)]
        for c in reversed(range(cps)):
            r0 = c * CHUNK
            rows = slice(r0, r0 + CHUNK)

            kv_cur = p_ref[rows, KV_A0:KV_A0 + 2 * KV_W]
            kv_prev = pkv_ref[...] if c == 0 else p_ref[r0 - CHUNK:r0, KV_A0:KV_A0 + 2 * KV_W]
            kk = jnp.concatenate([kv_prev[:, :KV_W], kv_cur[:, :KV_W]], axis=0)
            vv = jnp.concatenate([kv_prev[:, KV_W:], kv_cur[:, KV_W:]], axis=0)
            vv_b = vv.astype(BF16)
            dkk = jnp.zeros((2 * CHUNK, KV_W), F32)
            dvv = jnp.zeros((2 * CHUNK, KV_W), F32)
            for hk in range(groups):
                q_b = _stack_heads(p_ref, r0, Q_A0, hk, half_q[hk]).astype(BF16)
                do_b = _stack_heads(dmix_ref, r0, 0, hk, half_q[hk]).astype(BF16)
                p_b = pr_ref[c, hk]
                p = p_b.astype(F32)
                dpr = _dot_nt(do_b, vv_b)
                delta = jnp.sum(p * dpr, axis=-1, keepdims=True)
                ds_b = (p * (dpr - delta) * ATTN_SCALE).astype(BF16)
                dsink = -ps_ref[c, hk] * delta
                for j in range(GROUP):
                    dsk = dsk + jnp.where(lane == GROUP * hk + j, dsink[j * CHUNK:(j + 1) * CHUNK], 0.0)
                k_b = jnp.where(half_kv[hk], kk, 0.0).astype(BF16)
                for q, pair in enumerate(_unstack_heads(_dot(ds_b, k_b), hk)):
                    pi = (GROUP // 2) * hk + q
                    dp_ref[rows, Q_A0 + pi * 128:Q_A0 + (pi + 1) * 128] = pair.astype(BF16)
                dkk = dkk + _dot_tn(ds_b, q_b)
                dvv = dvv + _dot_tn(p_b, do_b)
            dp_ref[rows, KV_A0:KV_A0 + KV_W] = (dkk[CHUNK:] + later_kv[:, :KV_W]).astype(BF16)
            dp_ref[rows, KV_A0 + KV_W:KV_A0 + 2 * KV_W] = (dvv[CHUNK:] + later_kv[:, KV_W:]).astype(BF16)
            later_kv = jnp.concatenate([dkk[:CHUNK], dvv[:CHUNK]], axis=1)

            sin4, cos4 = _tile4(sin_ref[rows, :]), _tile4(cos_ref[rows, :])
            q_r = p_ref[rows, Q_R0:Q_R0 + RET_W]
            k_r = p_ref[rows, K_R0:K_R0 + RET_W] * RET_K_SCALE
            q_r = q_r * cos4 + _swap2(q_r, even) * sin4
            k_r = k_r * cos4 + _swap2(k_r, even) * sin4
            kz = k_r * ze_ref[...]
            dq_parts, dk_parts = [], []
            for h in range(N_RET_HEADS):
                sl = slice(h * RET_HEAD_DIM, (h + 1) * RET_HEAD_DIM)
                qh, kh = q_r[:, sl].astype(BF16), k_r[:, sl].astype(BF16)
                vh = p_ref[rows, V_R0 + h * RET_HEAD_DIM:V_R0 + (h + 1) * RET_HEAD_DIM].astype(BF16)
                st_b = st_ref[c, h].astype(BF16)
                gs = gs_all[h]
                gs_b = gs.astype(BF16)
                xi_h = xi_ref[:, sl]
                dm = dm_ref[h]
                a_b, on, rs = ra_ref[c, h], on_ref[rows, sl], rs_ref[c, h]
                g = p_ref[rows, G_R0 + h * RET_HEAD_DIM:G_R0 + (h + 1) * RET_HEAD_DIM]
                sg = _sigmoid(g)
                dr = dmix_ref[rows, ATTN_W + h * RET_HEAD_DIM:ATTN_W + (h + 1) * RET_HEAD_DIM].astype(F32)
                dp_ref[rows, G_R0 + h * RET_HEAD_DIM:G_R0 + (h + 1) * RET_HEAD_DIM] = (
                    dr * on * (sg * (1.0 + g * (1.0 - sg)))).astype(BF16)
                don = dr * g * sg
                do = rs * (don - jnp.mean(don, axis=-1, keepdims=True) - on * jnp.mean(don * on, axis=-1, keepdims=True))
                do_b = do.astype(BF16)
                dox_b = (do * xi_h).astype(BF16)
                da_b = (_dot_nt(do_b, vh) * dm).astype(BF16)
                dq_parts.append(_dot(da_b, kh) + _dot_nt(dox_b, st_b))
                dk_parts.append(_dot_tn(da_b, qh) + ze_ref[:, sl] * _dot_nt(vh, gs_b))
                dv = _dot_tn(a_b, do_b) + _dot(kz[:, sl].astype(BF16), gs_b)
                dp_ref[rows, V_R0 + h * RET_HEAD_DIM:V_R0 + (h + 1) * RET_HEAD_DIM] = dv.astype(BF16)
                gs_all[h] = dc_ref[0:1, sl] * gs + _dot_tn(qh, dox_b)
            dq = jnp.concatenate(dq_parts, axis=-1)
            dk = jnp.concatenate(dk_parts, axis=-1)
            dp_ref[rows, Q_R0:Q_R0 + RET_W] = (dq * cos4 - _swap2(dq, even) * sin4).astype(BF16)
            dp_ref[rows, K_R0:K_R0 + RET_W] = (RET_K_SCALE * (dk * cos4 - _swap2(dk, even) * sin4)).astype(BF16)

        for h in range(N_RET_HEADS):
            gstate[h] = gs_all[h]
        ckv[...] = later_kv
        dsk_acc[...] += dsk

        @pl.when(i == nb - 1)
        def _():
            dsk_ref[...] = jnp.sum(dsk_acc[...], axis=0, keepdims=True)

    rev = lambda i: nb - 1 - i
    return _hosted_call(
        body, name="mixer_bwd", grid=(nb,),
        in_specs=[
            pl.BlockSpec((cps * CHUNK, IN_W), lambda i: (rev(i), 0)),
            pl.BlockSpec((CHUNK, 2 * KV_W), lambda i: (jnp.maximum(cps * rev(i) - 1, 0), KV_A0 // (2 * KV_W))),
            pl.BlockSpec((cps * CHUNK, D_MODEL), lambda i: (rev(i), 0)),
            pl.BlockSpec((cps, N_RET_HEADS, RET_HEAD_DIM, RET_HEAD_DIM), lambda i: (rev(i), 0, 0, 0)),
            pl.BlockSpec((cps, groups, GROUP * CHUNK, 2 * CHUNK), lambda i: (rev(i), 0, 0, 0)),
            pl.BlockSpec((cps, groups, GROUP * CHUNK, 1), lambda i: (rev(i), 0, 0, 0)),
            pl.BlockSpec((cps, N_RET_HEADS, CHUNK, CHUNK), lambda i: (rev(i), 0, 0, 0)),
            pl.BlockSpec((cps * CHUNK, RET_W), lambda i: (rev(i), 0)),
            pl.BlockSpec((cps, N_RET_HEADS, CHUNK, 1), lambda i: (rev(i), 0, 0, 0)),
            pl.BlockSpec((cps * CHUNK, RET_HEAD_DIM), lambda i: (rev(i), 0)),
            pl.BlockSpec((cps * CHUNK, RET_HEAD_DIM), lambda i: (rev(i), 0)),
            _resident((N_RET_HEADS, CHUNK, CHUNK)), _resident((CHUNK, RET_W)), _resident((CHUNK, RET_W)), _resident((8, RET_W)),
        ],
        out_specs=[pl.BlockSpec((cps * CHUNK, IN_W), lambda i: (rev(i), 0)), pl.BlockSpec((1, 128), lambda i: (0, 0))],
        out_shape=[jax.ShapeDtypeStruct((s, IN_W), BF16), jax.ShapeDtypeStruct((1, 128), F32)],
        scratch_shapes=[pltpu.VMEM((N_RET_HEADS, RET_HEAD_DIM, RET_HEAD_DIM), F32), pltpu.VMEM((CHUNK, 2 * KV_W), F32),
                        pltpu.VMEM((CHUNK, 128), F32)],
        args=(proj, proj, dmix, states, probs, p_sinks, ret_scores, ret_normed, ret_rstd, sin, cos, d_intra, xi_full, zeta_full,
              decay_full), hosted=hosted)


def _in_proj_bwd(dproj, w_in_t, x, gain, dx1, hosted=None):
    s = x.shape[0]
    tm = min(BIG_TOKEN_TILE, s)

    def body(dp_ref, w_ref, x_ref, g_ref, dx1_ref, dx_ref, gacc_ref):
        @pl.when(pl.program_id(0) == 0)
        def _():
            gacc_ref[...] = jnp.zeros_like(gacc_ref)

        dh = _dot(dp_ref[...], w_ref[...])
        xv = x_ref[...]
        d1, dg = _rms_bwd(dh, xv, _rstd(xv), g_ref[...])
        dx_ref[...] = dx1_ref[...] + d1
        gacc_ref[0:1, :] += dg

    tok = lambda w: pl.BlockSpec((tm, w), lambda i: (i, 0))
    return _hosted_call(
        body, name="in_proj_bwd", grid=(s // tm,),
        in_specs=[tok(IN_W), _resident((IN_W, D_MODEL)), tok(D_MODEL), _resident((1, D_MODEL)), tok(D_MODEL)],
        out_specs=[tok(D_MODEL), pl.BlockSpec((8, D_MODEL), lambda i: (0, 0))],
        out_shape=[jax.ShapeDtypeStruct((s, D_MODEL), F32), jax.ShapeDtypeStruct((8, D_MODEL), F32)],
        scratch_shapes=[], args=(dproj, w_in_t, x, gain, dx1), hosted=hosted)


def _weight_grad(a, b, tn, name, by_block=False, hosted=None):
    s, m = a.shape
    n = b.shape[1]
    tk = min(WEIGHT_GRAD_TOKENS if m <= D_MODEL else WEIGHT_GRAD_TOKENS // 2, s)

    def body(a_ref, b_ref, o_ref):
        @pl.when(pl.program_id(1) == 0)
        def _():
            o_ref[...] = jnp.zeros_like(o_ref)

        o_ref[...] += _dot_tn(a_ref[...], b_ref[...])

    if by_block:
        out_spec = pl.BlockSpec((None, m, tn), lambda j, k: (j, 0, 0))
        out_shape = jax.ShapeDtypeStruct((n // tn, m, tn), F32)
    else:
        out_spec = pl.BlockSpec((m, tn), lambda j, k: (0, j))
        out_shape = jax.ShapeDtypeStruct((m, n), F32)
    (out,), lands = _hosted_call(
        body, name=name, grid=(n // tn, s // tk),
        in_specs=[pl.BlockSpec((tk, m), lambda j, k: (k, 0)), pl.BlockSpec((tk, tn), lambda j, k: (k, j))],
        out_specs=[out_spec], out_shape=[out_shape], scratch_shapes=[], args=(a, b), hosted=hosted)
    return out if hosted is None else (out, lands)


def _place():
    return lax.axis_index("x"), lax.axis_index("y"), lax.axis_index("c")


def _remote(src, dst, send_sems, recv_sems, k, to):
    return pltpu.make_async_remote_copy(src_ref=src, dst_ref=dst, send_sem=send_sems.at[k], recv_sem=recv_sems.at[k],
                                        device_id=to, device_id_type=MESH)


def _gather_level1_copies(w_refs, out_refs, send_sems, recv_sems, local_sems):
    x, y, c = _place()
    mine_at = 2 * x + y
    peers = [(x, y, 1 - c), (1 - x, y, c), (x, 1 - y, c), (1 - x, 1 - y, c)]
    local, sends, recvs = [], [], []
    for i, (w, out) in enumerate(zip(w_refs, out_refs)):
        half = w.shape[0] // 2
        src = w.at[pl.ds(pl.multiple_of(c * half, 16 if half % 16 == 0 else 8), half), :]
        mine = out.at[mine_at, c]
        local.append(pltpu.make_async_copy(src, mine, local_sems.at[i]))
        for k, p in enumerate(peers):
            sends.append(_remote(src, mine, send_sems, recv_sems, 4 * i + k, p))
            lands = out.at[mine_at, 1 - c] if k == 0 else out.at[2 * p[0] + p[1], c]
            recvs.append(_remote(src, lands, send_sems, recv_sems, 4 * i + k, p))
    return local, sends, recvs


def _gather_level1_start(w_refs, out_refs, send_sems, recv_sems, local_sems):
    local, sends, _ = _gather_level1_copies(w_refs, out_refs, send_sems, recv_sems, local_sems)
    for cp in local + sends:
        cp.start()


def _gather_level1_finish(w_refs, out_refs, send_sems, recv_sems, local_sems):
    local, sends, recvs = _gather_level1_copies(w_refs, out_refs, send_sems, recv_sems, local_sems)
    for cp in recvs:
        cp.wait_recv()
    for cp in sends:
        cp.wait_send()
    for cp in local:
        cp.wait()


def _gather_level2_copies(in_refs, out_refs, send_sems, recv_sems, local_sems):
    x, y, c = _place()
    chips = [(1 - x, y), (x, 1 - y), (1 - x, 1 - y)]
    sends, recvs = [], []
    for i, (src, out) in enumerate(zip(in_refs, out_refs)):
        for j, (px, py) in enumerate(chips):
            sends.append(_remote(src.at[2 * px + py, c], out.at[2 * px + py, c], send_sems, recv_sems, 3 * i + j, (x, y, 1 - c)))
            recvs.append(_remote(src.at[2 * px + py, c], out.at[2 * px + py, 1 - c], send_sems, recv_sems, 3 * i + j,
                                 (x, y, 1 - c)))
    return sends, recvs


def _gather_level2_start(in_refs, out_refs, send_sems, recv_sems, local_sems):
    for cp in _gather_level2_copies(in_refs, out_refs, send_sems, recv_sems, local_sems)[0]:
        cp.start()


def _gather_level2_finish(in_refs, out_refs, send_sems, recv_sems, local_sems):
    sends, recvs = _gather_level2_copies(in_refs, out_refs, send_sems, recv_sems, local_sems)
    for cp in recvs:
        cp.wait_recv()
    for cp in sends:
        cp.wait_send()


def _gathered_shape(w):
    r, cols = w.shape
    return jax.ShapeDtypeStruct((N_CHIPS, 2, r // 2, cols), w.dtype)


def _hosted_gather_level1(shards):
    n = len(shards)
    return _Hosted(shards, [_gathered_shape(w) for w in shards], {}, 4 * n, n, _gather_level1_start, _gather_level1_finish)


def _hosted_gather_level2(gathered):
    n = len(gathered)
    return _Hosted(gathered, [jax.ShapeDtypeStruct(g.shape, g.dtype) for g in gathered], {i: i for i in range(n)}, 3 * n, 0,
                   _gather_level2_start, _gather_level2_finish)


def _gather_now(shards, name, seq_len):
    n = len(shards)
    rows = min(512, seq_len)
    angle = 1.0 / jnp.power(10000.0, jnp.linspace(0.0, 1.0, RET_HEAD_DIM // 2, dtype=F32))
    sign = jnp.where(jnp.arange(RET_HEAD_DIM) % 2 == 0, -1.0, 1.0).astype(F32)
    angle_sign = jnp.concatenate([jnp.repeat(angle, 2)[None], sign[None], jnp.zeros((6, RET_HEAD_DIM), F32)], axis=0)

    def body(*refs):
        w_refs, as_ref, out_refs = list(refs[:n]), refs[n], list(refs[n + 1:2 * n + 1])
        sin_ref, cos_ref, send1, recv1, local1, send2, recv2 = refs[2 * n + 1:]
        _gather_level1_start(w_refs, out_refs, send1, recv1, local1)

        def fill(i, carry):
            r0 = pl.multiple_of(i * rows, rows)
            pos = (lax.broadcasted_iota(jnp.int32, (rows, RET_HEAD_DIM), 0) + i * rows).astype(F32)
            arg = pos * as_ref[0:1, :]
            sin_ref[pl.ds(r0, rows), :] = jnp.sin(arg) * as_ref[1:2, :]
            cos_ref[pl.ds(r0, rows), :] = jnp.cos(arg)
            return carry

        lax.fori_loop(0, seq_len // rows, fill, 0)
        _gather_level1_finish(w_refs, out_refs, send1, recv1, local1)
        _gather_level2_start(out_refs, out_refs, send2, recv2, None)
        _gather_level2_finish(out_refs, out_refs, send2, recv2, None)

    hbm, vmem = pl.BlockSpec(memory_space=pl.ANY), pl.BlockSpec(memory_space=pltpu.VMEM)
    table = jax.ShapeDtypeStruct((seq_len, RET_HEAD_DIM), F32)
    res = pl.pallas_call(
        body, name=name, out_shape=[_gathered_shape(w) for w in shards] + [table, table],
        in_specs=[hbm] * n + [vmem], out_specs=[hbm] * n + [vmem, vmem],
        scratch_shapes=[pltpu.SemaphoreType.DMA((4 * n,)), pltpu.SemaphoreType.DMA((4 * n,)), pltpu.SemaphoreType.DMA((n,)),
                        pltpu.SemaphoreType.DMA((3 * n,)), pltpu.SemaphoreType.DMA((3 * n,))],
        compiler_params=_params(),
    )(*shards, angle_sign)
    return res[:n], res[n], res[n + 1]


def _scatter_copies(g_refs, land_refs, send_sems, recv_sems, local_sems):
    x, y, c = _place()
    copies = []
    for i, (g, land) in enumerate(zip(g_refs, land_refs)):
        for k, (px, py, pc) in enumerate(_relations(x, y, c)):
            copies.append(_remote(g.at[2 * px + py, pc], land.at[k], send_sems, recv_sems, 7 * i + k, (px, py, pc)))
    return copies


def _scatter_start(g_refs, land_refs, send_sems, recv_sems, local_sems):
    for cp in _scatter_copies(g_refs, land_refs, send_sems, recv_sems, local_sems):
        cp.start()


def _scatter_finish(g_refs, land_refs, send_sems, recv_sems, local_sems):
    for cp in _scatter_copies(g_refs, land_refs, send_sems, recv_sems, local_sems):
        cp.wait()


def _hosted_scatter(grads):
    lands = [jax.ShapeDtypeStruct((N_DEV - 1,) + g.shape[2:], g.dtype) for g in grads]
    return _Hosted(grads, lands, {}, 7 * len(grads), 0, _scatter_start, _scatter_finish)


def _relations(x, y, c):
    rel = []
    for fx in (0, 1):
        for fy in (0, 1):
            for fc in (0, 1):
                if fx or fy or fc:
                    rel.append(((1 - x) if fx else x, (1 - y) if fy else y, (1 - c) if fc else c))
    return rel


def _join_halves(shards, small):
    n = len(shards)

    def body(*refs):
        in_refs, small_ref, out_refs, all_ref = refs[:n], refs[n], refs[n + 1:2 * n + 1], refs[2 * n + 1]
        send_sems, recv_sems = refs[2 * n + 2:]
        x, y, c = _place()
        slot = lambda p: all_ref.at[4 * p[0] + 2 * p[1] + p[2]]
        all_ref[4 * x + 2 * y + c] = small_ref[...]
        sends = [_remote(src.at[c], out.at[c], send_sems, recv_sems, i, (x, y, 1 - c))
                 for i, (src, out) in enumerate(zip(in_refs, out_refs))]
        recvs = [_remote(src.at[c], out.at[1 - c], send_sems, recv_sems, i, (x, y, 1 - c))
                 for i, (src, out) in enumerate(zip(in_refs, out_refs))]
        for k, p in enumerate(_relations(x, y, c)):
            sends.append(_remote(small_ref, slot((x, y, c)), send_sems, recv_sems, n + k, p))
            recvs.append(_remote(small_ref, slot(p), send_sems, recv_sems, n + k, p))
        for cp in sends:
            cp.start()
        for cp in recvs:
            cp.wait_recv()
        for cp in sends:
            cp.wait_send()

    hbm, vmem = pl.BlockSpec(memory_space=pl.ANY), pl.BlockSpec(memory_space=pltpu.VMEM)
    pairs = n + N_DEV - 1
    res = pl.pallas_call(
        body, name="grad_join_halves",
        out_shape=[jax.ShapeDtypeStruct(t.shape, t.dtype) for t in shards] + [jax.ShapeDtypeStruct((N_DEV,) + small.shape, F32)],
        in_specs=[hbm] * n + [vmem], out_specs=[hbm] * n + [vmem], input_output_aliases={i: i for i in range(n)},
        scratch_shapes=[pltpu.SemaphoreType.DMA((pairs,)), pltpu.SemaphoreType.DMA((pairs,))],
    )(*shards, small)
    return res[:n], res[n]


def _row_tile(rows, row_bytes, limit=1 << 20):
    best = 8
    for t in range(8, rows + 1, 8):
        if rows % t == 0 and t * row_bytes <= limit:
            best = t
    return best


def _sum_pieces(g, land, place, name):
    _, _, rh, cols = g.shape
    tr = _row_tile(rh, (N_DEV - 1) * cols * 4, 4 << 20)

    def body(p_ref, g_ref, l_ref, out_ref):
        acc = g_ref[...]
        for k in range(N_DEV - 1):
            acc = acc + l_ref[k].astype(F32)
        out_ref[...] = acc

    return pl.pallas_call(
        body, name=name,
        grid_spec=pltpu.PrefetchScalarGridSpec(
            num_scalar_prefetch=1, grid=(rh // tr,),
            in_specs=[pl.BlockSpec((None, None, tr, cols), lambda r, p: (p[0], p[1], r, 0)),
                      pl.BlockSpec((N_DEV - 1, tr, cols), lambda r, p: (0, r, 0))],
            out_specs=pl.BlockSpec((None, tr, cols), lambda r, p: (p[1], r, 0))),
        out_shape=jax.ShapeDtypeStruct((2, rh, cols), g.dtype),
        compiler_params=_params(("arbitrary",)),
    )(place, g, land)


def _adamw_math(w, g, m, v):
    m = ADAM_B1 * m + (1.0 - ADAM_B1) * g
    v = ADAM_B2 * v + (1.0 - ADAM_B2) * (g * g)
    m_hat = m / (1.0 - ADAM_B1 ** ADAM_STEP)
    v_hat = v / (1.0 - ADAM_B2 ** ADAM_STEP)
    delta = -ADAM_LR * (m_hat / (jnp.sqrt(v_hat) + ADAM_EPS) + ADAM_WD * w)
    return delta, m, v


def _adamw(w, g, m, v, name):
    r, cols = w.shape
    tr = _row_tile(r, cols * 4)

    def body(w_ref, g_ref, m_ref, v_ref, d_ref, nm_ref, nv_ref):
        d_ref[...], nm_ref[...], nv_ref[...] = _adamw_math(w_ref[...], g_ref[...], m_ref[...], v_ref[...])

    blk = pl.BlockSpec((tr, cols), lambda i: (i, 0))
    shape = jax.ShapeDtypeStruct((r, cols), F32)
    return pl.pallas_call(
        body, name=name, grid=(r // tr,), in_specs=[blk] * 4, out_specs=[blk] * 3, out_shape=[shape] * 3,
        compiler_params=_params(("arbitrary",)),
    )(w, g, m, v)


def _sum_devices(gathered):
    _, r, cols = gathered.shape

    def body(a_ref, g_ref):
        g = a_ref[0]
        for k in range(1, N_DEV):
            g = g + a_ref[k]
        g_ref[...] = g

    return pl.pallas_call(body, name="sum_small_grads", out_shape=jax.ShapeDtypeStruct((r, cols), F32))(gathered)


def _pack_conv(cw):
    flat = cw.reshape(-1)
    return jnp.pad(flat, (0, ROWS_CONV * D_MODEL - flat.shape[0])).reshape(ROWS_CONV, D_MODEL)


def _unpack_conv(rows):
    return rows.reshape(-1)[:CONV_WIDTH * UP_W // N_CHIPS].reshape(CONV_WIDTH, UP_W // N_CHIPS)


def _columns_to_shards(w):
    r, n = w.shape
    return jnp.transpose(w.reshape(r, N_CHIPS, n // N_CHIPS), (1, 0, 2))


def _shards_to_columns(w):
    _, r, n = w.shape
    return jnp.transpose(w, (1, 0, 2)).reshape(r, N_CHIPS * n)


def _pack_small(g_mix_pre, g_mix_post, g_ffn_pre, g_ffn_post, sinks, conv_b, loss):
    pad_row = lambda v: jnp.pad(v.reshape(1, -1), ((0, 0), (0, D_MODEL - v.size)))
    cb = jnp.pad(conv_b.reshape(-1), (0, 6 * D_MODEL - UP_W)).reshape(6, D_MODEL)
    zeros2 = jnp.zeros((2, D_MODEL), F32)
    return jnp.concatenate([g_mix_pre.reshape(1, -1), g_mix_post.reshape(1, -1), g_ffn_pre.reshape(1, -1),
                            g_ffn_post.reshape(1, -1), pad_row(sinks), pad_row(loss), zeros2, cb, zeros2], axis=0)


def _unpack_small(p):
    return dict(mix_pre_norm=p[0:1], mix_post_norm=p[1:2], ffn_pre_norm=p[2:3], ffn_post_norm=p[3:4],
                attn_sinks=p[4:5, :N_ATTN_HEADS], loss=p[5, 0], conv_b=p[8:14].reshape(1, -1)[:, :UP_W],
                conv_w=_unpack_conv(p[SMALL_ROWS:SMALL_ROWS + ROWS_CONV]))


def _local_step(x, target, g_mix_pre, w_in, sinks, w_out, g_mix_post, g_ffn_pre, w_up, conv_w, conv_b, w_down, g_ffn_post,
                distributed=True, rope=None):
    s = x.shape[0]
    consts = _ret_constants()
    sin, cos = _rope_tables(s) if rope is None else rope

    by_half = lambda g, rows: g.reshape(N_CHIPS, 2, rows // (2 * N_CHIPS), g.shape[-1])

    if distributed:
        (h1, proj), level1 = _in_proj(x, g_mix_pre, w_in, _hosted_gather_level1([w_out, w_up, w_down]))
        (mix, states, *kept), (w_out, w_up, w_down) = _mixer_fwd(proj, sinks, sin, cos, consts, _hosted_gather_level2(level1))
        w_out, w_down = w_out.reshape(D_MODEL, D_MODEL), w_down.reshape(D_FF, D_MODEL)
        w_up = w_up.reshape(N_CHIPS, D_MODEL, UP_W // N_CHIPS)
    else:
        (h1, proj), _ = _in_proj(x, g_mix_pre, w_in)
        (mix, states, *kept), _ = _mixer_fwd(proj, sinks, sin, cos, consts)
    mixed, x1, h2, u0 = _out_up_proj(mix, x, w_out, g_mix_post, g_ffn_pre, w_up)
    y, dy2, dout, du, conv_acc, tail_acc = _ffn_tail(u0, x1, target, conv_w, conv_b, w_down, g_ffn_post)
    du0, dx1, dmixed, dmix, head_acc = _ffn_head_bwd(du, conv_w, w_up, x1, g_ffn_pre, dout, mixed, g_mix_post, w_out)

    d_w_down = _weight_grad(y, dy2, 512, "grad_w_down")
    d_w_up = _weight_grad(h2, du0, UP_W // N_CHIPS, "grad_w_up", by_block=True)
    d_w_out = _weight_grad(mix, dmixed, D_MODEL, "grad_w_out")
    early = [by_half(d_w_down, D_FF), by_half(d_w_up, N_CHIPS * D_MODEL), by_half(d_w_out, D_MODEL)]
    (dproj, dsinks), early_lands = _mixer_bwd(proj, dmix, states, kept, sin, cos, consts,
                                              _hosted_scatter(early) if distributed else None)
    d_w_in_t = _weight_grad(dproj, h1, 512, "grad_w_in")
    late = [by_half(d_w_in_t, IN_W)]
    (grad_x, in_acc), late_lands = _in_proj_bwd(dproj, w_in, x, g_mix_pre, dx1, _hosted_scatter(late) if distributed else None)

    small = _pack_small(in_acc[0], head_acc[1], head_acc[0], tail_acc[0], dsinks[0, :N_ATTN_HEADS], conv_acc[3],
                        jnp.sum(tail_acc[1]))
    d_conv = jnp.pad(conv_acc[0:CONV_WIDTH].reshape(-1), (0, CONV_FULL_ROWS * D_MODEL - CONV_WIDTH * UP_W))
    small = jnp.concatenate([small, d_conv.reshape(CONV_FULL_ROWS, D_MODEL)], axis=0)
    grads = dict(w_down=early[0], w_up=early[1], w_out=early[2], w_in=late[0])
    lands = dict(zip(["w_down", "w_up", "w_out", "w_in"], early_lands + late_lands))
    return grad_x, grads, lands, small


def kernel(x, mix_pre_norm, w_in, attn_sinks, w_out, mix_post_norm, ffn_pre_norm, w_up, conv_w, conv_b, w_down, ffn_post_norm, loss_target, m_mix_pre_norm, m_w_in, m_attn_sinks, m_w_out, m_mix_post_norm, m_ffn_pre_norm, m_w_up, m_conv_w, m_conv_b, m_w_down, m_ffn_post_norm, v_mix_pre_norm, v_w_in, v_attn_sinks, v_w_out, v_mix_post_norm, v_ffn_pre_norm, v_w_up, v_conv_w, v_conv_b, v_w_down, v_ffn_post_norm):
    cx, cy, cc = _place()
    shard = 2 * cx + cy

    conv_rows = jnp.pad(conv_w[0], ((0, 16 - CONV_WIDTH), (0, 0)))
    w_in_t = jnp.swapaxes(w_in[0], 0, 1)
    (w_in_all, conv_all), sin, cos = _gather_now([w_in_t.astype(BF16), conv_rows], "gather_w_in", x.shape[1])
    conv_full = _shards_to_columns(conv_all[:, 0, :CONV_WIDTH])

    grad_x, grads, lands, small = _local_step(
        x[0], loss_target[0], mix_pre_norm, w_in_all.reshape(IN_W, D_MODEL), attn_sinks.reshape(-1), w_out[0].astype(BF16),
        mix_post_norm, ffn_pre_norm, w_up[0].astype(BF16), conv_full, conv_b, w_down[0].astype(BF16), ffn_post_norm,
        rope=(sin, cos))

    place = jnp.stack([shard, cc]).astype(jnp.int32)
    mats = ["w_in", "w_out", "w_up", "w_down"]
    halves = [_sum_pieces(grads[n], lands[n], place, "sum_grad_" + n) for n in mats]
    weights = dict(w_in=(w_in, m_w_in, v_w_in), w_out=(w_out, m_w_out, v_w_out), w_up=(w_up, m_w_up, v_w_up),
                   w_down=(w_down, m_w_down, v_w_down))
    mat_out = {}
    joined_all, small_all = _join_halves(halves, small)
    for n, joined in zip(mats, joined_all):
        w, m, v = (t[0] for t in weights[n])
        if n == "w_in":
            w, m, v = (jnp.swapaxes(t, 0, 1) for t in (w, m, v))
        res = (joined.reshape(w.shape),) + tuple(_adamw(w, joined.reshape(w.shape), m, v, "adamw_" + n))
        mat_out[n] = tuple(jnp.swapaxes(t, 0, 1) for t in res) if n == "w_in" else res

    small_sum = _sum_devices(small_all)
    d_conv_full = small_sum[SMALL_ROWS:].reshape(-1)[:CONV_WIDTH * UP_W].reshape(CONV_WIDTH, UP_W)
    d_conv_mine = lax.dynamic_slice_in_dim(d_conv_full, shard * (UP_W // N_CHIPS), UP_W // N_CHIPS, axis=1)
    g_s = jnp.concatenate([small_sum[:SMALL_ROWS], _pack_conv(d_conv_mine)], axis=0)
    zero = jnp.zeros((), F32)
    pack_rep = lambda a, b, c_, d, e, f, cw: jnp.concatenate([_pack_small(a, b, c_, d, e, f, zero), _pack_conv(cw[0])], axis=0)
    w_s = pack_rep(mix_pre_norm, mix_post_norm, ffn_pre_norm, ffn_post_norm, attn_sinks, conv_b, conv_w)
    m_s = pack_rep(m_mix_pre_norm, m_mix_post_norm, m_ffn_pre_norm, m_ffn_post_norm, m_attn_sinks, m_conv_b, m_conv_w)
    v_s = pack_rep(v_mix_pre_norm, v_mix_post_norm, v_ffn_pre_norm, v_ffn_post_norm, v_attn_sinks, v_conv_b, v_conv_w)
    delta_s, new_m_s, new_v_s = _adamw(w_s, g_s, m_s, v_s, "adamw_small")

    names = ["mix_pre_norm", "w_in", "attn_sinks", "w_out", "mix_post_norm", "ffn_pre_norm", "w_up", "conv_w", "conv_b",
             "w_down", "ffn_post_norm"]

    def leaves(which, packed_small):
        smalls = _unpack_small(packed_small)
        return [mat_out[n][which][None] if n in mat_out else (smalls[n][None] if n == "conv_w" else smalls[n]) for n in names]

    loss = _unpack_small(g_s)["loss"]
    return (loss, grad_x[None], *leaves(0, g_s), *leaves(1, delta_s), *leaves(2, new_m_s), *leaves(3, new_v_s))
```

```python
import math

import jax
import jax.numpy as jnp
from jax import lax
from jax.experimental import pallas as pl
from jax.experimental.pallas import tpu as pltpu

F32 = jnp.float32
BF16 = jnp.bfloat16

D_MODEL = 1024
HEAD_DIM = 64
ATTN_W = 512
N_ATTN_HEADS = 8
KV_W = 128
RET_W = 512
N_RET_HEADS = 4
RET_HEAD_DIM = 128
CHUNK = 128
IN_W = 2816
D_FF = 2816
UP_W = 2 * D_FF
CONV_WIDTH = 3
RMS_EPS = 1e-6
GN_EPS = 1e-6
MASK_VALUE = -1e30
ATTN_SCALE = HEAD_DIM ** -0.5
RET_K_SCALE = RET_HEAD_DIM ** -0.5
GELU_C = math.sqrt(2.0 / math.pi)
GELU_A = 0.044715

ADAM_LR = 0.001
ADAM_B1 = 0.9
ADAM_B2 = 0.999
ADAM_EPS = 1e-08
ADAM_WD = 0.01
ADAM_STEP = 10

N_CHIPS = 4
N_DEV = 8
MESH = pl.DeviceIdType.MESH
VMEM_LIMIT_V7X = 56 * 1024 * 1024
TOKEN_TILE = 256
BIG_TOKEN_TILE = 512
IN_PROJ_TOKEN_TILE = 1024
WEIGHT_GRAD_TOKENS = 2048
FFN_ROW_BLOCK = 64
HEAD_BWD_COLS = 512
MIXER_CHUNKS_PER_STEP = 4
Q_A0, KV_A0, Q_R0, K_R0, V_R0, G_R0 = 0, 512, 768, 1280, 1792, 2304

ROWS_CONV = 8
SMALL_ROWS = 16
CONV_FULL_ROWS = 24


def _params(sem=None, **kw):
    if sem is not None:
        kw["dimension_semantics"] = sem
    return pltpu.CompilerParams(vmem_limit_bytes=VMEM_LIMIT_V7X, **kw)


def _resident(shape):
    zeros = (0,) * len(shape)
    return pl.BlockSpec(shape, lambda *_: zeros, pipeline_mode=pl.Buffered(1))


class _Hosted:
    def __init__(self, ins, outs, aliases, n_pairs, n_local, start, finish):
        self.ins, self.outs, self.aliases = list(ins), list(outs), dict(aliases)
        self.n_pairs, self.n_local, self.start, self.finish = n_pairs, max(n_local, 1), start, finish


def _hosted_call(compute, *, name, grid, in_specs, out_specs, out_shape, scratch_shapes, args, hosted=None):
    params = _params(("arbitrary",) * len(grid))
    if hosted is None:
        res = pl.pallas_call(compute, name=name, grid=grid, in_specs=in_specs, out_specs=out_specs, out_shape=out_shape,
                             scratch_shapes=scratch_shapes, compiler_params=params)(*args)
        return list(res), []
    n_in, n_out, n_scr = len(in_specs), len(out_specs), len(scratch_shapes)
    h_in, h_out = len(hosted.ins), len(hosted.outs)

    def at(step_of):
        cond = pl.program_id(0) == step_of(grid[0])
        for d in range(1, len(grid)):
            cond = jnp.logical_and(cond, pl.program_id(d) == step_of(grid[d]))
        return cond

    def body(*refs):
        ins, refs = refs[:n_in], refs[n_in:]
        h_ins, refs = refs[:h_in], refs[h_in:]
        outs, refs = refs[:n_out], refs[n_out:]
        h_outs, refs = refs[:h_out], refs[h_out:]
        scr, sems = refs[:n_scr], refs[n_scr:]

        @pl.when(at(lambda n: 0))
        def _():
            hosted.start(h_ins, h_outs, *sems)

        compute(*ins, *outs, *scr)

        @pl.when(at(lambda n: n - 1))
        def _():
            hosted.finish(h_ins, h_outs, *sems)

    hbm = pl.BlockSpec(memory_space=pl.ANY)
    res = pl.pallas_call(
        body, name=name, grid=grid,
        in_specs=list(in_specs) + [hbm] * h_in, out_specs=list(out_specs) + [hbm] * h_out,
        out_shape=list(out_shape) + hosted.outs,
        scratch_shapes=list(scratch_shapes) + [pltpu.SemaphoreType.DMA((hosted.n_pairs,)), pltpu.SemaphoreType.DMA((hosted.n_pairs,)),
                                               pltpu.SemaphoreType.DMA((hosted.n_local,))],
        input_output_aliases={n_in + a: n_out + b for a, b in hosted.aliases.items()},
        compiler_params=params,
    )(*args, *hosted.ins)
    return list(res[:n_out]), list(res[n_out:])


def _dot(a, b):
    return jnp.dot(a, b, preferred_element_type=F32)


def _dot_nt(a, b):
    return lax.dot_general(a, b, (((1,), (1,)), ((), ())), preferred_element_type=F32)


def _dot_tn(a, b):
    return lax.dot_general(a, b, (((0,), (0,)), ((), ())), preferred_element_type=F32)


def _shift_matrix(n, by):
    row = lax.broadcasted_iota(jnp.int32, (n, n), 0)
    col = lax.broadcasted_iota(jnp.int32, (n, n), 1)
    return jnp.where(col == row + by, 1.0, 0.0).astype(BF16)


def _rstd(v):
    return lax.rsqrt(jnp.mean(v * v, axis=-1, keepdims=True) + RMS_EPS)


def _rms_bwd(dy, v, rstd, gain):
    n = v * rstd
    dgain = jnp.sum(dy * n, axis=0, keepdims=True)
    dn = dy * gain
    dv = rstd * (dn - n * jnp.mean(dn * n, axis=-1, keepdims=True))
    return dv, dgain


def _lane_lo(shape):
    return (lax.broadcasted_iota(jnp.int32, shape, 1) % 128) < HEAD_DIM


GROUP = N_ATTN_HEADS // (KV_W // HEAD_DIM)


def _attn_bias(first_chunk):
    qi = lax.broadcasted_iota(jnp.int32, (GROUP * CHUNK, 2 * CHUNK), 0) % CHUNK
    kj = lax.broadcasted_iota(jnp.int32, (GROUP * CHUNK, 2 * CHUNK), 1)
    valid = jnp.logical_and(kj > qi, kj <= qi + CHUNK)
    if first_chunk:
        valid = jnp.logical_and(valid, kj >= CHUNK)
    return jnp.where(valid, 0.0, MASK_VALUE)


def _half(shape, hk):
    lo = _lane_lo(shape)
    return lo if hk == 0 else jnp.logical_not(lo)


class _GroupMasks:
    def __init__(self, sk_ref):
        groups = range(KV_W // HEAD_DIM)
        self.q = [_half((CHUNK, 128), hk) for hk in groups]
        self.kv = [_half((2 * CHUNK, 128), hk) for hk in groups]
        self.sinks = [_group_sinks(sk_ref, hk) for hk in groups]


def _stack_heads(ref, row0, col0, hk, half):
    parts = []
    for j in range(GROUP):
        h = GROUP * hk + j
        pair = ref[row0:row0 + CHUNK, col0 + (h // 2) * 128:col0 + (h // 2 + 1) * 128].astype(F32)
        if h % 2 != hk:
            pair = pltpu.roll(pair, HEAD_DIM, 1)
        parts.append(jnp.where(half, pair, 0.0))
    return jnp.concatenate(parts, axis=0)


def _unstack_heads(stacked, hk):
    pairs = []
    for q in range(GROUP // 2):
        even, odd = stacked[2 * q * CHUNK:(2 * q + 1) * CHUNK], stacked[(2 * q + 1) * CHUNK:(2 * q + 2) * CHUNK]
        pairs.append(even + pltpu.roll(odd, HEAD_DIM, 1) if hk == 0 else pltpu.roll(even, HEAD_DIM, 1) + odd)
    return pairs


def _group_sinks(sk_ref, hk):
    row = lax.broadcasted_iota(jnp.int32, (GROUP * CHUNK, 1), 0)
    col = jnp.full((GROUP * CHUNK, 1), sk_ref[GROUP * hk], F32)
    for j in range(1, GROUP):
        col = jnp.where(row >= j * CHUNK, sk_ref[GROUP * hk + j], col)
    return col


def _attn_probs(q_b, kk_b, bias, sink):
    s = _dot_nt(q_b, kk_b) * ATTN_SCALE + bias
    m = jnp.maximum(jnp.max(s, axis=-1, keepdims=True), sink)
    e = jnp.exp(s - m)
    e_sink = jnp.exp(sink - m)
    inv = 1.0 / (jnp.sum(e, axis=-1, keepdims=True) + e_sink)
    return e * inv, e_sink * inv


def _even_lanes(shape):
    return (lax.broadcasted_iota(jnp.int32, shape, 1) % 2) == 0


def _swap2(v, even):
    return jnp.where(even, pltpu.roll(v, v.shape[1] - 1, 1), pltpu.roll(v, 1, 1))


def _tile4(v):
    return jnp.concatenate([v, v, v, v], axis=-1)


def _sigmoid(v):
    return 1.0 / (1.0 + jnp.exp(-v))


def _ret_constants():
    h = N_RET_HEADS
    log_gamma = jnp.log(1.0 - jnp.power(2.0, -5.0 - jnp.arange(h, dtype=F32)))
    idx = jnp.arange(CHUNK, dtype=F32)
    rel = idx[:, None] - idx[None, :]
    d_intra = jnp.where(rel[None] >= 0, jnp.exp(log_gamma[:, None, None] * jnp.maximum(rel, 0.0)[None]), 0.0)
    xi = jnp.exp(log_gamma[None, :] * (idx[:, None] + 1.0))
    zeta = jnp.exp(log_gamma[None, :] * (CHUNK - 1.0 - idx[:, None]))
    decay = jnp.exp(log_gamma * CHUNK)
    xi_full = jnp.repeat(xi, RET_HEAD_DIM, axis=1)
    zeta_full = jnp.repeat(zeta, RET_HEAD_DIM, axis=1)
    decay_full = jnp.broadcast_to(jnp.repeat(decay, RET_HEAD_DIM)[None, :], (8, RET_W))
    return d_intra.astype(F32), xi_full.astype(F32), zeta_full.astype(F32), decay_full.astype(F32)


def _rope_tables(s):
    pos = jnp.arange(s, dtype=F32)
    angle = 1.0 / jnp.power(10000.0, jnp.linspace(0.0, 1.0, RET_HEAD_DIM // 2, dtype=F32))
    angle = jnp.repeat(angle, 2)
    sign = jnp.where(jnp.arange(RET_HEAD_DIM) % 2 == 0, -1.0, 1.0).astype(F32)
    return jnp.sin(pos[:, None] * angle[None]) * sign[None], jnp.cos(pos[:, None] * angle[None])


def _in_proj(x, gain, w_in_t, hosted=None):
    s = x.shape[0]
    tm = min(IN_PROJ_TOKEN_TILE, s)

    def body(x_ref, g_ref, w_ref, h_ref, p_ref):
        xv = x_ref[...]
        h = (xv * _rstd(xv) * g_ref[...]).astype(BF16)
        h_ref[...] = h
        p_ref[...] = _dot_nt(h, w_ref[...])

    return _hosted_call(
        body, name="in_proj", grid=(s // tm,),
        in_specs=[pl.BlockSpec((tm, D_MODEL), lambda i: (i, 0)), _resident((1, D_MODEL)), _resident((IN_W, D_MODEL))],
        out_specs=[pl.BlockSpec((tm, D_MODEL), lambda i: (i, 0)), pl.BlockSpec((tm, IN_W), lambda i: (i, 0))],
        out_shape=[jax.ShapeDtypeStruct((s, D_MODEL), BF16), jax.ShapeDtypeStruct((s, IN_W), F32)],
        scratch_shapes=[], args=(x, gain, w_in_t), hosted=hosted)


def _mixer_fwd(proj, sinks, sin, cos, consts, hosted=None):
    s = proj.shape[0]
    nc = s // CHUNK
    cps = MIXER_CHUNKS_PER_STEP
    groups = KV_W // HEAD_DIM
    d_intra, xi_full, zeta_full, decay_full = consts

    def body(sk_ref, p_ref, pkv_ref, sin_ref, cos_ref, dm_ref, xi_ref, ze_ref, dc_ref,
             mix_ref, st_ref, pr_ref, ps_ref, ra_ref, on_ref, rs_ref, rq_ref, rk_ref, rz_ref, state):
        i = pl.program_id(0)

        @pl.when(i == 0)
        def _():
            state[...] = jnp.zeros_like(state)

        st = [state[h] for h in range(N_RET_HEADS)]
        bias_any = _attn_bias(False)
        bias_c0 = jnp.where(i == 0, _attn_bias(True), bias_any)
        even = _even_lanes((CHUNK, RET_W))
        masks = _GroupMasks(sk_ref)
        for c in range(cps):
            r0 = c * CHUNK
            rows = slice(r0, r0 + CHUNK)

            kv_cur = p_ref[rows, KV_A0:KV_A0 + 2 * KV_W]
            kv_prev = pkv_ref[...] if c == 0 else p_ref[r0 - CHUNK:r0, KV_A0:KV_A0 + 2 * KV_W]
            kk = jnp.concatenate([kv_prev[:, :KV_W], kv_cur[:, :KV_W]], axis=0)
            vv = jnp.concatenate([kv_prev[:, KV_W:], kv_cur[:, KV_W:]], axis=0)
            kk_b = kk.astype(BF16)
            bias = bias_c0 if c == 0 else bias_any
            for hk in range(KV_W // HEAD_DIM):
                q_b = _stack_heads(p_ref, r0, Q_A0, hk, masks.q[hk]).astype(BF16)
                p, p_sink = _attn_probs(q_b, kk_b, bias, masks.sinks[hk])
                p_b = p.astype(BF16)
                pr_ref[c, hk] = p_b
                ps_ref[c, hk] = p_sink
                v_b = jnp.where(masks.kv[hk], vv, 0.0).astype(BF16)
                for q, pair in enumerate(_unstack_heads(_dot(p_b, v_b), hk)):
                    pi = (GROUP // 2) * hk + q
                    mix_ref[rows, pi * 128:(pi + 1) * 128] = pair.astype(BF16)

            sin4, cos4 = _tile4(sin_ref[rows, :]), _tile4(cos_ref[rows, :])
            q_r = p_ref[rows, Q_R0:Q_R0 + RET_W]
            k_r = p_ref[rows, K_R0:K_R0 + RET_W] * RET_K_SCALE
            q_r = q_r * cos4 + _swap2(q_r, even) * sin4
            k_r = k_r * cos4 + _swap2(k_r, even) * sin4
            q_b, k_b, kz_b = q_r.astype(BF16), k_r.astype(BF16), (k_r * ze_ref[...]).astype(BF16)
            rq_ref[rows, :], rk_ref[rows, :], rz_ref[rows, :] = q_b, k_b, kz_b
            for h in range(N_RET_HEADS):
                sl = slice(h * RET_HEAD_DIM, (h + 1) * RET_HEAD_DIM)
                qh, kh = q_b[:, sl], k_b[:, sl]
                vh = p_ref[rows, V_R0 + h * RET_HEAD_DIM:V_R0 + (h + 1) * RET_HEAD_DIM].astype(BF16)
                st_ref[c, h] = st[h]
                a_b = (_dot_nt(qh, kh) * dm_ref[h]).astype(BF16)
                qx = (q_r[:, sl] * xi_ref[:, sl]).astype(BF16)
                o = _dot(jnp.concatenate([a_b, qx], axis=1), jnp.concatenate([vh, st[h].astype(BF16)], axis=0))
                st[h] = dc_ref[0:1, sl] * st[h] + _dot_tn(kz_b[:, sl], vh)
                mu = jnp.mean(o, axis=-1, keepdims=True)
                oc = o - mu
                rs = lax.rsqrt(jnp.mean(oc * oc, axis=-1, keepdims=True) + GN_EPS)
                on = oc * rs
                ra_ref[c, h], on_ref[rows, sl], rs_ref[c, h] = a_b, on, rs
                g = p_ref[rows, G_R0 + h * RET_HEAD_DIM:G_R0 + (h + 1) * RET_HEAD_DIM]
                mix_ref[rows, ATTN_W + h * RET_HEAD_DIM:ATTN_W + (h + 1) * RET_HEAD_DIM] = (g * _sigmoid(g) * on).astype(BF16)
        for h in range(N_RET_HEADS):
            state[h] = st[h]

    return _hosted_call(
        body, name="mixer_fwd", grid=(nc // cps,),
        in_specs=[
            pl.BlockSpec(memory_space=pltpu.SMEM),
            pl.BlockSpec((cps * CHUNK, IN_W), lambda i: (i, 0)),
            pl.BlockSpec((CHUNK, 2 * KV_W), lambda i: (jnp.maximum(cps * i - 1, 0), KV_A0 // (2 * KV_W))),
            pl.BlockSpec((cps * CHUNK, RET_HEAD_DIM), lambda i: (i, 0)),
            pl.BlockSpec((cps * CHUNK, RET_HEAD_DIM), lambda i: (i, 0)),
            _resident((N_RET_HEADS, CHUNK, CHUNK)), _resident((CHUNK, RET_W)), _resident((CHUNK, RET_W)), _resident((8, RET_W)),
        ],
        out_specs=[
            pl.BlockSpec((cps * CHUNK, D_MODEL), lambda i: (i, 0)),
            pl.BlockSpec((cps, N_RET_HEADS, RET_HEAD_DIM, RET_HEAD_DIM), lambda i: (i, 0, 0, 0)),
            pl.BlockSpec((cps, groups, GROUP * CHUNK, 2 * CHUNK), lambda i: (i, 0, 0, 0)),
            pl.BlockSpec((cps, groups, GROUP * CHUNK, 1), lambda i: (i, 0, 0, 0)),
            pl.BlockSpec((cps, N_RET_HEADS, CHUNK, CHUNK), lambda i: (i, 0, 0, 0)),
            pl.BlockSpec((cps * CHUNK, RET_W), lambda i: (i, 0)),
            pl.BlockSpec((cps, N_RET_HEADS, CHUNK, 1), lambda i: (i, 0, 0, 0)),
        ] + [pl.BlockSpec((cps * CHUNK, RET_W), lambda i: (i, 0))] * 3,
        out_shape=[jax.ShapeDtypeStruct((s, D_MODEL), BF16),
                   jax.ShapeDtypeStruct((nc, N_RET_HEADS, RET_HEAD_DIM, RET_HEAD_DIM), F32),
                   jax.ShapeDtypeStruct((nc, groups, GROUP * CHUNK, 2 * CHUNK), BF16),
                   jax.ShapeDtypeStruct((nc, groups, GROUP * CHUNK, 1), F32),
                   jax.ShapeDtypeStruct((nc, N_RET_HEADS, CHUNK, CHUNK), BF16),
                   jax.ShapeDtypeStruct((s, RET_W), F32),
                   jax.ShapeDtypeStruct((nc, N_RET_HEADS, CHUNK, 1), F32)] + [jax.ShapeDtypeStruct((s, RET_W), BF16)] * 3,
        scratch_shapes=[pltpu.VMEM((N_RET_HEADS, RET_HEAD_DIM, RET_HEAD_DIM), F32)],
        args=(sinks, proj, proj, sin, cos, d_intra, xi_full, zeta_full, decay_full), hosted=hosted)


def _out_up_proj(mix, x, w_out, g_post, g_pre, w_up):
    s = x.shape[0]
    tm = min(BIG_TOKEN_TILE, s)
    blk = UP_W // N_CHIPS

    def body(mix_ref, x_ref, wo_ref, g2_ref, g3_ref, wu_ref, mixed_ref, x1_ref, h2_ref, u0_ref):
        mixed = _dot(mix_ref[...], wo_ref[...])
        mixed_ref[...] = mixed
        x1 = x_ref[...] + mixed * _rstd(mixed) * g2_ref[...]
        x1_ref[...] = x1
        h2 = (x1 * _rstd(x1) * g3_ref[...]).astype(BF16)
        h2_ref[...] = h2
        for k in range(N_CHIPS):
            u0_ref[:, k * blk:(k + 1) * blk] = _dot(h2, wu_ref[k]).astype(BF16)

    tok = lambda w: pl.BlockSpec((tm, w), lambda i: (i, 0))
    return pl.pallas_call(
        body, name="out_up_proj", grid=(s // tm,),
        in_specs=[tok(D_MODEL), tok(D_MODEL), _resident((D_MODEL, D_MODEL)), _resident((1, D_MODEL)), _resident((1, D_MODEL)),
                  _resident((N_CHIPS, D_MODEL, blk))],
        out_specs=[tok(D_MODEL), tok(D_MODEL), tok(D_MODEL), tok(UP_W)],
        out_shape=[jax.ShapeDtypeStruct((s, D_MODEL), F32), jax.ShapeDtypeStruct((s, D_MODEL), F32),
                   jax.ShapeDtypeStruct((s, D_MODEL), BF16), jax.ShapeDtypeStruct((s, UP_W), BF16)],
        compiler_params=_params(("arbitrary",)),
    )(mix, x, w_out, g_post, g_pre, w_up)


def _ffn_tail(u0, x1, target, conv_w, conv_b, w_down, g_post):
    s = x1.shape[0]
    tm = TOKEN_TILE
    last = s // tm - 1
    rb, lanes = FFN_ROW_BLOCK, 128

    def body(u0_ref, x1_ref, t_ref, cw_ref, cb_ref, wd_ref, g_ref,
             y_ref, dy2_ref, dout_ref, du_ref, cacc_ref, gacc_ref, u1_s, u2_s, carry, gelu_s, slope_s, dy_s, cacc):
        i = pl.program_id(0)

        @pl.when(i == 0)
        def _():
            carry[...] = jnp.zeros_like(carry)
            cacc[...] = jnp.zeros_like(cacc)
            gacc_ref[...] = jnp.zeros_like(gacc_ref)

        shift1, shift2 = _shift_matrix(tm, -1), _shift_matrix(tm, -2)
        r8 = lax.broadcasted_iota(jnp.int32, (8, 1), 0)
        wide = 2 * lanes

        def shift_block(col):
            cols = slice(col, col + wide)
            u1_s[:, cols] = _dot(shift1, u0_ref[:, cols])
            u2_s[:, cols] = _dot(shift2, u0_ref[:, cols])
            c14, c15 = carry[14:15, cols], carry[15:16, cols]
            u1_s[0:8, cols] = jnp.where(r8 == 0, c15, u1_s[0:8, cols])
            u2_s[0:8, cols] = jnp.where(r8 == 0, c14, jnp.where(r8 == 1, c15, u2_s[0:8, cols]))

        def taps(col):
            return (cw_ref[0:1, col:col + lanes], cw_ref[1:2, col:col + lanes], cw_ref[2:3, col:col + lanes],
                    cb_ref[0:1, col:col + lanes])

        def shifted(r0, col):
            return (u2_s[r0:r0 + rb, col:col + lanes], u1_s[r0:r0 + rb, col:col + lanes],
                    u0_ref[r0:r0 + rb, col:col + lanes].astype(F32))

        def conv(r0, col, w):
            u2, u1, uc = shifted(r0, col)
            return w[0] * u2 + w[1] * u1 + w[2] * uc + w[3]

        fold = lambda v: jnp.sum(v.reshape(rb // 8, 8, lanes), axis=0)

        shift_block(0)
        shift_block(D_FF)
        for j in range(D_FF // lanes):
            cg, cv = j * lanes, D_FF + j * lanes
            if cg % wide == 0 and cg + wide < D_FF:
                shift_block(cg + wide)
                shift_block(cv + wide)
            wg, wv = taps(cg), taps(cv)
            for r0 in range(0, tm, rb):
                gate, val = conv(r0, cg, wg), conv(r0, cv, wv)
                g2 = gate * gate
                th = jnp.tanh(gate * (GELU_C + GELU_C * GELU_A * g2))
                hp = 0.5 * th + 0.5
                gelu = gate * hp
                dgelu = hp + gate * (1.0 - th * th) * (0.5 * GELU_C + 1.5 * GELU_C * GELU_A * g2)
                y_ref[r0:r0 + rb, cg:cg + lanes] = (gelu * val).astype(BF16)
                gelu_s[r0:r0 + rb, cg:cg + lanes] = gelu
                slope_s[r0:r0 + rb, cg:cg + lanes] = dgelu * val

        y2 = _dot(y_ref[...], wd_ref[...])
        r4 = _rstd(y2)
        gain = g_ref[...]
        out = x1_ref[...] + y2 * r4 * gain
        diff = out - t_ref[...]
        dout = diff * (1.0 / D_MODEL)
        dout_ref[...] = dout
        dy2, dgain = _rms_bwd(dout, y2, r4, gain)
        dy2_b = dy2.astype(BF16)
        dy2_ref[...] = dy2_b
        gacc_ref[0:1, :] += dgain
        gacc_ref[1:2, :] += 0.5 * jnp.sum(diff * dout, axis=0, keepdims=True)
        carry[...] = u0_ref[tm - 16:tm, :].astype(F32)

        dy_s[:, 0:wide] = _dot_nt(dy2_b, wd_ref[0:wide, :])
        for j in range(D_FF // lanes):
            cg, cv = j * lanes, D_FF + j * lanes
            if cg % wide == 0 and cg + wide < D_FF:
                dy_s[:, cg + wide:cg + 2 * wide] = _dot_nt(dy2_b, wd_ref[cg + wide:cg + 2 * wide, :])
            acc = [[jnp.zeros((8, lanes), F32) for _ in range(CONV_WIDTH + 1)] for _ in range(2)]
            for r0 in range(0, tm, rb):
                dy = dy_s[r0:r0 + rb, cg:cg + lanes]
                d_gate = dy * slope_s[r0:r0 + rb, cg:cg + lanes]
                d_val = dy * gelu_s[r0:r0 + rb, cg:cg + lanes]
                for side, (col, d) in enumerate(((cg, d_gate), (cv, d_val))):
                    du_ref[r0:r0 + rb, col:col + lanes] = d.astype(BF16)
                    for k, u in enumerate(shifted(r0, col)):
                        acc[side][k] = acc[side][k] + fold(d * u)
                    acc[side][CONV_WIDTH] = acc[side][CONV_WIDTH] + fold(d)
            for side, col in enumerate((cg, cv)):
                for k in range(CONV_WIDTH + 1):
                    cacc[8 * k:8 * k + 8, col:col + lanes] += acc[side][k]

        @pl.when(i == last)
        def _():
            for k in range(CONV_WIDTH + 1):
                cacc_ref[k:k + 1, :] = jnp.sum(cacc[8 * k:8 * k + 8, :], axis=0, keepdims=True)
            cacc_ref[CONV_WIDTH + 1:8, :] = jnp.zeros((8 - CONV_WIDTH - 1, UP_W), F32)

    tok = lambda w: pl.BlockSpec((tm, w), lambda i: (i, 0))
    return pl.pallas_call(
        body, name="ffn_tail", grid=(s // tm,),
        in_specs=[tok(UP_W), tok(D_MODEL), tok(D_MODEL), _resident((CONV_WIDTH, UP_W)), _resident((1, UP_W)),
                  _resident((D_FF, D_MODEL)), _resident((1, D_MODEL))],
        out_specs=[tok(D_FF), tok(D_MODEL), tok(D_MODEL), tok(UP_W),
                   pl.BlockSpec((8, UP_W), lambda i: (0, 0)), pl.BlockSpec((8, D_MODEL), lambda i: (0, 0))],
        out_shape=[jax.ShapeDtypeStruct((s, D_FF), BF16), jax.ShapeDtypeStruct((s, D_MODEL), BF16),
                   jax.ShapeDtypeStruct((s, D_MODEL), F32), jax.ShapeDtypeStruct((s, UP_W), BF16),
                   jax.ShapeDtypeStruct((8, UP_W), F32), jax.ShapeDtypeStruct((8, D_MODEL), F32)],
        scratch_shapes=[pltpu.VMEM((tm, UP_W), F32), pltpu.VMEM((tm, UP_W), F32), pltpu.VMEM((16, UP_W), F32),
                        pltpu.VMEM((tm, D_FF), F32), pltpu.VMEM((tm, D_FF), F32),
                        pltpu.VMEM((tm, D_FF), F32), pltpu.VMEM((8 * (CONV_WIDTH + 1), UP_W), F32)],
        compiler_params=_params(("arbitrary",)),
    )(u0, x1, target, conv_w, conv_b, w_down, g_post)


def _ffn_head_bwd(du, conv_w, w_up, x1, g_pre, dout, mixed, g_post, w_out):
    s = x1.shape[0]
    tm = TOKEN_TILE
    nt = s // tm
    blk = UP_W // N_CHIPS

    def body(du_ref, halo_ref, cw_ref, wu_ref, x1_ref, g3_ref, dout_ref, mixed_ref, g2_ref, wo_ref,
             du0_ref, dx1_ref, dmixed_ref, dmix_ref, gacc_ref, dbuf):
        i = pl.program_id(0)

        @pl.when(i == 0)
        def _():
            gacc_ref[...] = jnp.zeros_like(gacc_ref)

        dbuf[0:tm, :] = du_ref[...].astype(F32)
        dbuf[tm:tm + 16, :] = jnp.where(i < nt - 1, halo_ref[...].astype(F32), 0.0)
        dh2 = jnp.zeros((tm, D_MODEL), F32)
        for k in range(N_CHIPS):
            for c0 in range(0, blk, HEAD_BWD_COLS):
                width = min(HEAD_BWD_COLS, blk - c0)
                cols = slice(k * blk + c0, k * blk + c0 + width)
                du0_b = (cw_ref[2:3, cols] * dbuf[0:tm, cols] + cw_ref[1:2, cols] * dbuf[1:1 + tm, cols]
                         + cw_ref[0:1, cols] * dbuf[2:2 + tm, cols]).astype(BF16)
                du0_ref[:, cols] = du0_b
                dh2 = dh2 + _dot_nt(du0_b, wu_ref[k, :, c0:c0 + width])
        x1 = x1_ref[...]
        d3, dg3 = _rms_bwd(dh2, x1, _rstd(x1), g3_ref[...])
        dx1 = dout_ref[...] + d3
        dx1_ref[...] = dx1
        mixed = mixed_ref[...]
        dmixed, dg2 = _rms_bwd(dx1, mixed, _rstd(mixed), g2_ref[...])
        dmixed_b = dmixed.astype(BF16)
        dmixed_ref[...] = dmixed_b
        dmix_ref[...] = _dot_nt(dmixed_b, wo_ref[...]).astype(BF16)
        gacc_ref[0:1, :] += dg3
        gacc_ref[1:2, :] += dg2

    tok = lambda w: pl.BlockSpec((tm, w), lambda i: (i, 0))
    halo = pl.BlockSpec((16, UP_W), lambda i: (jnp.minimum(i + 1, nt - 1) * (tm // 16), 0))
    return pl.pallas_call(
        body, name="ffn_head_bwd", grid=(nt,),
        in_specs=[tok(UP_W), halo, _resident((CONV_WIDTH, UP_W)), _resident((N_CHIPS, D_MODEL, blk)), tok(D_MODEL),
                  _resident((1, D_MODEL)), tok(D_MODEL), tok(D_MODEL), _resident((1, D_MODEL)), _resident((D_MODEL, D_MODEL))],
        out_specs=[tok(UP_W), tok(D_MODEL), tok(D_MODEL), tok(D_MODEL), pl.BlockSpec((8, D_MODEL), lambda i: (0, 0))],
        out_shape=[jax.ShapeDtypeStruct((s, UP_W), BF16), jax.ShapeDtypeStruct((s, D_MODEL), F32),
                   jax.ShapeDtypeStruct((s, D_MODEL), BF16), jax.ShapeDtypeStruct((s, D_MODEL), BF16),
                   jax.ShapeDtypeStruct((8, D_MODEL), F32)],
        scratch_shapes=[pltpu.VMEM((tm + 16, UP_W), F32)],
        compiler_params=_params(("arbitrary",)),
    )(du, du, conv_w, w_up, x1, g_pre, dout, mixed, g_post, w_out)


def _mixer_bwd(proj, dmix, states, kept, sin, cos, consts, hosted=None):
    probs, p_sinks, ret_scores, ret_normed, ret_rstd, ret_q, ret_k, ret_kz = kept
    s = proj.shape[0]
    nc = s // CHUNK
    cps = MIXER_CHUNKS_PER_STEP
    nb = nc // cps
    groups = KV_W // HEAD_DIM
    d_intra, xi_full, zeta_full, decay_full = consts

    def body(p_ref, pkv_ref, dmix_ref, st_ref, pr_ref, ps_ref, ra_ref, on_ref, rs_ref, rq_ref, rk_ref, rz_ref,
             sin_ref, cos_ref, dm_ref, xi_ref, ze_ref, dc_ref, dp_ref, dsk_ref, gstate, ckv, dsk_acc):
        i = pl.program_id(0)
        block = nb - 1 - i

        @pl.when(i == 0)
        def _():
            gstate[...] = jnp.zeros_like(gstate)
            ckv[...] = jnp.zeros_like(ckv)
            dsk_acc[...] = jnp.zeros_like(dsk_acc)

        gs_all = [gstate[h] for h in range(N_RET_HEADS)]
        later_kv = ckv[...]
        lane = lax.broadcasted_iota(jnp.int32, (CHUNK, 128), 1)
        dsk = jnp.zeros((CHUNK, 128), F32)
        even = _even_lanes((CHUNK, RET_W))
        half_q = [_half((CHUNK, 128), hk) for hk in range(groups)]
        half_kv = [_half((2 * CHUNK, 128), hk) for hk in range(groups)]
        for c in reversed(range(cps)):
            r0 = c * CHUNK
            rows = slice(r0, r0 + CHUNK)

            kv_cur = p_ref[rows, KV_A0:KV_A0 + 2 * KV_W]
            kv_prev = pkv_ref[...] if c == 0 else p_ref[r0 - CHUNK:r0, KV_A0:KV_A0 + 2 * KV_W]
            kk = jnp.concatenate([kv_prev[:, :KV_W], kv_cur[:, :KV_W]], axis=0)
            vv = jnp.concatenate([kv_prev[:, KV_W:], kv_cur[:, KV_W:]], axis=0)
            vv_b = vv.astype(BF16)
            dkk = jnp.zeros((2 * CHUNK, KV_W), F32)
            dvv = jnp.zeros((2 * CHUNK, KV_W), F32)
            for hk in range(groups):
                q_b = _stack_heads(p_ref, r0, Q_A0, hk, half_q[hk]).astype(BF16)
                do_b = _stack_heads(dmix_ref, r0, 0, hk, half_q[hk]).astype(BF16)
                p_b = pr_ref[c, hk]
                p = p_b.astype(F32)
                dpr = _dot_nt(do_b, vv_b)
                delta = jnp.sum(p * dpr, axis=-1, keepdims=True)
                ds_b = (p * (dpr - delta) * ATTN_SCALE).astype(BF16)
                dsink = -ps_ref[c, hk] * delta
                for j in range(GROUP):
                    dsk = dsk + jnp.where(lane == GROUP * hk + j, dsink[j * CHUNK:(j + 1) * CHUNK], 0.0)
                k_b = jnp.where(half_kv[hk], kk, 0.0).astype(BF16)
                for q, pair in enumerate(_unstack_heads(_dot(ds_b, k_b), hk)):
                    pi = (GROUP // 2) * hk + q
                    dp_ref[rows, Q_A0 + pi * 128:Q_A0 + (pi + 1) * 128] = pair.astype(BF16)
                dkk = dkk + _dot_tn(ds_b, q_b)
                dvv = dvv + _dot_tn(p_b, do_b)
            dp_ref[rows, KV_A0:KV_A0 + KV_W] = (dkk[CHUNK:] + later_kv[:, :KV_W]).astype(BF16)
            dp_ref[rows, KV_A0 + KV_W:KV_A0 + 2 * KV_W] = (dvv[CHUNK:] + later_kv[:, KV_W:]).astype(BF16)
            later_kv = jnp.concatenate([dkk[:CHUNK], dvv[:CHUNK]], axis=1)

            sin4, cos4 = _tile4(sin_ref[rows, :]), _tile4(cos_ref[rows, :])
            dq_parts, dk_parts = [], []
            for h in range(N_RET_HEADS):
                sl = slice(h * RET_HEAD_DIM, (h + 1) * RET_HEAD_DIM)
                qh, kh = rq_ref[rows, sl], rk_ref[rows, sl]
                vh = p_ref[rows, V_R0 + h * RET_HEAD_DIM:V_R0 + (h + 1) * RET_HEAD_DIM].astype(BF16)
                st_b = st_ref[c, h].astype(BF16)
                gs = gs_all[h]
                gs_b = gs.astype(BF16)
                xi_h = xi_ref[:, sl]
                dm = dm_ref[h]
                a_b, on, rs = ra_ref[c, h], on_ref[rows, sl], rs_ref[c, h]
                g = p_ref[rows, G_R0 + h * RET_HEAD_DIM:G_R0 + (h + 1) * RET_HEAD_DIM]
                sg = _sigmoid(g)
                dr = dmix_ref[rows, ATTN_W + h * RET_HEAD_DIM:ATTN_W + (h + 1) * RET_HEAD_DIM].astype(F32)
                dp_ref[rows, G_R0 + h * RET_HEAD_DIM:G_R0 + (h + 1) * RET_HEAD_DIM] = (
                    dr * on * (sg * (1.0 + g * (1.0 - sg)))).astype(BF16)
                don = dr * g * sg
                do = rs * (don - jnp.mean(don, axis=-1, keepdims=True) - on * jnp.mean(don * on, axis=-1, keepdims=True))
                do_b = do.astype(BF16)
                dox_b = (do * xi_h).astype(BF16)
                da_b = (_dot_nt(do_b, vh) * dm).astype(BF16)
                dq_parts.append(_dot(da_b, kh) + _dot_nt(dox_b, st_b))
                dk_parts.append(_dot_tn(da_b, qh) + ze_ref[:, sl] * _dot_nt(vh, gs_b))
                dv = _dot_tn(a_b, do_b) + _dot(rz_ref[rows, sl], gs_b)
                dp_ref[rows, V_R0 + h * RET_HEAD_DIM:V_R0 + (h + 1) * RET_HEAD_DIM] = dv.astype(BF16)
                gs_all[h] = dc_ref[0:1, sl] * gs + _dot_tn(qh, dox_b)
            dq = jnp.concatenate(dq_parts, axis=-1)
            dk = jnp.concatenate(dk_parts, axis=-1)
            dp_ref[rows, Q_R0:Q_R0 + RET_W] = (dq * cos4 - _swap2(dq, even) * sin4).astype(BF16)
            dp_ref[rows, K_R0:K_R0 + RET_W] = (RET_K_SCALE * (dk * cos4 - _swap2(dk, even) * sin4)).astype(BF16)

        for h in range(N_RET_HEADS):
            gstate[h] = gs_all[h]
        ckv[...] = later_kv
        dsk_acc[...] += dsk

        @pl.when(i == nb - 1)
        def _():
            dsk_ref[...] = jnp.sum(dsk_acc[...], axis=0, keepdims=True)

    rev = lambda i: nb - 1 - i
    return _hosted_call(
        body, name="mixer_bwd", grid=(nb,),
        in_specs=[
            pl.BlockSpec((cps * CHUNK, IN_W), lambda i: (rev(i), 0)),
            pl.BlockSpec((CHUNK, 2 * KV_W), lambda i: (jnp.maximum(cps * rev(i) - 1, 0), KV_A0 // (2 * KV_W))),
            pl.BlockSpec((cps * CHUNK, D_MODEL), lambda i: (rev(i), 0)),
            pl.BlockSpec((cps, N_RET_HEADS, RET_HEAD_DIM, RET_HEAD_DIM), lambda i: (rev(i), 0, 0, 0)),
            pl.BlockSpec((cps, groups, GROUP * CHUNK, 2 * CHUNK), lambda i: (rev(i), 0, 0, 0)),
            pl.BlockSpec((cps, groups, GROUP * CHUNK, 1), lambda i: (rev(i), 0, 0, 0)),
            pl.BlockSpec((cps, N_RET_HEADS, CHUNK, CHUNK), lambda i: (rev(i), 0, 0, 0)),
            pl.BlockSpec((cps * CHUNK, RET_W), lambda i: (rev(i), 0)),
            pl.BlockSpec((cps, N_RET_HEADS, CHUNK, 1), lambda i: (rev(i), 0, 0, 0)),
            pl.BlockSpec((cps * CHUNK, RET_W), lambda i: (rev(i), 0)), pl.BlockSpec((cps * CHUNK, RET_W), lambda i: (rev(i), 0)),
            pl.BlockSpec((cps * CHUNK, RET_W), lambda i: (rev(i), 0)),
            pl.BlockSpec((cps * CHUNK, RET_HEAD_DIM), lambda i: (rev(i), 0)),
            pl.BlockSpec((cps * CHUNK, RET_HEAD_DIM), lambda i: (rev(i), 0)),
            _resident((N_RET_HEADS, CHUNK, CHUNK)), _resident((CHUNK, RET_W)), _resident((CHUNK, RET_W)), _resident((8, RET_W)),
        ],
        out_specs=[pl.BlockSpec((cps * CHUNK, IN_W), lambda i: (rev(i), 0)), pl.BlockSpec((1, 128), lambda i: (0, 0))],
        out_shape=[jax.ShapeDtypeStruct((s, IN_W), BF16), jax.ShapeDtypeStruct((1, 128), F32)],
        scratch_shapes=[pltpu.VMEM((N_RET_HEADS, RET_HEAD_DIM, RET_HEAD_DIM), F32), pltpu.VMEM((CHUNK, 2 * KV_W), F32),
                        pltpu.VMEM((CHUNK, 128), F32)],
        args=(proj, proj, dmix, states, probs, p_sinks, ret_scores, ret_normed, ret_rstd, ret_q, ret_k, ret_kz, sin, cos, d_intra,
              xi_full, zeta_full, decay_full), hosted=hosted)


def _in_proj_bwd(dproj, w_in_t, x, gain, dx1, hosted=None):
    s = x.shape[0]
    tm = min(BIG_TOKEN_TILE, s)

    def body(dp_ref, w_ref, x_ref, g_ref, dx1_ref, dx_ref, gacc_ref):
        @pl.when(pl.program_id(0) == 0)
        def _():
            gacc_ref[...] = jnp.zeros_like(gacc_ref)

        dh = _dot(dp_ref[...], w_ref[...])
        xv = x_ref[...]
        d1, dg = _rms_bwd(dh, xv, _rstd(xv), g_ref[...])
        dx_ref[...] = dx1_ref[...] + d1
        gacc_ref[0:1, :] += dg

    tok = lambda w: pl.BlockSpec((tm, w), lambda i: (i, 0))
    return _hosted_call(
        body, name="in_proj_bwd", grid=(s // tm,),
        in_specs=[tok(IN_W), _resident((IN_W, D_MODEL)), tok(D_MODEL), _resident((1, D_MODEL)), tok(D_MODEL)],
        out_specs=[tok(D_MODEL), pl.BlockSpec((8, D_MODEL), lambda i: (0, 0))],
        out_shape=[jax.ShapeDtypeStruct((s, D_MODEL), F32), jax.ShapeDtypeStruct((8, D_MODEL), F32)],
        scratch_shapes=[], args=(dproj, w_in_t, x, gain, dx1), hosted=hosted)


def _weight_grad(a, b, tn, name, by_block=False, hosted=None):
    s, m = a.shape
    n = b.shape[1]
    tk = min(WEIGHT_GRAD_TOKENS if m <= D_MODEL else WEIGHT_GRAD_TOKENS // 2, s)

    def body(a_ref, b_ref, o_ref):
        @pl.when(pl.program_id(1) == 0)
        def _():
            o_ref[...] = jnp.zeros_like(o_ref)

        o_ref[...] += _dot_tn(a_ref[...], b_ref[...])

    if by_block:
        out_spec = pl.BlockSpec((None, m, tn), lambda j, k: (j, 0, 0))
        out_shape = jax.ShapeDtypeStruct((n // tn, m, tn), F32)
    else:
        out_spec = pl.BlockSpec((m, tn), lambda j, k: (0, j))
        out_shape = jax.ShapeDtypeStruct((m, n), F32)
    (out,), lands = _hosted_call(
        body, name=name, grid=(n // tn, s // tk),
        in_specs=[pl.BlockSpec((tk, m), lambda j, k: (k, 0)), pl.BlockSpec((tk, tn), lambda j, k: (k, j))],
        out_specs=[out_spec], out_shape=[out_shape], scratch_shapes=[], args=(a, b), hosted=hosted)
    return out if hosted is None else (out, lands)


def _place():
    return lax.axis_index("x"), lax.axis_index("y"), lax.axis_index("c")


def _remote(src, dst, send_sems, recv_sems, k, to):
    return pltpu.make_async_remote_copy(src_ref=src, dst_ref=dst, send_sem=send_sems.at[k], recv_sem=recv_sems.at[k],
                                        device_id=to, device_id_type=MESH)


def _gather_level1_copies(w_refs, out_refs, send_sems, recv_sems, local_sems):
    x, y, c = _place()
    mine_at = 2 * x + y
    peers = [(x, y, 1 - c), (1 - x, y, c), (x, 1 - y, c), (1 - x, 1 - y, c)]
    local, sends, recvs = [], [], []
    for i, (w, out) in enumerate(zip(w_refs, out_refs)):
        half = w.shape[0] // 2
        src = w.at[pl.ds(pl.multiple_of(c * half, 16 if half % 16 == 0 else 8), half), :]
        mine = out.at[mine_at, c]
        local.append(pltpu.make_async_copy(src, mine, local_sems.at[i]))
        for k, p in enumerate(peers):
            sends.append(_remote(src, mine, send_sems, recv_sems, 4 * i + k, p))
            lands = out.at[mine_at, 1 - c] if k == 0 else out.at[2 * p[0] + p[1], c]
            recvs.append(_remote(src, lands, send_sems, recv_sems, 4 * i + k, p))
    return local, sends, recvs


def _gather_level1_start(w_refs, out_refs, send_sems, recv_sems, local_sems):
    local, sends, _ = _gather_level1_copies(w_refs, out_refs, send_sems, recv_sems, local_sems)
    for cp in local + sends:
        cp.start()


def _gather_level1_finish(w_refs, out_refs, send_sems, recv_sems, local_sems):
    local, sends, recvs = _gather_level1_copies(w_refs, out_refs, send_sems, recv_sems, local_sems)
    for cp in recvs:
        cp.wait_recv()
    for cp in sends:
        cp.wait_send()
    for cp in local:
        cp.wait()


def _gather_level2_copies(in_refs, out_refs, send_sems, recv_sems, local_sems):
    x, y, c = _place()
    chips = [(1 - x, y), (x, 1 - y), (1 - x, 1 - y)]
    sends, recvs = [], []
    for i, (src, out) in enumerate(zip(in_refs, out_refs)):
        for j, (px, py) in enumerate(chips):
            sends.append(_remote(src.at[2 * px + py, c], out.at[2 * px + py, c], send_sems, recv_sems, 3 * i + j, (x, y, 1 - c)))
            recvs.append(_remote(src.at[2 * px + py, c], out.at[2 * px + py, 1 - c], send_sems, recv_sems, 3 * i + j,
                                 (x, y, 1 - c)))
    return sends, recvs


def _gather_level2_start(in_refs, out_refs, send_sems, recv_sems, local_sems):
    for cp in _gather_level2_copies(in_refs, out_refs, send_sems, recv_sems, local_sems)[0]:
        cp.start()


def _gather_level2_finish(in_refs, out_refs, send_sems, recv_sems, local_sems):
    sends, recvs = _gather_level2_copies(in_refs, out_refs, send_sems, recv_sems, local_sems)
    for cp in recvs:
        cp.wait_recv()
    for cp in sends:
        cp.wait_send()


def _gathered_shape(w):
    r, cols = w.shape
    return jax.ShapeDtypeStruct((N_CHIPS, 2, r // 2, cols), w.dtype)


def _hosted_gather_level1(shards):
    n = len(shards)
    return _Hosted(shards, [_gathered_shape(w) for w in shards], {}, 4 * n, n, _gather_level1_start, _gather_level1_finish)


def _hosted_gather_level2(gathered):
    n = len(gathered)
    return _Hosted(gathered, [jax.ShapeDtypeStruct(g.shape, g.dtype) for g in gathered], {i: i for i in range(n)}, 3 * n, 0,
                   _gather_level2_start, _gather_level2_finish)


def _gather_now(shards, name, seq_len):
    n = len(shards)
    rows = min(512, seq_len)
    angle = 1.0 / jnp.power(10000.0, jnp.linspace(0.0, 1.0, RET_HEAD_DIM // 2, dtype=F32))
    sign = jnp.where(jnp.arange(RET_HEAD_DIM) % 2 == 0, -1.0, 1.0).astype(F32)
    angle_sign = jnp.concatenate([jnp.repeat(angle, 2)[None], sign[None], jnp.zeros((6, RET_HEAD_DIM), F32)], axis=0)

    def body(*refs):
        w_refs, as_ref, out_refs = list(refs[:n]), refs[n], list(refs[n + 1:2 * n + 1])
        sin_ref, cos_ref, send1, recv1, local1, send2, recv2 = refs[2 * n + 1:]
        _gather_level1_start(w_refs, out_refs, send1, recv1, local1)

        def fill(i, carry):
            r0 = pl.multiple_of(i * rows, rows)
            pos = (lax.broadcasted_iota(jnp.int32, (rows, RET_HEAD_DIM), 0) + i * rows).astype(F32)
            arg = pos * as_ref[0:1, :]
            sin_ref[pl.ds(r0, rows), :] = jnp.sin(arg) * as_ref[1:2, :]
            cos_ref[pl.ds(r0, rows), :] = jnp.cos(arg)
            return carry

        lax.fori_loop(0, seq_len // rows, fill, 0)
        _gather_level1_finish(w_refs, out_refs, send1, recv1, local1)
        _gather_level2_start(out_refs, out_refs, send2, recv2, None)
        _gather_level2_finish(out_refs, out_refs, send2, recv2, None)

    hbm, vmem = pl.BlockSpec(memory_space=pl.ANY), pl.BlockSpec(memory_space=pltpu.VMEM)
    table = jax.ShapeDtypeStruct((seq_len, RET_HEAD_DIM), F32)
    res = pl.pallas_call(
        body, name=name, out_shape=[_gathered_shape(w) for w in shards] + [table, table],
        in_specs=[hbm] * n + [vmem], out_specs=[hbm] * n + [vmem, vmem],
        scratch_shapes=[pltpu.SemaphoreType.DMA((4 * n,)), pltpu.SemaphoreType.DMA((4 * n,)), pltpu.SemaphoreType.DMA((n,)),
                        pltpu.SemaphoreType.DMA((3 * n,)), pltpu.SemaphoreType.DMA((3 * n,))],
        compiler_params=_params(),
    )(*shards, angle_sign)
    return res[:n], res[n], res[n + 1]


def _scatter_copies(g_refs, land_refs, send_sems, recv_sems, local_sems):
    x, y, c = _place()
    copies = []
    for i, (g, land) in enumerate(zip(g_refs, land_refs)):
        for k, (px, py, pc) in enumerate(_relations(x, y, c)):
            copies.append(_remote(g.at[2 * px + py, pc], land.at[k], send_sems, recv_sems, 7 * i + k, (px, py, pc)))
    return copies


def _scatter_start(g_refs, land_refs, send_sems, recv_sems, local_sems):
    for cp in _scatter_copies(g_refs, land_refs, send_sems, recv_sems, local_sems):
        cp.start()


def _scatter_finish(g_refs, land_refs, send_sems, recv_sems, local_sems):
    for cp in _scatter_copies(g_refs, land_refs, send_sems, recv_sems, local_sems):
        cp.wait()


def _hosted_scatter(grads):
    lands = [jax.ShapeDtypeStruct((N_DEV - 1,) + g.shape[2:], g.dtype) for g in grads]
    return _Hosted(grads, lands, {}, 7 * len(grads), 0, _scatter_start, _scatter_finish)


def _relations(x, y, c):
    rel = []
    for fx in (0, 1):
        for fy in (0, 1):
            for fc in (0, 1):
                if fx or fy or fc:
                    rel.append(((1 - x) if fx else x, (1 - y) if fy else y, (1 - c) if fc else c))
    return rel


def _join_halves(shards, small):
    n = len(shards)

    def body(*refs):
        in_refs, small_ref, out_refs, all_ref = refs[:n], refs[n], refs[n + 1:2 * n + 1], refs[2 * n + 1]
        send_sems, recv_sems = refs[2 * n + 2:]
        x, y, c = _place()
        slot = lambda p: all_ref.at[4 * p[0] + 2 * p[1] + p[2]]
        all_ref[4 * x + 2 * y + c] = small_ref[...]
        sends = [_remote(src.at[c], out.at[c], send_sems, recv_sems, i, (x, y, 1 - c))
                 for i, (src, out) in enumerate(zip(in_refs, out_refs))]
        recvs = [_remote(src.at[c], out.at[1 - c], send_sems, recv_sems, i, (x, y, 1 - c))
                 for i, (src, out) in enumerate(zip(in_refs, out_refs))]
        for k, p in enumerate(_relations(x, y, c)):
            sends.append(_remote(small_ref, slot((x, y, c)), send_sems, recv_sems, n + k, p))
            recvs.append(_remote(small_ref, slot(p), send_sems, recv_sems, n + k, p))
        for cp in sends:
            cp.start()
        for cp in recvs:
            cp.wait_recv()
        for cp in sends:
            cp.wait_send()

    hbm, vmem = pl.BlockSpec(memory_space=pl.ANY), pl.BlockSpec(memory_space=pltpu.VMEM)
    pairs = n + N_DEV - 1
    res = pl.pallas_call(
        body, name="grad_join_halves",
        out_shape=[jax.ShapeDtypeStruct(t.shape, t.dtype) for t in shards] + [jax.ShapeDtypeStruct((N_DEV,) + small.shape, F32)],
        in_specs=[hbm] * n + [vmem], out_specs=[hbm] * n + [vmem], input_output_aliases={i: i for i in range(n)},
        scratch_shapes=[pltpu.SemaphoreType.DMA((pairs,)), pltpu.SemaphoreType.DMA((pairs,))],
    )(*shards, small)
    return res[:n], res[n]


def _row_tile(rows, row_bytes, limit=1 << 20):
    best = 8
    for t in range(8, rows + 1, 8):
        if rows % t == 0 and t * row_bytes <= limit:
            best = t
    return best


def _sum_pieces(g, land, place, name):
    _, _, rh, cols = g.shape
    tr = _row_tile(rh, (N_DEV - 1) * cols * 4, 4 << 20)

    def body(p_ref, g_ref, l_ref, out_ref):
        acc = g_ref[...]
        for k in range(N_DEV - 1):
            acc = acc + l_ref[k].astype(F32)
        out_ref[...] = acc

    return pl.pallas_call(
        body, name=name,
        grid_spec=pltpu.PrefetchScalarGridSpec(
            num_scalar_prefetch=1, grid=(rh // tr,),
            in_specs=[pl.BlockSpec((None, None, tr, cols), lambda r, p: (p[0], p[1], r, 0)),
                      pl.BlockSpec((N_DEV - 1, tr, cols), lambda r, p: (0, r, 0))],
            out_specs=pl.BlockSpec((None, tr, cols), lambda r, p: (p[1], r, 0))),
        out_shape=jax.ShapeDtypeStruct((2, rh, cols), g.dtype),
        compiler_params=_params(("arbitrary",)),
    )(place, g, land)


def _adamw_math(w, g, m, v):
    m = ADAM_B1 * m + (1.0 - ADAM_B1) * g
    v = ADAM_B2 * v + (1.0 - ADAM_B2) * (g * g)
    m_hat = m / (1.0 - ADAM_B1 ** ADAM_STEP)
    v_hat = v / (1.0 - ADAM_B2 ** ADAM_STEP)
    delta = -ADAM_LR * (m_hat / (jnp.sqrt(v_hat) + ADAM_EPS) + ADAM_WD * w)
    return delta, m, v


def _adamw(w, g, m, v, name):
    r, cols = w.shape
    tr = _row_tile(r, cols * 4)

    def body(w_ref, g_ref, m_ref, v_ref, d_ref, nm_ref, nv_ref):
        d_ref[...], nm_ref[...], nv_ref[...] = _adamw_math(w_ref[...], g_ref[...], m_ref[...], v_ref[...])

    blk = pl.BlockSpec((tr, cols), lambda i: (i, 0))
    shape = jax.ShapeDtypeStruct((r, cols), F32)
    return pl.pallas_call(
        body, name=name, grid=(r // tr,), in_specs=[blk] * 4, out_specs=[blk] * 3, out_shape=[shape] * 3,
        compiler_params=_params(("arbitrary",)),
    )(w, g, m, v)


def _sum_devices(gathered):
    _, r, cols = gathered.shape

    def body(a_ref, g_ref):
        g = a_ref[0]
        for k in range(1, N_DEV):
            g = g + a_ref[k]
        g_ref[...] = g

    return pl.pallas_call(body, name="sum_small_grads", out_shape=jax.ShapeDtypeStruct((r, cols), F32))(gathered)


def _pack_conv(cw):
    flat = cw.reshape(-1)
    return jnp.pad(flat, (0, ROWS_CONV * D_MODEL - flat.shape[0])).reshape(ROWS_CONV, D_MODEL)


def _unpack_conv(rows):
    return rows.reshape(-1)[:CONV_WIDTH * UP_W // N_CHIPS].reshape(CONV_WIDTH, UP_W // N_CHIPS)


def _columns_to_shards(w):
    r, n = w.shape
    return jnp.transpose(w.reshape(r, N_CHIPS, n // N_CHIPS), (1, 0, 2))


def _shards_to_columns(w):
    _, r, n = w.shape
    return jnp.transpose(w, (1, 0, 2)).reshape(r, N_CHIPS * n)


def _pack_small(g_mix_pre, g_mix_post, g_ffn_pre, g_ffn_post, sinks, conv_b, loss):
    pad_row = lambda v: jnp.pad(v.reshape(1, -1), ((0, 0), (0, D_MODEL - v.size)))
    cb = jnp.pad(conv_b.reshape(-1), (0, 6 * D_MODEL - UP_W)).reshape(6, D_MODEL)
    zeros2 = jnp.zeros((2, D_MODEL), F32)
    return jnp.concatenate([g_mix_pre.reshape(1, -1), g_mix_post.reshape(1, -1), g_ffn_pre.reshape(1, -1),
                            g_ffn_post.reshape(1, -1), pad_row(sinks), pad_row(loss), zeros2, cb, zeros2], axis=0)


def _unpack_small(p):
    return dict(mix_pre_norm=p[0:1], mix_post_norm=p[1:2], ffn_pre_norm=p[2:3], ffn_post_norm=p[3:4],
                attn_sinks=p[4:5, :N_ATTN_HEADS], loss=p[5, 0], conv_b=p[8:14].reshape(1, -1)[:, :UP_W],
                conv_w=_unpack_conv(p[SMALL_ROWS:SMALL_ROWS + ROWS_CONV]))


def _local_step(x, target, g_mix_pre, w_in, sinks, w_out, g_mix_post, g_ffn_pre, w_up, conv_w, conv_b, w_down, g_ffn_post,
                distributed=True, rope=None):
    s = x.shape[0]
    consts = _ret_constants()
    sin, cos = _rope_tables(s) if rope is None else rope

    by_half = lambda g, rows: g.reshape(N_CHIPS, 2, rows // (2 * N_CHIPS), g.shape[-1])

    if distributed:
        (h1, proj), level1 = _in_proj(x, g_mix_pre, w_in, _hosted_gather_level1([w_out, w_up, w_down]))
        (mix, states, *kept), (w_out, w_up, w_down) = _mixer_fwd(proj, sinks, sin, cos, consts, _hosted_gather_level2(level1))
        w_out, w_down = w_out.reshape(D_MODEL, D_MODEL), w_down.reshape(D_FF, D_MODEL)
        w_up = w_up.reshape(N_CHIPS, D_MODEL, UP_W // N_CHIPS)
    else:
        (h1, proj), _ = _in_proj(x, g_mix_pre, w_in)
        (mix, states, *kept), _ = _mixer_fwd(proj, sinks, sin, cos, consts)
    mixed, x1, h2, u0 = _out_up_proj(mix, x, w_out, g_mix_post, g_ffn_pre, w_up)
    y, dy2, dout, du, conv_acc, tail_acc = _ffn_tail(u0, x1, target, conv_w, conv_b, w_down, g_ffn_post)
    du0, dx1, dmixed, dmix, head_acc = _ffn_head_bwd(du, conv_w, w_up, x1, g_ffn_pre, dout, mixed, g_mix_post, w_out)

    d_w_down = _weight_grad(y, dy2, 512, "grad_w_down")
    d_w_up = _weight_grad(h2, du0, UP_W // N_CHIPS, "grad_w_up", by_block=True)
    d_w_out = _weight_grad(mix, dmixed, D_MODEL, "grad_w_out")
    early = [by_half(d_w_down, D_FF), by_half(d_w_up, N_CHIPS * D_MODEL), by_half(d_w_out, D_MODEL)]
    (dproj, dsinks), early_lands = _mixer_bwd(proj, dmix, states, kept, sin, cos, consts,
                                              _hosted_scatter(early) if distributed else None)
    d_w_in_t = _weight_grad(dproj, h1, 512, "grad_w_in")
    late = [by_half(d_w_in_t, IN_W)]
    (grad_x, in_acc), late_lands = _in_proj_bwd(dproj, w_in, x, g_mix_pre, dx1, _hosted_scatter(late) if distributed else None)

    small = _pack_small(in_acc[0], head_acc[1], head_acc[0], tail_acc[0], dsinks[0, :N_ATTN_HEADS], conv_acc[3],
                        jnp.sum(tail_acc[1]))
    d_conv = jnp.pad(conv_acc[0:CONV_WIDTH].reshape(-1), (0, CONV_FULL_ROWS * D_MODEL - CONV_WIDTH * UP_W))
    small = jnp.concatenate([small, d_conv.reshape(CONV_FULL_ROWS, D_MODEL)], axis=0)
    grads = dict(w_down=early[0], w_up=early[1], w_out=early[2], w_in=late[0])
    lands = dict(zip(["w_down", "w_up", "w_out", "w_in"], early_lands + late_lands))
    return grad_x, grads, lands, small


def kernel(x, mix_pre_norm, w_in, attn_sinks, w_out, mix_post_norm, ffn_pre_norm, w_up, conv_w, conv_b, w_down, ffn_post_norm, loss_target, m_mix_pre_norm, m_w_in, m_attn_sinks, m_w_out, m_mix_post_norm, m_ffn_pre_norm, m_w_up, m_conv_w, m_conv_b, m_w_down, m_ffn_post_norm, v_mix_pre_norm, v_w_in, v_attn_sinks, v_w_out, v_mix_post_norm, v_ffn_pre_norm, v_w_up, v_conv_w, v_conv_b, v_w_down, v_ffn_post_norm):
    cx, cy, cc = _place()
    shard = 2 * cx + cy

    conv_rows = jnp.pad(conv_w[0], ((0, 16 - CONV_WIDTH), (0, 0)))
    w_in_t = jnp.swapaxes(w_in[0], 0, 1)
    (w_in_all, conv_all), sin, cos = _gather_now([w_in_t.astype(BF16), conv_rows], "gather_w_in", x.shape[1])
    conv_full = _shards_to_columns(conv_all[:, 0, :CONV_WIDTH])

    grad_x, grads, lands, small = _local_step(
        x[0], loss_target[0], mix_pre_norm, w_in_all.reshape(IN_W, D_MODEL), attn_sinks.reshape(-1), w_out[0].astype(BF16),
        mix_post_norm, ffn_pre_norm, w_up[0].astype(BF16), conv_full, conv_b, w_down[0].astype(BF16), ffn_post_norm,
        rope=(sin, cos))

    place = jnp.stack([shard, cc]).astype(jnp.int32)
    mats = ["w_in", "w_out", "w_up", "w_down"]
    halves = [_sum_pieces(grads[n], lands[n], place, "sum_grad_" + n) for n in mats]
    weights = dict(w_in=(w_in, m_w_in, v_w_in), w_out=(w_out, m_w_out, v_w_out), w_up=(w_up, m_w_up, v_w_up),
                   w_down=(w_down, m_w_down, v_w_down))
    mat_out = {}
    joined_all, small_all = _join_halves(halves, small)
    for n, joined in zip(mats, joined_all):
        w, m, v = (t[0] for t in weights[n])
        if n == "w_in":
            w, m, v = (jnp.swapaxes(t, 0, 1) for t in (w, m, v))
        res = (joined.reshape(w.shape),) + tuple(_adamw(w, joined.reshape(w.shape), m, v, "adamw_" + n))
        mat_out[n] = tuple(jnp.swapaxes(t, 0, 1) for t in res) if n == "w_in" else res

    small_sum = _sum_devices(small_all)
    d_conv_full = small_sum[SMALL_ROWS:].reshape(-1)[:CONV_WIDTH * UP_W].reshape(CONV_WIDTH, UP_W)
    d_conv_mine = lax.dynamic_slice_in_dim(d_conv_full, shard * (UP_W // N_CHIPS), UP_W // N_CHIPS, axis=1)
    g_s = jnp.concatenate([small_sum[:SMALL_ROWS], _pack_conv(d_conv_mine)], axis=0)
    zero = jnp.zeros((), F32)
    pack_rep = lambda a, b, c_, d, e, f, cw: jnp.concatenate([_pack_small(a, b, c_, d, e, f, zero), _pack_conv(cw[0])], axis=0)
    w_s = pack_rep(mix_pre_norm, mix_post_norm, ffn_pre_norm, ffn_post_norm, attn_sinks, conv_b, conv_w)
    m_s = pack_rep(m_mix_pre_norm, m_mix_post_norm, m_ffn_pre_norm, m_ffn_post_norm, m_attn_sinks, m_conv_b, m_conv_w)
    v_s = pack_rep(v_mix_pre_norm, v_mix_post_norm, v_ffn_pre_norm, v_ffn_post_norm, v_attn_sinks, v_conv_b, v_conv_w)
    delta_s, new_m_s, new_v_s = _adamw(w_s, g_s, m_s, v_s, "adamw_small")

    names = ["mix_pre_norm", "w_in", "attn_sinks", "w_out", "mix_post_norm", "ffn_pre_norm", "w_up", "conv_w", "conv_b",
             "w_down", "ffn_post_norm"]

    def leaves(which, packed_small):
        smalls = _unpack_small(packed_small)
        return [mat_out[n][which][None] if n in mat_out else (smalls[n][None] if n == "conv_w" else smalls[n]) for n in names]

    loss = _unpack_small(g_s)["loss"]
    return (loss, grad_x[None], *leaves(0, g_s), *leaves(1, delta_s), *leaves(2, new_m_s), *leaves(3, new_v_s))
```

```python
import math

import jax
import jax.numpy as jnp
from jax import lax
from jax.experimental import pallas as pl
from jax.experimental.pallas import tpu as pltpu

F32 = jnp.float32
BF16 = jnp.bfloat16

D_MODEL = 1024
HEAD_DIM = 64
ATTN_W = 512
N_ATTN_HEADS = 8
KV_W = 128
RET_W = 512
N_RET_HEADS = 4
RET_HEAD_DIM = 128
CHUNK = 128
IN_W = 2816
D_FF = 2816
UP_W = 2 * D_FF
CONV_WIDTH = 3
RMS_EPS = 1e-6
GN_EPS = 1e-6
MASK_VALUE = -1e30
ATTN_SCALE = HEAD_DIM ** -0.5
RET_K_SCALE = RET_HEAD_DIM ** -0.5
GELU_C = math.sqrt(2.0 / math.pi)
GELU_A = 0.044715

ADAM_LR = 0.001
ADAM_B1 = 0.9
ADAM_B2 = 0.999
ADAM_EPS = 1e-08
ADAM_WD = 0.01
ADAM_STEP = 10

N_CHIPS = 4
N_DEV = 8
MESH = pl.DeviceIdType.MESH
VMEM_LIMIT_V7X = 56 * 1024 * 1024
TOKEN_TILE = 256
BIG_TOKEN_TILE = 512
IN_PROJ_TOKEN_TILE = 1024
WEIGHT_GRAD_TOKENS = 2048
FFN_ROW_BLOCK = 64
HEAD_BWD_COLS = 512
MIXER_CHUNKS_PER_STEP = 4
Q_A0, KV_A0, Q_R0, K_R0, V_R0, G_R0 = 0, 512, 768, 1280, 1792, 2304

ROWS_CONV = 8
SMALL_ROWS = 16
CONV_FULL_ROWS = 24


def _params(sem=None, **kw):
    if sem is not None:
        kw["dimension_semantics"] = sem
    return pltpu.CompilerParams(vmem_limit_bytes=VMEM_LIMIT_V7X, **kw)


def _resident(shape):
    zeros = (0,) * len(shape)
    return pl.BlockSpec(shape, lambda *_: zeros, pipeline_mode=pl.Buffered(1))


class _Hosted:
    def __init__(self, ins, outs, aliases, n_pairs, n_local, start, finish):
        self.ins, self.outs, self.aliases = list(ins), list(outs), dict(aliases)
        self.n_pairs, self.n_local, self.start, self.finish = n_pairs, max(n_local, 1), start, finish


def _hosted_call(compute, *, name, grid, in_specs, out_specs, out_shape, scratch_shapes, args, hosted=None):
    params = _params(("arbitrary",) * len(grid))
    if hosted is None:
        res = pl.pallas_call(compute, name=name, grid=grid, in_specs=in_specs, out_specs=out_specs, out_shape=out_shape,
                             scratch_shapes=scratch_shapes, compiler_params=params)(*args)
        return list(res), []
    n_in, n_out, n_scr = len(in_specs), len(out_specs), len(scratch_shapes)
    h_in, h_out = len(hosted.ins), len(hosted.outs)

    def at(step_of):
        cond = pl.program_id(0) == step_of(grid[0])
        for d in range(1, len(grid)):
            cond = jnp.logical_and(cond, pl.program_id(d) == step_of(grid[d]))
        return cond

    def body(*refs):
        ins, refs = refs[:n_in], refs[n_in:]
        h_ins, refs = refs[:h_in], refs[h_in:]
        outs, refs = refs[:n_out], refs[n_out:]
        h_outs, refs = refs[:h_out], refs[h_out:]
        scr, sems = refs[:n_scr], refs[n_scr:]

        @pl.when(at(lambda n: 0))
        def _():
            hosted.start(h_ins, h_outs, *sems)

        compute(*ins, *outs, *scr)

        @pl.when(at(lambda n: n - 1))
        def _():
            hosted.finish(h_ins, h_outs, *sems)

    hbm = pl.BlockSpec(memory_space=pl.ANY)
    res = pl.pallas_call(
        body, name=name, grid=grid,
        in_specs=list(in_specs) + [hbm] * h_in, out_specs=list(out_specs) + [hbm] * h_out,
        out_shape=list(out_shape) + hosted.outs,
        scratch_shapes=list(scratch_shapes) + [pltpu.SemaphoreType.DMA((hosted.n_pairs,)), pltpu.SemaphoreType.DMA((hosted.n_pairs,)),
                                               pltpu.SemaphoreType.DMA((hosted.n_local,))],
        input_output_aliases={n_in + a: n_out + b for a, b in hosted.aliases.items()},
        compiler_params=params,
    )(*args, *hosted.ins)
    return list(res[:n_out]), list(res[n_out:])


def _dot(a, b):
    return jnp.dot(a, b, preferred_element_type=F32)


def _dot_nt(a, b):
    return lax.dot_general(a, b, (((1,), (1,)), ((), ())), preferred_element_type=F32)


def _dot_tn(a, b):
    return lax.dot_general(a, b, (((0,), (0,)), ((), ())), preferred_element_type=F32)


def _shift_matrix(n, by):
    row = lax.broadcasted_iota(jnp.int32, (n, n), 0)
    col = lax.broadcasted_iota(jnp.int32, (n, n), 1)
    return jnp.where(col == row + by, 1.0, 0.0).astype(BF16)


def _rstd(v):
    return lax.rsqrt(jnp.mean(v * v, axis=-1, keepdims=True) + RMS_EPS)


def _rms_bwd(dy, v, rstd, gain):
    n = v * rstd
    dgain = jnp.sum(dy * n, axis=0, keepdims=True)
    dn = dy * gain
    dv = rstd * (dn - n * jnp.mean(dn * n, axis=-1, keepdims=True))
    return dv, dgain


def _lane_lo(shape):
    return (lax.broadcasted_iota(jnp.int32, shape, 1) % 128) < HEAD_DIM


GROUP = N_ATTN_HEADS // (KV_W // HEAD_DIM)


def _attn_bias(first_chunk):
    qi = lax.broadcasted_iota(jnp.int32, (GROUP * CHUNK, 2 * CHUNK), 0) % CHUNK
    kj = lax.broadcasted_iota(jnp.int32, (GROUP * CHUNK, 2 * CHUNK), 1)
    valid = jnp.logical_and(kj > qi, kj <= qi + CHUNK)
    if first_chunk:
        valid = jnp.logical_and(valid, kj >= CHUNK)
    return jnp.where(valid, 0.0, MASK_VALUE)


def _half(shape, hk):
    lo = _lane_lo(shape)
    return lo if hk == 0 else jnp.logical_not(lo)


class _GroupMasks:
    def __init__(self, sk_ref):
        groups = range(KV_W // HEAD_DIM)
        self.q = [_half((CHUNK, 128), hk) for hk in groups]
        self.kv = [_half((2 * CHUNK, 128), hk) for hk in groups]
        self.sinks = [_group_sinks(sk_ref, hk) for hk in groups]


def _stack_heads(ref, row0, col0, hk, half):
    parts = []
    for j in range(GROUP):
        h = GROUP * hk + j
        pair = ref[row0:row0 + CHUNK, col0 + (h // 2) * 128:col0 + (h // 2 + 1) * 128].astype(F32)
        if h % 2 != hk:
            pair = pltpu.roll(pair, HEAD_DIM, 1)
        parts.append(jnp.where(half, pair, 0.0))
    return jnp.concatenate(parts, axis=0)


def _unstack_heads(stacked, hk):
    pairs = []
    for q in range(GROUP // 2):
        even, odd = stacked[2 * q * CHUNK:(2 * q + 1) * CHUNK], stacked[(2 * q + 1) * CHUNK:(2 * q + 2) * CHUNK]
        pairs.append(even + pltpu.roll(odd, HEAD_DIM, 1) if hk == 0 else pltpu.roll(even, HEAD_DIM, 1) + odd)
    return pairs


def _group_sinks(sk_ref, hk):
    row = lax.broadcasted_iota(jnp.int32, (GROUP * CHUNK, 1), 0)
    col = jnp.full((GROUP * CHUNK, 1), sk_ref[GROUP * hk], F32)
    for j in range(1, GROUP):
        col = jnp.where(row >= j * CHUNK, sk_ref[GROUP * hk + j], col)
    return col


def _attn_probs(q_b, kk_b, bias, sink):
    s = _dot_nt(q_b, kk_b) * ATTN_SCALE + bias
    m = jnp.maximum(jnp.max(s, axis=-1, keepdims=True), sink)
    e = jnp.exp(s - m)
    e_sink = jnp.exp(sink - m)
    inv = 1.0 / (jnp.sum(e, axis=-1, keepdims=True) + e_sink)
    return e * inv, e_sink * inv


def _even_lanes(shape):
    return (lax.broadcasted_iota(jnp.int32, shape, 1) % 2) == 0


def _swap2(v, even):
    return jnp.where(even, pltpu.roll(v, v.shape[1] - 1, 1), pltpu.roll(v, 1, 1))


def _tile4(v):
    return jnp.concatenate([v, v, v, v], axis=-1)


def _sigmoid(v):
    return 1.0 / (1.0 + jnp.exp(-v))


def _ret_constants():
    h = N_RET_HEADS
    log_gamma = jnp.log(1.0 - jnp.power(2.0, -5.0 - jnp.arange(h, dtype=F32)))
    idx = jnp.arange(CHUNK, dtype=F32)
    rel = idx[:, None] - idx[None, :]
    d_intra = jnp.where(rel[None] >= 0, jnp.exp(log_gamma[:, None, None] * jnp.maximum(rel, 0.0)[None]), 0.0)
    xi = jnp.exp(log_gamma[None, :] * (idx[:, None] + 1.0))
    zeta = jnp.exp(log_gamma[None, :] * (CHUNK - 1.0 - idx[:, None]))
    decay = jnp.exp(log_gamma * CHUNK)
    xi_full = jnp.repeat(xi, RET_HEAD_DIM, axis=1)
    zeta_full = jnp.repeat(zeta, RET_HEAD_DIM, axis=1)
    decay_full = jnp.broadcast_to(jnp.repeat(decay, RET_HEAD_DIM)[None, :], (8, RET_W))
    return d_intra.astype(F32), xi_full.astype(F32), zeta_full.astype(F32), decay_full.astype(F32)


def _rope_tables(s):
    pos = jnp.arange(s, dtype=F32)
    angle = 1.0 / jnp.power(10000.0, jnp.linspace(0.0, 1.0, RET_HEAD_DIM // 2, dtype=F32))
    angle = jnp.repeat(angle, 2)
    sign = jnp.where(jnp.arange(RET_HEAD_DIM) % 2 == 0, -1.0, 1.0).astype(F32)
    return jnp.sin(pos[:, None] * angle[None]) * sign[None], jnp.cos(pos[:, None] * angle[None])


def _in_proj(x, gain, w_in_t, hosted=None):
    s = x.shape[0]
    tm = min(IN_PROJ_TOKEN_TILE, s)

    def body(x_ref, g_ref, w_ref, h_ref, p_ref):
        xv = x_ref[...]
        h = (xv * _rstd(xv) * g_ref[...]).astype(BF16)
        h_ref[...] = h
        p_ref[...] = _dot_nt(h, w_ref[...])

    return _hosted_call(
        body, name="in_proj", grid=(s // tm,),
        in_specs=[pl.BlockSpec((tm, D_MODEL), lambda i: (i, 0)), _resident((1, D_MODEL)), _resident((IN_W, D_MODEL))],
        out_specs=[pl.BlockSpec((tm, D_MODEL), lambda i: (i, 0)), pl.BlockSpec((tm, IN_W), lambda i: (i, 0))],
        out_shape=[jax.ShapeDtypeStruct((s, D_MODEL), BF16), jax.ShapeDtypeStruct((s, IN_W), F32)],
        scratch_shapes=[], args=(x, gain, w_in_t), hosted=hosted)


def _mixer_fwd(proj, sinks, sin, cos, consts, hosted=None):
    s = proj.shape[0]
    nc = s // CHUNK
    cps = MIXER_CHUNKS_PER_STEP
    groups = KV_W // HEAD_DIM
    d_intra, xi_full, zeta_full, decay_full = consts

    def body(sk_ref, p_ref, pkv_ref, sin_ref, cos_ref, dm_ref, xi_ref, ze_ref, dc_ref,
             mix_ref, st_ref, pr_ref, ps_ref, ra_ref, on_ref, rs_ref, rq_ref, rk_ref, rz_ref, sq_ref, sg_ref, state):
        i = pl.program_id(0)

        @pl.when(i == 0)
        def _():
            state[...] = jnp.zeros_like(state)

        st = [state[h] for h in range(N_RET_HEADS)]
        bias_any = _attn_bias(False)
        bias_c0 = jnp.where(i == 0, _attn_bias(True), bias_any)
        even = _even_lanes((CHUNK, RET_W))
        masks = _GroupMasks(sk_ref)
        for c in range(cps):
            r0 = c * CHUNK
            rows = slice(r0, r0 + CHUNK)

            kv_cur = p_ref[rows, KV_A0:KV_A0 + 2 * KV_W]
            kv_prev = pkv_ref[...] if c == 0 else p_ref[r0 - CHUNK:r0, KV_A0:KV_A0 + 2 * KV_W]
            kk = jnp.concatenate([kv_prev[:, :KV_W], kv_cur[:, :KV_W]], axis=0)
            vv = jnp.concatenate([kv_prev[:, KV_W:], kv_cur[:, KV_W:]], axis=0)
            kk_b = kk.astype(BF16)
            bias = bias_c0 if c == 0 else bias_any
            for hk in range(KV_W // HEAD_DIM):
                q_b = _stack_heads(p_ref, r0, Q_A0, hk, masks.q[hk]).astype(BF16)
                p, p_sink = _attn_probs(q_b, kk_b, bias, masks.sinks[hk])
                p_b = p.astype(BF16)
                pr_ref[c, hk] = p_b
                ps_ref[c, hk] = p_sink
                sq_ref[c, hk] = q_b
                v_b = jnp.where(masks.kv[hk], vv, 0.0).astype(BF16)
                for q, pair in enumerate(_unstack_heads(_dot(p_b, v_b), hk)):
                    pi = (GROUP // 2) * hk + q
                    mix_ref[rows, pi * 128:(pi + 1) * 128] = pair.astype(BF16)

            sin4, cos4 = _tile4(sin_ref[rows, :]), _tile4(cos_ref[rows, :])
            q_r = p_ref[rows, Q_R0:Q_R0 + RET_W]
            k_r = p_ref[rows, K_R0:K_R0 + RET_W] * RET_K_SCALE
            q_r = q_r * cos4 + _swap2(q_r, even) * sin4
            k_r = k_r * cos4 + _swap2(k_r, even) * sin4
            q_b, k_b, kz_b = q_r.astype(BF16), k_r.astype(BF16), (k_r * ze_ref[...]).astype(BF16)
            rq_ref[rows, :], rk_ref[rows, :], rz_ref[rows, :] = q_b, k_b, kz_b
            for h in range(N_RET_HEADS):
                sl = slice(h * RET_HEAD_DIM, (h + 1) * RET_HEAD_DIM)
                qh, kh = q_b[:, sl], k_b[:, sl]
                vh = p_ref[rows, V_R0 + h * RET_HEAD_DIM:V_R0 + (h + 1) * RET_HEAD_DIM].astype(BF16)
                st_ref[c, h] = st[h]
                a_b = (_dot_nt(qh, kh) * dm_ref[h]).astype(BF16)
                qx = (q_r[:, sl] * xi_ref[:, sl]).astype(BF16)
                o = _dot(jnp.concatenate([a_b, qx], axis=1), jnp.concatenate([vh, st[h].astype(BF16)], axis=0))
                st[h] = dc_ref[0:1, sl] * st[h] + _dot_tn(kz_b[:, sl], vh)
                mu = jnp.mean(o, axis=-1, keepdims=True)
                oc = o - mu
                rs = lax.rsqrt(jnp.mean(oc * oc, axis=-1, keepdims=True) + GN_EPS)
                on = oc * rs
                ra_ref[c, h], on_ref[rows, sl], rs_ref[c, h] = a_b, on, rs
                g = p_ref[rows, G_R0 + h * RET_HEAD_DIM:G_R0 + (h + 1) * RET_HEAD_DIM]
                sg = _sigmoid(g)
                sg_ref[rows, sl] = sg
                mix_ref[rows, ATTN_W + h * RET_HEAD_DIM:ATTN_W + (h + 1) * RET_HEAD_DIM] = (g * sg * on).astype(BF16)
        for h in range(N_RET_HEADS):
            state[h] = st[h]

    return _hosted_call(
        body, name="mixer_fwd", grid=(nc // cps,),
        in_specs=[
            pl.BlockSpec(memory_space=pltpu.SMEM),
            pl.BlockSpec((cps * CHUNK, IN_W), lambda i: (i, 0)),
            pl.BlockSpec((CHUNK, 2 * KV_W), lambda i: (jnp.maximum(cps * i - 1, 0), KV_A0 // (2 * KV_W))),
            pl.BlockSpec((cps * CHUNK, RET_HEAD_DIM), lambda i: (i, 0)),
            pl.BlockSpec((cps * CHUNK, RET_HEAD_DIM), lambda i: (i, 0)),
            _resident((N_RET_HEADS, CHUNK, CHUNK)), _resident((CHUNK, RET_W)), _resident((CHUNK, RET_W)), _resident((8, RET_W)),
        ],
        out_specs=[
            pl.BlockSpec((cps * CHUNK, D_MODEL), lambda i: (i, 0)),
            pl.BlockSpec((cps, N_RET_HEADS, RET_HEAD_DIM, RET_HEAD_DIM), lambda i: (i, 0, 0, 0)),
            pl.BlockSpec((cps, groups, GROUP * CHUNK, 2 * CHUNK), lambda i: (i, 0, 0, 0)),
            pl.BlockSpec((cps, groups, GROUP * CHUNK, 1), lambda i: (i, 0, 0, 0)),
            pl.BlockSpec((cps, N_RET_HEADS, CHUNK, CHUNK), lambda i: (i, 0, 0, 0)),
            pl.BlockSpec((cps * CHUNK, RET_W), lambda i: (i, 0)),
            pl.BlockSpec((cps, N_RET_HEADS, CHUNK, 1), lambda i: (i, 0, 0, 0)),
        ] + [pl.BlockSpec((cps * CHUNK, RET_W), lambda i: (i, 0))] * 3 + [
            pl.BlockSpec((cps, groups, GROUP * CHUNK, 128), lambda i: (i, 0, 0, 0)),
            pl.BlockSpec((cps * CHUNK, RET_W), lambda i: (i, 0))],
        out_shape=[jax.ShapeDtypeStruct((s, D_MODEL), BF16),
                   jax.ShapeDtypeStruct((nc, N_RET_HEADS, RET_HEAD_DIM, RET_HEAD_DIM), F32),
                   jax.ShapeDtypeStruct((nc, groups, GROUP * CHUNK, 2 * CHUNK), BF16),
                   jax.ShapeDtypeStruct((nc, groups, GROUP * CHUNK, 1), F32),
                   jax.ShapeDtypeStruct((nc, N_RET_HEADS, CHUNK, CHUNK), BF16),
                   jax.ShapeDtypeStruct((s, RET_W), F32),
                   jax.ShapeDtypeStruct((nc, N_RET_HEADS, CHUNK, 1), F32)] + [jax.ShapeDtypeStruct((s, RET_W), BF16)] * 3 + [
                   jax.ShapeDtypeStruct((nc, groups, GROUP * CHUNK, 128), BF16), jax.ShapeDtypeStruct((s, RET_W), F32)],
        scratch_shapes=[pltpu.VMEM((N_RET_HEADS, RET_HEAD_DIM, RET_HEAD_DIM), F32)],
        args=(sinks, proj, proj, sin, cos, d_intra, xi_full, zeta_full, decay_full), hosted=hosted)


def _out_up_proj(mix, x, w_out, g_post, g_pre, w_up):
    s = x.shape[0]
    tm = min(BIG_TOKEN_TILE, s)
    blk = UP_W // N_CHIPS

    def body(mix_ref, x_ref, wo_ref, g2_ref, g3_ref, wu_ref, mixed_ref, x1_ref, h2_ref, u0_ref):
        mixed = _dot(mix_ref[...], wo_ref[...])
        mixed_ref[...] = mixed
        x1 = x_ref[...] + mixed * _rstd(mixed) * g2_ref[...]
        x1_ref[...] = x1
        h2 = (x1 * _rstd(x1) * g3_ref[...]).astype(BF16)
        h2_ref[...] = h2
        for k in range(N_CHIPS):
            u0_ref[:, k * blk:(k + 1) * blk] = _dot(h2, wu_ref[k]).astype(BF16)

    tok = lambda w: pl.BlockSpec((tm, w), lambda i: (i, 0))
    return pl.pallas_call(
        body, name="out_up_proj", grid=(s // tm,),
        in_specs=[tok(D_MODEL), tok(D_MODEL), _resident((D_MODEL, D_MODEL)), _resident((1, D_MODEL)), _resident((1, D_MODEL)),
                  _resident((N_CHIPS, D_MODEL, blk))],
        out_specs=[tok(D_MODEL), tok(D_MODEL), tok(D_MODEL), tok(UP_W)],
        out_shape=[jax.ShapeDtypeStruct((s, D_MODEL), F32), jax.ShapeDtypeStruct((s, D_MODEL), F32),
                   jax.ShapeDtypeStruct((s, D_MODEL), BF16), jax.ShapeDtypeStruct((s, UP_W), BF16)],
        compiler_params=_params(("arbitrary",)),
    )(mix, x, w_out, g_post, g_pre, w_up)


def _ffn_tail(u0, x1, target, conv_w, conv_b, w_down, g_post):
    s = x1.shape[0]
    tm = TOKEN_TILE
    last = s // tm - 1
    rb, lanes = FFN_ROW_BLOCK, 128

    def body(u0_ref, x1_ref, t_ref, cw_ref, cb_ref, wd_ref, g_ref,
             y_ref, dy2_ref, dout_ref, du_ref, cacc_ref, gacc_ref, u1_s, u2_s, carry, gelu_s, slope_s, dy_s, cacc):
        i = pl.program_id(0)

        @pl.when(i == 0)
        def _():
            carry[...] = jnp.zeros_like(carry)
            cacc[...] = jnp.zeros_like(cacc)
            gacc_ref[...] = jnp.zeros_like(gacc_ref)

        shift1, shift2 = _shift_matrix(tm, -1), _shift_matrix(tm, -2)
        r8 = lax.broadcasted_iota(jnp.int32, (8, 1), 0)
        wide = 2 * lanes

        def shift_block(col):
            cols = slice(col, col + wide)
            u1_s[:, cols] = _dot(shift1, u0_ref[:, cols])
            u2_s[:, cols] = _dot(shift2, u0_ref[:, cols])
            c14, c15 = carry[14:15, cols], carry[15:16, cols]
            u1_s[0:8, cols] = jnp.where(r8 == 0, c15, u1_s[0:8, cols])
            u2_s[0:8, cols] = jnp.where(r8 == 0, c14, jnp.where(r8 == 1, c15, u2_s[0:8, cols]))

        def taps(col):
            return (cw_ref[0:1, col:col + lanes], cw_ref[1:2, col:col + lanes], cw_ref[2:3, col:col + lanes],
                    cb_ref[0:1, col:col + lanes])

        def shifted(r0, col):
            return (u2_s[r0:r0 + rb, col:col + lanes], u1_s[r0:r0 + rb, col:col + lanes],
                    u0_ref[r0:r0 + rb, col:col + lanes].astype(F32))

        def conv(r0, col, w):
            u2, u1, uc = shifted(r0, col)
            return w[0] * u2 + w[1] * u1 + w[2] * uc + w[3]

        fold = lambda v: jnp.sum(v.reshape(rb // 8, 8, lanes), axis=0)

        shift_block(0)
        shift_block(D_FF)
        for j in range(D_FF // lanes):
            cg, cv = j * lanes, D_FF + j * lanes
            if cg % wide == 0 and cg + wide < D_FF:
                shift_block(cg + wide)
                shift_block(cv + wide)
            wg, wv = taps(cg), taps(cv)
            for r0 in range(0, tm, rb):
                gate, val = conv(r0, cg, wg), conv(r0, cv, wv)
                g2 = gate * gate
                th = jnp.tanh(gate * (GELU_C + GELU_C * GELU_A * g2))
                hp = 0.5 * th + 0.5
                gelu = gate * hp
                dgelu = hp + gate * (1.0 - th * th) * (0.5 * GELU_C + 1.5 * GELU_C * GELU_A * g2)
                y_ref[r0:r0 + rb, cg:cg + lanes] = (gelu * val).astype(BF16)
                gelu_s[r0:r0 + rb, cg:cg + lanes] = gelu
                slope_s[r0:r0 + rb, cg:cg + lanes] = dgelu * val

        y2 = _dot(y_ref[...], wd_ref[...])
        r4 = _rstd(y2)
        gain = g_ref[...]
        out = x1_ref[...] + y2 * r4 * gain
        diff = out - t_ref[...]
        dout = diff * (1.0 / D_MODEL)
        dout_ref[...] = dout
        dy2, dgain = _rms_bwd(dout, y2, r4, gain)
        dy2_b = dy2.astype(BF16)
        dy2_ref[...] = dy2_b
        gacc_ref[0:1, :] += dgain
        gacc_ref[1:2, :] += 0.5 * jnp.sum(diff * dout, axis=0, keepdims=True)
        carry[...] = u0_ref[tm - 16:tm, :].astype(F32)

        dy_s[:, 0:wide] = _dot_nt(dy2_b, wd_ref[0:wide, :])
        for j in range(D_FF // lanes):
            cg, cv = j * lanes, D_FF + j * lanes
            if cg % wide == 0 and cg + wide < D_FF:
                dy_s[:, cg + wide:cg + 2 * wide] = _dot_nt(dy2_b, wd_ref[cg + wide:cg + 2 * wide, :])
            acc = [[jnp.zeros((8, lanes), F32) for _ in range(CONV_WIDTH + 1)] for _ in range(2)]
            for r0 in range(0, tm, rb):
                dy = dy_s[r0:r0 + rb, cg:cg + lanes]
                d_gate = dy * slope_s[r0:r0 + rb, cg:cg + lanes]
                d_val = dy * gelu_s[r0:r0 + rb, cg:cg + lanes]
                for side, (col, d) in enumerate(((cg, d_gate), (cv, d_val))):
                    du_ref[r0:r0 + rb, col:col + lanes] = d.astype(BF16)
                    for k, u in enumerate(shifted(r0, col)):
                        acc[side][k] = acc[side][k] + fold(d * u)
                    acc[side][CONV_WIDTH] = acc[side][CONV_WIDTH] + fold(d)
            for side, col in enumerate((cg, cv)):
                for k in range(CONV_WIDTH + 1):
                    cacc[8 * k:8 * k + 8, col:col + lanes] += acc[side][k]

        @pl.when(i == last)
        def _():
            for k in range(CONV_WIDTH + 1):
                cacc_ref[k:k + 1, :] = jnp.sum(cacc[8 * k:8 * k + 8, :], axis=0, keepdims=True)
            cacc_ref[CONV_WIDTH + 1:8, :] = jnp.zeros((8 - CONV_WIDTH - 1, UP_W), F32)

    tok = lambda w: pl.BlockSpec((tm, w), lambda i: (i, 0))
    return pl.pallas_call(
        body, name="ffn_tail", grid=(s // tm,),
        in_specs=[tok(UP_W), tok(D_MODEL), tok(D_MODEL), _resident((CONV_WIDTH, UP_W)), _resident((1, UP_W)),
                  _resident((D_FF, D_MODEL)), _resident((1, D_MODEL))],
        out_specs=[tok(D_FF), tok(D_MODEL), tok(D_MODEL), tok(UP_W),
                   pl.BlockSpec((8, UP_W), lambda i: (0, 0)), pl.BlockSpec((8, D_MODEL), lambda i: (0, 0))],
        out_shape=[jax.ShapeDtypeStruct((s, D_FF), BF16), jax.ShapeDtypeStruct((s, D_MODEL), BF16),
                   jax.ShapeDtypeStruct((s, D_MODEL), F32), jax.ShapeDtypeStruct((s, UP_W), BF16),
                   jax.ShapeDtypeStruct((8, UP_W), F32), jax.ShapeDtypeStruct((8, D_MODEL), F32)],
        scratch_shapes=[pltpu.VMEM((tm, UP_W), F32), pltpu.VMEM((tm, UP_W), F32), pltpu.VMEM((16, UP_W), F32),
                        pltpu.VMEM((tm, D_FF), F32), pltpu.VMEM((tm, D_FF), F32),
                        pltpu.VMEM((tm, D_FF), F32), pltpu.VMEM((8 * (CONV_WIDTH + 1), UP_W), F32)],
        compiler_params=_params(("arbitrary",)),
    )(u0, x1, target, conv_w, conv_b, w_down, g_post)


def _ffn_head_bwd(du, conv_w, w_up, x1, g_pre, dout, mixed, g_post, w_out):
    s = x1.shape[0]
    tm = TOKEN_TILE
    nt = s // tm
    blk = UP_W // N_CHIPS

    def body(du_ref, halo_ref, cw_ref, wu_ref, x1_ref, g3_ref, dout_ref, mixed_ref, g2_ref, wo_ref,
             du0_ref, dx1_ref, dmixed_ref, dmix_ref, gacc_ref, dbuf):
        i = pl.program_id(0)

        @pl.when(i == 0)
        def _():
            gacc_ref[...] = jnp.zeros_like(gacc_ref)

        dbuf[0:tm, :] = du_ref[...].astype(F32)
        dbuf[tm:tm + 16, :] = jnp.where(i < nt - 1, halo_ref[...].astype(F32), 0.0)
        dh2 = jnp.zeros((tm, D_MODEL), F32)
        for k in range(N_CHIPS):
            for c0 in range(0, blk, HEAD_BWD_COLS):
                width = min(HEAD_BWD_COLS, blk - c0)
                cols = slice(k * blk + c0, k * blk + c0 + width)
                du0_b = (cw_ref[2:3, cols] * dbuf[0:tm, cols] + cw_ref[1:2, cols] * dbuf[1:1 + tm, cols]
                         + cw_ref[0:1, cols] * dbuf[2:2 + tm, cols]).astype(BF16)
                du0_ref[:, cols] = du0_b
                dh2 = dh2 + _dot_nt(du0_b, wu_ref[k, :, c0:c0 + width])
        x1 = x1_ref[...]
        d3, dg3 = _rms_bwd(dh2, x1, _rstd(x1), g3_ref[...])
        dx1 = dout_ref[...] + d3
        dx1_ref[...] = dx1
        mixed = mixed_ref[...]
        dmixed, dg2 = _rms_bwd(dx1, mixed, _rstd(mixed), g2_ref[...])
        dmixed_b = dmixed.astype(BF16)
        dmixed_ref[...] = dmixed_b
        dmix_ref[...] = _dot_nt(dmixed_b, wo_ref[...]).astype(BF16)
        gacc_ref[0:1, :] += dg3
        gacc_ref[1:2, :] += dg2

    tok = lambda w: pl.BlockSpec((tm, w), lambda i: (i, 0))
    halo = pl.BlockSpec((16, UP_W), lambda i: (jnp.minimum(i + 1, nt - 1) * (tm // 16), 0))
    return pl.pallas_call(
        body, name="ffn_head_bwd", grid=(nt,),
        in_specs=[tok(UP_W), halo, _resident((CONV_WIDTH, UP_W)), _resident((N_CHIPS, D_MODEL, blk)), tok(D_MODEL),
                  _resident((1, D_MODEL)), tok(D_MODEL), tok(D_MODEL), _resident((1, D_MODEL)), _resident((D_MODEL, D_MODEL))],
        out_specs=[tok(UP_W), tok(D_MODEL), tok(D_MODEL), tok(D_MODEL), pl.BlockSpec((8, D_MODEL), lambda i: (0, 0))],
        out_shape=[jax.ShapeDtypeStruct((s, UP_W), BF16), jax.ShapeDtypeStruct((s, D_MODEL), F32),
                   jax.ShapeDtypeStruct((s, D_MODEL), BF16), jax.ShapeDtypeStruct((s, D_MODEL), BF16),
                   jax.ShapeDtypeStruct((8, D_MODEL), F32)],
        scratch_shapes=[pltpu.VMEM((tm + 16, UP_W), F32)],
        compiler_params=_params(("arbitrary",)),
    )(du, du, conv_w, w_up, x1, g_pre, dout, mixed, g_post, w_out)


def _mixer_bwd(proj, dmix, states, kept, sin, cos, consts, hosted=None):
    probs, p_sinks, ret_scores, ret_normed, ret_rstd, ret_q, ret_k, ret_kz, stacked_q, gate_sig = kept
    s = proj.shape[0]
    nc = s // CHUNK
    cps = MIXER_CHUNKS_PER_STEP
    nb = nc // cps
    groups = KV_W // HEAD_DIM
    d_intra, xi_full, zeta_full, decay_full = consts

    def body(p_ref, pkv_ref, dmix_ref, st_ref, pr_ref, ps_ref, ra_ref, on_ref, rs_ref, rq_ref, rk_ref, rz_ref, sq_ref, sg_ref,
             sin_ref, cos_ref, dm_ref, xi_ref, ze_ref, dc_ref, dp_ref, dsk_ref, gstate, ckv, dsk_acc):
        i = pl.program_id(0)
        block = nb - 1 - i

        @pl.when(i == 0)
        def _():
            gstate[...] = jnp.zeros_like(gstate)
            ckv[...] = jnp.zeros_like(ckv)
            dsk_acc[...] = jnp.zeros_like(dsk_acc)

        gs_all = [gstate[h] for h in range(N_RET_HEADS)]
        later_kv = ckv[...]
        lane = lax.broadcasted_iota(jnp.int32, (CHUNK, 128), 1)
        dsk = jnp.zeros((CHUNK, 128), F32)
        even = _even_lanes((CHUNK, RET_W))
        half_q = [_half((CHUNK, 128), hk) for hk in range(groups)]
        half_kv = [_half((2 * CHUNK, 128), hk) for hk in range(groups)]
        for c in reversed(range(cps)):
            r0 = c * CHUNK
            rows = slice(r0, r0 + CHUNK)

            kv_cur = p_ref[rows, KV_A0:KV_A0 + 2 * KV_W]
            kv_prev = pkv_ref[...] if c == 0 else p_ref[r0 - CHUNK:r0, KV_A0:KV_A0 + 2 * KV_W]
            kk = jnp.concatenate([kv_prev[:, :KV_W], kv_cur[:, :KV_W]], axis=0)
            vv = jnp.concatenate([kv_prev[:, KV_W:], kv_cur[:, KV_W:]], axis=0)
            vv_b = vv.astype(BF16)
            dkk = jnp.zeros((2 * CHUNK, KV_W), F32)
            dvv = jnp.zeros((2 * CHUNK, KV_W), F32)
            for hk in range(groups):
                q_b = sq_ref[c, hk]
                do_b = _stack_heads(dmix_ref, r0, 0, hk, half_q[hk]).astype(BF16)
                p_b = pr_ref[c, hk]
                p = p_b.astype(F32)
                dpr = _dot_nt(do_b, vv_b)
                delta = jnp.sum(p * dpr, axis=-1, keepdims=True)
                ds_b = (p * (dpr - delta) * ATTN_SCALE).astype(BF16)
                dsink = -ps_ref[c, hk] * delta
                for j in range(GROUP):
                    dsk = dsk + jnp.where(lane == GROUP * hk + j, dsink[j * CHUNK:(j + 1) * CHUNK], 0.0)
                k_b = jnp.where(half_kv[hk], kk, 0.0).astype(BF16)
                for q, pair in enumerate(_unstack_heads(_dot(ds_b, k_b), hk)):
                    pi = (GROUP // 2) * hk + q
                    dp_ref[rows, Q_A0 + pi * 128:Q_A0 + (pi + 1) * 128] = pair.astype(BF16)
                dkk = dkk + _dot_tn(ds_b, q_b)
                dvv = dvv + _dot_tn(p_b, do_b)
            dp_ref[rows, KV_A0:KV_A0 + KV_W] = (dkk[CHUNK:] + later_kv[:, :KV_W]).astype(BF16)
            dp_ref[rows, KV_A0 + KV_W:KV_A0 + 2 * KV_W] = (dvv[CHUNK:] + later_kv[:, KV_W:]).astype(BF16)
            later_kv = jnp.concatenate([dkk[:CHUNK], dvv[:CHUNK]], axis=1)

            sin4, cos4 = _tile4(sin_ref[rows, :]), _tile4(cos_ref[rows, :])
            dq_parts, dk_parts = [], []
            for h in range(N_RET_HEADS):
                sl = slice(h * RET_HEAD_DIM, (h + 1) * RET_HEAD_DIM)
                qh, kh = rq_ref[rows, sl], rk_ref[rows, sl]
                vh = p_ref[rows, V_R0 + h * RET_HEAD_DIM:V_R0 + (h + 1) * RET_HEAD_DIM].astype(BF16)
                st_b = st_ref[c, h].astype(BF16)
                gs = gs_all[h]
                gs_b = gs.astype(BF16)
                xi_h = xi_ref[:, sl]
                dm = dm_ref[h]
                a_b, on, rs = ra_ref[c, h], on_ref[rows, sl], rs_ref[c, h]
                g = p_ref[rows, G_R0 + h * RET_HEAD_DIM:G_R0 + (h + 1) * RET_HEAD_DIM]
                sg = sg_ref[rows, sl]
                dr = dmix_ref[rows, ATTN_W + h * RET_HEAD_DIM:ATTN_W + (h + 1) * RET_HEAD_DIM].astype(F32)
                dp_ref[rows, G_R0 + h * RET_HEAD_DIM:G_R0 + (h + 1) * RET_HEAD_DIM] = (
                    dr * on * (sg * (1.0 + g * (1.0 - sg)))).astype(BF16)
                don = dr * g * sg
                do = rs * (don - jnp.mean(don, axis=-1, keepdims=True) - on * jnp.mean(don * on, axis=-1, keepdims=True))
                do_b = do.astype(BF16)
                dox_b = (do * xi_h).astype(BF16)
                da_b = (_dot_nt(do_b, vh) * dm).astype(BF16)
                dq_parts.append(_dot(da_b, kh) + _dot_nt(dox_b, st_b))
                dk_parts.append(_dot_tn(da_b, qh) + ze_ref[:, sl] * _dot_nt(vh, gs_b))
                dv = _dot_tn(a_b, do_b) + _dot(rz_ref[rows, sl], gs_b)
                dp_ref[rows, V_R0 + h * RET_HEAD_DIM:V_R0 + (h + 1) * RET_HEAD_DIM] = dv.astype(BF16)
                gs_all[h] = dc_ref[0:1, sl] * gs + _dot_tn(qh, dox_b)
            dq = jnp.concatenate(dq_parts, axis=-1)
            dk = jnp.concatenate(dk_parts, axis=-1)
            dp_ref[rows, Q_R0:Q_R0 + RET_W] = (dq * cos4 - _swap2(dq, even) * sin4).astype(BF16)
            dp_ref[rows, K_R0:K_R0 + RET_W] = (RET_K_SCALE * (dk * cos4 - _swap2(dk, even) * sin4)).astype(BF16)

        for h in range(N_RET_HEADS):
            gstate[h] = gs_all[h]
        ckv[...] = later_kv
        dsk_acc[...] += dsk

        @pl.when(i == nb - 1)
        def _():
            dsk_ref[...] = jnp.sum(dsk_acc[...], axis=0, keepdims=True)

    rev = lambda i: nb - 1 - i
    return _hosted_call(
        body, name="mixer_bwd", grid=(nb,),
        in_specs=[
            pl.BlockSpec((cps * CHUNK, IN_W), lambda i: (rev(i), 0)),
            pl.BlockSpec((CHUNK, 2 * KV_W), lambda i: (jnp.maximum(cps * rev(i) - 1, 0), KV_A0 // (2 * KV_W))),
            pl.BlockSpec((cps * CHUNK, D_MODEL), lambda i: (rev(i), 0)),
            pl.BlockSpec((cps, N_RET_HEADS, RET_HEAD_DIM, RET_HEAD_DIM), lambda i: (rev(i), 0, 0, 0)),
            pl.BlockSpec((cps, groups, GROUP * CHUNK, 2 * CHUNK), lambda i: (rev(i), 0, 0, 0)),
            pl.BlockSpec((cps, groups, GROUP * CHUNK, 1), lambda i: (rev(i), 0, 0, 0)),
            pl.BlockSpec((cps, N_RET_HEADS, CHUNK, CHUNK), lambda i: (rev(i), 0, 0, 0)),
            pl.BlockSpec((cps * CHUNK, RET_W), lambda i: (rev(i), 0)),
            pl.BlockSpec((cps, N_RET_HEADS, CHUNK, 1), lambda i: (rev(i), 0, 0, 0)),
            pl.BlockSpec((cps * CHUNK, RET_W), lambda i: (rev(i), 0)), pl.BlockSpec((cps * CHUNK, RET_W), lambda i: (rev(i), 0)),
            pl.BlockSpec((cps * CHUNK, RET_W), lambda i: (rev(i), 0)),
            pl.BlockSpec((cps, groups, GROUP * CHUNK, 128), lambda i: (rev(i), 0, 0, 0)),
            pl.BlockSpec((cps * CHUNK, RET_W), lambda i: (rev(i), 0)),
            pl.BlockSpec((cps * CHUNK, RET_HEAD_DIM), lambda i: (rev(i), 0)),
            pl.BlockSpec((cps * CHUNK, RET_HEAD_DIM), lambda i: (rev(i), 0)),
            _resident((N_RET_HEADS, CHUNK, CHUNK)), _resident((CHUNK, RET_W)), _resident((CHUNK, RET_W)), _resident((8, RET_W)),
        ],
        out_specs=[pl.BlockSpec((cps * CHUNK, IN_W), lambda i: (rev(i), 0)), pl.BlockSpec((1, 128), lambda i: (0, 0))],
        out_shape=[jax.ShapeDtypeStruct((s, IN_W), BF16), jax.ShapeDtypeStruct((1, 128), F32)],
        scratch_shapes=[pltpu.VMEM((N_RET_HEADS, RET_HEAD_DIM, RET_HEAD_DIM), F32), pltpu.VMEM((CHUNK, 2 * KV_W), F32),
                        pltpu.VMEM((CHUNK, 128), F32)],
        args=(proj, proj, dmix, states, probs, p_sinks, ret_scores, ret_normed, ret_rstd, ret_q, ret_k, ret_kz, stacked_q, gate_sig,
              sin, cos, d_intra, xi_full, zeta_full, decay_full), hosted=hosted)


def _in_proj_bwd(dproj, w_in_t, x, gain, dx1, hosted=None):
    s = x.shape[0]
    tm = min(BIG_TOKEN_TILE, s)

    def body(dp_ref, w_ref, x_ref, g_ref, dx1_ref, dx_ref, gacc_ref):
        @pl.when(pl.program_id(0) == 0)
        def _():
            gacc_ref[...] = jnp.zeros_like(gacc_ref)

        dh = _dot(dp_ref[...], w_ref[...])
        xv = x_ref[...]
        d1, dg = _rms_bwd(dh, xv, _rstd(xv), g_ref[...])
        dx_ref[...] = dx1_ref[...] + d1
        gacc_ref[0:1, :] += dg

    tok = lambda w: pl.BlockSpec((tm, w), lambda i: (i, 0))
    return _hosted_call(
        body, name="in_proj_bwd", grid=(s // tm,),
        in_specs=[tok(IN_W), _resident((IN_W, D_MODEL)), tok(D_MODEL), _resident((1, D_MODEL)), tok(D_MODEL)],
        out_specs=[tok(D_MODEL), pl.BlockSpec((8, D_MODEL), lambda i: (0, 0))],
        out_shape=[jax.ShapeDtypeStruct((s, D_MODEL), F32), jax.ShapeDtypeStruct((8, D_MODEL), F32)],
        scratch_shapes=[], args=(dproj, w_in_t, x, gain, dx1), hosted=hosted)


def _weight_grad(a, b, tn, name, by_block=False, hosted=None):
    s, m = a.shape
    n = b.shape[1]
    tk = min(WEIGHT_GRAD_TOKENS if m <= D_MODEL else WEIGHT_GRAD_TOKENS // 2, s)

    def body(a_ref, b_ref, o_ref):
        @pl.when(pl.program_id(1) == 0)
        def _():
            o_ref[...] = jnp.zeros_like(o_ref)

        o_ref[...] += _dot_tn(a_ref[...], b_ref[...])

    if by_block:
        out_spec = pl.BlockSpec((None, m, tn), lambda j, k: (j, 0, 0))
        out_shape = jax.ShapeDtypeStruct((n // tn, m, tn), F32)
    else:
        out_spec = pl.BlockSpec((m, tn), lambda j, k: (0, j))
        out_shape = jax.ShapeDtypeStruct((m, n), F32)
    (out,), lands = _hosted_call(
        body, name=name, grid=(n // tn, s // tk),
        in_specs=[pl.BlockSpec((tk, m), lambda j, k: (k, 0)), pl.BlockSpec((tk, tn), lambda j, k: (k, j))],
        out_specs=[out_spec], out_shape=[out_shape], scratch_shapes=[], args=(a, b), hosted=hosted)
    return out if hosted is None else (out, lands)


def _place():
    return lax.axis_index("x"), lax.axis_index("y"), lax.axis_index("c")


def _remote(src, dst, send_sems, recv_sems, k, to):
    return pltpu.make_async_remote_copy(src_ref=src, dst_ref=dst, send_sem=send_sems.at[k], recv_sem=recv_sems.at[k],
                                        device_id=to, device_id_type=MESH)


def _gather_level1_copies(w_refs, out_refs, send_sems, recv_sems, local_sems):
    x, y, c = _place()
    mine_at = 2 * x + y
    peers = [(x, y, 1 - c), (1 - x, y, c), (x, 1 - y, c), (1 - x, 1 - y, c)]
    local, sends, recvs = [], [], []
    for i, (w, out) in enumerate(zip(w_refs, out_refs)):
        half = w.shape[0] // 2
        src = w.at[pl.ds(pl.multiple_of(c * half, 16 if half % 16 == 0 else 8), half), :]
        mine = out.at[mine_at, c]
        local.append(pltpu.make_async_copy(src, mine, local_sems.at[i]))
        for k, p in enumerate(peers):
            sends.append(_remote(src, mine, send_sems, recv_sems, 4 * i + k, p))
            lands = out.at[mine_at, 1 - c] if k == 0 else out.at[2 * p[0] + p[1], c]
            recvs.append(_remote(src, lands, send_sems, recv_sems, 4 * i + k, p))
    return local, sends, recvs


def _gather_level1_start(w_refs, out_refs, send_sems, recv_sems, local_sems):
    local, sends, _ = _gather_level1_copies(w_refs, out_refs, send_sems, recv_sems, local_sems)
    for cp in local + sends:
        cp.start()


def _gather_level1_finish(w_refs, out_refs, send_sems, recv_sems, local_sems):
    local, sends, recvs = _gather_level1_copies(w_refs, out_refs, send_sems, recv_sems, local_sems)
    for cp in recvs:
        cp.wait_recv()
    for cp in sends:
        cp.wait_send()
    for cp in local:
        cp.wait()


def _gather_level2_copies(in_refs, out_refs, send_sems, recv_sems, local_sems):
    x, y, c = _place()
    chips = [(1 - x, y), (x, 1 - y), (1 - x, 1 - y)]
    sends, recvs = [], []
    for i, (src, out) in enumerate(zip(in_refs, out_refs)):
        for j, (px, py) in enumerate(chips):
            sends.append(_remote(src.at[2 * px + py, c], out.at[2 * px + py, c], send_sems, recv_sems, 3 * i + j, (x, y, 1 - c)))
            recvs.append(_remote(src.at[2 * px + py, c], out.at[2 * px + py, 1 - c], send_sems, recv_sems, 3 * i + j,
                                 (x, y, 1 - c)))
    return sends, recvs


def _gather_level2_start(in_refs, out_refs, send_sems, recv_sems, local_sems):
    for cp in _gather_level2_copies(in_refs, out_refs, send_sems, recv_sems, local_sems)[0]:
        cp.start()


def _gather_level2_finish(in_refs, out_refs, send_sems, recv_sems, local_sems):
    sends, recvs = _gather_level2_copies(in_refs, out_refs, send_sems, recv_sems, local_sems)
    for cp in recvs:
        cp.wait_recv()
    for cp in sends:
        cp.wait_send()


def _gathered_shape(w):
    r, cols = w.shape
    return jax.ShapeDtypeStruct((N_CHIPS, 2, r // 2, cols), w.dtype)


def _hosted_gather_level1(shards):
    n = len(shards)
    return _Hosted(shards, [_gathered_shape(w) for w in shards], {}, 4 * n, n, _gather_level1_start, _gather_level1_finish)


def _hosted_gather_level2(gathered):
    n = len(gathered)
    return _Hosted(gathered, [jax.ShapeDtypeStruct(g.shape, g.dtype) for g in gathered], {i: i for i in range(n)}, 3 * n, 0,
                   _gather_level2_start, _gather_level2_finish)


def _gather_now(shards, name, seq_len):
    n = len(shards)
    rows = min(512, seq_len)
    angle = 1.0 / jnp.power(10000.0, jnp.linspace(0.0, 1.0, RET_HEAD_DIM // 2, dtype=F32))
    sign = jnp.where(jnp.arange(RET_HEAD_DIM) % 2 == 0, -1.0, 1.0).astype(F32)
    angle_sign = jnp.concatenate([jnp.repeat(angle, 2)[None], sign[None], jnp.zeros((6, RET_HEAD_DIM), F32)], axis=0)

    def body(*refs):
        w_refs, as_ref, out_refs = list(refs[:n]), refs[n], list(refs[n + 1:2 * n + 1])
        sin_ref, cos_ref, send1, recv1, local1, send2, recv2 = refs[2 * n + 1:]
        _gather_level1_start(w_refs, out_refs, send1, recv1, local1)

        def fill(i, carry):
            r0 = pl.multiple_of(i * rows, rows)
            pos = (lax.broadcasted_iota(jnp.int32, (rows, RET_HEAD_DIM), 0) + i * rows).astype(F32)
            arg = pos * as_ref[0:1, :]
            sin_ref[pl.ds(r0, rows), :] = jnp.sin(arg) * as_ref[1:2, :]
            cos_ref[pl.ds(r0, rows), :] = jnp.cos(arg)
            return carry

        lax.fori_loop(0, seq_len // rows, fill, 0)
        _gather_level1_finish(w_refs, out_refs, send1, recv1, local1)
        _gather_level2_start(out_refs, out_refs, send2, recv2, None)
        _gather_level2_finish(out_refs, out_refs, send2, recv2, None)

    hbm, vmem = pl.BlockSpec(memory_space=pl.ANY), pl.BlockSpec(memory_space=pltpu.VMEM)
    table = jax.ShapeDtypeStruct((seq_len, RET_HEAD_DIM), F32)
    res = pl.pallas_call(
        body, name=name, out_shape=[_gathered_shape(w) for w in shards] + [table, table],
        in_specs=[hbm] * n + [vmem], out_specs=[hbm] * n + [vmem, vmem],
        scratch_shapes=[pltpu.SemaphoreType.DMA((4 * n,)), pltpu.SemaphoreType.DMA((4 * n,)), pltpu.SemaphoreType.DMA((n,)),
                        pltpu.SemaphoreType.DMA((3 * n,)), pltpu.SemaphoreType.DMA((3 * n,))],
        compiler_params=_params(),
    )(*shards, angle_sign)
    return res[:n], res[n], res[n + 1]


def _scatter_copies(g_refs, land_refs, send_sems, recv_sems, local_sems):
    x, y, c = _place()
    copies = []
    for i, (g, land) in enumerate(zip(g_refs, land_refs)):
        for k, (px, py, pc) in enumerate(_relations(x, y, c)):
            copies.append(_remote(g.at[2 * px + py, pc], land.at[k], send_sems, recv_sems, 7 * i + k, (px, py, pc)))
    return copies


def _scatter_start(g_refs, land_refs, send_sems, recv_sems, local_sems):
    for cp in _scatter_copies(g_refs, land_refs, send_sems, recv_sems, local_sems):
        cp.start()


def _scatter_finish(g_refs, land_refs, send_sems, recv_sems, local_sems):
    for cp in _scatter_copies(g_refs, land_refs, send_sems, recv_sems, local_sems):
        cp.wait()


def _hosted_scatter(grads):
    lands = [jax.ShapeDtypeStruct((N_DEV - 1,) + g.shape[2:], g.dtype) for g in grads]
    return _Hosted(grads, lands, {}, 7 * len(grads), 0, _scatter_start, _scatter_finish)


def _relations(x, y, c):
    rel = []
    for fx in (0, 1):
        for fy in (0, 1):
            for fc in (0, 1):
                if fx or fy or fc:
                    rel.append(((1 - x) if fx else x, (1 - y) if fy else y, (1 - c) if fc else c))
    return rel


def _join_halves(shards, small):
    n = len(shards)

    def body(*refs):
        in_refs, small_ref, out_refs, all_ref = refs[:n], refs[n], refs[n + 1:2 * n + 1], refs[2 * n + 1]
        send_sems, recv_sems = refs[2 * n + 2:]
        x, y, c = _place()
        slot = lambda p: all_ref.at[4 * p[0] + 2 * p[1] + p[2]]
        all_ref[4 * x + 2 * y + c] = small_ref[...]
        sends = [_remote(src.at[c], out.at[c], send_sems, recv_sems, i, (x, y, 1 - c))
                 for i, (src, out) in enumerate(zip(in_refs, out_refs))]
        recvs = [_remote(src.at[c], out.at[1 - c], send_sems, recv_sems, i, (x, y, 1 - c))
                 for i, (src, out) in enumerate(zip(in_refs, out_refs))]
        for k, p in enumerate(_relations(x, y, c)):
            sends.append(_remote(small_ref, slot((x, y, c)), send_sems, recv_sems, n + k, p))
            recvs.append(_remote(small_ref, slot(p), send_sems, recv_sems, n + k, p))
        for cp in sends:
            cp.start()
        for cp in recvs:
            cp.wait_recv()
        for cp in sends:
            cp.wait_send()

    hbm, vmem = pl.BlockSpec(memory_space=pl.ANY), pl.BlockSpec(memory_space=pltpu.VMEM)
    pairs = n + N_DEV - 1
    res = pl.pallas_call(
        body, name="grad_join_halves",
        out_shape=[jax.ShapeDtypeStruct(t.shape, t.dtype) for t in shards] + [jax.ShapeDtypeStruct((N_DEV,) + small.shape, F32)],
        in_specs=[hbm] * n + [vmem], out_specs=[hbm] * n + [vmem], input_output_aliases={i: i for i in range(n)},
        scratch_shapes=[pltpu.SemaphoreType.DMA((pairs,)), pltpu.SemaphoreType.DMA((pairs,))],
    )(*shards, small)
    return res[:n], res[n]


def _row_tile(rows, row_bytes, limit=1 << 20):
    best = 8
    for t in range(8, rows + 1, 8):
        if rows % t == 0 and t * row_bytes <= limit:
            best = t
    return best


def _sum_pieces(g, land, place, name):
    _, _, rh, cols = g.shape
    tr = _row_tile(rh, (N_DEV - 1) * cols * 4, 4 << 20)

    def body(p_ref, g_ref, l_ref, out_ref):
        acc = g_ref[...]
        for k in range(N_DEV - 1):
            acc = acc + l_ref[k].astype(F32)
        out_ref[...] = acc

    return pl.pallas_call(
        body, name=name,
        grid_spec=pltpu.PrefetchScalarGridSpec(
            num_scalar_prefetch=1, grid=(rh // tr,),
            in_specs=[pl.BlockSpec((None, None, tr, cols), lambda r, p: (p[0], p[1], r, 0)),
                      pl.BlockSpec((N_DEV - 1, tr, cols), lambda r, p: (0, r, 0))],
            out_specs=pl.BlockSpec((None, tr, cols), lambda r, p: (p[1], r, 0))),
        out_shape=jax.ShapeDtypeStruct((2, rh, cols), g.dtype),
        compiler_params=_params(("arbitrary",)),
    )(place, g, land)


def _adamw_math(w, g, m, v):
    m = ADAM_B1 * m + (1.0 - ADAM_B1) * g
    v = ADAM_B2 * v + (1.0 - ADAM_B2) * (g * g)
    m_hat = m / (1.0 - ADAM_B1 ** ADAM_STEP)
    v_hat = v / (1.0 - ADAM_B2 ** ADAM_STEP)
    delta = -ADAM_LR * (m_hat / (jnp.sqrt(v_hat) + ADAM_EPS) + ADAM_WD * w)
    return delta, m, v


def _adamw(w, g, m, v, name):
    r, cols = w.shape
    tr = _row_tile(r, cols * 4)

    def body(w_ref, g_ref, m_ref, v_ref, d_ref, nm_ref, nv_ref):
        d_ref[...], nm_ref[...], nv_ref[...] = _adamw_math(w_ref[...], g_ref[...], m_ref[...], v_ref[...])

    blk = pl.BlockSpec((tr, cols), lambda i: (i, 0))
    shape = jax.ShapeDtypeStruct((r, cols), F32)
    return pl.pallas_call(
        body, name=name, grid=(r // tr,), in_specs=[blk] * 4, out_specs=[blk] * 3, out_shape=[shape] * 3,
        compiler_params=_params(("arbitrary",)),
    )(w, g, m, v)


def _sum_devices(gathered):
    _, r, cols = gathered.shape

    def body(a_ref, g_ref):
        g = a_ref[0]
        for k in range(1, N_DEV):
            g = g + a_ref[k]
        g_ref[...] = g

    return pl.pallas_call(body, name="sum_small_grads", out_shape=jax.ShapeDtypeStruct((r, cols), F32))(gathered)


def _pack_conv(cw):
    flat = cw.reshape(-1)
    return jnp.pad(flat, (0, ROWS_CONV * D_MODEL - flat.shape[0])).reshape(ROWS_CONV, D_MODEL)


def _unpack_conv(rows):
    return rows.reshape(-1)[:CONV_WIDTH * UP_W // N_CHIPS].reshape(CONV_WIDTH, UP_W // N_CHIPS)


def _columns_to_shards(w):
    r, n = w.shape
    return jnp.transpose(w.reshape(r, N_CHIPS, n // N_CHIPS), (1, 0, 2))


def _shards_to_columns(w):
    _, r, n = w.shape
    return jnp.transpose(w, (1, 0, 2)).reshape(r, N_CHIPS * n)


def _pack_small(g_mix_pre, g_mix_post, g_ffn_pre, g_ffn_post, sinks, conv_b, loss):
    pad_row = lambda v: jnp.pad(v.reshape(1, -1), ((0, 0), (0, D_MODEL - v.size)))
    cb = jnp.pad(conv_b.reshape(-1), (0, 6 * D_MODEL - UP_W)).reshape(6, D_MODEL)
    zeros2 = jnp.zeros((2, D_MODEL), F32)
    return jnp.concatenate([g_mix_pre.reshape(1, -1), g_mix_post.reshape(1, -1), g_ffn_pre.reshape(1, -1),
                            g_ffn_post.reshape(1, -1), pad_row(sinks), pad_row(loss), zeros2, cb, zeros2], axis=0)


def _unpack_small(p):
    return dict(mix_pre_norm=p[0:1], mix_post_norm=p[1:2], ffn_pre_norm=p[2:3], ffn_post_norm=p[3:4],
                attn_sinks=p[4:5, :N_ATTN_HEADS], loss=p[5, 0], conv_b=p[8:14].reshape(1, -1)[:, :UP_W],
                conv_w=_unpack_conv(p[SMALL_ROWS:SMALL_ROWS + ROWS_CONV]))


def _local_step(x, target, g_mix_pre, w_in, sinks, w_out, g_mix_post, g_ffn_pre, w_up, conv_w, conv_b, w_down, g_ffn_post,
                distributed=True, rope=None):
    s = x.shape[0]
    consts = _ret_constants()
    sin, cos = _rope_tables(s) if rope is None else rope

    by_half = lambda g, rows: g.reshape(N_CHIPS, 2, rows // (2 * N_CHIPS), g.shape[-1])

    if distributed:
        (h1, proj), level1 = _in_proj(x, g_mix_pre, w_in, _hosted_gather_level1([w_out, w_up, w_down]))
        (mix, states, *kept), (w_out, w_up, w_down) = _mixer_fwd(proj, sinks, sin, cos, consts, _hosted_gather_level2(level1))
        w_out, w_down = w_out.reshape(D_MODEL, D_MODEL), w_down.reshape(D_FF, D_MODEL)
        w_up = w_up.reshape(N_CHIPS, D_MODEL, UP_W // N_CHIPS)
    else:
        (h1, proj), _ = _in_proj(x, g_mix_pre, w_in)
        (mix, states, *kept), _ = _mixer_fwd(proj, sinks, sin, cos, consts)
    mixed, x1, h2, u0 = _out_up_proj(mix, x, w_out, g_mix_post, g_ffn_pre, w_up)
    y, dy2, dout, du, conv_acc, tail_acc = _ffn_tail(u0, x1, target, conv_w, conv_b, w_down, g_ffn_post)
    du0, dx1, dmixed, dmix, head_acc = _ffn_head_bwd(du, conv_w, w_up, x1, g_ffn_pre, dout, mixed, g_mix_post, w_out)

    d_w_down = _weight_grad(y, dy2, 512, "grad_w_down")
    d_w_up = _weight_grad(h2, du0, UP_W // N_CHIPS, "grad_w_up", by_block=True)
    d_w_out = _weight_grad(mix, dmixed, D_MODEL, "grad_w_out")
    early = [by_half(d_w_down, D_FF), by_half(d_w_up, N_CHIPS * D_MODEL), by_half(d_w_out, D_MODEL)]
    (dproj, dsinks), early_lands = _mixer_bwd(proj, dmix, states, kept, sin, cos, consts,
                                              _hosted_scatter(early) if distributed else None)
    d_w_in_t = _weight_grad(dproj, h1, 512, "grad_w_in")
    late = [by_half(d_w_in_t, IN_W)]
    (grad_x, in_acc), late_lands = _in_proj_bwd(dproj, w_in, x, g_mix_pre, dx1, _hosted_scatter(late) if distributed else None)

    small = _pack_small(in_acc[0], head_acc[1], head_acc[0], tail_acc[0], dsinks[0, :N_ATTN_HEADS], conv_acc[3],
                        jnp.sum(tail_acc[1]))
    d_conv = jnp.pad(conv_acc[0:CONV_WIDTH].reshape(-1), (0, CONV_FULL_ROWS * D_MODEL - CONV_WIDTH * UP_W))
    small = jnp.concatenate([small, d_conv.reshape(CONV_FULL_ROWS, D_MODEL)], axis=0)
    grads = dict(w_down=early[0], w_up=early[1], w_out=early[2], w_in=late[0])
    lands = dict(zip(["w_down", "w_up", "w_out", "w_in"], early_lands + late_lands))
    return grad_x, grads, lands, small


def kernel(x, mix_pre_norm, w_in, attn_sinks, w_out, mix_post_norm, ffn_pre_norm, w_up, conv_w, conv_b, w_down, ffn_post_norm, loss_target, m_mix_pre_norm, m_w_in, m_attn_sinks, m_w_out, m_mix_post_norm, m_ffn_pre_norm, m_w_up, m_conv_w, m_conv_b, m_w_down, m_ffn_post_norm, v_mix_pre_norm, v_w_in, v_attn_sinks, v_w_out, v_mix_post_norm, v_ffn_pre_norm, v_w_up, v_conv_w, v_conv_b, v_w_down, v_ffn_post_norm):
    cx, cy, cc = _place()
    shard = 2 * cx + cy

    conv_rows = jnp.pad(conv_w[0], ((0, 16 - CONV_WIDTH), (0, 0)))
    w_in_t = jnp.swapaxes(w_in[0], 0, 1)
    (w_in_all, conv_all), sin, cos = _gather_now([w_in_t.astype(BF16), conv_rows], "gather_w_in", x.shape[1])
    conv_full = _shards_to_columns(conv_all[:, 0, :CONV_WIDTH])

    grad_x, grads, lands, small = _local_step(
        x[0], loss_target[0], mix_pre_norm, w_in_all.reshape(IN_W, D_MODEL), attn_sinks.reshape(-1), w_out[0].astype(BF16),
        mix_post_norm, ffn_pre_norm, w_up[0].astype(BF16), conv_full, conv_b, w_down[0].astype(BF16), ffn_post_norm,
        rope=(sin, cos))

    place = jnp.stack([shard, cc]).astype(jnp.int32)
    mats = ["w_in", "w_out", "w_up", "w_down"]
    halves = [_sum_pieces(grads[n], lands[n], place, "sum_grad_" + n) for n in mats]
    weights = dict(w_in=(w_in, m_w_in, v_w_in), w_out=(w_out, m_w_out, v_w_out), w_up=(w_up, m_w_up, v_w_up),
                   w_down=(w_down, m_w_down, v_w_down))
    mat_out = {}
    joined_all, small_all = _join_halves(halves, small)
    for n, joined in zip(mats, joined_all):
        w, m, v = (t[0] for t in weights[n])
        if n == "w_in":
            w, m, v = (jnp.swapaxes(t, 0, 1) for t in (w, m, v))
        res = (joined.reshape(w.shape),) + tuple(_adamw(w, joined.reshape(w.shape), m, v, "adamw_" + n))
        mat_out[n] = tuple(jnp.swapaxes(t, 0, 1) for t in res) if n == "w_in" else res

    small_sum = _sum_devices(small_all)
    d_conv_full = small_sum[SMALL_ROWS:].reshape(-1)[:CONV_WIDTH * UP_W].reshape(CONV_WIDTH, UP_W)
    d_conv_mine = lax.dynamic_slice_in_dim(d_conv_full, shard * (UP_W // N_CHIPS), UP_W // N_CHIPS, axis=1)
    g_s = jnp.concatenate([small_sum[:SMALL_ROWS], _pack_conv(d_conv_mine)], axis=0)
    zero = jnp.zeros((), F32)
    pack_rep = lambda a, b, c_, d, e, f, cw: jnp.concatenate([_pack_small(a, b, c_, d, e, f, zero), _pack_conv(cw[0])], axis=0)
    w_s = pack_rep(mix_pre_norm, mix_post_norm, ffn_pre_norm, ffn_post_norm, attn_sinks, conv_b, conv_w)
    m_s = pack_rep(m_mix_pre_norm, m_mix_post_norm, m_ffn_pre_norm, m_ffn_post_norm, m_attn_sinks, m_conv_b, m_conv_w)
    v_s = pack_rep(v_mix_pre_norm, v_mix_post_norm, v_ffn_pre_norm, v_ffn_post_norm, v_attn_sinks, v_conv_b, v_conv_w)
    delta_s, new_m_s, new_v_s = _adamw(w_s, g_s, m_s, v_s, "adamw_small")

    names = ["mix_pre_norm", "w_in", "attn_sinks", "w_out", "mix_post_norm", "ffn_pre_norm", "w_up", "conv_w", "conv_b",
             "w_down", "ffn_post_norm"]

    def leaves(which, packed_small):
        smalls = _unpack_small(packed_small)
        return [mat_out[n][which][None] if n in mat_out else (smalls[n][None] if n == "conv_w" else smalls[n]) for n in names]

    loss = _unpack_small(g_s)["loss"]
    return (loss, grad_x[None], *leaves(0, g_s), *leaves(1, delta_s), *leaves(2, new_m_s), *leaves(3, new_v_s))
```

```python
import math

import jax
import jax.numpy as jnp
from jax import lax
from jax.experimental import pallas as pl
from jax.experimental.pallas import tpu as pltpu

F32 = jnp.float32
BF16 = jnp.bfloat16

D_MODEL = 1024
HEAD_DIM = 64
ATTN_W = 512
N_ATTN_HEADS = 8
KV_W = 128
RET_W = 512
N_RET_HEADS = 4
RET_HEAD_DIM = 128
CHUNK = 128
IN_W = 2816
D_FF = 2816
UP_W = 2 * D_FF
CONV_WIDTH = 3
RMS_EPS = 1e-6
GN_EPS = 1e-6
MASK_VALUE = -1e30
ATTN_SCALE = HEAD_DIM ** -0.5
RET_K_SCALE = RET_HEAD_DIM ** -0.5
GELU_C = math.sqrt(2.0 / math.pi)
GELU_A = 0.044715

ADAM_LR = 0.001
ADAM_B1 = 0.9
ADAM_B2 = 0.999
ADAM_EPS = 1e-08
ADAM_WD = 0.01
ADAM_STEP = 10

N_CHIPS = 4
N_DEV = 8
MESH = pl.DeviceIdType.MESH
VMEM_LIMIT_V7X = 56 * 1024 * 1024
TOKEN_TILE = 256
BIG_TOKEN_TILE = 512
IN_PROJ_TOKEN_TILE = 1024
WEIGHT_GRAD_TOKENS = 2048
FFN_ROW_BLOCK = 64
HEAD_BWD_COLS = 512
MIXER_CHUNKS_PER_STEP = 4
Q_A0, KV_A0, Q_R0, K_R0, V_R0, G_R0 = 0, 512, 768, 1280, 1792, 2304

ROWS_CONV = 8
SMALL_ROWS = 16
CONV_FULL_ROWS = 24


def _params(sem=None, **kw):
    if sem is not None:
        kw["dimension_semantics"] = sem
    return pltpu.CompilerParams(vmem_limit_bytes=VMEM_LIMIT_V7X, **kw)


def _resident(shape):
    zeros = (0,) * len(shape)
    return pl.BlockSpec(shape, lambda *_: zeros, pipeline_mode=pl.Buffered(1))


class _Hosted:
    def __init__(self, ins, outs, aliases, n_pairs, n_local, start, finish):
        self.ins, self.outs, self.aliases = list(ins), list(outs), dict(aliases)
        self.n_pairs, self.n_local, self.start, self.finish = n_pairs, max(n_local, 1), start, finish


def _hosted_call(compute, *, name, grid, in_specs, out_specs, out_shape, scratch_shapes, args, hosted=None):
    params = _params(("arbitrary",) * len(grid))
    if hosted is None:
        res = pl.pallas_call(compute, name=name, grid=grid, in_specs=in_specs, out_specs=out_specs, out_shape=out_shape,
                             scratch_shapes=scratch_shapes, compiler_params=params)(*args)
        return list(res), []
    n_in, n_out, n_scr = len(in_specs), len(out_specs), len(scratch_shapes)
    h_in, h_out = len(hosted.ins), len(hosted.outs)

    def at(step_of):
        cond = pl.program_id(0) == step_of(grid[0])
        for d in range(1, len(grid)):
            cond = jnp.logical_and(cond, pl.program_id(d) == step_of(grid[d]))
        return cond

    def body(*refs):
        ins, refs = refs[:n_in], refs[n_in:]
        h_ins, refs = refs[:h_in], refs[h_in:]
        outs, refs = refs[:n_out], refs[n_out:]
        h_outs, refs = refs[:h_out], refs[h_out:]
        scr, sems = refs[:n_scr], refs[n_scr:]

        @pl.when(at(lambda n: 0))
        def _():
            hosted.start(h_ins, h_outs, *sems)

        compute(*ins, *outs, *scr)

        @pl.when(at(lambda n: n - 1))
        def _():
            hosted.finish(h_ins, h_outs, *sems)

    hbm = pl.BlockSpec(memory_space=pl.ANY)
    res = pl.pallas_call(
        body, name=name, grid=grid,
        in_specs=list(in_specs) + [hbm] * h_in, out_specs=list(out_specs) + [hbm] * h_out,
        out_shape=list(out_shape) + hosted.outs,
        scratch_shapes=list(scratch_shapes) + [pltpu.SemaphoreType.DMA((hosted.n_pairs,)), pltpu.SemaphoreType.DMA((hosted.n_pairs,)),
                                               pltpu.SemaphoreType.DMA((hosted.n_local,))],
        input_output_aliases={n_in + a: n_out + b for a, b in hosted.aliases.items()},
        compiler_params=params,
    )(*args, *hosted.ins)
    return list(res[:n_out]), list(res[n_out:])


def _dot(a, b):
    return jnp.dot(a, b, preferred_element_type=F32)


def _dot_nt(a, b):
    return lax.dot_general(a, b, (((1,), (1,)), ((), ())), preferred_element_type=F32)


def _dot_tn(a, b):
    return lax.dot_general(a, b, (((0,), (0,)), ((), ())), preferred_element_type=F32)


def _shift_matrix(n, by):
    row = lax.broadcasted_iota(jnp.int32, (n, n), 0)
    col = lax.broadcasted_iota(jnp.int32, (n, n), 1)
    return jnp.where(col == row + by, 1.0, 0.0).astype(BF16)


def _rstd(v):
    return lax.rsqrt(jnp.mean(v * v, axis=-1, keepdims=True) + RMS_EPS)


def _rms_bwd(dy, v, rstd, gain):
    n = v * rstd
    dgain = jnp.sum(dy * n, axis=0, keepdims=True)
    dn = dy * gain
    dv = rstd * (dn - n * jnp.mean(dn * n, axis=-1, keepdims=True))
    return dv, dgain


def _lane_lo(shape):
    return (lax.broadcasted_iota(jnp.int32, shape, 1) % 128) < HEAD_DIM


GROUP = N_ATTN_HEADS // (KV_W // HEAD_DIM)


def _attn_bias(first_chunk):
    qi = lax.broadcasted_iota(jnp.int32, (GROUP * CHUNK, 2 * CHUNK), 0) % CHUNK
    kj = lax.broadcasted_iota(jnp.int32, (GROUP * CHUNK, 2 * CHUNK), 1)
    valid = jnp.logical_and(kj > qi, kj <= qi + CHUNK)
    if first_chunk:
        valid = jnp.logical_and(valid, kj >= CHUNK)
    return jnp.where(valid, 0.0, MASK_VALUE)


def _half(shape, hk):
    lo = _lane_lo(shape)
    return lo if hk == 0 else jnp.logical_not(lo)


class _GroupMasks:
    def __init__(self, sk_ref):
        groups = range(KV_W // HEAD_DIM)
        self.q = [_half((CHUNK, 128), hk) for hk in groups]
        self.kv = [_half((2 * CHUNK, 128), hk) for hk in groups]
        self.sinks = [_group_sinks(sk_ref, hk) for hk in groups]


def _stack_heads(ref, row0, col0, hk, half):
    parts = []
    for j in range(GROUP):
        h = GROUP * hk + j
        pair = ref[row0:row0 + CHUNK, col0 + (h // 2) * 128:col0 + (h // 2 + 1) * 128].astype(F32)
        if h % 2 != hk:
            pair = pltpu.roll(pair, HEAD_DIM, 1)
        parts.append(jnp.where(half, pair, 0.0))
    return jnp.concatenate(parts, axis=0)


def _unstack_heads(stacked, hk):
    pairs = []
    for q in range(GROUP // 2):
        even, odd = stacked[2 * q * CHUNK:(2 * q + 1) * CHUNK], stacked[(2 * q + 1) * CHUNK:(2 * q + 2) * CHUNK]
        pairs.append(even + pltpu.roll(odd, HEAD_DIM, 1) if hk == 0 else pltpu.roll(even, HEAD_DIM, 1) + odd)
    return pairs


def _group_sinks(sk_ref, hk):
    row = lax.broadcasted_iota(jnp.int32, (GROUP * CHUNK, 1), 0)
    col = jnp.full((GROUP * CHUNK, 1), sk_ref[GROUP * hk], F32)
    for j in range(1, GROUP):
        col = jnp.where(row >= j * CHUNK, sk_ref[GROUP * hk + j], col)
    return col


def _attn_probs(q_b, kk_b, bias, sink):
    s = _dot_nt(q_b, kk_b) * ATTN_SCALE + bias
    m = jnp.maximum(jnp.max(s, axis=-1, keepdims=True), sink)
    e = jnp.exp(s - m)
    e_sink = jnp.exp(sink - m)
    inv = 1.0 / (jnp.sum(e, axis=-1, keepdims=True) + e_sink)
    return e * inv, e_sink * inv


def _even_lanes(shape):
    return (lax.broadcasted_iota(jnp.int32, shape, 1) % 2) == 0


def _swap2(v, even):
    return jnp.where(even, pltpu.roll(v, v.shape[1] - 1, 1), pltpu.roll(v, 1, 1))


def _tile4(v):
    return jnp.concatenate([v, v, v, v], axis=-1)


def _sigmoid(v):
    return 1.0 / (1.0 + jnp.exp(-v))


def _ret_constants():
    h = N_RET_HEADS
    log_gamma = jnp.log(1.0 - jnp.power(2.0, -5.0 - jnp.arange(h, dtype=F32)))
    idx = jnp.arange(CHUNK, dtype=F32)
    rel = idx[:, None] - idx[None, :]
    d_intra = jnp.where(rel[None] >= 0, jnp.exp(log_gamma[:, None, None] * jnp.maximum(rel, 0.0)[None]), 0.0)
    xi = jnp.exp(log_gamma[None, :] * (idx[:, None] + 1.0))
    zeta = jnp.exp(log_gamma[None, :] * (CHUNK - 1.0 - idx[:, None]))
    decay = jnp.exp(log_gamma * CHUNK)
    xi_full = jnp.repeat(xi, RET_HEAD_DIM, axis=1)
    zeta_full = jnp.repeat(zeta, RET_HEAD_DIM, axis=1)
    decay_full = jnp.broadcast_to(jnp.repeat(decay, RET_HEAD_DIM)[None, :], (8, RET_W))
    return d_intra.astype(F32), xi_full.astype(F32), zeta_full.astype(F32), decay_full.astype(F32)


def _rope_tables(s):
    pos = jnp.arange(s, dtype=F32)
    angle = 1.0 / jnp.power(10000.0, jnp.linspace(0.0, 1.0, RET_HEAD_DIM // 2, dtype=F32))
    angle = jnp.repeat(angle, 2)
    sign = jnp.where(jnp.arange(RET_HEAD_DIM) % 2 == 0, -1.0, 1.0).astype(F32)
    return jnp.sin(pos[:, None] * angle[None]) * sign[None], jnp.cos(pos[:, None] * angle[None])


def _in_proj(x, gain, w_in_t, hosted=None):
    s = x.shape[0]
    tm = min(IN_PROJ_TOKEN_TILE, s)

    def body(x_ref, g_ref, w_ref, h_ref, p_ref):
        xv = x_ref[...]
        h = (xv * _rstd(xv) * g_ref[...]).astype(BF16)
        h_ref[...] = h
        p_ref[...] = _dot_nt(h, w_ref[...])

    return _hosted_call(
        body, name="in_proj", grid=(s // tm,),
        in_specs=[pl.BlockSpec((tm, D_MODEL), lambda i: (i, 0)), _resident((1, D_MODEL)), _resident((IN_W, D_MODEL))],
        out_specs=[pl.BlockSpec((tm, D_MODEL), lambda i: (i, 0)), pl.BlockSpec((tm, IN_W), lambda i: (i, 0))],
        out_shape=[jax.ShapeDtypeStruct((s, D_MODEL), BF16), jax.ShapeDtypeStruct((s, IN_W), F32)],
        scratch_shapes=[], args=(x, gain, w_in_t), hosted=hosted)


def _mixer_fwd(proj, sinks, sin, cos, consts, hosted=None):
    s = proj.shape[0]
    nc = s // CHUNK
    cps = MIXER_CHUNKS_PER_STEP
    groups = KV_W // HEAD_DIM
    d_intra, xi_full, zeta_full, decay_full = consts

    def body(sk_ref, p_ref, pkv_ref, sin_ref, cos_ref, dm_ref, xi_ref, ze_ref, dc_ref,
             mix_ref, st_ref, pr_ref, ps_ref, ra_ref, on_ref, rs_ref, rq_ref, rk_ref, rz_ref, sq_ref, sg_ref, state):
        i = pl.program_id(0)

        @pl.when(i == 0)
        def _():
            state[...] = jnp.zeros_like(state)

        st = [state[h] for h in range(N_RET_HEADS)]
        bias_any = _attn_bias(False)
        bias_c0 = jnp.where(i == 0, _attn_bias(True), bias_any)
        even = _even_lanes((CHUNK, RET_W))
        masks = _GroupMasks(sk_ref)
        for c in range(cps):
            r0 = c * CHUNK
            rows = slice(r0, r0 + CHUNK)

            kv_cur = p_ref[rows, KV_A0:KV_A0 + 2 * KV_W]
            kv_prev = pkv_ref[...] if c == 0 else p_ref[r0 - CHUNK:r0, KV_A0:KV_A0 + 2 * KV_W]
            kk = jnp.concatenate([kv_prev[:, :KV_W], kv_cur[:, :KV_W]], axis=0)
            vv = jnp.concatenate([kv_prev[:, KV_W:], kv_cur[:, KV_W:]], axis=0)
            kk_b = kk.astype(BF16)
            bias = bias_c0 if c == 0 else bias_any
            for hk in range(KV_W // HEAD_DIM):
                q_b = _stack_heads(p_ref, r0, Q_A0, hk, masks.q[hk]).astype(BF16)
                p, p_sink = _attn_probs(q_b, kk_b, bias, masks.sinks[hk])
                p_b = p.astype(BF16)
                pr_ref[c, hk] = p_b
                ps_ref[c, hk] = p_sink
                sq_ref[c, hk] = q_b
                v_b = jnp.where(masks.kv[hk], vv, 0.0).astype(BF16)
                for q, pair in enumerate(_unstack_heads(_dot(p_b, v_b), hk)):
                    pi = (GROUP // 2) * hk + q
                    mix_ref[rows, pi * 128:(pi + 1) * 128] = pair.astype(BF16)

            sin4, cos4 = _tile4(sin_ref[rows, :]), _tile4(cos_ref[rows, :])
            q_r = p_ref[rows, Q_R0:Q_R0 + RET_W]
            k_r = p_ref[rows, K_R0:K_R0 + RET_W] * RET_K_SCALE
            q_r = q_r * cos4 + _swap2(q_r, even) * sin4
            k_r = k_r * cos4 + _swap2(k_r, even) * sin4
            q_b, k_b, kz_b = q_r.astype(BF16), k_r.astype(BF16), (k_r * ze_ref[...]).astype(BF16)
            rq_ref[rows, :], rk_ref[rows, :], rz_ref[rows, :] = q_b, k_b, kz_b
            for h in range(N_RET_HEADS):
                sl = slice(h * RET_HEAD_DIM, (h + 1) * RET_HEAD_DIM)
                qh, kh = q_b[:, sl], k_b[:, sl]
                vh = p_ref[rows, V_R0 + h * RET_HEAD_DIM:V_R0 + (h + 1) * RET_HEAD_DIM].astype(BF16)
                st_ref[c, h] = st[h]
                a_b = (_dot_nt(qh, kh) * dm_ref[h]).astype(BF16)
                qx = (q_r[:, sl] * xi_ref[:, sl]).astype(BF16)
                o = _dot(jnp.concatenate([a_b, qx], axis=1), jnp.concatenate([vh, st[h].astype(BF16)], axis=0))
                st[h] = dc_ref[0:1, sl] * st[h] + _dot_tn(kz_b[:, sl], vh)
                mu = jnp.mean(o, axis=-1, keepdims=True)
                oc = o - mu
                rs = lax.rsqrt(jnp.mean(oc * oc, axis=-1, keepdims=True) + GN_EPS)
                on = oc * rs
                ra_ref[c, h], on_ref[rows, sl], rs_ref[c, h] = a_b, on, rs
                g = p_ref[rows, G_R0 + h * RET_HEAD_DIM:G_R0 + (h + 1) * RET_HEAD_DIM]
                sg = _sigmoid(g)
                sg_ref[rows, sl] = sg
                mix_ref[rows, ATTN_W + h * RET_HEAD_DIM:ATTN_W + (h + 1) * RET_HEAD_DIM] = (g * sg * on).astype(BF16)
        for h in range(N_RET_HEADS):
            state[h] = st[h]

    return _hosted_call(
        body, name="mixer_fwd", grid=(nc // cps,),
        in_specs=[
            pl.BlockSpec(memory_space=pltpu.SMEM),
            pl.BlockSpec((cps * CHUNK, IN_W), lambda i: (i, 0)),
            pl.BlockSpec((CHUNK, 2 * KV_W), lambda i: (jnp.maximum(cps * i - 1, 0), KV_A0 // (2 * KV_W))),
            pl.BlockSpec((cps * CHUNK, RET_HEAD_DIM), lambda i: (i, 0)),
            pl.BlockSpec((cps * CHUNK, RET_HEAD_DIM), lambda i: (i, 0)),
            _resident((N_RET_HEADS, CHUNK, CHUNK)), _resident((CHUNK, RET_W)), _resident((CHUNK, RET_W)), _resident((8, RET_W)),
        ],
        out_specs=[
            pl.BlockSpec((cps * CHUNK, D_MODEL), lambda i: (i, 0)),
            pl.BlockSpec((cps, N_RET_HEADS, RET_HEAD_DIM, RET_HEAD_DIM), lambda i: (i, 0, 0, 0)),
            pl.BlockSpec((cps, groups, GROUP * CHUNK, 2 * CHUNK), lambda i: (i, 0, 0, 0)),
            pl.BlockSpec((cps, groups, GROUP * CHUNK, 1), lambda i: (i, 0, 0, 0)),
            pl.BlockSpec((cps, N_RET_HEADS, CHUNK, CHUNK), lambda i: (i, 0, 0, 0)),
            pl.BlockSpec((cps * CHUNK, RET_W), lambda i: (i, 0)),
            pl.BlockSpec((cps, N_RET_HEADS, CHUNK, 1), lambda i: (i, 0, 0, 0)),
        ] + [pl.BlockSpec((cps * CHUNK, RET_W), lambda i: (i, 0))] * 3 + [
            pl.BlockSpec((cps, groups, GROUP * CHUNK, 128), lambda i: (i, 0, 0, 0)),
            pl.BlockSpec((cps * CHUNK, RET_W), lambda i: (i, 0))],
        out_shape=[jax.ShapeDtypeStruct((s, D_MODEL), BF16),
                   jax.ShapeDtypeStruct((nc, N_RET_HEADS, RET_HEAD_DIM, RET_HEAD_DIM), F32),
                   jax.ShapeDtypeStruct((nc, groups, GROUP * CHUNK, 2 * CHUNK), BF16),
                   jax.ShapeDtypeStruct((nc, groups, GROUP * CHUNK, 1), F32),
                   jax.ShapeDtypeStruct((nc, N_RET_HEADS, CHUNK, CHUNK), BF16),
                   jax.ShapeDtypeStruct((s, RET_W), F32),
                   jax.ShapeDtypeStruct((nc, N_RET_HEADS, CHUNK, 1), F32)] + [jax.ShapeDtypeStruct((s, RET_W), BF16)] * 3 + [
                   jax.ShapeDtypeStruct((nc, groups, GROUP * CHUNK, 128), BF16), jax.ShapeDtypeStruct((s, RET_W), F32)],
        scratch_shapes=[pltpu.VMEM((N_RET_HEADS, RET_HEAD_DIM, RET_HEAD_DIM), F32)],
        args=(sinks, proj, proj, sin, cos, d_intra, xi_full, zeta_full, decay_full), hosted=hosted)


def _out_up_proj(mix, x, w_out, g_post, g_pre, w_up):
    s = x.shape[0]
    tm = min(BIG_TOKEN_TILE, s)
    blk = UP_W // N_CHIPS

    def body(mix_ref, x_ref, wo_ref, g2_ref, g3_ref, wu_ref, mixed_ref, x1_ref, h2_ref, u0_ref):
        mixed = _dot(mix_ref[...], wo_ref[...])
        mixed_ref[...] = mixed
        x1 = x_ref[...] + mixed * _rstd(mixed) * g2_ref[...]
        x1_ref[...] = x1
        h2 = (x1 * _rstd(x1) * g3_ref[...]).astype(BF16)
        h2_ref[...] = h2
        for k in range(N_CHIPS):
            u0_ref[:, k * blk:(k + 1) * blk] = _dot(h2, wu_ref[k]).astype(BF16)

    tok = lambda w: pl.BlockSpec((tm, w), lambda i: (i, 0))
    return pl.pallas_call(
        body, name="out_up_proj", grid=(s // tm,),
        in_specs=[tok(D_MODEL), tok(D_MODEL), _resident((D_MODEL, D_MODEL)), _resident((1, D_MODEL)), _resident((1, D_MODEL)),
                  _resident((N_CHIPS, D_MODEL, blk))],
        out_specs=[tok(D_MODEL), tok(D_MODEL), tok(D_MODEL), tok(UP_W)],
        out_shape=[jax.ShapeDtypeStruct((s, D_MODEL), F32), jax.ShapeDtypeStruct((s, D_MODEL), F32),
                   jax.ShapeDtypeStruct((s, D_MODEL), BF16), jax.ShapeDtypeStruct((s, UP_W), BF16)],
        compiler_params=_params(("arbitrary",)),
    )(mix, x, w_out, g_post, g_pre, w_up)


def _ffn_tail(u0, x1, target, conv_w, conv_b, w_down, g_post):
    s = x1.shape[0]
    tm = TOKEN_TILE
    last = s // tm - 1
    rb, lanes = FFN_ROW_BLOCK, 128

    def body(u0_ref, x1_ref, t_ref, cw_ref, cb_ref, wd_ref, g_ref,
             y_ref, dy2_ref, dout_ref, du_ref, cacc_ref, gacc_ref, u1_s, u2_s, carry, gelu_s, slope_s, dy_s, cacc):
        i = pl.program_id(0)

        @pl.when(i == 0)
        def _():
            carry[...] = jnp.zeros_like(carry)
            cacc[...] = jnp.zeros_like(cacc)
            gacc_ref[...] = jnp.zeros_like(gacc_ref)

        shift1, shift2 = _shift_matrix(tm, -1), _shift_matrix(tm, -2)
        r8 = lax.broadcasted_iota(jnp.int32, (8, 1), 0)
        wide = 2 * lanes

        def shift_block(col):
            cols = slice(col, col + wide)
            u1_s[:, cols] = _dot(shift1, u0_ref[:, cols])
            u2_s[:, cols] = _dot(shift2, u0_ref[:, cols])
            c14, c15 = carry[14:15, cols], carry[15:16, cols]
            u1_s[0:8, cols] = jnp.where(r8 == 0, c15, u1_s[0:8, cols])
            u2_s[0:8, cols] = jnp.where(r8 == 0, c14, jnp.where(r8 == 1, c15, u2_s[0:8, cols]))

        def taps(col):
            return (cw_ref[0:1, col:col + lanes], cw_ref[1:2, col:col + lanes], cw_ref[2:3, col:col + lanes],
                    cb_ref[0:1, col:col + lanes])

        def shifted(r0, col):
            return (u2_s[r0:r0 + rb, col:col + lanes], u1_s[r0:r0 + rb, col:col + lanes],
                    u0_ref[r0:r0 + rb, col:col + lanes].astype(F32))

        def conv(r0, col, w):
            u2, u1, uc = shifted(r0, col)
            return w[0] * u2 + w[1] * u1 + w[2] * uc + w[3]

        fold = lambda v: jnp.sum(v.reshape(rb // 8, 8, lanes), axis=0)

        shift_block(0)
        shift_block(D_FF)
        for j in range(D_FF // lanes):
            cg, cv = j * lanes, D_FF + j * lanes
            if cg % wide == 0 and cg + wide < D_FF:
                shift_block(cg + wide)
                shift_block(cv + wide)
            wg, wv = taps(cg), taps(cv)
            for r0 in range(0, tm, rb):
                gate, val = conv(r0, cg, wg), conv(r0, cv, wv)
                g2 = gate * gate
                th = jnp.tanh(gate * (GELU_C + GELU_C * GELU_A * g2))
                hp = 0.5 * th + 0.5
                gelu = gate * hp
                dgelu = hp + gate * (1.0 - th * th) * (0.5 * GELU_C + 1.5 * GELU_C * GELU_A * g2)
                y_ref[r0:r0 + rb, cg:cg + lanes] = (gelu * val).astype(BF16)
                gelu_s[r0:r0 + rb, cg:cg + lanes] = gelu
                slope_s[r0:r0 + rb, cg:cg + lanes] = dgelu * val

        y2 = _dot(y_ref[...], wd_ref[...])
        r4 = _rstd(y2)
        gain = g_ref[...]
        out = x1_ref[...] + y2 * r4 * gain
        diff = out - t_ref[...]
        dout = diff * (1.0 / D_MODEL)
        dout_ref[...] = dout
        dy2, dgain = _rms_bwd(dout, y2, r4, gain)
        dy2_b = dy2.astype(BF16)
        dy2_ref[...] = dy2_b
        gacc_ref[0:1, :] += dgain
        gacc_ref[1:2, :] += 0.5 * jnp.sum(diff * dout, axis=0, keepdims=True)
        carry[...] = u0_ref[tm - 16:tm, :].astype(F32)

        dy_s[:, 0:wide] = _dot_nt(dy2_b, wd_ref[0:wide, :])
        for j in range(D_FF // lanes):
            cg, cv = j * lanes, D_FF + j * lanes
            if cg % wide == 0 and cg + wide < D_FF:
                dy_s[:, cg + wide:cg + 2 * wide] = _dot_nt(dy2_b, wd_ref[cg + wide:cg + 2 * wide, :])
            acc = [[jnp.zeros((8, lanes), F32) for _ in range(CONV_WIDTH + 1)] for _ in range(2)]
            for r0 in range(0, tm, rb):
                dy = dy_s[r0:r0 + rb, cg:cg + lanes]
                d_gate = dy * slope_s[r0:r0 + rb, cg:cg + lanes]
                d_val = dy * gelu_s[r0:r0 + rb, cg:cg + lanes]
                for side, (col, d) in enumerate(((cg, d_gate), (cv, d_val))):
                    du_ref[r0:r0 + rb, col:col + lanes] = d.astype(BF16)
                    for k, u in enumerate(shifted(r0, col)):
                        acc[side][k] = acc[side][k] + fold(d * u)
                    acc[side][CONV_WIDTH] = acc[side][CONV_WIDTH] + fold(d)
            for side, col in enumerate((cg, cv)):
                for k in range(CONV_WIDTH + 1):
                    cacc[8 * k:8 * k + 8, col:col + lanes] += acc[side][k]

        @pl.when(i == last)
        def _():
            for k in range(CONV_WIDTH + 1):
                cacc_ref[k:k + 1, :] = jnp.sum(cacc[8 * k:8 * k + 8, :], axis=0, keepdims=True)
            cacc_ref[CONV_WIDTH + 1:8, :] = jnp.zeros((8 - CONV_WIDTH - 1, UP_W), F32)

    tok = lambda w: pl.BlockSpec((tm, w), lambda i: (i, 0))
    return pl.pallas_call(
        body, name="ffn_tail", grid=(s // tm,),
        in_specs=[tok(UP_W), tok(D_MODEL), tok(D_MODEL), _resident((CONV_WIDTH, UP_W)), _resident((1, UP_W)),
                  _resident((D_FF, D_MODEL)), _resident((1, D_MODEL))],
        out_specs=[tok(D_FF), tok(D_MODEL), tok(D_MODEL), tok(UP_W),
                   pl.BlockSpec((8, UP_W), lambda i: (0, 0)), pl.BlockSpec((8, D_MODEL), lambda i: (0, 0))],
        out_shape=[jax.ShapeDtypeStruct((s, D_FF), BF16), jax.ShapeDtypeStruct((s, D_MODEL), BF16),
                   jax.ShapeDtypeStruct((s, D_MODEL), F32), jax.ShapeDtypeStruct((s, UP_W), BF16),
                   jax.ShapeDtypeStruct((8, UP_W), F32), jax.ShapeDtypeStruct((8, D_MODEL), F32)],
        scratch_shapes=[pltpu.VMEM((tm, UP_W), F32), pltpu.VMEM((tm, UP_W), F32), pltpu.VMEM((16, UP_W), F32),
                        pltpu.VMEM((tm, D_FF), F32), pltpu.VMEM((tm, D_FF), F32),
                        pltpu.VMEM((tm, D_FF), F32), pltpu.VMEM((8 * (CONV_WIDTH + 1), UP_W), F32)],
        compiler_params=_params(("arbitrary",)),
    )(u0, x1, target, conv_w, conv_b, w_down, g_post)


def _ffn_head_bwd(du, conv_w, w_up, x1, g_pre, dout, mixed, g_post, w_out, hosted=None):
    s = x1.shape[0]
    tm = TOKEN_TILE
    nt = s // tm
    blk = UP_W // N_CHIPS

    def body(du_ref, halo_ref, cw_ref, wu_ref, x1_ref, g3_ref, dout_ref, mixed_ref, g2_ref, wo_ref,
             du0_ref, dx1_ref, dmixed_ref, dmix_ref, gacc_ref, dbuf):
        i = pl.program_id(0)

        @pl.when(i == 0)
        def _():
            gacc_ref[...] = jnp.zeros_like(gacc_ref)

        dbuf[0:tm, :] = du_ref[...].astype(F32)
        dbuf[tm:tm + 16, :] = jnp.where(i < nt - 1, halo_ref[...].astype(F32), 0.0)
        dh2 = jnp.zeros((tm, D_MODEL), F32)
        for k in range(N_CHIPS):
            for c0 in range(0, blk, HEAD_BWD_COLS):
                width = min(HEAD_BWD_COLS, blk - c0)
                cols = slice(k * blk + c0, k * blk + c0 + width)
                du0_b = (cw_ref[2:3, cols] * dbuf[0:tm, cols] + cw_ref[1:2, cols] * dbuf[1:1 + tm, cols]
                         + cw_ref[0:1, cols] * dbuf[2:2 + tm, cols]).astype(BF16)
                du0_ref[:, cols] = du0_b
                dh2 = dh2 + _dot_nt(du0_b, wu_ref[k, :, c0:c0 + width])
        x1 = x1_ref[...]
        d3, dg3 = _rms_bwd(dh2, x1, _rstd(x1), g3_ref[...])
        dx1 = dout_ref[...] + d3
        dx1_ref[...] = dx1
        mixed = mixed_ref[...]
        dmixed, dg2 = _rms_bwd(dx1, mixed, _rstd(mixed), g2_ref[...])
        dmixed_b = dmixed.astype(BF16)
        dmixed_ref[...] = dmixed_b
        dmix_ref[...] = _dot_nt(dmixed_b, wo_ref[...]).astype(BF16)
        gacc_ref[0:1, :] += dg3
        gacc_ref[1:2, :] += dg2

    tok = lambda w: pl.BlockSpec((tm, w), lambda i: (i, 0))
    halo = pl.BlockSpec((16, UP_W), lambda i: (jnp.minimum(i + 1, nt - 1) * (tm // 16), 0))
    return _hosted_call(
        body, name="ffn_head_bwd", grid=(nt,),
        in_specs=[tok(UP_W), halo, _resident((CONV_WIDTH, UP_W)), _resident((N_CHIPS, D_MODEL, blk)), tok(D_MODEL),
                  _resident((1, D_MODEL)), tok(D_MODEL), tok(D_MODEL), _resident((1, D_MODEL)), _resident((D_MODEL, D_MODEL))],
        out_specs=[tok(UP_W), tok(D_MODEL), tok(D_MODEL), tok(D_MODEL), pl.BlockSpec((8, D_MODEL), lambda i: (0, 0))],
        out_shape=[jax.ShapeDtypeStruct((s, UP_W), BF16), jax.ShapeDtypeStruct((s, D_MODEL), F32),
                   jax.ShapeDtypeStruct((s, D_MODEL), BF16), jax.ShapeDtypeStruct((s, D_MODEL), BF16),
                   jax.ShapeDtypeStruct((8, D_MODEL), F32)],
        scratch_shapes=[pltpu.VMEM((tm + 16, UP_W), F32)],
        args=(du, du, conv_w, w_up, x1, g_pre, dout, mixed, g_post, w_out), hosted=hosted)


def _mixer_bwd(proj, dmix, states, kept, sin, cos, consts, hosted=None):
    probs, p_sinks, ret_scores, ret_normed, ret_rstd, ret_q, ret_k, ret_kz, stacked_q, gate_sig = kept
    s = proj.shape[0]
    nc = s // CHUNK
    cps = MIXER_CHUNKS_PER_STEP
    nb = nc // cps
    groups = KV_W // HEAD_DIM
    d_intra, xi_full, zeta_full, decay_full = consts

    def body(p_ref, pkv_ref, dmix_ref, st_ref, pr_ref, ps_ref, ra_ref, on_ref, rs_ref, rq_ref, rk_ref, rz_ref, sq_ref, sg_ref,
             sin_ref, cos_ref, dm_ref, xi_ref, ze_ref, dc_ref, dp_ref, dsk_ref, gstate, ckv, dsk_acc):
        i = pl.program_id(0)
        block = nb - 1 - i

        @pl.when(i == 0)
        def _():
            gstate[...] = jnp.zeros_like(gstate)
            ckv[...] = jnp.zeros_like(ckv)
            dsk_acc[...] = jnp.zeros_like(dsk_acc)

        gs_all = [gstate[h] for h in range(N_RET_HEADS)]
        later_kv = ckv[...]
        lane = lax.broadcasted_iota(jnp.int32, (CHUNK, 128), 1)
        dsk = jnp.zeros((CHUNK, 128), F32)
        even = _even_lanes((CHUNK, RET_W))
        half_q = [_half((CHUNK, 128), hk) for hk in range(groups)]
        half_kv = [_half((2 * CHUNK, 128), hk) for hk in range(groups)]
        for c in reversed(range(cps)):
            r0 = c * CHUNK
            rows = slice(r0, r0 + CHUNK)

            kv_cur = p_ref[rows, KV_A0:KV_A0 + 2 * KV_W]
            kv_prev = pkv_ref[...] if c == 0 else p_ref[r0 - CHUNK:r0, KV_A0:KV_A0 + 2 * KV_W]
            kk = jnp.concatenate([kv_prev[:, :KV_W], kv_cur[:, :KV_W]], axis=0)
            vv = jnp.concatenate([kv_prev[:, KV_W:], kv_cur[:, KV_W:]], axis=0)
            vv_b = vv.astype(BF16)
            dkk = jnp.zeros((2 * CHUNK, KV_W), F32)
            dvv = jnp.zeros((2 * CHUNK, KV_W), F32)
            for hk in range(groups):
                q_b = sq_ref[c, hk]
                do_b = _stack_heads(dmix_ref, r0, 0, hk, half_q[hk]).astype(BF16)
                p_b = pr_ref[c, hk]
                p = p_b.astype(F32)
                dpr = _dot_nt(do_b, vv_b)
                delta = jnp.sum(p * dpr, axis=-1, keepdims=True)
                ds_b = (p * (dpr - delta) * ATTN_SCALE).astype(BF16)
                dsink = -ps_ref[c, hk] * delta
                for j in range(GROUP):
                    dsk = dsk + jnp.where(lane == GROUP * hk + j, dsink[j * CHUNK:(j + 1) * CHUNK], 0.0)
                k_b = jnp.where(half_kv[hk], kk, 0.0).astype(BF16)
                for q, pair in enumerate(_unstack_heads(_dot(ds_b, k_b), hk)):
                    pi = (GROUP // 2) * hk + q
                    dp_ref[rows, Q_A0 + pi * 128:Q_A0 + (pi + 1) * 128] = pair.astype(BF16)
                dkk = dkk + _dot_tn(ds_b, q_b)
                dvv = dvv + _dot_tn(p_b, do_b)
            dp_ref[rows, KV_A0:KV_A0 + KV_W] = (dkk[CHUNK:] + later_kv[:, :KV_W]).astype(BF16)
            dp_ref[rows, KV_A0 + KV_W:KV_A0 + 2 * KV_W] = (dvv[CHUNK:] + later_kv[:, KV_W:]).astype(BF16)
            later_kv = jnp.concatenate([dkk[:CHUNK], dvv[:CHUNK]], axis=1)

            sin4, cos4 = _tile4(sin_ref[rows, :]), _tile4(cos_ref[rows, :])
            dq_parts, dk_parts = [], []
            for h in range(N_RET_HEADS):
                sl = slice(h * RET_HEAD_DIM, (h + 1) * RET_HEAD_DIM)
                qh, kh = rq_ref[rows, sl], rk_ref[rows, sl]
                vh = p_ref[rows, V_R0 + h * RET_HEAD_DIM:V_R0 + (h + 1) * RET_HEAD_DIM].astype(BF16)
                st_b = st_ref[c, h].astype(BF16)
                gs = gs_all[h]
                gs_b = gs.astype(BF16)
                xi_h = xi_ref[:, sl]
                dm = dm_ref[h]
                a_b, on, rs = ra_ref[c, h], on_ref[rows, sl], rs_ref[c, h]
                g = p_ref[rows, G_R0 + h * RET_HEAD_DIM:G_R0 + (h + 1) * RET_HEAD_DIM]
                sg = sg_ref[rows, sl]
                dr = dmix_ref[rows, ATTN_W + h * RET_HEAD_DIM:ATTN_W + (h + 1) * RET_HEAD_DIM].astype(F32)
                dp_ref[rows, G_R0 + h * RET_HEAD_DIM:G_R0 + (h + 1) * RET_HEAD_DIM] = (
                    dr * on * (sg * (1.0 + g * (1.0 - sg)))).astype(BF16)
                don = dr * g * sg
                do = rs * (don - jnp.mean(don, axis=-1, keepdims=True) - on * jnp.mean(don * on, axis=-1, keepdims=True))
                do_b = do.astype(BF16)
                dox_b = (do * xi_h).astype(BF16)
                da_b = (_dot_nt(do_b, vh) * dm).astype(BF16)
                dq_parts.append(_dot(da_b, kh) + _dot_nt(dox_b, st_b))
                dk_parts.append(_dot_tn(da_b, qh) + ze_ref[:, sl] * _dot_nt(vh, gs_b))
                dv = _dot_tn(a_b, do_b) + _dot(rz_ref[rows, sl], gs_b)
                dp_ref[rows, V_R0 + h * RET_HEAD_DIM:V_R0 + (h + 1) * RET_HEAD_DIM] = dv.astype(BF16)
                gs_all[h] = dc_ref[0:1, sl] * gs + _dot_tn(qh, dox_b)
            dq = jnp.concatenate(dq_parts, axis=-1)
            dk = jnp.concatenate(dk_parts, axis=-1)
            dp_ref[rows, Q_R0:Q_R0 + RET_W] = (dq * cos4 - _swap2(dq, even) * sin4).astype(BF16)
            dp_ref[rows, K_R0:K_R0 + RET_W] = (RET_K_SCALE * (dk * cos4 - _swap2(dk, even) * sin4)).astype(BF16)

        for h in range(N_RET_HEADS):
            gstate[h] = gs_all[h]
        ckv[...] = later_kv
        dsk_acc[...] += dsk

        @pl.when(i == nb - 1)
        def _():
            dsk_ref[...] = jnp.sum(dsk_acc[...], axis=0, keepdims=True)

    rev = lambda i: nb - 1 - i
    return _hosted_call(
        body, name="mixer_bwd", grid=(nb,),
        in_specs=[
            pl.BlockSpec((cps * CHUNK, IN_W), lambda i: (rev(i), 0)),
            pl.BlockSpec((CHUNK, 2 * KV_W), lambda i: (jnp.maximum(cps * rev(i) - 1, 0), KV_A0 // (2 * KV_W))),
            pl.BlockSpec((cps * CHUNK, D_MODEL), lambda i: (rev(i), 0)),
            pl.BlockSpec((cps, N_RET_HEADS, RET_HEAD_DIM, RET_HEAD_DIM), lambda i: (rev(i), 0, 0, 0)),
            pl.BlockSpec((cps, groups, GROUP * CHUNK, 2 * CHUNK), lambda i: (rev(i), 0, 0, 0)),
            pl.BlockSpec((cps, groups, GROUP * CHUNK, 1), lambda i: (rev(i), 0, 0, 0)),
            pl.BlockSpec((cps, N_RET_HEADS, CHUNK, CHUNK), lambda i: (rev(i), 0, 0, 0)),
            pl.BlockSpec((cps * CHUNK, RET_W), lambda i: (rev(i), 0)),
            pl.BlockSpec((cps, N_RET_HEADS, CHUNK, 1), lambda i: (rev(i), 0, 0, 0)),
            pl.BlockSpec((cps * CHUNK, RET_W), lambda i: (rev(i), 0)), pl.BlockSpec((cps * CHUNK, RET_W), lambda i: (rev(i), 0)),
            pl.BlockSpec((cps * CHUNK, RET_W), lambda i: (rev(i), 0)),
            pl.BlockSpec((cps, groups, GROUP * CHUNK, 128), lambda i: (rev(i), 0, 0, 0)),
            pl.BlockSpec((cps * CHUNK, RET_W), lambda i: (rev(i), 0)),
            pl.BlockSpec((cps * CHUNK, RET_HEAD_DIM), lambda i: (rev(i), 0)),
            pl.BlockSpec((cps * CHUNK, RET_HEAD_DIM), lambda i: (rev(i), 0)),
            _resident((N_RET_HEADS, CHUNK, CHUNK)), _resident((CHUNK, RET_W)), _resident((CHUNK, RET_W)), _resident((8, RET_W)),
        ],
        out_specs=[pl.BlockSpec((cps * CHUNK, IN_W), lambda i: (rev(i), 0)), pl.BlockSpec((1, 128), lambda i: (0, 0))],
        out_shape=[jax.ShapeDtypeStruct((s, IN_W), BF16), jax.ShapeDtypeStruct((1, 128), F32)],
        scratch_shapes=[pltpu.VMEM((N_RET_HEADS, RET_HEAD_DIM, RET_HEAD_DIM), F32), pltpu.VMEM((CHUNK, 2 * KV_W), F32),
                        pltpu.VMEM((CHUNK, 128), F32)],
        args=(proj, proj, dmix, states, probs, p_sinks, ret_scores, ret_normed, ret_rstd, ret_q, ret_k, ret_kz, stacked_q, gate_sig,
              sin, cos, d_intra, xi_full, zeta_full, decay_full), hosted=hosted)


def _in_proj_bwd(dproj, w_in_t, x, gain, dx1, hosted=None):
    s = x.shape[0]
    tm = min(BIG_TOKEN_TILE, s)

    def body(dp_ref, w_ref, x_ref, g_ref, dx1_ref, dx_ref, gacc_ref):
        @pl.when(pl.program_id(0) == 0)
        def _():
            gacc_ref[...] = jnp.zeros_like(gacc_ref)

        dh = _dot(dp_ref[...], w_ref[...])
        xv = x_ref[...]
        d1, dg = _rms_bwd(dh, xv, _rstd(xv), g_ref[...])
        dx_ref[...] = dx1_ref[...] + d1
        gacc_ref[0:1, :] += dg

    tok = lambda w: pl.BlockSpec((tm, w), lambda i: (i, 0))
    return _hosted_call(
        body, name="in_proj_bwd", grid=(s // tm,),
        in_specs=[tok(IN_W), _resident((IN_W, D_MODEL)), tok(D_MODEL), _resident((1, D_MODEL)), tok(D_MODEL)],
        out_specs=[tok(D_MODEL), pl.BlockSpec((8, D_MODEL), lambda i: (0, 0))],
        out_shape=[jax.ShapeDtypeStruct((s, D_MODEL), F32), jax.ShapeDtypeStruct((8, D_MODEL), F32)],
        scratch_shapes=[], args=(dproj, w_in_t, x, gain, dx1), hosted=hosted)


def _weight_grad(a, b, tn, name, by_block=False, hosted=None):
    s, m = a.shape
    n = b.shape[1]
    tk = min(WEIGHT_GRAD_TOKENS if m <= D_MODEL else WEIGHT_GRAD_TOKENS // 2, s)

    def body(a_ref, b_ref, o_ref):
        @pl.when(pl.program_id(1) == 0)
        def _():
            o_ref[...] = jnp.zeros_like(o_ref)

        o_ref[...] += _dot_tn(a_ref[...], b_ref[...])

    if by_block:
        out_spec = pl.BlockSpec((None, m, tn), lambda j, k: (j, 0, 0))
        out_shape = jax.ShapeDtypeStruct((n // tn, m, tn), F32)
    else:
        out_spec = pl.BlockSpec((m, tn), lambda j, k: (0, j))
        out_shape = jax.ShapeDtypeStruct((m, n), F32)
    (out,), lands = _hosted_call(
        body, name=name, grid=(n // tn, s // tk),
        in_specs=[pl.BlockSpec((tk, m), lambda j, k: (k, 0)), pl.BlockSpec((tk, tn), lambda j, k: (k, j))],
        out_specs=[out_spec], out_shape=[out_shape], scratch_shapes=[], args=(a, b), hosted=hosted)
    return out if hosted is None else (out, lands)


def _place():
    return lax.axis_index("x"), lax.axis_index("y"), lax.axis_index("c")


def _remote(src, dst, send_sems, recv_sems, k, to):
    return pltpu.make_async_remote_copy(src_ref=src, dst_ref=dst, send_sem=send_sems.at[k], recv_sem=recv_sems.at[k],
                                        device_id=to, device_id_type=MESH)


def _gather_level1_copies(w_refs, out_refs, send_sems, recv_sems, local_sems):
    x, y, c = _place()
    mine_at = 2 * x + y
    peers = [(x, y, 1 - c), (1 - x, y, c), (x, 1 - y, c), (1 - x, 1 - y, c)]
    local, sends, recvs = [], [], []
    for i, (w, out) in enumerate(zip(w_refs, out_refs)):
        half = w.shape[0] // 2
        src = w.at[pl.ds(pl.multiple_of(c * half, 16 if half % 16 == 0 else 8), half), :]
        mine = out.at[mine_at, c]
        local.append(pltpu.make_async_copy(src, mine, local_sems.at[i]))
        for k, p in enumerate(peers):
            sends.append(_remote(src, mine, send_sems, recv_sems, 4 * i + k, p))
            lands = out.at[mine_at, 1 - c] if k == 0 else out.at[2 * p[0] + p[1], c]
            recvs.append(_remote(src, lands, send_sems, recv_sems, 4 * i + k, p))
    return local, sends, recvs


def _gather_level1_start(w_refs, out_refs, send_sems, recv_sems, local_sems):
    local, sends, _ = _gather_level1_copies(w_refs, out_refs, send_sems, recv_sems, local_sems)
    for cp in local + sends:
        cp.start()


def _gather_level1_finish(w_refs, out_refs, send_sems, recv_sems, local_sems):
    local, sends, recvs = _gather_level1_copies(w_refs, out_refs, send_sems, recv_sems, local_sems)
    for cp in recvs:
        cp.wait_recv()
    for cp in sends:
        cp.wait_send()
    for cp in local:
        cp.wait()


def _gather_level2_copies(in_refs, out_refs, send_sems, recv_sems, local_sems):
    x, y, c = _place()
    chips = [(1 - x, y), (x, 1 - y), (1 - x, 1 - y)]
    sends, recvs = [], []
    for i, (src, out) in enumerate(zip(in_refs, out_refs)):
        for j, (px, py) in enumerate(chips):
            sends.append(_remote(src.at[2 * px + py, c], out.at[2 * px + py, c], send_sems, recv_sems, 3 * i + j, (x, y, 1 - c)))
            recvs.append(_remote(src.at[2 * px + py, c], out.at[2 * px + py, 1 - c], send_sems, recv_sems, 3 * i + j,
                                 (x, y, 1 - c)))
    return sends, recvs


def _gather_level2_start(in_refs, out_refs, send_sems, recv_sems, local_sems):
    for cp in _gather_level2_copies(in_refs, out_refs, send_sems, recv_sems, local_sems)[0]:
        cp.start()


def _gather_level2_finish(in_refs, out_refs, send_sems, recv_sems, local_sems):
    sends, recvs = _gather_level2_copies(in_refs, out_refs, send_sems, recv_sems, local_sems)
    for cp in recvs:
        cp.wait_recv()
    for cp in sends:
        cp.wait_send()


def _gathered_shape(w):
    r, cols = w.shape
    return jax.ShapeDtypeStruct((N_CHIPS, 2, r // 2, cols), w.dtype)


def _hosted_gather_level1(shards):
    n = len(shards)
    return _Hosted(shards, [_gathered_shape(w) for w in shards], {}, 4 * n, n, _gather_level1_start, _gather_level1_finish)


def _hosted_gather_level2(gathered):
    n = len(gathered)
    return _Hosted(gathered, [jax.ShapeDtypeStruct(g.shape, g.dtype) for g in gathered], {i: i for i in range(n)}, 3 * n, 0,
                   _gather_level2_start, _gather_level2_finish)


def _gather_now(shards, name, seq_len):
    n = len(shards)
    rows = min(512, seq_len)
    angle = 1.0 / jnp.power(10000.0, jnp.linspace(0.0, 1.0, RET_HEAD_DIM // 2, dtype=F32))
    sign = jnp.where(jnp.arange(RET_HEAD_DIM) % 2 == 0, -1.0, 1.0).astype(F32)
    angle_sign = jnp.concatenate([jnp.repeat(angle, 2)[None], sign[None], jnp.zeros((6, RET_HEAD_DIM), F32)], axis=0)

    def body(*refs):
        w_refs, as_ref, out_refs = list(refs[:n]), refs[n], list(refs[n + 1:2 * n + 1])
        sin_ref, cos_ref, send1, recv1, local1, send2, recv2 = refs[2 * n + 1:]
        _gather_level1_start(w_refs, out_refs, send1, recv1, local1)

        def fill(i, carry):
            r0 = pl.multiple_of(i * rows, rows)
            pos = (lax.broadcasted_iota(jnp.int32, (rows, RET_HEAD_DIM), 0) + i * rows).astype(F32)
            arg = pos * as_ref[0:1, :]
            sin_ref[pl.ds(r0, rows), :] = jnp.sin(arg) * as_ref[1:2, :]
            cos_ref[pl.ds(r0, rows), :] = jnp.cos(arg)
            return carry

        lax.fori_loop(0, seq_len // rows, fill, 0)
        _gather_level1_finish(w_refs, out_refs, send1, recv1, local1)
        _gather_level2_start(out_refs, out_refs, send2, recv2, None)
        _gather_level2_finish(out_refs, out_refs, send2, recv2, None)

    hbm, vmem = pl.BlockSpec(memory_space=pl.ANY), pl.BlockSpec(memory_space=pltpu.VMEM)
    table = jax.ShapeDtypeStruct((seq_len, RET_HEAD_DIM), F32)
    res = pl.pallas_call(
        body, name=name, out_shape=[_gathered_shape(w) for w in shards] + [table, table],
        in_specs=[hbm] * n + [vmem], out_specs=[hbm] * n + [vmem, vmem],
        scratch_shapes=[pltpu.SemaphoreType.DMA((4 * n,)), pltpu.SemaphoreType.DMA((4 * n,)), pltpu.SemaphoreType.DMA((n,)),
                        pltpu.SemaphoreType.DMA((3 * n,)), pltpu.SemaphoreType.DMA((3 * n,))],
        compiler_params=_params(),
    )(*shards, angle_sign)
    return res[:n], res[n], res[n + 1]


def _scatter_copies(g_refs, land_refs, send_sems, recv_sems, local_sems):
    x, y, c = _place()
    copies = []
    for i, (g, land) in enumerate(zip(g_refs, land_refs)):
        for k, (px, py, pc) in enumerate(_relations(x, y, c)):
            copies.append(_remote(g.at[2 * px + py, pc], land.at[k], send_sems, recv_sems, 7 * i + k, (px, py, pc)))
    return copies


def _scatter_start(g_refs, land_refs, send_sems, recv_sems, local_sems):
    for cp in _scatter_copies(g_refs, land_refs, send_sems, recv_sems, local_sems):
        cp.start()


def _scatter_finish(g_refs, land_refs, send_sems, recv_sems, local_sems):
    for cp in _scatter_copies(g_refs, land_refs, send_sems, recv_sems, local_sems):
        cp.wait()


def _hosted_scatter(grads):
    lands = [jax.ShapeDtypeStruct((N_DEV - 1,) + g.shape[2:], g.dtype) for g in grads]
    return _Hosted(grads, lands, {}, 7 * len(grads), 0, _scatter_start, _scatter_finish)


def _relations(x, y, c):
    rel = []
    for fx in (0, 1):
        for fy in (0, 1):
            for fc in (0, 1):
                if fx or fy or fc:
                    rel.append(((1 - x) if fx else x, (1 - y) if fy else y, (1 - c) if fc else c))
    return rel


def _join_halves(shards, small):
    n = len(shards)

    def body(*refs):
        in_refs, small_ref, out_refs, all_ref = refs[:n], refs[n], refs[n + 1:2 * n + 1], refs[2 * n + 1]
        send_sems, recv_sems = refs[2 * n + 2:]
        x, y, c = _place()
        slot = lambda p: all_ref.at[4 * p[0] + 2 * p[1] + p[2]]
        all_ref[4 * x + 2 * y + c] = small_ref[...]
        sends = [_remote(src.at[c], out.at[c], send_sems, recv_sems, i, (x, y, 1 - c))
                 for i, (src, out) in enumerate(zip(in_refs, out_refs))]
        recvs = [_remote(src.at[c], out.at[1 - c], send_sems, recv_sems, i, (x, y, 1 - c))
                 for i, (src, out) in enumerate(zip(in_refs, out_refs))]
        for k, p in enumerate(_relations(x, y, c)):
            sends.append(_remote(small_ref, slot((x, y, c)), send_sems, recv_sems, n + k, p))
            recvs.append(_remote(small_ref, slot(p), send_sems, recv_sems, n + k, p))
        for cp in sends:
            cp.start()
        for cp in recvs:
            cp.wait_recv()
        for cp in sends:
            cp.wait_send()

    hbm, vmem = pl.BlockSpec(memory_space=pl.ANY), pl.BlockSpec(memory_space=pltpu.VMEM)
    pairs = n + N_DEV - 1
    res = pl.pallas_call(
        body, name="grad_join_halves",
        out_shape=[jax.ShapeDtypeStruct(t.shape, t.dtype) for t in shards] + [jax.ShapeDtypeStruct((N_DEV,) + small.shape, F32)],
        in_specs=[hbm] * n + [vmem], out_specs=[hbm] * n + [vmem], input_output_aliases={i: i for i in range(n)},
        scratch_shapes=[pltpu.SemaphoreType.DMA((pairs,)), pltpu.SemaphoreType.DMA((pairs,))],
    )(*shards, small)
    return res[:n], res[n]


def _row_tile(rows, row_bytes, limit=1 << 20):
    best = 8
    for t in range(8, rows + 1, 8):
        if rows % t == 0 and t * row_bytes <= limit:
            best = t
    return best


def _sum_pieces(g, land, place, name):
    _, _, rh, cols = g.shape
    tr = _row_tile(rh, (N_DEV - 1) * cols * 4, 4 << 20)

    def body(p_ref, g_ref, l_ref, out_ref):
        acc = g_ref[...]
        for k in range(N_DEV - 1):
            acc = acc + l_ref[k].astype(F32)
        out_ref[...] = acc

    return pl.pallas_call(
        body, name=name,
        grid_spec=pltpu.PrefetchScalarGridSpec(
            num_scalar_prefetch=1, grid=(rh // tr,),
            in_specs=[pl.BlockSpec((None, None, tr, cols), lambda r, p: (p[0], p[1], r, 0)),
                      pl.BlockSpec((N_DEV - 1, tr, cols), lambda r, p: (0, r, 0))],
            out_specs=pl.BlockSpec((None, tr, cols), lambda r, p: (p[1], r, 0))),
        out_shape=jax.ShapeDtypeStruct((2, rh, cols), g.dtype),
        compiler_params=_params(("arbitrary",)),
    )(place, g, land)


def _adamw_math(w, g, m, v):
    m = ADAM_B1 * m + (1.0 - ADAM_B1) * g
    v = ADAM_B2 * v + (1.0 - ADAM_B2) * (g * g)
    m_hat = m / (1.0 - ADAM_B1 ** ADAM_STEP)
    v_hat = v / (1.0 - ADAM_B2 ** ADAM_STEP)
    delta = -ADAM_LR * (m_hat / (jnp.sqrt(v_hat) + ADAM_EPS) + ADAM_WD * w)
    return delta, m, v


def _adamw(w, g, m, v, name):
    r, cols = w.shape
    tr = _row_tile(r, cols * 4)

    def body(w_ref, g_ref, m_ref, v_ref, d_ref, nm_ref, nv_ref):
        d_ref[...], nm_ref[...], nv_ref[...] = _adamw_math(w_ref[...], g_ref[...], m_ref[...], v_ref[...])

    blk = pl.BlockSpec((tr, cols), lambda i: (i, 0))
    shape = jax.ShapeDtypeStruct((r, cols), F32)
    return pl.pallas_call(
        body, name=name, grid=(r // tr,), in_specs=[blk] * 4, out_specs=[blk] * 3, out_shape=[shape] * 3,
        compiler_params=_params(("arbitrary",)),
    )(w, g, m, v)


def _sum_devices(gathered):
    _, r, cols = gathered.shape

    def body(a_ref, g_ref):
        g = a_ref[0]
        for k in range(1, N_DEV):
            g = g + a_ref[k]
        g_ref[...] = g

    return pl.pallas_call(body, name="sum_small_grads", out_shape=jax.ShapeDtypeStruct((r, cols), F32))(gathered)


def _pack_conv(cw):
    flat = cw.reshape(-1)
    return jnp.pad(flat, (0, ROWS_CONV * D_MODEL - flat.shape[0])).reshape(ROWS_CONV, D_MODEL)


def _unpack_conv(rows):
    return rows.reshape(-1)[:CONV_WIDTH * UP_W // N_CHIPS].reshape(CONV_WIDTH, UP_W // N_CHIPS)


def _columns_to_shards(w):
    r, n = w.shape
    return jnp.transpose(w.reshape(r, N_CHIPS, n // N_CHIPS), (1, 0, 2))


def _shards_to_columns(w):
    _, r, n = w.shape
    return jnp.transpose(w, (1, 0, 2)).reshape(r, N_CHIPS * n)


def _pack_small(g_mix_pre, g_mix_post, g_ffn_pre, g_ffn_post, sinks, conv_b, loss):
    pad_row = lambda v: jnp.pad(v.reshape(1, -1), ((0, 0), (0, D_MODEL - v.size)))
    cb = jnp.pad(conv_b.reshape(-1), (0, 6 * D_MODEL - UP_W)).reshape(6, D_MODEL)
    zeros2 = jnp.zeros((2, D_MODEL), F32)
    return jnp.concatenate([g_mix_pre.reshape(1, -1), g_mix_post.reshape(1, -1), g_ffn_pre.reshape(1, -1),
                            g_ffn_post.reshape(1, -1), pad_row(sinks), pad_row(loss), zeros2, cb, zeros2], axis=0)


def _unpack_small(p):
    return dict(mix_pre_norm=p[0:1], mix_post_norm=p[1:2], ffn_pre_norm=p[2:3], ffn_post_norm=p[3:4],
                attn_sinks=p[4:5, :N_ATTN_HEADS], loss=p[5, 0], conv_b=p[8:14].reshape(1, -1)[:, :UP_W],
                conv_w=_unpack_conv(p[SMALL_ROWS:SMALL_ROWS + ROWS_CONV]))


def _local_step(x, target, g_mix_pre, w_in, sinks, w_out, g_mix_post, g_ffn_pre, w_up, conv_w, conv_b, w_down, g_ffn_post,
                distributed=True, rope=None):
    s = x.shape[0]
    consts = _ret_constants()
    sin, cos = _rope_tables(s) if rope is None else rope

    by_half = lambda g, rows: g.reshape(N_CHIPS, 2, rows // (2 * N_CHIPS), g.shape[-1])

    if distributed:
        (h1, proj), level1 = _in_proj(x, g_mix_pre, w_in, _hosted_gather_level1([w_out, w_up, w_down]))
        (mix, states, *kept), (w_out, w_up, w_down) = _mixer_fwd(proj, sinks, sin, cos, consts, _hosted_gather_level2(level1))
        w_out, w_down = w_out.reshape(D_MODEL, D_MODEL), w_down.reshape(D_FF, D_MODEL)
        w_up = w_up.reshape(N_CHIPS, D_MODEL, UP_W // N_CHIPS)
    else:
        (h1, proj), _ = _in_proj(x, g_mix_pre, w_in)
        (mix, states, *kept), _ = _mixer_fwd(proj, sinks, sin, cos, consts)
    mixed, x1, h2, u0 = _out_up_proj(mix, x, w_out, g_mix_post, g_ffn_pre, w_up)
    y, dy2, dout, du, conv_acc, tail_acc = _ffn_tail(u0, x1, target, conv_w, conv_b, w_down, g_ffn_post)
    d_w_down = by_half(_weight_grad(y, dy2, 512, "grad_w_down"), D_FF)
    (du0, dx1, dmixed, dmix, head_acc), down_lands = _ffn_head_bwd(
        du, conv_w, w_up, x1, g_ffn_pre, dout, mixed, g_mix_post, w_out, _hosted_scatter([d_w_down]) if distributed else None)
    d_w_up = _weight_grad(h2, du0, UP_W // N_CHIPS, "grad_w_up", by_block=True)
    d_w_out = _weight_grad(mix, dmixed, D_MODEL, "grad_w_out")
    early = [d_w_down, by_half(d_w_up, N_CHIPS * D_MODEL), by_half(d_w_out, D_MODEL)]
    (dproj, dsinks), early_lands = _mixer_bwd(proj, dmix, states, kept, sin, cos, consts,
                                              _hosted_scatter(early[1:]) if distributed else None)
    early_lands = down_lands + early_lands
    d_w_in_t = _weight_grad(dproj, h1, 512, "grad_w_in")
    late = [by_half(d_w_in_t, IN_W)]
    (grad_x, in_acc), late_lands = _in_proj_bwd(dproj, w_in, x, g_mix_pre, dx1, _hosted_scatter(late) if distributed else None)

    small = _pack_small(in_acc[0], head_acc[1], head_acc[0], tail_acc[0], dsinks[0, :N_ATTN_HEADS], conv_acc[3],
                        jnp.sum(tail_acc[1]))
    d_conv = jnp.pad(conv_acc[0:CONV_WIDTH].reshape(-1), (0, CONV_FULL_ROWS * D_MODEL - CONV_WIDTH * UP_W))
    small = jnp.concatenate([small, d_conv.reshape(CONV_FULL_ROWS, D_MODEL)], axis=0)
    grads = dict(w_down=early[0], w_up=early[1], w_out=early[2], w_in=late[0])
    lands = dict(zip(["w_down", "w_up", "w_out", "w_in"], early_lands + late_lands))
    return grad_x, grads, lands, small


def kernel(x, mix_pre_norm, w_in, attn_sinks, w_out, mix_post_norm, ffn_pre_norm, w_up, conv_w, conv_b, w_down, ffn_post_norm, loss_target, m_mix_pre_norm, m_w_in, m_attn_sinks, m_w_out, m_mix_post_norm, m_ffn_pre_norm, m_w_up, m_conv_w, m_conv_b, m_w_down, m_ffn_post_norm, v_mix_pre_norm, v_w_in, v_attn_sinks, v_w_out, v_mix_post_norm, v_ffn_pre_norm, v_w_up, v_conv_w, v_conv_b, v_w_down, v_ffn_post_norm):
    cx, cy, cc = _place()
    shard = 2 * cx + cy

    conv_rows = jnp.pad(conv_w[0], ((0, 16 - CONV_WIDTH), (0, 0)))
    w_in_t = jnp.swapaxes(w_in[0], 0, 1)
    (w_in_all, conv_all), sin, cos = _gather_now([w_in_t.astype(BF16), conv_rows], "gather_w_in", x.shape[1])
    conv_full = _shards_to_columns(conv_all[:, 0, :CONV_WIDTH])

    grad_x, grads, lands, small = _local_step(
        x[0], loss_target[0], mix_pre_norm, w_in_all.reshape(IN_W, D_MODEL), attn_sinks.reshape(-1), w_out[0].astype(BF16),
        mix_post_norm, ffn_pre_norm, w_up[0].astype(BF16), conv_full, conv_b, w_down[0].astype(BF16), ffn_post_norm,
        rope=(sin, cos))

    place = jnp.stack([shard, cc]).astype(jnp.int32)
    mats = ["w_in", "w_out", "w_up", "w_down"]
    halves = [_sum_pieces(grads[n], lands[n], place, "sum_grad_" + n) for n in mats]
    weights = dict(w_in=(w_in, m_w_in, v_w_in), w_out=(w_out, m_w_out, v_w_out), w_up=(w_up, m_w_up, v_w_up),
                   w_down=(w_down, m_w_down, v_w_down))
    mat_out = {}
    joined_all, small_all = _join_halves(halves, small)
    for n, joined in zip(mats, joined_all):
        w, m, v = (t[0] for t in weights[n])
        if n == "w_in":
            w, m, v = (jnp.swapaxes(t, 0, 1) for t in (w, m, v))
        res = (joined.reshape(w.shape),) + tuple(_adamw(w, joined.reshape(w.shape), m, v, "adamw_" + n))
        mat_out[n] = tuple(jnp.swapaxes(t, 0, 1) for t in res) if n == "w_in" else res

    small_sum = _sum_devices(small_all)
    d_conv_full = small_sum[SMALL_ROWS:].reshape(-1)[:CONV_WIDTH * UP_W].reshape(CONV_WIDTH, UP_W)
    d_conv_mine = lax.dynamic_slice_in_dim(d_conv_full, shard * (UP_W // N_CHIPS), UP_W // N_CHIPS, axis=1)
    g_s = jnp.concatenate([small_sum[:SMALL_ROWS], _pack_conv(d_conv_mine)], axis=0)
    zero = jnp.zeros((), F32)
    pack_rep = lambda a, b, c_, d, e, f, cw: jnp.concatenate([_pack_small(a, b, c_, d, e, f, zero), _pack_conv(cw[0])], axis=0)
    w_s = pack_rep(mix_pre_norm, mix_post_norm, ffn_pre_norm, ffn_post_norm, attn_sinks, conv_b, conv_w)
    m_s = pack_rep(m_mix_pre_norm, m_mix_post_norm, m_ffn_pre_norm, m_ffn_post_norm, m_attn_sinks, m_conv_b, m_conv_w)
    v_s = pack_rep(v_mix_pre_norm, v_mix_post_norm, v_ffn_pre_norm, v_ffn_post_norm, v_attn_sinks, v_conv_b, v_conv_w)
    delta_s, new_m_s, new_v_s = _adamw(w_s, g_s, m_s, v_s, "adamw_small")

    names = ["mix_pre_norm", "w_in", "attn_sinks", "w_out", "mix_post_norm", "ffn_pre_norm", "w_up", "conv_w", "conv_b",
             "w_down", "ffn_post_norm"]

    def leaves(which, packed_small):
        smalls = _unpack_small(packed_small)
        return [mat_out[n][which][None] if n in mat_out else (smalls[n][None] if n == "conv_w" else smalls[n]) for n in names]

    loss = _unpack_small(g_s)["loss"]
    return (loss, grad_x[None], *leaves(0, g_s), *leaves(1, delta_s), *leaves(2, new_m_s), *leaves(3, new_v_s))
```

```python
import math

import jax
import jax.numpy as jnp
from jax import lax
from jax.experimental import pallas as pl
from jax.experimental.pallas import tpu as pltpu

F32 = jnp.float32
BF16 = jnp.bfloat16

D_MODEL = 1024
HEAD_DIM = 64
ATTN_W = 512
N_ATTN_HEADS = 8
KV_W = 128
RET_W = 512
N_RET_HEADS = 4
RET_HEAD_DIM = 128
CHUNK = 128
IN_W = 2816
D_FF = 2816
UP_W = 2 * D_FF
CONV_WIDTH = 3
RMS_EPS = 1e-6
GN_EPS = 1e-6
MASK_VALUE = -1e30
ATTN_SCALE = HEAD_DIM ** -0.5
RET_K_SCALE = RET_HEAD_DIM ** -0.5
GELU_C = math.sqrt(2.0 / math.pi)
GELU_A = 0.044715

ADAM_LR = 0.001
ADAM_B1 = 0.9
ADAM_B2 = 0.999
ADAM_EPS = 1e-08
ADAM_WD = 0.01
ADAM_STEP = 10

N_CHIPS = 4
N_DEV = 8
MESH = pl.DeviceIdType.MESH
VMEM_LIMIT_V7X = 56 * 1024 * 1024
TOKEN_TILE = 256
BIG_TOKEN_TILE = 512
IN_PROJ_TOKEN_TILE = 1024
WEIGHT_GRAD_TOKENS = 2048
FFN_ROW_BLOCK = 64
HEAD_BWD_COLS = 512
MIXER_CHUNKS_PER_STEP = 4
Q_A0, KV_A0, Q_R0, K_R0, V_R0, G_R0 = 0, 512, 768, 1280, 1792, 2304

ROWS_CONV = 8
SMALL_ROWS = 16
CONV_FULL_ROWS = 24


def _params(sem=None, **kw):
    if sem is not None:
        kw["dimension_semantics"] = sem
    return pltpu.CompilerParams(vmem_limit_bytes=VMEM_LIMIT_V7X, **kw)


def _resident(shape):
    zeros = (0,) * len(shape)
    return pl.BlockSpec(shape, lambda *_: zeros, pipeline_mode=pl.Buffered(1))


class _Hosted:
    def __init__(self, ins, outs, aliases, n_pairs, n_local, start, finish):
        self.ins, self.outs, self.aliases = list(ins), list(outs), dict(aliases)
        self.n_pairs, self.n_local, self.start, self.finish = n_pairs, max(n_local, 1), start, finish


def _hosted_call(compute, *, name, grid, in_specs, out_specs, out_shape, scratch_shapes, args, hosted=None):
    params = _params(("arbitrary",) * len(grid))
    if hosted is None:
        res = pl.pallas_call(compute, name=name, grid=grid, in_specs=in_specs, out_specs=out_specs, out_shape=out_shape,
                             scratch_shapes=scratch_shapes, compiler_params=params)(*args)
        return list(res), []
    n_in, n_out, n_scr = len(in_specs), len(out_specs), len(scratch_shapes)
    h_in, h_out = len(hosted.ins), len(hosted.outs)

    def at(step_of):
        cond = pl.program_id(0) == step_of(grid[0])
        for d in range(1, len(grid)):
            cond = jnp.logical_and(cond, pl.program_id(d) == step_of(grid[d]))
        return cond

    def body(*refs):
        ins, refs = refs[:n_in], refs[n_in:]
        h_ins, refs = refs[:h_in], refs[h_in:]
        outs, refs = refs[:n_out], refs[n_out:]
        h_outs, refs = refs[:h_out], refs[h_out:]
        scr, sems = refs[:n_scr], refs[n_scr:]

        @pl.when(at(lambda n: 0))
        def _():
            hosted.start(h_ins, h_outs, *sems)

        compute(*ins, *outs, *scr)

        @pl.when(at(lambda n: n - 1))
        def _():
            hosted.finish(h_ins, h_outs, *sems)

    hbm = pl.BlockSpec(memory_space=pl.ANY)
    res = pl.pallas_call(
        body, name=name, grid=grid,
        in_specs=list(in_specs) + [hbm] * h_in, out_specs=list(out_specs) + [hbm] * h_out,
        out_shape=list(out_shape) + hosted.outs,
        scratch_shapes=list(scratch_shapes) + [pltpu.SemaphoreType.DMA((hosted.n_pairs,)), pltpu.SemaphoreType.DMA((hosted.n_pairs,)),
                                               pltpu.SemaphoreType.DMA((hosted.n_local,))],
        input_output_aliases={n_in + a: n_out + b for a, b in hosted.aliases.items()},
        compiler_params=params,
    )(*args, *hosted.ins)
    return list(res[:n_out]), list(res[n_out:])


def _dot(a, b):
    return jnp.dot(a, b, preferred_element_type=F32)


def _dot_nt(a, b):
    return lax.dot_general(a, b, (((1,), (1,)), ((), ())), preferred_element_type=F32)


def _dot_tn(a, b):
    return lax.dot_general(a, b, (((0,), (0,)), ((), ())), preferred_element_type=F32)


def _shift_matrix(n, by):
    row = lax.broadcasted_iota(jnp.int32, (n, n), 0)
    col = lax.broadcasted_iota(jnp.int32, (n, n), 1)
    return jnp.where(col == row + by, 1.0, 0.0).astype(BF16)


def _rstd(v):
    return lax.rsqrt(jnp.mean(v * v, axis=-1, keepdims=True) + RMS_EPS)


def _rms_bwd(dy, v, rstd, gain):
    n = v * rstd
    dgain = jnp.sum(dy * n, axis=0, keepdims=True)
    dn = dy * gain
    dv = rstd * (dn - n * jnp.mean(dn * n, axis=-1, keepdims=True))
    return dv, dgain


def _lane_lo(shape):
    return (lax.broadcasted_iota(jnp.int32, shape, 1) % 128) < HEAD_DIM


GROUP = N_ATTN_HEADS // (KV_W // HEAD_DIM)


def _attn_bias(first_chunk):
    qi = lax.broadcasted_iota(jnp.int32, (GROUP * CHUNK, 2 * CHUNK), 0) % CHUNK
    kj = lax.broadcasted_iota(jnp.int32, (GROUP * CHUNK, 2 * CHUNK), 1)
    valid = jnp.logical_and(kj > qi, kj <= qi + CHUNK)
    if first_chunk:
        valid = jnp.logical_and(valid, kj >= CHUNK)
    return jnp.where(valid, 0.0, MASK_VALUE)


def _half(shape, hk):
    lo = _lane_lo(shape)
    return lo if hk == 0 else jnp.logical_not(lo)


class _GroupMasks:
    def __init__(self, sk_ref):
        groups = range(KV_W // HEAD_DIM)
        self.q = [_half((CHUNK, 128), hk) for hk in groups]
        self.kv = [_half((2 * CHUNK, 128), hk) for hk in groups]
        self.sinks = [_group_sinks(sk_ref, hk) for hk in groups]


def _stack_heads(ref, row0, col0, hk, half):
    parts = []
    for j in range(GROUP):
        h = GROUP * hk + j
        pair = ref[row0:row0 + CHUNK, col0 + (h // 2) * 128:col0 + (h // 2 + 1) * 128].astype(F32)
        if h % 2 != hk:
            pair = pltpu.roll(pair, HEAD_DIM, 1)
        parts.append(jnp.where(half, pair, 0.0))
    return jnp.concatenate(parts, axis=0)


def _unstack_heads(stacked, hk):
    pairs = []
    for q in range(GROUP // 2):
        even, odd = stacked[2 * q * CHUNK:(2 * q + 1) * CHUNK], stacked[(2 * q + 1) * CHUNK:(2 * q + 2) * CHUNK]
        pairs.append(even + pltpu.roll(odd, HEAD_DIM, 1) if hk == 0 else pltpu.roll(even, HEAD_DIM, 1) + odd)
    return pairs


def _group_sinks(sk_ref, hk):
    row = lax.broadcasted_iota(jnp.int32, (GROUP * CHUNK, 1), 0)
    col = jnp.full((GROUP * CHUNK, 1), sk_ref[GROUP * hk], F32)
    for j in range(1, GROUP):
        col = jnp.where(row >= j * CHUNK, sk_ref[GROUP * hk + j], col)
    return col


def _attn_probs(q_b, kk_b, bias, sink):
    s = _dot_nt(q_b, kk_b) * ATTN_SCALE + bias
    m = jnp.maximum(jnp.max(s, axis=-1, keepdims=True), sink)
    e = jnp.exp(s - m)
    e_sink = jnp.exp(sink - m)
    inv = 1.0 / (jnp.sum(e, axis=-1, keepdims=True) + e_sink)
    return e * inv, e_sink * inv


def _even_lanes(shape):
    return (lax.broadcasted_iota(jnp.int32, shape, 1) % 2) == 0


def _swap2(v, even):
    return jnp.where(even, pltpu.roll(v, v.shape[1] - 1, 1), pltpu.roll(v, 1, 1))


def _tile4(v):
    return jnp.concatenate([v, v, v, v], axis=-1)


def _sigmoid(v):
    return 1.0 / (1.0 + jnp.exp(-v))


def _ret_constants():
    h = N_RET_HEADS
    log_gamma = jnp.log(1.0 - jnp.power(2.0, -5.0 - jnp.arange(h, dtype=F32)))
    idx = jnp.arange(CHUNK, dtype=F32)
    rel = idx[:, None] - idx[None, :]
    d_intra = jnp.where(rel[None] >= 0, jnp.exp(log_gamma[:, None, None] * jnp.maximum(rel, 0.0)[None]), 0.0)
    xi = jnp.exp(log_gamma[None, :] * (idx[:, None] + 1.0))
    zeta = jnp.exp(log_gamma[None, :] * (CHUNK - 1.0 - idx[:, None]))
    decay = jnp.exp(log_gamma * CHUNK)
    xi_full = jnp.repeat(xi, RET_HEAD_DIM, axis=1)
    zeta_full = jnp.repeat(zeta, RET_HEAD_DIM, axis=1)
    decay_full = jnp.broadcast_to(jnp.repeat(decay, RET_HEAD_DIM)[None, :], (8, RET_W))
    return d_intra.astype(F32), xi_full.astype(F32), zeta_full.astype(F32), decay_full.astype(F32)


def _rope_tables(s):
    pos = jnp.arange(s, dtype=F32)
    angle = 1.0 / jnp.power(10000.0, jnp.linspace(0.0, 1.0, RET_HEAD_DIM // 2, dtype=F32))
    angle = jnp.repeat(angle, 2)
    sign = jnp.where(jnp.arange(RET_HEAD_DIM) % 2 == 0, -1.0, 1.0).astype(F32)
    return jnp.sin(pos[:, None] * angle[None]) * sign[None], jnp.cos(pos[:, None] * angle[None])


def _in_proj(x, gain, w_in_t, hosted=None):
    s = x.shape[0]
    tm = min(IN_PROJ_TOKEN_TILE, s)

    def body(x_ref, g_ref, w_ref, h_ref, p_ref):
        xv = x_ref[...]
        h = (xv * _rstd(xv) * g_ref[...]).astype(BF16)
        h_ref[...] = h
        p_ref[...] = _dot_nt(h, w_ref[...])

    return _hosted_call(
        body, name="in_proj", grid=(s // tm,),
        in_specs=[pl.BlockSpec((tm, D_MODEL), lambda i: (i, 0)), _resident((1, D_MODEL)), _resident((IN_W, D_MODEL))],
        out_specs=[pl.BlockSpec((tm, D_MODEL), lambda i: (i, 0)), pl.BlockSpec((tm, IN_W), lambda i: (i, 0))],
        out_shape=[jax.ShapeDtypeStruct((s, D_MODEL), BF16), jax.ShapeDtypeStruct((s, IN_W), F32)],
        scratch_shapes=[], args=(x, gain, w_in_t), hosted=hosted)


def _mixer_fwd(proj, sinks, sin, cos, consts, hosted=None):
    s = proj.shape[0]
    nc = s // CHUNK
    cps = MIXER_CHUNKS_PER_STEP
    groups = KV_W // HEAD_DIM
    d_intra, xi_full, zeta_full, decay_full = consts

    def body(sk_ref, p_ref, pkv_ref, sin_ref, cos_ref, dm_ref, xi_ref, ze_ref, dc_ref,
             mix_ref, st_ref, pr_ref, ps_ref, ra_ref, on_ref, rs_ref, rq_ref, rk_ref, rz_ref, sq_ref, sg_ref, state):
        i = pl.program_id(0)

        @pl.when(i == 0)
        def _():
            state[...] = jnp.zeros_like(state)

        st = [state[h] for h in range(N_RET_HEADS)]
        bias_any = _attn_bias(False)
        bias_c0 = jnp.where(i == 0, _attn_bias(True), bias_any)
        even = _even_lanes((CHUNK, RET_W))
        masks = _GroupMasks(sk_ref)
        for c in range(cps):
            r0 = c * CHUNK
            rows = slice(r0, r0 + CHUNK)

            kv_cur = p_ref[rows, KV_A0:KV_A0 + 2 * KV_W]
            kv_prev = pkv_ref[...] if c == 0 else p_ref[r0 - CHUNK:r0, KV_A0:KV_A0 + 2 * KV_W]
            kk = jnp.concatenate([kv_prev[:, :KV_W], kv_cur[:, :KV_W]], axis=0)
            vv = jnp.concatenate([kv_prev[:, KV_W:], kv_cur[:, KV_W:]], axis=0)
            kk_b = kk.astype(BF16)
            bias = bias_c0 if c == 0 else bias_any
            for hk in range(KV_W // HEAD_DIM):
                q_b = _stack_heads(p_ref, r0, Q_A0, hk, masks.q[hk]).astype(BF16)
                p, p_sink = _attn_probs(q_b, kk_b, bias, masks.sinks[hk])
                p_b = p.astype(BF16)
                pr_ref[c, hk] = p_b
                ps_ref[c, hk] = p_sink
                sq_ref[c, hk] = q_b
                v_b = jnp.where(masks.kv[hk], vv, 0.0).astype(BF16)
                for q, pair in enumerate(_unstack_heads(_dot(p_b, v_b), hk)):
                    pi = (GROUP // 2) * hk + q
                    mix_ref[rows, pi * 128:(pi + 1) * 128] = pair.astype(BF16)

            sin4, cos4 = _tile4(sin_ref[rows, :]), _tile4(cos_ref[rows, :])
            q_r = p_ref[rows, Q_R0:Q_R0 + RET_W]
            k_r = p_ref[rows, K_R0:K_R0 + RET_W] * RET_K_SCALE
            q_r = q_r * cos4 + _swap2(q_r, even) * sin4
            k_r = k_r * cos4 + _swap2(k_r, even) * sin4
            q_b, k_b, kz_b = q_r.astype(BF16), k_r.astype(BF16), (k_r * ze_ref[...]).astype(BF16)
            rq_ref[rows, :], rk_ref[rows, :], rz_ref[rows, :] = q_b, k_b, kz_b
            for h in range(N_RET_HEADS):
                sl = slice(h * RET_HEAD_DIM, (h + 1) * RET_HEAD_DIM)
                qh, kh = q_b[:, sl], k_b[:, sl]
                vh = p_ref[rows, V_R0 + h * RET_HEAD_DIM:V_R0 + (h + 1) * RET_HEAD_DIM].astype(BF16)
                st_ref[c, h] = st[h]
                a_b = (_dot_nt(qh, kh) * dm_ref[h]).astype(BF16)
                qx = (q_r[:, sl] * xi_ref[:, sl]).astype(BF16)
                o = _dot(jnp.concatenate([a_b, qx], axis=1), jnp.concatenate([vh, st[h].astype(BF16)], axis=0))
                st[h] = dc_ref[0:1, sl] * st[h] + _dot_tn(kz_b[:, sl], vh)
                mu = jnp.mean(o, axis=-1, keepdims=True)
                oc = o - mu
                rs = lax.rsqrt(jnp.mean(oc * oc, axis=-1, keepdims=True) + GN_EPS)
                on = oc * rs
                ra_ref[c, h], on_ref[rows, sl], rs_ref[c, h] = a_b, on, rs
                g = p_ref[rows, G_R0 + h * RET_HEAD_DIM:G_R0 + (h + 1) * RET_HEAD_DIM]
                sg = _sigmoid(g)
                sg_ref[rows, sl] = sg
                mix_ref[rows, ATTN_W + h * RET_HEAD_DIM:ATTN_W + (h + 1) * RET_HEAD_DIM] = (g * sg * on).astype(BF16)
        for h in range(N_RET_HEADS):
            state[h] = st[h]

    return _hosted_call(
        body, name="mixer_fwd", grid=(nc // cps,),
        in_specs=[
            pl.BlockSpec(memory_space=pltpu.SMEM),
            pl.BlockSpec((cps * CHUNK, IN_W), lambda i: (i, 0)),
            pl.BlockSpec((CHUNK, 2 * KV_W), lambda i: (jnp.maximum(cps * i - 1, 0), KV_A0 // (2 * KV_W))),
            pl.BlockSpec((cps * CHUNK, RET_HEAD_DIM), lambda i: (i, 0)),
            pl.BlockSpec((cps * CHUNK, RET_HEAD_DIM), lambda i: (i, 0)),
            _resident((N_RET_HEADS, CHUNK, CHUNK)), _resident((CHUNK, RET_W)), _resident((CHUNK, RET_W)), _resident((8, RET_W)),
        ],
        out_specs=[
            pl.BlockSpec((cps * CHUNK, D_MODEL), lambda i: (i, 0)),
            pl.BlockSpec((cps, N_RET_HEADS, RET_HEAD_DIM, RET_HEAD_DIM), lambda i: (i, 0, 0, 0)),
            pl.BlockSpec((cps, groups, GROUP * CHUNK, 2 * CHUNK), lambda i: (i, 0, 0, 0)),
            pl.BlockSpec((cps, groups, GROUP * CHUNK, 1), lambda i: (i, 0, 0, 0)),
            pl.BlockSpec((cps, N_RET_HEADS, CHUNK, CHUNK), lambda i: (i, 0, 0, 0)),
            pl.BlockSpec((cps * CHUNK, RET_W), lambda i: (i, 0)),
            pl.BlockSpec((cps, N_RET_HEADS, CHUNK, 1), lambda i: (i, 0, 0, 0)),
        ] + [pl.BlockSpec((cps * CHUNK, RET_W), lambda i: (i, 0))] * 3 + [
            pl.BlockSpec((cps, groups, GROUP * CHUNK, 128), lambda i: (i, 0, 0, 0)),
            pl.BlockSpec((cps * CHUNK, RET_W), lambda i: (i, 0))],
        out_shape=[jax.ShapeDtypeStruct((s, D_MODEL), BF16),
                   jax.ShapeDtypeStruct((nc, N_RET_HEADS, RET_HEAD_DIM, RET_HEAD_DIM), F32),
                   jax.ShapeDtypeStruct((nc, groups, GROUP * CHUNK, 2 * CHUNK), BF16),
                   jax.ShapeDtypeStruct((nc, groups, GROUP * CHUNK, 1), F32),
                   jax.ShapeDtypeStruct((nc, N_RET_HEADS, CHUNK, CHUNK), BF16),
                   jax.ShapeDtypeStruct((s, RET_W), F32),
                   jax.ShapeDtypeStruct((nc, N_RET_HEADS, CHUNK, 1), F32)] + [jax.ShapeDtypeStruct((s, RET_W), BF16)] * 3 + [
                   jax.ShapeDtypeStruct((nc, groups, GROUP * CHUNK, 128), BF16), jax.ShapeDtypeStruct((s, RET_W), F32)],
        scratch_shapes=[pltpu.VMEM((N_RET_HEADS, RET_HEAD_DIM, RET_HEAD_DIM), F32)],
        args=(sinks, proj, proj, sin, cos, d_intra, xi_full, zeta_full, decay_full), hosted=hosted)


def _out_up_proj(mix, x, w_out, g_post, g_pre, w_up):
    s = x.shape[0]
    tm = min(BIG_TOKEN_TILE, s)
    blk = UP_W // N_CHIPS

    def body(mix_ref, x_ref, wo_ref, g2_ref, g3_ref, wu_ref, mixed_ref, x1_ref, h2_ref, u0_ref):
        mixed = _dot(mix_ref[...], wo_ref[...])
        mixed_ref[...] = mixed
        x1 = x_ref[...] + mixed * _rstd(mixed) * g2_ref[...]
        x1_ref[...] = x1
        h2 = (x1 * _rstd(x1) * g3_ref[...]).astype(BF16)
        h2_ref[...] = h2
        for k in range(N_CHIPS):
            u0_ref[:, k * blk:(k + 1) * blk] = _dot(h2, wu_ref[k]).astype(BF16)

    tok = lambda w: pl.BlockSpec((tm, w), lambda i: (i, 0))
    return pl.pallas_call(
        body, name="out_up_proj", grid=(s // tm,),
        in_specs=[tok(D_MODEL), tok(D_MODEL), _resident((D_MODEL, D_MODEL)), _resident((1, D_MODEL)), _resident((1, D_MODEL)),
                  _resident((N_CHIPS, D_MODEL, blk))],
        out_specs=[tok(D_MODEL), tok(D_MODEL), tok(D_MODEL), tok(UP_W)],
        out_shape=[jax.ShapeDtypeStruct((s, D_MODEL), F32), jax.ShapeDtypeStruct((s, D_MODEL), F32),
                   jax.ShapeDtypeStruct((s, D_MODEL), BF16), jax.ShapeDtypeStruct((s, UP_W), BF16)],
        compiler_params=_params(("arbitrary",)),
    )(mix, x, w_out, g_post, g_pre, w_up)


def _ffn_tail(u0, x1, target, conv_w, conv_b, w_down, g_post):
    s = x1.shape[0]
    tm = TOKEN_TILE
    last = s // tm - 1
    rb, lanes = FFN_ROW_BLOCK, 128

    def body(u0_ref, x1_ref, t_ref, cw_ref, cb_ref, wd_ref, g_ref,
             y_ref, dy2_ref, dout_ref, du_ref, cacc_ref, gacc_ref, u1_s, u2_s, carry, gelu_s, slope_s, dy_s, cacc):
        i = pl.program_id(0)

        @pl.when(i == 0)
        def _():
            carry[...] = jnp.zeros_like(carry)
            cacc[...] = jnp.zeros_like(cacc)
            gacc_ref[...] = jnp.zeros_like(gacc_ref)

        shift1, shift2 = _shift_matrix(tm, -1), _shift_matrix(tm, -2)
        r8 = lax.broadcasted_iota(jnp.int32, (8, 1), 0)
        wide = 2 * lanes

        def shift_block(col):
            cols = slice(col, col + wide)
            u1_s[:, cols] = _dot(shift1, u0_ref[:, cols])
            u2_s[:, cols] = _dot(shift2, u0_ref[:, cols])
            c14, c15 = carry[14:15, cols], carry[15:16, cols]
            u1_s[0:8, cols] = jnp.where(r8 == 0, c15, u1_s[0:8, cols])
            u2_s[0:8, cols] = jnp.where(r8 == 0, c14, jnp.where(r8 == 1, c15, u2_s[0:8, cols]))

        def taps(col):
            return (cw_ref[0:1, col:col + lanes], cw_ref[1:2, col:col + lanes], cw_ref[2:3, col:col + lanes],
                    cb_ref[0:1, col:col + lanes])

        def shifted(r0, col):
            return (u2_s[r0:r0 + rb, col:col + lanes], u1_s[r0:r0 + rb, col:col + lanes],
                    u0_ref[r0:r0 + rb, col:col + lanes].astype(F32))

        def conv(r0, col, w):
            u2, u1, uc = shifted(r0, col)
            return w[0] * u2 + w[1] * u1 + w[2] * uc + w[3]

        fold = lambda v: jnp.sum(v.reshape(rb // 8, 8, lanes), axis=0)

        shift_block(0)
        shift_block(D_FF)
        for j in range(D_FF // lanes):
            cg, cv = j * lanes, D_FF + j * lanes
            if cg % wide == 0 and cg + wide < D_FF:
                shift_block(cg + wide)
                shift_block(cv + wide)
            wg, wv = taps(cg), taps(cv)
            for r0 in range(0, tm, rb):
                gate, val = conv(r0, cg, wg), conv(r0, cv, wv)
                g2 = gate * gate
                th = jnp.tanh(gate * (GELU_C + GELU_C * GELU_A * g2))
                hp = 0.5 * th + 0.5
                gelu = gate * hp
                dgelu = hp + gate * (1.0 - th * th) * (0.5 * GELU_C + 1.5 * GELU_C * GELU_A * g2)
                y_ref[r0:r0 + rb, cg:cg + lanes] = (gelu * val).astype(BF16)
                gelu_s[r0:r0 + rb, cg:cg + lanes] = gelu
                slope_s[r0:r0 + rb, cg:cg + lanes] = dgelu * val

        y2 = _dot(y_ref[...], wd_ref[...])
        r4 = _rstd(y2)
        gain = g_ref[...]
        out = x1_ref[...] + y2 * r4 * gain
        diff = out - t_ref[...]
        dout = diff * (1.0 / D_MODEL)
        dout_ref[...] = dout
        dy2, dgain = _rms_bwd(dout, y2, r4, gain)
        dy2_b = dy2.astype(BF16)
        dy2_ref[...] = dy2_b
        gacc_ref[0:1, :] += dgain
        gacc_ref[1:2, :] += 0.5 * jnp.sum(diff * dout, axis=0, keepdims=True)
        carry[...] = u0_ref[tm - 16:tm, :].astype(F32)

        dy_s[:, 0:wide] = _dot_nt(dy2_b, wd_ref[0:wide, :])
        for j in range(D_FF // lanes):
            cg, cv = j * lanes, D_FF + j * lanes
            if cg % wide == 0 and cg + wide < D_FF:
                dy_s[:, cg + wide:cg + 2 * wide] = _dot_nt(dy2_b, wd_ref[cg + wide:cg + 2 * wide, :])
            acc = [[jnp.zeros((8, lanes), F32) for _ in range(CONV_WIDTH + 1)] for _ in range(2)]
            for r0 in range(0, tm, rb):
                dy = dy_s[r0:r0 + rb, cg:cg + lanes]
                d_gate = dy * slope_s[r0:r0 + rb, cg:cg + lanes]
                d_val = dy * gelu_s[r0:r0 + rb, cg:cg + lanes]
                for side, (col, d) in enumerate(((cg, d_gate), (cv, d_val))):
                    du_ref[r0:r0 + rb, col:col + lanes] = d.astype(BF16)
                    for k, u in enumerate(shifted(r0, col)):
                        acc[side][k] = acc[side][k] + fold(d * u)
                    acc[side][CONV_WIDTH] = acc[side][CONV_WIDTH] + fold(d)
            for side, col in enumerate((cg, cv)):
                for k in range(CONV_WIDTH + 1):
                    cacc[8 * k:8 * k + 8, col:col + lanes] += acc[side][k]

        @pl.when(i == last)
        def _():
            for k in range(CONV_WIDTH + 1):
                cacc_ref[k:k + 1, :] = jnp.sum(cacc[8 * k:8 * k + 8, :], axis=0, keepdims=True)
            cacc_ref[CONV_WIDTH + 1:8, :] = jnp.zeros((8 - CONV_WIDTH - 1, UP_W), F32)

    tok = lambda w: pl.BlockSpec((tm, w), lambda i: (i, 0))
    return pl.pallas_call(
        body, name="ffn_tail", grid=(s // tm,),
        in_specs=[tok(UP_W), tok(D_MODEL), tok(D_MODEL), _resident((CONV_WIDTH, UP_W)), _resident((1, UP_W)),
                  _resident((D_FF, D_MODEL)), _resident((1, D_MODEL))],
        out_specs=[tok(D_FF), tok(D_MODEL), tok(D_MODEL), tok(UP_W),
                   pl.BlockSpec((8, UP_W), lambda i: (0, 0)), pl.BlockSpec((8, D_MODEL), lambda i: (0, 0))],
        out_shape=[jax.ShapeDtypeStruct((s, D_FF), BF16), jax.ShapeDtypeStruct((s, D_MODEL), BF16),
                   jax.ShapeDtypeStruct((s, D_MODEL), F32), jax.ShapeDtypeStruct((s, UP_W), BF16),
                   jax.ShapeDtypeStruct((8, UP_W), F32), jax.ShapeDtypeStruct((8, D_MODEL), F32)],
        scratch_shapes=[pltpu.VMEM((tm, UP_W), F32), pltpu.VMEM((tm, UP_W), F32), pltpu.VMEM((16, UP_W), F32),
                        pltpu.VMEM((tm, D_FF), F32), pltpu.VMEM((tm, D_FF), F32),
                        pltpu.VMEM((tm, D_FF), F32), pltpu.VMEM((8 * (CONV_WIDTH + 1), UP_W), F32)],
        compiler_params=_params(("arbitrary",)),
    )(u0, x1, target, conv_w, conv_b, w_down, g_post)


def _ffn_head_bwd(du, conv_w, w_up, x1, g_pre, dout, mixed, g_post, w_out, hosted=None):
    s = x1.shape[0]
    tm = TOKEN_TILE
    nt = s // tm
    blk = UP_W // N_CHIPS

    def body(du_ref, halo_ref, cw_ref, wu_ref, x1_ref, g3_ref, dout_ref, mixed_ref, g2_ref, wo_ref,
             du0_ref, dx1_ref, dmixed_ref, dmix_ref, gacc_ref, dbuf):
        i = pl.program_id(0)

        @pl.when(i == 0)
        def _():
            gacc_ref[...] = jnp.zeros_like(gacc_ref)

        dbuf[0:tm, :] = du_ref[...].astype(F32)
        dbuf[tm:tm + 16, :] = jnp.where(i < nt - 1, halo_ref[...].astype(F32), 0.0)
        dh2 = jnp.zeros((tm, D_MODEL), F32)
        for k in range(N_CHIPS):
            for c0 in range(0, blk, HEAD_BWD_COLS):
                width = min(HEAD_BWD_COLS, blk - c0)
                cols = slice(k * blk + c0, k * blk + c0 + width)
                du0_b = (cw_ref[2:3, cols] * dbuf[0:tm, cols] + cw_ref[1:2, cols] * dbuf[1:1 + tm, cols]
                         + cw_ref[0:1, cols] * dbuf[2:2 + tm, cols]).astype(BF16)
                du0_ref[:, cols] = du0_b
                dh2 = dh2 + _dot_nt(du0_b, wu_ref[k, :, c0:c0 + width])
        x1 = x1_ref[...]
        d3, dg3 = _rms_bwd(dh2, x1, _rstd(x1), g3_ref[...])
        dx1 = dout_ref[...] + d3
        dx1_ref[...] = dx1
        mixed = mixed_ref[...]
        dmixed, dg2 = _rms_bwd(dx1, mixed, _rstd(mixed), g2_ref[...])
        dmixed_b = dmixed.astype(BF16)
        dmixed_ref[...] = dmixed_b
        dmix_ref[...] = _dot_nt(dmixed_b, wo_ref[...]).astype(BF16)
        gacc_ref[0:1, :] += dg3
        gacc_ref[1:2, :] += dg2

    tok = lambda w: pl.BlockSpec((tm, w), lambda i: (i, 0))
    halo = pl.BlockSpec((16, UP_W), lambda i: (jnp.minimum(i + 1, nt - 1) * (tm // 16), 0))
    return _hosted_call(
        body, name="ffn_head_bwd", grid=(nt,),
        in_specs=[tok(UP_W), halo, _resident((CONV_WIDTH, UP_W)), _resident((N_CHIPS, D_MODEL, blk)), tok(D_MODEL),
                  _resident((1, D_MODEL)), tok(D_MODEL), tok(D_MODEL), _resident((1, D_MODEL)), _resident((D_MODEL, D_MODEL))],
        out_specs=[tok(UP_W), tok(D_MODEL), tok(D_MODEL), tok(D_MODEL), pl.BlockSpec((8, D_MODEL), lambda i: (0, 0))],
        out_shape=[jax.ShapeDtypeStruct((s, UP_W), BF16), jax.ShapeDtypeStruct((s, D_MODEL), F32),
                   jax.ShapeDtypeStruct((s, D_MODEL), BF16), jax.ShapeDtypeStruct((s, D_MODEL), BF16),
                   jax.ShapeDtypeStruct((8, D_MODEL), F32)],
        scratch_shapes=[pltpu.VMEM((tm + 16, UP_W), F32)],
        args=(du, du, conv_w, w_up, x1, g_pre, dout, mixed, g_post, w_out), hosted=hosted)


def _mixer_bwd(proj, dmix, states, kept, sin, cos, consts, hosted=None):
    probs, p_sinks, ret_scores, ret_normed, ret_rstd, ret_q, ret_k, ret_kz, stacked_q, gate_sig = kept
    s = proj.shape[0]
    nc = s // CHUNK
    cps = MIXER_CHUNKS_PER_STEP
    nb = nc // cps
    groups = KV_W // HEAD_DIM
    d_intra, xi_full, zeta_full, decay_full = consts

    def body(p_ref, pkv_ref, dmix_ref, st_ref, pr_ref, ps_ref, ra_ref, on_ref, rs_ref, rq_ref, rk_ref, rz_ref, sq_ref, sg_ref,
             sin_ref, cos_ref, dm_ref, xi_ref, ze_ref, dc_ref, dp_ref, dsk_ref, gstate, ckv, dsk_acc):
        i = pl.program_id(0)
        block = nb - 1 - i

        @pl.when(i == 0)
        def _():
            gstate[...] = jnp.zeros_like(gstate)
            ckv[...] = jnp.zeros_like(ckv)
            dsk_acc[...] = jnp.zeros_like(dsk_acc)

        gs_all = [gstate[h] for h in range(N_RET_HEADS)]
        later_kv = ckv[...]
        lane = lax.broadcasted_iota(jnp.int32, (CHUNK, 128), 1)
        dsk = jnp.zeros((CHUNK, 128), F32)
        even = _even_lanes((CHUNK, RET_W))
        half_q = [_half((CHUNK, 128), hk) for hk in range(groups)]
        half_kv = [_half((2 * CHUNK, 128), hk) for hk in range(groups)]
        for c in reversed(range(cps)):
            r0 = c * CHUNK
            rows = slice(r0, r0 + CHUNK)

            kv_cur = p_ref[rows, KV_A0:KV_A0 + 2 * KV_W]
            kv_prev = pkv_ref[...] if c == 0 else p_ref[r0 - CHUNK:r0, KV_A0:KV_A0 + 2 * KV_W]
            kk = jnp.concatenate([kv_prev[:, :KV_W], kv_cur[:, :KV_W]], axis=0)
            vv = jnp.concatenate([kv_prev[:, KV_W:], kv_cur[:, KV_W:]], axis=0)
            vv_b = vv.astype(BF16)
            dkk = jnp.zeros((2 * CHUNK, KV_W), F32)
            dvv = jnp.zeros((2 * CHUNK, KV_W), F32)
            for hk in range(groups):
                q_b = sq_ref[c, hk]
                do_b = _stack_heads(dmix_ref, r0, 0, hk, half_q[hk]).astype(BF16)
                p_b = pr_ref[c, hk]
                p = p_b.astype(F32)
                dpr = _dot_nt(do_b, vv_b)
                delta = jnp.sum(p * dpr, axis=-1, keepdims=True)
                ds_b = (p * (dpr - delta) * ATTN_SCALE).astype(BF16)
                dsink = -ps_ref[c, hk] * delta
                for j in range(GROUP):
                    dsk = dsk + jnp.where(lane == GROUP * hk + j, dsink[j * CHUNK:(j + 1) * CHUNK], 0.0)
                k_b = jnp.where(half_kv[hk], kk, 0.0).astype(BF16)
                for q, pair in enumerate(_unstack_heads(_dot(ds_b, k_b), hk)):
                    pi = (GROUP // 2) * hk + q
                    dp_ref[rows, Q_A0 + pi * 128:Q_A0 + (pi + 1) * 128] = pair.astype(BF16)
                dkk = dkk + _dot_tn(ds_b, q_b)
                dvv = dvv + _dot_tn(p_b, do_b)
            dp_ref[rows, KV_A0:KV_A0 + KV_W] = (dkk[CHUNK:] + later_kv[:, :KV_W]).astype(BF16)
            dp_ref[rows, KV_A0 + KV_W:KV_A0 + 2 * KV_W] = (dvv[CHUNK:] + later_kv[:, KV_W:]).astype(BF16)
            later_kv = jnp.concatenate([dkk[:CHUNK], dvv[:CHUNK]], axis=1)

            sin4, cos4 = _tile4(sin_ref[rows, :]), _tile4(cos_ref[rows, :])
            dq_parts, dk_parts = [], []
            for h in range(N_RET_HEADS):
                sl = slice(h * RET_HEAD_DIM, (h + 1) * RET_HEAD_DIM)
                qh, kh = rq_ref[rows, sl], rk_ref[rows, sl]
                vh = p_ref[rows, V_R0 + h * RET_HEAD_DIM:V_R0 + (h + 1) * RET_HEAD_DIM].astype(BF16)
                st_b = st_ref[c, h].astype(BF16)
                gs = gs_all[h]
                gs_b = gs.astype(BF16)
                xi_h = xi_ref[:, sl]
                dm = dm_ref[h]
                a_b, on, rs = ra_ref[c, h], on_ref[rows, sl], rs_ref[c, h]
                g = p_ref[rows, G_R0 + h * RET_HEAD_DIM:G_R0 + (h + 1) * RET_HEAD_DIM]
                sg = sg_ref[rows, sl]
                dr = dmix_ref[rows, ATTN_W + h * RET_HEAD_DIM:ATTN_W + (h + 1) * RET_HEAD_DIM].astype(F32)
                dp_ref[rows, G_R0 + h * RET_HEAD_DIM:G_R0 + (h + 1) * RET_HEAD_DIM] = (
                    dr * on * (sg * (1.0 + g * (1.0 - sg)))).astype(BF16)
                don = dr * g * sg
                do = rs * (don - jnp.mean(don, axis=-1, keepdims=True) - on * jnp.mean(don * on, axis=-1, keepdims=True))
                do_b = do.astype(BF16)
                dox_b = (do * xi_h).astype(BF16)
                da_b = (_dot_nt(do_b, vh) * dm).astype(BF16)
                dq_parts.append(_dot(da_b, kh) + _dot_nt(dox_b, st_b))
                dk_parts.append(_dot_tn(da_b, qh) + ze_ref[:, sl] * _dot_nt(vh, gs_b))
                dv = _dot_tn(a_b, do_b) + _dot(rz_ref[rows, sl], gs_b)
                dp_ref[rows, V_R0 + h * RET_HEAD_DIM:V_R0 + (h + 1) * RET_HEAD_DIM] = dv.astype(BF16)
                gs_all[h] = dc_ref[0:1, sl] * gs + _dot_tn(qh, dox_b)
            dq = jnp.concatenate(dq_parts, axis=-1)
            dk = jnp.concatenate(dk_parts, axis=-1)
            dp_ref[rows, Q_R0:Q_R0 + RET_W] = (dq * cos4 - _swap2(dq, even) * sin4).astype(BF16)
            dp_ref[rows, K_R0:K_R0 + RET_W] = (RET_K_SCALE * (dk * cos4 - _swap2(dk, even) * sin4)).astype(BF16)

        for h in range(N_RET_HEADS):
            gstate[h] = gs_all[h]
        ckv[...] = later_kv
        dsk_acc[...] += dsk

        @pl.when(i == nb - 1)
        def _():
            dsk_ref[...] = jnp.sum(dsk_acc[...], axis=0, keepdims=True)

    rev = lambda i: nb - 1 - i
    return _hosted_call(
        body, name="mixer_bwd", grid=(nb,),
        in_specs=[
            pl.BlockSpec((cps * CHUNK, IN_W), lambda i: (rev(i), 0)),
            pl.BlockSpec((CHUNK, 2 * KV_W), lambda i: (jnp.maximum(cps * rev(i) - 1, 0), KV_A0 // (2 * KV_W))),
            pl.BlockSpec((cps * CHUNK, D_MODEL), lambda i: (rev(i), 0)),
            pl.BlockSpec((cps, N_RET_HEADS, RET_HEAD_DIM, RET_HEAD_DIM), lambda i: (rev(i), 0, 0, 0)),
            pl.BlockSpec((cps, groups, GROUP * CHUNK, 2 * CHUNK), lambda i: (rev(i), 0, 0, 0)),
            pl.BlockSpec((cps, groups, GROUP * CHUNK, 1), lambda i: (rev(i), 0, 0, 0)),
            pl.BlockSpec((cps, N_RET_HEADS, CHUNK, CHUNK), lambda i: (rev(i), 0, 0, 0)),
            pl.BlockSpec((cps * CHUNK, RET_W), lambda i: (rev(i), 0)),
            pl.BlockSpec((cps, N_RET_HEADS, CHUNK, 1), lambda i: (rev(i), 0, 0, 0)),
            pl.BlockSpec((cps * CHUNK, RET_W), lambda i: (rev(i), 0)), pl.BlockSpec((cps * CHUNK, RET_W), lambda i: (rev(i), 0)),
            pl.BlockSpec((cps * CHUNK, RET_W), lambda i: (rev(i), 0)),
            pl.BlockSpec((cps, groups, GROUP * CHUNK, 128), lambda i: (rev(i), 0, 0, 0)),
            pl.BlockSpec((cps * CHUNK, RET_W), lambda i: (rev(i), 0)),
            pl.BlockSpec((cps * CHUNK, RET_HEAD_DIM), lambda i: (rev(i), 0)),
            pl.BlockSpec((cps * CHUNK, RET_HEAD_DIM), lambda i: (rev(i), 0)),
            _resident((N_RET_HEADS, CHUNK, CHUNK)), _resident((CHUNK, RET_W)), _resident((CHUNK, RET_W)), _resident((8, RET_W)),
        ],
        out_specs=[pl.BlockSpec((cps * CHUNK, IN_W), lambda i: (rev(i), 0)), pl.BlockSpec((1, 128), lambda i: (0, 0))],
        out_shape=[jax.ShapeDtypeStruct((s, IN_W), BF16), jax.ShapeDtypeStruct((1, 128), F32)],
        scratch_shapes=[pltpu.VMEM((N_RET_HEADS, RET_HEAD_DIM, RET_HEAD_DIM), F32), pltpu.VMEM((CHUNK, 2 * KV_W), F32),
                        pltpu.VMEM((CHUNK, 128), F32)],
        args=(proj, proj, dmix, states, probs, p_sinks, ret_scores, ret_normed, ret_rstd, ret_q, ret_k, ret_kz, stacked_q, gate_sig,
              sin, cos, d_intra, xi_full, zeta_full, decay_full), hosted=hosted)


def _in_proj_bwd(dproj, w_in_t, x, gain, dx1, hosted=None):
    s = x.shape[0]
    tm = min(BIG_TOKEN_TILE, s)

    def body(dp_ref, w_ref, x_ref, g_ref, dx1_ref, dx_ref, gacc_ref):
        @pl.when(pl.program_id(0) == 0)
        def _():
            gacc_ref[...] = jnp.zeros_like(gacc_ref)

        dh = _dot(dp_ref[...], w_ref[...])
        xv = x_ref[...]
        d1, dg = _rms_bwd(dh, xv, _rstd(xv), g_ref[...])
        dx_ref[...] = dx1_ref[...] + d1
        gacc_ref[0:1, :] += dg

    tok = lambda w: pl.BlockSpec((tm, w), lambda i: (i, 0))
    return _hosted_call(
        body, name="in_proj_bwd", grid=(s // tm,),
        in_specs=[tok(IN_W), _resident((IN_W, D_MODEL)), tok(D_MODEL), _resident((1, D_MODEL)), tok(D_MODEL)],
        out_specs=[tok(D_MODEL), pl.BlockSpec((8, D_MODEL), lambda i: (0, 0))],
        out_shape=[jax.ShapeDtypeStruct((s, D_MODEL), F32), jax.ShapeDtypeStruct((8, D_MODEL), F32)],
        scratch_shapes=[], args=(dproj, w_in_t, x, gain, dx1), hosted=hosted)


def _weight_grad(a, b, tn, name, by_block=False, hosted=None):
    s, m = a.shape
    n = b.shape[1]
    tk = min(WEIGHT_GRAD_TOKENS if m <= D_MODEL else WEIGHT_GRAD_TOKENS // 2, s)

    def body(a_ref, b_ref, o_ref):
        @pl.when(pl.program_id(1) == 0)
        def _():
            o_ref[...] = jnp.zeros_like(o_ref)

        o_ref[...] += _dot_tn(a_ref[...], b_ref[...])

    if by_block:
        out_spec = pl.BlockSpec((None, m, tn), lambda j, k: (j, 0, 0))
        out_shape = jax.ShapeDtypeStruct((n // tn, m, tn), F32)
    else:
        out_spec = pl.BlockSpec((m, tn), lambda j, k: (0, j))
        out_shape = jax.ShapeDtypeStruct((m, n), F32)
    (out,), lands = _hosted_call(
        body, name=name, grid=(n // tn, s // tk),
        in_specs=[pl.BlockSpec((tk, m), lambda j, k: (k, 0)), pl.BlockSpec((tk, tn), lambda j, k: (k, j))],
        out_specs=[out_spec], out_shape=[out_shape], scratch_shapes=[], args=(a, b), hosted=hosted)
    return out if hosted is None else (out, lands)


def _place():
    return lax.axis_index("x"), lax.axis_index("y"), lax.axis_index("c")


def _remote(src, dst, send_sems, recv_sems, k, to):
    return pltpu.make_async_remote_copy(src_ref=src, dst_ref=dst, send_sem=send_sems.at[k], recv_sem=recv_sems.at[k],
                                        device_id=to, device_id_type=MESH)


def _gather_level1_copies(w_refs, out_refs, send_sems, recv_sems, local_sems):
    x, y, c = _place()
    mine_at = 2 * x + y
    peers = [(x, y, 1 - c), (1 - x, y, c), (x, 1 - y, c), (1 - x, 1 - y, c)]
    local, sends, recvs = [], [], []
    for i, (w, out) in enumerate(zip(w_refs, out_refs)):
        half = w.shape[0] // 2
        src = w.at[pl.ds(pl.multiple_of(c * half, 16 if half % 16 == 0 else 8), half), :]
        mine = out.at[mine_at, c]
        local.append(pltpu.make_async_copy(src, mine, local_sems.at[i]))
        for k, p in enumerate(peers):
            sends.append(_remote(src, mine, send_sems, recv_sems, 4 * i + k, p))
            lands = out.at[mine_at, 1 - c] if k == 0 else out.at[2 * p[0] + p[1], c]
            recvs.append(_remote(src, lands, send_sems, recv_sems, 4 * i + k, p))
    return local, sends, recvs


def _gather_level1_start(w_refs, out_refs, send_sems, recv_sems, local_sems):
    local, sends, _ = _gather_level1_copies(w_refs, out_refs, send_sems, recv_sems, local_sems)
    for cp in local + sends:
        cp.start()


def _gather_level1_finish(w_refs, out_refs, send_sems, recv_sems, local_sems):
    local, sends, recvs = _gather_level1_copies(w_refs, out_refs, send_sems, recv_sems, local_sems)
    for cp in recvs:
        cp.wait_recv()
    for cp in sends:
        cp.wait_send()
    for cp in local:
        cp.wait()


def _gather_level2_copies(in_refs, out_refs, send_sems, recv_sems, local_sems):
    x, y, c = _place()
    chips = [(1 - x, y), (x, 1 - y), (1 - x, 1 - y)]
    sends, recvs = [], []
    for i, (src, out) in enumerate(zip(in_refs, out_refs)):
        for j, (px, py) in enumerate(chips):
            sends.append(_remote(src.at[2 * px + py, c], out.at[2 * px + py, c], send_sems, recv_sems, 3 * i + j, (x, y, 1 - c)))
            recvs.append(_remote(src.at[2 * px + py, c], out.at[2 * px + py, 1 - c], send_sems, recv_sems, 3 * i + j,
                                 (x, y, 1 - c)))
    return sends, recvs


def _gather_level2_start(in_refs, out_refs, send_sems, recv_sems, local_sems):
    for cp in _gather_level2_copies(in_refs, out_refs, send_sems, recv_sems, local_sems)[0]:
        cp.start()


def _gather_level2_finish(in_refs, out_refs, send_sems, recv_sems, local_sems):
    sends, recvs = _gather_level2_copies(in_refs, out_refs, send_sems, recv_sems, local_sems)
    for cp in recvs:
        cp.wait_recv()
    for cp in sends:
        cp.wait_send()


def _gathered_shape(w):
    r, cols = w.shape
    return jax.ShapeDtypeStruct((N_CHIPS, 2, r // 2, cols), w.dtype)


def _hosted_gather_level1(shards):
    n = len(shards)
    return _Hosted(shards, [_gathered_shape(w) for w in shards], {}, 4 * n, n, _gather_level1_start, _gather_level1_finish)


def _hosted_gather_level2(gathered):
    n = len(gathered)
    return _Hosted(gathered, [jax.ShapeDtypeStruct(g.shape, g.dtype) for g in gathered], {i: i for i in range(n)}, 3 * n, 0,
                   _gather_level2_start, _gather_level2_finish)


def _gather_now(shards, name, seq_len):
    n = len(shards)
    rows = min(512, seq_len)
    angle = 1.0 / jnp.power(10000.0, jnp.linspace(0.0, 1.0, RET_HEAD_DIM // 2, dtype=F32))
    sign = jnp.where(jnp.arange(RET_HEAD_DIM) % 2 == 0, -1.0, 1.0).astype(F32)
    angle_sign = jnp.concatenate([jnp.repeat(angle, 2)[None], sign[None], jnp.zeros((6, RET_HEAD_DIM), F32)], axis=0)

    def body(*refs):
        w_refs, as_ref, out_refs = list(refs[:n]), refs[n], list(refs[n + 1:2 * n + 1])
        sin_ref, cos_ref, send1, recv1, local1, send2, recv2 = refs[2 * n + 1:]
        _gather_level1_start(w_refs, out_refs, send1, recv1, local1)

        def fill(i, carry):
            r0 = pl.multiple_of(i * rows, rows)
            pos = (lax.broadcasted_iota(jnp.int32, (rows, RET_HEAD_DIM), 0) + i * rows).astype(F32)
            arg = pos * as_ref[0:1, :]
            sin_ref[pl.ds(r0, rows), :] = jnp.sin(arg) * as_ref[1:2, :]
            cos_ref[pl.ds(r0, rows), :] = jnp.cos(arg)
            return carry

        lax.fori_loop(0, seq_len // rows, fill, 0)
        _gather_level1_finish(w_refs, out_refs, send1, recv1, local1)
        _gather_level2_start(out_refs, out_refs, send2, recv2, None)
        _gather_level2_finish(out_refs, out_refs, send2, recv2, None)

    hbm, vmem = pl.BlockSpec(memory_space=pl.ANY), pl.BlockSpec(memory_space=pltpu.VMEM)
    table = jax.ShapeDtypeStruct((seq_len, RET_HEAD_DIM), F32)
    res = pl.pallas_call(
        body, name=name, out_shape=[_gathered_shape(w) for w in shards] + [table, table],
        in_specs=[hbm] * n + [vmem], out_specs=[hbm] * n + [vmem, vmem],
        scratch_shapes=[pltpu.SemaphoreType.DMA((4 * n,)), pltpu.SemaphoreType.DMA((4 * n,)), pltpu.SemaphoreType.DMA((n,)),
                        pltpu.SemaphoreType.DMA((3 * n,)), pltpu.SemaphoreType.DMA((3 * n,))],
        compiler_params=_params(),
    )(*shards, angle_sign)
    return res[:n], res[n], res[n + 1]


def _scatter_copies(g_refs, land_refs, send_sems, recv_sems, local_sems):
    x, y, c = _place()
    copies = []
    for i, (g, land) in enumerate(zip(g_refs, land_refs)):
        for k, (px, py, pc) in enumerate(_relations(x, y, c)):
            copies.append(_remote(g.at[2 * px + py, pc], land.at[k], send_sems, recv_sems, 7 * i + k, (px, py, pc)))
    return copies


def _scatter_start(g_refs, land_refs, send_sems, recv_sems, local_sems):
    for cp in _scatter_copies(g_refs, land_refs, send_sems, recv_sems, local_sems):
        cp.start()


def _scatter_finish(g_refs, land_refs, send_sems, recv_sems, local_sems):
    for cp in _scatter_copies(g_refs, land_refs, send_sems, recv_sems, local_sems):
        cp.wait()


def _hosted_scatter(grads):
    lands = [jax.ShapeDtypeStruct((N_DEV - 1,) + g.shape[2:], g.dtype) for g in grads]
    return _Hosted(grads, lands, {}, 7 * len(grads), 0, _scatter_start, _scatter_finish)


def _relations(x, y, c):
    rel = []
    for fx in (0, 1):
        for fy in (0, 1):
            for fc in (0, 1):
                if fx or fy or fc:
                    rel.append(((1 - x) if fx else x, (1 - y) if fy else y, (1 - c) if fc else c))
    return rel


def _join_halves(shards, small):
    n = len(shards)

    def body(*refs):
        in_refs, small_ref, out_refs, all_ref = refs[:n], refs[n], refs[n + 1:2 * n + 1], refs[2 * n + 1]
        send_sems, recv_sems = refs[2 * n + 2:]
        x, y, c = _place()
        slot = lambda p: all_ref.at[4 * p[0] + 2 * p[1] + p[2]]
        all_ref[4 * x + 2 * y + c] = small_ref[...]
        sends = [_remote(src.at[c], out.at[c], send_sems, recv_sems, i, (x, y, 1 - c))
                 for i, (src, out) in enumerate(zip(in_refs, out_refs))]
        recvs = [_remote(src.at[c], out.at[1 - c], send_sems, recv_sems, i, (x, y, 1 - c))
                 for i, (src, out) in enumerate(zip(in_refs, out_refs))]
        for k, p in enumerate(_relations(x, y, c)):
            sends.append(_remote(small_ref, slot((x, y, c)), send_sems, recv_sems, n + k, p))
            recvs.append(_remote(small_ref, slot(p), send_sems, recv_sems, n + k, p))
        for cp in sends:
            cp.start()
        for cp in recvs:
            cp.wait_recv()
        for cp in sends:
            cp.wait_send()

    hbm, vmem = pl.BlockSpec(memory_space=pl.ANY), pl.BlockSpec(memory_space=pltpu.VMEM)
    pairs = n + N_DEV - 1
    res = pl.pallas_call(
        body, name="grad_join_halves",
        out_shape=[jax.ShapeDtypeStruct(t.shape, t.dtype) for t in shards] + [jax.ShapeDtypeStruct((N_DEV,) + small.shape, F32)],
        in_specs=[hbm] * n + [vmem], out_specs=[hbm] * n + [vmem], input_output_aliases={i: i for i in range(n)},
        scratch_shapes=[pltpu.SemaphoreType.DMA((pairs,)), pltpu.SemaphoreType.DMA((pairs,))],
    )(*shards, small)
    return res[:n], res[n]


def _row_tile(rows, row_bytes, limit=1 << 20):
    best = 8
    for t in range(8, rows + 1, 8):
        if rows % t == 0 and t * row_bytes <= limit:
            best = t
    return best


def _sum_pieces(g, land, place, name):
    _, _, rh, cols = g.shape
    tr = _row_tile(rh, (N_DEV - 1) * cols * 4, 4 << 20)

    def body(p_ref, g_ref, l_ref, out_ref):
        acc = g_ref[...]
        for k in range(N_DEV - 1):
            acc = acc + l_ref[k].astype(F32)
        out_ref[...] = acc

    return pl.pallas_call(
        body, name=name,
        grid_spec=pltpu.PrefetchScalarGridSpec(
            num_scalar_prefetch=1, grid=(rh // tr,),
            in_specs=[pl.BlockSpec((None, None, tr, cols), lambda r, p: (p[0], p[1], r, 0)),
                      pl.BlockSpec((N_DEV - 1, tr, cols), lambda r, p: (0, r, 0))],
            out_specs=pl.BlockSpec((None, tr, cols), lambda r, p: (p[1], r, 0))),
        out_shape=jax.ShapeDtypeStruct((2, rh, cols), g.dtype),
        compiler_params=_params(("arbitrary",)),
    )(place, g, land)


def _adamw_math(w, g, m, v):
    m = ADAM_B1 * m + (1.0 - ADAM_B1) * g
    v = ADAM_B2 * v + (1.0 - ADAM_B2) * (g * g)
    m_hat = m / (1.0 - ADAM_B1 ** ADAM_STEP)
    v_hat = v / (1.0 - ADAM_B2 ** ADAM_STEP)
    delta = -ADAM_LR * (m_hat / (jnp.sqrt(v_hat) + ADAM_EPS) + ADAM_WD * w)
    return delta, m, v


def _adamw(w, g, m, v, name):
    r, cols = w.shape
    tr = _row_tile(r, cols * 4)

    def body(w_ref, g_ref, m_ref, v_ref, d_ref, nm_ref, nv_ref):
        d_ref[...], nm_ref[...], nv_ref[...] = _adamw_math(w_ref[...], g_ref[...], m_ref[...], v_ref[...])

    blk = pl.BlockSpec((tr, cols), lambda i: (i, 0))
    shape = jax.ShapeDtypeStruct((r, cols), F32)
    return pl.pallas_call(
        body, name=name, grid=(r // tr,), in_specs=[blk] * 4, out_specs=[blk] * 3, out_shape=[shape] * 3,
        compiler_params=_params(("arbitrary",)),
    )(w, g, m, v)


def _sum_devices(gathered):
    _, r, cols = gathered.shape

    def body(a_ref, g_ref):
        g = a_ref[0]
        for k in range(1, N_DEV):
            g = g + a_ref[k]
        g_ref[...] = g

    return pl.pallas_call(body, name="sum_small_grads", out_shape=jax.ShapeDtypeStruct((r, cols), F32))(gathered)


def _pack_conv(cw):
    flat = cw.reshape(-1)
    return jnp.pad(flat, (0, ROWS_CONV * D_MODEL - flat.shape[0])).reshape(ROWS_CONV, D_MODEL)


def _unpack_conv(rows):
    return rows.reshape(-1)[:CONV_WIDTH * UP_W // N_CHIPS].reshape(CONV_WIDTH, UP_W // N_CHIPS)


def _columns_to_shards(w):
    r, n = w.shape
    return jnp.transpose(w.reshape(r, N_CHIPS, n // N_CHIPS), (1, 0, 2))


def _shards_to_columns(w):
    _, r, n = w.shape
    return jnp.transpose(w, (1, 0, 2)).reshape(r, N_CHIPS * n)


def _pack_small(g_mix_pre, g_mix_post, g_ffn_pre, g_ffn_post, sinks, conv_b, loss):
    pad_row = lambda v: jnp.pad(v.reshape(1, -1), ((0, 0), (0, D_MODEL - v.size)))
    cb = jnp.pad(conv_b.reshape(-1), (0, 6 * D_MODEL - UP_W)).reshape(6, D_MODEL)
    zeros2 = jnp.zeros((2, D_MODEL), F32)
    return jnp.concatenate([g_mix_pre.reshape(1, -1), g_mix_post.reshape(1, -1), g_ffn_pre.reshape(1, -1),
                            g_ffn_post.reshape(1, -1), pad_row(sinks), pad_row(loss), zeros2, cb, zeros2], axis=0)


def _unpack_small(p):
    return dict(mix_pre_norm=p[0:1], mix_post_norm=p[1:2], ffn_pre_norm=p[2:3], ffn_post_norm=p[3:4],
                attn_sinks=p[4:5, :N_ATTN_HEADS], loss=p[5, 0], conv_b=p[8:14].reshape(1, -1)[:, :UP_W],
                conv_w=_unpack_conv(p[SMALL_ROWS:SMALL_ROWS + ROWS_CONV]))


def _local_step(x, target, g_mix_pre, w_in, sinks, w_out, g_mix_post, g_ffn_pre, w_up, conv_w, conv_b, w_down, g_ffn_post,
                distributed=True, rope=None):
    s = x.shape[0]
    consts = _ret_constants()
    sin, cos = _rope_tables(s) if rope is None else rope

    by_half = lambda g, rows: g.reshape(N_CHIPS, 2, rows // (2 * N_CHIPS), g.shape[-1])

    if distributed:
        (h1, proj), level1 = _in_proj(x, g_mix_pre, w_in, _hosted_gather_level1([w_out, w_up, w_down]))
        (mix, states, *kept), (w_out, w_up, w_down) = _mixer_fwd(proj, sinks, sin, cos, consts, _hosted_gather_level2(level1))
        w_out, w_down = w_out.reshape(D_MODEL, D_MODEL), w_down.reshape(D_FF, D_MODEL)
        w_up = w_up.reshape(N_CHIPS, D_MODEL, UP_W // N_CHIPS)
    else:
        (h1, proj), _ = _in_proj(x, g_mix_pre, w_in)
        (mix, states, *kept), _ = _mixer_fwd(proj, sinks, sin, cos, consts)
    mixed, x1, h2, u0 = _out_up_proj(mix, x, w_out, g_mix_post, g_ffn_pre, w_up)
    y, dy2, dout, du, conv_acc, tail_acc = _ffn_tail(u0, x1, target, conv_w, conv_b, w_down, g_ffn_post)
    d_w_down = by_half(_weight_grad(y, dy2, 512, "grad_w_down"), D_FF)
    (du0, dx1, dmixed, dmix, head_acc), _ = _ffn_head_bwd(du, conv_w, w_up, x1, g_ffn_pre, dout, mixed, g_mix_post, w_out)
    if distributed:
        d_w_up, down_lands = _weight_grad(h2, du0, UP_W // N_CHIPS, "grad_w_up", by_block=True,
                                          hosted=_hosted_scatter([d_w_down]))
    else:
        d_w_up, down_lands = _weight_grad(h2, du0, UP_W // N_CHIPS, "grad_w_up", by_block=True), []
    d_w_out = _weight_grad(mix, dmixed, D_MODEL, "grad_w_out")
    early = [d_w_down, by_half(d_w_up, N_CHIPS * D_MODEL), by_half(d_w_out, D_MODEL)]
    (dproj, dsinks), early_lands = _mixer_bwd(proj, dmix, states, kept, sin, cos, consts,
                                              _hosted_scatter(early[1:]) if distributed else None)
    early_lands = down_lands + early_lands
    d_w_in_t = _weight_grad(dproj, h1, 512, "grad_w_in")
    late = [by_half(d_w_in_t, IN_W)]
    (grad_x, in_acc), late_lands = _in_proj_bwd(dproj, w_in, x, g_mix_pre, dx1, _hosted_scatter(late) if distributed else None)

    small = _pack_small(in_acc[0], head_acc[1], head_acc[0], tail_acc[0], dsinks[0, :N_ATTN_HEADS], conv_acc[3],
                        jnp.sum(tail_acc[1]))
    d_conv = jnp.pad(conv_acc[0:CONV_WIDTH].reshape(-1), (0, CONV_FULL_ROWS * D_MODEL - CONV_WIDTH * UP_W))
    small = jnp.concatenate([small, d_conv.reshape(CONV_FULL_ROWS, D_MODEL)], axis=0)
    grads = dict(w_down=early[0], w_up=early[1], w_out=early[2], w_in=late[0])
    lands = dict(zip(["w_down", "w_up", "w_out", "w_in"], early_lands + late_lands))
    return grad_x, grads, lands, small


def kernel(x, mix_pre_norm, w_in, attn_sinks, w_out, mix_post_norm, ffn_pre_norm, w_up, conv_w, conv_b, w_down, ffn_post_norm, loss_target, m_mix_pre_norm, m_w_in, m_attn_sinks, m_w_out, m_mix_post_norm, m_ffn_pre_norm, m_w_up, m_conv_w, m_conv_b, m_w_down, m_ffn_post_norm, v_mix_pre_norm, v_w_in, v_attn_sinks, v_w_out, v_mix_post_norm, v_ffn_pre_norm, v_w_up, v_conv_w, v_conv_b, v_w_down, v_ffn_post_norm):
    cx, cy, cc = _place()
    shard = 2 * cx + cy

    conv_rows = jnp.pad(conv_w[0], ((0, 16 - CONV_WIDTH), (0, 0)))
    w_in_t = jnp.swapaxes(w_in[0], 0, 1)
    (w_in_all, conv_all), sin, cos = _gather_now([w_in_t.astype(BF16), conv_rows], "gather_w_in", x.shape[1])
    conv_full = _shards_to_columns(conv_all[:, 0, :CONV_WIDTH])

    grad_x, grads, lands, small = _local_step(
        x[0], loss_target[0], mix_pre_norm, w_in_all.reshape(IN_W, D_MODEL), attn_sinks.reshape(-1), w_out[0].astype(BF16),
        mix_post_norm, ffn_pre_norm, w_up[0].astype(BF16), conv_full, conv_b, w_down[0].astype(BF16), ffn_post_norm,
        rope=(sin, cos))

    place = jnp.stack([shard, cc]).astype(jnp.int32)
    mats = ["w_in", "w_out", "w_up", "w_down"]
    halves = [_sum_pieces(grads[n], lands[n], place, "sum_grad_" + n) for n in mats]
    weights = dict(w_in=(w_in, m_w_in, v_w_in), w_out=(w_out, m_w_out, v_w_out), w_up=(w_up, m_w_up, v_w_up),
                   w_down=(w_down, m_w_down, v_w_down))
    mat_out = {}
    joined_all, small_all = _join_halves(halves, small)
    for n, joined in zip(mats, joined_all):
        w, m, v = (t[0] for t in weights[n])
        if n == "w_in":
            w, m, v = (jnp.swapaxes(t, 0, 1) for t in (w, m, v))
        res = (joined.reshape(w.shape),) + tuple(_adamw(w, joined.reshape(w.shape), m, v, "adamw_" + n))
        mat_out[n] = tuple(jnp.swapaxes(t, 0, 1) for t in res) if n == "w_in" else res

    small_sum = _sum_devices(small_all)
    d_conv_full = small_sum[SMALL_ROWS:].reshape(-1)[:CONV_WIDTH * UP_W].reshape(CONV_WIDTH, UP_W)
    d_conv_mine = lax.dynamic_slice_in_dim(d_conv_full, shard * (UP_W // N_CHIPS), UP_W // N_CHIPS, axis=1)
    g_s = jnp.concatenate([small_sum[:SMALL_ROWS], _pack_conv(d_conv_mine)], axis=0)
    zero = jnp.zeros((), F32)
    pack_rep = lambda a, b, c_, d, e, f, cw: jnp.concatenate([_pack_small(a, b, c_, d, e, f, zero), _pack_conv(cw[0])], axis=0)
    w_s = pack_rep(mix_pre_norm, mix_post_norm, ffn_pre_norm, ffn_post_norm, attn_sinks, conv_b, conv_w)
    m_s = pack_rep(m_mix_pre_norm, m_mix_post_norm, m_ffn_pre_norm, m_ffn_post_norm, m_attn_sinks, m_conv_b, m_conv_w)
    v_s = pack_rep(v_mix_pre_norm, v_mix_post_norm, v_ffn_pre_norm, v_ffn_post_norm, v_attn_sinks, v_conv_b, v_conv_w)
    delta_s, new_m_s, new_v_s = _adamw(w_s, g_s, m_s, v_s, "adamw_small")

    names = ["mix_pre_norm", "w_in", "attn_sinks", "w_out", "mix_post_norm", "ffn_pre_norm", "w_up", "conv_w", "conv_b",
             "w_down", "ffn_post_norm"]

    def leaves(which, packed_small):
        smalls = _unpack_small(packed_small)
        return [mat_out[n][which][None] if n in mat_out else (smalls[n][None] if n == "conv_w" else smalls[n]) for n in names]

    loss = _unpack_small(g_s)["loss"]
    return (loss, grad_x[None], *leaves(0, g_s), *leaves(1, delta_s), *leaves(2, new_m_s), *leaves(3, new_v_s))
```

```python
import math

import jax
import jax.numpy as jnp
from jax import lax
from jax.experimental import pallas as pl
from jax.experimental.pallas import tpu as pltpu

F32 = jnp.float32
BF16 = jnp.bfloat16

D_MODEL = 1024
HEAD_DIM = 64
ATTN_W = 512
N_ATTN_HEADS = 8
KV_W = 128
RET_W = 512
N_RET_HEADS = 4
RET_HEAD_DIM = 128
CHUNK = 128
IN_W = 2816
D_FF = 2816
UP_W = 2 * D_FF
CONV_WIDTH = 3
RMS_EPS = 1e-6
GN_EPS = 1e-6
MASK_VALUE = -1e30
ATTN_SCALE = HEAD_DIM ** -0.5
RET_K_SCALE = RET_HEAD_DIM ** -0.5
GELU_C = math.sqrt(2.0 / math.pi)
GELU_A = 0.044715

ADAM_LR = 0.001
ADAM_B1 = 0.9
ADAM_B2 = 0.999
ADAM_EPS = 1e-08
ADAM_WD = 0.01
ADAM_STEP = 10

N_CHIPS = 4
N_DEV = 8
MESH = pl.DeviceIdType.MESH
VMEM_LIMIT_V7X = 56 * 1024 * 1024
TOKEN_TILE = 256
BIG_TOKEN_TILE = 512
IN_PROJ_TOKEN_TILE = 1024
WEIGHT_GRAD_TOKENS = 2048
FFN_ROW_BLOCK = 64
HEAD_BWD_COLS = 512
MIXER_CHUNKS_PER_STEP = 4
Q_A0, KV_A0, Q_R0, K_R0, V_R0, G_R0 = 0, 512, 768, 1280, 1792, 2304

ROWS_CONV = 8
SMALL_ROWS = 16
CONV_FULL_ROWS = 24


def _params(sem=None, **kw):
    if sem is not None:
        kw["dimension_semantics"] = sem
    return pltpu.CompilerParams(vmem_limit_bytes=VMEM_LIMIT_V7X, **kw)


def _resident(shape):
    zeros = (0,) * len(shape)
    return pl.BlockSpec(shape, lambda *_: zeros, pipeline_mode=pl.Buffered(1))


class _Hosted:
    def __init__(self, ins, outs, aliases, n_pairs, n_local, start, finish):
        self.ins, self.outs, self.aliases = list(ins), list(outs), dict(aliases)
        self.n_pairs, self.n_local, self.start, self.finish = n_pairs, max(n_local, 1), start, finish


def _hosted_call(compute, *, name, grid, in_specs, out_specs, out_shape, scratch_shapes, args, hosted=None):
    params = _params(("arbitrary",) * len(grid))
    if hosted is None:
        res = pl.pallas_call(compute, name=name, grid=grid, in_specs=in_specs, out_specs=out_specs, out_shape=out_shape,
                             scratch_shapes=scratch_shapes, compiler_params=params)(*args)
        return list(res), []
    n_in, n_out, n_scr = len(in_specs), len(out_specs), len(scratch_shapes)
    h_in, h_out = len(hosted.ins), len(hosted.outs)

    def at(step_of):
        cond = pl.program_id(0) == step_of(grid[0])
        for d in range(1, len(grid)):
            cond = jnp.logical_and(cond, pl.program_id(d) == step_of(grid[d]))
        return cond

    def body(*refs):
        ins, refs = refs[:n_in], refs[n_in:]
        h_ins, refs = refs[:h_in], refs[h_in:]
        outs, refs = refs[:n_out], refs[n_out:]
        h_outs, refs = refs[:h_out], refs[h_out:]
        scr, sems = refs[:n_scr], refs[n_scr:]

        @pl.when(at(lambda n: 0))
        def _():
            hosted.start(h_ins, h_outs, *sems)

        compute(*ins, *outs, *scr)

        @pl.when(at(lambda n: n - 1))
        def _():
            hosted.finish(h_ins, h_outs, *sems)

    hbm = pl.BlockSpec(memory_space=pl.ANY)
    res = pl.pallas_call(
        body, name=name, grid=grid,
        in_specs=list(in_specs) + [hbm] * h_in, out_specs=list(out_specs) + [hbm] * h_out,
        out_shape=list(out_shape) + hosted.outs,
        scratch_shapes=list(scratch_shapes) + [pltpu.SemaphoreType.DMA((hosted.n_pairs,)), pltpu.SemaphoreType.DMA((hosted.n_pairs,)),
                                               pltpu.SemaphoreType.DMA((hosted.n_local,))],
        input_output_aliases={n_in + a: n_out + b for a, b in hosted.aliases.items()},
        compiler_params=params,
    )(*args, *hosted.ins)
    return list(res[:n_out]), list(res[n_out:])


def _dot(a, b):
    return jnp.dot(a, b, preferred_element_type=F32)


def _dot_nt(a, b):
    return lax.dot_general(a, b, (((1,), (1,)), ((), ())), preferred_element_type=F32)


def _dot_tn(a, b):
    return lax.dot_general(a, b, (((0,), (0,)), ((), ())), preferred_element_type=F32)


def _shift_matrix(n, by):
    row = lax.broadcasted_iota(jnp.int32, (n, n), 0)
    col = lax.broadcasted_iota(jnp.int32, (n, n), 1)
    return jnp.where(col == row + by, 1.0, 0.0).astype(BF16)


def _rstd(v):
    return lax.rsqrt(jnp.mean(v * v, axis=-1, keepdims=True) + RMS_EPS)


def _rms_bwd(dy, v, rstd, gain):
    n = v * rstd
    dgain = jnp.sum(dy * n, axis=0, keepdims=True)
    dn = dy * gain
    dv = rstd * (dn - n * jnp.mean(dn * n, axis=-1, keepdims=True))
    return dv, dgain


def _lane_lo(shape):
    return (lax.broadcasted_iota(jnp.int32, shape, 1) % 128) < HEAD_DIM


GROUP = N_ATTN_HEADS // (KV_W // HEAD_DIM)


def _attn_bias(first_chunk):
    qi = lax.broadcasted_iota(jnp.int32, (GROUP * CHUNK, 2 * CHUNK), 0) % CHUNK
    kj = lax.broadcasted_iota(jnp.int32, (GROUP * CHUNK, 2 * CHUNK), 1)
    valid = jnp.logical_and(kj > qi, kj <= qi + CHUNK)
    if first_chunk:
        valid = jnp.logical_and(valid, kj >= CHUNK)
    return jnp.where(valid, 0.0, MASK_VALUE)


def _half(shape, hk):
    lo = _lane_lo(shape)
    return lo if hk == 0 else jnp.logical_not(lo)


class _GroupMasks:
    def __init__(self, sk_ref):
        groups = range(KV_W // HEAD_DIM)
        self.q = [_half((CHUNK, 128), hk) for hk in groups]
        self.kv = [_half((2 * CHUNK, 128), hk) for hk in groups]
        self.sinks = [_group_sinks(sk_ref, hk) for hk in groups]


def _stack_heads(ref, row0, col0, hk, half):
    parts = []
    for j in range(GROUP):
        h = GROUP * hk + j
        pair = ref[row0:row0 + CHUNK, col0 + (h // 2) * 128:col0 + (h // 2 + 1) * 128].astype(F32)
        if h % 2 != hk:
            pair = pltpu.roll(pair, HEAD_DIM, 1)
        parts.append(jnp.where(half, pair, 0.0))
    return jnp.concatenate(parts, axis=0)


def _unstack_heads(stacked, hk):
    pairs = []
    for q in range(GROUP // 2):
        even, odd = stacked[2 * q * CHUNK:(2 * q + 1) * CHUNK], stacked[(2 * q + 1) * CHUNK:(2 * q + 2) * CHUNK]
        pairs.append(even + pltpu.roll(odd, HEAD_DIM, 1) if hk == 0 else pltpu.roll(even, HEAD_DIM, 1) + odd)
    return pairs


def _group_sinks(sk_ref, hk):
    row = lax.broadcasted_iota(jnp.int32, (GROUP * CHUNK, 1), 0)
    col = jnp.full((GROUP * CHUNK, 1), sk_ref[GROUP * hk], F32)
    for j in range(1, GROUP):
        col = jnp.where(row >= j * CHUNK, sk_ref[GROUP * hk + j], col)
    return col


def _attn_probs(q_b, kk_b, bias, sink):
    s = _dot_nt(q_b, kk_b) * ATTN_SCALE + bias
    m = jnp.maximum(jnp.max(s, axis=-1, keepdims=True), sink)
    e = jnp.exp(s - m)
    e_sink = jnp.exp(sink - m)
    inv = 1.0 / (jnp.sum(e, axis=-1, keepdims=True) + e_sink)
    return e * inv, e_sink * inv


def _even_lanes(shape):
    return (lax.broadcasted_iota(jnp.int32, shape, 1) % 2) == 0


def _swap2(v, even):
    return jnp.where(even, pltpu.roll(v, v.shape[1] - 1, 1), pltpu.roll(v, 1, 1))


def _tile4(v):
    return jnp.concatenate([v, v, v, v], axis=-1)


def _sigmoid(v):
    return 1.0 / (1.0 + jnp.exp(-v))


def _ret_constants():
    h = N_RET_HEADS
    log_gamma = jnp.log(1.0 - jnp.power(2.0, -5.0 - jnp.arange(h, dtype=F32)))
    idx = jnp.arange(CHUNK, dtype=F32)
    rel = idx[:, None] - idx[None, :]
    d_intra = jnp.where(rel[None] >= 0, jnp.exp(log_gamma[:, None, None] * jnp.maximum(rel, 0.0)[None]), 0.0)
    xi = jnp.exp(log_gamma[None, :] * (idx[:, None] + 1.0))
    zeta = jnp.exp(log_gamma[None, :] * (CHUNK - 1.0 - idx[:, None]))
    decay = jnp.exp(log_gamma * CHUNK)
    xi_full = jnp.repeat(xi, RET_HEAD_DIM, axis=1)
    zeta_full = jnp.repeat(zeta, RET_HEAD_DIM, axis=1)
    decay_full = jnp.broadcast_to(jnp.repeat(decay, RET_HEAD_DIM)[None, :], (8, RET_W))
    return d_intra.astype(F32), xi_full.astype(F32), zeta_full.astype(F32), decay_full.astype(F32)


def _rope_tables(s):
    pos = jnp.arange(s, dtype=F32)
    angle = 1.0 / jnp.power(10000.0, jnp.linspace(0.0, 1.0, RET_HEAD_DIM // 2, dtype=F32))
    angle = jnp.repeat(angle, 2)
    sign = jnp.where(jnp.arange(RET_HEAD_DIM) % 2 == 0, -1.0, 1.0).astype(F32)
    return jnp.sin(pos[:, None] * angle[None]) * sign[None], jnp.cos(pos[:, None] * angle[None])


def _in_proj(x, gain, w_in_t, hosted=None):
    s = x.shape[0]
    tm = min(IN_PROJ_TOKEN_TILE, s)

    def body(x_ref, g_ref, w_ref, h_ref, p_ref):
        xv = x_ref[...]
        h = (xv * _rstd(xv) * g_ref[...]).astype(BF16)
        h_ref[...] = h
        p_ref[...] = _dot_nt(h, w_ref[...])

    return _hosted_call(
        body, name="in_proj", grid=(s // tm,),
        in_specs=[pl.BlockSpec((tm, D_MODEL), lambda i: (i, 0)), _resident((1, D_MODEL)), _resident((IN_W, D_MODEL))],
        out_specs=[pl.BlockSpec((tm, D_MODEL), lambda i: (i, 0)), pl.BlockSpec((tm, IN_W), lambda i: (i, 0))],
        out_shape=[jax.ShapeDtypeStruct((s, D_MODEL), BF16), jax.ShapeDtypeStruct((s, IN_W), F32)],
        scratch_shapes=[], args=(x, gain, w_in_t), hosted=hosted)


def _mixer_fwd(proj, sinks, sin, cos, consts, hosted=None):
    s = proj.shape[0]
    nc = s // CHUNK
    cps = MIXER_CHUNKS_PER_STEP
    groups = KV_W // HEAD_DIM
    d_intra, xi_full, zeta_full, decay_full = consts

    def body(sk_ref, p_ref, pkv_ref, sin_ref, cos_ref, dm_ref, xi_ref, ze_ref, dc_ref,
             mix_ref, st_ref, pr_ref, ps_ref, ra_ref, on_ref, rs_ref, rq_ref, rk_ref, rz_ref, sq_ref, sg_ref, state):
        i = pl.program_id(0)

        @pl.when(i == 0)
        def _():
            state[...] = jnp.zeros_like(state)

        st = [state[h] for h in range(N_RET_HEADS)]
        bias_any = _attn_bias(False)
        bias_c0 = jnp.where(i == 0, _attn_bias(True), bias_any)
        even = _even_lanes((CHUNK, RET_W))
        masks = _GroupMasks(sk_ref)
        for c in range(cps):
            r0 = c * CHUNK
            rows = slice(r0, r0 + CHUNK)

            kv_cur = p_ref[rows, KV_A0:KV_A0 + 2 * KV_W]
            kv_prev = pkv_ref[...] if c == 0 else p_ref[r0 - CHUNK:r0, KV_A0:KV_A0 + 2 * KV_W]
            kk = jnp.concatenate([kv_prev[:, :KV_W], kv_cur[:, :KV_W]], axis=0)
            vv = jnp.concatenate([kv_prev[:, KV_W:], kv_cur[:, KV_W:]], axis=0)
            kk_b = kk.astype(BF16)
            bias = bias_c0 if c == 0 else bias_any
            for hk in range(KV_W // HEAD_DIM):
                q_b = _stack_heads(p_ref, r0, Q_A0, hk, masks.q[hk]).astype(BF16)
                p, p_sink = _attn_probs(q_b, kk_b, bias, masks.sinks[hk])
                p_b = p.astype(BF16)
                pr_ref[c, hk] = p_b
                ps_ref[c, hk] = p_sink
                sq_ref[c, hk] = q_b
                v_b = jnp.where(masks.kv[hk], vv, 0.0).astype(BF16)
                for q, pair in enumerate(_unstack_heads(_dot(p_b, v_b), hk)):
                    pi = (GROUP // 2) * hk + q
                    mix_ref[rows, pi * 128:(pi + 1) * 128] = pair.astype(BF16)

            sin4, cos4 = _tile4(sin_ref[rows, :]), _tile4(cos_ref[rows, :])
            q_r = p_ref[rows, Q_R0:Q_R0 + RET_W]
            k_r = p_ref[rows, K_R0:K_R0 + RET_W] * RET_K_SCALE
            q_r = q_r * cos4 + _swap2(q_r, even) * sin4
            k_r = k_r * cos4 + _swap2(k_r, even) * sin4
            q_b, k_b, kz_b = q_r.astype(BF16), k_r.astype(BF16), (k_r * ze_ref[...]).astype(BF16)
            rq_ref[rows, :], rk_ref[rows, :], rz_ref[rows, :] = q_b, k_b, kz_b
            for h in range(N_RET_HEADS):
                sl = slice(h * RET_HEAD_DIM, (h + 1) * RET_HEAD_DIM)
                qh, kh = q_b[:, sl], k_b[:, sl]
                vh = p_ref[rows, V_R0 + h * RET_HEAD_DIM:V_R0 + (h + 1) * RET_HEAD_DIM].astype(BF16)
                st_ref[c, h] = st[h]
                a_b = (_dot_nt(qh, kh) * dm_ref[h]).astype(BF16)
                qx = (q_r[:, sl] * xi_ref[:, sl]).astype(BF16)
                o = _dot(jnp.concatenate([a_b, qx], axis=1), jnp.concatenate([vh, st[h].astype(BF16)], axis=0))
                st[h] = dc_ref[0:1, sl] * st[h] + _dot_tn(kz_b[:, sl], vh)
                mu = jnp.mean(o, axis=-1, keepdims=True)
                oc = o - mu
                rs = lax.rsqrt(jnp.mean(oc * oc, axis=-1, keepdims=True) + GN_EPS)
                on = oc * rs
                ra_ref[c, h], on_ref[rows, sl], rs_ref[c, h] = a_b, on, rs
                g = p_ref[rows, G_R0 + h * RET_HEAD_DIM:G_R0 + (h + 1) * RET_HEAD_DIM]
                sg = _sigmoid(g)
                sg_ref[rows, sl] = sg
                mix_ref[rows, ATTN_W + h * RET_HEAD_DIM:ATTN_W + (h + 1) * RET_HEAD_DIM] = (g * sg * on).astype(BF16)
        for h in range(N_RET_HEADS):
            state[h] = st[h]

    return _hosted_call(
        body, name="mixer_fwd", grid=(nc // cps,),
        in_specs=[
            pl.BlockSpec(memory_space=pltpu.SMEM),
            pl.BlockSpec((cps * CHUNK, IN_W), lambda i: (i, 0)),
            pl.BlockSpec((CHUNK, 2 * KV_W), lambda i: (jnp.maximum(cps * i - 1, 0), KV_A0 // (2 * KV_W))),
            pl.BlockSpec((cps * CHUNK, RET_HEAD_DIM), lambda i: (i, 0)),
            pl.BlockSpec((cps * CHUNK, RET_HEAD_DIM), lambda i: (i, 0)),
            _resident((N_RET_HEADS, CHUNK, CHUNK)), _resident((CHUNK, RET_W)), _resident((CHUNK, RET_W)), _resident((8, RET_W)),
        ],
        out_specs=[
            pl.BlockSpec((cps * CHUNK, D_MODEL), lambda i: (i, 0)),
            pl.BlockSpec((cps, N_RET_HEADS, RET_HEAD_DIM, RET_HEAD_DIM), lambda i: (i, 0, 0, 0)),
            pl.BlockSpec((cps, groups, GROUP * CHUNK, 2 * CHUNK), lambda i: (i, 0, 0, 0)),
            pl.BlockSpec((cps, groups, GROUP * CHUNK, 1), lambda i: (i, 0, 0, 0)),
            pl.BlockSpec((cps, N_RET_HEADS, CHUNK, CHUNK), lambda i: (i, 0, 0, 0)),
            pl.BlockSpec((cps * CHUNK, RET_W), lambda i: (i, 0)),
            pl.BlockSpec((cps, N_RET_HEADS, CHUNK, 1), lambda i: (i, 0, 0, 0)),
        ] + [pl.BlockSpec((cps * CHUNK, RET_W), lambda i: (i, 0))] * 3 + [
            pl.BlockSpec((cps, groups, GROUP * CHUNK, 128), lambda i: (i, 0, 0, 0)),
            pl.BlockSpec((cps * CHUNK, RET_W), lambda i: (i, 0))],
        out_shape=[jax.ShapeDtypeStruct((s, D_MODEL), BF16),
                   jax.ShapeDtypeStruct((nc, N_RET_HEADS, RET_HEAD_DIM, RET_HEAD_DIM), F32),
                   jax.ShapeDtypeStruct((nc, groups, GROUP * CHUNK, 2 * CHUNK), BF16),
                   jax.ShapeDtypeStruct((nc, groups, GROUP * CHUNK, 1), F32),
                   jax.ShapeDtypeStruct((nc, N_RET_HEADS, CHUNK, CHUNK), BF16),
                   jax.ShapeDtypeStruct((s, RET_W), F32),
                   jax.ShapeDtypeStruct((nc, N_RET_HEADS, CHUNK, 1), F32)] + [jax.ShapeDtypeStruct((s, RET_W), BF16)] * 3 + [
                   jax.ShapeDtypeStruct((nc, groups, GROUP * CHUNK, 128), BF16), jax.ShapeDtypeStruct((s, RET_W), F32)],
        scratch_shapes=[pltpu.VMEM((N_RET_HEADS, RET_HEAD_DIM, RET_HEAD_DIM), F32)],
        args=(sinks, proj, proj, sin, cos, d_intra, xi_full, zeta_full, decay_full), hosted=hosted)


def _out_up_proj(mix, x, w_out, g_post, g_pre, w_up):
    s = x.shape[0]
    tm = min(BIG_TOKEN_TILE, s)
    blk = UP_W // N_CHIPS

    def body(mix_ref, x_ref, wo_ref, g2_ref, g3_ref, wu_ref, mixed_ref, x1_ref, h2_ref, u0_ref):
        mixed = _dot(mix_ref[...], wo_ref[...])
        mixed_ref[...] = mixed
        x1 = x_ref[...] + mixed * _rstd(mixed) * g2_ref[...]
        x1_ref[...] = x1
        h2 = (x1 * _rstd(x1) * g3_ref[...]).astype(BF16)
        h2_ref[...] = h2
        for k in range(N_CHIPS):
            u0_ref[:, k * blk:(k + 1) * blk] = _dot(h2, wu_ref[k]).astype(BF16)

    tok = lambda w: pl.BlockSpec((tm, w), lambda i: (i, 0))
    return pl.pallas_call(
        body, name="out_up_proj", grid=(s // tm,),
        in_specs=[tok(D_MODEL), tok(D_MODEL), _resident((D_MODEL, D_MODEL)), _resident((1, D_MODEL)), _resident((1, D_MODEL)),
                  _resident((N_CHIPS, D_MODEL, blk))],
        out_specs=[tok(D_MODEL), tok(D_MODEL), tok(D_MODEL), tok(UP_W)],
        out_shape=[jax.ShapeDtypeStruct((s, D_MODEL), F32), jax.ShapeDtypeStruct((s, D_MODEL), F32),
                   jax.ShapeDtypeStruct((s, D_MODEL), BF16), jax.ShapeDtypeStruct((s, UP_W), BF16)],
        compiler_params=_params(("arbitrary",)),
    )(mix, x, w_out, g_post, g_pre, w_up)


def _ffn_tail(u0, x1, target, conv_w, conv_b, w_down, g_post):
    s = x1.shape[0]
    tm = TOKEN_TILE
    last = s // tm - 1
    rb, lanes = FFN_ROW_BLOCK, 128

    def body(u0_ref, x1_ref, t_ref, cw_ref, cb_ref, wd_ref, g_ref,
             y_ref, dy2_ref, dout_ref, du_ref, cacc_ref, gacc_ref, u1_s, u2_s, carry, gelu_s, slope_s, dy_s, cacc):
        i = pl.program_id(0)

        @pl.when(i == 0)
        def _():
            carry[...] = jnp.zeros_like(carry)
            cacc[...] = jnp.zeros_like(cacc)
            gacc_ref[...] = jnp.zeros_like(gacc_ref)

        shift1, shift2 = _shift_matrix(tm, -1), _shift_matrix(tm, -2)
        r8 = lax.broadcasted_iota(jnp.int32, (8, 1), 0)
        wide = 2 * lanes

        def shift_block(col):
            cols = slice(col, col + wide)
            u1_s[:, cols] = _dot(shift1, u0_ref[:, cols])
            u2_s[:, cols] = _dot(shift2, u0_ref[:, cols])
            c14, c15 = carry[14:15, cols], carry[15:16, cols]
            u1_s[0:8, cols] = jnp.where(r8 == 0, c15, u1_s[0:8, cols])
            u2_s[0:8, cols] = jnp.where(r8 == 0, c14, jnp.where(r8 == 1, c15, u2_s[0:8, cols]))

        def taps(col):
            return (cw_ref[0:1, col:col + lanes], cw_ref[1:2, col:col + lanes], cw_ref[2:3, col:col + lanes],
                    cb_ref[0:1, col:col + lanes])

        def shifted(r0, col):
            return (u2_s[r0:r0 + rb, col:col + lanes], u1_s[r0:r0 + rb, col:col + lanes],
                    u0_ref[r0:r0 + rb, col:col + lanes].astype(F32))

        def conv(r0, col, w):
            u2, u1, uc = shifted(r0, col)
            return w[0] * u2 + w[1] * u1 + w[2] * uc + w[3]

        fold = lambda v: jnp.sum(v.reshape(rb // 8, 8, lanes), axis=0)

        shift_block(0)
        shift_block(D_FF)
        for j in range(D_FF // lanes):
            cg, cv = j * lanes, D_FF + j * lanes
            if cg % wide == 0 and cg + wide < D_FF:
                shift_block(cg + wide)
                shift_block(cv + wide)
            wg, wv = taps(cg), taps(cv)
            for r0 in range(0, tm, rb):
                gate, val = conv(r0, cg, wg), conv(r0, cv, wv)
                g2 = gate * gate
                th = jnp.tanh(gate * (GELU_C + GELU_C * GELU_A * g2))
                hp = 0.5 * th + 0.5
                gelu = gate * hp
                dgelu = hp + gate * (1.0 - th * th) * (0.5 * GELU_C + 1.5 * GELU_C * GELU_A * g2)
                y_ref[r0:r0 + rb, cg:cg + lanes] = (gelu * val).astype(BF16)
                gelu_s[r0:r0 + rb, cg:cg + lanes] = gelu
                slope_s[r0:r0 + rb, cg:cg + lanes] = dgelu * val

        y2 = _dot(y_ref[...], wd_ref[...])
        r4 = _rstd(y2)
        gain = g_ref[...]
        out = x1_ref[...] + y2 * r4 * gain
        diff = out - t_ref[...]
        dout = diff * (1.0 / D_MODEL)
        dout_ref[...] = dout
        dy2, dgain = _rms_bwd(dout, y2, r4, gain)
        dy2_b = dy2.astype(BF16)
        dy2_ref[...] = dy2_b
        gacc_ref[0:1, :] += dgain
        gacc_ref[1:2, :] += 0.5 * jnp.sum(diff * dout, axis=0, keepdims=True)
        carry[...] = u0_ref[tm - 16:tm, :].astype(F32)

        dy_s[:, 0:wide] = _dot_nt(dy2_b, wd_ref[0:wide, :])
        for j in range(D_FF // lanes):
            cg, cv = j * lanes, D_FF + j * lanes
            if cg % wide == 0 and cg + wide < D_FF:
                dy_s[:, cg + wide:cg + 2 * wide] = _dot_nt(dy2_b, wd_ref[cg + wide:cg + 2 * wide, :])
            acc = [[jnp.zeros((8, lanes), F32) for _ in range(CONV_WIDTH + 1)] for _ in range(2)]
            for r0 in range(0, tm, rb):
                dy = dy_s[r0:r0 + rb, cg:cg + lanes]
                d_gate = dy * slope_s[r0:r0 + rb, cg:cg + lanes]
                d_val = dy * gelu_s[r0:r0 + rb, cg:cg + lanes]
                for side, (col, d) in enumerate(((cg, d_gate), (cv, d_val))):
                    du_ref[r0:r0 + rb, col:col + lanes] = d.astype(BF16)
                    for k, u in enumerate(shifted(r0, col)):
                        acc[side][k] = acc[side][k] + fold(d * u)
                    acc[side][CONV_WIDTH] = acc[side][CONV_WIDTH] + fold(d)
            for side, col in enumerate((cg, cv)):
                for k in range(CONV_WIDTH + 1):
                    cacc[8 * k:8 * k + 8, col:col + lanes] += acc[side][k]

        @pl.when(i == last)
        def _():
            for k in range(CONV_WIDTH + 1):
                cacc_ref[k:k + 1, :] = jnp.sum(cacc[8 * k:8 * k + 8, :], axis=0, keepdims=True)
            cacc_ref[CONV_WIDTH + 1:8, :] = jnp.zeros((8 - CONV_WIDTH - 1, UP_W), F32)

    tok = lambda w: pl.BlockSpec((tm, w), lambda i: (i, 0))
    return pl.pallas_call(
        body, name="ffn_tail", grid=(s // tm,),
        in_specs=[tok(UP_W), tok(D_MODEL), tok(D_MODEL), _resident((CONV_WIDTH, UP_W)), _resident((1, UP_W)),
                  _resident((D_FF, D_MODEL)), _resident((1, D_MODEL))],
        out_specs=[tok(D_FF), tok(D_MODEL), tok(D_MODEL), tok(UP_W),
                   pl.BlockSpec((8, UP_W), lambda i: (0, 0)), pl.BlockSpec((8, D_MODEL), lambda i: (0, 0))],
        out_shape=[jax.ShapeDtypeStruct((s, D_FF), BF16), jax.ShapeDtypeStruct((s, D_MODEL), BF16),
                   jax.ShapeDtypeStruct((s, D_MODEL), F32), jax.ShapeDtypeStruct((s, UP_W), BF16),
                   jax.ShapeDtypeStruct((8, UP_W), F32), jax.ShapeDtypeStruct((8, D_MODEL), F32)],
        scratch_shapes=[pltpu.VMEM((tm, UP_W), F32), pltpu.VMEM((tm, UP_W), F32), pltpu.VMEM((16, UP_W), F32),
                        pltpu.VMEM((tm, D_FF), F32), pltpu.VMEM((tm, D_FF), F32),
                        pltpu.VMEM((tm, D_FF), F32), pltpu.VMEM((8 * (CONV_WIDTH + 1), UP_W), F32)],
        compiler_params=_params(("arbitrary",)),
    )(u0, x1, target, conv_w, conv_b, w_down, g_post)


def _ffn_head_bwd(du, conv_w, w_up, x1, g_pre, dout, mixed, g_post, w_out, hosted=None):
    s = x1.shape[0]
    tm = TOKEN_TILE
    nt = s // tm
    blk = UP_W // N_CHIPS

    def body(du_ref, halo_ref, cw_ref, wu_ref, x1_ref, g3_ref, dout_ref, mixed_ref, g2_ref, wo_ref,
             du0_ref, dx1_ref, dmixed_ref, dmix_ref, gacc_ref, dbuf):
        i = pl.program_id(0)

        @pl.when(i == 0)
        def _():
            gacc_ref[...] = jnp.zeros_like(gacc_ref)

        dbuf[0:tm, :] = du_ref[...].astype(F32)
        dbuf[tm:tm + 16, :] = jnp.where(i < nt - 1, halo_ref[...].astype(F32), 0.0)
        dh2 = jnp.zeros((tm, D_MODEL), F32)
        for k in range(N_CHIPS):
            for c0 in range(0, blk, HEAD_BWD_COLS):
                width = min(HEAD_BWD_COLS, blk - c0)
                cols = slice(k * blk + c0, k * blk + c0 + width)
                du0_b = (cw_ref[2:3, cols] * dbuf[0:tm, cols] + cw_ref[1:2, cols] * dbuf[1:1 + tm, cols]
                         + cw_ref[0:1, cols] * dbuf[2:2 + tm, cols]).astype(BF16)
                du0_ref[:, cols] = du0_b
                dh2 = dh2 + _dot_nt(du0_b, wu_ref[k, :, c0:c0 + width])
        x1 = x1_ref[...]
        d3, dg3 = _rms_bwd(dh2, x1, _rstd(x1), g3_ref[...])
        dx1 = dout_ref[...] + d3
        dx1_ref[...] = dx1
        mixed = mixed_ref[...]
        dmixed, dg2 = _rms_bwd(dx1, mixed, _rstd(mixed), g2_ref[...])
        dmixed_b = dmixed.astype(BF16)
        dmixed_ref[...] = dmixed_b
        dmix_ref[...] = _dot_nt(dmixed_b, wo_ref[...]).astype(BF16)
        gacc_ref[0:1, :] += dg3
        gacc_ref[1:2, :] += dg2

    tok = lambda w: pl.BlockSpec((tm, w), lambda i: (i, 0))
    halo = pl.BlockSpec((16, UP_W), lambda i: (jnp.minimum(i + 1, nt - 1) * (tm // 16), 0))
    return _hosted_call(
        body, name="ffn_head_bwd", grid=(nt,),
        in_specs=[tok(UP_W), halo, _resident((CONV_WIDTH, UP_W)), _resident((N_CHIPS, D_MODEL, blk)), tok(D_MODEL),
                  _resident((1, D_MODEL)), tok(D_MODEL), tok(D_MODEL), _resident((1, D_MODEL)), _resident((D_MODEL, D_MODEL))],
        out_specs=[tok(UP_W), tok(D_MODEL), tok(D_MODEL), tok(D_MODEL), pl.BlockSpec((8, D_MODEL), lambda i: (0, 0))],
        out_shape=[jax.ShapeDtypeStruct((s, UP_W), BF16), jax.ShapeDtypeStruct((s, D_MODEL), F32),
                   jax.ShapeDtypeStruct((s, D_MODEL), BF16), jax.ShapeDtypeStruct((s, D_MODEL), BF16),
                   jax.ShapeDtypeStruct((8, D_MODEL), F32)],
        scratch_shapes=[pltpu.VMEM((tm + 16, UP_W), F32)],
        args=(du, du, conv_w, w_up, x1, g_pre, dout, mixed, g_post, w_out), hosted=hosted)


def _mixer_bwd(proj, dmix, states, kept, sin, cos, consts, hosted=None):
    probs, p_sinks, ret_scores, ret_normed, ret_rstd, ret_q, ret_k, ret_kz, stacked_q, gate_sig = kept
    s = proj.shape[0]
    nc = s // CHUNK
    cps = MIXER_CHUNKS_PER_STEP
    nb = nc // cps
    groups = KV_W // HEAD_DIM
    d_intra, xi_full, zeta_full, decay_full = consts

    def body(p_ref, pkv_ref, dmix_ref, st_ref, pr_ref, ps_ref, ra_ref, on_ref, rs_ref, rq_ref, rk_ref, rz_ref, sq_ref, sg_ref,
             sin_ref, cos_ref, dm_ref, xi_ref, ze_ref, dc_ref, dp_ref, dsk_ref, gstate, ckv, dsk_acc):
        i = pl.program_id(0)
        block = nb - 1 - i

        @pl.when(i == 0)
        def _():
            gstate[...] = jnp.zeros_like(gstate)
            ckv[...] = jnp.zeros_like(ckv)
            dsk_acc[...] = jnp.zeros_like(dsk_acc)

        gs_all = [gstate[h] for h in range(N_RET_HEADS)]
        later_kv = ckv[...]
        lane = lax.broadcasted_iota(jnp.int32, (CHUNK, 128), 1)
        dsk = jnp.zeros((CHUNK, 128), F32)
        even = _even_lanes((CHUNK, RET_W))
        half_q = [_half((CHUNK, 128), hk) for hk in range(groups)]
        half_kv = [_half((2 * CHUNK, 128), hk) for hk in range(groups)]
        for c in reversed(range(cps)):
            r0 = c * CHUNK
            rows = slice(r0, r0 + CHUNK)

            kv_cur = p_ref[rows, KV_A0:KV_A0 + 2 * KV_W]
            kv_prev = pkv_ref[...] if c == 0 else p_ref[r0 - CHUNK:r0, KV_A0:KV_A0 + 2 * KV_W]
            kk = jnp.concatenate([kv_prev[:, :KV_W], kv_cur[:, :KV_W]], axis=0)
            vv = jnp.concatenate([kv_prev[:, KV_W:], kv_cur[:, KV_W:]], axis=0)
            vv_b = vv.astype(BF16)
            dkk = jnp.zeros((2 * CHUNK, KV_W), F32)
            dvv = jnp.zeros((2 * CHUNK, KV_W), F32)
            for hk in range(groups):
                q_b = sq_ref[c, hk]
                do_b = _stack_heads(dmix_ref, r0, 0, hk, half_q[hk]).astype(BF16)
                p_b = pr_ref[c, hk]
                p = p_b.astype(F32)
                dpr = _dot_nt(do_b, vv_b)
                delta = jnp.sum(p * dpr, axis=-1, keepdims=True)
                ds_b = (p * (dpr - delta) * ATTN_SCALE).astype(BF16)
                dsink = -ps_ref[c, hk] * delta
                for j in range(GROUP):
                    dsk = dsk + jnp.where(lane == GROUP * hk + j, dsink[j * CHUNK:(j + 1) * CHUNK], 0.0)
                k_b = jnp.where(half_kv[hk], kk, 0.0).astype(BF16)
                for q, pair in enumerate(_unstack_heads(_dot(ds_b, k_b), hk)):
                    pi = (GROUP // 2) * hk + q
                    dp_ref[rows, Q_A0 + pi * 128:Q_A0 + (pi + 1) * 128] = pair.astype(BF16)
                dkk = dkk + _dot_tn(ds_b, q_b)
                dvv = dvv + _dot_tn(p_b, do_b)
            dp_ref[rows, KV_A0:KV_A0 + KV_W] = (dkk[CHUNK:] + later_kv[:, :KV_W]).astype(BF16)
            dp_ref[rows, KV_A0 + KV_W:KV_A0 + 2 * KV_W] = (dvv[CHUNK:] + later_kv[:, KV_W:]).astype(BF16)
            later_kv = jnp.concatenate([dkk[:CHUNK], dvv[:CHUNK]], axis=1)

            sin4, cos4 = _tile4(sin_ref[rows, :]), _tile4(cos_ref[rows, :])
            dq_parts, dk_parts = [], []
            for h in range(N_RET_HEADS):
                sl = slice(h * RET_HEAD_DIM, (h + 1) * RET_HEAD_DIM)
                qh, kh = rq_ref[rows, sl], rk_ref[rows, sl]
                vh = p_ref[rows, V_R0 + h * RET_HEAD_DIM:V_R0 + (h + 1) * RET_HEAD_DIM].astype(BF16)
                st_b = st_ref[c, h].astype(BF16)
                gs = gs_all[h]
                gs_b = gs.astype(BF16)
                xi_h = xi_ref[:, sl]
                dm = dm_ref[h]
                a_b, on, rs = ra_ref[c, h], on_ref[rows, sl], rs_ref[c, h]
                g = p_ref[rows, G_R0 + h * RET_HEAD_DIM:G_R0 + (h + 1) * RET_HEAD_DIM]
                sg = sg_ref[rows, sl]
                dr = dmix_ref[rows, ATTN_W + h * RET_HEAD_DIM:ATTN_W + (h + 1) * RET_HEAD_DIM].astype(F32)
                dp_ref[rows, G_R0 + h * RET_HEAD_DIM:G_R0 + (h + 1) * RET_HEAD_DIM] = (
                    dr * on * (sg * (1.0 + g * (1.0 - sg)))).astype(BF16)
                don = dr * g * sg
                do = rs * (don - jnp.mean(don, axis=-1, keepdims=True) - on * jnp.mean(don * on, axis=-1, keepdims=True))
                do_b = do.astype(BF16)
                dox_b = (do * xi_h).astype(BF16)
                da_b = (_dot_nt(do_b, vh) * dm).astype(BF16)
                dq_parts.append(_dot(da_b, kh) + _dot_nt(dox_b, st_b))
                dk_parts.append(_dot_tn(da_b, qh) + ze_ref[:, sl] * _dot_nt(vh, gs_b))
                dv = _dot_tn(a_b, do_b) + _dot(rz_ref[rows, sl], gs_b)
                dp_ref[rows, V_R0 + h * RET_HEAD_DIM:V_R0 + (h + 1) * RET_HEAD_DIM] = dv.astype(BF16)
                gs_all[h] = dc_ref[0:1, sl] * gs + _dot_tn(qh, dox_b)
            dq = jnp.concatenate(dq_parts, axis=-1)
            dk = jnp.concatenate(dk_parts, axis=-1)
            dp_ref[rows, Q_R0:Q_R0 + RET_W] = (dq * cos4 - _swap2(dq, even) * sin4).astype(BF16)
            dp_ref[rows, K_R0:K_R0 + RET_W] = (RET_K_SCALE * (dk * cos4 - _swap2(dk, even) * sin4)).astype(BF16)

        for h in range(N_RET_HEADS):
            gstate[h] = gs_all[h]
        ckv[...] = later_kv
        dsk_acc[...] += dsk

        @pl.when(i == nb - 1)
        def _():
            dsk_ref[...] = jnp.sum(dsk_acc[...], axis=0, keepdims=True)

    rev = lambda i: nb - 1 - i
    return _hosted_call(
        body, name="mixer_bwd", grid=(nb,),
        in_specs=[
            pl.BlockSpec((cps * CHUNK, IN_W), lambda i: (rev(i), 0)),
            pl.BlockSpec((CHUNK, 2 * KV_W), lambda i: (jnp.maximum(cps * rev(i) - 1, 0), KV_A0 // (2 * KV_W))),
            pl.BlockSpec((cps * CHUNK, D_MODEL), lambda i: (rev(i), 0)),
            pl.BlockSpec((cps, N_RET_HEADS, RET_HEAD_DIM, RET_HEAD_DIM), lambda i: (rev(i), 0, 0, 0)),
            pl.BlockSpec((cps, groups, GROUP * CHUNK, 2 * CHUNK), lambda i: (rev(i), 0, 0, 0)),
            pl.BlockSpec((cps, groups, GROUP * CHUNK, 1), lambda i: (rev(i), 0, 0, 0)),
            pl.BlockSpec((cps, N_RET_HEADS, CHUNK, CHUNK), lambda i: (rev(i), 0, 0, 0)),
            pl.BlockSpec((cps * CHUNK, RET_W), lambda i: (rev(i), 0)),
            pl.BlockSpec((cps, N_RET_HEADS, CHUNK, 1), lambda i: (rev(i), 0, 0, 0)),
            pl.BlockSpec((cps * CHUNK, RET_W), lambda i: (rev(i), 0)), pl.BlockSpec((cps * CHUNK, RET_W), lambda i: (rev(i), 0)),
            pl.BlockSpec((cps * CHUNK, RET_W), lambda i: (rev(i), 0)),
            pl.BlockSpec((cps, groups, GROUP * CHUNK, 128), lambda i: (rev(i), 0, 0, 0)),
            pl.BlockSpec((cps * CHUNK, RET_W), lambda i: (rev(i), 0)),
            pl.BlockSpec((cps * CHUNK, RET_HEAD_DIM), lambda i: (rev(i), 0)),
            pl.BlockSpec((cps * CHUNK, RET_HEAD_DIM), lambda i: (rev(i), 0)),
            _resident((N_RET_HEADS, CHUNK, CHUNK)), _resident((CHUNK, RET_W)), _resident((CHUNK, RET_W)), _resident((8, RET_W)),
        ],
        out_specs=[pl.BlockSpec((cps * CHUNK, IN_W), lambda i: (rev(i), 0)), pl.BlockSpec((1, 128), lambda i: (0, 0))],
        out_shape=[jax.ShapeDtypeStruct((s, IN_W), BF16), jax.ShapeDtypeStruct((1, 128), F32)],
        scratch_shapes=[pltpu.VMEM((N_RET_HEADS, RET_HEAD_DIM, RET_HEAD_DIM), F32), pltpu.VMEM((CHUNK, 2 * KV_W), F32),
                        pltpu.VMEM((CHUNK, 128), F32)],
        args=(proj, proj, dmix, states, probs, p_sinks, ret_scores, ret_normed, ret_rstd, ret_q, ret_k, ret_kz, stacked_q, gate_sig,
              sin, cos, d_intra, xi_full, zeta_full, decay_full), hosted=hosted)


def _in_proj_bwd(dproj, w_in_t, x, gain, dx1, hosted=None):
    s = x.shape[0]
    tm = min(BIG_TOKEN_TILE, s)

    def body(dp_ref, w_ref, x_ref, g_ref, dx1_ref, dx_ref, gacc_ref):
        @pl.when(pl.program_id(0) == 0)
        def _():
            gacc_ref[...] = jnp.zeros_like(gacc_ref)

        dh = _dot(dp_ref[...], w_ref[...])
        xv = x_ref[...]
        d1, dg = _rms_bwd(dh, xv, _rstd(xv), g_ref[...])
        dx_ref[...] = dx1_ref[...] + d1
        gacc_ref[0:1, :] += dg

    tok = lambda w: pl.BlockSpec((tm, w), lambda i: (i, 0))
    return _hosted_call(
        body, name="in_proj_bwd", grid=(s // tm,),
        in_specs=[tok(IN_W), _resident((IN_W, D_MODEL)), tok(D_MODEL), _resident((1, D_MODEL)), tok(D_MODEL)],
        out_specs=[tok(D_MODEL), pl.BlockSpec((8, D_MODEL), lambda i: (0, 0))],
        out_shape=[jax.ShapeDtypeStruct((s, D_MODEL), F32), jax.ShapeDtypeStruct((8, D_MODEL), F32)],
        scratch_shapes=[], args=(dproj, w_in_t, x, gain, dx1), hosted=hosted)


def _weight_grad(a, b, tn, name, by_block=False, hosted=None):
    s, m = a.shape
    n = b.shape[1]
    tk = min(WEIGHT_GRAD_TOKENS if m <= D_MODEL else WEIGHT_GRAD_TOKENS // 2, s)

    def body(a_ref, b_ref, o_ref):
        @pl.when(pl.program_id(1) == 0)
        def _():
            o_ref[...] = jnp.zeros_like(o_ref)

        o_ref[...] += _dot_tn(a_ref[...], b_ref[...])

    if by_block:
        out_spec = pl.BlockSpec((None, m, tn), lambda j, k: (j, 0, 0))
        out_shape = jax.ShapeDtypeStruct((n // tn, m, tn), F32)
    else:
        out_spec = pl.BlockSpec((m, tn), lambda j, k: (0, j))
        out_shape = jax.ShapeDtypeStruct((m, n), F32)
    (out,), lands = _hosted_call(
        body, name=name, grid=(n // tn, s // tk),
        in_specs=[pl.BlockSpec((tk, m), lambda j, k: (k, 0)), pl.BlockSpec((tk, tn), lambda j, k: (k, j))],
        out_specs=[out_spec], out_shape=[out_shape], scratch_shapes=[], args=(a, b), hosted=hosted)
    return out if hosted is None else (out, lands)


def _place():
    return lax.axis_index("x"), lax.axis_index("y"), lax.axis_index("c")


def _remote(src, dst, send_sems, recv_sems, k, to):
    return pltpu.make_async_remote_copy(src_ref=src, dst_ref=dst, send_sem=send_sems.at[k], recv_sem=recv_sems.at[k],
                                        device_id=to, device_id_type=MESH)


def _gather_level1_copies(w_refs, out_refs, send_sems, recv_sems, local_sems):
    x, y, c = _place()
    mine_at = 2 * x + y
    peers = [(x, y, 1 - c), (1 - x, y, c), (x, 1 - y, c), (1 - x, 1 - y, c)]
    local, sends, recvs = [], [], []
    for i, (w, out) in enumerate(zip(w_refs, out_refs)):
        half = w.shape[0] // 2
        src = w.at[pl.ds(pl.multiple_of(c * half, 16 if half % 16 == 0 else 8), half), :]
        mine = out.at[mine_at, c]
        local.append(pltpu.make_async_copy(src, mine, local_sems.at[i]))
        for k, p in enumerate(peers):
            sends.append(_remote(src, mine, send_sems, recv_sems, 4 * i + k, p))
            lands = out.at[mine_at, 1 - c] if k == 0 else out.at[2 * p[0] + p[1], c]
            recvs.append(_remote(src, lands, send_sems, recv_sems, 4 * i + k, p))
    return local, sends, recvs


def _gather_level1_start(w_refs, out_refs, send_sems, recv_sems, local_sems):
    local, sends, _ = _gather_level1_copies(w_refs, out_refs, send_sems, recv_sems, local_sems)
    for cp in local + sends:
        cp.start()


def _gather_level1_finish(w_refs, out_refs, send_sems, recv_sems, local_sems):
    local, sends, recvs = _gather_level1_copies(w_refs, out_refs, send_sems, recv_sems, local_sems)
    for cp in recvs:
        cp.wait_recv()
    for cp in sends:
        cp.wait_send()
    for cp in local:
        cp.wait()


def _gather_level2_copies(in_refs, out_refs, send_sems, recv_sems, local_sems):
    x, y, c = _place()
    chips = [(1 - x, y), (x, 1 - y), (1 - x, 1 - y)]
    sends, recvs = [], []
    for i, (src, out) in enumerate(zip(in_refs, out_refs)):
        for j, (px, py) in enumerate(chips):
            sends.append(_remote(src.at[2 * px + py, c], out.at[2 * px + py, c], send_sems, recv_sems, 3 * i + j, (x, y, 1 - c)))
            recvs.append(_remote(src.at[2 * px + py, c], out.at[2 * px + py, 1 - c], send_sems, recv_sems, 3 * i + j,
                                 (x, y, 1 - c)))
    return sends, recvs


def _gather_level2_start(in_refs, out_refs, send_sems, recv_sems, local_sems):
    for cp in _gather_level2_copies(in_refs, out_refs, send_sems, recv_sems, local_sems)[0]:
        cp.start()


def _gather_level2_finish(in_refs, out_refs, send_sems, recv_sems, local_sems):
    sends, recvs = _gather_level2_copies(in_refs, out_refs, send_sems, recv_sems, local_sems)
    for cp in recvs:
        cp.wait_recv()
    for cp in sends:
        cp.wait_send()


def _gathered_shape(w):
    r, cols = w.shape
    return jax.ShapeDtypeStruct((N_CHIPS, 2, r // 2, cols), w.dtype)


def _hosted_gather_level1(shards):
    n = len(shards)
    return _Hosted(shards, [_gathered_shape(w) for w in shards], {}, 4 * n, n, _gather_level1_start, _gather_level1_finish)


def _hosted_gather_level2(gathered):
    n = len(gathered)
    return _Hosted(gathered, [jax.ShapeDtypeStruct(g.shape, g.dtype) for g in gathered], {i: i for i in range(n)}, 3 * n, 0,
                   _gather_level2_start, _gather_level2_finish)


def _gather_now(shards, name, seq_len):
    n = len(shards)
    rows = min(512, seq_len)
    angle = 1.0 / jnp.power(10000.0, jnp.linspace(0.0, 1.0, RET_HEAD_DIM // 2, dtype=F32))
    sign = jnp.where(jnp.arange(RET_HEAD_DIM) % 2 == 0, -1.0, 1.0).astype(F32)
    angle_sign = jnp.concatenate([jnp.repeat(angle, 2)[None], sign[None], jnp.zeros((6, RET_HEAD_DIM), F32)], axis=0)

    def body(*refs):
        w_refs, as_ref, out_refs = list(refs[:n]), refs[n], list(refs[n + 1:2 * n + 1])
        sin_ref, cos_ref, send1, recv1, local1, send2, recv2 = refs[2 * n + 1:]
        _gather_level1_start(w_refs, out_refs, send1, recv1, local1)

        def fill(i, carry):
            r0 = pl.multiple_of(i * rows, rows)
            pos = (lax.broadcasted_iota(jnp.int32, (rows, RET_HEAD_DIM), 0) + i * rows).astype(F32)
            arg = pos * as_ref[0:1, :]
            sin_ref[pl.ds(r0, rows), :] = jnp.sin(arg) * as_ref[1:2, :]
            cos_ref[pl.ds(r0, rows), :] = jnp.cos(arg)
            return carry

        lax.fori_loop(0, seq_len // rows, fill, 0)
        _gather_level1_finish(w_refs, out_refs, send1, recv1, local1)
        _gather_level2_start(out_refs, out_refs, send2, recv2, None)
        _gather_level2_finish(out_refs, out_refs, send2, recv2, None)

    hbm, vmem = pl.BlockSpec(memory_space=pl.ANY), pl.BlockSpec(memory_space=pltpu.VMEM)
    table = jax.ShapeDtypeStruct((seq_len, RET_HEAD_DIM), F32)
    res = pl.pallas_call(
        body, name=name, out_shape=[_gathered_shape(w) for w in shards] + [table, table],
        in_specs=[hbm] * n + [vmem], out_specs=[hbm] * n + [vmem, vmem],
        scratch_shapes=[pltpu.SemaphoreType.DMA((4 * n,)), pltpu.SemaphoreType.DMA((4 * n,)), pltpu.SemaphoreType.DMA((n,)),
                        pltpu.SemaphoreType.DMA((3 * n,)), pltpu.SemaphoreType.DMA((3 * n,))],
        compiler_params=_params(),
    )(*shards, angle_sign)
    return res[:n], res[n], res[n + 1]


def _scatter_copies(g_refs, land_refs, send_sems, recv_sems, local_sems):
    x, y, c = _place()
    copies = []
    for i, (g, land) in enumerate(zip(g_refs, land_refs)):
        for k, (px, py, pc) in enumerate(_relations(x, y, c)):
            copies.append(_remote(g.at[2 * px + py, pc], land.at[k], send_sems, recv_sems, 7 * i + k, (px, py, pc)))
    return copies


def _scatter_start(g_refs, land_refs, send_sems, recv_sems, local_sems):
    for cp in _scatter_copies(g_refs, land_refs, send_sems, recv_sems, local_sems):
        cp.start()


def _scatter_finish(g_refs, land_refs, send_sems, recv_sems, local_sems):
    for cp in _scatter_copies(g_refs, land_refs, send_sems, recv_sems, local_sems):
        cp.wait()


def _hosted_scatter(grads):
    lands = [jax.ShapeDtypeStruct((N_DEV - 1,) + g.shape[2:], g.dtype) for g in grads]
    return _Hosted(grads, lands, {}, 7 * len(grads), 0, _scatter_start, _scatter_finish)


def _relations(x, y, c):
    rel = []
    for fx in (0, 1):
        for fy in (0, 1):
            for fc in (0, 1):
                if fx or fy or fc:
                    rel.append(((1 - x) if fx else x, (1 - y) if fy else y, (1 - c) if fc else c))
    return rel


def _join_halves(shards, small):
    n = len(shards)

    def body(*refs):
        in_refs, small_ref, out_refs, all_ref = refs[:n], refs[n], refs[n + 1:2 * n + 1], refs[2 * n + 1]
        send_sems, recv_sems = refs[2 * n + 2:]
        x, y, c = _place()
        slot = lambda p: all_ref.at[4 * p[0] + 2 * p[1] + p[2]]
        all_ref[4 * x + 2 * y + c] = small_ref[...]
        sends = [_remote(src.at[c], out.at[c], send_sems, recv_sems, i, (x, y, 1 - c))
                 for i, (src, out) in enumerate(zip(in_refs, out_refs))]
        recvs = [_remote(src.at[c], out.at[1 - c], send_sems, recv_sems, i, (x, y, 1 - c))
                 for i, (src, out) in enumerate(zip(in_refs, out_refs))]
        for k, p in enumerate(_relations(x, y, c)):
            sends.append(_remote(small_ref, slot((x, y, c)), send_sems, recv_sems, n + k, p))
            recvs.append(_remote(small_ref, slot(p), send_sems, recv_sems, n + k, p))
        for cp in sends:
            cp.start()
        for cp in recvs:
            cp.wait_recv()
        for cp in sends:
            cp.wait_send()

    hbm, vmem = pl.BlockSpec(memory_space=pl.ANY), pl.BlockSpec(memory_space=pltpu.VMEM)
    pairs = n + N_DEV - 1
    res = pl.pallas_call(
        body, name="grad_join_halves",
        out_shape=[jax.ShapeDtypeStruct(t.shape, t.dtype) for t in shards] + [jax.ShapeDtypeStruct((N_DEV,) + small.shape, F32)],
        in_specs=[hbm] * n + [vmem], out_specs=[hbm] * n + [vmem], input_output_aliases={i: i for i in range(n)},
        scratch_shapes=[pltpu.SemaphoreType.DMA((pairs,)), pltpu.SemaphoreType.DMA((pairs,))],
    )(*shards, small)
    return res[:n], res[n]


def _row_tile(rows, row_bytes, limit=1 << 20):
    best = 8
    for t in range(8, rows + 1, 8):
        if rows % t == 0 and t * row_bytes <= limit:
            best = t
    return best


def _sum_pieces(g, land, place, name):
    _, _, rh, cols = g.shape
    tr = _row_tile(rh, (N_DEV - 1) * cols * 4, 4 << 20)

    def body(p_ref, g_ref, l_ref, out_ref):
        acc = g_ref[...]
        for k in range(N_DEV - 1):
            acc = acc + l_ref[k].astype(F32)
        out_ref[...] = acc

    return pl.pallas_call(
        body, name=name,
        grid_spec=pltpu.PrefetchScalarGridSpec(
            num_scalar_prefetch=1, grid=(rh // tr,),
            in_specs=[pl.BlockSpec((None, None, tr, cols), lambda r, p: (p[0], p[1], r, 0)),
                      pl.BlockSpec((N_DEV - 1, tr, cols), lambda r, p: (0, r, 0))],
            out_specs=pl.BlockSpec((None, tr, cols), lambda r, p: (p[1], r, 0))),
        out_shape=jax.ShapeDtypeStruct((2, rh, cols), g.dtype),
        compiler_params=_params(("arbitrary",)),
    )(place, g, land)


def _adamw_math(w, g, m, v):
    m = ADAM_B1 * m + (1.0 - ADAM_B1) * g
    v = ADAM_B2 * v + (1.0 - ADAM_B2) * (g * g)
    m_hat = m / (1.0 - ADAM_B1 ** ADAM_STEP)
    v_hat = v / (1.0 - ADAM_B2 ** ADAM_STEP)
    delta = -ADAM_LR * (m_hat / (jnp.sqrt(v_hat) + ADAM_EPS) + ADAM_WD * w)
    return delta, m, v


def _adamw(w, g, m, v, name, pass_gradient=False):
    r, cols = w.shape
    tr = _row_tile(r, cols * 4)
    n_out = 4 if pass_gradient else 3

    def body(w_ref, g_ref, m_ref, v_ref, *out_refs):
        g_val = g_ref[...]
        d_ref, nm_ref, nv_ref = out_refs[-3:]
        d_ref[...], nm_ref[...], nv_ref[...] = _adamw_math(w_ref[...], g_val, m_ref[...], v_ref[...])
        if pass_gradient:
            out_refs[0][...] = g_val

    blk = pl.BlockSpec((tr, cols), lambda i: (i, 0))
    shape = jax.ShapeDtypeStruct((r, cols), F32)
    return pl.pallas_call(
        body, name=name, grid=(r // tr,), in_specs=[blk] * 4, out_specs=[blk] * n_out, out_shape=[shape] * n_out,
        compiler_params=_params(("arbitrary",)),
    )(w, g, m, v)


def _sum_devices(gathered):
    _, r, cols = gathered.shape

    def body(a_ref, g_ref):
        g = a_ref[0]
        for k in range(1, N_DEV):
            g = g + a_ref[k]
        g_ref[...] = g

    return pl.pallas_call(body, name="sum_small_grads", out_shape=jax.ShapeDtypeStruct((r, cols), F32))(gathered)


def _pack_conv(cw):
    flat = cw.reshape(-1)
    return jnp.pad(flat, (0, ROWS_CONV * D_MODEL - flat.shape[0])).reshape(ROWS_CONV, D_MODEL)


def _unpack_conv(rows):
    return rows.reshape(-1)[:CONV_WIDTH * UP_W // N_CHIPS].reshape(CONV_WIDTH, UP_W // N_CHIPS)


def _columns_to_shards(w):
    r, n = w.shape
    return jnp.transpose(w.reshape(r, N_CHIPS, n // N_CHIPS), (1, 0, 2))


def _shards_to_columns(w):
    _, r, n = w.shape
    return jnp.transpose(w, (1, 0, 2)).reshape(r, N_CHIPS * n)


def _pack_small(g_mix_pre, g_mix_post, g_ffn_pre, g_ffn_post, sinks, conv_b, loss):
    pad_row = lambda v: jnp.pad(v.reshape(1, -1), ((0, 0), (0, D_MODEL - v.size)))
    cb = jnp.pad(conv_b.reshape(-1), (0, 6 * D_MODEL - UP_W)).reshape(6, D_MODEL)
    zeros2 = jnp.zeros((2, D_MODEL), F32)
    return jnp.concatenate([g_mix_pre.reshape(1, -1), g_mix_post.reshape(1, -1), g_ffn_pre.reshape(1, -1),
                            g_ffn_post.reshape(1, -1), pad_row(sinks), pad_row(loss), zeros2, cb, zeros2], axis=0)


def _unpack_small(p):
    return dict(mix_pre_norm=p[0:1], mix_post_norm=p[1:2], ffn_pre_norm=p[2:3], ffn_post_norm=p[3:4],
                attn_sinks=p[4:5, :N_ATTN_HEADS], loss=p[5, 0], conv_b=p[8:14].reshape(1, -1)[:, :UP_W],
                conv_w=_unpack_conv(p[SMALL_ROWS:SMALL_ROWS + ROWS_CONV]))


def _local_step(x, target, g_mix_pre, w_in, sinks, w_out, g_mix_post, g_ffn_pre, w_up, conv_w, conv_b, w_down, g_ffn_post,
                distributed=True, rope=None):
    s = x.shape[0]
    consts = _ret_constants()
    sin, cos = _rope_tables(s) if rope is None else rope

    by_half = lambda g, rows: g.reshape(N_CHIPS, 2, rows // (2 * N_CHIPS), g.shape[-1])

    if distributed:
        (h1, proj), level1 = _in_proj(x, g_mix_pre, w_in, _hosted_gather_level1([w_out, w_up, w_down]))
        (mix, states, *kept), (w_out, w_up, w_down) = _mixer_fwd(proj, sinks, sin, cos, consts, _hosted_gather_level2(level1))
        w_out, w_down = w_out.reshape(D_MODEL, D_MODEL), w_down.reshape(D_FF, D_MODEL)
        w_up = w_up.reshape(N_CHIPS, D_MODEL, UP_W // N_CHIPS)
    else:
        (h1, proj), _ = _in_proj(x, g_mix_pre, w_in)
        (mix, states, *kept), _ = _mixer_fwd(proj, sinks, sin, cos, consts)
    mixed, x1, h2, u0 = _out_up_proj(mix, x, w_out, g_mix_post, g_ffn_pre, w_up)
    y, dy2, dout, du, conv_acc, tail_acc = _ffn_tail(u0, x1, target, conv_w, conv_b, w_down, g_ffn_post)
    d_w_down = by_half(_weight_grad(y, dy2, 512, "grad_w_down"), D_FF)
    (du0, dx1, dmixed, dmix, head_acc), down_lands = _ffn_head_bwd(
        du, conv_w, w_up, x1, g_ffn_pre, dout, mixed, g_mix_post, w_out, _hosted_scatter([d_w_down]) if distributed else None)
    d_w_up = _weight_grad(h2, du0, UP_W // N_CHIPS, "grad_w_up", by_block=True)
    d_w_out = _weight_grad(mix, dmixed, D_MODEL, "grad_w_out")
    early = [d_w_down, by_half(d_w_up, N_CHIPS * D_MODEL), by_half(d_w_out, D_MODEL)]
    (dproj, dsinks), early_lands = _mixer_bwd(proj, dmix, states, kept, sin, cos, consts,
                                              _hosted_scatter(early[1:]) if distributed else None)
    early_lands = down_lands + early_lands
    d_w_in_t = _weight_grad(dproj, h1, 512, "grad_w_in")
    late = [by_half(d_w_in_t, IN_W)]
    (grad_x, in_acc), late_lands = _in_proj_bwd(dproj, w_in, x, g_mix_pre, dx1, _hosted_scatter(late) if distributed else None)

    small = _pack_small(in_acc[0], head_acc[1], head_acc[0], tail_acc[0], dsinks[0, :N_ATTN_HEADS], conv_acc[3],
                        jnp.sum(tail_acc[1]))
    d_conv = jnp.pad(conv_acc[0:CONV_WIDTH].reshape(-1), (0, CONV_FULL_ROWS * D_MODEL - CONV_WIDTH * UP_W))
    small = jnp.concatenate([small, d_conv.reshape(CONV_FULL_ROWS, D_MODEL)], axis=0)
    grads = dict(w_down=early[0], w_up=early[1], w_out=early[2], w_in=late[0])
    lands = dict(zip(["w_down", "w_up", "w_out", "w_in"], early_lands + late_lands))
    return grad_x, grads, lands, small


def kernel(x, mix_pre_norm, w_in, attn_sinks, w_out, mix_post_norm, ffn_pre_norm, w_up, conv_w, conv_b, w_down, ffn_post_norm, loss_target, m_mix_pre_norm, m_w_in, m_attn_sinks, m_w_out, m_mix_post_norm, m_ffn_pre_norm, m_w_up, m_conv_w, m_conv_b, m_w_down, m_ffn_post_norm, v_mix_pre_norm, v_w_in, v_attn_sinks, v_w_out, v_mix_post_norm, v_ffn_pre_norm, v_w_up, v_conv_w, v_conv_b, v_w_down, v_ffn_post_norm):
    cx, cy, cc = _place()
    shard = 2 * cx + cy

    conv_rows = jnp.pad(conv_w[0], ((0, 16 - CONV_WIDTH), (0, 0)))
    w_in_t = jnp.swapaxes(w_in[0], 0, 1)
    (w_in_all, conv_all), sin, cos = _gather_now([w_in_t.astype(BF16), conv_rows], "gather_w_in", x.shape[1])
    conv_full = _shards_to_columns(conv_all[:, 0, :CONV_WIDTH])

    grad_x, grads, lands, small = _local_step(
        x[0], loss_target[0], mix_pre_norm, w_in_all.reshape(IN_W, D_MODEL), attn_sinks.reshape(-1), w_out[0].astype(BF16),
        mix_post_norm, ffn_pre_norm, w_up[0].astype(BF16), conv_full, conv_b, w_down[0].astype(BF16), ffn_post_norm,
        rope=(sin, cos))

    place = jnp.stack([shard, cc]).astype(jnp.int32)
    mats = ["w_in", "w_out", "w_up", "w_down"]
    halves = [_sum_pieces(grads[n], lands[n], place, "sum_grad_" + n) for n in mats]
    weights = dict(w_in=(w_in, m_w_in, v_w_in), w_out=(w_out, m_w_out, v_w_out), w_up=(w_up, m_w_up, v_w_up),
                   w_down=(w_down, m_w_down, v_w_down))
    mat_out = {}
    joined_all, small_all = _join_halves(halves, small)
    for n, joined in zip(mats, joined_all):
        w, m, v = (t[0] for t in weights[n])
        if n == "w_in":
            w, m, v = (jnp.swapaxes(t, 0, 1) for t in (w, m, v))
        res = tuple(_adamw(w, joined.reshape(w.shape), m, v, "adamw_" + n, pass_gradient=True))
        mat_out[n] = tuple(jnp.swapaxes(t, 0, 1) for t in res) if n == "w_in" else res

    small_sum = _sum_devices(small_all)
    d_conv_full = small_sum[SMALL_ROWS:].reshape(-1)[:CONV_WIDTH * UP_W].reshape(CONV_WIDTH, UP_W)
    d_conv_mine = lax.dynamic_slice_in_dim(d_conv_full, shard * (UP_W // N_CHIPS), UP_W // N_CHIPS, axis=1)
    g_s = jnp.concatenate([small_sum[:SMALL_ROWS], _pack_conv(d_conv_mine)], axis=0)
    zero = jnp.zeros((), F32)
    pack_rep = lambda a, b, c_, d, e, f, cw: jnp.concatenate([_pack_small(a, b, c_, d, e, f, zero), _pack_conv(cw[0])], axis=0)
    w_s = pack_rep(mix_pre_norm, mix_post_norm, ffn_pre_norm, ffn_post_norm, attn_sinks, conv_b, conv_w)
    m_s = pack_rep(m_mix_pre_norm, m_mix_post_norm, m_ffn_pre_norm, m_ffn_post_norm, m_attn_sinks, m_conv_b, m_conv_w)
    v_s = pack_rep(v_mix_pre_norm, v_mix_post_norm, v_ffn_pre_norm, v_ffn_post_norm, v_attn_sinks, v_conv_b, v_conv_w)
    delta_s, new_m_s, new_v_s = _adamw(w_s, g_s, m_s, v_s, "adamw_small")

    names = ["mix_pre_norm", "w_in", "attn_sinks", "w_out", "mix_post_norm", "ffn_pre_norm", "w_up", "conv_w", "conv_b",
             "w_down", "ffn_post_norm"]

    def leaves(which, packed_small):
        smalls = _unpack_small(packed_small)
        return [mat_out[n][which][None] if n in mat_out else (smalls[n][None] if n == "conv_w" else smalls[n]) for n in names]

    loss = _unpack_small(g_s)["loss"]
    return (loss, grad_x[None], *leaves(0, g_s), *leaves(1, delta_s), *leaves(2, new_m_s), *leaves(3, new_v_s))
```

```python
import math

import jax
import jax.numpy as jnp
from jax import lax
from jax.experimental import pallas as pl
from jax.experimental.pallas import tpu as pltpu

F32 = jnp.float32
BF16 = jnp.bfloat16

D_MODEL = 1024
HEAD_DIM = 64
ATTN_W = 512
N_ATTN_HEADS = 8
KV_W = 128
RET_W = 512
N_RET_HEADS = 4
RET_HEAD_DIM = 128
CHUNK = 128
IN_W = 2816
D_FF = 2816
UP_W = 2 * D_FF
CONV_WIDTH = 3
RMS_EPS = 1e-6
GN_EPS = 1e-6
MASK_VALUE = -1e30
ATTN_SCALE = HEAD_DIM ** -0.5
RET_K_SCALE = RET_HEAD_DIM ** -0.5
GELU_C = math.sqrt(2.0 / math.pi)
GELU_A = 0.044715

ADAM_LR = 0.001
ADAM_B1 = 0.9
ADAM_B2 = 0.999
ADAM_EPS = 1e-08
ADAM_WD = 0.01
ADAM_STEP = 10

N_CHIPS = 4
N_DEV = 8
MESH = pl.DeviceIdType.MESH
VMEM_LIMIT_V7X = 56 * 1024 * 1024
TOKEN_TILE = 256
BIG_TOKEN_TILE = 512
IN_PROJ_TOKEN_TILE = 1024
WEIGHT_GRAD_TOKENS = 2048
FFN_ROW_BLOCK = 64
HEAD_BWD_COLS = 512
MIXER_CHUNKS_PER_STEP = 4
Q_A0, KV_A0, Q_R0, K_R0, V_R0, G_R0 = 0, 512, 768, 1280, 1792, 2304

ROWS_CONV = 8
SMALL_ROWS = 16
CONV_FULL_ROWS = 24


def _params(sem=None, **kw):
    if sem is not None:
        kw["dimension_semantics"] = sem
    return pltpu.CompilerParams(vmem_limit_bytes=VMEM_LIMIT_V7X, **kw)


def _resident(shape):
    zeros = (0,) * len(shape)
    return pl.BlockSpec(shape, lambda *_: zeros, pipeline_mode=pl.Buffered(1))


class _Hosted:
    def __init__(self, ins, outs, aliases, n_pairs, n_local, start, finish):
        self.ins, self.outs, self.aliases = list(ins), list(outs), dict(aliases)
        self.n_pairs, self.n_local, self.start, self.finish = n_pairs, max(n_local, 1), start, finish


def _hosted_call(compute, *, name, grid, in_specs, out_specs, out_shape, scratch_shapes, args, hosted=None):
    params = _params(("arbitrary",) * len(grid))
    if hosted is None:
        res = pl.pallas_call(compute, name=name, grid=grid, in_specs=in_specs, out_specs=out_specs, out_shape=out_shape,
                             scratch_shapes=scratch_shapes, compiler_params=params)(*args)
        return list(res), []
    n_in, n_out, n_scr = len(in_specs), len(out_specs), len(scratch_shapes)
    h_in, h_out = len(hosted.ins), len(hosted.outs)

    def at(step_of):
        cond = pl.program_id(0) == step_of(grid[0])
        for d in range(1, len(grid)):
            cond = jnp.logical_and(cond, pl.program_id(d) == step_of(grid[d]))
        return cond

    def body(*refs):
        ins, refs = refs[:n_in], refs[n_in:]
        h_ins, refs = refs[:h_in], refs[h_in:]
        outs, refs = refs[:n_out], refs[n_out:]
        h_outs, refs = refs[:h_out], refs[h_out:]
        scr, sems = refs[:n_scr], refs[n_scr:]

        @pl.when(at(lambda n: 0))
        def _():
            hosted.start(h_ins, h_outs, *sems)

        compute(*ins, *outs, *scr)

        @pl.when(at(lambda n: n - 1))
        def _():
            hosted.finish(h_ins, h_outs, *sems)

    hbm = pl.BlockSpec(memory_space=pl.ANY)
    res = pl.pallas_call(
        body, name=name, grid=grid,
        in_specs=list(in_specs) + [hbm] * h_in, out_specs=list(out_specs) + [hbm] * h_out,
        out_shape=list(out_shape) + hosted.outs,
        scratch_shapes=list(scratch_shapes) + [pltpu.SemaphoreType.DMA((hosted.n_pairs,)), pltpu.SemaphoreType.DMA((hosted.n_pairs,)),
                                               pltpu.SemaphoreType.DMA((hosted.n_local,))],
        input_output_aliases={n_in + a: n_out + b for a, b in hosted.aliases.items()},
        compiler_params=params,
    )(*args, *hosted.ins)
    return list(res[:n_out]), list(res[n_out:])


def _dot(a, b):
    return jnp.dot(a, b, preferred_element_type=F32)


def _dot_nt(a, b):
    return lax.dot_general(a, b, (((1,), (1,)), ((), ())), preferred_element_type=F32)


def _dot_tn(a, b):
    return lax.dot_general(a, b, (((0,), (0,)), ((), ())), preferred_element_type=F32)


def _shift_matrix(n, by):
    row = lax.broadcasted_iota(jnp.int32, (n, n), 0)
    col = lax.broadcasted_iota(jnp.int32, (n, n), 1)
    return jnp.where(col == row + by, 1.0, 0.0).astype(BF16)


def _rstd(v):
    return lax.rsqrt(jnp.mean(v * v, axis=-1, keepdims=True) + RMS_EPS)


def _rms_bwd(dy, v, rstd, gain):
    n = v * rstd
    dgain = jnp.sum(dy * n, axis=0, keepdims=True)
    dn = dy * gain
    dv = rstd * (dn - n * jnp.mean(dn * n, axis=-1, keepdims=True))
    return dv, dgain


def _lane_lo(shape):
    return (lax.broadcasted_iota(jnp.int32, shape, 1) % 128) < HEAD_DIM


GROUP = N_ATTN_HEADS // (KV_W // HEAD_DIM)


def _attn_bias(first_chunk):
    qi = lax.broadcasted_iota(jnp.int32, (GROUP * CHUNK, 2 * CHUNK), 0) % CHUNK
    kj = lax.broadcasted_iota(jnp.int32, (GROUP * CHUNK, 2 * CHUNK), 1)
    valid = jnp.logical_and(kj > qi, kj <= qi + CHUNK)
    if first_chunk:
        valid = jnp.logical_and(valid, kj >= CHUNK)
    return jnp.where(valid, 0.0, MASK_VALUE)


def _half(shape, hk):
    lo = _lane_lo(shape)
    return lo if hk == 0 else jnp.logical_not(lo)


class _GroupMasks:
    def __init__(self, sk_ref):
        groups = range(KV_W // HEAD_DIM)
        self.q = [_half((CHUNK, 128), hk) for hk in groups]
        self.kv = [_half((2 * CHUNK, 128), hk) for hk in groups]
        self.sinks = [_group_sinks(sk_ref, hk) for hk in groups]


def _stack_heads(ref, row0, col0, hk, half):
    parts = []
    for j in range(GROUP):
        h = GROUP * hk + j
        pair = ref[row0:row0 + CHUNK, col0 + (h // 2) * 128:col0 + (h // 2 + 1) * 128].astype(F32)
        if h % 2 != hk:
            pair = pltpu.roll(pair, HEAD_DIM, 1)
        parts.append(jnp.where(half, pair, 0.0))
    return jnp.concatenate(parts, axis=0)


def _unstack_heads(stacked, hk):
    pairs = []
    for q in range(GROUP // 2):
        even, odd = stacked[2 * q * CHUNK:(2 * q + 1) * CHUNK], stacked[(2 * q + 1) * CHUNK:(2 * q + 2) * CHUNK]
        pairs.append(even + pltpu.roll(odd, HEAD_DIM, 1) if hk == 0 else pltpu.roll(even, HEAD_DIM, 1) + odd)
    return pairs


def _group_sinks(sk_ref, hk):
    row = lax.broadcasted_iota(jnp.int32, (GROUP * CHUNK, 1), 0)
    col = jnp.full((GROUP * CHUNK, 1), sk_ref[GROUP * hk], F32)
    for j in range(1, GROUP):
        col = jnp.where(row >= j * CHUNK, sk_ref[GROUP * hk + j], col)
    return col


def _attn_probs(q_b, kk_b, bias, sink):
    s = _dot_nt(q_b, kk_b) * ATTN_SCALE + bias
    m = jnp.maximum(jnp.max(s, axis=-1, keepdims=True), sink)
    e = jnp.exp(s - m)
    e_sink = jnp.exp(sink - m)
    inv = 1.0 / (jnp.sum(e, axis=-1, keepdims=True) + e_sink)
    return e * inv, e_sink * inv


def _even_lanes(shape):
    return (lax.broadcasted_iota(jnp.int32, shape, 1) % 2) == 0


def _swap2(v, even):
    return jnp.where(even, pltpu.roll(v, v.shape[1] - 1, 1), pltpu.roll(v, 1, 1))


def _tile4(v):
    return jnp.concatenate([v, v, v, v], axis=-1)


def _sigmoid(v):
    return 1.0 / (1.0 + jnp.exp(-v))


def _ret_constants():
    h = N_RET_HEADS
    log_gamma = jnp.log(1.0 - jnp.power(2.0, -5.0 - jnp.arange(h, dtype=F32)))
    idx = jnp.arange(CHUNK, dtype=F32)
    rel = idx[:, None] - idx[None, :]
    d_intra = jnp.where(rel[None] >= 0, jnp.exp(log_gamma[:, None, None] * jnp.maximum(rel, 0.0)[None]), 0.0)
    xi = jnp.exp(log_gamma[None, :] * (idx[:, None] + 1.0))
    zeta = jnp.exp(log_gamma[None, :] * (CHUNK - 1.0 - idx[:, None]))
    decay = jnp.exp(log_gamma * CHUNK)
    xi_full = jnp.repeat(xi, RET_HEAD_DIM, axis=1)
    zeta_full = jnp.repeat(zeta, RET_HEAD_DIM, axis=1)
    decay_full = jnp.broadcast_to(jnp.repeat(decay, RET_HEAD_DIM)[None, :], (8, RET_W))
    return d_intra.astype(F32), xi_full.astype(F32), zeta_full.astype(F32), decay_full.astype(F32)


def _rope_tables(s):
    pos = jnp.arange(s, dtype=F32)
    angle = 1.0 / jnp.power(10000.0, jnp.linspace(0.0, 1.0, RET_HEAD_DIM // 2, dtype=F32))
    angle = jnp.repeat(angle, 2)
    sign = jnp.where(jnp.arange(RET_HEAD_DIM) % 2 == 0, -1.0, 1.0).astype(F32)
    return jnp.sin(pos[:, None] * angle[None]) * sign[None], jnp.cos(pos[:, None] * angle[None])


def _in_proj(x, gain, w_in_t, hosted=None):
    s = x.shape[0]
    tm = min(IN_PROJ_TOKEN_TILE, s)

    def body(x_ref, g_ref, w_ref, h_ref, p_ref):
        xv = x_ref[...]
        h = (xv * _rstd(xv) * g_ref[...]).astype(BF16)
        h_ref[...] = h
        p_ref[...] = _dot_nt(h, w_ref[...])

    return _hosted_call(
        body, name="in_proj", grid=(s // tm,),
        in_specs=[pl.BlockSpec((tm, D_MODEL), lambda i: (i, 0)), _resident((1, D_MODEL)), _resident((IN_W, D_MODEL))],
        out_specs=[pl.BlockSpec((tm, D_MODEL), lambda i: (i, 0)), pl.BlockSpec((tm, IN_W), lambda i: (i, 0))],
        out_shape=[jax.ShapeDtypeStruct((s, D_MODEL), BF16), jax.ShapeDtypeStruct((s, IN_W), F32)],
        scratch_shapes=[], args=(x, gain, w_in_t), hosted=hosted)


def _mixer_fwd(proj, sinks, sin, cos, consts, hosted=None):
    s = proj.shape[0]
    nc = s // CHUNK
    cps = MIXER_CHUNKS_PER_STEP
    groups = KV_W // HEAD_DIM
    d_intra, xi_full, zeta_full, decay_full = consts

    def body(sk_ref, p_ref, pkv_ref, sin_ref, cos_ref, dm_ref, xi_ref, ze_ref, dc_ref,
             mix_ref, st_ref, pr_ref, ps_ref, ra_ref, on_ref, rs_ref, rq_ref, rk_ref, rz_ref, sq_ref, sg_ref, state):
        i = pl.program_id(0)

        @pl.when(i == 0)
        def _():
            state[...] = jnp.zeros_like(state)

        st = [state[h] for h in range(N_RET_HEADS)]
        bias_any = _attn_bias(False)
        bias_c0 = jnp.where(i == 0, _attn_bias(True), bias_any)
        even = _even_lanes((CHUNK, RET_W))
        masks = _GroupMasks(sk_ref)
        for c in range(cps):
            r0 = c * CHUNK
            rows = slice(r0, r0 + CHUNK)

            kv_cur = p_ref[rows, KV_A0:KV_A0 + 2 * KV_W]
            kv_prev = pkv_ref[...] if c == 0 else p_ref[r0 - CHUNK:r0, KV_A0:KV_A0 + 2 * KV_W]
            kk = jnp.concatenate([kv_prev[:, :KV_W], kv_cur[:, :KV_W]], axis=0)
            vv = jnp.concatenate([kv_prev[:, KV_W:], kv_cur[:, KV_W:]], axis=0)
            kk_b = kk.astype(BF16)
            bias = bias_c0 if c == 0 else bias_any
            for hk in range(KV_W // HEAD_DIM):
                q_b = _stack_heads(p_ref, r0, Q_A0, hk, masks.q[hk]).astype(BF16)
                p, p_sink = _attn_probs(q_b, kk_b, bias, masks.sinks[hk])
                p_b = p.astype(BF16)
                pr_ref[c, hk] = p_b
                ps_ref[c, hk] = p_sink
                sq_ref[c, hk] = q_b
                v_b = jnp.where(masks.kv[hk], vv, 0.0).astype(BF16)
                for q, pair in enumerate(_unstack_heads(_dot(p_b, v_b), hk)):
                    pi = (GROUP // 2) * hk + q
                    mix_ref[rows, pi * 128:(pi + 1) * 128] = pair.astype(BF16)

            sin4, cos4 = _tile4(sin_ref[rows, :]), _tile4(cos_ref[rows, :])
            q_r = p_ref[rows, Q_R0:Q_R0 + RET_W]
            k_r = p_ref[rows, K_R0:K_R0 + RET_W] * RET_K_SCALE
            q_r = q_r * cos4 + _swap2(q_r, even) * sin4
            k_r = k_r * cos4 + _swap2(k_r, even) * sin4
            q_b, k_b, kz_b = q_r.astype(BF16), k_r.astype(BF16), (k_r * ze_ref[...]).astype(BF16)
            rq_ref[rows, :], rk_ref[rows, :], rz_ref[rows, :] = q_b, k_b, kz_b
            for h in range(N_RET_HEADS):
                sl = slice(h * RET_HEAD_DIM, (h + 1) * RET_HEAD_DIM)
                qh, kh = q_b[:, sl], k_b[:, sl]
                vh = p_ref[rows, V_R0 + h * RET_HEAD_DIM:V_R0 + (h + 1) * RET_HEAD_DIM].astype(BF16)
                st_ref[c, h] = st[h]
                a_b = (_dot_nt(qh, kh) * dm_ref[h]).astype(BF16)
                qx = (q_r[:, sl] * xi_ref[:, sl]).astype(BF16)
                o = _dot(jnp.concatenate([a_b, qx], axis=1), jnp.concatenate([vh, st[h].astype(BF16)], axis=0))
                st[h] = dc_ref[0:1, sl] * st[h] + _dot_tn(kz_b[:, sl], vh)
                mu = jnp.mean(o, axis=-1, keepdims=True)
                oc = o - mu
                rs = lax.rsqrt(jnp.mean(oc * oc, axis=-1, keepdims=True) + GN_EPS)
                on = oc * rs
                ra_ref[c, h], on_ref[rows, sl], rs_ref[c, h] = a_b, on, rs
                g = p_ref[rows, G_R0 + h * RET_HEAD_DIM:G_R0 + (h + 1) * RET_HEAD_DIM]
                sg = _sigmoid(g)
                sg_ref[rows, sl] = sg
                mix_ref[rows, ATTN_W + h * RET_HEAD_DIM:ATTN_W + (h + 1) * RET_HEAD_DIM] = (g * sg * on).astype(BF16)
        for h in range(N_RET_HEADS):
            state[h] = st[h]

    return _hosted_call(
        body, name="mixer_fwd", grid=(nc // cps,),
        in_specs=[
            pl.BlockSpec(memory_space=pltpu.SMEM),
            pl.BlockSpec((cps * CHUNK, IN_W), lambda i: (i, 0)),
            pl.BlockSpec((CHUNK, 2 * KV_W), lambda i: (jnp.maximum(cps * i - 1, 0), KV_A0 // (2 * KV_W))),
            pl.BlockSpec((cps * CHUNK, RET_HEAD_DIM), lambda i: (i, 0)),
            pl.BlockSpec((cps * CHUNK, RET_HEAD_DIM), lambda i: (i, 0)),
            _resident((N_RET_HEADS, CHUNK, CHUNK)), _resident((CHUNK, RET_W)), _resident((CHUNK, RET_W)), _resident((8, RET_W)),
        ],
        out_specs=[
            pl.BlockSpec((cps * CHUNK, D_MODEL), lambda i: (i, 0)),
            pl.BlockSpec((cps, N_RET_HEADS, RET_HEAD_DIM, RET_HEAD_DIM), lambda i: (i, 0, 0, 0)),
            pl.BlockSpec((cps, groups, GROUP * CHUNK, 2 * CHUNK), lambda i: (i, 0, 0, 0)),
            pl.BlockSpec((cps, groups, GROUP * CHUNK, 1), lambda i: (i, 0, 0, 0)),
            pl.BlockSpec((cps, N_RET_HEADS, CHUNK, CHUNK), lambda i: (i, 0, 0, 0)),
            pl.BlockSpec((cps * CHUNK, RET_W), lambda i: (i, 0)),
            pl.BlockSpec((cps, N_RET_HEADS, CHUNK, 1), lambda i: (i, 0, 0, 0)),
        ] + [pl.BlockSpec((cps * CHUNK, RET_W), lambda i: (i, 0))] * 3 + [
            pl.BlockSpec((cps, groups, GROUP * CHUNK, 128), lambda i: (i, 0, 0, 0)),
            pl.BlockSpec((cps * CHUNK, RET_W), lambda i: (i, 0))],
        out_shape=[jax.ShapeDtypeStruct((s, D_MODEL), BF16),
                   jax.ShapeDtypeStruct((nc, N_RET_HEADS, RET_HEAD_DIM, RET_HEAD_DIM), F32),
                   jax.ShapeDtypeStruct((nc, groups, GROUP * CHUNK, 2 * CHUNK), BF16),
                   jax.ShapeDtypeStruct((nc, groups, GROUP * CHUNK, 1), F32),
                   jax.ShapeDtypeStruct((nc, N_RET_HEADS, CHUNK, CHUNK), BF16),
                   jax.ShapeDtypeStruct((s, RET_W), F32),
                   jax.ShapeDtypeStruct((nc, N_RET_HEADS, CHUNK, 1), F32)] + [jax.ShapeDtypeStruct((s, RET_W), BF16)] * 3 + [
                   jax.ShapeDtypeStruct((nc, groups, GROUP * CHUNK, 128), BF16), jax.ShapeDtypeStruct((s, RET_W), F32)],
        scratch_shapes=[pltpu.VMEM((N_RET_HEADS, RET_HEAD_DIM, RET_HEAD_DIM), F32)],
        args=(sinks, proj, proj, sin, cos, d_intra, xi_full, zeta_full, decay_full), hosted=hosted)


def _out_up_proj(mix, x, w_out, g_post, g_pre, w_up):
    s = x.shape[0]
    tm = min(BIG_TOKEN_TILE, s)
    blk = UP_W // N_CHIPS

    def body(mix_ref, x_ref, wo_ref, g2_ref, g3_ref, wu_ref, mixed_ref, x1_ref, h2_ref, u0_ref):
        mixed = _dot(mix_ref[...], wo_ref[...])
        mixed_ref[...] = mixed
        x1 = x_ref[...] + mixed * _rstd(mixed) * g2_ref[...]
        x1_ref[...] = x1
        h2 = (x1 * _rstd(x1) * g3_ref[...]).astype(BF16)
        h2_ref[...] = h2
        for k in range(N_CHIPS):
            u0_ref[:, k * blk:(k + 1) * blk] = _dot(h2, wu_ref[k]).astype(BF16)

    tok = lambda w: pl.BlockSpec((tm, w), lambda i: (i, 0))
    return pl.pallas_call(
        body, name="out_up_proj", grid=(s // tm,),
        in_specs=[tok(D_MODEL), tok(D_MODEL), _resident((D_MODEL, D_MODEL)), _resident((1, D_MODEL)), _resident((1, D_MODEL)),
                  _resident((N_CHIPS, D_MODEL, blk))],
        out_specs=[tok(D_MODEL), tok(D_MODEL), tok(D_MODEL), tok(UP_W)],
        out_shape=[jax.ShapeDtypeStruct((s, D_MODEL), F32), jax.ShapeDtypeStruct((s, D_MODEL), F32),
                   jax.ShapeDtypeStruct((s, D_MODEL), BF16), jax.ShapeDtypeStruct((s, UP_W), BF16)],
        compiler_params=_params(("arbitrary",)),
    )(mix, x, w_out, g_post, g_pre, w_up)


def _ffn_tail(u0, x1, target, conv_w, conv_b, w_down, g_post):
    s = x1.shape[0]
    tm = TOKEN_TILE
    last = s // tm - 1
    rb, lanes = FFN_ROW_BLOCK, 128

    def body(u0_ref, x1_ref, t_ref, cw_ref, cb_ref, wd_ref, g_ref,
             y_ref, dy2_ref, dout_ref, du_ref, cacc_ref, gacc_ref, u1_s, u2_s, carry, gelu_s, slope_s, dy_s, cacc):
        i = pl.program_id(0)

        @pl.when(i == 0)
        def _():
            carry[...] = jnp.zeros_like(carry)
            cacc[...] = jnp.zeros_like(cacc)
            gacc_ref[...] = jnp.zeros_like(gacc_ref)

        shift1, shift2 = _shift_matrix(tm, -1), _shift_matrix(tm, -2)
        r8 = lax.broadcasted_iota(jnp.int32, (8, 1), 0)
        wide = 2 * lanes

        def shift_block(col):
            cols = slice(col, col + wide)
            u1_s[:, cols] = _dot(shift1, u0_ref[:, cols])
            u2_s[:, cols] = _dot(shift2, u0_ref[:, cols])
            c14, c15 = carry[14:15, cols], carry[15:16, cols]
            u1_s[0:8, cols] = jnp.where(r8 == 0, c15, u1_s[0:8, cols])
            u2_s[0:8, cols] = jnp.where(r8 == 0, c14, jnp.where(r8 == 1, c15, u2_s[0:8, cols]))

        def taps(col):
            return (cw_ref[0:1, col:col + lanes], cw_ref[1:2, col:col + lanes], cw_ref[2:3, col:col + lanes],
                    cb_ref[0:1, col:col + lanes])

        def shifted(r0, col):
            return (u2_s[r0:r0 + rb, col:col + lanes], u1_s[r0:r0 + rb, col:col + lanes],
                    u0_ref[r0:r0 + rb, col:col + lanes].astype(F32))

        def conv(r0, col, w):
            u2, u1, uc = shifted(r0, col)
            return w[0] * u2 + w[1] * u1 + w[2] * uc + w[3]

        fold = lambda v: jnp.sum(v.reshape(rb // 8, 8, lanes), axis=0)

        shift_block(0)
        shift_block(D_FF)
        for j in range(D_FF // lanes):
            cg, cv = j * lanes, D_FF + j * lanes
            if cg % wide == 0 and cg + wide < D_FF:
                shift_block(cg + wide)
                shift_block(cv + wide)
            wg, wv = taps(cg), taps(cv)
            for r0 in range(0, tm, rb):
                gate, val = conv(r0, cg, wg), conv(r0, cv, wv)
                g2 = gate * gate
                th = jnp.tanh(gate * (GELU_C + GELU_C * GELU_A * g2))
                hp = 0.5 * th + 0.5
                gelu = gate * hp
                dgelu = hp + gate * (1.0 - th * th) * (0.5 * GELU_C + 1.5 * GELU_C * GELU_A * g2)
                y_ref[r0:r0 + rb, cg:cg + lanes] = (gelu * val).astype(BF16)
                gelu_s[r0:r0 + rb, cg:cg + lanes] = gelu
                slope_s[r0:r0 + rb, cg:cg + lanes] = dgelu * val

        y2 = _dot(y_ref[...], wd_ref[...])
        r4 = _rstd(y2)
        gain = g_ref[...]
        out = x1_ref[...] + y2 * r4 * gain
        diff = out - t_ref[...]
        dout = diff * (1.0 / D_MODEL)
        dout_ref[...] = dout
        dy2, dgain = _rms_bwd(dout, y2, r4, gain)
        dy2_b = dy2.astype(BF16)
        dy2_ref[...] = dy2_b
        gacc_ref[0:1, :] += dgain
        gacc_ref[1:2, :] += 0.5 * jnp.sum(diff * dout, axis=0, keepdims=True)
        carry[...] = u0_ref[tm - 16:tm, :].astype(F32)

        dy_s[:, 0:wide] = _dot_nt(dy2_b, wd_ref[0:wide, :])
        for j in range(D_FF // lanes):
            cg, cv = j * lanes, D_FF + j * lanes
            if cg % wide == 0 and cg + wide < D_FF:
                dy_s[:, cg + wide:cg + 2 * wide] = _dot_nt(dy2_b, wd_ref[cg + wide:cg + 2 * wide, :])
            acc = [[jnp.zeros((8, lanes), F32) for _ in range(CONV_WIDTH + 1)] for _ in range(2)]
            for r0 in range(0, tm, rb):
                dy = dy_s[r0:r0 + rb, cg:cg + lanes]
                d_gate = dy * slope_s[r0:r0 + rb, cg:cg + lanes]
                d_val = dy * gelu_s[r0:r0 + rb, cg:cg + lanes]
                for side, (col, d) in enumerate(((cg, d_gate), (cv, d_val))):
                    du_ref[r0:r0 + rb, col:col + lanes] = d.astype(BF16)
                    for k, u in enumerate(shifted(r0, col)):
                        acc[side][k] = acc[side][k] + fold(d * u)
                    acc[side][CONV_WIDTH] = acc[side][CONV_WIDTH] + fold(d)
            for side, col in enumerate((cg, cv)):
                for k in range(CONV_WIDTH + 1):
                    cacc[8 * k:8 * k + 8, col:col + lanes] += acc[side][k]

        @pl.when(i == last)
        def _():
            for k in range(CONV_WIDTH + 1):
                cacc_ref[k:k + 1, :] = jnp.sum(cacc[8 * k:8 * k + 8, :], axis=0, keepdims=True)
            cacc_ref[CONV_WIDTH + 1:8, :] = jnp.zeros((8 - CONV_WIDTH - 1, UP_W), F32)

    tok = lambda w: pl.BlockSpec((tm, w), lambda i: (i, 0))
    return pl.pallas_call(
        body, name="ffn_tail", grid=(s // tm,),
        in_specs=[tok(UP_W), tok(D_MODEL), tok(D_MODEL), _resident((CONV_WIDTH, UP_W)), _resident((1, UP_W)),
                  _resident((D_FF, D_MODEL)), _resident((1, D_MODEL))],
        out_specs=[tok(D_FF), tok(D_MODEL), tok(D_MODEL), tok(UP_W),
                   pl.BlockSpec((8, UP_W), lambda i: (0, 0)), pl.BlockSpec((8, D_MODEL), lambda i: (0, 0))],
        out_shape=[jax.ShapeDtypeStruct((s, D_FF), BF16), jax.ShapeDtypeStruct((s, D_MODEL), BF16),
                   jax.ShapeDtypeStruct((s, D_MODEL), F32), jax.ShapeDtypeStruct((s, UP_W), BF16),
                   jax.ShapeDtypeStruct((8, UP_W), F32), jax.ShapeDtypeStruct((8, D_MODEL), F32)],
        scratch_shapes=[pltpu.VMEM((tm, UP_W), F32), pltpu.VMEM((tm, UP_W), F32), pltpu.VMEM((16, UP_W), F32),
                        pltpu.VMEM((tm, D_FF), F32), pltpu.VMEM((tm, D_FF), F32),
                        pltpu.VMEM((tm, D_FF), F32), pltpu.VMEM((8 * (CONV_WIDTH + 1), UP_W), F32)],
        compiler_params=_params(("arbitrary",)),
    )(u0, x1, target, conv_w, conv_b, w_down, g_post)


def _ffn_head_bwd(du, conv_w, w_up, x1, g_pre, dout, mixed, g_post, w_out, hosted=None):
    s = x1.shape[0]
    tm = TOKEN_TILE
    nt = s // tm
    blk = UP_W // N_CHIPS

    def body(du_ref, halo_ref, cw_ref, wu_ref, x1_ref, g3_ref, dout_ref, mixed_ref, g2_ref, wo_ref,
             du0_ref, dx1_ref, dmixed_ref, dmix_ref, gacc_ref, dbuf):
        i = pl.program_id(0)

        @pl.when(i == 0)
        def _():
            gacc_ref[...] = jnp.zeros_like(gacc_ref)

        dbuf[0:tm, :] = du_ref[...].astype(F32)
        dbuf[tm:tm + 16, :] = jnp.where(i < nt - 1, halo_ref[...].astype(F32), 0.0)
        dh2 = jnp.zeros((tm, D_MODEL), F32)
        for k in range(N_CHIPS):
            for c0 in range(0, blk, HEAD_BWD_COLS):
                width = min(HEAD_BWD_COLS, blk - c0)
                cols = slice(k * blk + c0, k * blk + c0 + width)
                du0_b = (cw_ref[2:3, cols] * dbuf[0:tm, cols] + cw_ref[1:2, cols] * dbuf[1:1 + tm, cols]
                         + cw_ref[0:1, cols] * dbuf[2:2 + tm, cols]).astype(BF16)
                du0_ref[:, cols] = du0_b
                dh2 = dh2 + _dot_nt(du0_b, wu_ref[k, :, c0:c0 + width])
        x1 = x1_ref[...]
        d3, dg3 = _rms_bwd(dh2, x1, _rstd(x1), g3_ref[...])
        dx1 = dout_ref[...] + d3
        dx1_ref[...] = dx1
        mixed = mixed_ref[...]
        dmixed, dg2 = _rms_bwd(dx1, mixed, _rstd(mixed), g2_ref[...])
        dmixed_b = dmixed.astype(BF16)
        dmixed_ref[...] = dmixed_b
        dmix_ref[...] = _dot_nt(dmixed_b, wo_ref[...]).astype(BF16)
        gacc_ref[0:1, :] += dg3
        gacc_ref[1:2, :] += dg2

    tok = lambda w: pl.BlockSpec((tm, w), lambda i: (i, 0))
    halo = pl.BlockSpec((16, UP_W), lambda i: (jnp.minimum(i + 1, nt - 1) * (tm // 16), 0))
    return _hosted_call(
        body, name="ffn_head_bwd", grid=(nt,),
        in_specs=[tok(UP_W), halo, _resident((CONV_WIDTH, UP_W)), _resident((N_CHIPS, D_MODEL, blk)), tok(D_MODEL),
                  _resident((1, D_MODEL)), tok(D_MODEL), tok(D_MODEL), _resident((1, D_MODEL)), _resident((D_MODEL, D_MODEL))],
        out_specs=[tok(UP_W), tok(D_MODEL), tok(D_MODEL), tok(D_MODEL), pl.BlockSpec((8, D_MODEL), lambda i: (0, 0))],
        out_shape=[jax.ShapeDtypeStruct((s, UP_W), BF16), jax.ShapeDtypeStruct((s, D_MODEL), F32),
                   jax.ShapeDtypeStruct((s, D_MODEL), BF16), jax.ShapeDtypeStruct((s, D_MODEL), BF16),
                   jax.ShapeDtypeStruct((8, D_MODEL), F32)],
        scratch_shapes=[pltpu.VMEM((tm + 16, UP_W), F32)],
        args=(du, du, conv_w, w_up, x1, g_pre, dout, mixed, g_post, w_out), hosted=hosted)


def _mixer_bwd(proj, dmix, states, kept, sin, cos, consts, hosted=None):
    probs, p_sinks, ret_scores, ret_normed, ret_rstd, ret_q, ret_k, ret_kz, stacked_q, gate_sig = kept
    s = proj.shape[0]
    nc = s // CHUNK
    cps = MIXER_CHUNKS_PER_STEP
    nb = nc // cps
    groups = KV_W // HEAD_DIM
    d_intra, xi_full, zeta_full, decay_full = consts

    def body(p_ref, pkv_ref, dmix_ref, st_ref, pr_ref, ps_ref, ra_ref, on_ref, rs_ref, rq_ref, rk_ref, rz_ref, sq_ref, sg_ref,
             sin_ref, cos_ref, dm_ref, xi_ref, ze_ref, dc_ref, dp_ref, dsk_ref, gstate, ckv, dsk_acc):
        i = pl.program_id(0)
        block = nb - 1 - i

        @pl.when(i == 0)
        def _():
            gstate[...] = jnp.zeros_like(gstate)
            ckv[...] = jnp.zeros_like(ckv)
            dsk_acc[...] = jnp.zeros_like(dsk_acc)

        gs_all = [gstate[h] for h in range(N_RET_HEADS)]
        later_kv = ckv[...]
        lane = lax.broadcasted_iota(jnp.int32, (CHUNK, 128), 1)
        dsk = jnp.zeros((CHUNK, 128), F32)
        even = _even_lanes((CHUNK, RET_W))
        half_q = [_half((CHUNK, 128), hk) for hk in range(groups)]
        half_kv = [_half((2 * CHUNK, 128), hk) for hk in range(groups)]
        for c in reversed(range(cps)):
            r0 = c * CHUNK
            rows = slice(r0, r0 + CHUNK)

            kv_cur = p_ref[rows, KV_A0:KV_A0 + 2 * KV_W]
            kv_prev = pkv_ref[...] if c == 0 else p_ref[r0 - CHUNK:r0, KV_A0:KV_A0 + 2 * KV_W]
            kk = jnp.concatenate([kv_prev[:, :KV_W], kv_cur[:, :KV_W]], axis=0)
            vv = jnp.concatenate([kv_prev[:, KV_W:], kv_cur[:, KV_W:]], axis=0)
            vv_b = vv.astype(BF16)
            dkk = jnp.zeros((2 * CHUNK, KV_W), F32)
            dvv = jnp.zeros((2 * CHUNK, KV_W), F32)
            for hk in range(groups):
                q_b = sq_ref[c, hk]
                do_b = _stack_heads(dmix_ref, r0, 0, hk, half_q[hk]).astype(BF16)
                p_b = pr_ref[c, hk]
                p = p_b.astype(F32)
                dpr = _dot_nt(do_b, vv_b)
                delta = jnp.sum(p * dpr, axis=-1, keepdims=True)
                ds_b = (p * (dpr - delta) * ATTN_SCALE).astype(BF16)
                dsink = -ps_ref[c, hk] * delta
                for j in range(GROUP):
                    dsk = dsk + jnp.where(lane == GROUP * hk + j, dsink[j * CHUNK:(j + 1) * CHUNK], 0.0)
                k_b = jnp.where(half_kv[hk], kk, 0.0).astype(BF16)
                for q, pair in enumerate(_unstack_heads(_dot(ds_b, k_b), hk)):
                    pi = (GROUP // 2) * hk + q
                    dp_ref[rows, Q_A0 + pi * 128:Q_A0 + (pi + 1) * 128] = pair.astype(BF16)
                dkk = dkk + _dot_tn(ds_b, q_b)
                dvv = dvv + _dot_tn(p_b, do_b)
            dp_ref[rows, KV_A0:KV_A0 + KV_W] = (dkk[CHUNK:] + later_kv[:, :KV_W]).astype(BF16)
            dp_ref[rows, KV_A0 + KV_W:KV_A0 + 2 * KV_W] = (dvv[CHUNK:] + later_kv[:, KV_W:]).astype(BF16)
            later_kv = jnp.concatenate([dkk[:CHUNK], dvv[:CHUNK]], axis=1)

            sin4, cos4 = _tile4(sin_ref[rows, :]), _tile4(cos_ref[rows, :])
            dq_parts, dk_parts = [], []
            for h in range(N_RET_HEADS):
                sl = slice(h * RET_HEAD_DIM, (h + 1) * RET_HEAD_DIM)
                qh, kh = rq_ref[rows, sl], rk_ref[rows, sl]
                vh = p_ref[rows, V_R0 + h * RET_HEAD_DIM:V_R0 + (h + 1) * RET_HEAD_DIM].astype(BF16)
                st_b = st_ref[c, h].astype(BF16)
                gs = gs_all[h]
                gs_b = gs.astype(BF16)
                xi_h = xi_ref[:, sl]
                dm = dm_ref[h]
                a_b, on, rs = ra_ref[c, h], on_ref[rows, sl], rs_ref[c, h]
                g = p_ref[rows, G_R0 + h * RET_HEAD_DIM:G_R0 + (h + 1) * RET_HEAD_DIM]
                sg = sg_ref[rows, sl]
                dr = dmix_ref[rows, ATTN_W + h * RET_HEAD_DIM:ATTN_W + (h + 1) * RET_HEAD_DIM].astype(F32)
                dp_ref[rows, G_R0 + h * RET_HEAD_DIM:G_R0 + (h + 1) * RET_HEAD_DIM] = (
                    dr * on * (sg * (1.0 + g * (1.0 - sg)))).astype(BF16)
                don = dr * g * sg
                do = rs * (don - jnp.mean(don, axis=-1, keepdims=True) - on * jnp.mean(don * on, axis=-1, keepdims=True))
                do_b = do.astype(BF16)
                dox_b = (do * xi_h).astype(BF16)
                da_b = (_dot_nt(do_b, vh) * dm).astype(BF16)
                dq_parts.append(_dot(da_b, kh) + _dot_nt(dox_b, st_b))
                dk_parts.append(_dot_tn(da_b, qh) + ze_ref[:, sl] * _dot_nt(vh, gs_b))
                dv = _dot_tn(a_b, do_b) + _dot(rz_ref[rows, sl], gs_b)
                dp_ref[rows, V_R0 + h * RET_HEAD_DIM:V_R0 + (h + 1) * RET_HEAD_DIM] = dv.astype(BF16)
                gs_all[h] = dc_ref[0:1, sl] * gs + _dot_tn(qh, dox_b)
            dq = jnp.concatenate(dq_parts, axis=-1)
            dk = jnp.concatenate(dk_parts, axis=-1)
            dp_ref[rows, Q_R0:Q_R0 + RET_W] = (dq * cos4 - _swap2(dq, even) * sin4).astype(BF16)
            dp_ref[rows, K_R0:K_R0 + RET_W] = (RET_K_SCALE * (dk * cos4 - _swap2(dk, even) * sin4)).astype(BF16)

        for h in range(N_RET_HEADS):
            gstate[h] = gs_all[h]
        ckv[...] = later_kv
        dsk_acc[...] += dsk

        @pl.when(i == nb - 1)
        def _():
            dsk_ref[...] = jnp.sum(dsk_acc[...], axis=0, keepdims=True)

    rev = lambda i: nb - 1 - i
    return _hosted_call(
        body, name="mixer_bwd", grid=(nb,),
        in_specs=[
            pl.BlockSpec((cps * CHUNK, IN_W), lambda i: (rev(i), 0)),
            pl.BlockSpec((CHUNK, 2 * KV_W), lambda i: (jnp.maximum(cps * rev(i) - 1, 0), KV_A0 // (2 * KV_W))),
            pl.BlockSpec((cps * CHUNK, D_MODEL), lambda i: (rev(i), 0)),
            pl.BlockSpec((cps, N_RET_HEADS, RET_HEAD_DIM, RET_HEAD_DIM), lambda i: (rev(i), 0, 0, 0)),
            pl.BlockSpec((cps, groups, GROUP * CHUNK, 2 * CHUNK), lambda i: (rev(i), 0, 0, 0)),
            pl.BlockSpec((cps, groups, GROUP * CHUNK, 1), lambda i: (rev(i), 0, 0, 0)),
            pl.BlockSpec((cps, N_RET_HEADS, CHUNK, CHUNK), lambda i: (rev(i), 0, 0, 0)),
            pl.BlockSpec((cps * CHUNK, RET_W), lambda i: (rev(i), 0)),
            pl.BlockSpec((cps, N_RET_HEADS, CHUNK, 1), lambda i: (rev(i), 0, 0, 0)),
            pl.BlockSpec((cps * CHUNK, RET_W), lambda i: (rev(i), 0)), pl.BlockSpec((cps * CHUNK, RET_W), lambda i: (rev(i), 0)),
            pl.BlockSpec((cps * CHUNK, RET_W), lambda i: (rev(i), 0)),
            pl.BlockSpec((cps, groups, GROUP * CHUNK, 128), lambda i: (rev(i), 0, 0, 0)),
            pl.BlockSpec((cps * CHUNK, RET_W), lambda i: (rev(i), 0)),
            pl.BlockSpec((cps * CHUNK, RET_HEAD_DIM), lambda i: (rev(i), 0)),
            pl.BlockSpec((cps * CHUNK, RET_HEAD_DIM), lambda i: (rev(i), 0)),
            _resident((N_RET_HEADS, CHUNK, CHUNK)), _resident((CHUNK, RET_W)), _resident((CHUNK, RET_W)), _resident((8, RET_W)),
        ],
        out_specs=[pl.BlockSpec((cps * CHUNK, IN_W), lambda i: (rev(i), 0)), pl.BlockSpec((1, 128), lambda i: (0, 0))],
        out_shape=[jax.ShapeDtypeStruct((s, IN_W), BF16), jax.ShapeDtypeStruct((1, 128), F32)],
        scratch_shapes=[pltpu.VMEM((N_RET_HEADS, RET_HEAD_DIM, RET_HEAD_DIM), F32), pltpu.VMEM((CHUNK, 2 * KV_W), F32),
                        pltpu.VMEM((CHUNK, 128), F32)],
        args=(proj, proj, dmix, states, probs, p_sinks, ret_scores, ret_normed, ret_rstd, ret_q, ret_k, ret_kz, stacked_q, gate_sig,
              sin, cos, d_intra, xi_full, zeta_full, decay_full), hosted=hosted)


def _in_proj_bwd(dproj, w_in_t, x, gain, dx1, hosted=None):
    s = x.shape[0]
    tm = min(BIG_TOKEN_TILE, s)

    def body(dp_ref, w_ref, x_ref, g_ref, dx1_ref, dx_ref, gacc_ref):
        @pl.when(pl.program_id(0) == 0)
        def _():
            gacc_ref[...] = jnp.zeros_like(gacc_ref)

        dh = _dot(dp_ref[...], w_ref[...])
        xv = x_ref[...]
        d1, dg = _rms_bwd(dh, xv, _rstd(xv), g_ref[...])
        dx_ref[...] = dx1_ref[...] + d1
        gacc_ref[0:1, :] += dg

    tok = lambda w: pl.BlockSpec((tm, w), lambda i: (i, 0))
    return _hosted_call(
        body, name="in_proj_bwd", grid=(s // tm,),
        in_specs=[tok(IN_W), _resident((IN_W, D_MODEL)), tok(D_MODEL), _resident((1, D_MODEL)), tok(D_MODEL)],
        out_specs=[tok(D_MODEL), pl.BlockSpec((8, D_MODEL), lambda i: (0, 0))],
        out_shape=[jax.ShapeDtypeStruct((s, D_MODEL), F32), jax.ShapeDtypeStruct((8, D_MODEL), F32)],
        scratch_shapes=[], args=(dproj, w_in_t, x, gain, dx1), hosted=hosted)


def _weight_grad(a, b, tn, name, by_block=False, hosted=None, also_bf16=False):
    s, m = a.shape
    n = b.shape[1]
    tk = min(WEIGHT_GRAD_TOKENS if m <= D_MODEL else WEIGHT_GRAD_TOKENS // 2, s)

    def body(a_ref, b_ref, o_ref, *narrow_refs):
        @pl.when(pl.program_id(1) == 0)
        def _():
            o_ref[...] = jnp.zeros_like(o_ref)

        o_ref[...] += _dot_tn(a_ref[...], b_ref[...])

        if also_bf16:
            @pl.when(pl.program_id(1) == pl.num_programs(1) - 1)
            def _():
                narrow_refs[0][...] = o_ref[...].astype(BF16)

    if by_block:
        out_spec = pl.BlockSpec((None, m, tn), lambda j, k: (j, 0, 0))
        out_dims = (n // tn, m, tn)
    else:
        out_spec = pl.BlockSpec((m, tn), lambda j, k: (0, j))
        out_dims = (m, n)
    dtypes = [F32, BF16] if also_bf16 else [F32]
    outs, lands = _hosted_call(
        body, name=name, grid=(n // tn, s // tk),
        in_specs=[pl.BlockSpec((tk, m), lambda j, k: (k, 0)), pl.BlockSpec((tk, tn), lambda j, k: (k, j))],
        out_specs=[out_spec] * len(dtypes), out_shape=[jax.ShapeDtypeStruct(out_dims, t) for t in dtypes],
        scratch_shapes=[], args=(a, b), hosted=hosted)
    out = tuple(outs) if also_bf16 else outs[0]
    return out if hosted is None else (out, lands)


def _place():
    return lax.axis_index("x"), lax.axis_index("y"), lax.axis_index("c")


def _remote(src, dst, send_sems, recv_sems, k, to):
    return pltpu.make_async_remote_copy(src_ref=src, dst_ref=dst, send_sem=send_sems.at[k], recv_sem=recv_sems.at[k],
                                        device_id=to, device_id_type=MESH)


def _gather_level1_copies(w_refs, out_refs, send_sems, recv_sems, local_sems):
    x, y, c = _place()
    mine_at = 2 * x + y
    peers = [(x, y, 1 - c), (1 - x, y, c), (x, 1 - y, c), (1 - x, 1 - y, c)]
    local, sends, recvs = [], [], []
    for i, (w, out) in enumerate(zip(w_refs, out_refs)):
        half = w.shape[0] // 2
        src = w.at[pl.ds(pl.multiple_of(c * half, 16 if half % 16 == 0 else 8), half), :]
        mine = out.at[mine_at, c]
        local.append(pltpu.make_async_copy(src, mine, local_sems.at[i]))
        for k, p in enumerate(peers):
            sends.append(_remote(src, mine, send_sems, recv_sems, 4 * i + k, p))
            lands = out.at[mine_at, 1 - c] if k == 0 else out.at[2 * p[0] + p[1], c]
            recvs.append(_remote(src, lands, send_sems, recv_sems, 4 * i + k, p))
    return local, sends, recvs


def _gather_level1_start(w_refs, out_refs, send_sems, recv_sems, local_sems):
    local, sends, _ = _gather_level1_copies(w_refs, out_refs, send_sems, recv_sems, local_sems)
    for cp in local + sends:
        cp.start()


def _gather_level1_finish(w_refs, out_refs, send_sems, recv_sems, local_sems):
    local, sends, recvs = _gather_level1_copies(w_refs, out_refs, send_sems, recv_sems, local_sems)
    for cp in recvs:
        cp.wait_recv()
    for cp in sends:
        cp.wait_send()
    for cp in local:
        cp.wait()


def _gather_level2_copies(in_refs, out_refs, send_sems, recv_sems, local_sems):
    x, y, c = _place()
    chips = [(1 - x, y), (x, 1 - y), (1 - x, 1 - y)]
    sends, recvs = [], []
    for i, (src, out) in enumerate(zip(in_refs, out_refs)):
        for j, (px, py) in enumerate(chips):
            sends.append(_remote(src.at[2 * px + py, c], out.at[2 * px + py, c], send_sems, recv_sems, 3 * i + j, (x, y, 1 - c)))
            recvs.append(_remote(src.at[2 * px + py, c], out.at[2 * px + py, 1 - c], send_sems, recv_sems, 3 * i + j,
                                 (x, y, 1 - c)))
    return sends, recvs


def _gather_level2_start(in_refs, out_refs, send_sems, recv_sems, local_sems):
    for cp in _gather_level2_copies(in_refs, out_refs, send_sems, recv_sems, local_sems)[0]:
        cp.start()


def _gather_level2_finish(in_refs, out_refs, send_sems, recv_sems, local_sems):
    sends, recvs = _gather_level2_copies(in_refs, out_refs, send_sems, recv_sems, local_sems)
    for cp in recvs:
        cp.wait_recv()
    for cp in sends:
        cp.wait_send()


def _gathered_shape(w):
    r, cols = w.shape
    return jax.ShapeDtypeStruct((N_CHIPS, 2, r // 2, cols), w.dtype)


def _hosted_gather_level1(shards):
    n = len(shards)
    return _Hosted(shards, [_gathered_shape(w) for w in shards], {}, 4 * n, n, _gather_level1_start, _gather_level1_finish)


def _hosted_gather_level2(gathered):
    n = len(gathered)
    return _Hosted(gathered, [jax.ShapeDtypeStruct(g.shape, g.dtype) for g in gathered], {i: i for i in range(n)}, 3 * n, 0,
                   _gather_level2_start, _gather_level2_finish)


def _gather_now(shards, name, seq_len):
    n = len(shards)
    rows = min(512, seq_len)
    angle = 1.0 / jnp.power(10000.0, jnp.linspace(0.0, 1.0, RET_HEAD_DIM // 2, dtype=F32))
    sign = jnp.where(jnp.arange(RET_HEAD_DIM) % 2 == 0, -1.0, 1.0).astype(F32)
    angle_sign = jnp.concatenate([jnp.repeat(angle, 2)[None], sign[None], jnp.zeros((6, RET_HEAD_DIM), F32)], axis=0)

    def body(*refs):
        w_refs, as_ref, out_refs = list(refs[:n]), refs[n], list(refs[n + 1:2 * n + 1])
        sin_ref, cos_ref, send1, recv1, local1, send2, recv2 = refs[2 * n + 1:]
        _gather_level1_start(w_refs, out_refs, send1, recv1, local1)

        def fill(i, carry):
            r0 = pl.multiple_of(i * rows, rows)
            pos = (lax.broadcasted_iota(jnp.int32, (rows, RET_HEAD_DIM), 0) + i * rows).astype(F32)
            arg = pos * as_ref[0:1, :]
            sin_ref[pl.ds(r0, rows), :] = jnp.sin(arg) * as_ref[1:2, :]
            cos_ref[pl.ds(r0, rows), :] = jnp.cos(arg)
            return carry

        lax.fori_loop(0, seq_len // rows, fill, 0)
        _gather_level1_finish(w_refs, out_refs, send1, recv1, local1)
        _gather_level2_start(out_refs, out_refs, send2, recv2, None)
        _gather_level2_finish(out_refs, out_refs, send2, recv2, None)

    hbm, vmem = pl.BlockSpec(memory_space=pl.ANY), pl.BlockSpec(memory_space=pltpu.VMEM)
    table = jax.ShapeDtypeStruct((seq_len, RET_HEAD_DIM), F32)
    res = pl.pallas_call(
        body, name=name, out_shape=[_gathered_shape(w) for w in shards] + [table, table],
        in_specs=[hbm] * n + [vmem], out_specs=[hbm] * n + [vmem, vmem],
        scratch_shapes=[pltpu.SemaphoreType.DMA((4 * n,)), pltpu.SemaphoreType.DMA((4 * n,)), pltpu.SemaphoreType.DMA((n,)),
                        pltpu.SemaphoreType.DMA((3 * n,)), pltpu.SemaphoreType.DMA((3 * n,))],
        compiler_params=_params(),
    )(*shards, angle_sign)
    return res[:n], res[n], res[n + 1]


def _scatter_copies(g_refs, land_refs, send_sems, recv_sems, local_sems):
    x, y, c = _place()
    copies = []
    for i, (g, land) in enumerate(zip(g_refs, land_refs)):
        for k, (px, py, pc) in enumerate(_relations(x, y, c)):
            copies.append(_remote(g.at[2 * px + py, pc], land.at[k], send_sems, recv_sems, 7 * i + k, (px, py, pc)))
    return copies


def _scatter_start(g_refs, land_refs, send_sems, recv_sems, local_sems):
    for cp in _scatter_copies(g_refs, land_refs, send_sems, recv_sems, local_sems):
        cp.start()


def _scatter_finish(g_refs, land_refs, send_sems, recv_sems, local_sems):
    for cp in _scatter_copies(g_refs, land_refs, send_sems, recv_sems, local_sems):
        cp.wait()


def _hosted_scatter(grads):
    lands = [jax.ShapeDtypeStruct((N_DEV - 1,) + g.shape[2:], g.dtype) for g in grads]
    return _Hosted(grads, lands, {}, 7 * len(grads), 0, _scatter_start, _scatter_finish)


def _relations(x, y, c):
    rel = []
    for fx in (0, 1):
        for fy in (0, 1):
            for fc in (0, 1):
                if fx or fy or fc:
                    rel.append(((1 - x) if fx else x, (1 - y) if fy else y, (1 - c) if fc else c))
    return rel


def _join_halves(shards, small):
    n = len(shards)

    def body(*refs):
        in_refs, small_ref, out_refs, all_ref = refs[:n], refs[n], refs[n + 1:2 * n + 1], refs[2 * n + 1]
        send_sems, recv_sems = refs[2 * n + 2:]
        x, y, c = _place()
        slot = lambda p: all_ref.at[4 * p[0] + 2 * p[1] + p[2]]
        all_ref[4 * x + 2 * y + c] = small_ref[...]
        sends = [_remote(src.at[c], out.at[c], send_sems, recv_sems, i, (x, y, 1 - c))
                 for i, (src, out) in enumerate(zip(in_refs, out_refs))]
        recvs = [_remote(src.at[c], out.at[1 - c], send_sems, recv_sems, i, (x, y, 1 - c))
                 for i, (src, out) in enumerate(zip(in_refs, out_refs))]
        for k, p in enumerate(_relations(x, y, c)):
            sends.append(_remote(small_ref, slot((x, y, c)), send_sems, recv_sems, n + k, p))
            recvs.append(_remote(small_ref, slot(p), send_sems, recv_sems, n + k, p))
        for cp in sends:
            cp.start()
        for cp in recvs:
            cp.wait_recv()
        for cp in sends:
            cp.wait_send()

    hbm, vmem = pl.BlockSpec(memory_space=pl.ANY), pl.BlockSpec(memory_space=pltpu.VMEM)
    pairs = n + N_DEV - 1
    res = pl.pallas_call(
        body, name="grad_join_halves",
        out_shape=[jax.ShapeDtypeStruct(t.shape, t.dtype) for t in shards] + [jax.ShapeDtypeStruct((N_DEV,) + small.shape, F32)],
        in_specs=[hbm] * n + [vmem], out_specs=[hbm] * n + [vmem], input_output_aliases={i: i for i in range(n)},
        scratch_shapes=[pltpu.SemaphoreType.DMA((pairs,)), pltpu.SemaphoreType.DMA((pairs,))],
    )(*shards, small)
    return res[:n], res[n]


def _row_tile(rows, row_bytes, limit=1 << 20):
    best = 8
    for t in range(8, rows + 1, 8):
        if rows % t == 0 and t * row_bytes <= limit:
            best = t
    return best


def _sum_pieces(g, land, place, name):
    _, _, rh, cols = g.shape
    tr = _row_tile(rh, (N_DEV - 1) * cols * 4, 4 << 20)

    def body(p_ref, g_ref, l_ref, out_ref):
        acc = g_ref[...]
        for k in range(N_DEV - 1):
            acc = acc + l_ref[k].astype(F32)
        out_ref[...] = acc

    return pl.pallas_call(
        body, name=name,
        grid_spec=pltpu.PrefetchScalarGridSpec(
            num_scalar_prefetch=1, grid=(rh // tr,),
            in_specs=[pl.BlockSpec((None, None, tr, cols), lambda r, p: (p[0], p[1], r, 0)),
                      pl.BlockSpec((N_DEV - 1, tr, cols), lambda r, p: (0, r, 0))],
            out_specs=pl.BlockSpec((None, tr, cols), lambda r, p: (p[1], r, 0))),
        out_shape=jax.ShapeDtypeStruct((2, rh, cols), g.dtype),
        compiler_params=_params(("arbitrary",)),
    )(place, g, land)


def _adamw_math(w, g, m, v):
    m = ADAM_B1 * m + (1.0 - ADAM_B1) * g
    v = ADAM_B2 * v + (1.0 - ADAM_B2) * (g * g)
    m_hat = m / (1.0 - ADAM_B1 ** ADAM_STEP)
    v_hat = v / (1.0 - ADAM_B2 ** ADAM_STEP)
    delta = -ADAM_LR * (m_hat / (jnp.sqrt(v_hat) + ADAM_EPS) + ADAM_WD * w)
    return delta, m, v


def _adamw(w, g, m, v, name, pass_gradient=False):
    r, cols = w.shape
    tr = _row_tile(r, cols * 4)
    n_out = 4 if pass_gradient else 3

    def body(w_ref, g_ref, m_ref, v_ref, *out_refs):
        g_val = g_ref[...]
        d_ref, nm_ref, nv_ref = out_refs[-3:]
        d_ref[...], nm_ref[...], nv_ref[...] = _adamw_math(w_ref[...], g_val, m_ref[...], v_ref[...])
        if pass_gradient:
            out_refs[0][...] = g_val

    blk = pl.BlockSpec((tr, cols), lambda i: (i, 0))
    shape = jax.ShapeDtypeStruct((r, cols), F32)
    return pl.pallas_call(
        body, name=name, grid=(r // tr,), in_specs=[blk] * 4, out_specs=[blk] * n_out, out_shape=[shape] * n_out,
        compiler_params=_params(("arbitrary",)),
    )(w, g, m, v)


def _sum_devices(gathered):
    _, r, cols = gathered.shape

    def body(a_ref, g_ref):
        g = a_ref[0]
        for k in range(1, N_DEV):
            g = g + a_ref[k]
        g_ref[...] = g

    return pl.pallas_call(body, name="sum_small_grads", out_shape=jax.ShapeDtypeStruct((r, cols), F32))(gathered)


def _pack_conv(cw):
    flat = cw.reshape(-1)
    return jnp.pad(flat, (0, ROWS_CONV * D_MODEL - flat.shape[0])).reshape(ROWS_CONV, D_MODEL)


def _unpack_conv(rows):
    return rows.reshape(-1)[:CONV_WIDTH * UP_W // N_CHIPS].reshape(CONV_WIDTH, UP_W // N_CHIPS)


def _columns_to_shards(w):
    r, n = w.shape
    return jnp.transpose(w.reshape(r, N_CHIPS, n // N_CHIPS), (1, 0, 2))


def _shards_to_columns(w):
    _, r, n = w.shape
    return jnp.transpose(w, (1, 0, 2)).reshape(r, N_CHIPS * n)


def _pack_small(g_mix_pre, g_mix_post, g_ffn_pre, g_ffn_post, sinks, conv_b, loss):
    pad_row = lambda v: jnp.pad(v.reshape(1, -1), ((0, 0), (0, D_MODEL - v.size)))
    cb = jnp.pad(conv_b.reshape(-1), (0, 6 * D_MODEL - UP_W)).reshape(6, D_MODEL)
    zeros2 = jnp.zeros((2, D_MODEL), F32)
    return jnp.concatenate([g_mix_pre.reshape(1, -1), g_mix_post.reshape(1, -1), g_ffn_pre.reshape(1, -1),
                            g_ffn_post.reshape(1, -1), pad_row(sinks), pad_row(loss), zeros2, cb, zeros2], axis=0)


def _unpack_small(p):
    return dict(mix_pre_norm=p[0:1], mix_post_norm=p[1:2], ffn_pre_norm=p[2:3], ffn_post_norm=p[3:4],
                attn_sinks=p[4:5, :N_ATTN_HEADS], loss=p[5, 0], conv_b=p[8:14].reshape(1, -1)[:, :UP_W],
                conv_w=_unpack_conv(p[SMALL_ROWS:SMALL_ROWS + ROWS_CONV]))


def _local_step(x, target, g_mix_pre, w_in, sinks, w_out, g_mix_post, g_ffn_pre, w_up, conv_w, conv_b, w_down, g_ffn_post,
                distributed=True, rope=None):
    s = x.shape[0]
    consts = _ret_constants()
    sin, cos = _rope_tables(s) if rope is None else rope

    by_half = lambda g, rows: g.reshape(N_CHIPS, 2, rows // (2 * N_CHIPS), g.shape[-1])

    if distributed:
        (h1, proj), level1 = _in_proj(x, g_mix_pre, w_in, _hosted_gather_level1([w_out, w_up, w_down]))
        (mix, states, *kept), (w_out, w_up, w_down) = _mixer_fwd(proj, sinks, sin, cos, consts, _hosted_gather_level2(level1))
        w_out, w_down = w_out.reshape(D_MODEL, D_MODEL), w_down.reshape(D_FF, D_MODEL)
        w_up = w_up.reshape(N_CHIPS, D_MODEL, UP_W // N_CHIPS)
    else:
        (h1, proj), _ = _in_proj(x, g_mix_pre, w_in)
        (mix, states, *kept), _ = _mixer_fwd(proj, sinks, sin, cos, consts)
    mixed, x1, h2, u0 = _out_up_proj(mix, x, w_out, g_mix_post, g_ffn_pre, w_up)
    y, dy2, dout, du, conv_acc, tail_acc = _ffn_tail(u0, x1, target, conv_w, conv_b, w_down, g_ffn_post)
    d_w_down = by_half(_weight_grad(y, dy2, 512, "grad_w_down"), D_FF)
    (du0, dx1, dmixed, dmix, head_acc), _ = _ffn_head_bwd(
        du, conv_w, w_up, x1, g_ffn_pre, dout, mixed, g_mix_post, w_out, None)
    d_w_up, d_w_up_sent = _weight_grad(h2, du0, UP_W // N_CHIPS, "grad_w_up", by_block=True, also_bf16=True)
    d_w_out = _weight_grad(mix, dmixed, D_MODEL, "grad_w_out")
    early = [d_w_down, by_half(d_w_up, N_CHIPS * D_MODEL), by_half(d_w_out, D_MODEL)]
    sent = [early[0], by_half(d_w_up_sent, N_CHIPS * D_MODEL), early[2]]
    (dproj, dsinks), early_lands = _mixer_bwd(proj, dmix, states, kept, sin, cos, consts,
                                              _hosted_scatter(sent) if distributed else None)
    d_w_in_t = _weight_grad(dproj, h1, 512, "grad_w_in")
    late = [by_half(d_w_in_t, IN_W)]
    (grad_x, in_acc), late_lands = _in_proj_bwd(dproj, w_in, x, g_mix_pre, dx1, _hosted_scatter(late) if distributed else None)

    small = _pack_small(in_acc[0], head_acc[1], head_acc[0], tail_acc[0], dsinks[0, :N_ATTN_HEADS], conv_acc[3],
                        jnp.sum(tail_acc[1]))
    d_conv = jnp.pad(conv_acc[0:CONV_WIDTH].reshape(-1), (0, CONV_FULL_ROWS * D_MODEL - CONV_WIDTH * UP_W))
    small = jnp.concatenate([small, d_conv.reshape(CONV_FULL_ROWS, D_MODEL)], axis=0)
    grads = dict(w_down=early[0], w_up=early[1], w_out=early[2], w_in=late[0])
    lands = dict(zip(["w_down", "w_up", "w_out", "w_in"], early_lands + late_lands))
    return grad_x, grads, lands, small


def kernel(x, mix_pre_norm, w_in, attn_sinks, w_out, mix_post_norm, ffn_pre_norm, w_up, conv_w, conv_b, w_down, ffn_post_norm, loss_target, m_mix_pre_norm, m_w_in, m_attn_sinks, m_w_out, m_mix_post_norm, m_ffn_pre_norm, m_w_up, m_conv_w, m_conv_b, m_w_down, m_ffn_post_norm, v_mix_pre_norm, v_w_in, v_attn_sinks, v_w_out, v_mix_post_norm, v_ffn_pre_norm, v_w_up, v_conv_w, v_conv_b, v_w_down, v_ffn_post_norm):
    cx, cy, cc = _place()
    shard = 2 * cx + cy

    conv_rows = jnp.pad(conv_w[0], ((0, 16 - CONV_WIDTH), (0, 0)))
    w_in_t = jnp.swapaxes(w_in[0], 0, 1)
    (w_in_all, conv_all), sin, cos = _gather_now([w_in_t.astype(BF16), conv_rows], "gather_w_in", x.shape[1])
    conv_full = _shards_to_columns(conv_all[:, 0, :CONV_WIDTH])

    grad_x, grads, lands, small = _local_step(
        x[0], loss_target[0], mix_pre_norm, w_in_all.reshape(IN_W, D_MODEL), attn_sinks.reshape(-1), w_out[0].astype(BF16),
        mix_post_norm, ffn_pre_norm, w_up[0].astype(BF16), conv_full, conv_b, w_down[0].astype(BF16), ffn_post_norm,
        rope=(sin, cos))

    place = jnp.stack([shard, cc]).astype(jnp.int32)
    mats = ["w_in", "w_out", "w_up", "w_down"]
    halves = [_sum_pieces(grads[n], lands[n], place, "sum_grad_" + n) for n in mats]
    weights = dict(w_in=(w_in, m_w_in, v_w_in), w_out=(w_out, m_w_out, v_w_out), w_up=(w_up, m_w_up, v_w_up),
                   w_down=(w_down, m_w_down, v_w_down))
    mat_out = {}
    joined_all, small_all = _join_halves(halves, small)
    for n, joined in zip(mats, joined_all):
        w, m, v = (t[0] for t in weights[n])
        if n == "w_in":
            w, m, v = (jnp.swapaxes(t, 0, 1) for t in (w, m, v))
        res = tuple(_adamw(w, joined.reshape(w.shape), m, v, "adamw_" + n, pass_gradient=True))
        mat_out[n] = tuple(jnp.swapaxes(t, 0, 1) for t in res) if n == "w_in" else res

    small_sum = _sum_devices(small_all)
    d_conv_full = small_sum[SMALL_ROWS:].reshape(-1)[:CONV_WIDTH * UP_W].reshape(CONV_WIDTH, UP_W)
    d_conv_mine = lax.dynamic_slice_in_dim(d_conv_full, shard * (UP_W // N_CHIPS), UP_W // N_CHIPS, axis=1)
    g_s = jnp.concatenate([small_sum[:SMALL_ROWS], _pack_conv(d_conv_mine)], axis=0)
    zero = jnp.zeros((), F32)
    pack_rep = lambda a, b, c_, d, e, f, cw: jnp.concatenate([_pack_small(a, b, c_, d, e, f, zero), _pack_conv(cw[0])], axis=0)
    w_s = pack_rep(mix_pre_norm, mix_post_norm, ffn_pre_norm, ffn_post_norm, attn_sinks, conv_b, conv_w)
    m_s = pack_rep(m_mix_pre_norm, m_mix_post_norm, m_ffn_pre_norm, m_ffn_post_norm, m_attn_sinks, m_conv_b, m_conv_w)
    v_s = pack_rep(v_mix_pre_norm, v_mix_post_norm, v_ffn_pre_norm, v_ffn_post_norm, v_attn_sinks, v_conv_b, v_conv_w)
    delta_s, new_m_s, new_v_s = _adamw(w_s, g_s, m_s, v_s, "adamw_small")

    names = ["mix_pre_norm", "w_in", "attn_sinks", "w_out", "mix_post_norm", "ffn_pre_norm", "w_up", "conv_w", "conv_b",
             "w_down", "ffn_post_norm"]

    def leaves(which, packed_small):
        smalls = _unpack_small(packed_small)
        return [mat_out[n][which][None] if n in mat_out else (smalls[n][None] if n == "conv_w" else smalls[n]) for n in names]

    loss = _unpack_small(g_s)["loss"]
    return (loss, grad_x[None], *leaves(0, g_s), *leaves(1, delta_s), *leaves(2, new_m_s), *leaves(3, new_v_s))
```

```python
import math

import jax
import jax.numpy as jnp
from jax import lax
from jax.experimental import pallas as pl
from jax.experimental.pallas import tpu as pltpu

F32 = jnp.float32
BF16 = jnp.bfloat16

D_MODEL = 1024
HEAD_DIM = 64
ATTN_W = 512
N_ATTN_HEADS = 8
KV_W = 128
RET_W = 512
N_RET_HEADS = 4
RET_HEAD_DIM = 128
CHUNK = 128
IN_W = 2816
D_FF = 2816
UP_W = 2 * D_FF
CONV_WIDTH = 3
RMS_EPS = 1e-6
GN_EPS = 1e-6
MASK_VALUE = -1e30
ATTN_SCALE = HEAD_DIM ** -0.5
RET_K_SCALE = RET_HEAD_DIM ** -0.5
GELU_C = math.sqrt(2.0 / math.pi)
GELU_A = 0.044715

ADAM_LR = 0.001
ADAM_B1 = 0.9
ADAM_B2 = 0.999
ADAM_EPS = 1e-08
ADAM_WD = 0.01
ADAM_STEP = 10

N_CHIPS = 4
N_DEV = 8
MESH = pl.DeviceIdType.MESH
VMEM_LIMIT_V7X = 56 * 1024 * 1024
TOKEN_TILE = 256
BIG_TOKEN_TILE = 512
IN_PROJ_TOKEN_TILE = 1024
WEIGHT_GRAD_TOKENS = 2048
FFN_ROW_BLOCK = 64
HEAD_BWD_COLS = 512
MIXER_CHUNKS_PER_STEP = 4
Q_A0, KV_A0, Q_R0, K_R0, V_R0, G_R0 = 0, 512, 768, 1280, 1792, 2304

ROWS_CONV = 8
SMALL_ROWS = 16
CONV_FULL_ROWS = 24


def _params(sem=None, **kw):
    if sem is not None:
        kw["dimension_semantics"] = sem
    return pltpu.CompilerParams(vmem_limit_bytes=VMEM_LIMIT_V7X, **kw)


def _resident(shape):
    zeros = (0,) * len(shape)
    return pl.BlockSpec(shape, lambda *_: zeros, pipeline_mode=pl.Buffered(1))


class _Hosted:
    def __init__(self, ins, outs, aliases, n_pairs, n_local, start, finish):
        self.ins, self.outs, self.aliases = list(ins), list(outs), dict(aliases)
        self.n_pairs, self.n_local, self.start, self.finish = n_pairs, max(n_local, 1), start, finish


def _hosted_call(compute, *, name, grid, in_specs, out_specs, out_shape, scratch_shapes, args, hosted=None):
    params = _params(("arbitrary",) * len(grid))
    if hosted is None:
        res = pl.pallas_call(compute, name=name, grid=grid, in_specs=in_specs, out_specs=out_specs, out_shape=out_shape,
                             scratch_shapes=scratch_shapes, compiler_params=params)(*args)
        return list(res), []
    n_in, n_out, n_scr = len(in_specs), len(out_specs), len(scratch_shapes)
    h_in, h_out = len(hosted.ins), len(hosted.outs)

    def at(step_of):
        cond = pl.program_id(0) == step_of(grid[0])
        for d in range(1, len(grid)):
            cond = jnp.logical_and(cond, pl.program_id(d) == step_of(grid[d]))
        return cond

    def body(*refs):
        ins, refs = refs[:n_in], refs[n_in:]
        h_ins, refs = refs[:h_in], refs[h_in:]
        outs, refs = refs[:n_out], refs[n_out:]
        h_outs, refs = refs[:h_out], refs[h_out:]
        scr, sems = refs[:n_scr], refs[n_scr:]

        @pl.when(at(lambda n: 0))
        def _():
            hosted.start(h_ins, h_outs, *sems)

        compute(*ins, *outs, *scr)

        @pl.when(at(lambda n: n - 1))
        def _():
            hosted.finish(h_ins, h_outs, *sems)

    hbm = pl.BlockSpec(memory_space=pl.ANY)
    res = pl.pallas_call(
        body, name=name, grid=grid,
        in_specs=list(in_specs) + [hbm] * h_in, out_specs=list(out_specs) + [hbm] * h_out,
        out_shape=list(out_shape) + hosted.outs,
        scratch_shapes=list(scratch_shapes) + [pltpu.SemaphoreType.DMA((hosted.n_pairs,)), pltpu.SemaphoreType.DMA((hosted.n_pairs,)),
                                               pltpu.SemaphoreType.DMA((hosted.n_local,))],
        input_output_aliases={n_in + a: n_out + b for a, b in hosted.aliases.items()},
        compiler_params=params,
    )(*args, *hosted.ins)
    return list(res[:n_out]), list(res[n_out:])


def _dot(a, b):
    return jnp.dot(a, b, preferred_element_type=F32)


def _dot_nt(a, b):
    return lax.dot_general(a, b, (((1,), (1,)), ((), ())), preferred_element_type=F32)


def _dot_tn(a, b):
    return lax.dot_general(a, b, (((0,), (0,)), ((), ())), preferred_element_type=F32)


def _shift_matrix(n, by):
    row = lax.broadcasted_iota(jnp.int32, (n, n), 0)
    col = lax.broadcasted_iota(jnp.int32, (n, n), 1)
    return jnp.where(col == row + by, 1.0, 0.0).astype(BF16)


def _rstd(v):
    return lax.rsqrt(jnp.mean(v * v, axis=-1, keepdims=True) + RMS_EPS)


def _rms_bwd(dy, v, rstd, gain):
    n = v * rstd
    dgain = jnp.sum(dy * n, axis=0, keepdims=True)
    dn = dy * gain
    dv = rstd * (dn - n * jnp.mean(dn * n, axis=-1, keepdims=True))
    return dv, dgain


def _lane_lo(shape):
    return (lax.broadcasted_iota(jnp.int32, shape, 1) % 128) < HEAD_DIM


GROUP = N_ATTN_HEADS // (KV_W // HEAD_DIM)


def _attn_bias(first_chunk):
    qi = lax.broadcasted_iota(jnp.int32, (GROUP * CHUNK, 2 * CHUNK), 0) % CHUNK
    kj = lax.broadcasted_iota(jnp.int32, (GROUP * CHUNK, 2 * CHUNK), 1)
    valid = jnp.logical_and(kj > qi, kj <= qi + CHUNK)
    if first_chunk:
        valid = jnp.logical_and(valid, kj >= CHUNK)
    return jnp.where(valid, 0.0, MASK_VALUE)


def _half(shape, hk):
    lo = _lane_lo(shape)
    return lo if hk == 0 else jnp.logical_not(lo)


class _GroupMasks:
    def __init__(self, sk_ref):
        groups = range(KV_W // HEAD_DIM)
        self.q = [_half((CHUNK, 128), hk) for hk in groups]
        self.kv = [_half((2 * CHUNK, 128), hk) for hk in groups]
        self.sinks = [_group_sinks(sk_ref, hk) for hk in groups]


def _stack_heads(ref, row0, col0, hk, half):
    parts = []
    for j in range(GROUP):
        h = GROUP * hk + j
        pair = ref[row0:row0 + CHUNK, col0 + (h // 2) * 128:col0 + (h // 2 + 1) * 128].astype(F32)
        if h % 2 != hk:
            pair = pltpu.roll(pair, HEAD_DIM, 1)
        parts.append(jnp.where(half, pair, 0.0))
    return jnp.concatenate(parts, axis=0)


def _unstack_heads(stacked, hk):
    pairs = []
    for q in range(GROUP // 2):
        even, odd = stacked[2 * q * CHUNK:(2 * q + 1) * CHUNK], stacked[(2 * q + 1) * CHUNK:(2 * q + 2) * CHUNK]
        pairs.append(even + pltpu.roll(odd, HEAD_DIM, 1) if hk == 0 else pltpu.roll(even, HEAD_DIM, 1) + odd)
    return pairs


def _group_sinks(sk_ref, hk):
    row = lax.broadcasted_iota(jnp.int32, (GROUP * CHUNK, 1), 0)
    col = jnp.full((GROUP * CHUNK, 1), sk_ref[GROUP * hk], F32)
    for j in range(1, GROUP):
        col = jnp.where(row >= j * CHUNK, sk_ref[GROUP * hk + j], col)
    return col


def _attn_probs(q_b, kk_b, bias, sink):
    s = _dot_nt(q_b, kk_b) * ATTN_SCALE + bias
    m = jnp.maximum(jnp.max(s, axis=-1, keepdims=True), sink)
    e = jnp.exp(s - m)
    e_sink = jnp.exp(sink - m)
    inv = 1.0 / (jnp.sum(e, axis=-1, keepdims=True) + e_sink)
    return e * inv, e_sink * inv


def _even_lanes(shape):
    return (lax.broadcasted_iota(jnp.int32, shape, 1) % 2) == 0


def _swap2(v, even):
    return jnp.where(even, pltpu.roll(v, v.shape[1] - 1, 1), pltpu.roll(v, 1, 1))


def _tile4(v):
    return jnp.concatenate([v, v, v, v], axis=-1)


def _sigmoid(v):
    return 1.0 / (1.0 + jnp.exp(-v))


def _ret_constants():
    h = N_RET_HEADS
    log_gamma = jnp.log(1.0 - jnp.power(2.0, -5.0 - jnp.arange(h, dtype=F32)))
    idx = jnp.arange(CHUNK, dtype=F32)
    rel = idx[:, None] - idx[None, :]
    d_intra = jnp.where(rel[None] >= 0, jnp.exp(log_gamma[:, None, None] * jnp.maximum(rel, 0.0)[None]), 0.0)
    xi = jnp.exp(log_gamma[None, :] * (idx[:, None] + 1.0))
    zeta = jnp.exp(log_gamma[None, :] * (CHUNK - 1.0 - idx[:, None]))
    decay = jnp.exp(log_gamma * CHUNK)
    xi_full = jnp.repeat(xi, RET_HEAD_DIM, axis=1)
    zeta_full = jnp.repeat(zeta, RET_HEAD_DIM, axis=1)
    decay_full = jnp.broadcast_to(jnp.repeat(decay, RET_HEAD_DIM)[None, :], (8, RET_W))
    return d_intra.astype(F32), xi_full.astype(F32), zeta_full.astype(F32), decay_full.astype(F32)


def _rope_tables(s):
    pos = jnp.arange(s, dtype=F32)
    angle = 1.0 / jnp.power(10000.0, jnp.linspace(0.0, 1.0, RET_HEAD_DIM // 2, dtype=F32))
    angle = jnp.repeat(angle, 2)
    sign = jnp.where(jnp.arange(RET_HEAD_DIM) % 2 == 0, -1.0, 1.0).astype(F32)
    return jnp.sin(pos[:, None] * angle[None]) * sign[None], jnp.cos(pos[:, None] * angle[None])


def _in_proj(x, gain, w_in_t, hosted=None):
    s = x.shape[0]
    tm = min(IN_PROJ_TOKEN_TILE, s)

    def body(x_ref, g_ref, w_ref, h_ref, p_ref):
        xv = x_ref[...]
        h = (xv * _rstd(xv) * g_ref[...]).astype(BF16)
        h_ref[...] = h
        p_ref[...] = _dot_nt(h, w_ref[...])

    return _hosted_call(
        body, name="in_proj", grid=(s // tm,),
        in_specs=[pl.BlockSpec((tm, D_MODEL), lambda i: (i, 0)), _resident((1, D_MODEL)), _resident((IN_W, D_MODEL))],
        out_specs=[pl.BlockSpec((tm, D_MODEL), lambda i: (i, 0)), pl.BlockSpec((tm, IN_W), lambda i: (i, 0))],
        out_shape=[jax.ShapeDtypeStruct((s, D_MODEL), BF16), jax.ShapeDtypeStruct((s, IN_W), F32)],
        scratch_shapes=[], args=(x, gain, w_in_t), hosted=hosted)


def _mixer_fwd(proj, sinks, sin, cos, consts, hosted=None):
    s = proj.shape[0]
    nc = s // CHUNK
    cps = MIXER_CHUNKS_PER_STEP
    groups = KV_W // HEAD_DIM
    d_intra, xi_full, zeta_full, decay_full = consts

    def body(sk_ref, p_ref, pkv_ref, sin_ref, cos_ref, dm_ref, xi_ref, ze_ref, dc_ref,
             mix_ref, st_ref, pr_ref, ps_ref, ra_ref, on_ref, rs_ref, rq_ref, rk_ref, rz_ref, sq_ref, sg_ref, state):
        i = pl.program_id(0)

        @pl.when(i == 0)
        def _():
            state[...] = jnp.zeros_like(state)

        st = [state[h] for h in range(N_RET_HEADS)]
        bias_any = _attn_bias(False)
        bias_c0 = jnp.where(i == 0, _attn_bias(True), bias_any)
        even = _even_lanes((CHUNK, RET_W))
        masks = _GroupMasks(sk_ref)
        for c in range(cps):
            r0 = c * CHUNK
            rows = slice(r0, r0 + CHUNK)

            kv_cur = p_ref[rows, KV_A0:KV_A0 + 2 * KV_W]
            kv_prev = pkv_ref[...] if c == 0 else p_ref[r0 - CHUNK:r0, KV_A0:KV_A0 + 2 * KV_W]
            kk = jnp.concatenate([kv_prev[:, :KV_W], kv_cur[:, :KV_W]], axis=0)
            vv = jnp.concatenate([kv_prev[:, KV_W:], kv_cur[:, KV_W:]], axis=0)
            kk_b = kk.astype(BF16)
            bias = bias_c0 if c == 0 else bias_any
            for hk in range(KV_W // HEAD_DIM):
                q_b = _stack_heads(p_ref, r0, Q_A0, hk, masks.q[hk]).astype(BF16)
                p, p_sink = _attn_probs(q_b, kk_b, bias, masks.sinks[hk])
                p_b = p.astype(BF16)
                pr_ref[c, hk] = p_b
                ps_ref[c, hk] = p_sink
                sq_ref[c, hk] = q_b
                v_b = jnp.where(masks.kv[hk], vv, 0.0).astype(BF16)
                for q, pair in enumerate(_unstack_heads(_dot(p_b, v_b), hk)):
                    pi = (GROUP // 2) * hk + q
                    mix_ref[rows, pi * 128:(pi + 1) * 128] = pair.astype(BF16)

            sin4, cos4 = _tile4(sin_ref[rows, :]), _tile4(cos_ref[rows, :])
            q_r = p_ref[rows, Q_R0:Q_R0 + RET_W]
            k_r = p_ref[rows, K_R0:K_R0 + RET_W] * RET_K_SCALE
            q_r = q_r * cos4 + _swap2(q_r, even) * sin4
            k_r = k_r * cos4 + _swap2(k_r, even) * sin4
            q_b, k_b, kz_b = q_r.astype(BF16), k_r.astype(BF16), (k_r * ze_ref[...]).astype(BF16)
            rq_ref[rows, :], rk_ref[rows, :], rz_ref[rows, :] = q_b, k_b, kz_b
            for h in range(N_RET_HEADS):
                sl = slice(h * RET_HEAD_DIM, (h + 1) * RET_HEAD_DIM)
                qh, kh = q_b[:, sl], k_b[:, sl]
                vh = p_ref[rows, V_R0 + h * RET_HEAD_DIM:V_R0 + (h + 1) * RET_HEAD_DIM].astype(BF16)
                st_ref[c, h] = st[h]
                a_b = (_dot_nt(qh, kh) * dm_ref[h]).astype(BF16)
                qx = (q_r[:, sl] * xi_ref[:, sl]).astype(BF16)
                o = _dot(jnp.concatenate([a_b, qx], axis=1), jnp.concatenate([vh, st[h].astype(BF16)], axis=0))
                st[h] = dc_ref[0:1, sl] * st[h] + _dot_tn(kz_b[:, sl], vh)
                mu = jnp.mean(o, axis=-1, keepdims=True)
                oc = o - mu
                rs = lax.rsqrt(jnp.mean(oc * oc, axis=-1, keepdims=True) + GN_EPS)
                on = oc * rs
                ra_ref[c, h], on_ref[rows, sl], rs_ref[c, h] = a_b, on, rs
                g = p_ref[rows, G_R0 + h * RET_HEAD_DIM:G_R0 + (h + 1) * RET_HEAD_DIM]
                sg = _sigmoid(g)
                sg_ref[rows, sl] = sg
                mix_ref[rows, ATTN_W + h * RET_HEAD_DIM:ATTN_W + (h + 1) * RET_HEAD_DIM] = (g * sg * on).astype(BF16)
        for h in range(N_RET_HEADS):
            state[h] = st[h]

    return _hosted_call(
        body, name="mixer_fwd", grid=(nc // cps,),
        in_specs=[
            pl.BlockSpec(memory_space=pltpu.SMEM),
            pl.BlockSpec((cps * CHUNK, IN_W), lambda i: (i, 0)),
            pl.BlockSpec((CHUNK, 2 * KV_W), lambda i: (jnp.maximum(cps * i - 1, 0), KV_A0 // (2 * KV_W))),
            pl.BlockSpec((cps * CHUNK, RET_HEAD_DIM), lambda i: (i, 0)),
            pl.BlockSpec((cps * CHUNK, RET_HEAD_DIM), lambda i: (i, 0)),
            _resident((N_RET_HEADS, CHUNK, CHUNK)), _resident((CHUNK, RET_W)), _resident((CHUNK, RET_W)), _resident((8, RET_W)),
        ],
        out_specs=[
            pl.BlockSpec((cps * CHUNK, D_MODEL), lambda i: (i, 0)),
            pl.BlockSpec((cps, N_RET_HEADS, RET_HEAD_DIM, RET_HEAD_DIM), lambda i: (i, 0, 0, 0)),
            pl.BlockSpec((cps, groups, GROUP * CHUNK, 2 * CHUNK), lambda i: (i, 0, 0, 0)),
            pl.BlockSpec((cps, groups, GROUP * CHUNK, 1), lambda i: (i, 0, 0, 0)),
            pl.BlockSpec((cps, N_RET_HEADS, CHUNK, CHUNK), lambda i: (i, 0, 0, 0)),
            pl.BlockSpec((cps * CHUNK, RET_W), lambda i: (i, 0)),
            pl.BlockSpec((cps, N_RET_HEADS, CHUNK, 1), lambda i: (i, 0, 0, 0)),
        ] + [pl.BlockSpec((cps * CHUNK, RET_W), lambda i: (i, 0))] * 3 + [
            pl.BlockSpec((cps, groups, GROUP * CHUNK, 128), lambda i: (i, 0, 0, 0)),
            pl.BlockSpec((cps * CHUNK, RET_W), lambda i: (i, 0))],
        out_shape=[jax.ShapeDtypeStruct((s, D_MODEL), BF16),
                   jax.ShapeDtypeStruct((nc, N_RET_HEADS, RET_HEAD_DIM, RET_HEAD_DIM), F32),
                   jax.ShapeDtypeStruct((nc, groups, GROUP * CHUNK, 2 * CHUNK), BF16),
                   jax.ShapeDtypeStruct((nc, groups, GROUP * CHUNK, 1), F32),
                   jax.ShapeDtypeStruct((nc, N_RET_HEADS, CHUNK, CHUNK), BF16),
                   jax.ShapeDtypeStruct((s, RET_W), F32),
                   jax.ShapeDtypeStruct((nc, N_RET_HEADS, CHUNK, 1), F32)] + [jax.ShapeDtypeStruct((s, RET_W), BF16)] * 3 + [
                   jax.ShapeDtypeStruct((nc, groups, GROUP * CHUNK, 128), BF16), jax.ShapeDtypeStruct((s, RET_W), F32)],
        scratch_shapes=[pltpu.VMEM((N_RET_HEADS, RET_HEAD_DIM, RET_HEAD_DIM), F32)],
        args=(sinks, proj, proj, sin, cos, d_intra, xi_full, zeta_full, decay_full), hosted=hosted)


def _out_up_proj(mix, x, w_out, g_post, g_pre, w_up):
    s = x.shape[0]
    tm = min(BIG_TOKEN_TILE, s)
    blk = UP_W // N_CHIPS

    def body(mix_ref, x_ref, wo_ref, g2_ref, g3_ref, wu_ref, mixed_ref, x1_ref, h2_ref, u0_ref):
        mixed = _dot(mix_ref[...], wo_ref[...])
        mixed_ref[...] = mixed
        x1 = x_ref[...] + mixed * _rstd(mixed) * g2_ref[...]
        x1_ref[...] = x1
        h2 = (x1 * _rstd(x1) * g3_ref[...]).astype(BF16)
        h2_ref[...] = h2
        for k in range(N_CHIPS):
            u0_ref[:, k * blk:(k + 1) * blk] = _dot(h2, wu_ref[k]).astype(BF16)

    tok = lambda w: pl.BlockSpec((tm, w), lambda i: (i, 0))
    return pl.pallas_call(
        body, name="out_up_proj", grid=(s // tm,),
        in_specs=[tok(D_MODEL), tok(D_MODEL), _resident((D_MODEL, D_MODEL)), _resident((1, D_MODEL)), _resident((1, D_MODEL)),
                  _resident((N_CHIPS, D_MODEL, blk))],
        out_specs=[tok(D_MODEL), tok(D_MODEL), tok(D_MODEL), tok(UP_W)],
        out_shape=[jax.ShapeDtypeStruct((s, D_MODEL), F32), jax.ShapeDtypeStruct((s, D_MODEL), F32),
                   jax.ShapeDtypeStruct((s, D_MODEL), BF16), jax.ShapeDtypeStruct((s, UP_W), BF16)],
        compiler_params=_params(("arbitrary",)),
    )(mix, x, w_out, g_post, g_pre, w_up)


def _ffn_tail(u0, x1, target, conv_w, conv_b, w_down, g_post):
    s = x1.shape[0]
    tm = TOKEN_TILE
    last = s // tm - 1
    rb, lanes = FFN_ROW_BLOCK, 128

    def body(u0_ref, x1_ref, t_ref, cw_ref, cb_ref, wd_ref, g_ref,
             y_ref, dy2_ref, dout_ref, du_ref, cacc_ref, gacc_ref, u1_s, u2_s, carry, gelu_s, slope_s, dy_s, cacc):
        i = pl.program_id(0)

        @pl.when(i == 0)
        def _():
            carry[...] = jnp.zeros_like(carry)
            cacc[...] = jnp.zeros_like(cacc)
            gacc_ref[...] = jnp.zeros_like(gacc_ref)

        shift1, shift2 = _shift_matrix(tm, -1), _shift_matrix(tm, -2)
        r8 = lax.broadcasted_iota(jnp.int32, (8, 1), 0)
        wide = 2 * lanes

        def shift_block(col):
            cols = slice(col, col + wide)
            u1_s[:, cols] = _dot(shift1, u0_ref[:, cols])
            u2_s[:, cols] = _dot(shift2, u0_ref[:, cols])
            c14, c15 = carry[14:15, cols], carry[15:16, cols]
            u1_s[0:8, cols] = jnp.where(r8 == 0, c15, u1_s[0:8, cols])
            u2_s[0:8, cols] = jnp.where(r8 == 0, c14, jnp.where(r8 == 1, c15, u2_s[0:8, cols]))

        def taps(col):
            return (cw_ref[0:1, col:col + lanes], cw_ref[1:2, col:col + lanes], cw_ref[2:3, col:col + lanes],
                    cb_ref[0:1, col:col + lanes])

        def shifted(r0, col):
            return (u2_s[r0:r0 + rb, col:col + lanes], u1_s[r0:r0 + rb, col:col + lanes],
                    u0_ref[r0:r0 + rb, col:col + lanes].astype(F32))

        def conv(r0, col, w):
            u2, u1, uc = shifted(r0, col)
            return w[0] * u2 + w[1] * u1 + w[2] * uc + w[3]

        fold = lambda v: jnp.sum(v.reshape(rb // 8, 8, lanes), axis=0)

        shift_block(0)
        shift_block(D_FF)
        for j in range(D_FF // lanes):
            cg, cv = j * lanes, D_FF + j * lanes
            if cg % wide == 0 and cg + wide < D_FF:
                shift_block(cg + wide)
                shift_block(cv + wide)
            wg, wv = taps(cg), taps(cv)
            for r0 in range(0, tm, rb):
                gate, val = conv(r0, cg, wg), conv(r0, cv, wv)
                g2 = gate * gate
                th = jnp.tanh(gate * (GELU_C + GELU_C * GELU_A * g2))
                hp = 0.5 * th + 0.5
                gelu = gate * hp
                dgelu = hp + gate * (1.0 - th * th) * (0.5 * GELU_C + 1.5 * GELU_C * GELU_A * g2)
                y_ref[r0:r0 + rb, cg:cg + lanes] = (gelu * val).astype(BF16)
                gelu_s[r0:r0 + rb, cg:cg + lanes] = gelu
                slope_s[r0:r0 + rb, cg:cg + lanes] = dgelu * val

        y2 = _dot(y_ref[...], wd_ref[...])
        r4 = _rstd(y2)
        gain = g_ref[...]
        out = x1_ref[...] + y2 * r4 * gain
        diff = out - t_ref[...]
        dout = diff * (1.0 / D_MODEL)
        dout_ref[...] = dout
        dy2, dgain = _rms_bwd(dout, y2, r4, gain)
        dy2_b = dy2.astype(BF16)
        dy2_ref[...] = dy2_b
        gacc_ref[0:1, :] += dgain
        gacc_ref[1:2, :] += 0.5 * jnp.sum(diff * dout, axis=0, keepdims=True)
        carry[...] = u0_ref[tm - 16:tm, :].astype(F32)

        dy_s[:, 0:wide] = _dot_nt(dy2_b, wd_ref[0:wide, :])
        for j in range(D_FF // lanes):
            cg, cv = j * lanes, D_FF + j * lanes
            if cg % wide == 0 and cg + wide < D_FF:
                dy_s[:, cg + wide:cg + 2 * wide] = _dot_nt(dy2_b, wd_ref[cg + wide:cg + 2 * wide, :])
            acc = [[jnp.zeros((8, lanes), F32) for _ in range(CONV_WIDTH + 1)] for _ in range(2)]
            for r0 in range(0, tm, rb):
                dy = dy_s[r0:r0 + rb, cg:cg + lanes]
                d_gate = dy * slope_s[r0:r0 + rb, cg:cg + lanes]
                d_val = dy * gelu_s[r0:r0 + rb, cg:cg + lanes]
                for side, (col, d) in enumerate(((cg, d_gate), (cv, d_val))):
                    du_ref[r0:r0 + rb, col:col + lanes] = d.astype(BF16)
                    for k, u in enumerate(shifted(r0, col)):
                        acc[side][k] = acc[side][k] + fold(d * u)
                    acc[side][CONV_WIDTH] = acc[side][CONV_WIDTH] + fold(d)
            for side, col in enumerate((cg, cv)):
                for k in range(CONV_WIDTH + 1):
                    cacc[8 * k:8 * k + 8, col:col + lanes] += acc[side][k]

        @pl.when(i == last)
        def _():
            for k in range(CONV_WIDTH + 1):
                cacc_ref[k:k + 1, :] = jnp.sum(cacc[8 * k:8 * k + 8, :], axis=0, keepdims=True)
            cacc_ref[CONV_WIDTH + 1:8, :] = jnp.zeros((8 - CONV_WIDTH - 1, UP_W), F32)

    tok = lambda w: pl.BlockSpec((tm, w), lambda i: (i, 0))
    return pl.pallas_call(
        body, name="ffn_tail", grid=(s // tm,),
        in_specs=[tok(UP_W), tok(D_MODEL), tok(D_MODEL), _resident((CONV_WIDTH, UP_W)), _resident((1, UP_W)),
                  _resident((D_FF, D_MODEL)), _resident((1, D_MODEL))],
        out_specs=[tok(D_FF), tok(D_MODEL), tok(D_MODEL), tok(UP_W),
                   pl.BlockSpec((8, UP_W), lambda i: (0, 0)), pl.BlockSpec((8, D_MODEL), lambda i: (0, 0))],
        out_shape=[jax.ShapeDtypeStruct((s, D_FF), BF16), jax.ShapeDtypeStruct((s, D_MODEL), BF16),
                   jax.ShapeDtypeStruct((s, D_MODEL), F32), jax.ShapeDtypeStruct((s, UP_W), BF16),
                   jax.ShapeDtypeStruct((8, UP_W), F32), jax.ShapeDtypeStruct((8, D_MODEL), F32)],
        scratch_shapes=[pltpu.VMEM((tm, UP_W), F32), pltpu.VMEM((tm, UP_W), F32), pltpu.VMEM((16, UP_W), F32),
                        pltpu.VMEM((tm, D_FF), F32), pltpu.VMEM((tm, D_FF), F32),
                        pltpu.VMEM((tm, D_FF), F32), pltpu.VMEM((8 * (CONV_WIDTH + 1), UP_W), F32)],
        compiler_params=_params(("arbitrary",)),
    )(u0, x1, target, conv_w, conv_b, w_down, g_post)


def _ffn_head_bwd(du, conv_w, w_up, x1, g_pre, dout, mixed, g_post, w_out, hosted=None):
    s = x1.shape[0]
    tm = TOKEN_TILE
    nt = s // tm
    blk = UP_W // N_CHIPS

    def body(du_ref, halo_ref, cw_ref, wu_ref, x1_ref, g3_ref, dout_ref, mixed_ref, g2_ref, wo_ref,
             du0_ref, dx1_ref, dmixed_ref, dmix_ref, gacc_ref, dbuf):
        i = pl.program_id(0)

        @pl.when(i == 0)
        def _():
            gacc_ref[...] = jnp.zeros_like(gacc_ref)

        dbuf[0:tm, :] = du_ref[...].astype(F32)
        dbuf[tm:tm + 16, :] = jnp.where(i < nt - 1, halo_ref[...].astype(F32), 0.0)
        dh2 = jnp.zeros((tm, D_MODEL), F32)
        for k in range(N_CHIPS):
            for c0 in range(0, blk, HEAD_BWD_COLS):
                width = min(HEAD_BWD_COLS, blk - c0)
                cols = slice(k * blk + c0, k * blk + c0 + width)
                du0_b = (cw_ref[2:3, cols] * dbuf[0:tm, cols] + cw_ref[1:2, cols] * dbuf[1:1 + tm, cols]
                         + cw_ref[0:1, cols] * dbuf[2:2 + tm, cols]).astype(BF16)
                du0_ref[:, cols] = du0_b
                dh2 = dh2 + _dot_nt(du0_b, wu_ref[k, :, c0:c0 + width])
        x1 = x1_ref[...]
        d3, dg3 = _rms_bwd(dh2, x1, _rstd(x1), g3_ref[...])
        dx1 = dout_ref[...] + d3
        dx1_ref[...] = dx1
        mixed = mixed_ref[...]
        dmixed, dg2 = _rms_bwd(dx1, mixed, _rstd(mixed), g2_ref[...])
        dmixed_b = dmixed.astype(BF16)
        dmixed_ref[...] = dmixed_b
        dmix_ref[...] = _dot_nt(dmixed_b, wo_ref[...]).astype(BF16)
        gacc_ref[0:1, :] += dg3
        gacc_ref[1:2, :] += dg2

    tok = lambda w: pl.BlockSpec((tm, w), lambda i: (i, 0))
    halo = pl.BlockSpec((16, UP_W), lambda i: (jnp.minimum(i + 1, nt - 1) * (tm // 16), 0))
    return _hosted_call(
        body, name="ffn_head_bwd", grid=(nt,),
        in_specs=[tok(UP_W), halo, _resident((CONV_WIDTH, UP_W)), _resident((N_CHIPS, D_MODEL, blk)), tok(D_MODEL),
                  _resident((1, D_MODEL)), tok(D_MODEL), tok(D_MODEL), _resident((1, D_MODEL)), _resident((D_MODEL, D_MODEL))],
        out_specs=[tok(UP_W), tok(D_MODEL), tok(D_MODEL), tok(D_MODEL), pl.BlockSpec((8, D_MODEL), lambda i: (0, 0))],
        out_shape=[jax.ShapeDtypeStruct((s, UP_W), BF16), jax.ShapeDtypeStruct((s, D_MODEL), F32),
                   jax.ShapeDtypeStruct((s, D_MODEL), BF16), jax.ShapeDtypeStruct((s, D_MODEL), BF16),
                   jax.ShapeDtypeStruct((8, D_MODEL), F32)],
        scratch_shapes=[pltpu.VMEM((tm + 16, UP_W), F32)],
        args=(du, du, conv_w, w_up, x1, g_pre, dout, mixed, g_post, w_out), hosted=hosted)


def _mixer_bwd(proj, dmix, states, kept, sin, cos, consts, hosted=None):
    probs, p_sinks, ret_scores, ret_normed, ret_rstd, ret_q, ret_k, ret_kz, stacked_q, gate_sig = kept
    s = proj.shape[0]
    nc = s // CHUNK
    cps = MIXER_CHUNKS_PER_STEP
    nb = nc // cps
    groups = KV_W // HEAD_DIM
    d_intra, xi_full, zeta_full, decay_full = consts

    def body(p_ref, pkv_ref, dmix_ref, st_ref, pr_ref, ps_ref, ra_ref, on_ref, rs_ref, rq_ref, rk_ref, rz_ref, sq_ref, sg_ref,
             sin_ref, cos_ref, dm_ref, xi_ref, ze_ref, dc_ref, dp_ref, dsk_ref, gstate, ckv, dsk_acc):
        i = pl.program_id(0)
        block = nb - 1 - i

        @pl.when(i == 0)
        def _():
            gstate[...] = jnp.zeros_like(gstate)
            ckv[...] = jnp.zeros_like(ckv)
            dsk_acc[...] = jnp.zeros_like(dsk_acc)

        gs_all = [gstate[h] for h in range(N_RET_HEADS)]
        later_kv = ckv[...]
        lane = lax.broadcasted_iota(jnp.int32, (CHUNK, 128), 1)
        dsk = jnp.zeros((CHUNK, 128), F32)
        even = _even_lanes((CHUNK, RET_W))
        half_q = [_half((CHUNK, 128), hk) for hk in range(groups)]
        half_kv = [_half((2 * CHUNK, 128), hk) for hk in range(groups)]
        for c in reversed(range(cps)):
            r0 = c * CHUNK
            rows = slice(r0, r0 + CHUNK)

            kv_cur = p_ref[rows, KV_A0:KV_A0 + 2 * KV_W]
            kv_prev = pkv_ref[...] if c == 0 else p_ref[r0 - CHUNK:r0, KV_A0:KV_A0 + 2 * KV_W]
            kk = jnp.concatenate([kv_prev[:, :KV_W], kv_cur[:, :KV_W]], axis=0)
            vv = jnp.concatenate([kv_prev[:, KV_W:], kv_cur[:, KV_W:]], axis=0)
            vv_b = vv.astype(BF16)
            dkk = jnp.zeros((2 * CHUNK, KV_W), F32)
            dvv = jnp.zeros((2 * CHUNK, KV_W), F32)
            for hk in range(groups):
                q_b = sq_ref[c, hk]
                do_b = _stack_heads(dmix_ref, r0, 0, hk, half_q[hk]).astype(BF16)
                p_b = pr_ref[c, hk]
                p = p_b.astype(F32)
                dpr = _dot_nt(do_b, vv_b)
                delta = jnp.sum(p * dpr, axis=-1, keepdims=True)
                ds_b = (p * (dpr - delta) * ATTN_SCALE).astype(BF16)
                dsink = -ps_ref[c, hk] * delta
                for j in range(GROUP):
                    dsk = dsk + jnp.where(lane == GROUP * hk + j, dsink[j * CHUNK:(j + 1) * CHUNK], 0.0)
                k_b = jnp.where(half_kv[hk], kk, 0.0).astype(BF16)
                for q, pair in enumerate(_unstack_heads(_dot(ds_b, k_b), hk)):
                    pi = (GROUP // 2) * hk + q
                    dp_ref[rows, Q_A0 + pi * 128:Q_A0 + (pi + 1) * 128] = pair.astype(BF16)
                dkk = dkk + _dot_tn(ds_b, q_b)
                dvv = dvv + _dot_tn(p_b, do_b)
            dp_ref[rows, KV_A0:KV_A0 + KV_W] = (dkk[CHUNK:] + later_kv[:, :KV_W]).astype(BF16)
            dp_ref[rows, KV_A0 + KV_W:KV_A0 + 2 * KV_W] = (dvv[CHUNK:] + later_kv[:, KV_W:]).astype(BF16)
            later_kv = jnp.concatenate([dkk[:CHUNK], dvv[:CHUNK]], axis=1)

            sin4, cos4 = _tile4(sin_ref[rows, :]), _tile4(cos_ref[rows, :])
            dq_parts, dk_parts = [], []
            for h in range(N_RET_HEADS):
                sl = slice(h * RET_HEAD_DIM, (h + 1) * RET_HEAD_DIM)
                qh, kh = rq_ref[rows, sl], rk_ref[rows, sl]
                vh = p_ref[rows, V_R0 + h * RET_HEAD_DIM:V_R0 + (h + 1) * RET_HEAD_DIM].astype(BF16)
                st_b = st_ref[c, h].astype(BF16)
                gs = gs_all[h]
                gs_b = gs.astype(BF16)
                xi_h = xi_ref[:, sl]
                dm = dm_ref[h]
                a_b, on, rs = ra_ref[c, h], on_ref[rows, sl], rs_ref[c, h]
                g = p_ref[rows, G_R0 + h * RET_HEAD_DIM:G_R0 + (h + 1) * RET_HEAD_DIM]
                sg = sg_ref[rows, sl]
                dr = dmix_ref[rows, ATTN_W + h * RET_HEAD_DIM:ATTN_W + (h + 1) * RET_HEAD_DIM].astype(F32)
                dp_ref[rows, G_R0 + h * RET_HEAD_DIM:G_R0 + (h + 1) * RET_HEAD_DIM] = (
                    dr * on * (sg * (1.0 + g * (1.0 - sg)))).astype(BF16)
                don = dr * g * sg
                do = rs * (don - jnp.mean(don, axis=-1, keepdims=True) - on * jnp.mean(don * on, axis=-1, keepdims=True))
                do_b = do.astype(BF16)
                dox_b = (do * xi_h).astype(BF16)
                da_b = (_dot_nt(do_b, vh) * dm).astype(BF16)
                dq_parts.append(_dot(da_b, kh) + _dot_nt(dox_b, st_b))
                dk_parts.append(_dot_tn(da_b, qh) + ze_ref[:, sl] * _dot_nt(vh, gs_b))
                dv = _dot_tn(a_b, do_b) + _dot(rz_ref[rows, sl], gs_b)
                dp_ref[rows, V_R0 + h * RET_HEAD_DIM:V_R0 + (h + 1) * RET_HEAD_DIM] = dv.astype(BF16)
                gs_all[h] = dc_ref[0:1, sl] * gs + _dot_tn(qh, dox_b)
            dq = jnp.concatenate(dq_parts, axis=-1)
            dk = jnp.concatenate(dk_parts, axis=-1)
            dp_ref[rows, Q_R0:Q_R0 + RET_W] = (dq * cos4 - _swap2(dq, even) * sin4).astype(BF16)
            dp_ref[rows, K_R0:K_R0 + RET_W] = (RET_K_SCALE * (dk * cos4 - _swap2(dk, even) * sin4)).astype(BF16)

        for h in range(N_RET_HEADS):
            gstate[h] = gs_all[h]
        ckv[...] = later_kv
        dsk_acc[...] += dsk

        @pl.when(i == nb - 1)
        def _():
            dsk_ref[...] = jnp.sum(dsk_acc[...], axis=0, keepdims=True)

    rev = lambda i: nb - 1 - i
    return _hosted_call(
        body, name="mixer_bwd", grid=(nb,),
        in_specs=[
            pl.BlockSpec((cps * CHUNK, IN_W), lambda i: (rev(i), 0)),
            pl.BlockSpec((CHUNK, 2 * KV_W), lambda i: (jnp.maximum(cps * rev(i) - 1, 0), KV_A0 // (2 * KV_W))),
            pl.BlockSpec((cps * CHUNK, D_MODEL), lambda i: (rev(i), 0)),
            pl.BlockSpec((cps, N_RET_HEADS, RET_HEAD_DIM, RET_HEAD_DIM), lambda i: (rev(i), 0, 0, 0)),
            pl.BlockSpec((cps, groups, GROUP * CHUNK, 2 * CHUNK), lambda i: (rev(i), 0, 0, 0)),
            pl.BlockSpec((cps, groups, GROUP * CHUNK, 1), lambda i: (rev(i), 0, 0, 0)),
            pl.BlockSpec((cps, N_RET_HEADS, CHUNK, CHUNK), lambda i: (rev(i), 0, 0, 0)),
            pl.BlockSpec((cps * CHUNK, RET_W), lambda i: (rev(i), 0)),
            pl.BlockSpec((cps, N_RET_HEADS, CHUNK, 1), lambda i: (rev(i), 0, 0, 0)),
            pl.BlockSpec((cps * CHUNK, RET_W), lambda i: (rev(i), 0)), pl.BlockSpec((cps * CHUNK, RET_W), lambda i: (rev(i), 0)),
            pl.BlockSpec((cps * CHUNK, RET_W), lambda i: (rev(i), 0)),
            pl.BlockSpec((cps, groups, GROUP * CHUNK, 128), lambda i: (rev(i), 0, 0, 0)),
            pl.BlockSpec((cps * CHUNK, RET_W), lambda i: (rev(i), 0)),
            pl.BlockSpec((cps * CHUNK, RET_HEAD_DIM), lambda i: (rev(i), 0)),
            pl.BlockSpec((cps * CHUNK, RET_HEAD_DIM), lambda i: (rev(i), 0)),
            _resident((N_RET_HEADS, CHUNK, CHUNK)), _resident((CHUNK, RET_W)), _resident((CHUNK, RET_W)), _resident((8, RET_W)),
        ],
        out_specs=[pl.BlockSpec((cps * CHUNK, IN_W), lambda i: (rev(i), 0)), pl.BlockSpec((1, 128), lambda i: (0, 0))],
        out_shape=[jax.ShapeDtypeStruct((s, IN_W), BF16), jax.ShapeDtypeStruct((1, 128), F32)],
        scratch_shapes=[pltpu.VMEM((N_RET_HEADS, RET_HEAD_DIM, RET_HEAD_DIM), F32), pltpu.VMEM((CHUNK, 2 * KV_W), F32),
                        pltpu.VMEM((CHUNK, 128), F32)],
        args=(proj, proj, dmix, states, probs, p_sinks, ret_scores, ret_normed, ret_rstd, ret_q, ret_k, ret_kz, stacked_q, gate_sig,
              sin, cos, d_intra, xi_full, zeta_full, decay_full), hosted=hosted)


def _in_proj_bwd(dproj, w_in_t, x, gain, dx1, hosted=None):
    s = x.shape[0]
    tm = min(BIG_TOKEN_TILE, s)

    def body(dp_ref, w_ref, x_ref, g_ref, dx1_ref, dx_ref, gacc_ref):
        @pl.when(pl.program_id(0) == 0)
        def _():
            gacc_ref[...] = jnp.zeros_like(gacc_ref)

        dh = _dot(dp_ref[...], w_ref[...])
        xv = x_ref[...]
        d1, dg = _rms_bwd(dh, xv, _rstd(xv), g_ref[...])
        dx_ref[...] = dx1_ref[...] + d1
        gacc_ref[0:1, :] += dg

    tok = lambda w: pl.BlockSpec((tm, w), lambda i: (i, 0))
    return _hosted_call(
        body, name="in_proj_bwd", grid=(s // tm,),
        in_specs=[tok(IN_W), _resident((IN_W, D_MODEL)), tok(D_MODEL), _resident((1, D_MODEL)), tok(D_MODEL)],
        out_specs=[tok(D_MODEL), pl.BlockSpec((8, D_MODEL), lambda i: (0, 0))],
        out_shape=[jax.ShapeDtypeStruct((s, D_MODEL), F32), jax.ShapeDtypeStruct((8, D_MODEL), F32)],
        scratch_shapes=[], args=(dproj, w_in_t, x, gain, dx1), hosted=hosted)


def _weight_grad(a, b, tn, name, by_block=False, hosted=None, also_bf16=False):
    s, m = a.shape
    n = b.shape[1]
    tk = min(WEIGHT_GRAD_TOKENS if m <= D_MODEL else WEIGHT_GRAD_TOKENS // 2, s)

    def body(a_ref, b_ref, o_ref, *narrow_refs):
        @pl.when(pl.program_id(1) == 0)
        def _():
            o_ref[...] = jnp.zeros_like(o_ref)

        o_ref[...] += _dot_tn(a_ref[...], b_ref[...])

        if also_bf16:
            @pl.when(pl.program_id(1) == pl.num_programs(1) - 1)
            def _():
                narrow_refs[0][...] = o_ref[...].astype(BF16)

    if by_block:
        out_spec = pl.BlockSpec((None, m, tn), lambda j, k: (j, 0, 0))
        out_dims = (n // tn, m, tn)
    else:
        out_spec = pl.BlockSpec((m, tn), lambda j, k: (0, j))
        out_dims = (m, n)
    dtypes = [F32, BF16] if also_bf16 else [F32]
    outs, lands = _hosted_call(
        body, name=name, grid=(n // tn, s // tk),
        in_specs=[pl.BlockSpec((tk, m), lambda j, k: (k, 0)), pl.BlockSpec((tk, tn), lambda j, k: (k, j))],
        out_specs=[out_spec] * len(dtypes), out_shape=[jax.ShapeDtypeStruct(out_dims, t) for t in dtypes],
        scratch_shapes=[], args=(a, b), hosted=hosted)
    out = tuple(outs) if also_bf16 else outs[0]
    return out if hosted is None else (out, lands)


def _place():
    return lax.axis_index("x"), lax.axis_index("y"), lax.axis_index("c")


def _remote(src, dst, send_sems, recv_sems, k, to):
    return pltpu.make_async_remote_copy(src_ref=src, dst_ref=dst, send_sem=send_sems.at[k], recv_sem=recv_sems.at[k],
                                        device_id=to, device_id_type=MESH)


def _gather_level1_copies(w_refs, out_refs, send_sems, recv_sems, local_sems):
    x, y, c = _place()
    mine_at = 2 * x + y
    peers = [(x, y, 1 - c), (1 - x, y, c), (x, 1 - y, c), (1 - x, 1 - y, c)]
    local, sends, recvs = [], [], []
    for i, (w, out) in enumerate(zip(w_refs, out_refs)):
        half = w.shape[0] // 2
        src = w.at[pl.ds(pl.multiple_of(c * half, 16 if half % 16 == 0 else 8), half), :]
        mine = out.at[mine_at, c]
        local.append(pltpu.make_async_copy(src, mine, local_sems.at[i]))
        for k, p in enumerate(peers):
            sends.append(_remote(src, mine, send_sems, recv_sems, 4 * i + k, p))
            lands = out.at[mine_at, 1 - c] if k == 0 else out.at[2 * p[0] + p[1], c]
            recvs.append(_remote(src, lands, send_sems, recv_sems, 4 * i + k, p))
    return local, sends, recvs


def _gather_level1_start(w_refs, out_refs, send_sems, recv_sems, local_sems):
    local, sends, _ = _gather_level1_copies(w_refs, out_refs, send_sems, recv_sems, local_sems)
    for cp in local + sends:
        cp.start()


def _gather_level1_finish(w_refs, out_refs, send_sems, recv_sems, local_sems):
    local, sends, recvs = _gather_level1_copies(w_refs, out_refs, send_sems, recv_sems, local_sems)
    for cp in recvs:
        cp.wait_recv()
    for cp in sends:
        cp.wait_send()
    for cp in local:
        cp.wait()


def _gather_level2_copies(in_refs, out_refs, send_sems, recv_sems, local_sems, first_sem=0):
    x, y, c = _place()
    chips = [(1 - x, y), (x, 1 - y), (1 - x, 1 - y)]
    sends, recvs = [], []
    for i, (src, out) in enumerate(zip(in_refs, out_refs)):
        for j, (px, py) in enumerate(chips):
            sem = first_sem + 3 * i + j
            sends.append(_remote(src.at[2 * px + py, c], out.at[2 * px + py, c], send_sems, recv_sems, sem, (x, y, 1 - c)))
            recvs.append(_remote(src.at[2 * px + py, c], out.at[2 * px + py, 1 - c], send_sems, recv_sems, sem, (x, y, 1 - c)))
    return sends, recvs


def _gather_both_finish(w_refs, out_refs, send_sems, recv_sems, local_sems):
    _gather_level1_finish(w_refs, out_refs, send_sems, recv_sems, local_sems)
    sends, recvs = _gather_level2_copies(out_refs, out_refs, send_sems, recv_sems, local_sems, 4 * len(w_refs))
    for cp in sends:
        cp.start()
    for cp in recvs:
        cp.wait_recv()
    for cp in sends:
        cp.wait_send()


def _gather_level2_start(in_refs, out_refs, send_sems, recv_sems, local_sems):
    for cp in _gather_level2_copies(in_refs, out_refs, send_sems, recv_sems, local_sems)[0]:
        cp.start()


def _gather_level2_finish(in_refs, out_refs, send_sems, recv_sems, local_sems):
    sends, recvs = _gather_level2_copies(in_refs, out_refs, send_sems, recv_sems, local_sems)
    for cp in recvs:
        cp.wait_recv()
    for cp in sends:
        cp.wait_send()


def _gathered_shape(w):
    r, cols = w.shape
    return jax.ShapeDtypeStruct((N_CHIPS, 2, r // 2, cols), w.dtype)


def _hosted_gather_level1(shards):
    n = len(shards)
    return _Hosted(shards, [_gathered_shape(w) for w in shards], {}, 4 * n, n, _gather_level1_start, _gather_level1_finish)


def _hosted_gather_both(shards):
    n = len(shards)
    return _Hosted(shards, [_gathered_shape(w) for w in shards], {}, 7 * n, n, _gather_level1_start, _gather_both_finish)


def _hosted_gather_level2(gathered):
    n = len(gathered)
    return _Hosted(gathered, [jax.ShapeDtypeStruct(g.shape, g.dtype) for g in gathered], {i: i for i in range(n)}, 3 * n, 0,
                   _gather_level2_start, _gather_level2_finish)


def _gather_now(shards, name, seq_len):
    n = len(shards)
    rows = min(512, seq_len)
    angle = 1.0 / jnp.power(10000.0, jnp.linspace(0.0, 1.0, RET_HEAD_DIM // 2, dtype=F32))
    sign = jnp.where(jnp.arange(RET_HEAD_DIM) % 2 == 0, -1.0, 1.0).astype(F32)
    angle_sign = jnp.concatenate([jnp.repeat(angle, 2)[None], sign[None], jnp.zeros((6, RET_HEAD_DIM), F32)], axis=0)

    def body(*refs):
        w_refs, as_ref, out_refs = list(refs[:n]), refs[n], list(refs[n + 1:2 * n + 1])
        sin_ref, cos_ref, send1, recv1, local1, send2, recv2 = refs[2 * n + 1:]
        _gather_level1_start(w_refs, out_refs, send1, recv1, local1)

        def fill(i, carry):
            r0 = pl.multiple_of(i * rows, rows)
            pos = (lax.broadcasted_iota(jnp.int32, (rows, RET_HEAD_DIM), 0) + i * rows).astype(F32)
            arg = pos * as_ref[0:1, :]
            sin_ref[pl.ds(r0, rows), :] = jnp.sin(arg) * as_ref[1:2, :]
            cos_ref[pl.ds(r0, rows), :] = jnp.cos(arg)
            return carry

        lax.fori_loop(0, seq_len // rows, fill, 0)
        _gather_level1_finish(w_refs, out_refs, send1, recv1, local1)
        _gather_level2_start(out_refs, out_refs, send2, recv2, None)
        _gather_level2_finish(out_refs, out_refs, send2, recv2, None)

    hbm, vmem = pl.BlockSpec(memory_space=pl.ANY), pl.BlockSpec(memory_space=pltpu.VMEM)
    table = jax.ShapeDtypeStruct((seq_len, RET_HEAD_DIM), F32)
    res = pl.pallas_call(
        body, name=name, out_shape=[_gathered_shape(w) for w in shards] + [table, table],
        in_specs=[hbm] * n + [vmem], out_specs=[hbm] * n + [vmem, vmem],
        scratch_shapes=[pltpu.SemaphoreType.DMA((4 * n,)), pltpu.SemaphoreType.DMA((4 * n,)), pltpu.SemaphoreType.DMA((n,)),
                        pltpu.SemaphoreType.DMA((3 * n,)), pltpu.SemaphoreType.DMA((3 * n,))],
        compiler_params=_params(),
    )(*shards, angle_sign)
    return res[:n], res[n], res[n + 1]


def _scatter_copies(g_refs, land_refs, send_sems, recv_sems, local_sems):
    x, y, c = _place()
    copies = []
    for i, (g, land) in enumerate(zip(g_refs, land_refs)):
        for k, (px, py, pc) in enumerate(_relations(x, y, c)):
            copies.append(_remote(g.at[2 * px + py, pc], land.at[k], send_sems, recv_sems, 7 * i + k, (px, py, pc)))
    return copies


def _scatter_start(g_refs, land_refs, send_sems, recv_sems, local_sems):
    for cp in _scatter_copies(g_refs, land_refs, send_sems, recv_sems, local_sems):
        cp.start()


def _scatter_finish(g_refs, land_refs, send_sems, recv_sems, local_sems):
    for cp in _scatter_copies(g_refs, land_refs, send_sems, recv_sems, local_sems):
        cp.wait()


def _hosted_scatter(grads):
    lands = [jax.ShapeDtypeStruct((N_DEV - 1,) + g.shape[2:], g.dtype) for g in grads]
    return _Hosted(grads, lands, {}, 7 * len(grads), 0, _scatter_start, _scatter_finish)


def _relations(x, y, c):
    rel = []
    for fx in (0, 1):
        for fy in (0, 1):
            for fc in (0, 1):
                if fx or fy or fc:
                    rel.append(((1 - x) if fx else x, (1 - y) if fy else y, (1 - c) if fc else c))
    return rel


def _join_halves(shards, small):
    n = len(shards)

    def body(*refs):
        in_refs, small_ref, out_refs, all_ref = refs[:n], refs[n], refs[n + 1:2 * n + 1], refs[2 * n + 1]
        send_sems, recv_sems = refs[2 * n + 2:]
        x, y, c = _place()
        slot = lambda p: all_ref.at[4 * p[0] + 2 * p[1] + p[2]]
        all_ref[4 * x + 2 * y + c] = small_ref[...]
        sends = [_remote(src.at[c], out.at[c], send_sems, recv_sems, i, (x, y, 1 - c))
                 for i, (src, out) in enumerate(zip(in_refs, out_refs))]
        recvs = [_remote(src.at[c], out.at[1 - c], send_sems, recv_sems, i, (x, y, 1 - c))
                 for i, (src, out) in enumerate(zip(in_refs, out_refs))]
        for k, p in enumerate(_relations(x, y, c)):
            sends.append(_remote(small_ref, slot((x, y, c)), send_sems, recv_sems, n + k, p))
            recvs.append(_remote(small_ref, slot(p), send_sems, recv_sems, n + k, p))
        for cp in sends:
            cp.start()
        for cp in recvs:
            cp.wait_recv()
        for cp in sends:
            cp.wait_send()

    hbm, vmem = pl.BlockSpec(memory_space=pl.ANY), pl.BlockSpec(memory_space=pltpu.VMEM)
    pairs = n + N_DEV - 1
    res = pl.pallas_call(
        body, name="grad_join_halves",
        out_shape=[jax.ShapeDtypeStruct(t.shape, t.dtype) for t in shards] + [jax.ShapeDtypeStruct((N_DEV,) + small.shape, F32)],
        in_specs=[hbm] * n + [vmem], out_specs=[hbm] * n + [vmem], input_output_aliases={i: i for i in range(n)},
        scratch_shapes=[pltpu.SemaphoreType.DMA((pairs,)), pltpu.SemaphoreType.DMA((pairs,))],
    )(*shards, small)
    return res[:n], res[n]


def _row_tile(rows, row_bytes, limit=1 << 20):
    best = 8
    for t in range(8, rows + 1, 8):
        if rows % t == 0 and t * row_bytes <= limit:
            best = t
    return best


def _sum_pieces(g, land, place, name):
    _, _, rh, cols = g.shape
    tr = _row_tile(rh, (N_DEV - 1) * cols * 4, 4 << 20)

    def body(p_ref, g_ref, l_ref, out_ref):
        acc = g_ref[...]
        for k in range(N_DEV - 1):
            acc = acc + l_ref[k].astype(F32)
        out_ref[...] = acc

    return pl.pallas_call(
        body, name=name,
        grid_spec=pltpu.PrefetchScalarGridSpec(
            num_scalar_prefetch=1, grid=(rh // tr,),
            in_specs=[pl.BlockSpec((None, None, tr, cols), lambda r, p: (p[0], p[1], r, 0)),
                      pl.BlockSpec((N_DEV - 1, tr, cols), lambda r, p: (0, r, 0))],
            out_specs=pl.BlockSpec((None, tr, cols), lambda r, p: (p[1], r, 0))),
        out_shape=jax.ShapeDtypeStruct((2, rh, cols), g.dtype),
        compiler_params=_params(("arbitrary",)),
    )(place, g, land)


def _adamw_math(w, g, m, v):
    m = ADAM_B1 * m + (1.0 - ADAM_B1) * g
    v = ADAM_B2 * v + (1.0 - ADAM_B2) * (g * g)
    m_hat = m / (1.0 - ADAM_B1 ** ADAM_STEP)
    v_hat = v / (1.0 - ADAM_B2 ** ADAM_STEP)
    delta = -ADAM_LR * (m_hat / (jnp.sqrt(v_hat) + ADAM_EPS) + ADAM_WD * w)
    return delta, m, v


def _adamw(w, g, m, v, name, pass_gradient=False):
    r, cols = w.shape
    tr = _row_tile(r, cols * 4)
    n_out = 4 if pass_gradient else 3

    def body(w_ref, g_ref, m_ref, v_ref, *out_refs):
        g_val = g_ref[...]
        d_ref, nm_ref, nv_ref = out_refs[-3:]
        d_ref[...], nm_ref[...], nv_ref[...] = _adamw_math(w_ref[...], g_val, m_ref[...], v_ref[...])
        if pass_gradient:
            out_refs[0][...] = g_val

    blk = pl.BlockSpec((tr, cols), lambda i: (i, 0))
    shape = jax.ShapeDtypeStruct((r, cols), F32)
    return pl.pallas_call(
        body, name=name, grid=(r // tr,), in_specs=[blk] * 4, out_specs=[blk] * n_out, out_shape=[shape] * n_out,
        compiler_params=_params(("arbitrary",)),
    )(w, g, m, v)


def _sum_devices(gathered):
    _, r, cols = gathered.shape

    def body(a_ref, g_ref):
        g = a_ref[0]
        for k in range(1, N_DEV):
            g = g + a_ref[k]
        g_ref[...] = g

    return pl.pallas_call(body, name="sum_small_grads", out_shape=jax.ShapeDtypeStruct((r, cols), F32))(gathered)


def _pack_conv(cw):
    flat = cw.reshape(-1)
    return jnp.pad(flat, (0, ROWS_CONV * D_MODEL - flat.shape[0])).reshape(ROWS_CONV, D_MODEL)


def _unpack_conv(rows):
    return rows.reshape(-1)[:CONV_WIDTH * UP_W // N_CHIPS].reshape(CONV_WIDTH, UP_W // N_CHIPS)


def _columns_to_shards(w):
    r, n = w.shape
    return jnp.transpose(w.reshape(r, N_CHIPS, n // N_CHIPS), (1, 0, 2))


def _shards_to_columns(w):
    _, r, n = w.shape
    return jnp.transpose(w, (1, 0, 2)).reshape(r, N_CHIPS * n)


def _pack_small(g_mix_pre, g_mix_post, g_ffn_pre, g_ffn_post, sinks, conv_b, loss):
    pad_row = lambda v: jnp.pad(v.reshape(1, -1), ((0, 0), (0, D_MODEL - v.size)))
    cb = jnp.pad(conv_b.reshape(-1), (0, 6 * D_MODEL - UP_W)).reshape(6, D_MODEL)
    zeros2 = jnp.zeros((2, D_MODEL), F32)
    return jnp.concatenate([g_mix_pre.reshape(1, -1), g_mix_post.reshape(1, -1), g_ffn_pre.reshape(1, -1),
                            g_ffn_post.reshape(1, -1), pad_row(sinks), pad_row(loss), zeros2, cb, zeros2], axis=0)


def _unpack_small(p):
    return dict(mix_pre_norm=p[0:1], mix_post_norm=p[1:2], ffn_pre_norm=p[2:3], ffn_post_norm=p[3:4],
                attn_sinks=p[4:5, :N_ATTN_HEADS], loss=p[5, 0], conv_b=p[8:14].reshape(1, -1)[:, :UP_W],
                conv_w=_unpack_conv(p[SMALL_ROWS:SMALL_ROWS + ROWS_CONV]))


def _local_step(x, target, g_mix_pre, w_in, sinks, w_out, g_mix_post, g_ffn_pre, w_up, conv_w, conv_b, w_down, g_ffn_post,
                distributed=True, rope=None):
    s = x.shape[0]
    consts = _ret_constants()
    sin, cos = _rope_tables(s) if rope is None else rope

    by_half = lambda g, rows: g.reshape(N_CHIPS, 2, rows // (2 * N_CHIPS), g.shape[-1])

    if distributed:
        (h1, proj), (w_out, w_up, w_down) = _in_proj(x, g_mix_pre, w_in, _hosted_gather_both([w_out, w_up, w_down]))
        (mix, states, *kept), _ = _mixer_fwd(proj, sinks, sin, cos, consts)
        w_out, w_down = w_out.reshape(D_MODEL, D_MODEL), w_down.reshape(D_FF, D_MODEL)
        w_up = w_up.reshape(N_CHIPS, D_MODEL, UP_W // N_CHIPS)
    else:
        (h1, proj), _ = _in_proj(x, g_mix_pre, w_in)
        (mix, states, *kept), _ = _mixer_fwd(proj, sinks, sin, cos, consts)
    mixed, x1, h2, u0 = _out_up_proj(mix, x, w_out, g_mix_post, g_ffn_pre, w_up)
    y, dy2, dout, du, conv_acc, tail_acc = _ffn_tail(u0, x1, target, conv_w, conv_b, w_down, g_ffn_post)
    d_w_down = by_half(_weight_grad(y, dy2, 512, "grad_w_down"), D_FF)
    (du0, dx1, dmixed, dmix, head_acc), _ = _ffn_head_bwd(
        du, conv_w, w_up, x1, g_ffn_pre, dout, mixed, g_mix_post, w_out, None)
    d_w_up, d_w_up_sent = _weight_grad(h2, du0, UP_W // N_CHIPS, "grad_w_up", by_block=True, also_bf16=True)
    d_w_out = _weight_grad(mix, dmixed, D_MODEL, "grad_w_out")
    early = [d_w_down, by_half(d_w_up, N_CHIPS * D_MODEL), by_half(d_w_out, D_MODEL)]
    sent = [early[0], by_half(d_w_up_sent, N_CHIPS * D_MODEL), early[2]]
    (dproj, dsinks), early_lands = _mixer_bwd(proj, dmix, states, kept, sin, cos, consts,
                                              _hosted_scatter(sent) if distributed else None)
    d_w_in_t = _weight_grad(dproj, h1, 512, "grad_w_in")
    late = [by_half(d_w_in_t, IN_W)]
    (grad_x, in_acc), late_lands = _in_proj_bwd(dproj, w_in, x, g_mix_pre, dx1, _hosted_scatter(late) if distributed else None)

    small = _pack_small(in_acc[0], head_acc[1], head_acc[0], tail_acc[0], dsinks[0, :N_ATTN_HEADS], conv_acc[3],
                        jnp.sum(tail_acc[1]))
    d_conv = jnp.pad(conv_acc[0:CONV_WIDTH].reshape(-1), (0, CONV_FULL_ROWS * D_MODEL - CONV_WIDTH * UP_W))
    small = jnp.concatenate([small, d_conv.reshape(CONV_FULL_ROWS, D_MODEL)], axis=0)
    grads = dict(w_down=early[0], w_up=early[1], w_out=early[2], w_in=late[0])
    lands = dict(zip(["w_down", "w_up", "w_out", "w_in"], early_lands + late_lands))
    return grad_x, grads, lands, small


def kernel(x, mix_pre_norm, w_in, attn_sinks, w_out, mix_post_norm, ffn_pre_norm, w_up, conv_w, conv_b, w_down, ffn_post_norm, loss_target, m_mix_pre_norm, m_w_in, m_attn_sinks, m_w_out, m_mix_post_norm, m_ffn_pre_norm, m_w_up, m_conv_w, m_conv_b, m_w_down, m_ffn_post_norm, v_mix_pre_norm, v_w_in, v_attn_sinks, v_w_out, v_mix_post_norm, v_ffn_pre_norm, v_w_up, v_conv_w, v_conv_b, v_w_down, v_ffn_post_norm):
    cx, cy, cc = _place()
    shard = 2 * cx + cy

    conv_rows = jnp.pad(conv_w[0], ((0, 16 - CONV_WIDTH), (0, 0)))
    w_in_t = jnp.swapaxes(w_in[0], 0, 1)
    (w_in_all, conv_all), sin, cos = _gather_now([w_in_t.astype(BF16), conv_rows], "gather_w_in", x.shape[1])
    conv_full = _shards_to_columns(conv_all[:, 0, :CONV_WIDTH])

    grad_x, grads, lands, small = _local_step(
        x[0], loss_target[0], mix_pre_norm, w_in_all.reshape(IN_W, D_MODEL), attn_sinks.reshape(-1), w_out[0].astype(BF16),
        mix_post_norm, ffn_pre_norm, w_up[0].astype(BF16), conv_full, conv_b, w_down[0].astype(BF16), ffn_post_norm,
        rope=(sin, cos))

    place = jnp.stack([shard, cc]).astype(jnp.int32)
    mats = ["w_in", "w_out", "w_up", "w_down"]
    halves = [_sum_pieces(grads[n], lands[n], place, "sum_grad_" + n) for n in mats]
    weights = dict(w_in=(w_in, m_w_in, v_w_in), w_out=(w_out, m_w_out, v_w_out), w_up=(w_up, m_w_up, v_w_up),
                   w_down=(w_down, m_w_down, v_w_down))
    mat_out = {}
    joined_all, small_all = _join_halves(halves, small)
    for n, joined in zip(mats, joined_all):
        w, m, v = (t[0] for t in weights[n])
        if n == "w_in":
            w, m, v = (jnp.swapaxes(t, 0, 1) for t in (w, m, v))
        res = tuple(_adamw(w, joined.reshape(w.shape), m, v, "adamw_" + n, pass_gradient=True))
        mat_out[n] = tuple(jnp.swapaxes(t, 0, 1) for t in res) if n == "w_in" else res

    small_sum = _sum_devices(small_all)
    d_conv_full = small_sum[SMALL_ROWS:].reshape(-1)[:CONV_WIDTH * UP_W].reshape(CONV_WIDTH, UP_W)
    d_conv_mine = lax.dynamic_slice_in_dim(d_conv_full, shard * (UP_W // N_CHIPS), UP_W // N_CHIPS, axis=1)
    g_s = jnp.concatenate([small_sum[:SMALL_ROWS], _pack_conv(d_conv_mine)], axis=0)
    zero = jnp.zeros((), F32)
    pack_rep = lambda a, b, c_, d, e, f, cw: jnp.concatenate([_pack_small(a, b, c_, d, e, f, zero), _pack_conv(cw[0])], axis=0)
    w_s = pack_rep(mix_pre_norm, mix_post_norm, ffn_pre_norm, ffn_post_norm, attn_sinks, conv_b, conv_w)
    m_s = pack_rep(m_mix_pre_norm, m_mix_post_norm, m_ffn_pre_norm, m_ffn_post_norm, m_attn_sinks, m_conv_b, m_conv_w)
    v_s = pack_rep(v_mix_pre_norm, v_mix_post_norm, v_ffn_pre_norm, v_ffn_post_norm, v_attn_sinks, v_conv_b, v_conv_w)
    delta_s, new_m_s, new_v_s = _adamw(w_s, g_s, m_s, v_s, "adamw_small")

    names = ["mix_pre_norm", "w_in", "attn_sinks", "w_out", "mix_post_norm", "ffn_pre_norm", "w_up", "conv_w", "conv_b",
             "w_down", "ffn_post_norm"]

    def leaves(which, packed_small):
        smalls = _unpack_small(packed_small)
        return [mat_out[n][which][None] if n in mat_out else (smalls[n][None] if n == "conv_w" else smalls[n]) for n in names]

    loss = _unpack_small(g_s)["loss"]
    return (loss, grad_x[None], *leaves(0, g_s), *leaves(1, delta_s), *leaves(2, new_m_s), *leaves(3, new_v_s))
```
